```python
import math
import jax, jax.numpy as jnp
from jax import lax
import numpy as np

D_MODEL = 1024
BATCH = 8
SEQ = 16384
DEPTH = 1

ATTN_Q_HEADS = 8
ATTN_KV_HEADS = 2
HEAD_DIM = 64
ATTN_WIDTH = ATTN_Q_HEADS * HEAD_DIM
KV_WIDTH = ATTN_KV_HEADS * HEAD_DIM
WINDOW = 128
ATTN_BLOCK = 128
N_REL_BUCKETS = 32
REL_MAX_DISTANCE = 128

SSM_HEADS = 8
SSM_HEAD_DIM = 64
SSM_WIDTH = SSM_HEADS * SSM_HEAD_DIM
SSM_STATE = 128
SSM_GROUPS = 2
SSM_BC = SSM_GROUPS * SSM_STATE
SSM_CONV = 4
SSM_CHUNK = 256

MIX_WIDTH = ATTN_WIDTH + SSM_WIDTH
IN_SPLITS = (ATTN_WIDTH, KV_WIDTH, KV_WIDTH, SSM_WIDTH, SSM_WIDTH, SSM_BC, SSM_BC, SSM_HEADS)
IN_PROJ_WIDTH = sum(IN_SPLITS)

D_FF = 2816
FFN_CONV = 3

NORM_EPS = 1e-6
N_MOD = 6

kernel_name = "hymba_swa_sink_ssd_convffn_adaln"


def _split_last(t, sizes):
    points = np.cumsum(sizes)[:-1].tolist()
    return jnp.split(t, points, axis=-1)


def rms_norm(x, w):
    xf = x.astype(jnp.float32)
    y = xf * lax.rsqrt(jnp.mean(xf * xf, axis=-1, keepdims=True) + NORM_EPS)
    return (y * w.astype(jnp.float32)).astype(x.dtype)


def causal_dwconv(x, w, b):
    k = w.shape[0]
    y = lax.conv_general_dilated(
        x, w[:, None, :].astype(x.dtype), window_strides=(1,), padding=[(k - 1, 0)],
        dimension_numbers=("NWC", "WIO", "NWC"), feature_group_count=x.shape[-1])
    return y + b.astype(x.dtype)


def t5_causal_bucket(dist):
    max_exact = N_REL_BUCKETS // 2
    nf = jnp.maximum(dist, 1).astype(jnp.float32)
    large = max_exact + (jnp.log(nf / max_exact) / math.log(REL_MAX_DISTANCE / max_exact)
                         * (N_REL_BUCKETS - max_exact)).astype(jnp.int32)
    large = jnp.minimum(large, N_REL_BUCKETS - 1)
    return jnp.where(dist < max_exact, dist, large)


def sliding_window_attention(q, k, v, rel_bias, sinks):
    bsz, s = q.shape[0], q.shape[1]
    L = ATTN_BLOCK
    nb = s // L
    g = ATTN_Q_HEADS // ATTN_KV_HEADS
    qb = q.reshape(bsz, nb, L, ATTN_KV_HEADS, g, HEAD_DIM)

    def band_keys(t):
        prev = jnp.pad(t, ((0, 0), (L, 0), (0, 0), (0, 0)))[:, :s]
        prev = prev.reshape(bsz, nb, L, ATTN_KV_HEADS, HEAD_DIM)
        cur = t.reshape(bsz, nb, L, ATTN_KV_HEADS, HEAD_DIM)
        return jnp.concatenate([prev, cur], axis=2)

    kb, vb = band_keys(k), band_keys(v)
    qi = jnp.arange(L, dtype=jnp.int32)[:, None] + L
    kj = jnp.arange(2 * L, dtype=jnp.int32)[None, :]
    dist = qi - kj
    band = (dist >= 0) & (dist < WINDOW)
    blk = jnp.arange(nb, dtype=jnp.int32)[:, None, None]
    valid = band[None] & ((blk > 0) | (kj[None] >= L))
    bucket = t5_causal_bucket(jnp.maximum(dist, 0))
    bias = rel_bias.astype(jnp.float32)[bucket]
    bias = bias.transpose(2, 0, 1).reshape(ATTN_KV_HEADS, g, L, 2 * L)

    scale = HEAD_DIM ** -0.5
    sc = jnp.einsum("bnqhgd,bnkhd->bnhgqk", qb, kb).astype(jnp.float32) * scale + bias[None, None]
    sc = jnp.where(valid[None, :, None, None], sc, -1e30)
    sink = sinks.astype(jnp.float32).reshape(ATTN_KV_HEADS, g)[None, None, :, :, None, None]
    m = jnp.maximum(jnp.max(sc, axis=-1, keepdims=True), sink)
    p = jnp.exp(sc - m)
    probs = p / (jnp.sum(p, axis=-1, keepdims=True) + jnp.exp(sink - m))
    o = jnp.einsum("bnhgqk,bnkhd->bnqhgd", probs.astype(v.dtype), vb)
    return o.reshape(bsz, s, ATTN_WIDTH)


def segsum(a):
    t = a.shape[-1]
    cs = jnp.cumsum(a, axis=-1)
    diff = cs[..., :, None] - cs[..., None, :]
    mask = jnp.tril(jnp.ones((t, t), dtype=bool))
    return jnp.where(mask, diff, -jnp.inf)


def ssd_chunked(x, dt, a, bm, cm):
    bsz, s = x.shape[0], x.shape[1]
    pad = (-s) % SSM_CHUNK
    if pad:
        padw = lambda t: jnp.pad(t, [(0, 0), (0, pad)] + [(0, 0)] * (t.ndim - 2))
        x, dt, bm, cm = padw(x), padw(dt), padw(bm), padw(cm)
    sp = s + pad
    nc, L = sp // SSM_CHUNK, SSM_CHUNK
    xc = (x * dt[..., None]).reshape(bsz, nc, L, SSM_HEADS, SSM_HEAD_DIM)
    adt = (dt * a).reshape(bsz, nc, L, SSM_HEADS).transpose(0, 3, 1, 2)
    bc = bm.reshape(bsz, nc, L, SSM_HEADS, SSM_STATE)
    cc = cm.reshape(bsz, nc, L, SSM_HEADS, SSM_STATE)
    a_cum = jnp.cumsum(adt, axis=-1)

    decay = jnp.exp(segsum(adt))
    cb = jnp.einsum("bclhn,bcshn->bhcls", cc, bc)
    y_diag = jnp.einsum("bhcls,bcshp->bclhp", cb * decay, xc)

    to_end = jnp.exp(a_cum[..., -1:] - a_cum).transpose(0, 2, 3, 1)
    states = jnp.einsum("bclhn,bclhp->bchpn", bc * to_end[..., None], xc)

    chunk_decay = jnp.exp(a_cum[..., -1]).transpose(2, 0, 1)

    def step(h, inp):
        dec, st = inp
        return dec[:, :, None, None] * h + st, h

    h0 = jnp.zeros((bsz, SSM_HEADS, SSM_HEAD_DIM, SSM_STATE), jnp.float32)
    _, prev = lax.scan(step, h0, (chunk_decay, states.transpose(1, 0, 2, 3, 4)))
    prev = prev.transpose(1, 0, 2, 3, 4)
    from_start = jnp.exp(a_cum).transpose(0, 2, 3, 1)
    y_off = jnp.einsum("bclhn,bchpn->bclhp", cc * from_start[..., None], prev)
    y = (y_diag + y_off).reshape(bsz, sp, SSM_HEADS, SSM_HEAD_DIM)
    return y[:, :s]


def mamba2_mixer(xs, z, bm, cm, dt_raw, conv_w, conv_b, dt_bias, a_log, d_skip, norm_w):
    bsz, s = xs.shape[0], xs.shape[1]
    xbc = jax.nn.silu(causal_dwconv(jnp.concatenate([xs, bm, cm], axis=-1), conv_w, conv_b))
    xs, bm, cm = _split_last(xbc, (SSM_WIDTH, SSM_BC, SSM_BC))
    rep = SSM_HEADS // SSM_GROUPS
    xh = xs.reshape(bsz, s, SSM_HEADS, SSM_HEAD_DIM).astype(jnp.float32)
    bh = jnp.repeat(bm.reshape(bsz, s, SSM_GROUPS, SSM_STATE), rep, axis=2).astype(jnp.float32)
    ch = jnp.repeat(cm.reshape(bsz, s, SSM_GROUPS, SSM_STATE), rep, axis=2).astype(jnp.float32)
    dt = jax.nn.softplus(dt_raw.astype(jnp.float32) + dt_bias.astype(jnp.float32))
    a = -jnp.exp(a_log.astype(jnp.float32))
    y = ssd_chunked(xh, dt, a, bh, ch) + d_skip.astype(jnp.float32)[:, None] * xh
    y = y.reshape(bsz, s, SSM_WIDTH) * jax.nn.silu(z.astype(jnp.float32))
    yg = y.reshape(bsz, s, SSM_GROUPS, SSM_WIDTH // SSM_GROUPS)
    yg = yg * lax.rsqrt(jnp.mean(yg * yg, axis=-1, keepdims=True) + NORM_EPS)
    y = yg.reshape(bsz, s, SSM_WIDTH) * norm_w.astype(jnp.float32)
    return y.astype(xs.dtype)


def _fwd_setup_inputs(seed: int = 0) -> dict:
    key = jax.random.key(seed)
    ks = iter(jax.random.split(key, 32))
    f32 = jnp.float32
    nrm = lambda shape, scale: jax.random.normal(next(ks), shape, f32) * scale
    gain = lambda shape: 1.0 + nrm(shape, 0.05)

    x = nrm((BATCH, SEQ, D_MODEL), 1.0)
    c = nrm((BATCH, D_MODEL), 1.0)
    rel_bias = nrm((N_REL_BUCKETS, ATTN_Q_HEADS), 0.5)
    w_ada = nrm((DEPTH, D_MODEL, N_MOD * D_MODEL), D_MODEL ** -0.5)
    b_ada = nrm((DEPTH, N_MOD * D_MODEL), 0.02)
    pre_mix_w = gain((DEPTH, D_MODEL))
    w_in = nrm((DEPTH, D_MODEL, IN_PROJ_WIDTH), D_MODEL ** -0.5)
    attn_sinks = nrm((DEPTH, ATTN_Q_HEADS), 0.5)
    ssm_conv_w = nrm((DEPTH, SSM_CONV, SSM_WIDTH + 2 * SSM_BC), SSM_CONV ** -0.5)
    ssm_conv_b = nrm((DEPTH, SSM_WIDTH + 2 * SSM_BC), 0.02)
    dt0 = jnp.exp(jax.random.uniform(next(ks), (DEPTH, SSM_HEADS), f32,
                                     math.log(1e-3), math.log(1e-1)))
    ssm_dt_bias = dt0 + jnp.log(-jnp.expm1(-dt0))
    ssm_a_log = jnp.log(jax.random.uniform(next(ks), (DEPTH, SSM_HEADS), f32, 1.0, 16.0))
    ssm_d = gain((DEPTH, SSM_HEADS))
    ssm_norm_w = gain((DEPTH, SSM_WIDTH))
    w_out = nrm((DEPTH, MIX_WIDTH, D_MODEL), MIX_WIDTH ** -0.5)
    post_mix_w = gain((DEPTH, D_MODEL))
    pre_ffn_w = gain((DEPTH, D_MODEL))
    w_up = nrm((DEPTH, D_MODEL, 2 * D_FF), D_MODEL ** -0.5)
    ffn_conv_w = nrm((DEPTH, FFN_CONV, 2 * D_FF), FFN_CONV ** -0.5)
    ffn_conv_b = nrm((DEPTH, 2 * D_FF), 0.02)
    w_down = nrm((DEPTH, D_FF, D_MODEL), D_FF ** -0.5)
    post_ffn_w = gain((DEPTH, D_MODEL))
    return {"x": x, "c": c, "rel_bias": rel_bias, "w_ada": w_ada, "b_ada": b_ada,
            "pre_mix_w": pre_mix_w, "w_in": w_in, "attn_sinks": attn_sinks,
            "ssm_conv_w": ssm_conv_w, "ssm_conv_b": ssm_conv_b, "ssm_dt_bias": ssm_dt_bias,
            "ssm_a_log": ssm_a_log, "ssm_d": ssm_d, "ssm_norm_w": ssm_norm_w,
            "w_out": w_out, "post_mix_w": post_mix_w, "pre_ffn_w": pre_ffn_w,
            "w_up": w_up, "ffn_conv_w": ffn_conv_w, "ffn_conv_b": ffn_conv_b,
            "w_down": w_down, "post_ffn_w": post_ffn_w}


def _fwd_reference(x, c, rel_bias, w_ada, b_ada, pre_mix_w, w_in, attn_sinks, ssm_conv_w,
              ssm_conv_b, ssm_dt_bias, ssm_a_log, ssm_d, ssm_norm_w, w_out, post_mix_w,
              pre_ffn_w, w_up, ffn_conv_w, ffn_conv_b, w_down, post_ffn_w):
    bsz, s = x.shape[0], x.shape[1]
    cond = jax.nn.silu(c)
    for l in range(DEPTH):
        mod = (cond @ w_ada[l] + b_ada[l])[:, None, :]
        shift1, scale1, gate1, shift2, scale2, gate2 = jnp.split(mod, N_MOD, axis=-1)

        h = rms_norm(x, pre_mix_w[l]) * (1.0 + scale1) + shift1
        proj = h @ w_in[l]
        q, k, v, xs, z, bm, cm, dt_raw = _split_last(proj, IN_SPLITS)
        attn = sliding_window_attention(
            q.reshape(bsz, s, ATTN_Q_HEADS, HEAD_DIM),
            k.reshape(bsz, s, ATTN_KV_HEADS, HEAD_DIM),
            v.reshape(bsz, s, ATTN_KV_HEADS, HEAD_DIM),
            rel_bias, attn_sinks[l])
        ssm = mamba2_mixer(xs, z, bm, cm, dt_raw, ssm_conv_w[l], ssm_conv_b[l],
                           ssm_dt_bias[l], ssm_a_log[l], ssm_d[l], ssm_norm_w[l])
        mixed = jnp.concatenate([attn, ssm], axis=-1) @ w_out[l]
        x = x + gate1 * rms_norm(mixed, post_mix_w[l])

        h = rms_norm(x, pre_ffn_w[l]) * (1.0 + scale2) + shift2
        u = causal_dwconv(h @ w_up[l], ffn_conv_w[l], ffn_conv_b[l])
        u_gate, u_val = jnp.split(u, 2, axis=-1)
        f = (jax.nn.gelu(u_gate, approximate=True) * u_val) @ w_down[l]
        x = x + gate2 * rms_norm(f, post_ffn_w[l])
    return x


import jax as _jax
import jax.numpy as _jnp

TWIN_FORMAT = 'train_step'
FWD_PARAMS = ['x', 'c', 'rel_bias', 'w_ada', 'b_ada', 'pre_mix_w', 'w_in', 'attn_sinks', 'ssm_conv_w', 'ssm_conv_b', 'ssm_dt_bias', 'ssm_a_log', 'ssm_d', 'ssm_norm_w', 'w_out', 'post_mix_w', 'pre_ffn_w', 'w_up', 'ffn_conv_w', 'ffn_conv_b', 'w_down', 'post_ffn_w']
TWIN_WEIGHTS = ['rel_bias', 'w_ada', 'b_ada', 'pre_mix_w', 'w_in', 'attn_sinks', 'ssm_conv_w', 'ssm_conv_b', 'ssm_dt_bias', 'ssm_a_log', 'ssm_d', 'ssm_norm_w', 'w_out', 'post_mix_w', 'pre_ffn_w', 'w_up', 'ffn_conv_w', 'ffn_conv_b', 'w_down', 'post_ffn_w']
TWIN_DIFF_INPUT = 'x'
TWIN_INPUTS = ['x', 'c', 'rel_bias', 'w_ada', 'b_ada', 'pre_mix_w', 'w_in', 'attn_sinks', 'ssm_conv_w', 'ssm_conv_b', 'ssm_dt_bias', 'ssm_a_log', 'ssm_d', 'ssm_norm_w', 'w_out', 'post_mix_w', 'pre_ffn_w', 'w_up', 'ffn_conv_w', 'ffn_conv_b', 'w_down', 'post_ffn_w', 'loss_target', 'm_rel_bias', 'm_w_ada', 'm_b_ada', 'm_pre_mix_w', 'm_w_in', 'm_attn_sinks', 'm_ssm_conv_w', 'm_ssm_conv_b', 'm_ssm_dt_bias', 'm_ssm_a_log', 'm_ssm_d', 'm_ssm_norm_w', 'm_w_out', 'm_post_mix_w', 'm_pre_ffn_w', 'm_w_up', 'm_ffn_conv_w', 'm_ffn_conv_b', 'm_w_down', 'm_post_ffn_w', 'v_rel_bias', 'v_w_ada', 'v_b_ada', 'v_pre_mix_w', 'v_w_in', 'v_attn_sinks', 'v_ssm_conv_w', 'v_ssm_conv_b', 'v_ssm_dt_bias', 'v_ssm_a_log', 'v_ssm_d', 'v_ssm_norm_w', 'v_w_out', 'v_post_mix_w', 'v_pre_ffn_w', 'v_w_up', 'v_ffn_conv_w', 'v_ffn_conv_b', 'v_w_down', 'v_post_ffn_w']
TWIN_OUTPUTS = ['loss', 'grad_x', 'grad_rel_bias', 'grad_w_ada', 'grad_b_ada', 'grad_pre_mix_w', 'grad_w_in', 'grad_attn_sinks', 'grad_ssm_conv_w', 'grad_ssm_conv_b', 'grad_ssm_dt_bias', 'grad_ssm_a_log', 'grad_ssm_d', 'grad_ssm_norm_w', 'grad_w_out', 'grad_post_mix_w', 'grad_pre_ffn_w', 'grad_w_up', 'grad_ffn_conv_w', 'grad_ffn_conv_b', 'grad_w_down', 'grad_post_ffn_w', 'delta_rel_bias', 'delta_w_ada', 'delta_b_ada', 'delta_pre_mix_w', 'delta_w_in', 'delta_attn_sinks', 'delta_ssm_conv_w', 'delta_ssm_conv_b', 'delta_ssm_dt_bias', 'delta_ssm_a_log', 'delta_ssm_d', 'delta_ssm_norm_w', 'delta_w_out', 'delta_post_mix_w', 'delta_pre_ffn_w', 'delta_w_up', 'delta_ffn_conv_w', 'delta_ffn_conv_b', 'delta_w_down', 'delta_post_ffn_w', 'new_m_rel_bias', 'new_m_w_ada', 'new_m_b_ada', 'new_m_pre_mix_w', 'new_m_w_in', 'new_m_attn_sinks', 'new_m_ssm_conv_w', 'new_m_ssm_conv_b', 'new_m_ssm_dt_bias', 'new_m_ssm_a_log', 'new_m_ssm_d', 'new_m_ssm_norm_w', 'new_m_w_out', 'new_m_post_mix_w', 'new_m_pre_ffn_w', 'new_m_w_up', 'new_m_ffn_conv_w', 'new_m_ffn_conv_b', 'new_m_w_down', 'new_m_post_ffn_w', 'new_v_rel_bias', 'new_v_w_ada', 'new_v_b_ada', 'new_v_pre_mix_w', 'new_v_w_in', 'new_v_attn_sinks', 'new_v_ssm_conv_w', 'new_v_ssm_conv_b', 'new_v_ssm_dt_bias', 'new_v_ssm_a_log', 'new_v_ssm_d', 'new_v_ssm_norm_w', 'new_v_w_out', 'new_v_post_mix_w', 'new_v_pre_ffn_w', 'new_v_w_up', 'new_v_ffn_conv_w', 'new_v_ffn_conv_b', 'new_v_w_down', 'new_v_post_ffn_w']
TWIN_LEAF_KINDS = {'loss': 'loss', 'grad_x': 'grad_x', 'grad_rel_bias': 'grad_w', 'grad_w_ada': 'grad_w', 'grad_b_ada': 'grad_w', 'grad_pre_mix_w': 'grad_w', 'grad_w_in': 'grad_w', 'grad_attn_sinks': 'grad_w', 'grad_ssm_conv_w': 'grad_w', 'grad_ssm_conv_b': 'grad_w', 'grad_ssm_dt_bias': 'grad_w', 'grad_ssm_a_log': 'grad_w', 'grad_ssm_d': 'grad_w', 'grad_ssm_norm_w': 'grad_w', 'grad_w_out': 'grad_w', 'grad_post_mix_w': 'grad_w', 'grad_pre_ffn_w': 'grad_w', 'grad_w_up': 'grad_w', 'grad_ffn_conv_w': 'grad_w', 'grad_ffn_conv_b': 'grad_w', 'grad_w_down': 'grad_w', 'grad_post_ffn_w': 'grad_w', 'delta_rel_bias': 'delta_w', 'delta_w_ada': 'delta_w', 'delta_b_ada': 'delta_w', 'delta_pre_mix_w': 'delta_w', 'delta_w_in': 'delta_w', 'delta_attn_sinks': 'delta_w', 'delta_ssm_conv_w': 'delta_w', 'delta_ssm_conv_b': 'delta_w', 'delta_ssm_dt_bias': 'delta_w', 'delta_ssm_a_log': 'delta_w', 'delta_ssm_d': 'delta_w', 'delta_ssm_norm_w': 'delta_w', 'delta_w_out': 'delta_w', 'delta_post_mix_w': 'delta_w', 'delta_pre_ffn_w': 'delta_w', 'delta_w_up': 'delta_w', 'delta_ffn_conv_w': 'delta_w', 'delta_ffn_conv_b': 'delta_w', 'delta_w_down': 'delta_w', 'delta_post_ffn_w': 'delta_w', 'new_m_rel_bias': 'new_m', 'new_m_w_ada': 'new_m', 'new_m_b_ada': 'new_m', 'new_m_pre_mix_w': 'new_m', 'new_m_w_in': 'new_m', 'new_m_attn_sinks': 'new_m', 'new_m_ssm_conv_w': 'new_m', 'new_m_ssm_conv_b': 'new_m', 'new_m_ssm_dt_bias': 'new_m', 'new_m_ssm_a_log': 'new_m', 'new_m_ssm_d': 'new_m', 'new_m_ssm_norm_w': 'new_m', 'new_m_w_out': 'new_m', 'new_m_post_mix_w': 'new_m', 'new_m_pre_ffn_w': 'new_m', 'new_m_w_up': 'new_m', 'new_m_ffn_conv_w': 'new_m', 'new_m_ffn_conv_b': 'new_m', 'new_m_w_down': 'new_m', 'new_m_post_ffn_w': 'new_m', 'new_v_rel_bias': 'new_v', 'new_v_w_ada': 'new_v', 'new_v_b_ada': 'new_v', 'new_v_pre_mix_w': 'new_v', 'new_v_w_in': 'new_v', 'new_v_attn_sinks': 'new_v', 'new_v_ssm_conv_w': 'new_v', 'new_v_ssm_conv_b': 'new_v', 'new_v_ssm_dt_bias': 'new_v', 'new_v_ssm_a_log': 'new_v', 'new_v_ssm_d': 'new_v', 'new_v_ssm_norm_w': 'new_v', 'new_v_w_out': 'new_v', 'new_v_post_mix_w': 'new_v', 'new_v_pre_ffn_w': 'new_v', 'new_v_w_up': 'new_v', 'new_v_ffn_conv_w': 'new_v', 'new_v_ffn_conv_b': 'new_v', 'new_v_w_down': 'new_v', 'new_v_post_ffn_w': 'new_v'}


def _forward(args):
    return _fwd_reference(*[args[k] for k in FWD_PARAMS])


def _output_shape():
    def fwd():
        inp = _fwd_setup_inputs(0)
        return _fwd_reference(*[inp[k] for k in FWD_PARAMS])
    out = _jax.eval_shape(fwd)
    return out.shape, out.dtype

N_MICROBATCH = 1
ADAM_LR = 0.001
ADAM_B1 = 0.9
ADAM_B2 = 0.999
ADAM_EPS = 1e-08
ADAM_WD = 0.01
ADAM_STEP = 10
PER_EXAMPLE_BATCH_AXIS = {'x': 0, 'c': 0, 'loss_target': 0}
SHARED_INPUTS = []
_WEIGHT_DTYPES = {'rel_bias': _jnp.float32, 'w_ada': _jnp.float32, 'b_ada': _jnp.float32, 'pre_mix_w': _jnp.float32, 'w_in': _jnp.float32, 'attn_sinks': _jnp.float32, 'ssm_conv_w': _jnp.float32, 'ssm_conv_b': _jnp.float32, 'ssm_dt_bias': _jnp.float32, 'ssm_a_log': _jnp.float32, 'ssm_d': _jnp.float32, 'ssm_norm_w': _jnp.float32, 'w_out': _jnp.float32, 'post_mix_w': _jnp.float32, 'pre_ffn_w': _jnp.float32, 'w_up': _jnp.float32, 'ffn_conv_w': _jnp.float32, 'ffn_conv_b': _jnp.float32, 'w_down': _jnp.float32, 'post_ffn_w': _jnp.float32}
MOMENT_SCALE = {'rel_bias': 2.068619e-01, 'w_ada': 1.155788e+01, 'b_ada': 2.390375e+01, 'pre_mix_w': 6.417529e-01, 'w_in': 5.060942e+00, 'attn_sinks': 1.860639e-01, 'ssm_conv_w': 3.139537e+00, 'ssm_conv_b': 5.268894e+00, 'ssm_dt_bias': 1.757181e+01, 'ssm_a_log': 1.457976e+01, 'ssm_d': 1.226012e+01, 'ssm_norm_w': 5.517345e+00, 'w_out': 9.088499e+00, 'post_mix_w': 5.433617e+01, 'pre_ffn_w': 1.917738e+00, 'w_up': 2.071145e+00, 'ffn_conv_w': 2.447723e+00, 'ffn_conv_b': 3.251858e+00, 'w_down': 4.189489e+00, 'post_ffn_w': 5.388788e+01}


def _to_microbatches(a, axis):
    t = _jnp.moveaxis(a, axis, 0)
    t = t.reshape((N_MICROBATCH, t.shape[0] // N_MICROBATCH) + t.shape[1:])
    return _jnp.moveaxis(t, 1, axis + 1)


def setup_inputs(seed: int = 0) -> dict:
    inp = _fwd_setup_inputs(seed)
    key = _jax.random.fold_in(_jax.random.key(seed), 7919)
    shape, _ = _output_shape()
    out = dict(inp)
    out["loss_target"] = _jax.random.normal(_jax.random.fold_in(key, 0), shape, _jnp.float32)
    for i, name in enumerate(TWIN_WEIGHTS):
        w = inp[name].astype(_jnp.float32)
        if MOMENT_SCALE is None:
            s = _jnp.sqrt(_jnp.mean(_jnp.square(w)) + 1e-30)
        else:
            s = MOMENT_SCALE[name]
        km, kv = _jax.random.split(_jax.random.fold_in(key, i + 1))
        out[name] = w
        out["m_" + name] = s * _jax.random.normal(km, w.shape, _jnp.float32)
        out["v_" + name] = (s * s) * _jax.random.uniform(kv, w.shape, _jnp.float32, 0.5, 1.5)
    if N_MICROBATCH > 1:
        for name, axis in PER_EXAMPLE_BATCH_AXIS.items():
            out[name] = _to_microbatches(out[name], axis)
    return {'x': out['x'], 'c': out['c'], 'rel_bias': out['rel_bias'], 'w_ada': out['w_ada'], 'b_ada': out['b_ada'], 'pre_mix_w': out['pre_mix_w'], 'w_in': out['w_in'], 'attn_sinks': out['attn_sinks'], 'ssm_conv_w': out['ssm_conv_w'], 'ssm_conv_b': out['ssm_conv_b'], 'ssm_dt_bias': out['ssm_dt_bias'], 'ssm_a_log': out['ssm_a_log'], 'ssm_d': out['ssm_d'], 'ssm_norm_w': out['ssm_norm_w'], 'w_out': out['w_out'], 'post_mix_w': out['post_mix_w'], 'pre_ffn_w': out['pre_ffn_w'], 'w_up': out['w_up'], 'ffn_conv_w': out['ffn_conv_w'], 'ffn_conv_b': out['ffn_conv_b'], 'w_down': out['w_down'], 'post_ffn_w': out['post_ffn_w'], 'loss_target': out['loss_target'], 'm_rel_bias': out['m_rel_bias'], 'm_w_ada': out['m_w_ada'], 'm_b_ada': out['m_b_ada'], 'm_pre_mix_w': out['m_pre_mix_w'], 'm_w_in': out['m_w_in'], 'm_attn_sinks': out['m_attn_sinks'], 'm_ssm_conv_w': out['m_ssm_conv_w'], 'm_ssm_conv_b': out['m_ssm_conv_b'], 'm_ssm_dt_bias': out['m_ssm_dt_bias'], 'm_ssm_a_log': out['m_ssm_a_log'], 'm_ssm_d': out['m_ssm_d'], 'm_ssm_norm_w': out['m_ssm_norm_w'], 'm_w_out': out['m_w_out'], 'm_post_mix_w': out['m_post_mix_w'], 'm_pre_ffn_w': out['m_pre_ffn_w'], 'm_w_up': out['m_w_up'], 'm_ffn_conv_w': out['m_ffn_conv_w'], 'm_ffn_conv_b': out['m_ffn_conv_b'], 'm_w_down': out['m_w_down'], 'm_post_ffn_w': out['m_post_ffn_w'], 'v_rel_bias': out['v_rel_bias'], 'v_w_ada': out['v_w_ada'], 'v_b_ada': out['v_b_ada'], 'v_pre_mix_w': out['v_pre_mix_w'], 'v_w_in': out['v_w_in'], 'v_attn_sinks': out['v_attn_sinks'], 'v_ssm_conv_w': out['v_ssm_conv_w'], 'v_ssm_conv_b': out['v_ssm_conv_b'], 'v_ssm_dt_bias': out['v_ssm_dt_bias'], 'v_ssm_a_log': out['v_ssm_a_log'], 'v_ssm_d': out['v_ssm_d'], 'v_ssm_norm_w': out['v_ssm_norm_w'], 'v_w_out': out['v_w_out'], 'v_post_mix_w': out['v_post_mix_w'], 'v_pre_ffn_w': out['v_pre_ffn_w'], 'v_w_up': out['v_w_up'], 'v_ffn_conv_w': out['v_ffn_conv_w'], 'v_ffn_conv_b': out['v_ffn_conv_b'], 'v_w_down': out['v_w_down'], 'v_post_ffn_w': out['v_post_ffn_w']}


def _loss(weights, diff, rest, loss_target):
    with _jax.named_scope("forward"):
        args = {**rest, TWIN_DIFF_INPUT: diff, **{k: w.astype(_WEIGHT_DTYPES[k]) for k, w in weights.items()}}
        y = _forward(args)
    with _jax.named_scope("loss_head"):
        err = _jnp.square(y.astype(_jnp.float32) - loss_target)
        return 0.5 * _jnp.sum(_jnp.mean(err, axis=-1)) if err.ndim else 0.5 * err


def _adamw(w, g, m, v):
    m = ADAM_B1 * m + (1.0 - ADAM_B1) * g
    v = ADAM_B2 * v + (1.0 - ADAM_B2) * _jnp.square(g)
    m_hat = m / (1.0 - ADAM_B1 ** ADAM_STEP)
    v_hat = v / (1.0 - ADAM_B2 ** ADAM_STEP)
    delta = -ADAM_LR * (m_hat / (_jnp.sqrt(v_hat) + ADAM_EPS) + ADAM_WD * w)
    return delta, m, v


def reference(x, c, rel_bias, w_ada, b_ada, pre_mix_w, w_in, attn_sinks, ssm_conv_w, ssm_conv_b, ssm_dt_bias, ssm_a_log, ssm_d, ssm_norm_w, w_out, post_mix_w, pre_ffn_w, w_up, ffn_conv_w, ffn_conv_b, w_down, post_ffn_w, loss_target, m_rel_bias, m_w_ada, m_b_ada, m_pre_mix_w, m_w_in, m_attn_sinks, m_ssm_conv_w, m_ssm_conv_b, m_ssm_dt_bias, m_ssm_a_log, m_ssm_d, m_ssm_norm_w, m_w_out, m_post_mix_w, m_pre_ffn_w, m_w_up, m_ffn_conv_w, m_ffn_conv_b, m_w_down, m_post_ffn_w, v_rel_bias, v_w_ada, v_b_ada, v_pre_mix_w, v_w_in, v_attn_sinks, v_ssm_conv_w, v_ssm_conv_b, v_ssm_dt_bias, v_ssm_a_log, v_ssm_d, v_ssm_norm_w, v_w_out, v_post_mix_w, v_pre_ffn_w, v_w_up, v_ffn_conv_w, v_ffn_conv_b, v_w_down, v_post_ffn_w):
    given = dict(x=x, c=c, rel_bias=rel_bias, w_ada=w_ada, b_ada=b_ada, pre_mix_w=pre_mix_w, w_in=w_in, attn_sinks=attn_sinks, ssm_conv_w=ssm_conv_w, ssm_conv_b=ssm_conv_b, ssm_dt_bias=ssm_dt_bias, ssm_a_log=ssm_a_log, ssm_d=ssm_d, ssm_norm_w=ssm_norm_w, w_out=w_out, post_mix_w=post_mix_w, pre_ffn_w=pre_ffn_w, w_up=w_up, ffn_conv_w=ffn_conv_w, ffn_conv_b=ffn_conv_b, w_down=w_down, post_ffn_w=post_ffn_w, loss_target=loss_target, m_rel_bias=m_rel_bias, m_w_ada=m_w_ada, m_b_ada=m_b_ada, m_pre_mix_w=m_pre_mix_w, m_w_in=m_w_in, m_attn_sinks=m_attn_sinks, m_ssm_conv_w=m_ssm_conv_w, m_ssm_conv_b=m_ssm_conv_b, m_ssm_dt_bias=m_ssm_dt_bias, m_ssm_a_log=m_ssm_a_log, m_ssm_d=m_ssm_d, m_ssm_norm_w=m_ssm_norm_w, m_w_out=m_w_out, m_post_mix_w=m_post_mix_w, m_pre_ffn_w=m_pre_ffn_w, m_w_up=m_w_up, m_ffn_conv_w=m_ffn_conv_w, m_ffn_conv_b=m_ffn_conv_b, m_w_down=m_w_down, m_post_ffn_w=m_post_ffn_w, v_rel_bias=v_rel_bias, v_w_ada=v_w_ada, v_b_ada=v_b_ada, v_pre_mix_w=v_pre_mix_w, v_w_in=v_w_in, v_attn_sinks=v_attn_sinks, v_ssm_conv_w=v_ssm_conv_w, v_ssm_conv_b=v_ssm_conv_b, v_ssm_dt_bias=v_ssm_dt_bias, v_ssm_a_log=v_ssm_a_log, v_ssm_d=v_ssm_d, v_ssm_norm_w=v_ssm_norm_w, v_w_out=v_w_out, v_post_mix_w=v_post_mix_w, v_pre_ffn_w=v_pre_ffn_w, v_w_up=v_w_up, v_ffn_conv_w=v_ffn_conv_w, v_ffn_conv_b=v_ffn_conv_b, v_w_down=v_w_down, v_post_ffn_w=v_post_ffn_w)
    weights = {n: given[n] for n in TWIN_WEIGHTS}
    shared = {n: given[n] for n in SHARED_INPUTS}
    per_example = {n: given[n] for n in ['x', 'c']}
    grad_fn = _jax.value_and_grad(_loss, argnums=(0, 1))

    def one_microbatch(ex, loss_target):
        ex = dict(ex)
        diff = ex.pop(TWIN_DIFF_INPUT)
        return grad_fn(weights, diff, {**shared, **ex}, loss_target)

    if N_MICROBATCH == 1:
        loss, (grad_w, grad_x) = one_microbatch(per_example, given["loss_target"])
    else:
        def body(carry, xs):
            loss_sum, grad_sum = carry
            l_k, (gw_k, gx_k) = one_microbatch(xs[0], xs[1])
            with _jax.named_scope("update"):
                return (loss_sum + l_k, _jax.tree.map(_jnp.add, grad_sum, gw_k)), gx_k

        init = (_jnp.zeros((), _jnp.float32), _jax.tree.map(_jnp.zeros_like, weights))
        (loss, grad_w), grad_x = _jax.lax.scan(body, init, (per_example, given["loss_target"]))
    with _jax.named_scope("update"):
        delta_w, new_m, new_v = {}, {}, {}
        for n in TWIN_WEIGHTS:
            delta_w[n], new_m[n], new_v[n] = _adamw(weights[n], grad_w[n], given["m_" + n], given["v_" + n])
    return (loss, grad_x, *[grad_w[n] for n in TWIN_WEIGHTS], *[delta_w[n] for n in TWIN_WEIGHTS],
            *[new_m[n] for n in TWIN_WEIGHTS], *[new_v[n] for n in TWIN_WEIGHTS])
```

```python
import functools
import math

import numpy as np
import jax
import jax.numpy as jnp
from jax import lax
from jax.experimental import pallas as pl
from jax.experimental.pallas import tpu as pltpu

F32 = jnp.float32
BF16 = jnp.bfloat16
HIGHEST = lax.Precision.HIGHEST
MESH_ID = pl.DeviceIdType.MESH

N_DEV = 8
D_MODEL = 1024
N_Q_HEADS = 8
N_KV_HEADS = 2
HEAD_DIM = 64
ATTN_WIDTH = 512
KV_WIDTH = 128
ATTN_BLOCK = 128
N_BUCKETS = 32
REL_MAX_DIST = 128
SSM_HEADS = 8
SSM_HEAD_DIM = 64
SSM_WIDTH = 512
SSM_STATE = 128
SSM_GROUPS = 2
SSM_BC = 256
SSM_CONV = 4
SSM_CHUNK = 256
XBC_WIDTH = SSM_WIDTH + 2 * SSM_BC
D_FF = 2816
FFN_CONV = 3
NORM_EPS = 1e-6
N_MOD = 6
IN_PROJ_WIDTH = 2312
QKV_W = ATTN_WIDTH + 2 * KV_WIDTH
OFF_XBC = QKV_W
OFF_Z = OFF_XBC + XBC_WIDTH
OFF_DT = OFF_Z + SSM_WIDTH
DT_PAD = 128
PROJ_PAD = OFF_DT + DT_PAD
FF_CHUNK = 1408

ADAM_LR = 0.001
ADAM_B1 = 0.9
ADAM_B2 = 0.999
ADAM_EPS = 1e-08
ADAM_WD = 0.01
ADAM_STEP = 10

TOKEN_TILE = 256
HALO = 8
NEXT = 16
VMEM_LIMIT = 56 * 1024 * 1024


def _params(vmem=VMEM_LIMIT, n_axes=1):
    return pltpu.CompilerParams(dimension_semantics=("arbitrary",) * n_axes, vmem_limit_bytes=vmem)


def _b(x):
    return x.astype(BF16)


def _nn(a, b):
    return jnp.dot(_b(a), _b(b), preferred_element_type=F32)


def _nt(a, b):
    return lax.dot_general(_b(a), _b(b), (((1,), (1,)), ((), ())), preferred_element_type=F32)


def _tn(a, b):
    return lax.dot_general(_b(a), _b(b), (((0,), (0,)), ((), ())), preferred_element_type=F32)


@jax.custom_vjp
def mm(a, b):
    return _nn(a, b)


mm.defvjp(lambda a, b: (_nn(a, b), (a, b)),
          lambda r, g: (_nt(g, r[1]).astype(r[0].dtype), _tn(r[0], g).astype(r[1].dtype)))


@jax.custom_vjp
def mm_nt(a, b):
    return _nt(a, b)


mm_nt.defvjp(lambda a, b: (_nt(a, b), (a, b)),
             lambda r, g: (_nn(g, r[1]).astype(r[0].dtype), _tn(g, r[0]).astype(r[1].dtype)))


@jax.custom_vjp
def mm_tn(a, b):
    return _tn(a, b)


mm_tn.defvjp(lambda a, b: (_tn(a, b), (a, b)),
             lambda r, g: (_nt(r[1], g).astype(r[0].dtype), _nn(r[0], g).astype(r[1].dtype)))


def _rms(x, w):
    return x * lax.rsqrt(jnp.mean(x * x, axis=-1, keepdims=True) + NORM_EPS) * w


def _norm_mod(x, w, scale, shift):
    return _rms(x, w) * (1.0 + scale) + shift


def _silu(x):
    return x * jax.nn.sigmoid(x)


def _conv_rows(xin, w, k):
    acc = xin * w[k - 1:k, :]
    for j in range(1, k):
        acc = acc + pltpu.roll(xin, j, axis=0) * w[k - 1 - j:k - j, :]
    return acc


def _conv_rows_t(du, w, k):
    n = du.shape[0]
    acc = du * w[k - 1:k, :]
    for j in range(1, k):
        acc = acc + pltpu.roll(du, n - j, axis=0) * w[k - 1 - j:k - j, :]
    return acc


def _row(i):
    return (i, 0)


def _const(i):
    return (0, 0)


def _vec(n):
    return pl.BlockSpec((1, n), _const)


def _block_index(p):
    return 4 * p[0] + 2 * p[1] + p[2]


def all_gather(arrs, name):
    n = len(arrs)

    def body(*refs):
        ins, outs = refs[:n], refs[n:2 * n]
        send_sems, recv_sems, local_sems = refs[2 * n:]
        x, y, c = lax.axis_index("x"), lax.axis_index("y"), lax.axis_index("c")
        me, sibling = (x, y, c), (x, y, 1 - c)
        chips = [(1 - x, y), (x, 1 - y), (1 - x, 1 - y)]

        def copy(a, k, block, to, src=None):
            dst = outs[a].at[_block_index(block)]
            return pltpu.make_async_remote_copy(
                src_ref=dst if src is None else src, dst_ref=dst,
                send_sem=send_sems.at[a * 7 + k], recv_sem=recv_sems.at[a * 7 + k],
                device_id=to, device_id_type=MESH_ID)

        mine = [pltpu.make_async_copy(ins[a], outs[a].at[_block_index(me)], local_sems.at[a]) for a in range(n)]
        for cp in mine:
            cp.start()
        first = []
        for a in range(n):
            first.append(copy(a, 0, me, sibling, src=ins[a]))
            first += [copy(a, 1 + j, me, (*chip, c), src=ins[a]) for j, chip in enumerate(chips)]
        for cp in first:
            cp.start()
        passed = []
        for j, chip in enumerate(chips):
            for a in range(n):
                copy(a, 1 + j, (*chip, c), me).wait_recv()
                cp = copy(a, 4 + j, (*chip, c), sibling)
                cp.start()
                passed.append(cp)
        for a in range(n):
            copy(a, 0, sibling, me).wait_recv()
            for j, chip in enumerate(chips):
                copy(a, 4 + j, (*chip, 1 - c), me).wait_recv()
        for cp in first + passed:
            cp.wait_send()
        for cp in mine:
            cp.wait()

    any_spec = pl.BlockSpec(memory_space=pl.ANY)
    return pl.pallas_call(
        body, name=name,
        out_shape=[jax.ShapeDtypeStruct((N_DEV,) + a.shape, a.dtype) for a in arrs],
        in_specs=[any_spec] * n, out_specs=[any_spec] * n,
        scratch_shapes=[pltpu.SemaphoreType.DMA((7 * n,)), pltpu.SemaphoreType.DMA((7 * n,)),
                        pltpu.SemaphoreType.DMA((n,))],
    )(*arrs)


def all_to_all(arrs, name):
    n = len(arrs)
    flips = [(0, 0, 1), (0, 1, 0), (0, 1, 1), (1, 0, 0), (1, 0, 1), (1, 1, 0), (1, 1, 1)]

    def body(*refs):
        ins, outs = refs[:n], refs[n:2 * n]
        send_sems, recv_sems, local_sems = refs[2 * n:]
        pos = (lax.axis_index("x"), lax.axis_index("y"), lax.axis_index("c"))
        me = _block_index(pos)
        peers = [tuple(1 - p if f else p for p, f in zip(pos, flip)) for flip in flips]

        def copy(a, k):
            peer = peers[k]
            return pltpu.make_async_remote_copy(
                src_ref=ins[a].at[_block_index(peer)], dst_ref=outs[a].at[me],
                send_sem=send_sems.at[a * 7 + k], recv_sem=recv_sems.at[a * 7 + k],
                device_id=peer, device_id_type=MESH_ID)

        def landed(a, k):
            slot = outs[a].at[_block_index(peers[k])]
            return pltpu.make_async_remote_copy(
                src_ref=slot, dst_ref=slot,
                send_sem=send_sems.at[a * 7 + k], recv_sem=recv_sems.at[a * 7 + k],
                device_id=peers[k], device_id_type=MESH_ID)

        mine = [pltpu.make_async_copy(ins[a].at[me], outs[a].at[me], local_sems.at[a]) for a in range(n)]
        for cp in mine:
            cp.start()
        sent = [copy(a, k) for a in range(n) for k in range(7)]
        for cp in sent:
            cp.start()
        for a in range(n):
            for k in range(7):
                landed(a, k).wait_recv()
        for cp in sent:
            cp.wait_send()
        for cp in mine:
            cp.wait()

    any_spec = pl.BlockSpec(memory_space=pl.ANY)
    return pl.pallas_call(
        body, name=name,
        out_shape=[jax.ShapeDtypeStruct(a.shape, a.dtype) for a in arrs],
        in_specs=[any_spec] * n, out_specs=[any_spec] * n,
        scratch_shapes=[pltpu.SemaphoreType.DMA((7 * n,)), pltpu.SemaphoreType.DMA((7 * n,)),
                        pltpu.SemaphoreType.DMA((n,))],
    )(*arrs)


def ada_fwd(c_all, w_ada, b_cols):
    def body(c_ref, w_ref, b_ref, o_ref):
        o_ref[...] = _nn(_silu(c_ref[...]), w_ref[...]) + b_ref[...]

    return pl.pallas_call(body, name="ada_fwd",
                          out_shape=jax.ShapeDtypeStruct((N_DEV, w_ada.shape[1]), F32),
                          compiler_params=_params(n_axes=0))(c_all, w_ada, b_cols)


def ada_bwd(c_all, g_cols):
    def body(c_ref, g_ref, o_ref):
        o_ref[...] = _tn(_silu(c_ref[...]), g_ref[...])

    return pl.pallas_call(body, name="ada_bwd",
                          out_shape=jax.ShapeDtypeStruct((c_all.shape[1], g_cols.shape[1]), F32),
                          compiler_params=_params(n_axes=0))(c_all, g_cols)


def matmul_nn(a, w16, out_dtype, name, tm=TOKEN_TILE):
    s, k = a.shape
    n = w16.shape[1]

    def body(a_ref, w_ref, o_ref):
        o_ref[...] = jnp.dot(_b(a_ref[...]), w_ref[...], preferred_element_type=F32).astype(out_dtype)

    return pl.pallas_call(
        body, name=name, grid=(s // tm,),
        in_specs=[pl.BlockSpec((tm, k), _row), pl.BlockSpec((k, n), _const)],
        out_specs=pl.BlockSpec((tm, n), _row),
        out_shape=jax.ShapeDtypeStruct((s, n), out_dtype),
        compiler_params=_params())(a, w16)


def matmul_tn(a, b, name, bm, bn, tk=512):
    s, m = a.shape
    n = b.shape[1]
    tk = min(tk, s)

    def body(a_ref, b_ref, o_ref):
        @pl.when(pl.program_id(2) == 0)
        def _():
            o_ref[...] = jnp.zeros_like(o_ref)

        o_ref[...] += _tn(a_ref[...], b_ref[...])

    return pl.pallas_call(
        body, name=name, grid=(m // bm, n // bn, s // tk),
        in_specs=[pl.BlockSpec((tk, bm), lambda i, j, k: (k, i)), pl.BlockSpec((tk, bn), lambda i, j, k: (k, j))],
        out_specs=pl.BlockSpec((bm, bn), lambda i, j, k: (i, j)),
        out_shape=jax.ShapeDtypeStruct((m, n), F32),
        compiler_params=_params(n_axes=3))(a, b)


def pre_mix_inproj(x, w, scale, shift, w_in16, tm=TOKEN_TILE):
    s = x.shape[0]

    def body(x_ref, w_ref, sc_ref, sh_ref, win_ref, h_ref, qkv_ref, xbc_ref, z_ref, dt_ref):
        h16 = _b(_norm_mod(x_ref[...], w_ref[...], sc_ref[...], sh_ref[...]))
        h_ref[...] = h16
        dot = lambda lo, hi: jnp.dot(h16, win_ref[:, lo:hi], preferred_element_type=F32)
        qkv_ref[...] = _b(dot(0, OFF_XBC))
        xbc_ref[...] = dot(OFF_XBC, OFF_Z)
        z_ref[...] = dot(OFF_Z, OFF_DT)
        dt_ref[...] = dot(OFF_DT, PROJ_PAD)

    tile = lambda n: pl.BlockSpec((tm, n), _row)
    return pl.pallas_call(
        body, name="pre_mix_inproj", grid=(s // tm,),
        in_specs=[tile(D_MODEL), _vec(D_MODEL), _vec(D_MODEL), _vec(D_MODEL), pl.BlockSpec((D_MODEL, PROJ_PAD), _const)],
        out_specs=[tile(D_MODEL), tile(QKV_W), tile(XBC_WIDTH), tile(SSM_WIDTH), tile(DT_PAD)],
        out_shape=[jax.ShapeDtypeStruct((s, D_MODEL), BF16), jax.ShapeDtypeStruct((s, QKV_W), BF16),
                   jax.ShapeDtypeStruct((s, XBC_WIDTH), F32), jax.ShapeDtypeStruct((s, SSM_WIDTH), F32),
                   jax.ShapeDtypeStruct((s, DT_PAD), F32)],
        compiler_params=_params())(x, w, scale, shift, w_in16)


def _attn_block(q, kp, kc, vp, vc, bias, sinks, first):
    lq = ATTN_BLOCK
    qi = lax.broadcasted_iota(jnp.int32, (lq, 2 * lq), 0) + lq
    kj = lax.broadcasted_iota(jnp.int32, (lq, 2 * lq), 1)
    dist = qi - kj
    valid = (dist >= 0) & (dist < REL_MAX_DIST) & (kj >= jnp.where(first, lq, 0))
    lane0 = (lax.broadcasted_iota(jnp.int32, (1, 128), 1) == 0).astype(F32)
    group = N_Q_HEADS // N_KV_HEADS
    outs = []
    for hk in range(N_KV_HEADS):
        cols = slice(hk * HEAD_DIM, (hk + 1) * HEAD_DIM)
        kb = jnp.concatenate([kp[:, cols], kc[:, cols]], axis=0)
        vb = jnp.concatenate([vp[:, cols], vc[:, cols]], axis=0)
        for g in range(group):
            h = hk * group + g
            sc = mm_nt(q[:, h * HEAD_DIM:(h + 1) * HEAD_DIM], kb) * (HEAD_DIM ** -0.5) + bias[h]
            sc = jnp.where(valid, sc, -1e30)
            sink = jnp.sum(sinks[h] * lane0, axis=-1, keepdims=True)
            m = lax.stop_gradient(jnp.maximum(jnp.max(sc, axis=-1, keepdims=True), sink))
            p = jnp.exp(sc - m)
            probs = p / (jnp.sum(p, axis=-1, keepdims=True) + jnp.exp(sink - m))
            outs.append(mm(probs, vb))
    return jnp.concatenate(outs, axis=1)


def _attn_in_specs(nb, clamp):
    lq = ATTN_BLOCK
    cur = lambda n: jnp.minimum(n, nb - 1) if clamp else n
    prev = lambda n: jnp.maximum(cur(n) - 1, 0)
    kcol, vcol = ATTN_WIDTH // KV_WIDTH, ATTN_WIDTH // KV_WIDTH + 1
    return [pl.BlockSpec((lq, ATTN_WIDTH), lambda n: (cur(n), 0)),
            pl.BlockSpec((lq, KV_WIDTH), lambda n: (prev(n), kcol)),
            pl.BlockSpec((lq, KV_WIDTH), lambda n: (cur(n), kcol)),
            pl.BlockSpec((lq, KV_WIDTH), lambda n: (prev(n), vcol)),
            pl.BlockSpec((lq, KV_WIDTH), lambda n: (cur(n), vcol)),
            pl.BlockSpec((N_Q_HEADS, lq, 2 * lq), lambda n: (0, 0, 0)),
            pl.BlockSpec((N_Q_HEADS, 128), _const)]


def attn_fwd(qkv, bias, sinks_rows):
    s = qkv.shape[0]
    nb = s // ATTN_BLOCK

    def body(q_ref, kp_ref, kc_ref, vp_ref, vc_ref, bias_ref, sk_ref, o_ref):
        f = lambda r: r[...].astype(F32)
        o = _attn_block(f(q_ref), f(kp_ref), f(kc_ref), f(vp_ref), f(vc_ref),
                        [bias_ref[h] for h in range(N_Q_HEADS)],
                        [sk_ref[h:h + 1, :] for h in range(N_Q_HEADS)], pl.program_id(0) == 0)
        o_ref[...] = _b(o)

    return pl.pallas_call(
        body, name="attn_fwd", grid=(nb,),
        in_specs=_attn_in_specs(nb, False),
        out_specs=pl.BlockSpec((ATTN_BLOCK, ATTN_WIDTH), _row),
        out_shape=jax.ShapeDtypeStruct((s, ATTN_WIDTH), BF16),
        compiler_params=_params())(qkv, qkv, qkv, qkv, qkv, bias, sinks_rows)


def attn_bwd(qkv, bias, sinks_rows, d_attn):
    s = qkv.shape[0]
    lq = ATTN_BLOCK
    nb = s // lq

    def body(q_ref, kp_ref, kc_ref, vp_ref, vc_ref, bias_ref, sk_ref, do_ref,
             dq_ref, dk_ref, dv_ref, dbias_ref, dsk_ref, carry_k, carry_v):
        n = pl.program_id(0)

        @pl.when(n == 0)
        def _():
            dbias_ref[...] = jnp.zeros_like(dbias_ref)
            dsk_ref[...] = jnp.zeros_like(dsk_ref)
            carry_k[...] = jnp.zeros_like(carry_k)
            carry_v[...] = jnp.zeros_like(carry_v)

        @pl.when(n < nb)
        def _():
            f = lambda r: r[...].astype(F32)
            fn = functools.partial(_attn_block, first=(n == 0))
            _, vjp = jax.vjp(fn, f(q_ref), f(kp_ref), f(kc_ref), f(vp_ref), f(vc_ref),
                             [bias_ref[h] for h in range(N_Q_HEADS)],
                             [sk_ref[h:h + 1, :] for h in range(N_Q_HEADS)])
            dq, dkp, dkc, dvp, dvc, dbias, dsk = vjp(f(do_ref))
            dq_ref[...] = _b(dq)
            dk_ref[...] = _b(carry_k[...] + dkp)
            dv_ref[...] = _b(carry_v[...] + dvp)
            carry_k[...] = dkc
            carry_v[...] = dvc
            for h in range(N_Q_HEADS):
                dbias_ref[h] += dbias[h]
                dsk_ref[h:h + 1, :] += dsk[h]

        @pl.when(n == nb)
        def _():
            dk_ref[...] = _b(carry_k[...])
            dv_ref[...] = _b(carry_v[...])

    cur = lambda n: (jnp.minimum(n, nb - 1), 0)
    done = lambda n: (jnp.maximum(n - 1, 0), 0)
    return pl.pallas_call(
        body, name="attn_bwd", grid=(nb + 1,),
        in_specs=_attn_in_specs(nb, True) + [pl.BlockSpec((lq, ATTN_WIDTH), cur)],
        out_specs=[pl.BlockSpec((lq, ATTN_WIDTH), cur), pl.BlockSpec((lq, KV_WIDTH), done),
                   pl.BlockSpec((lq, KV_WIDTH), done),
                   pl.BlockSpec((N_Q_HEADS, lq, 2 * lq), lambda n: (0, 0, 0)), pl.BlockSpec((N_Q_HEADS, 128), _const)],
        out_shape=[jax.ShapeDtypeStruct((s, ATTN_WIDTH), BF16), jax.ShapeDtypeStruct((s, KV_WIDTH), BF16),
                   jax.ShapeDtypeStruct((s, KV_WIDTH), BF16),
                   jax.ShapeDtypeStruct((N_Q_HEADS, lq, 2 * lq), F32), jax.ShapeDtypeStruct((N_Q_HEADS, 128), F32)],
        scratch_shapes=[pltpu.VMEM((lq, KV_WIDTH), F32), pltpu.VMEM((lq, KV_WIDTH), F32)],
        compiler_params=_params())(qkv, qkv, qkv, qkv, qkv, bias, sinks_rows, d_attn)


def rel_bias_grad(dbias, bucket):
    def body(db_ref, bk_ref, o_ref):
        rows = lax.broadcasted_iota(jnp.int32, (N_BUCKETS, 128), 0)
        lanes = lax.broadcasted_iota(jnp.int32, (N_BUCKETS, 128), 1)
        bk = bk_ref[...]

        def per_bucket(b, acc):
            hit = (bk == b).astype(F32)
            for h in range(N_Q_HEADS):
                val = jnp.sum(db_ref[h] * hit, keepdims=True)
                acc = acc + jnp.where((rows == b) & (lanes == h), val, 0.0)
            return acc

        o_ref[...] = lax.fori_loop(0, N_BUCKETS, per_bucket, jnp.zeros((N_BUCKETS, 128), F32))

    return pl.pallas_call(body, name="rel_bias_grad", out_shape=jax.ShapeDtypeStruct((N_BUCKETS, 128), F32),
                          compiler_params=_params(n_axes=0))(dbias, bucket)


def _ssm_core(u, z, dt_raw, hprev, dt_bias, a_log, dskip, norm_w):
    lc = u.shape[0]
    xbc = _silu(u)
    xs, bm, cm = xbc[:, :SSM_WIDTH], xbc[:, SSM_WIDTH:SSM_WIDTH + SSM_BC], xbc[:, SSM_WIDTH + SSM_BC:]
    dt = jax.nn.softplus(dt_raw + dt_bias)
    adt = dt * (-jnp.exp(a_log))
    ri = lax.broadcasted_iota(jnp.int32, (lc, lc), 0)
    ci = lax.broadcasted_iota(jnp.int32, (lc, lc), 1)
    causal = ri >= ci
    acum = jnp.dot(causal.astype(F32), adt, precision=HIGHEST)
    acum_t = acum.T
    last = acum[lc - 1:lc, :]
    per_group = SSM_HEADS // SSM_GROUPS
    ys, hs = [], []
    for g in range(SSM_GROUPS):
        bg = bm[:, g * SSM_STATE:(g + 1) * SSM_STATE]
        cg = cm[:, g * SSM_STATE:(g + 1) * SSM_STATE]
        cb = mm_nt(cg, bg)
        for hh in range(per_group):
            h = g * per_group + hh
            xh = xs[:, h * SSM_HEAD_DIM:(h + 1) * SSM_HEAD_DIM]
            xc = xh * dt[:, h:h + 1]
            col, rowv, lasth = acum[:, h:h + 1], acum_t[h:h + 1, :], last[:, h:h + 1]
            decay = jnp.exp(jnp.where(causal, col - rowv, -1e30))
            hp = hprev[h * SSM_HEAD_DIM:(h + 1) * SSM_HEAD_DIM, :]
            y = mm(cb * decay, xc) + mm_nt(cg * jnp.exp(col), hp)
            st = mm_tn(xc, bg * jnp.exp(lasth - col))
            hs.append(jnp.exp(lasth) * hp + st)
            ys.append(y + dskip[:, h:h + 1] * xh)
    y = jnp.concatenate(ys, axis=1) * _silu(z)
    gw = SSM_WIDTH // SSM_GROUPS
    outs = []
    for g in range(SSM_GROUPS):
        yg = y[:, g * gw:(g + 1) * gw]
        outs.append(yg * lax.rsqrt(jnp.mean(yg * yg, axis=-1, keepdims=True) + NORM_EPS))
    return jnp.concatenate(outs, axis=1) * norm_w, jnp.concatenate(hs, axis=0)


def _ssm_param_specs():
    return [pl.BlockSpec((SSM_CONV, XBC_WIDTH), _const), _vec(XBC_WIDTH), _vec(128), _vec(128), _vec(128),
            _vec(SSM_WIDTH)]


def ssm_fwd(xbc_raw, z, dt_raw, conv_w, conv_b, dt_bias, a_log, dskip, norm_w):
    s = xbc_raw.shape[0]
    lc = SSM_CHUNK
    nc = s // lc
    hrows = SSM_HEADS * SSM_HEAD_DIM

    def body(x_ref, halo_ref, z_ref, dt_ref, cw_ref, cb_ref, dtb_ref, al_ref, dk_ref, nw_ref,
             o_ref, hp_ref, state):
        i = pl.program_id(0)

        @pl.when(i == 0)
        def _():
            state[...] = jnp.zeros_like(state)

        halo = halo_ref[...] * (i > 0).astype(F32)
        xin = jnp.concatenate([halo, x_ref[...]], axis=0)
        u = (_conv_rows(xin, cw_ref[...], SSM_CONV) + cb_ref[...])[HALO:]
        hprev = state[...]
        hp_ref[...] = hprev
        out, hnew = _ssm_core(u, z_ref[...], dt_ref[...], hprev, dtb_ref[...], al_ref[...], dk_ref[...], nw_ref[...])
        o_ref[...] = _b(out)
        state[...] = hnew

    tile = lambda n: pl.BlockSpec((lc, n), _row)
    halo_spec = pl.BlockSpec((HALO, XBC_WIDTH), lambda i: (jnp.maximum(i * (lc // HALO) - 1, 0), 0))
    return pl.pallas_call(
        body, name="ssm_fwd", grid=(nc,),
        in_specs=[tile(XBC_WIDTH), halo_spec, tile(SSM_WIDTH), tile(DT_PAD)] + _ssm_param_specs(),
        out_specs=[tile(SSM_WIDTH), pl.BlockSpec((hrows, SSM_STATE), _row)],
        out_shape=[jax.ShapeDtypeStruct((s, SSM_WIDTH), BF16), jax.ShapeDtypeStruct((nc * hrows, SSM_STATE), F32)],
        scratch_shapes=[pltpu.VMEM((hrows, SSM_STATE), F32)],
        compiler_params=_params())(xbc_raw, xbc_raw, z, dt_raw, conv_w, conv_b, dt_bias, a_log, dskip, norm_w)


def ssm_bwd(xbc_raw, z, dt_raw, hprev_all, d_out, conv_w, conv_b, dt_bias, a_log, dskip, norm_w):
    s = xbc_raw.shape[0]
    lc = SSM_CHUNK
    nc = s // lc
    hrows = SSM_HEADS * SSM_HEAD_DIM

    def body(x_ref, halo_ref, z_ref, dt_ref, hp_ref, do_ref, cw_ref, cb_ref, dtb_ref, al_ref, dk_ref, nw_ref,
             dx_ref, dz_ref, ddt_ref, dcw_ref, dcb_ref, ddtb_ref, dal_ref, ddk_ref, dnw_ref, dstate, du_next):
        i = pl.program_id(0)
        chunk = nc - 1 - i

        @pl.when(i == 0)
        def _():
            dstate[...] = jnp.zeros_like(dstate)
            du_next[...] = jnp.zeros_like(du_next)
            for r in (dcw_ref, dcb_ref, ddtb_ref, dal_ref, ddk_ref, dnw_ref):
                r[...] = jnp.zeros_like(r)

        halo = halo_ref[...] * (chunk > 0).astype(F32)
        xin = jnp.concatenate([halo, x_ref[...]], axis=0)
        cw = cw_ref[...]
        u = (_conv_rows(xin, cw, SSM_CONV) + cb_ref[...])[HALO:]
        _, vjp = jax.vjp(_ssm_core, u, z_ref[...], dt_ref[...], hp_ref[...], dtb_ref[...], al_ref[...],
                         dk_ref[...], nw_ref[...])
        du, dz, ddt, dhp, ddtb, dal, ddk, dnw = vjp((do_ref[...], dstate[...]))
        dstate[...] = dhp
        dz_ref[...] = dz
        ddt_ref[...] = ddt
        du_ext = jnp.concatenate([du, du_next[...]], axis=0)
        dx_ref[...] = _conv_rows_t(du_ext, cw, SSM_CONV)[:lc]
        du_next[...] = du[:HALO]
        rows = [jnp.sum(du * pltpu.roll(xin, j, axis=0)[HALO:] if j else du * xin[HALO:], axis=0, keepdims=True)
                for j in range(SSM_CONV)]
        dcw_ref[...] += jnp.concatenate(rows[::-1] + [jnp.zeros((8 - SSM_CONV, XBC_WIDTH), F32)], axis=0)
        dcb_ref[...] += jnp.sum(du, axis=0, keepdims=True)
        ddtb_ref[...] += ddtb
        dal_ref[...] += dal
        ddk_ref[...] += ddk
        dnw_ref[...] += dnw

    rev = lambda i: (nc - 1 - i, 0)
    tile = lambda n: pl.BlockSpec((lc, n), rev)
    halo_spec = pl.BlockSpec((HALO, XBC_WIDTH), lambda i: (jnp.maximum((nc - 1 - i) * (lc // HALO) - 1, 0), 0))
    acc = lambda r, n: pl.BlockSpec((r, n), _const)
    return pl.pallas_call(
        body, name="ssm_bwd", grid=(nc,),
        in_specs=[tile(XBC_WIDTH), halo_spec, tile(SSM_WIDTH), tile(DT_PAD), pl.BlockSpec((hrows, SSM_STATE), rev),
                  tile(SSM_WIDTH)] + _ssm_param_specs(),
        out_specs=[tile(XBC_WIDTH), tile(SSM_WIDTH), tile(DT_PAD), acc(8, XBC_WIDTH), acc(1, XBC_WIDTH),
                   acc(1, 128), acc(1, 128), acc(1, 128), acc(1, SSM_WIDTH)],
        out_shape=[jax.ShapeDtypeStruct((s, XBC_WIDTH), F32), jax.ShapeDtypeStruct((s, SSM_WIDTH), F32),
                   jax.ShapeDtypeStruct((s, DT_PAD), F32), jax.ShapeDtypeStruct((8, XBC_WIDTH), F32),
                   jax.ShapeDtypeStruct((1, XBC_WIDTH), F32), jax.ShapeDtypeStruct((1, 128), F32),
                   jax.ShapeDtypeStruct((1, 128), F32), jax.ShapeDtypeStruct((1, 128), F32),
                   jax.ShapeDtypeStruct((1, SSM_WIDTH), F32)],
        scratch_shapes=[pltpu.VMEM((hrows, SSM_STATE), F32), pltpu.VMEM((HALO, XBC_WIDTH), F32)],
        compiler_params=_params())(xbc_raw, xbc_raw, z, dt_raw, hprev_all, d_out, conv_w, conv_b, dt_bias, a_log,
                                   dskip, norm_w)


def mix_out(attn, ssm, x, w_out16, gate1, post_mix_w, pre_ffn_w, scale2, shift2, tm=TOKEN_TILE):
    s = x.shape[0]

    def body(a_ref, s_ref, x_ref, w_ref, g_ref, pw_ref, fw_ref, sc_ref, sh_ref, mixed_ref, x1_ref, h2_ref):
        mixed = (jnp.dot(a_ref[...], w_ref[:ATTN_WIDTH, :], preferred_element_type=F32)
                 + jnp.dot(s_ref[...], w_ref[ATTN_WIDTH:, :], preferred_element_type=F32))
        mixed_ref[...] = mixed
        x1 = x_ref[...] + g_ref[...] * _rms(mixed, pw_ref[...])
        x1_ref[...] = x1
        h2_ref[...] = _b(_norm_mod(x1, fw_ref[...], sc_ref[...], sh_ref[...]))

    tile = lambda n: pl.BlockSpec((tm, n), _row)
    return pl.pallas_call(
        body, name="mix_out", grid=(s // tm,),
        in_specs=[tile(ATTN_WIDTH), tile(SSM_WIDTH), tile(D_MODEL), pl.BlockSpec((D_MODEL, D_MODEL), _const)]
        + [_vec(D_MODEL)] * 5,
        out_specs=[tile(D_MODEL)] * 3,
        out_shape=[jax.ShapeDtypeStruct((s, D_MODEL), F32), jax.ShapeDtypeStruct((s, D_MODEL), F32),
                   jax.ShapeDtypeStruct((s, D_MODEL), BF16)],
        compiler_params=_params())(attn, ssm, x, w_out16, gate1, post_mix_w, pre_ffn_w, scale2, shift2)


def _gate(ug, uv):
    return jax.nn.gelu(ug, approximate=True) * uv


def ffn_fwd(u_raw, conv_w, conv_b, w_down16, x1, target, gate2, post_ffn_w, tm=TOKEN_TILE):
    s = x1.shape[0]

    def body(u_ref, halo_ref, cw_ref, cb_ref, wd_ref, x1_ref, t_ref, g_ref, pw_ref,
             f_ref, dffn_ref, dy_ref, loss_ref, dg_ref, dpw_ref):
        i = pl.program_id(0)

        @pl.when(i == 0)
        def _():
            loss_ref[...] = jnp.zeros_like(loss_ref)
            dg_ref[...] = jnp.zeros_like(dg_ref)
            dpw_ref[...] = jnp.zeros_like(dpw_ref)

        keep = (i > 0).astype(F32)
        ffn = jnp.zeros((tm, D_MODEL), F32)
        for lo in range(0, D_FF, FF_CHUNK):
            halves = []
            for base in (lo, D_FF + lo):
                cols = slice(base, base + FF_CHUNK)
                xin = jnp.concatenate([halo_ref[:, cols] * keep, u_ref[:, cols]], axis=0)
                halves.append((_conv_rows(xin, cw_ref[:, cols], FFN_CONV) + cb_ref[:, cols])[HALO:])
            f16 = _b(_gate(*halves))
            f_ref[:, lo:lo + FF_CHUNK] = f16
            ffn = ffn + jnp.dot(f16, wd_ref[lo:lo + FF_CHUNK, :], preferred_element_type=F32)
        x1 = x1_ref[...]
        post = lambda ffn_, g_, w_: x1 + g_ * _rms(ffn_, w_)
        x2, vjp = jax.vjp(post, ffn, g_ref[...], pw_ref[...])
        err = x2 - t_ref[...]
        dy = err * (1.0 / D_MODEL)
        dy_ref[...] = dy
        loss_ref[...] += 0.5 * jnp.sum(jnp.mean(err * err, axis=-1, keepdims=True))
        dffn, dg, dpw = vjp(dy)
        dffn_ref[...] = _b(dffn)
        dg_ref[...] += dg
        dpw_ref[...] += dpw

    tile = lambda n: pl.BlockSpec((tm, n), _row)
    halo_spec = pl.BlockSpec((HALO, 2 * D_FF), lambda i: (jnp.maximum(i * (tm // HALO) - 1, 0), 0))
    return pl.pallas_call(
        body, name="ffn_fwd", grid=(s // tm,),
        in_specs=[tile(2 * D_FF), halo_spec, pl.BlockSpec((FFN_CONV, 2 * D_FF), _const), _vec(2 * D_FF),
                  pl.BlockSpec((D_FF, D_MODEL), _const), tile(D_MODEL), tile(D_MODEL), _vec(D_MODEL), _vec(D_MODEL)],
        out_specs=[tile(D_FF), tile(D_MODEL), tile(D_MODEL), _vec(128), _vec(D_MODEL), _vec(D_MODEL)],
        out_shape=[jax.ShapeDtypeStruct((s, D_FF), BF16), jax.ShapeDtypeStruct((s, D_MODEL), BF16),
                   jax.ShapeDtypeStruct((s, D_MODEL), F32), jax.ShapeDtypeStruct((1, 128), F32),
                   jax.ShapeDtypeStruct((1, D_MODEL), F32), jax.ShapeDtypeStruct((1, D_MODEL), F32)],
        compiler_params=_params())(u_raw, u_raw, conv_w, conv_b, w_down16, x1, target, gate2, post_ffn_w)


def ffn_bwd(u_raw, d_ffn, conv_w, conv_b, w_down_t16, tm=TOKEN_TILE):
    s = u_raw.shape[0]
    nt = s // tm
    ext = HALO + tm + NEXT

    def body(u_ref, halo_ref, unext_ref, d_ref, dnext_ref, cw_ref, cb_ref, wdt_ref, du_ref, dcw_ref, dcb_ref):
        i = pl.program_id(0)

        @pl.when(i == 0)
        def _():
            dcw_ref[...] = jnp.zeros_like(dcw_ref)
            dcb_ref[...] = jnp.zeros_like(dcb_ref)

        keep_prev = (i > 0).astype(F32)
        keep_next = (i < nt - 1).astype(F32)
        dff = jnp.concatenate([jnp.zeros((HALO, D_MODEL), F32), d_ref[...].astype(F32),
                               dnext_ref[...].astype(F32) * keep_next], axis=0)
        rid = lax.broadcasted_iota(jnp.int32, (ext, 1), 0)
        own = ((rid >= HALO) & (rid < HALO + tm)).astype(F32)
        for lo in range(0, D_FF, FF_CHUNK):
            xins, us, cws = [], [], []
            for base in (lo, D_FF + lo):
                cols = slice(base, base + FF_CHUNK)
                xin = jnp.concatenate([halo_ref[:, cols] * keep_prev, u_ref[:, cols], unext_ref[:, cols]], axis=0)
                xins.append(xin)
                cws.append(cw_ref[:, cols])
                us.append(_conv_rows(xin, cws[-1], FFN_CONV) + cb_ref[:, cols])
            df = jnp.dot(_b(dff), wdt_ref[:, lo:lo + FF_CHUNK], preferred_element_type=F32)
            _, vjp = jax.vjp(_gate, us[0], us[1])
            for base, xin, cw, du in zip((lo, D_FF + lo), xins, cws, vjp(df)):
                cols = slice(base, base + FF_CHUNK)
                du_ref[:, cols] = _b(_conv_rows_t(du, cw, FFN_CONV)[HALO:HALO + tm])
                du_own = du * own
                rows = [jnp.sum(du_own * (pltpu.roll(xin, j, axis=0) if j else xin), axis=0, keepdims=True)
                        for j in range(FFN_CONV)]
                dcw_ref[:, cols] += jnp.concatenate(rows[::-1] + [jnp.zeros((8 - FFN_CONV, FF_CHUNK), F32)], axis=0)
                dcb_ref[:, cols] += jnp.sum(du_own, axis=0, keepdims=True)

    tile = lambda n: pl.BlockSpec((tm, n), _row)
    halo_spec = pl.BlockSpec((HALO, 2 * D_FF), lambda i: (jnp.maximum(i * (tm // HALO) - 1, 0), 0))
    nxt = lambda i: (jnp.minimum((i + 1) * (tm // NEXT), s // NEXT - 1), 0)
    return pl.pallas_call(
        body, name="ffn_bwd", grid=(nt,),
        in_specs=[tile(2 * D_FF), halo_spec, pl.BlockSpec((NEXT, 2 * D_FF), nxt), tile(D_MODEL),
                  pl.BlockSpec((NEXT, D_MODEL), nxt), pl.BlockSpec((FFN_CONV, 2 * D_FF), _const), _vec(2 * D_FF),
                  pl.BlockSpec((D_MODEL, D_FF), _const)],
        out_specs=[tile(2 * D_FF), pl.BlockSpec((8, 2 * D_FF), _const), _vec(2 * D_FF)],
        out_shape=[jax.ShapeDtypeStruct((s, 2 * D_FF), BF16), jax.ShapeDtypeStruct((8, 2 * D_FF), F32),
                   jax.ShapeDtypeStruct((1, 2 * D_FF), F32)],
        compiler_params=_params())(u_raw, u_raw, u_raw, d_ffn, d_ffn, conv_w, conv_b, w_down_t16)


def mix_bwd(dh2, x1, dy, mixed, w_out_t16, pre_ffn_w, scale2, shift2, gate1, post_mix_w, tm=TOKEN_TILE):
    s = x1.shape[0]

    def body(dh_ref, x1_ref, dy_ref, mx_ref, w_ref, fw_ref, sc_ref, sh_ref, g_ref, pw_ref,
             dx1_ref, dm_ref, da_ref, ds_ref, dfw_ref, dsc_ref, dsh_ref, dg_ref, dpw_ref):
        accs = (dfw_ref, dsc_ref, dsh_ref, dg_ref, dpw_ref)

        @pl.when(pl.program_id(0) == 0)
        def _():
            for r in accs:
                r[...] = jnp.zeros_like(r)

        _, vjp = jax.vjp(_norm_mod, x1_ref[...], fw_ref[...], sc_ref[...], sh_ref[...])
        dx1, dfw, dsc, dsh = vjp(dh_ref[...])
        dx1 = dx1 + dy_ref[...]
        dx1_ref[...] = dx1
        post = lambda m_, g_, w_: g_ * _rms(m_, w_)
        _, vjp2 = jax.vjp(post, mx_ref[...], g_ref[...], pw_ref[...])
        dmixed, dg, dpw = vjp2(dx1)
        dm16 = _b(dmixed)
        dm_ref[...] = dm16
        dmix_in = jnp.dot(dm16, w_ref[...], preferred_element_type=F32)
        da_ref[...] = _b(dmix_in[:, :ATTN_WIDTH])
        ds_ref[...] = dmix_in[:, ATTN_WIDTH:]
        for r, v in zip(accs, (dfw, dsc, dsh, dg, dpw)):
            r[...] += v

    tile = lambda n: pl.BlockSpec((tm, n), _row)
    return pl.pallas_call(
        body, name="mix_bwd", grid=(s // tm,),
        in_specs=[tile(D_MODEL)] * 4 + [pl.BlockSpec((D_MODEL, D_MODEL), _const)] + [_vec(D_MODEL)] * 5,
        out_specs=[tile(D_MODEL), tile(D_MODEL), tile(ATTN_WIDTH), tile(SSM_WIDTH)] + [_vec(D_MODEL)] * 5,
        out_shape=[jax.ShapeDtypeStruct((s, D_MODEL), F32), jax.ShapeDtypeStruct((s, D_MODEL), BF16),
                   jax.ShapeDtypeStruct((s, ATTN_WIDTH), BF16), jax.ShapeDtypeStruct((s, SSM_WIDTH), F32)]
        + [jax.ShapeDtypeStruct((1, D_MODEL), F32)] * 5,
        compiler_params=_params())(dh2, x1, dy, mixed, w_out_t16, pre_ffn_w, scale2, shift2, gate1, post_mix_w)


def inproj_bwd(dqkv, dxbc, dz, ddt, x, dx1, w_in_t16, pre_mix_w, scale1, shift1, tm=TOKEN_TILE):
    s = x.shape[0]

    def body(dq_ref, dxbc_ref, dz_ref, ddt_ref, x_ref, dx1_ref, w_ref, pw_ref, sc_ref, sh_ref,
             gx_ref, dpw_ref, dsc_ref, dsh_ref):
        accs = (dpw_ref, dsc_ref, dsh_ref)

        @pl.when(pl.program_id(0) == 0)
        def _():
            for r in accs:
                r[...] = jnp.zeros_like(r)

        dot = lambda r, lo, hi: jnp.dot(_b(r[...]), w_ref[lo:hi, :], preferred_element_type=F32)
        dh = (dot(dq_ref, 0, OFF_XBC) + dot(dxbc_ref, OFF_XBC, OFF_Z) + dot(dz_ref, OFF_Z, OFF_DT)
              + dot(ddt_ref, OFF_DT, PROJ_PAD))
        _, vjp = jax.vjp(_norm_mod, x_ref[...], pw_ref[...], sc_ref[...], sh_ref[...])
        dx, dpw, dsc, dsh = vjp(dh)
        gx_ref[...] = dx1_ref[...] + dx
        for r, v in zip(accs, (dpw, dsc, dsh)):
            r[...] += v

    tile = lambda n: pl.BlockSpec((tm, n), _row)
    return pl.pallas_call(
        body, name="inproj_bwd", grid=(s // tm,),
        in_specs=[tile(QKV_W), tile(XBC_WIDTH), tile(SSM_WIDTH), tile(DT_PAD), tile(D_MODEL), tile(D_MODEL),
                  pl.BlockSpec((PROJ_PAD, D_MODEL), _const)] + [_vec(D_MODEL)] * 3,
        out_specs=[tile(D_MODEL)] + [_vec(D_MODEL)] * 3,
        out_shape=[jax.ShapeDtypeStruct((s, D_MODEL), F32)] + [jax.ShapeDtypeStruct((1, D_MODEL), F32)] * 3,
        compiler_params=_params())(dqkv, dxbc, dz, ddt, x, dx1, w_in_t16, pre_mix_w, scale1, shift1)


def adamw(parts, w, m, v, name):
    p, r, n = parts.shape
    tr = r
    for cand in (256, 128, 64, 32, 16, 8):
        if r % cand == 0 and r > cand:
            tr = cand
            break

    def body(p_ref, w_ref, m_ref, v_ref, g_ref, d_ref, nm_ref, nv_ref):
        g = p_ref[0]
        for k in range(1, p):
            g = g + p_ref[k]
        new_m = ADAM_B1 * m_ref[...] + (1.0 - ADAM_B1) * g
        new_v = ADAM_B2 * v_ref[...] + (1.0 - ADAM_B2) * jnp.square(g)
        m_hat = new_m / (1.0 - ADAM_B1 ** ADAM_STEP)
        v_hat = new_v / (1.0 - ADAM_B2 ** ADAM_STEP)
        g_ref[...] = g
        d_ref[...] = -ADAM_LR * (m_hat / (jnp.sqrt(v_hat) + ADAM_EPS) + ADAM_WD * w_ref[...])
        nm_ref[...] = new_m
        nv_ref[...] = new_v

    tile = pl.BlockSpec((tr, n), _row)
    return pl.pallas_call(
        body, name=name, grid=(r // tr,),
        in_specs=[pl.BlockSpec((p, tr, n), lambda i: (0, i, 0)), tile, tile, tile],
        out_specs=[tile] * 4, out_shape=[jax.ShapeDtypeStruct((r, n), F32)] * 4,
        compiler_params=_params())(parts, w, m, v)


def _bucket_table():
    lq = ATTN_BLOCK
    qi = np.arange(lq)[:, None] + lq
    kj = np.arange(2 * lq)[None, :]
    dist = qi - kj
    d = np.maximum(dist, 0)
    max_exact = N_BUCKETS // 2
    nf = np.maximum(d, 1).astype(np.float32)
    large = max_exact + (np.log(nf / max_exact) / math.log(REL_MAX_DIST / max_exact)
                         * (N_BUCKETS - max_exact)).astype(np.int32)
    large = np.minimum(large, N_BUCKETS - 1)
    bucket = np.where(d < max_exact, d, large).astype(np.int32)
    in_band = (dist >= 0) & (dist < REL_MAX_DIST)
    return bucket, np.where(in_band, bucket, -1).astype(np.int32)


def _cols_from_blocks(g):
    return jnp.transpose(g, (1, 0, 2)).reshape(g.shape[1], N_DEV * g.shape[2])


def _cols_to_blocks(a):
    r, n = a.shape
    return jnp.transpose(a.reshape(r, N_DEV, n // N_DEV), (1, 0, 2))


def _perm_in(w):
    pad = jnp.zeros((w.shape[0], DT_PAD - SSM_HEADS), w.dtype)
    return jnp.concatenate([w[:, :768], w[:, 768:1280], w[:, 1792:2304], w[:, 1280:1792], w[:, 2304:2312], pad], axis=1)


def _unperm_in(g):
    return jnp.concatenate([g[:, :768], g[:, 768:1280], g[:, 1792:2304], g[:, 1280:1792], g[:, 2304:2312]], axis=1)


def _lane_pad(v, n=128):
    return jnp.pad(v, ((0, 0), (0, n - v.shape[1])))


_SMALL = (("rel_bias", (N_BUCKETS, N_Q_HEADS)), ("b_ada", (1, N_MOD * D_MODEL)), ("pre_mix_w", (1, D_MODEL)),
          ("attn_sinks", (1, N_Q_HEADS)), ("ssm_conv_b", (1, XBC_WIDTH)), ("ssm_dt_bias", (1, SSM_HEADS)),
          ("ssm_a_log", (1, SSM_HEADS)), ("ssm_d", (1, SSM_HEADS)), ("ssm_norm_w", (1, SSM_WIDTH)),
          ("post_mix_w", (1, D_MODEL)), ("pre_ffn_w", (1, D_MODEL)), ("ffn_conv_b", (1, 2 * D_FF)),
          ("post_ffn_w", (1, D_MODEL)))
_SLAB_ROWS = 160


def _pack_small(vals, head):
    flat = jnp.concatenate([head.reshape(-1)] + [vals[k].reshape(-1) for k, _ in _SMALL])
    return jnp.pad(flat, (0, _SLAB_ROWS * 128 - flat.shape[0])).reshape(_SLAB_ROWS, 128)


def _unpack_small(slab):
    flat = slab.reshape(-1)
    out, off = {}, 128
    for k, shp in _SMALL:
        size = shp[0] * shp[1]
        out[k] = flat[off:off + size].reshape(shp)
        off += size
    return flat[0], out


def kernel(x, c, rel_bias, w_ada, b_ada, pre_mix_w, w_in, attn_sinks, ssm_conv_w, ssm_conv_b, ssm_dt_bias, ssm_a_log, ssm_d, ssm_norm_w, w_out, post_mix_w, pre_ffn_w, w_up, ffn_conv_w, ffn_conv_b, w_down, post_ffn_w, loss_target, m_rel_bias, m_w_ada, m_b_ada, m_pre_mix_w, m_w_in, m_attn_sinks, m_ssm_conv_w, m_ssm_conv_b, m_ssm_dt_bias, m_ssm_a_log, m_ssm_d, m_ssm_norm_w, m_w_out, m_post_mix_w, m_pre_ffn_w, m_w_up, m_ffn_conv_w, m_ffn_conv_b, m_w_down, m_post_ffn_w, v_rel_bias, v_w_ada, v_b_ada, v_pre_mix_w, v_w_in, v_attn_sinks, v_ssm_conv_w, v_ssm_conv_b, v_ssm_dt_bias, v_ssm_a_log, v_ssm_d, v_ssm_norm_w, v_w_out, v_post_mix_w, v_pre_ffn_w, v_w_up, v_ffn_conv_w, v_ffn_conv_b, v_w_down, v_post_ffn_w):
    weights = dict(rel_bias=rel_bias, w_ada=w_ada, b_ada=b_ada, pre_mix_w=pre_mix_w, w_in=w_in, attn_sinks=attn_sinks, ssm_conv_w=ssm_conv_w, ssm_conv_b=ssm_conv_b, ssm_dt_bias=ssm_dt_bias, ssm_a_log=ssm_a_log, ssm_d=ssm_d, ssm_norm_w=ssm_norm_w, w_out=w_out, post_mix_w=post_mix_w, pre_ffn_w=pre_ffn_w, w_up=w_up, ffn_conv_w=ffn_conv_w, ffn_conv_b=ffn_conv_b, w_down=w_down, post_ffn_w=post_ffn_w)
    mom_m = dict(rel_bias=m_rel_bias, w_ada=m_w_ada, b_ada=m_b_ada, pre_mix_w=m_pre_mix_w, w_in=m_w_in, attn_sinks=m_attn_sinks, ssm_conv_w=m_ssm_conv_w, ssm_conv_b=m_ssm_conv_b, ssm_dt_bias=m_ssm_dt_bias, ssm_a_log=m_ssm_a_log, ssm_d=m_ssm_d, ssm_norm_w=m_ssm_norm_w, w_out=m_w_out, post_mix_w=m_post_mix_w, pre_ffn_w=m_pre_ffn_w, w_up=m_w_up, ffn_conv_w=m_ffn_conv_w, ffn_conv_b=m_ffn_conv_b, w_down=m_w_down, post_ffn_w=m_post_ffn_w)
    mom_v = dict(rel_bias=v_rel_bias, w_ada=v_w_ada, b_ada=v_b_ada, pre_mix_w=v_pre_mix_w, w_in=v_w_in, attn_sinks=v_attn_sinks, ssm_conv_w=v_ssm_conv_w, ssm_conv_b=v_ssm_conv_b, ssm_dt_bias=v_ssm_dt_bias, ssm_a_log=v_ssm_a_log, ssm_d=v_ssm_d, ssm_norm_w=v_ssm_norm_w, w_out=v_w_out, post_mix_w=v_post_mix_w, pre_ffn_w=v_pre_ffn_w, w_up=v_w_up, ffn_conv_w=v_ffn_conv_w, ffn_conv_b=v_ffn_conv_b, w_down=v_w_down, post_ffn_w=v_post_ffn_w)
    order = ['rel_bias', 'w_ada', 'b_ada', 'pre_mix_w', 'w_in', 'attn_sinks', 'ssm_conv_w', 'ssm_conv_b', 'ssm_dt_bias', 'ssm_a_log', 'ssm_d', 'ssm_norm_w', 'w_out', 'post_mix_w', 'pre_ffn_w', 'w_up', 'ffn_conv_w', 'ffn_conv_b', 'w_down', 'post_ffn_w']

    me = 4 * lax.axis_index("x") + 2 * lax.axis_index("y") + lax.axis_index("c")
    xs_ = x[0]
    target = loss_target[0]

    (w_in_g, w_out_g, w_up_g, w_down_g, scw_g, fcw_g, c_g) = all_gather(
        [_b(w_in[0]), _b(w_out[0]), _b(w_up[0]), _b(w_down[0]), ssm_conv_w[0], ffn_conv_w[0], c], "gather_weights")
    w_in16 = _perm_in(_cols_from_blocks(w_in_g))
    w_in_t16 = w_in16.T
    w_out16 = w_out_g.reshape(D_MODEL, D_MODEL)
    w_out_t16 = w_out16.T
    w_up16 = _cols_from_blocks(w_up_g)
    w_up_t16 = w_up16.T
    w_down16 = w_down_g.reshape(D_FF, D_MODEL)
    w_down_t16 = w_down16.T
    ssm_cw = _cols_from_blocks(scw_g)
    ffn_cw = _cols_from_blocks(fcw_g)
    c_all = c_g.reshape(N_DEV, D_MODEL)

    n_cols = w_ada.shape[2]
    b_cols = lax.dynamic_slice(b_ada, (0, me * n_cols), (1, n_cols))
    mod_part = ada_fwd(c_all, w_ada[0], b_cols)
    (mod_rows,) = all_to_all([mod_part.reshape(N_DEV, 1, n_cols)], "scatter_mod")
    mod = mod_rows.reshape(N_MOD, 1, D_MODEL)
    shift1, scale1, gate1, shift2, scale2, gate2 = (mod[i] for i in range(N_MOD))

    bucket, bucket_band = _bucket_table()
    bias = jnp.transpose(rel_bias[bucket], (2, 0, 1))
    sinks_rows = jnp.broadcast_to(attn_sinks[0][:, None], (N_Q_HEADS, 128))
    dt_bias, a_log, dskip = _lane_pad(ssm_dt_bias), _lane_pad(ssm_a_log), _lane_pad(ssm_d)

    h1, qkv, xbc_raw, z, dt_raw = pre_mix_inproj(xs_, pre_mix_w, scale1, shift1, w_in16)
    attn = attn_fwd(qkv, bias, sinks_rows)
    ssm, hprev_all = ssm_fwd(xbc_raw, z, dt_raw, ssm_cw, ssm_conv_b, dt_bias, a_log, dskip, ssm_norm_w)
    mixed, x1, h2 = mix_out(attn, ssm, xs_, w_out16, gate1, post_mix_w, pre_ffn_w, scale2, shift2)
    u_raw = matmul_nn(h2, w_up16, F32, "up_proj")
    f16, d_ffn, dy, loss_part, d_gate2, d_post_ffn_w = ffn_fwd(
        u_raw, ffn_cw, ffn_conv_b, w_down16, x1, target, gate2, post_ffn_w)

    du_raw, d_ffn_cw, d_ffn_cb = ffn_bwd(u_raw, d_ffn, ffn_cw, ffn_conv_b, w_down_t16)
    g_w_down = matmul_tn(f16, d_ffn, "grad_w_down", FF_CHUNK, D_MODEL)
    dh2 = matmul_nn(du_raw, w_up_t16, F32, "up_proj_bwd")
    g_w_up = matmul_tn(h2, du_raw, "grad_w_up", D_MODEL, FF_CHUNK)
    (dx1, dmixed, d_attn, d_ssm, d_pre_ffn_w, d_scale2, d_shift2, d_gate1, d_post_mix_w) = mix_bwd(
        dh2, x1, dy, mixed, w_out_t16, pre_ffn_w, scale2, shift2, gate1, post_mix_w)
    g_w_out = jnp.concatenate([matmul_tn(attn, dmixed, "grad_w_out_attn", ATTN_WIDTH, D_MODEL),
                               matmul_tn(ssm, dmixed, "grad_w_out_ssm", SSM_WIDTH, D_MODEL)], axis=0)
    dq, dk, dv, dbias, dsinks = attn_bwd(qkv, bias, sinks_rows, d_attn)
    d_rel_bias = rel_bias_grad(dbias, jnp.asarray(bucket_band))[:, :N_Q_HEADS]
    (dxbc, dz, ddt, d_ssm_cw, d_ssm_cb, d_dt_bias, d_a_log, d_dskip, d_norm_w) = ssm_bwd(
        xbc_raw, z, dt_raw, hprev_all, d_ssm, ssm_cw, ssm_conv_b, dt_bias, a_log, dskip, ssm_norm_w)
    dqkv = jnp.concatenate([dq, dk, dv], axis=1)
    grad_x, d_pre_mix_w, d_scale1, d_shift1 = inproj_bwd(
        dqkv, dxbc, dz, ddt, xs_, dx1, w_in_t16, pre_mix_w, scale1, shift1)
    g_w_in = _unperm_in(jnp.concatenate(
        [matmul_tn(h1, dqkv, "grad_w_in_qkv", D_MODEL, QKV_W), matmul_tn(h1, dxbc, "grad_w_in_xbc", D_MODEL, XBC_WIDTH),
         matmul_tn(h1, dz, "grad_w_in_z", D_MODEL, SSM_WIDTH), matmul_tn(h1, ddt, "grad_w_in_dt", D_MODEL, DT_PAD)],
        axis=1))

    d_mod = jnp.concatenate([d_shift1, d_scale1, d_gate1, d_shift2, d_scale2, d_gate2], axis=1)
    (p_w_in, p_w_out, p_w_up, p_w_down, p_scw, p_fcw, d_mod_rows) = all_to_all(
        [_cols_to_blocks(g_w_in), g_w_out.reshape(N_DEV, D_MODEL // N_DEV, D_MODEL), _cols_to_blocks(g_w_up),
         g_w_down.reshape(N_DEV, D_FF // N_DEV, D_MODEL), _cols_to_blocks(d_ssm_cw[:SSM_CONV]),
         _cols_to_blocks(d_ffn_cw[:FFN_CONV]), d_mod.reshape(N_DEV, 1, n_cols)], "scatter_grads")
    g_w_ada = ada_bwd(c_all, d_mod_rows.reshape(N_DEV, n_cols))

    small_g = dict(rel_bias=d_rel_bias, b_ada=d_mod, pre_mix_w=d_pre_mix_w, attn_sinks=dsinks[:, 0],
                   ssm_conv_b=d_ssm_cb, ssm_dt_bias=d_dt_bias[:, :SSM_HEADS], ssm_a_log=d_a_log[:, :SSM_HEADS],
                   ssm_d=d_dskip[:, :SSM_HEADS], ssm_norm_w=d_norm_w, post_mix_w=d_post_mix_w,
                   pre_ffn_w=d_pre_ffn_w, ffn_conv_b=d_ffn_cb, post_ffn_w=d_post_ffn_w)
    (slab_all,) = all_gather([_pack_small(small_g, loss_part)], "gather_small")

    zero_head = jnp.zeros((128,), F32)
    res = {}
    slab = adamw(slab_all, _pack_small(weights, zero_head), _pack_small(mom_m, zero_head),
                 _pack_small(mom_v, zero_head), "adamw_small")
    loss, res_g = _unpack_small(slab[0])
    unpacked = [res_g] + [_unpack_small(s_)[1] for s_ in slab[1:]]
    for k, _ in _SMALL:
        res[k] = tuple(u[k] for u in unpacked)
    for k, parts in (("w_in", p_w_in), ("w_out", p_w_out), ("w_up", p_w_up), ("w_down", p_w_down),
                     ("ssm_conv_w", p_scw), ("ffn_conv_w", p_fcw), ("w_ada", g_w_ada[None])):
        res[k] = tuple(o[None] for o in adamw(parts, weights[k][0], mom_m[k][0], mom_v[k][0], "adamw_" + k))

    outs = [loss, grad_x[None]]
    for field in range(4):
        outs += [res[k][field] for k in order]
    return tuple(outs)
```

```python
import functools
import math

import numpy as np
import jax
import jax.numpy as jnp
from jax import lax
from jax.experimental import pallas as pl
from jax.experimental.pallas import tpu as pltpu

F32 = jnp.float32
BF16 = jnp.bfloat16
HIGHEST = lax.Precision.HIGHEST
MESH_ID = pl.DeviceIdType.MESH

N_DEV = 8
D_MODEL = 1024
N_Q_HEADS = 8
N_KV_HEADS = 2
HEAD_DIM = 64
ATTN_WIDTH = 512
KV_WIDTH = 128
ATTN_BLOCK = 128
N_BUCKETS = 32
REL_MAX_DIST = 128
SSM_HEADS = 8
SSM_HEAD_DIM = 64
SSM_WIDTH = 512
SSM_STATE = 128
SSM_GROUPS = 2
SSM_BC = 256
SSM_CONV = 4
SSM_CHUNK = 256
XBC_WIDTH = SSM_WIDTH + 2 * SSM_BC
D_FF = 2816
FFN_CONV = 3
NORM_EPS = 1e-6
N_MOD = 6
IN_PROJ_WIDTH = 2312
QKV_W = ATTN_WIDTH + 2 * KV_WIDTH
OFF_XBC = QKV_W
OFF_Z = OFF_XBC + XBC_WIDTH
OFF_DT = OFF_Z + SSM_WIDTH
DT_PAD = 128
PROJ_PAD = OFF_DT + DT_PAD
FF_CHUNK = 1408

ADAM_LR = 0.001
ADAM_B1 = 0.9
ADAM_B2 = 0.999
ADAM_EPS = 1e-08
ADAM_WD = 0.01
ADAM_STEP = 10

TOKEN_TILE = 256
HALO = 8
NEXT = 16
VMEM_LIMIT = 56 * 1024 * 1024


def _params(vmem=VMEM_LIMIT, n_axes=1):
    return pltpu.CompilerParams(dimension_semantics=("arbitrary",) * n_axes, vmem_limit_bytes=vmem)


def _b(x):
    return x.astype(BF16)


def _nn(a, b):
    return jnp.dot(_b(a), _b(b), preferred_element_type=F32)


def _nt(a, b):
    return lax.dot_general(_b(a), _b(b), (((1,), (1,)), ((), ())), preferred_element_type=F32)


def _tn(a, b):
    return lax.dot_general(_b(a), _b(b), (((0,), (0,)), ((), ())), preferred_element_type=F32)


@jax.custom_vjp
def mm(a, b):
    return _nn(a, b)


mm.defvjp(lambda a, b: (_nn(a, b), (a, b)),
          lambda r, g: (_nt(g, r[1]).astype(r[0].dtype), _tn(r[0], g).astype(r[1].dtype)))


@jax.custom_vjp
def mm_nt(a, b):
    return _nt(a, b)


mm_nt.defvjp(lambda a, b: (_nt(a, b), (a, b)),
             lambda r, g: (_nn(g, r[1]).astype(r[0].dtype), _tn(g, r[0]).astype(r[1].dtype)))


@jax.custom_vjp
def mm_tn(a, b):
    return _tn(a, b)


mm_tn.defvjp(lambda a, b: (_tn(a, b), (a, b)),
             lambda r, g: (_nt(r[1], g).astype(r[0].dtype), _nn(r[0], g).astype(r[1].dtype)))


def _rms(x, w):
    return x * lax.rsqrt(jnp.mean(x * x, axis=-1, keepdims=True) + NORM_EPS) * w


def _norm_mod(x, w, scale, shift):
    return _rms(x, w) * (1.0 + scale) + shift


def _silu(x):
    return x * jax.nn.sigmoid(x)


def _conv_rows(xin, w, k):
    acc = xin * w[k - 1:k, :]
    for j in range(1, k):
        acc = acc + pltpu.roll(xin, j, axis=0) * w[k - 1 - j:k - j, :]
    return acc


def _conv_rows_t(du, w, k):
    n = du.shape[0]
    acc = du * w[k - 1:k, :]
    for j in range(1, k):
        acc = acc + pltpu.roll(du, n - j, axis=0) * w[k - 1 - j:k - j, :]
    return acc


def _row(i):
    return (i, 0)


def _const(i):
    return (0, 0)


def _vec(n):
    return pl.BlockSpec((1, n), _const)


def _block_index(p):
    return 4 * p[0] + 2 * p[1] + p[2]


def all_gather(arrs, name):
    n = len(arrs)

    def body(*refs):
        ins, outs = refs[:n], refs[n:2 * n]
        send_sems, recv_sems, local_sems = refs[2 * n:]
        x, y, c = lax.axis_index("x"), lax.axis_index("y"), lax.axis_index("c")
        me, sibling = (x, y, c), (x, y, 1 - c)
        chips = [(1 - x, y), (x, 1 - y), (1 - x, 1 - y)]

        def copy(a, k, block, to, src=None):
            dst = outs[a].at[_block_index(block)]
            return pltpu.make_async_remote_copy(
                src_ref=dst if src is None else src, dst_ref=dst,
                send_sem=send_sems.at[a * 7 + k], recv_sem=recv_sems.at[a * 7 + k],
                device_id=to, device_id_type=MESH_ID)

        mine = [pltpu.make_async_copy(ins[a], outs[a].at[_block_index(me)], local_sems.at[a]) for a in range(n)]
        for cp in mine:
            cp.start()
        first = []
        for a in range(n):
            first.append(copy(a, 0, me, sibling, src=ins[a]))
            first += [copy(a, 1 + j, me, (*chip, c), src=ins[a]) for j, chip in enumerate(chips)]
        for cp in first:
            cp.start()
        passed = []
        for j, chip in enumerate(chips):
            for a in range(n):
                copy(a, 1 + j, (*chip, c), me).wait_recv()
                cp = copy(a, 4 + j, (*chip, c), sibling)
                cp.start()
                passed.append(cp)
        for a in range(n):
            copy(a, 0, sibling, me).wait_recv()
            for j, chip in enumerate(chips):
                copy(a, 4 + j, (*chip, 1 - c), me).wait_recv()
        for cp in first + passed:
            cp.wait_send()
        for cp in mine:
            cp.wait()

    any_spec = pl.BlockSpec(memory_space=pl.ANY)
    return pl.pallas_call(
        body, name=name,
        out_shape=[jax.ShapeDtypeStruct((N_DEV,) + a.shape, a.dtype) for a in arrs],
        in_specs=[any_spec] * n, out_specs=[any_spec] * n,
        scratch_shapes=[pltpu.SemaphoreType.DMA((7 * n,)), pltpu.SemaphoreType.DMA((7 * n,)),
                        pltpu.SemaphoreType.DMA((n,))],
    )(*arrs)


ALL_FLIPS = ((0, 0, 1), (0, 1, 0), (0, 1, 1), (1, 0, 0), (1, 0, 1), (1, 1, 0), (1, 1, 1))
CHIP_FLIPS = ((0, 1, 0), (1, 0, 0), (1, 1, 0))


def _chip_index(p):
    return 2 * p[0] + p[1]


def all_to_all(arrs, name, flips=ALL_FLIPS, index=_block_index):
    n = len(arrs)
    nf = len(flips)

    def body(*refs):
        ins, outs = refs[:n], refs[n:2 * n]
        send_sems, recv_sems, local_sems = refs[2 * n:]
        pos = (lax.axis_index("x"), lax.axis_index("y"), lax.axis_index("c"))
        me = index(pos)
        peers = [tuple(1 - p if f else p for p, f in zip(pos, flip)) for flip in flips]

        def copy(a, k):
            peer = peers[k]
            return pltpu.make_async_remote_copy(
                src_ref=ins[a].at[index(peer)], dst_ref=outs[a].at[me],
                send_sem=send_sems.at[a * nf + k], recv_sem=recv_sems.at[a * nf + k],
                device_id=peer, device_id_type=MESH_ID)

        def landed(a, k):
            slot = outs[a].at[index(peers[k])]
            return pltpu.make_async_remote_copy(
                src_ref=slot, dst_ref=slot,
                send_sem=send_sems.at[a * nf + k], recv_sem=recv_sems.at[a * nf + k],
                device_id=peers[k], device_id_type=MESH_ID)

        mine = [pltpu.make_async_copy(ins[a].at[me], outs[a].at[me], local_sems.at[a]) for a in range(n)]
        for cp in mine:
            cp.start()
        sent = [copy(a, k) for a in range(n) for k in range(nf)]
        for cp in sent:
            cp.start()
        for a in range(n):
            for k in range(nf):
                landed(a, k).wait_recv()
        for cp in sent:
            cp.wait_send()
        for cp in mine:
            cp.wait()

    any_spec = pl.BlockSpec(memory_space=pl.ANY)
    return pl.pallas_call(
        body, name=name,
        out_shape=[jax.ShapeDtypeStruct(a.shape, a.dtype) for a in arrs],
        in_specs=[any_spec] * n, out_specs=[any_spec] * n,
        scratch_shapes=[pltpu.SemaphoreType.DMA((nf * n,)), pltpu.SemaphoreType.DMA((nf * n,)),
                        pltpu.SemaphoreType.DMA((n,))],
    )(*arrs)


N_CHIPS = 4


def pair_exchange(arrs, name):
    n = len(arrs)

    def body(*refs):
        ins, outs = refs[:n], refs[n:2 * n]
        send_sems, recv_sems = refs[2 * n:]
        x, y, c = lax.axis_index("x"), lax.axis_index("y"), lax.axis_index("c")
        sibling = (x, y, 1 - c)
        sent = []
        for a in range(n):
            for q in range(N_CHIPS):
                cp = pltpu.make_async_remote_copy(
                    src_ref=ins[a].at[2 * q + (1 - c)], dst_ref=outs[a].at[q],
                    send_sem=send_sems.at[a * N_CHIPS + q], recv_sem=recv_sems.at[a * N_CHIPS + q],
                    device_id=sibling, device_id_type=MESH_ID)
                cp.start()
                sent.append(cp)
        for cp in sent:
            cp.wait_recv()
        for cp in sent:
            cp.wait_send()

    any_spec = pl.BlockSpec(memory_space=pl.ANY)
    return pl.pallas_call(
        body, name=name,
        out_shape=[jax.ShapeDtypeStruct((N_CHIPS,) + a.shape[1:], a.dtype) for a in arrs],
        in_specs=[any_spec] * n, out_specs=[any_spec] * n,
        scratch_shapes=[pltpu.SemaphoreType.DMA((N_CHIPS * n,)), pltpu.SemaphoreType.DMA((N_CHIPS * n,))],
    )(*arrs)


def pair_sum(full, got, core, name):
    _, r, n = full.shape
    tr = _row_tile(r)

    def body(c_ref, mine_ref, got_ref, o_ref):
        o_ref[...] = _b(mine_ref[...] + got_ref[...])

    grid_spec = pltpu.PrefetchScalarGridSpec(
        num_scalar_prefetch=1, grid=(N_CHIPS, r // tr),
        in_specs=[pl.BlockSpec((1, tr, n), lambda q, i, c_ref: (2 * q + c_ref[0], i, 0)),
                  pl.BlockSpec((1, tr, n), lambda q, i, c_ref: (q, i, 0))],
        out_specs=pl.BlockSpec((1, tr, n), lambda q, i, c_ref: (q, i, 0)))
    return pl.pallas_call(body, name=name, grid_spec=grid_spec,
                          out_shape=jax.ShapeDtypeStruct((N_CHIPS, r, n), BF16),
                          compiler_params=_params(n_axes=2))(core, full, got)


def _row_tile(r):
    for cand in (256, 128, 64, 32, 16):
        if r % cand == 0 and r > cand:
            return cand
    return r


def ada_fwd(c_all, w_ada, b_cols):
    def body(c_ref, w_ref, b_ref, o_ref):
        o_ref[...] = _nn(_silu(c_ref[...]), w_ref[...]) + b_ref[...]

    return pl.pallas_call(body, name="ada_fwd",
                          out_shape=jax.ShapeDtypeStruct((N_DEV, w_ada.shape[1]), F32),
                          compiler_params=_params(n_axes=0))(c_all, w_ada, b_cols)


def ada_bwd(c_all, g_cols):
    def body(c_ref, g_ref, o_ref):
        o_ref[...] = _tn(_silu(c_ref[...]), g_ref[...])

    return pl.pallas_call(body, name="ada_bwd",
                          out_shape=jax.ShapeDtypeStruct((c_all.shape[1], g_cols.shape[1]), F32),
                          compiler_params=_params(n_axes=0))(c_all, g_cols)


def matmul_tn(a, b, name, bm, bn, tk=512):
    s, m = a.shape
    n = b.shape[1]
    tk = min(tk, s)

    def body(a_ref, b_ref, o_ref):
        @pl.when(pl.program_id(2) == 0)
        def _():
            o_ref[...] = jnp.zeros_like(o_ref)

        o_ref[...] += _tn(a_ref[...], b_ref[...])

    return pl.pallas_call(
        body, name=name, grid=(m // bm, n // bn, s // tk),
        in_specs=[pl.BlockSpec((tk, bm), lambda i, j, k: (k, i)), pl.BlockSpec((tk, bn), lambda i, j, k: (k, j))],
        out_specs=pl.BlockSpec((bm, bn), lambda i, j, k: (i, j)),
        out_shape=jax.ShapeDtypeStruct((m, n), F32),
        compiler_params=_params(n_axes=3))(a, b)


def pre_mix_inproj(x, w, scale, shift, w_in16, tm=TOKEN_TILE):
    s = x.shape[0]

    def body(x_ref, w_ref, sc_ref, sh_ref, win_ref, h_ref, qkv_ref, xbc_ref, z_ref, dt_ref):
        h16 = _b(_norm_mod(x_ref[...], w_ref[...], sc_ref[...], sh_ref[...]))
        h_ref[...] = h16
        dot = lambda lo, hi: jnp.dot(h16, win_ref[:, lo:hi], preferred_element_type=F32)
        qkv_ref[...] = _b(dot(0, OFF_XBC))
        xbc_ref[...] = dot(OFF_XBC, OFF_Z)
        z_ref[...] = dot(OFF_Z, OFF_DT)
        dt_ref[...] = dot(OFF_DT, PROJ_PAD)

    tile = lambda n: pl.BlockSpec((tm, n), _row)
    return pl.pallas_call(
        body, name="pre_mix_inproj", grid=(s // tm,),
        in_specs=[tile(D_MODEL), _vec(D_MODEL), _vec(D_MODEL), _vec(D_MODEL), pl.BlockSpec((D_MODEL, PROJ_PAD), _const)],
        out_specs=[tile(D_MODEL), tile(QKV_W), tile(XBC_WIDTH), tile(SSM_WIDTH), tile(DT_PAD)],
        out_shape=[jax.ShapeDtypeStruct((s, D_MODEL), BF16), jax.ShapeDtypeStruct((s, QKV_W), BF16),
                   jax.ShapeDtypeStruct((s, XBC_WIDTH), F32), jax.ShapeDtypeStruct((s, SSM_WIDTH), F32),
                   jax.ShapeDtypeStruct((s, DT_PAD), F32)],
        compiler_params=_params())(x, w, scale, shift, w_in16)


def _attn_block(q, kp, kc, vp, vc, bias, sinks, first):
    lq = ATTN_BLOCK
    qi = lax.broadcasted_iota(jnp.int32, (lq, 2 * lq), 0) + lq
    kj = lax.broadcasted_iota(jnp.int32, (lq, 2 * lq), 1)
    dist = qi - kj
    valid = (dist >= 0) & (dist < REL_MAX_DIST) & (kj >= jnp.where(first, lq, 0))
    lane0 = (lax.broadcasted_iota(jnp.int32, (1, 128), 1) == 0).astype(F32)
    group = N_Q_HEADS // N_KV_HEADS
    outs = []
    for hk in range(N_KV_HEADS):
        cols = slice(hk * HEAD_DIM, (hk + 1) * HEAD_DIM)
        kb = jnp.concatenate([kp[:, cols], kc[:, cols]], axis=0)
        vb = jnp.concatenate([vp[:, cols], vc[:, cols]], axis=0)
        for g in range(group):
            h = hk * group + g
            sc = mm_nt(q[:, h * HEAD_DIM:(h + 1) * HEAD_DIM], kb) * (HEAD_DIM ** -0.5) + bias[h]
            sc = jnp.where(valid, sc, -1e30)
            sink = jnp.sum(sinks[h] * lane0, axis=-1, keepdims=True)
            m = lax.stop_gradient(jnp.maximum(jnp.max(sc, axis=-1, keepdims=True), sink))
            p = jnp.exp(sc - m)
            probs = p / (jnp.sum(p, axis=-1, keepdims=True) + jnp.exp(sink - m))
            outs.append(mm(probs, vb))
    return jnp.concatenate(outs, axis=1)


def _attn_in_specs(nb, clamp):
    lq = ATTN_BLOCK
    cur = lambda n: jnp.minimum(n, nb - 1) if clamp else n
    prev = lambda n: jnp.maximum(cur(n) - 1, 0)
    kcol, vcol = ATTN_WIDTH // KV_WIDTH, ATTN_WIDTH // KV_WIDTH + 1
    return [pl.BlockSpec((lq, ATTN_WIDTH), lambda n: (cur(n), 0)),
            pl.BlockSpec((lq, KV_WIDTH), lambda n: (prev(n), kcol)),
            pl.BlockSpec((lq, KV_WIDTH), lambda n: (cur(n), kcol)),
            pl.BlockSpec((lq, KV_WIDTH), lambda n: (prev(n), vcol)),
            pl.BlockSpec((lq, KV_WIDTH), lambda n: (cur(n), vcol)),
            pl.BlockSpec((N_Q_HEADS, lq, 2 * lq), lambda n: (0, 0, 0)),
            pl.BlockSpec((N_Q_HEADS, 128), _const)]


def attn_fwd(qkv, bias, sinks_rows):
    s = qkv.shape[0]
    nb = s // ATTN_BLOCK

    def body(q_ref, kp_ref, kc_ref, vp_ref, vc_ref, bias_ref, sk_ref, o_ref):
        f = lambda r: r[...].astype(F32)
        o = _attn_block(f(q_ref), f(kp_ref), f(kc_ref), f(vp_ref), f(vc_ref),
                        [bias_ref[h] for h in range(N_Q_HEADS)],
                        [sk_ref[h:h + 1, :] for h in range(N_Q_HEADS)], pl.program_id(0) == 0)
        o_ref[...] = _b(o)

    return pl.pallas_call(
        body, name="attn_fwd", grid=(nb,),
        in_specs=_attn_in_specs(nb, False),
        out_specs=pl.BlockSpec((ATTN_BLOCK, ATTN_WIDTH), _row),
        out_shape=jax.ShapeDtypeStruct((s, ATTN_WIDTH), BF16),
        compiler_params=_params())(qkv, qkv, qkv, qkv, qkv, bias, sinks_rows)


def attn_bwd(qkv, bias, sinks_rows, d_attn):
    s = qkv.shape[0]
    lq = ATTN_BLOCK
    nb = s // lq

    def body(q_ref, kp_ref, kc_ref, vp_ref, vc_ref, bias_ref, sk_ref, do_ref,
             dq_ref, dk_ref, dv_ref, dbias_ref, dsk_ref, carry_k, carry_v):
        n = pl.program_id(0)

        @pl.when(n == 0)
        def _():
            dbias_ref[...] = jnp.zeros_like(dbias_ref)
            dsk_ref[...] = jnp.zeros_like(dsk_ref)
            carry_k[...] = jnp.zeros_like(carry_k)
            carry_v[...] = jnp.zeros_like(carry_v)

        @pl.when(n < nb)
        def _():
            f = lambda r: r[...].astype(F32)
            fn = functools.partial(_attn_block, first=(n == 0))
            _, vjp = jax.vjp(fn, f(q_ref), f(kp_ref), f(kc_ref), f(vp_ref), f(vc_ref),
                             [bias_ref[h] for h in range(N_Q_HEADS)],
                             [sk_ref[h:h + 1, :] for h in range(N_Q_HEADS)])
            dq, dkp, dkc, dvp, dvc, dbias, dsk = vjp(f(do_ref))
            dq_ref[...] = _b(dq)
            dk_ref[...] = _b(carry_k[...] + dkp)
            dv_ref[...] = _b(carry_v[...] + dvp)
            carry_k[...] = dkc
            carry_v[...] = dvc
            for h in range(N_Q_HEADS):
                dbias_ref[h] += dbias[h]
                dsk_ref[h:h + 1, :] += dsk[h]

        @pl.when(n == nb)
        def _():
            dk_ref[...] = _b(carry_k[...])
            dv_ref[...] = _b(carry_v[...])

    cur = lambda n: (jnp.minimum(n, nb - 1), 0)
    done = lambda n: (jnp.maximum(n - 1, 0), 0)
    return pl.pallas_call(
        body, name="attn_bwd", grid=(nb + 1,),
        in_specs=_attn_in_specs(nb, True) + [pl.BlockSpec((lq, ATTN_WIDTH), cur)],
        out_specs=[pl.BlockSpec((lq, ATTN_WIDTH), cur), pl.BlockSpec((lq, KV_WIDTH), done),
                   pl.BlockSpec((lq, KV_WIDTH), done),
                   pl.BlockSpec((N_Q_HEADS, lq, 2 * lq), lambda n: (0, 0, 0)), pl.BlockSpec((N_Q_HEADS, 128), _const)],
        out_shape=[jax.ShapeDtypeStruct((s, ATTN_WIDTH), BF16), jax.ShapeDtypeStruct((s, KV_WIDTH), BF16),
                   jax.ShapeDtypeStruct((s, KV_WIDTH), BF16),
                   jax.ShapeDtypeStruct((N_Q_HEADS, lq, 2 * lq), F32), jax.ShapeDtypeStruct((N_Q_HEADS, 128), F32)],
        scratch_shapes=[pltpu.VMEM((lq, KV_WIDTH), F32), pltpu.VMEM((lq, KV_WIDTH), F32)],
        compiler_params=_params())(qkv, qkv, qkv, qkv, qkv, bias, sinks_rows, d_attn)


def rel_bias_table(rel_bias, bucket):
    def body(rb_ref, bk_ref, o_ref):
        bk = bk_ref[...]
        accs = [jnp.zeros(bk.shape, F32) for _ in range(N_Q_HEADS)]
        for b in range(N_BUCKETS):
            hit = bk == b
            accs = [jnp.where(hit, rb_ref[b, h], acc) for h, acc in enumerate(accs)]
        for h in range(N_Q_HEADS):
            o_ref[h] = accs[h]

    return pl.pallas_call(
        body, name="rel_bias_table",
        in_specs=[pl.BlockSpec(memory_space=pltpu.SMEM), pl.BlockSpec(memory_space=pltpu.VMEM)],
        out_shape=jax.ShapeDtypeStruct((N_Q_HEADS,) + bucket.shape, F32),
        compiler_params=_params(n_axes=0))(rel_bias, bucket)


def rel_bias_grad(dbias, bucket):
    def body(db_ref, bk_ref, o_ref):
        rows = lax.broadcasted_iota(jnp.int32, (N_BUCKETS, 128), 0)
        lanes = lax.broadcasted_iota(jnp.int32, (N_BUCKETS, 128), 1)
        bk = bk_ref[...]

        def per_bucket(b, acc):
            hit = (bk == b).astype(F32)
            for h in range(N_Q_HEADS):
                val = jnp.sum(db_ref[h] * hit, keepdims=True)
                acc = acc + jnp.where((rows == b) & (lanes == h), val, 0.0)
            return acc

        o_ref[...] = lax.fori_loop(0, N_BUCKETS, per_bucket, jnp.zeros((N_BUCKETS, 128), F32))

    return pl.pallas_call(body, name="rel_bias_grad", out_shape=jax.ShapeDtypeStruct((N_BUCKETS, 128), F32),
                          compiler_params=_params(n_axes=0))(dbias, bucket)


def _ssm_core(u, z, dt_raw, hprev, dt_bias, a_log, dskip, norm_w):
    lc = u.shape[0]
    xbc = _silu(u)
    xs, bm, cm = xbc[:, :SSM_WIDTH], xbc[:, SSM_WIDTH:SSM_WIDTH + SSM_BC], xbc[:, SSM_WIDTH + SSM_BC:]
    dt = jax.nn.softplus(dt_raw + dt_bias)
    adt = dt * (-jnp.exp(a_log))
    ri = lax.broadcasted_iota(jnp.int32, (lc, lc), 0)
    ci = lax.broadcasted_iota(jnp.int32, (lc, lc), 1)
    causal = ri >= ci
    acum = jnp.dot(causal.astype(F32), adt, precision=HIGHEST)
    acum_t = acum.T
    last = acum[lc - 1:lc, :]
    per_group = SSM_HEADS // SSM_GROUPS
    ys, hs = [], []
    for g in range(SSM_GROUPS):
        bg = bm[:, g * SSM_STATE:(g + 1) * SSM_STATE]
        cg = cm[:, g * SSM_STATE:(g + 1) * SSM_STATE]
        cb = mm_nt(cg, bg)
        for hh in range(per_group):
            h = g * per_group + hh
            xh = xs[:, h * SSM_HEAD_DIM:(h + 1) * SSM_HEAD_DIM]
            xc = xh * dt[:, h:h + 1]
            col, rowv, lasth = acum[:, h:h + 1], acum_t[h:h + 1, :], last[:, h:h + 1]
            decay = jnp.exp(jnp.where(causal, col - rowv, -1e30))
            hp = hprev[h * SSM_HEAD_DIM:(h + 1) * SSM_HEAD_DIM, :]
            y = mm(cb * decay, xc) + mm_nt(cg * jnp.exp(col), hp)
            st = mm_tn(xc, bg * jnp.exp(lasth - col))
            hs.append(jnp.exp(lasth) * hp + st)
            ys.append(y + dskip[:, h:h + 1] * xh)
    y = jnp.concatenate(ys, axis=1) * _silu(z)
    gw = SSM_WIDTH // SSM_GROUPS
    outs = []
    for g in range(SSM_GROUPS):
        yg = y[:, g * gw:(g + 1) * gw]
        outs.append(yg * lax.rsqrt(jnp.mean(yg * yg, axis=-1, keepdims=True) + NORM_EPS))
    return jnp.concatenate(outs, axis=1) * norm_w, jnp.concatenate(hs, axis=0)


def _ssm_param_specs():
    return [pl.BlockSpec((SSM_CONV, XBC_WIDTH), _const), _vec(XBC_WIDTH), _vec(128), _vec(128), _vec(128),
            _vec(SSM_WIDTH)]


def ssm_fwd(xbc_raw, z, dt_raw, conv_w, conv_b, dt_bias, a_log, dskip, norm_w):
    s = xbc_raw.shape[0]
    lc = SSM_CHUNK
    nc = s // lc
    hrows = SSM_HEADS * SSM_HEAD_DIM

    def body(x_ref, halo_ref, z_ref, dt_ref, cw_ref, cb_ref, dtb_ref, al_ref, dk_ref, nw_ref,
             o_ref, hp_ref, state):
        i = pl.program_id(0)

        @pl.when(i == 0)
        def _():
            state[...] = jnp.zeros_like(state)

        halo = halo_ref[...] * (i > 0).astype(F32)
        xin = jnp.concatenate([halo, x_ref[...]], axis=0)
        u = (_conv_rows(xin, cw_ref[...], SSM_CONV) + cb_ref[...])[HALO:]
        hprev = state[...]
        hp_ref[...] = hprev
        out, hnew = _ssm_core(u, z_ref[...], dt_ref[...], hprev, dtb_ref[...], al_ref[...], dk_ref[...], nw_ref[...])
        o_ref[...] = _b(out)
        state[...] = hnew

    tile = lambda n: pl.BlockSpec((lc, n), _row)
    halo_spec = pl.BlockSpec((HALO, XBC_WIDTH), lambda i: (jnp.maximum(i * (lc // HALO) - 1, 0), 0))
    return pl.pallas_call(
        body, name="ssm_fwd", grid=(nc,),
        in_specs=[tile(XBC_WIDTH), halo_spec, tile(SSM_WIDTH), tile(DT_PAD)] + _ssm_param_specs(),
        out_specs=[tile(SSM_WIDTH), pl.BlockSpec((hrows, SSM_STATE), _row)],
        out_shape=[jax.ShapeDtypeStruct((s, SSM_WIDTH), BF16), jax.ShapeDtypeStruct((nc * hrows, SSM_STATE), F32)],
        scratch_shapes=[pltpu.VMEM((hrows, SSM_STATE), F32)],
        compiler_params=_params())(xbc_raw, xbc_raw, z, dt_raw, conv_w, conv_b, dt_bias, a_log, dskip, norm_w)


def ssm_bwd(xbc_raw, z, dt_raw, hprev_all, d_out, conv_w, conv_b, dt_bias, a_log, dskip, norm_w):
    s = xbc_raw.shape[0]
    lc = SSM_CHUNK
    nc = s // lc
    hrows = SSM_HEADS * SSM_HEAD_DIM

    def body(x_ref, halo_ref, z_ref, dt_ref, hp_ref, do_ref, cw_ref, cb_ref, dtb_ref, al_ref, dk_ref, nw_ref,
             dx_ref, dz_ref, ddt_ref, dcw_ref, dcb_ref, ddtb_ref, dal_ref, ddk_ref, dnw_ref, dstate, du_next):
        i = pl.program_id(0)
        chunk = nc - 1 - i

        @pl.when(i == 0)
        def _():
            dstate[...] = jnp.zeros_like(dstate)
            du_next[...] = jnp.zeros_like(du_next)
            for r in (dcw_ref, dcb_ref, ddtb_ref, dal_ref, ddk_ref, dnw_ref):
                r[...] = jnp.zeros_like(r)

        halo = halo_ref[...] * (chunk > 0).astype(F32)
        xin = jnp.concatenate([halo, x_ref[...]], axis=0)
        cw = cw_ref[...]
        u = (_conv_rows(xin, cw, SSM_CONV) + cb_ref[...])[HALO:]
        _, vjp = jax.vjp(_ssm_core, u, z_ref[...], dt_ref[...], hp_ref[...], dtb_ref[...], al_ref[...],
                         dk_ref[...], nw_ref[...])
        du, dz, ddt, dhp, ddtb, dal, ddk, dnw = vjp((do_ref[...], dstate[...]))
        dstate[...] = dhp
        dz_ref[...] = dz
        ddt_ref[...] = ddt
        du_ext = jnp.concatenate([du, du_next[...]], axis=0)
        dx_ref[...] = _conv_rows_t(du_ext, cw, SSM_CONV)[:lc]
        du_next[...] = du[:HALO]
        rows = [jnp.sum(du * pltpu.roll(xin, j, axis=0)[HALO:] if j else du * xin[HALO:], axis=0, keepdims=True)
                for j in range(SSM_CONV)]
        dcw_ref[...] += jnp.concatenate(rows[::-1] + [jnp.zeros((8 - SSM_CONV, XBC_WIDTH), F32)], axis=0)
        dcb_ref[...] += jnp.sum(du, axis=0, keepdims=True)
        ddtb_ref[...] += ddtb
        dal_ref[...] += dal
        ddk_ref[...] += ddk
        dnw_ref[...] += dnw

    rev = lambda i: (nc - 1 - i, 0)
    tile = lambda n: pl.BlockSpec((lc, n), rev)
    halo_spec = pl.BlockSpec((HALO, XBC_WIDTH), lambda i: (jnp.maximum((nc - 1 - i) * (lc // HALO) - 1, 0), 0))
    acc = lambda r, n: pl.BlockSpec((r, n), _const)
    return pl.pallas_call(
        body, name="ssm_bwd", grid=(nc,),
        in_specs=[tile(XBC_WIDTH), halo_spec, tile(SSM_WIDTH), tile(DT_PAD), pl.BlockSpec((hrows, SSM_STATE), rev),
                  tile(SSM_WIDTH)] + _ssm_param_specs(),
        out_specs=[tile(XBC_WIDTH), tile(SSM_WIDTH), tile(DT_PAD), acc(8, XBC_WIDTH), acc(1, XBC_WIDTH),
                   acc(1, 128), acc(1, 128), acc(1, 128), acc(1, SSM_WIDTH)],
        out_shape=[jax.ShapeDtypeStruct((s, XBC_WIDTH), F32), jax.ShapeDtypeStruct((s, SSM_WIDTH), F32),
                   jax.ShapeDtypeStruct((s, DT_PAD), F32), jax.ShapeDtypeStruct((8, XBC_WIDTH), F32),
                   jax.ShapeDtypeStruct((1, XBC_WIDTH), F32), jax.ShapeDtypeStruct((1, 128), F32),
                   jax.ShapeDtypeStruct((1, 128), F32), jax.ShapeDtypeStruct((1, 128), F32),
                   jax.ShapeDtypeStruct((1, SSM_WIDTH), F32)],
        scratch_shapes=[pltpu.VMEM((hrows, SSM_STATE), F32), pltpu.VMEM((HALO, XBC_WIDTH), F32)],
        compiler_params=_params())(xbc_raw, xbc_raw, z, dt_raw, hprev_all, d_out, conv_w, conv_b, dt_bias, a_log,
                                   dskip, norm_w)


def mix_out(attn, ssm, x, w_out16, gate1, post_mix_w, pre_ffn_w, scale2, shift2, tm=TOKEN_TILE):
    s = x.shape[0]

    def body(a_ref, s_ref, x_ref, w_ref, g_ref, pw_ref, fw_ref, sc_ref, sh_ref, mixed_ref, x1_ref, h2_ref):
        mixed = (jnp.dot(a_ref[...], w_ref[:ATTN_WIDTH, :], preferred_element_type=F32)
                 + jnp.dot(s_ref[...], w_ref[ATTN_WIDTH:, :], preferred_element_type=F32))
        mixed_ref[...] = mixed
        x1 = x_ref[...] + g_ref[...] * _rms(mixed, pw_ref[...])
        x1_ref[...] = x1
        h2_ref[...] = _b(_norm_mod(x1, fw_ref[...], sc_ref[...], sh_ref[...]))

    tile = lambda n: pl.BlockSpec((tm, n), _row)
    return pl.pallas_call(
        body, name="mix_out", grid=(s // tm,),
        in_specs=[tile(ATTN_WIDTH), tile(SSM_WIDTH), tile(D_MODEL), pl.BlockSpec((D_MODEL, D_MODEL), _const)]
        + [_vec(D_MODEL)] * 5,
        out_specs=[tile(D_MODEL)] * 3,
        out_shape=[jax.ShapeDtypeStruct((s, D_MODEL), F32), jax.ShapeDtypeStruct((s, D_MODEL), F32),
                   jax.ShapeDtypeStruct((s, D_MODEL), BF16)],
        compiler_params=_params())(attn, ssm, x, w_out16, gate1, post_mix_w, pre_ffn_w, scale2, shift2)


def _gate(ug, uv):
    return jax.nn.gelu(ug, approximate=True) * uv


def _gate_bwd(ug, uv, df):
    k0, k1 = math.sqrt(2.0 / math.pi), 0.044715
    sq = ug * ug
    t = jnp.tanh(k0 * ug * (1.0 + k1 * sq))
    half = 0.5 * (1.0 + t)
    slope = half + (0.5 * k0) * ug * (1.0 - t * t) * (1.0 + (3.0 * k1) * sq)
    return df * uv * slope, df * (ug * half)


def _resident(shape):
    return pl.BlockSpec(shape, _const, pipeline_mode=pl.Buffered(1))


def up_gate(h2, w_up16, conv_w, conv_b, tm=TOKEN_TILE):
    s = h2.shape[0]

    def body(h_ref, halo_ref, w_ref, cw_ref, cb_ref, u_ref, uraw_ref, f_ref):
        halo = halo_ref[...]
        halo = jnp.where(pl.program_id(0) > 0, halo, jnp.zeros_like(halo))
        hin = jnp.concatenate([halo, h_ref[...]], axis=0)
        for lo in range(0, D_FF, FF_CHUNK):
            halves = []
            for base in (lo, D_FF + lo):
                cols = slice(base, base + FF_CHUNK)
                uraw = jnp.dot(hin, w_ref[:, cols], preferred_element_type=F32)
                uraw_ref[:, cols] = _b(uraw[NEXT:])
                u = (_conv_rows(uraw, cw_ref[:, cols], FFN_CONV) + cb_ref[:, cols])[NEXT:]
                u_ref[:, cols] = u
                halves.append(u)
            f_ref[:, lo:lo + FF_CHUNK] = _b(_gate(*halves))

    tile = lambda n: pl.BlockSpec((tm, n), _row)
    halo_spec = pl.BlockSpec((NEXT, D_MODEL), lambda i: (jnp.maximum(i * (tm // NEXT) - 1, 0), 0))
    return pl.pallas_call(
        body, name="up_gate", grid=(s // tm,),
        in_specs=[tile(D_MODEL), halo_spec, _resident((D_MODEL, 2 * D_FF)),
                  pl.BlockSpec((FFN_CONV, 2 * D_FF), _const), _vec(2 * D_FF)],
        out_specs=[tile(2 * D_FF), tile(2 * D_FF), tile(D_FF)],
        out_shape=[jax.ShapeDtypeStruct((s, 2 * D_FF), F32), jax.ShapeDtypeStruct((s, 2 * D_FF), BF16),
                   jax.ShapeDtypeStruct((s, D_FF), BF16)],
        compiler_params=_params())(h2, h2, w_up16, conv_w, conv_b)


def down_loss(f16, w_down16, x1, target, gate2, post_ffn_w, tm=TOKEN_TILE):
    s = x1.shape[0]

    def body(f_ref, wd_ref, x1_ref, t_ref, g_ref, pw_ref, dffn_ref, dy_ref, loss_ref, dg_ref, dpw_ref):
        i = pl.program_id(0)

        @pl.when(i == 0)
        def _():
            loss_ref[...] = jnp.zeros_like(loss_ref)
            dg_ref[...] = jnp.zeros_like(dg_ref)
            dpw_ref[...] = jnp.zeros_like(dpw_ref)

        ffn = jnp.dot(f_ref[...], wd_ref[...], preferred_element_type=F32)
        x1 = x1_ref[...]
        post = lambda ffn_, g_, w_: x1 + g_ * _rms(ffn_, w_)
        x2, vjp = jax.vjp(post, ffn, g_ref[...], pw_ref[...])
        err = x2 - t_ref[...]
        dy = err * (1.0 / D_MODEL)
        dy_ref[...] = dy
        loss_ref[...] += 0.5 * jnp.sum(jnp.mean(err * err, axis=-1, keepdims=True))
        dffn, dg, dpw = vjp(dy)
        dffn_ref[...] = _b(dffn)
        dg_ref[...] += dg
        dpw_ref[...] += dpw

    tile = lambda n: pl.BlockSpec((tm, n), _row)
    return pl.pallas_call(
        body, name="down_loss", grid=(s // tm,),
        in_specs=[tile(D_FF), _resident((D_FF, D_MODEL)), tile(D_MODEL), tile(D_MODEL), _vec(D_MODEL), _vec(D_MODEL)],
        out_specs=[tile(D_MODEL), tile(D_MODEL), _vec(128), _vec(D_MODEL), _vec(D_MODEL)],
        out_shape=[jax.ShapeDtypeStruct((s, D_MODEL), BF16), jax.ShapeDtypeStruct((s, D_MODEL), F32),
                   jax.ShapeDtypeStruct((1, 128), F32), jax.ShapeDtypeStruct((1, D_MODEL), F32),
                   jax.ShapeDtypeStruct((1, D_MODEL), F32)],
        compiler_params=_params())(f16, w_down16, x1, target, gate2, post_ffn_w)


BWD_CHUNK = 256


def ffn_bwd(u, u_raw16, d_ffn, conv_w, w_down_t16, w_up_t16, tm=TOKEN_TILE):
    s = u.shape[0]
    nt = s // tm

    def body(u_ref, unext_ref, uraw_ref, d_ref, dnext_ref, cw_ref, wdt_ref, wut_ref,
             du_ref, dh_ref, dcw_ref, dcb_ref):
        i = pl.program_id(0)

        @pl.when(i == 0)
        def _():
            dcw_ref[...] = jnp.zeros_like(dcw_ref)
            dcb_ref[...] = jnp.zeros_like(dcb_ref)

        dnext = dnext_ref[...]
        dnext = jnp.where(i < nt - 1, dnext, jnp.zeros_like(dnext))
        dff = jnp.concatenate([d_ref[...], dnext], axis=0)
        rows_ext = tm + NEXT
        dh = jnp.zeros((tm, D_MODEL), F32)
        for lo in range(0, D_FF, BWD_CHUNK):
            gcols, vcols = slice(lo, lo + BWD_CHUNK), slice(D_FF + lo, D_FF + lo + BWD_CHUNK)
            ug = jnp.concatenate([u_ref[:, gcols], unext_ref[:, gcols]], axis=0)
            uv = jnp.concatenate([u_ref[:, vcols], unext_ref[:, vcols]], axis=0)
            df = jnp.dot(dff, wdt_ref[:, gcols], preferred_element_type=F32)
            for cols, du in zip((gcols, vcols), _gate_bwd(ug, uv, df)):
                cw = cw_ref[:, cols]
                du1 = pltpu.roll(du, rows_ext - 1, axis=0)
                du2 = pltpu.roll(du, rows_ext - 2, axis=0)
                du_raw16 = _b((du * cw[2:3, :] + du1 * cw[1:2, :] + du2 * cw[0:1, :])[:tm])
                du_ref[:, cols] = du_raw16
                dh = dh + jnp.dot(du_raw16, wut_ref[cols, :], preferred_element_type=F32)
                xr = uraw_ref[:, cols].astype(F32)
                rows = [jnp.sum(xr * d_[:tm], axis=0, keepdims=True) for d_ in (du2, du1, du)]
                dcw_ref[:, cols] += jnp.concatenate(rows + [jnp.zeros((8 - FFN_CONV, BWD_CHUNK), F32)], axis=0)
                dcb_ref[:, cols] += jnp.sum(du[:tm], axis=0, keepdims=True)
        dh_ref[...] = dh

    tile = lambda n: pl.BlockSpec((tm, n), _row)
    nxt = lambda i: (jnp.minimum((i + 1) * (tm // NEXT), s // NEXT - 1), 0)
    return pl.pallas_call(
        body, name="ffn_bwd", grid=(nt,),
        in_specs=[tile(2 * D_FF), pl.BlockSpec((NEXT, 2 * D_FF), nxt), tile(2 * D_FF), tile(D_MODEL),
                  pl.BlockSpec((NEXT, D_MODEL), nxt), pl.BlockSpec((FFN_CONV, 2 * D_FF), _const),
                  _resident((D_MODEL, D_FF)), _resident((2 * D_FF, D_MODEL))],
        out_specs=[tile(2 * D_FF), tile(D_MODEL), pl.BlockSpec((8, 2 * D_FF), _const), _vec(2 * D_FF)],
        out_shape=[jax.ShapeDtypeStruct((s, 2 * D_FF), BF16), jax.ShapeDtypeStruct((s, D_MODEL), F32),
                   jax.ShapeDtypeStruct((8, 2 * D_FF), F32), jax.ShapeDtypeStruct((1, 2 * D_FF), F32)],
        compiler_params=_params())(u, u, u_raw16, d_ffn, d_ffn, conv_w, w_down_t16, w_up_t16)


def mix_bwd(dh2, x1, dy, mixed, w_out_t16, pre_ffn_w, scale2, shift2, gate1, post_mix_w, tm=TOKEN_TILE):
    s = x1.shape[0]

    def body(dh_ref, x1_ref, dy_ref, mx_ref, w_ref, fw_ref, sc_ref, sh_ref, g_ref, pw_ref,
             dx1_ref, dm_ref, da_ref, ds_ref, dfw_ref, dsc_ref, dsh_ref, dg_ref, dpw_ref):
        accs = (dfw_ref, dsc_ref, dsh_ref, dg_ref, dpw_ref)

        @pl.when(pl.program_id(0) == 0)
        def _():
            for r in accs:
                r[...] = jnp.zeros_like(r)

        _, vjp = jax.vjp(_norm_mod, x1_ref[...], fw_ref[...], sc_ref[...], sh_ref[...])
        dx1, dfw, dsc, dsh = vjp(dh_ref[...])
        dx1 = dx1 + dy_ref[...]
        dx1_ref[...] = dx1
        post = lambda m_, g_, w_: g_ * _rms(m_, w_)
        _, vjp2 = jax.vjp(post, mx_ref[...], g_ref[...], pw_ref[...])
        dmixed, dg, dpw = vjp2(dx1)
        dm16 = _b(dmixed)
        dm_ref[...] = dm16
        dmix_in = jnp.dot(dm16, w_ref[...], preferred_element_type=F32)
        da_ref[...] = _b(dmix_in[:, :ATTN_WIDTH])
        ds_ref[...] = dmix_in[:, ATTN_WIDTH:]
        for r, v in zip(accs, (dfw, dsc, dsh, dg, dpw)):
            r[...] += v

    tile = lambda n: pl.BlockSpec((tm, n), _row)
    return pl.pallas_call(
        body, name="mix_bwd", grid=(s // tm,),
        in_specs=[tile(D_MODEL)] * 4 + [pl.BlockSpec((D_MODEL, D_MODEL), _const)] + [_vec(D_MODEL)] * 5,
        out_specs=[tile(D_MODEL), tile(D_MODEL), tile(ATTN_WIDTH), tile(SSM_WIDTH)] + [_vec(D_MODEL)] * 5,
        out_shape=[jax.ShapeDtypeStruct((s, D_MODEL), F32), jax.ShapeDtypeStruct((s, D_MODEL), BF16),
                   jax.ShapeDtypeStruct((s, ATTN_WIDTH), BF16), jax.ShapeDtypeStruct((s, SSM_WIDTH), F32)]
        + [jax.ShapeDtypeStruct((1, D_MODEL), F32)] * 5,
        compiler_params=_params())(dh2, x1, dy, mixed, w_out_t16, pre_ffn_w, scale2, shift2, gate1, post_mix_w)


def inproj_bwd(dqkv, dxbc, dz, ddt, x, dx1, w_in_t16, pre_mix_w, scale1, shift1, tm=TOKEN_TILE):
    s = x.shape[0]

    def body(dq_ref, dxbc_ref, dz_ref, ddt_ref, x_ref, dx1_ref, w_ref, pw_ref, sc_ref, sh_ref,
             gx_ref, dpw_ref, dsc_ref, dsh_ref):
        accs = (dpw_ref, dsc_ref, dsh_ref)

        @pl.when(pl.program_id(0) == 0)
        def _():
            for r in accs:
                r[...] = jnp.zeros_like(r)

        dot = lambda r, lo, hi: jnp.dot(_b(r[...]), w_ref[lo:hi, :], preferred_element_type=F32)
        dh = (dot(dq_ref, 0, OFF_XBC) + dot(dxbc_ref, OFF_XBC, OFF_Z) + dot(dz_ref, OFF_Z, OFF_DT)
              + dot(ddt_ref, OFF_DT, PROJ_PAD))
        _, vjp = jax.vjp(_norm_mod, x_ref[...], pw_ref[...], sc_ref[...], sh_ref[...])
        dx, dpw, dsc, dsh = vjp(dh)
        gx_ref[...] = dx1_ref[...] + dx
        for r, v in zip(accs, (dpw, dsc, dsh)):
            r[...] += v

    tile = lambda n: pl.BlockSpec((tm, n), _row)
    return pl.pallas_call(
        body, name="inproj_bwd", grid=(s // tm,),
        in_specs=[tile(QKV_W), tile(XBC_WIDTH), tile(SSM_WIDTH), tile(DT_PAD), tile(D_MODEL), tile(D_MODEL),
                  pl.BlockSpec((PROJ_PAD, D_MODEL), _const)] + [_vec(D_MODEL)] * 3,
        out_specs=[tile(D_MODEL)] + [_vec(D_MODEL)] * 3,
        out_shape=[jax.ShapeDtypeStruct((s, D_MODEL), F32)] + [jax.ShapeDtypeStruct((1, D_MODEL), F32)] * 3,
        compiler_params=_params())(dqkv, dxbc, dz, ddt, x, dx1, w_in_t16, pre_mix_w, scale1, shift1)


def adamw(parts, w, m, v, name):
    p, r, n = parts.shape
    tr = _row_tile(r)

    def body(p_ref, w_ref, m_ref, v_ref, g_ref, d_ref, nm_ref, nv_ref):
        g = p_ref[0].astype(F32)
        for k in range(1, p):
            g = g + p_ref[k].astype(F32)
        new_m = ADAM_B1 * m_ref[...] + (1.0 - ADAM_B1) * g
        new_v = ADAM_B2 * v_ref[...] + (1.0 - ADAM_B2) * jnp.square(g)
        m_hat = new_m / (1.0 - ADAM_B1 ** ADAM_STEP)
        v_hat = new_v / (1.0 - ADAM_B2 ** ADAM_STEP)
        g_ref[...] = g
        d_ref[...] = -ADAM_LR * (m_hat / (jnp.sqrt(v_hat) + ADAM_EPS) + ADAM_WD * w_ref[...])
        nm_ref[...] = new_m
        nv_ref[...] = new_v

    tile = pl.BlockSpec((tr, n), _row)
    return pl.pallas_call(
        body, name=name, grid=(r // tr,),
        in_specs=[pl.BlockSpec((p, tr, n), lambda i: (0, i, 0)), tile, tile, tile],
        out_specs=[tile] * 4, out_shape=[jax.ShapeDtypeStruct((r, n), F32)] * 4,
        compiler_params=_params())(parts, w, m, v)


def _bucket_table():
    lq = ATTN_BLOCK
    qi = np.arange(lq)[:, None] + lq
    kj = np.arange(2 * lq)[None, :]
    dist = qi - kj
    d = np.maximum(dist, 0)
    max_exact = N_BUCKETS // 2
    nf = np.maximum(d, 1).astype(np.float32)
    large = max_exact + (np.log(nf / max_exact) / math.log(REL_MAX_DIST / max_exact)
                         * (N_BUCKETS - max_exact)).astype(np.int32)
    large = np.minimum(large, N_BUCKETS - 1)
    bucket = np.where(d < max_exact, d, large).astype(np.int32)
    in_band = (dist >= 0) & (dist < REL_MAX_DIST)
    return np.where(in_band, bucket, -1).astype(np.int32)


def _cols_from_blocks(g):
    return jnp.transpose(g, (1, 0, 2)).reshape(g.shape[1], N_DEV * g.shape[2])


def _cols_to_blocks(a):
    r, n = a.shape
    return jnp.transpose(a.reshape(r, N_DEV, n // N_DEV), (1, 0, 2))


def _perm_in(w):
    pad = jnp.zeros((w.shape[0], DT_PAD - SSM_HEADS), w.dtype)
    return jnp.concatenate([w[:, :768], w[:, 768:1280], w[:, 1792:2304], w[:, 1280:1792], w[:, 2304:2312], pad], axis=1)


def _unperm_in(g):
    return jnp.concatenate([g[:, :768], g[:, 768:1280], g[:, 1792:2304], g[:, 1280:1792], g[:, 2304:2312]], axis=1)


def _lane_pad(v, n=128):
    return jnp.pad(v, ((0, 0), (0, n - v.shape[1])))


_SMALL = (("rel_bias", (N_BUCKETS, N_Q_HEADS)), ("b_ada", (1, N_MOD * D_MODEL)), ("pre_mix_w", (1, D_MODEL)),
          ("attn_sinks", (1, N_Q_HEADS)), ("ssm_conv_b", (1, XBC_WIDTH)), ("ssm_dt_bias", (1, SSM_HEADS)),
          ("ssm_a_log", (1, SSM_HEADS)), ("ssm_d", (1, SSM_HEADS)), ("ssm_norm_w", (1, SSM_WIDTH)),
          ("post_mix_w", (1, D_MODEL)), ("pre_ffn_w", (1, D_MODEL)), ("ffn_conv_b", (1, 2 * D_FF)),
          ("post_ffn_w", (1, D_MODEL)))
_SLAB_ROWS = 160


def _pack_small(vals, head):
    flat = jnp.concatenate([head.reshape(-1)] + [vals[k].reshape(-1) for k, _ in _SMALL])
    return jnp.pad(flat, (0, _SLAB_ROWS * 128 - flat.shape[0])).reshape(_SLAB_ROWS, 128)


def _unpack_small(slab):
    flat = slab.reshape(-1)
    out, off = {}, 128
    for k, shp in _SMALL:
        size = shp[0] * shp[1]
        out[k] = flat[off:off + size].reshape(shp)
        off += size
    return flat[0], out


def kernel(x, c, rel_bias, w_ada, b_ada, pre_mix_w, w_in, attn_sinks, ssm_conv_w, ssm_conv_b, ssm_dt_bias, ssm_a_log, ssm_d, ssm_norm_w, w_out, post_mix_w, pre_ffn_w, w_up, ffn_conv_w, ffn_conv_b, w_down, post_ffn_w, loss_target, m_rel_bias, m_w_ada, m_b_ada, m_pre_mix_w, m_w_in, m_attn_sinks, m_ssm_conv_w, m_ssm_conv_b, m_ssm_dt_bias, m_ssm_a_log, m_ssm_d, m_ssm_norm_w, m_w_out, m_post_mix_w, m_pre_ffn_w, m_w_up, m_ffn_conv_w, m_ffn_conv_b, m_w_down, m_post_ffn_w, v_rel_bias, v_w_ada, v_b_ada, v_pre_mix_w, v_w_in, v_attn_sinks, v_ssm_conv_w, v_ssm_conv_b, v_ssm_dt_bias, v_ssm_a_log, v_ssm_d, v_ssm_norm_w, v_w_out, v_post_mix_w, v_pre_ffn_w, v_w_up, v_ffn_conv_w, v_ffn_conv_b, v_w_down, v_post_ffn_w):
    weights = dict(rel_bias=rel_bias, w_ada=w_ada, b_ada=b_ada, pre_mix_w=pre_mix_w, w_in=w_in, attn_sinks=attn_sinks, ssm_conv_w=ssm_conv_w, ssm_conv_b=ssm_conv_b, ssm_dt_bias=ssm_dt_bias, ssm_a_log=ssm_a_log, ssm_d=ssm_d, ssm_norm_w=ssm_norm_w, w_out=w_out, post_mix_w=post_mix_w, pre_ffn_w=pre_ffn_w, w_up=w_up, ffn_conv_w=ffn_conv_w, ffn_conv_b=ffn_conv_b, w_down=w_down, post_ffn_w=post_ffn_w)
    mom_m = dict(rel_bias=m_rel_bias, w_ada=m_w_ada, b_ada=m_b_ada, pre_mix_w=m_pre_mix_w, w_in=m_w_in, attn_sinks=m_attn_sinks, ssm_conv_w=m_ssm_conv_w, ssm_conv_b=m_ssm_conv_b, ssm_dt_bias=m_ssm_dt_bias, ssm_a_log=m_ssm_a_log, ssm_d=m_ssm_d, ssm_norm_w=m_ssm_norm_w, w_out=m_w_out, post_mix_w=m_post_mix_w, pre_ffn_w=m_pre_ffn_w, w_up=m_w_up, ffn_conv_w=m_ffn_conv_w, ffn_conv_b=m_ffn_conv_b, w_down=m_w_down, post_ffn_w=m_post_ffn_w)
    mom_v = dict(rel_bias=v_rel_bias, w_ada=v_w_ada, b_ada=v_b_ada, pre_mix_w=v_pre_mix_w, w_in=v_w_in, attn_sinks=v_attn_sinks, ssm_conv_w=v_ssm_conv_w, ssm_conv_b=v_ssm_conv_b, ssm_dt_bias=v_ssm_dt_bias, ssm_a_log=v_ssm_a_log, ssm_d=v_ssm_d, ssm_norm_w=v_ssm_norm_w, w_out=v_w_out, post_mix_w=v_post_mix_w, pre_ffn_w=v_pre_ffn_w, w_up=v_w_up, ffn_conv_w=v_ffn_conv_w, ffn_conv_b=v_ffn_conv_b, w_down=v_w_down, post_ffn_w=v_post_ffn_w)
    order = ['rel_bias', 'w_ada', 'b_ada', 'pre_mix_w', 'w_in', 'attn_sinks', 'ssm_conv_w', 'ssm_conv_b', 'ssm_dt_bias', 'ssm_a_log', 'ssm_d', 'ssm_norm_w', 'w_out', 'post_mix_w', 'pre_ffn_w', 'w_up', 'ffn_conv_w', 'ffn_conv_b', 'w_down', 'post_ffn_w']

    me = 4 * lax.axis_index("x") + 2 * lax.axis_index("y") + lax.axis_index("c")
    xs_ = x[0]
    target = loss_target[0]

    (w_in_g, w_out_g, w_up_g, w_down_g, scw_g, fcw_g, c_g) = all_gather(
        [_b(w_in[0]), _b(w_out[0]), _b(w_up[0]), _b(w_down[0]), ssm_conv_w[0], ffn_conv_w[0], c], "gather_weights")
    w_in16 = _perm_in(_cols_from_blocks(w_in_g))
    w_in_t16 = w_in16.T
    w_out16 = w_out_g.reshape(D_MODEL, D_MODEL)
    w_out_t16 = w_out16.T
    w_up16 = _cols_from_blocks(w_up_g)
    w_up_t16 = w_up16.T
    w_down16 = w_down_g.reshape(D_FF, D_MODEL)
    w_down_t16 = w_down16.T
    ssm_cw = _cols_from_blocks(scw_g)
    ffn_cw = _cols_from_blocks(fcw_g)
    c_all = c_g.reshape(N_DEV, D_MODEL)

    n_cols = w_ada.shape[2]
    b_cols = lax.dynamic_slice(b_ada, (0, me * n_cols), (1, n_cols))
    mod_part = ada_fwd(c_all, w_ada[0], b_cols)
    (mod_rows,) = all_to_all([mod_part.reshape(N_DEV, 1, n_cols)], "scatter_mod")
    mod = mod_rows.reshape(N_MOD, 1, D_MODEL)
    shift1, scale1, gate1, shift2, scale2, gate2 = (mod[i] for i in range(N_MOD))

    bucket_band = jnp.asarray(_bucket_table())
    bias = rel_bias_table(rel_bias, bucket_band)
    sinks_rows = jnp.broadcast_to(attn_sinks[0][:, None], (N_Q_HEADS, 128))
    dt_bias, a_log, dskip = _lane_pad(ssm_dt_bias), _lane_pad(ssm_a_log), _lane_pad(ssm_d)

    h1, qkv, xbc_raw, z, dt_raw = pre_mix_inproj(xs_, pre_mix_w, scale1, shift1, w_in16)
    attn = attn_fwd(qkv, bias, sinks_rows)
    ssm, hprev_all = ssm_fwd(xbc_raw, z, dt_raw, ssm_cw, ssm_conv_b, dt_bias, a_log, dskip, ssm_norm_w)
    mixed, x1, h2 = mix_out(attn, ssm, xs_, w_out16, gate1, post_mix_w, pre_ffn_w, scale2, shift2)
    u, u_raw16, f16 = up_gate(h2, w_up16, ffn_cw, ffn_conv_b)
    d_ffn, dy, loss_part, d_gate2, d_post_ffn_w = down_loss(f16, w_down16, x1, target, gate2, post_ffn_w)

    du_raw, dh2, d_ffn_cw, d_ffn_cb = ffn_bwd(u, u_raw16, d_ffn, ffn_cw, w_down_t16, w_up_t16)
    g_w_down = matmul_tn(f16, d_ffn, "grad_w_down", FF_CHUNK, D_MODEL)
    g_w_up = matmul_tn(h2, du_raw, "grad_w_up", D_MODEL, FF_CHUNK)
    (dx1, dmixed, d_attn, d_ssm, d_pre_ffn_w, d_scale2, d_shift2, d_gate1, d_post_mix_w) = mix_bwd(
        dh2, x1, dy, mixed, w_out_t16, pre_ffn_w, scale2, shift2, gate1, post_mix_w)
    g_w_out = jnp.concatenate([matmul_tn(attn, dmixed, "grad_w_out_attn", ATTN_WIDTH, D_MODEL),
                               matmul_tn(ssm, dmixed, "grad_w_out_ssm", SSM_WIDTH, D_MODEL)], axis=0)
    dq, dk, dv, dbias, dsinks = attn_bwd(qkv, bias, sinks_rows, d_attn)
    d_rel_bias = rel_bias_grad(dbias, bucket_band)[:, :N_Q_HEADS]
    (dxbc, dz, ddt, d_ssm_cw, d_ssm_cb, d_dt_bias, d_a_log, d_dskip, d_norm_w) = ssm_bwd(
        xbc_raw, z, dt_raw, hprev_all, d_ssm, ssm_cw, ssm_conv_b, dt_bias, a_log, dskip, ssm_norm_w)
    dqkv = jnp.concatenate([dq, dk, dv], axis=1)
    grad_x, d_pre_mix_w, d_scale1, d_shift1 = inproj_bwd(
        dqkv, dxbc, dz, ddt, xs_, dx1, w_in_t16, pre_mix_w, scale1, shift1)
    g_w_in = _unperm_in(jnp.concatenate(
        [matmul_tn(h1, dqkv, "grad_w_in_qkv", D_MODEL, QKV_W), matmul_tn(h1, dxbc, "grad_w_in_xbc", D_MODEL, XBC_WIDTH),
         matmul_tn(h1, dz, "grad_w_in_z", D_MODEL, SSM_WIDTH), matmul_tn(h1, ddt, "grad_w_in_dt", D_MODEL, DT_PAD)],
        axis=1))

    d_mod = jnp.concatenate([d_shift1, d_scale1, d_gate1, d_shift2, d_scale2, d_gate2], axis=1)
    sharded = ("w_in", "w_out", "w_up", "w_down", "ssm_conv_w", "ffn_conv_w")
    full = [_cols_to_blocks(g_w_in), g_w_out.reshape(N_DEV, D_MODEL // N_DEV, D_MODEL), _cols_to_blocks(g_w_up),
            g_w_down.reshape(N_DEV, D_FF // N_DEV, D_MODEL), _cols_to_blocks(d_ssm_cw[:SSM_CONV]),
            _cols_to_blocks(d_ffn_cw[:FFN_CONV])]
    core = lax.axis_index("c").astype(jnp.int32).reshape(1)
    got = pair_exchange(full, "pair_grads")
    chip_sums = [pair_sum(f_, g_, core, "pair_sum_" + k) for k, f_, g_ in zip(sharded, full, got)]
    chip_parts = all_to_all(chip_sums, "scatter_grads", CHIP_FLIPS, _chip_index)
    (d_mod_rows,) = all_to_all([d_mod.reshape(N_DEV, 1, n_cols)], "scatter_dmod")
    g_w_ada = ada_bwd(c_all, d_mod_rows.reshape(N_DEV, n_cols))

    small_g = dict(rel_bias=d_rel_bias, b_ada=d_mod, pre_mix_w=d_pre_mix_w, attn_sinks=dsinks[:, 0],
                   ssm_conv_b=d_ssm_cb, ssm_dt_bias=d_dt_bias[:, :SSM_HEADS], ssm_a_log=d_a_log[:, :SSM_HEADS],
                   ssm_d=d_dskip[:, :SSM_HEADS], ssm_norm_w=d_norm_w, post_mix_w=d_post_mix_w,
                   pre_ffn_w=d_pre_ffn_w, ffn_conv_b=d_ffn_cb, post_ffn_w=d_post_ffn_w)
    (slab_all,) = all_gather([_pack_small(small_g, loss_part)], "gather_small")

    zero_head = jnp.zeros((128,), F32)
    res = {}
    slab = adamw(slab_all, _pack_small(weights, zero_head), _pack_small(mom_m, zero_head),
                 _pack_small(mom_v, zero_head), "adamw_small")
    loss, res_g = _unpack_small(slab[0])
    unpacked = [res_g] + [_unpack_small(s_)[1] for s_ in slab[1:]]
    for k, _ in _SMALL:
        res[k] = tuple(u[k] for u in unpacked)
    for k, parts in list(zip(sharded, chip_parts)) + [("w_ada", g_w_ada[None])]:
        res[k] = tuple(o[None] for o in adamw(parts, weights[k][0], mom_m[k][0], mom_v[k][0], "adamw_" + k))

    outs = [loss, grad_x[None]]
    for field in range(4):
        outs += [res[k][field] for k in order]
    return tuple(outs)
```

```python
import functools
import math

import numpy as np
import jax
import jax.numpy as jnp
from jax import lax
from jax.experimental import pallas as pl
from jax.experimental.pallas import tpu as pltpu

F32 = jnp.float32
BF16 = jnp.bfloat16
HIGHEST = lax.Precision.HIGHEST
MESH_ID = pl.DeviceIdType.MESH

N_DEV = 8
D_MODEL = 1024
N_Q_HEADS = 8
N_KV_HEADS = 2
HEAD_DIM = 64
ATTN_WIDTH = 512
KV_WIDTH = 128
ATTN_BLOCK = 128
N_BUCKETS = 32
REL_MAX_DIST = 128
SSM_HEADS = 8
SSM_HEAD_DIM = 64
SSM_WIDTH = 512
SSM_STATE = 128
SSM_GROUPS = 2
SSM_BC = 256
SSM_CONV = 4
SSM_CHUNK = 256
XBC_WIDTH = SSM_WIDTH + 2 * SSM_BC
D_FF = 2816
FFN_CONV = 3
NORM_EPS = 1e-6
N_MOD = 6
IN_PROJ_WIDTH = 2312
QKV_W = ATTN_WIDTH + 2 * KV_WIDTH
OFF_XBC = QKV_W
OFF_Z = OFF_XBC + XBC_WIDTH
OFF_DT = OFF_Z + SSM_WIDTH
DT_PAD = 128
PROJ_PAD = OFF_DT + DT_PAD
FF_CHUNK = 1408

ADAM_LR = 0.001
ADAM_B1 = 0.9
ADAM_B2 = 0.999
ADAM_EPS = 1e-08
ADAM_WD = 0.01
ADAM_STEP = 10

TOKEN_TILE = 256
HALO = 8
NEXT = 16
VMEM_LIMIT = 56 * 1024 * 1024


def _params(vmem=VMEM_LIMIT, n_axes=1):
    return pltpu.CompilerParams(dimension_semantics=("arbitrary",) * n_axes, vmem_limit_bytes=vmem)


def _b(x):
    return x.astype(BF16)


def _nn(a, b):
    return jnp.dot(_b(a), _b(b), preferred_element_type=F32)


def _nt(a, b):
    return lax.dot_general(_b(a), _b(b), (((1,), (1,)), ((), ())), preferred_element_type=F32)


def _tn(a, b):
    return lax.dot_general(_b(a), _b(b), (((0,), (0,)), ((), ())), preferred_element_type=F32)


@jax.custom_vjp
def mm(a, b):
    return _nn(a, b)


mm.defvjp(lambda a, b: (_nn(a, b), (a, b)),
          lambda r, g: (_nt(g, r[1]).astype(r[0].dtype), _tn(r[0], g).astype(r[1].dtype)))


@jax.custom_vjp
def mm_nt(a, b):
    return _nt(a, b)


mm_nt.defvjp(lambda a, b: (_nt(a, b), (a, b)),
             lambda r, g: (_nn(g, r[1]).astype(r[0].dtype), _tn(g, r[0]).astype(r[1].dtype)))


@jax.custom_vjp
def mm_tn(a, b):
    return _tn(a, b)


mm_tn.defvjp(lambda a, b: (_tn(a, b), (a, b)),
             lambda r, g: (_nt(r[1], g).astype(r[0].dtype), _nn(r[0], g).astype(r[1].dtype)))


def _rms(x, w):
    return x * lax.rsqrt(jnp.mean(x * x, axis=-1, keepdims=True) + NORM_EPS) * w


def _norm_mod(x, w, scale, shift):
    return _rms(x, w) * (1.0 + scale) + shift


def _silu(x):
    return x * jax.nn.sigmoid(x)


def _conv_rows(xin, w, k):
    acc = xin * w[k - 1:k, :]
    for j in range(1, k):
        acc = acc + pltpu.roll(xin, j, axis=0) * w[k - 1 - j:k - j, :]
    return acc


def _conv_rows_t(du, w, k):
    n = du.shape[0]
    acc = du * w[k - 1:k, :]
    for j in range(1, k):
        acc = acc + pltpu.roll(du, n - j, axis=0) * w[k - 1 - j:k - j, :]
    return acc


def _row(i):
    return (i, 0)


def _const(i):
    return (0, 0)


def _vec(n):
    return pl.BlockSpec((1, n), _const)


def _block_index(p):
    return 4 * p[0] + 2 * p[1] + p[2]


def all_gather(arrs, name):
    n = len(arrs)

    def body(*refs):
        ins, outs = refs[:n], refs[n:2 * n]
        send_sems, recv_sems, local_sems = refs[2 * n:]
        x, y, c = lax.axis_index("x"), lax.axis_index("y"), lax.axis_index("c")
        me, sibling = (x, y, c), (x, y, 1 - c)
        chips = [(1 - x, y), (x, 1 - y), (1 - x, 1 - y)]

        def copy(a, k, block, to, src=None):
            dst = outs[a].at[_block_index(block)]
            return pltpu.make_async_remote_copy(
                src_ref=dst if src is None else src, dst_ref=dst,
                send_sem=send_sems.at[a * 7 + k], recv_sem=recv_sems.at[a * 7 + k],
                device_id=to, device_id_type=MESH_ID)

        mine = [pltpu.make_async_copy(ins[a], outs[a].at[_block_index(me)], local_sems.at[a]) for a in range(n)]
        for cp in mine:
            cp.start()
        first = []
        for a in range(n):
            first.append(copy(a, 0, me, sibling, src=ins[a]))
            first += [copy(a, 1 + j, me, (*chip, c), src=ins[a]) for j, chip in enumerate(chips)]
        for cp in first:
            cp.start()
        passed = []
        for j, chip in enumerate(chips):
            for a in range(n):
                copy(a, 1 + j, (*chip, c), me).wait_recv()
                cp = copy(a, 4 + j, (*chip, c), sibling)
                cp.start()
                passed.append(cp)
        for a in range(n):
            copy(a, 0, sibling, me).wait_recv()
            for j, chip in enumerate(chips):
                copy(a, 4 + j, (*chip, 1 - c), me).wait_recv()
        for cp in first + passed:
            cp.wait_send()
        for cp in mine:
            cp.wait()

    any_spec = pl.BlockSpec(memory_space=pl.ANY)
    return pl.pallas_call(
        body, name=name,
        out_shape=[jax.ShapeDtypeStruct((N_DEV,) + a.shape, a.dtype) for a in arrs],
        in_specs=[any_spec] * n, out_specs=[any_spec] * n,
        scratch_shapes=[pltpu.SemaphoreType.DMA((7 * n,)), pltpu.SemaphoreType.DMA((7 * n,)),
                        pltpu.SemaphoreType.DMA((n,))],
    )(*arrs)


ALL_FLIPS = ((0, 0, 1), (0, 1, 0), (0, 1, 1), (1, 0, 0), (1, 0, 1), (1, 1, 0), (1, 1, 1))
CHIP_FLIPS = ((0, 1, 0), (1, 0, 0), (1, 1, 0))


def _chip_index(p):
    return 2 * p[0] + p[1]


def all_to_all(arrs, name, flips=ALL_FLIPS, index=_block_index):
    n = len(arrs)
    nf = len(flips)

    def body(*refs):
        ins, outs = refs[:n], refs[n:2 * n]
        send_sems, recv_sems, local_sems = refs[2 * n:]
        pos = (lax.axis_index("x"), lax.axis_index("y"), lax.axis_index("c"))
        me = index(pos)
        peers = [tuple(1 - p if f else p for p, f in zip(pos, flip)) for flip in flips]

        def copy(a, k):
            peer = peers[k]
            return pltpu.make_async_remote_copy(
                src_ref=ins[a].at[index(peer)], dst_ref=outs[a].at[me],
                send_sem=send_sems.at[a * nf + k], recv_sem=recv_sems.at[a * nf + k],
                device_id=peer, device_id_type=MESH_ID)

        def landed(a, k):
            slot = outs[a].at[index(peers[k])]
            return pltpu.make_async_remote_copy(
                src_ref=slot, dst_ref=slot,
                send_sem=send_sems.at[a * nf + k], recv_sem=recv_sems.at[a * nf + k],
                device_id=peers[k], device_id_type=MESH_ID)

        mine = [pltpu.make_async_copy(ins[a].at[me], outs[a].at[me], local_sems.at[a]) for a in range(n)]
        for cp in mine:
            cp.start()
        sent = [copy(a, k) for a in range(n) for k in range(nf)]
        for cp in sent:
            cp.start()
        for a in range(n):
            for k in range(nf):
                landed(a, k).wait_recv()
        for cp in sent:
            cp.wait_send()
        for cp in mine:
            cp.wait()

    any_spec = pl.BlockSpec(memory_space=pl.ANY)
    return pl.pallas_call(
        body, name=name,
        out_shape=[jax.ShapeDtypeStruct(a.shape, a.dtype) for a in arrs],
        in_specs=[any_spec] * n, out_specs=[any_spec] * n,
        scratch_shapes=[pltpu.SemaphoreType.DMA((nf * n,)), pltpu.SemaphoreType.DMA((nf * n,)),
                        pltpu.SemaphoreType.DMA((n,))],
    )(*arrs)


def _direct_exchange(src, dst, sems, scatter):
    send_sems, recv_sems, local_sem = sems
    pos = (lax.axis_index("x"), lax.axis_index("y"), lax.axis_index("c"))
    me = _block_index(pos)
    peers = [tuple(1 - p if f else p for p, f in zip(pos, flip)) for flip in ALL_FLIPS]

    def outgoing(k):
        return pltpu.make_async_remote_copy(
            src_ref=src.at[_block_index(peers[k])] if scatter else src, dst_ref=dst.at[me],
            send_sem=send_sems.at[k], recv_sem=recv_sems.at[k], device_id=peers[k], device_id_type=MESH_ID)

    def incoming(k):
        slot = dst.at[_block_index(peers[k])]
        return pltpu.make_async_remote_copy(
            src_ref=slot, dst_ref=slot, send_sem=send_sems.at[k], recv_sem=recv_sems.at[k],
            device_id=peers[k], device_id_type=MESH_ID)

    def local():
        return pltpu.make_async_copy(src.at[me] if scatter else src, dst.at[me], local_sem)

    def start():
        local().start()
        for k in range(len(ALL_FLIPS)):
            outgoing(k).start()

    def finish():
        for k in range(len(ALL_FLIPS)):
            incoming(k).wait_recv()
        for k in range(len(ALL_FLIPS)):
            outgoing(k).wait_send()
        local().wait()

    return start, finish


def hosted_call(body, exchanges, steps, n_in, n_out, **call):
    n_ex = len(exchanges)

    def wrapped(*refs):
        ins, srcs = refs[:n_in], refs[n_in:n_in + n_ex]
        outs = refs[n_in + n_ex:n_in + n_ex + n_out]
        dsts = refs[n_in + n_ex + n_out:n_in + 2 * n_ex + n_out]
        rest = refs[n_in + 2 * n_ex + n_out:]
        scratch, sems = rest[:len(rest) - 3 * n_ex], rest[len(rest) - 3 * n_ex:]
        plans = [_direct_exchange(srcs[e], dsts[e], sems[3 * e:3 * e + 3], exchanges[e][1]) for e in range(n_ex)]

        @pl.when(pl.program_id(0) == 0)
        def _():
            for start, _ in plans:
                start()

        body(*ins, *outs, *scratch)

        @pl.when(pl.program_id(0) == steps - 1)
        def _():
            for _, finish in plans:
                finish()

    any_spec = pl.BlockSpec(memory_space=pl.ANY)
    landings = [jax.ShapeDtypeStruct(src.shape if scatter else (N_DEV,) + src.shape, src.dtype)
                for src, scatter in exchanges]
    n_flips = len(ALL_FLIPS)
    sems = [pltpu.SemaphoreType.DMA((n_flips,)), pltpu.SemaphoreType.DMA((n_flips,)), pltpu.SemaphoreType.DMA(())]
    return pl.pallas_call(
        wrapped, grid=(steps,),
        in_specs=list(call.pop("in_specs")) + [any_spec] * n_ex,
        out_specs=list(call.pop("out_specs")) + [any_spec] * n_ex,
        out_shape=list(call.pop("out_shape")) + landings,
        scratch_shapes=list(call.pop("scratch_shapes", [])) + sems * n_ex,
        **call)


def _grid_call(body, steps, args, exchanges, **call):
    if not exchanges:
        return pl.pallas_call(body, grid=(steps,), **call)(*args)
    srcs = [src for src, _ in exchanges]
    return hosted_call(body, exchanges, steps, len(args), len(call["out_shape"]), **call)(*args, *srcs)


N_CHIPS = 4


def pair_exchange(arrs, name):
    n = len(arrs)

    def body(*refs):
        ins, outs = refs[:n], refs[n:2 * n]
        send_sems, recv_sems = refs[2 * n:]
        x, y, c = lax.axis_index("x"), lax.axis_index("y"), lax.axis_index("c")
        sibling = (x, y, 1 - c)
        sent = []
        for a in range(n):
            for q in range(N_CHIPS):
                cp = pltpu.make_async_remote_copy(
                    src_ref=ins[a].at[2 * q + (1 - c)], dst_ref=outs[a].at[q],
                    send_sem=send_sems.at[a * N_CHIPS + q], recv_sem=recv_sems.at[a * N_CHIPS + q],
                    device_id=sibling, device_id_type=MESH_ID)
                cp.start()
                sent.append(cp)
        for cp in sent:
            cp.wait_recv()
        for cp in sent:
            cp.wait_send()

    any_spec = pl.BlockSpec(memory_space=pl.ANY)
    return pl.pallas_call(
        body, name=name,
        out_shape=[jax.ShapeDtypeStruct((N_CHIPS,) + a.shape[1:], a.dtype) for a in arrs],
        in_specs=[any_spec] * n, out_specs=[any_spec] * n,
        scratch_shapes=[pltpu.SemaphoreType.DMA((N_CHIPS * n,)), pltpu.SemaphoreType.DMA((N_CHIPS * n,))],
    )(*arrs)


def pair_sum(full, got, core, name):
    _, r, n = full.shape
    tr = _row_tile(r)

    def body(c_ref, mine_ref, got_ref, o_ref):
        o_ref[...] = _b(mine_ref[...] + got_ref[...])

    grid_spec = pltpu.PrefetchScalarGridSpec(
        num_scalar_prefetch=1, grid=(N_CHIPS, r // tr),
        in_specs=[pl.BlockSpec((1, tr, n), lambda q, i, c_ref: (2 * q + c_ref[0], i, 0)),
                  pl.BlockSpec((1, tr, n), lambda q, i, c_ref: (q, i, 0))],
        out_specs=pl.BlockSpec((1, tr, n), lambda q, i, c_ref: (q, i, 0)))
    return pl.pallas_call(body, name=name, grid_spec=grid_spec,
                          out_shape=jax.ShapeDtypeStruct((N_CHIPS, r, n), BF16),
                          compiler_params=_params(n_axes=2))(core, full, got)


def _row_tile(r):
    for cand in (256, 128, 64, 32, 16):
        if r % cand == 0 and r > cand:
            return cand
    return r


def ada_fwd(c_all, w_ada, b_cols):
    def body(c_ref, w_ref, b_ref, o_ref):
        o_ref[...] = _nn(_silu(c_ref[...]), w_ref[...]) + b_ref[...]

    return pl.pallas_call(body, name="ada_fwd",
                          out_shape=jax.ShapeDtypeStruct((N_DEV, w_ada.shape[1]), F32),
                          compiler_params=_params(n_axes=0))(c_all, w_ada, b_cols)


def ada_bwd(c_all, g_cols):
    def body(c_ref, g_ref, o_ref):
        o_ref[...] = _tn(_silu(c_ref[...]), g_ref[...])

    return pl.pallas_call(body, name="ada_bwd",
                          out_shape=jax.ShapeDtypeStruct((c_all.shape[1], g_cols.shape[1]), F32),
                          compiler_params=_params(n_axes=0))(c_all, g_cols)


def matmul_tn(a, b, name, bm, bn, tk=512):
    s, m = a.shape
    n = b.shape[1]
    tk = min(tk, s)

    def body(a_ref, b_ref, o_ref):
        @pl.when(pl.program_id(2) == 0)
        def _():
            o_ref[...] = jnp.zeros_like(o_ref)

        o_ref[...] += _tn(a_ref[...], b_ref[...])

    return pl.pallas_call(
        body, name=name, grid=(m // bm, n // bn, s // tk),
        in_specs=[pl.BlockSpec((tk, bm), lambda i, j, k: (k, i)), pl.BlockSpec((tk, bn), lambda i, j, k: (k, j))],
        out_specs=pl.BlockSpec((bm, bn), lambda i, j, k: (i, j)),
        out_shape=jax.ShapeDtypeStruct((m, n), F32),
        compiler_params=_params(n_axes=3))(a, b)


def pre_mix_inproj(x, w, scale, shift, w_in16, exchange=None, tm=TOKEN_TILE):
    s = x.shape[0]

    def body(x_ref, w_ref, sc_ref, sh_ref, win_ref, h_ref, qkv_ref, xbc_ref, z_ref, dt_ref):
        h16 = _b(_norm_mod(x_ref[...], w_ref[...], sc_ref[...], sh_ref[...]))
        h_ref[...] = h16
        dot = lambda lo, hi: jnp.dot(h16, win_ref[:, lo:hi], preferred_element_type=F32)
        qkv_ref[...] = _b(dot(0, OFF_XBC))
        xbc_ref[...] = dot(OFF_XBC, OFF_Z)
        z_ref[...] = dot(OFF_Z, OFF_DT)
        dt_ref[...] = dot(OFF_DT, PROJ_PAD)

    tile = lambda n: pl.BlockSpec((tm, n), _row)
    return _grid_call(
        body, s // tm, (x, w, scale, shift, w_in16), exchange, name="pre_mix_inproj",
        in_specs=[tile(D_MODEL), _vec(D_MODEL), _vec(D_MODEL), _vec(D_MODEL), pl.BlockSpec((D_MODEL, PROJ_PAD), _const)],
        out_specs=[tile(D_MODEL), tile(QKV_W), tile(XBC_WIDTH), tile(SSM_WIDTH), tile(DT_PAD)],
        out_shape=[jax.ShapeDtypeStruct((s, D_MODEL), BF16), jax.ShapeDtypeStruct((s, QKV_W), BF16),
                   jax.ShapeDtypeStruct((s, XBC_WIDTH), F32), jax.ShapeDtypeStruct((s, SSM_WIDTH), F32),
                   jax.ShapeDtypeStruct((s, DT_PAD), F32)],
        compiler_params=_params())


def _attn_block(q, kp, kc, vp, vc, bias, sinks, first):
    lq = ATTN_BLOCK
    qi = lax.broadcasted_iota(jnp.int32, (lq, 2 * lq), 0) + lq
    kj = lax.broadcasted_iota(jnp.int32, (lq, 2 * lq), 1)
    dist = qi - kj
    valid = (dist >= 0) & (dist < REL_MAX_DIST) & (kj >= jnp.where(first, lq, 0))
    lanes = lax.broadcasted_iota(jnp.int32, (1, 128), 1)
    group = N_Q_HEADS // N_KV_HEADS
    outs = []
    for hk in range(N_KV_HEADS):
        cols = slice(hk * HEAD_DIM, (hk + 1) * HEAD_DIM)
        kb = jnp.concatenate([kp[:, cols], kc[:, cols]], axis=0)
        vb = jnp.concatenate([vp[:, cols], vc[:, cols]], axis=0)
        for g in range(group):
            h = hk * group + g
            sc = mm_nt(q[:, h * HEAD_DIM:(h + 1) * HEAD_DIM], kb) * (HEAD_DIM ** -0.5) + bias[h]
            sc = jnp.where(valid, sc, -1e30)
            sink = jnp.sum(jnp.where(lanes == h, sinks, 0.0), axis=-1, keepdims=True)
            m = lax.stop_gradient(jnp.maximum(jnp.max(sc, axis=-1, keepdims=True), sink))
            p = jnp.exp(sc - m)
            probs = p / (jnp.sum(p, axis=-1, keepdims=True) + jnp.exp(sink - m))
            outs.append(mm(probs, vb))
    return jnp.concatenate(outs, axis=1)


def _attn_in_specs(nb, clamp):
    lq = ATTN_BLOCK
    cur = lambda n: jnp.minimum(n, nb - 1) if clamp else n
    prev = lambda n: jnp.maximum(cur(n) - 1, 0)
    kcol, vcol = ATTN_WIDTH // KV_WIDTH, ATTN_WIDTH // KV_WIDTH + 1
    return [pl.BlockSpec((lq, ATTN_WIDTH), lambda n: (cur(n), 0)),
            pl.BlockSpec((lq, KV_WIDTH), lambda n: (prev(n), kcol)),
            pl.BlockSpec((lq, KV_WIDTH), lambda n: (cur(n), kcol)),
            pl.BlockSpec((lq, KV_WIDTH), lambda n: (prev(n), vcol)),
            pl.BlockSpec((lq, KV_WIDTH), lambda n: (cur(n), vcol)),
            pl.BlockSpec((N_Q_HEADS, lq, 2 * lq), lambda n: (0, 0, 0)),
            _vec(128)]


def attn_fwd(qkv, bias, sinks_rows, exchange=None):
    s = qkv.shape[0]
    nb = s // ATTN_BLOCK

    def body(q_ref, kp_ref, kc_ref, vp_ref, vc_ref, bias_ref, sk_ref, o_ref):
        f = lambda r: r[...].astype(F32)
        o = _attn_block(f(q_ref), f(kp_ref), f(kc_ref), f(vp_ref), f(vc_ref),
                        [bias_ref[h] for h in range(N_Q_HEADS)],
                        sk_ref[...], pl.program_id(0) == 0)
        o_ref[...] = _b(o)

    return _grid_call(
        body, nb, (qkv, qkv, qkv, qkv, qkv, bias, sinks_rows), exchange, name="attn_fwd",
        in_specs=_attn_in_specs(nb, False),
        out_specs=[pl.BlockSpec((ATTN_BLOCK, ATTN_WIDTH), _row)],
        out_shape=[jax.ShapeDtypeStruct((s, ATTN_WIDTH), BF16)],
        compiler_params=_params())


def attn_bwd(qkv, bias, sinks_rows, d_attn, exchange=None):
    s = qkv.shape[0]
    lq = ATTN_BLOCK
    nb = s // lq

    def body(q_ref, kp_ref, kc_ref, vp_ref, vc_ref, bias_ref, sk_ref, do_ref,
             dq_ref, dk_ref, dv_ref, dbias_ref, dsk_ref, carry_k, carry_v):
        n = pl.program_id(0)

        @pl.when(n == 0)
        def _():
            dbias_ref[...] = jnp.zeros_like(dbias_ref)
            dsk_ref[...] = jnp.zeros_like(dsk_ref)
            carry_k[...] = jnp.zeros_like(carry_k)
            carry_v[...] = jnp.zeros_like(carry_v)

        @pl.when(n < nb)
        def _():
            f = lambda r: r[...].astype(F32)
            fn = functools.partial(_attn_block, first=(n == 0))
            _, vjp = jax.vjp(fn, f(q_ref), f(kp_ref), f(kc_ref), f(vp_ref), f(vc_ref),
                             [bias_ref[h] for h in range(N_Q_HEADS)],
                             sk_ref[...])
            dq, dkp, dkc, dvp, dvc, dbias, dsk = vjp(f(do_ref))
            dq_ref[...] = _b(dq)
            dk_ref[...] = _b(carry_k[...] + dkp)
            dv_ref[...] = _b(carry_v[...] + dvp)
            carry_k[...] = dkc
            carry_v[...] = dvc
            dsk_ref[...] += dsk
            for h in range(N_Q_HEADS):
                dbias_ref[h] += dbias[h]

        @pl.when(n == nb)
        def _():
            dk_ref[...] = _b(carry_k[...])
            dv_ref[...] = _b(carry_v[...])

    cur = lambda n: (jnp.minimum(n, nb - 1), 0)
    done = lambda n: (jnp.maximum(n - 1, 0), 0)
    return _grid_call(
        body, nb + 1, (qkv, qkv, qkv, qkv, qkv, bias, sinks_rows, d_attn), exchange, name="attn_bwd",
        in_specs=_attn_in_specs(nb, True) + [pl.BlockSpec((lq, ATTN_WIDTH), cur)],
        out_specs=[pl.BlockSpec((lq, ATTN_WIDTH), cur), pl.BlockSpec((lq, KV_WIDTH), done),
                   pl.BlockSpec((lq, KV_WIDTH), done),
                   pl.BlockSpec((N_Q_HEADS, lq, 2 * lq), lambda n: (0, 0, 0)), _vec(128)],
        out_shape=[jax.ShapeDtypeStruct((s, ATTN_WIDTH), BF16), jax.ShapeDtypeStruct((s, KV_WIDTH), BF16),
                   jax.ShapeDtypeStruct((s, KV_WIDTH), BF16),
                   jax.ShapeDtypeStruct((N_Q_HEADS, lq, 2 * lq), F32), jax.ShapeDtypeStruct((1, 128), F32)],
        scratch_shapes=[pltpu.VMEM((lq, KV_WIDTH), F32), pltpu.VMEM((lq, KV_WIDTH), F32)],
        compiler_params=_params())


def rel_bias_table(rel_bias, bucket):
    def body(rb_ref, bk_ref, o_ref):
        bk = bk_ref[...]
        accs = [jnp.zeros(bk.shape, F32) for _ in range(N_Q_HEADS)]
        for b in range(N_BUCKETS):
            hit = bk == b
            accs = [jnp.where(hit, rb_ref[b, h], acc) for h, acc in enumerate(accs)]
        for h in range(N_Q_HEADS):
            o_ref[h] = accs[h]

    return pl.pallas_call(
        body, name="rel_bias_table",
        in_specs=[pl.BlockSpec(memory_space=pltpu.SMEM), pl.BlockSpec(memory_space=pltpu.VMEM)],
        out_shape=jax.ShapeDtypeStruct((N_Q_HEADS,) + bucket.shape, F32),
        compiler_params=_params(n_axes=0))(rel_bias, bucket)


def rel_bias_grad(dbias, bucket):
    def body(db_ref, bk_ref, o_ref):
        rows = lax.broadcasted_iota(jnp.int32, (N_BUCKETS, 128), 0)
        lanes = lax.broadcasted_iota(jnp.int32, (N_BUCKETS, 128), 1)
        bk = bk_ref[...]

        def per_bucket(b, acc):
            hit = (bk == b).astype(F32)
            for h in range(N_Q_HEADS):
                val = jnp.sum(db_ref[h] * hit, keepdims=True)
                acc = acc + jnp.where((rows == b) & (lanes == h), val, 0.0)
            return acc

        o_ref[...] = lax.fori_loop(0, N_BUCKETS, per_bucket, jnp.zeros((N_BUCKETS, 128), F32))

    return pl.pallas_call(body, name="rel_bias_grad", out_shape=jax.ShapeDtypeStruct((N_BUCKETS, 128), F32),
                          compiler_params=_params(n_axes=0))(dbias, bucket)


def _ssm_core(u, z, dt_raw, hprev, dt_bias, a_log, dskip, norm_w):
    lc = u.shape[0]
    xbc = _silu(u)
    xs, bm, cm = xbc[:, :SSM_WIDTH], xbc[:, SSM_WIDTH:SSM_WIDTH + SSM_BC], xbc[:, SSM_WIDTH + SSM_BC:]
    dt = jax.nn.softplus(dt_raw + dt_bias)
    adt = dt * (-jnp.exp(a_log))
    ri = lax.broadcasted_iota(jnp.int32, (lc, lc), 0)
    ci = lax.broadcasted_iota(jnp.int32, (lc, lc), 1)
    causal = ri >= ci
    acum = jnp.dot(causal.astype(F32), adt, precision=HIGHEST)
    acum_t = acum.T
    last = acum[lc - 1:lc, :]
    per_group = SSM_HEADS // SSM_GROUPS
    ys, hs = [], []
    for g in range(SSM_GROUPS):
        bg = bm[:, g * SSM_STATE:(g + 1) * SSM_STATE]
        cg = cm[:, g * SSM_STATE:(g + 1) * SSM_STATE]
        cb = mm_nt(cg, bg)
        for hh in range(per_group):
            h = g * per_group + hh
            xh = xs[:, h * SSM_HEAD_DIM:(h + 1) * SSM_HEAD_DIM]
            xc = xh * dt[:, h:h + 1]
            col, rowv, lasth = acum[:, h:h + 1], acum_t[h:h + 1, :], last[:, h:h + 1]
            decay = jnp.exp(jnp.where(causal, col - rowv, -1e30))
            hp = hprev[h * SSM_HEAD_DIM:(h + 1) * SSM_HEAD_DIM, :]
            y = mm(cb * decay, xc) + mm_nt(cg * jnp.exp(col), hp)
            st = mm_tn(xc, bg * jnp.exp(lasth - col))
            hs.append(jnp.exp(lasth) * hp + st)
            ys.append(y + dskip[:, h:h + 1] * xh)
    y = jnp.concatenate(ys, axis=1) * _silu(z)
    gw = SSM_WIDTH // SSM_GROUPS
    outs = []
    for g in range(SSM_GROUPS):
        yg = y[:, g * gw:(g + 1) * gw]
        outs.append(yg * lax.rsqrt(jnp.mean(yg * yg, axis=-1, keepdims=True) + NORM_EPS))
    return jnp.concatenate(outs, axis=1) * norm_w, jnp.concatenate(hs, axis=0)


def _ssm_param_specs():
    return [pl.BlockSpec((SSM_CONV, XBC_WIDTH), _const), _vec(XBC_WIDTH), _vec(128), _vec(128), _vec(128),
            _vec(SSM_WIDTH)]


def ssm_fwd(xbc_raw, z, dt_raw, conv_w, conv_b, dt_bias, a_log, dskip, norm_w, exchange=None):
    s = xbc_raw.shape[0]
    lc = SSM_CHUNK
    nc = s // lc
    hrows = SSM_HEADS * SSM_HEAD_DIM

    def body(x_ref, halo_ref, z_ref, dt_ref, cw_ref, cb_ref, dtb_ref, al_ref, dk_ref, nw_ref,
             o_ref, hp_ref, state):
        i = pl.program_id(0)

        @pl.when(i == 0)
        def _():
            state[...] = jnp.zeros_like(state)

        halo = halo_ref[...] * (i > 0).astype(F32)
        xin = jnp.concatenate([halo, x_ref[...]], axis=0)
        u = (_conv_rows(xin, cw_ref[...], SSM_CONV) + cb_ref[...])[HALO:]
        hprev = state[...]
        hp_ref[...] = hprev
        out, hnew = _ssm_core(u, z_ref[...], dt_ref[...], hprev, dtb_ref[...], al_ref[...], dk_ref[...], nw_ref[...])
        o_ref[...] = _b(out)
        state[...] = hnew

    tile = lambda n: pl.BlockSpec((lc, n), _row)
    halo_spec = pl.BlockSpec((HALO, XBC_WIDTH), lambda i: (jnp.maximum(i * (lc // HALO) - 1, 0), 0))
    return _grid_call(
        body, nc, (xbc_raw, xbc_raw, z, dt_raw, conv_w, conv_b, dt_bias, a_log, dskip, norm_w), exchange,
        name="ssm_fwd",
        in_specs=[tile(XBC_WIDTH), halo_spec, tile(SSM_WIDTH), tile(DT_PAD)] + _ssm_param_specs(),
        out_specs=[tile(SSM_WIDTH), pl.BlockSpec((hrows, SSM_STATE), _row)],
        out_shape=[jax.ShapeDtypeStruct((s, SSM_WIDTH), BF16), jax.ShapeDtypeStruct((nc * hrows, SSM_STATE), F32)],
        scratch_shapes=[pltpu.VMEM((hrows, SSM_STATE), F32)],
        compiler_params=_params())


def ssm_bwd(xbc_raw, z, dt_raw, hprev_all, d_out, conv_w, conv_b, dt_bias, a_log, dskip, norm_w, exchange=None):
    s = xbc_raw.shape[0]
    lc = SSM_CHUNK
    nc = s // lc
    hrows = SSM_HEADS * SSM_HEAD_DIM

    def body(x_ref, halo_ref, z_ref, dt_ref, hp_ref, do_ref, cw_ref, cb_ref, dtb_ref, al_ref, dk_ref, nw_ref,
             dx_ref, dz_ref, ddt_ref, dcw_ref, dcb_ref, ddtb_ref, dal_ref, ddk_ref, dnw_ref, dstate, du_next):
        i = pl.program_id(0)
        chunk = nc - 1 - i

        @pl.when(i == 0)
        def _():
            dstate[...] = jnp.zeros_like(dstate)
            du_next[...] = jnp.zeros_like(du_next)
            for r in (dcw_ref, dcb_ref, ddtb_ref, dal_ref, ddk_ref, dnw_ref):
                r[...] = jnp.zeros_like(r)

        halo = halo_ref[...] * (chunk > 0).astype(F32)
        xin = jnp.concatenate([halo, x_ref[...]], axis=0)
        cw = cw_ref[...]
        u = (_conv_rows(xin, cw, SSM_CONV) + cb_ref[...])[HALO:]
        _, vjp = jax.vjp(_ssm_core, u, z_ref[...], dt_ref[...], hp_ref[...], dtb_ref[...], al_ref[...],
                         dk_ref[...], nw_ref[...])
        du, dz, ddt, dhp, ddtb, dal, ddk, dnw = vjp((do_ref[...], dstate[...]))
        dstate[...] = dhp
        dz_ref[...] = dz
        ddt_ref[...] = ddt
        du_ext = jnp.concatenate([du, du_next[...]], axis=0)
        dx_ref[...] = _conv_rows_t(du_ext, cw, SSM_CONV)[:lc]
        du_next[...] = du[:HALO]
        rows = [jnp.sum(du * pltpu.roll(xin, j, axis=0)[HALO:] if j else du * xin[HALO:], axis=0, keepdims=True)
                for j in range(SSM_CONV)]
        dcw_ref[...] += jnp.concatenate(rows[::-1] + [jnp.zeros((8 - SSM_CONV, XBC_WIDTH), F32)], axis=0)
        dcb_ref[...] += jnp.sum(du, axis=0, keepdims=True)
        ddtb_ref[...] += ddtb
        dal_ref[...] += dal
        ddk_ref[...] += ddk
        dnw_ref[...] += dnw

    rev = lambda i: (nc - 1 - i, 0)
    tile = lambda n: pl.BlockSpec((lc, n), rev)
    halo_spec = pl.BlockSpec((HALO, XBC_WIDTH), lambda i: (jnp.maximum((nc - 1 - i) * (lc // HALO) - 1, 0), 0))
    acc = lambda r, n: pl.BlockSpec((r, n), _const)
    return _grid_call(
        body, nc, (xbc_raw, xbc_raw, z, dt_raw, hprev_all, d_out, conv_w, conv_b, dt_bias, a_log, dskip, norm_w),
        exchange, name="ssm_bwd",
        in_specs=[tile(XBC_WIDTH), halo_spec, tile(SSM_WIDTH), tile(DT_PAD), pl.BlockSpec((hrows, SSM_STATE), rev),
                  tile(SSM_WIDTH)] + _ssm_param_specs(),
        out_specs=[tile(XBC_WIDTH), tile(SSM_WIDTH), tile(DT_PAD), acc(8, XBC_WIDTH), acc(1, XBC_WIDTH),
                   acc(1, 128), acc(1, 128), acc(1, 128), acc(1, SSM_WIDTH)],
        out_shape=[jax.ShapeDtypeStruct((s, XBC_WIDTH), F32), jax.ShapeDtypeStruct((s, SSM_WIDTH), F32),
                   jax.ShapeDtypeStruct((s, DT_PAD), F32), jax.ShapeDtypeStruct((8, XBC_WIDTH), F32),
                   jax.ShapeDtypeStruct((1, XBC_WIDTH), F32), jax.ShapeDtypeStruct((1, 128), F32),
                   jax.ShapeDtypeStruct((1, 128), F32), jax.ShapeDtypeStruct((1, 128), F32),
                   jax.ShapeDtypeStruct((1, SSM_WIDTH), F32)],
        scratch_shapes=[pltpu.VMEM((hrows, SSM_STATE), F32), pltpu.VMEM((HALO, XBC_WIDTH), F32)],
        compiler_params=_params())


def mix_out(attn, ssm, x, w_out16, gate1, post_mix_w, pre_ffn_w, scale2, shift2, tm=TOKEN_TILE):
    s = x.shape[0]

    def body(a_ref, s_ref, x_ref, w_ref, g_ref, pw_ref, fw_ref, sc_ref, sh_ref, mixed_ref, x1_ref, h2_ref):
        mixed = (jnp.dot(a_ref[...], w_ref[:ATTN_WIDTH, :], preferred_element_type=F32)
                 + jnp.dot(s_ref[...], w_ref[ATTN_WIDTH:, :], preferred_element_type=F32))
        mixed_ref[...] = mixed
        x1 = x_ref[...] + g_ref[...] * _rms(mixed, pw_ref[...])
        x1_ref[...] = x1
        h2_ref[...] = _b(_norm_mod(x1, fw_ref[...], sc_ref[...], sh_ref[...]))

    tile = lambda n: pl.BlockSpec((tm, n), _row)
    return pl.pallas_call(
        body, name="mix_out", grid=(s // tm,),
        in_specs=[tile(ATTN_WIDTH), tile(SSM_WIDTH), tile(D_MODEL), pl.BlockSpec((D_MODEL, D_MODEL), _const)]
        + [_vec(D_MODEL)] * 5,
        out_specs=[tile(D_MODEL)] * 3,
        out_shape=[jax.ShapeDtypeStruct((s, D_MODEL), F32), jax.ShapeDtypeStruct((s, D_MODEL), F32),
                   jax.ShapeDtypeStruct((s, D_MODEL), BF16)],
        compiler_params=_params())(attn, ssm, x, w_out16, gate1, post_mix_w, pre_ffn_w, scale2, shift2)


def _gate(ug, uv):
    return jax.nn.gelu(ug, approximate=True) * uv


def _gate_bwd(ug, uv, df):
    k0, k1 = math.sqrt(2.0 / math.pi), 0.044715
    sq = ug * ug
    t = jnp.tanh(k0 * ug * (1.0 + k1 * sq))
    half = 0.5 * (1.0 + t)
    slope = half + (0.5 * k0) * ug * (1.0 - t * t) * (1.0 + (3.0 * k1) * sq)
    return df * uv * slope, df * (ug * half)


def _resident(shape):
    return pl.BlockSpec(shape, _const, pipeline_mode=pl.Buffered(1))


def up_gate(h2, w_up16, conv_w, conv_b, tm=TOKEN_TILE):
    s = h2.shape[0]

    def body(h_ref, halo_ref, w_ref, cw_ref, cb_ref, u_ref, uraw_ref, f_ref):
        halo = halo_ref[...]
        halo = jnp.where(pl.program_id(0) > 0, halo, jnp.zeros_like(halo))
        hin = jnp.concatenate([halo, h_ref[...]], axis=0)
        for lo in range(0, D_FF, FF_CHUNK):
            halves = []
            for base in (lo, D_FF + lo):
                cols = slice(base, base + FF_CHUNK)
                uraw = jnp.dot(hin, w_ref[:, cols], preferred_element_type=F32)
                uraw_ref[:, cols] = _b(uraw[NEXT:])
                u = (_conv_rows(uraw, cw_ref[:, cols], FFN_CONV) + cb_ref[:, cols])[NEXT:]
                u_ref[:, cols] = u
                halves.append(u)
            f_ref[:, lo:lo + FF_CHUNK] = _b(_gate(*halves))

    tile = lambda n: pl.BlockSpec((tm, n), _row)
    halo_spec = pl.BlockSpec((NEXT, D_MODEL), lambda i: (jnp.maximum(i * (tm // NEXT) - 1, 0), 0))
    return pl.pallas_call(
        body, name="up_gate", grid=(s // tm,),
        in_specs=[tile(D_MODEL), halo_spec, _resident((D_MODEL, 2 * D_FF)),
                  pl.BlockSpec((FFN_CONV, 2 * D_FF), _const), _vec(2 * D_FF)],
        out_specs=[tile(2 * D_FF), tile(2 * D_FF), tile(D_FF)],
        out_shape=[jax.ShapeDtypeStruct((s, 2 * D_FF), F32), jax.ShapeDtypeStruct((s, 2 * D_FF), BF16),
                   jax.ShapeDtypeStruct((s, D_FF), BF16)],
        compiler_params=_params())(h2, h2, w_up16, conv_w, conv_b)


def down_loss(f16, w_down16, x1, target, gate2, post_ffn_w, tm=TOKEN_TILE):
    s = x1.shape[0]

    def body(f_ref, wd_ref, x1_ref, t_ref, g_ref, pw_ref, dffn_ref, dy_ref, loss_ref, dg_ref, dpw_ref):
        i = pl.program_id(0)

        @pl.when(i == 0)
        def _():
            loss_ref[...] = jnp.zeros_like(loss_ref)
            dg_ref[...] = jnp.zeros_like(dg_ref)
            dpw_ref[...] = jnp.zeros_like(dpw_ref)

        ffn = jnp.dot(f_ref[...], wd_ref[...], preferred_element_type=F32)
        x1 = x1_ref[...]
        post = lambda ffn_, g_, w_: x1 + g_ * _rms(ffn_, w_)
        x2, vjp = jax.vjp(post, ffn, g_ref[...], pw_ref[...])
        err = x2 - t_ref[...]
        dy = err * (1.0 / D_MODEL)
        dy_ref[...] = dy
        loss_ref[...] += 0.5 * jnp.sum(jnp.mean(err * err, axis=-1, keepdims=True))
        dffn, dg, dpw = vjp(dy)
        dffn_ref[...] = _b(dffn)
        dg_ref[...] += dg
        dpw_ref[...] += dpw

    tile = lambda n: pl.BlockSpec((tm, n), _row)
    return pl.pallas_call(
        body, name="down_loss", grid=(s // tm,),
        in_specs=[tile(D_FF), _resident((D_FF, D_MODEL)), tile(D_MODEL), tile(D_MODEL), _vec(D_MODEL), _vec(D_MODEL)],
        out_specs=[tile(D_MODEL), tile(D_MODEL), _vec(128), _vec(D_MODEL), _vec(D_MODEL)],
        out_shape=[jax.ShapeDtypeStruct((s, D_MODEL), BF16), jax.ShapeDtypeStruct((s, D_MODEL), F32),
                   jax.ShapeDtypeStruct((1, 128), F32), jax.ShapeDtypeStruct((1, D_MODEL), F32),
                   jax.ShapeDtypeStruct((1, D_MODEL), F32)],
        compiler_params=_params())(f16, w_down16, x1, target, gate2, post_ffn_w)


BWD_CHUNK = 256


def ffn_bwd(u, u_raw16, d_ffn, conv_w, w_down_t16, w_up_t16, tm=TOKEN_TILE):
    s = u.shape[0]
    nt = s // tm

    def body(u_ref, unext_ref, uraw_ref, d_ref, dnext_ref, cw_ref, wdt_ref, wut_ref,
             du_ref, dh_ref, dcw_ref, dcb_ref):
        i = pl.program_id(0)

        @pl.when(i == 0)
        def _():
            dcw_ref[...] = jnp.zeros_like(dcw_ref)
            dcb_ref[...] = jnp.zeros_like(dcb_ref)

        dnext = dnext_ref[...]
        dnext = jnp.where(i < nt - 1, dnext, jnp.zeros_like(dnext))
        dff = jnp.concatenate([d_ref[...], dnext], axis=0)
        rows_ext = tm + NEXT
        dh = jnp.zeros((tm, D_MODEL), F32)
        for lo in range(0, D_FF, BWD_CHUNK):
            gcols, vcols = slice(lo, lo + BWD_CHUNK), slice(D_FF + lo, D_FF + lo + BWD_CHUNK)
            ug = jnp.concatenate([u_ref[:, gcols], unext_ref[:, gcols]], axis=0)
            uv = jnp.concatenate([u_ref[:, vcols], unext_ref[:, vcols]], axis=0)
            df = jnp.dot(dff, wdt_ref[:, gcols], preferred_element_type=F32)
            for cols, du in zip((gcols, vcols), _gate_bwd(ug, uv, df)):
                cw = cw_ref[:, cols]
                du1 = pltpu.roll(du, rows_ext - 1, axis=0)
                du2 = pltpu.roll(du, rows_ext - 2, axis=0)
                du_raw16 = _b((du * cw[2:3, :] + du1 * cw[1:2, :] + du2 * cw[0:1, :])[:tm])
                du_ref[:, cols] = du_raw16
                dh = dh + jnp.dot(du_raw16, wut_ref[cols, :], preferred_element_type=F32)
                xr = uraw_ref[:, cols].astype(F32)
                rows = [jnp.sum(xr * d_[:tm], axis=0, keepdims=True) for d_ in (du2, du1, du)]
                dcw_ref[:, cols] += jnp.concatenate(rows + [jnp.zeros((8 - FFN_CONV, BWD_CHUNK), F32)], axis=0)
                dcb_ref[:, cols] += jnp.sum(du[:tm], axis=0, keepdims=True)
        dh_ref[...] = dh

    tile = lambda n: pl.BlockSpec((tm, n), _row)
    nxt = lambda i: (jnp.minimum((i + 1) * (tm // NEXT), s // NEXT - 1), 0)
    return pl.pallas_call(
        body, name="ffn_bwd", grid=(nt,),
        in_specs=[tile(2 * D_FF), pl.BlockSpec((NEXT, 2 * D_FF), nxt), tile(2 * D_FF), tile(D_MODEL),
                  pl.BlockSpec((NEXT, D_MODEL), nxt), pl.BlockSpec((FFN_CONV, 2 * D_FF), _const),
                  _resident((D_MODEL, D_FF)), _resident((2 * D_FF, D_MODEL))],
        out_specs=[tile(2 * D_FF), tile(D_MODEL), pl.BlockSpec((8, 2 * D_FF), _const), _vec(2 * D_FF)],
        out_shape=[jax.ShapeDtypeStruct((s, 2 * D_FF), BF16), jax.ShapeDtypeStruct((s, D_MODEL), F32),
                   jax.ShapeDtypeStruct((8, 2 * D_FF), F32), jax.ShapeDtypeStruct((1, 2 * D_FF), F32)],
        compiler_params=_params())(u, u, u_raw16, d_ffn, d_ffn, conv_w, w_down_t16, w_up_t16)


def mix_bwd(dh2, x1, dy, mixed, w_out_t16, pre_ffn_w, scale2, shift2, gate1, post_mix_w, tm=TOKEN_TILE):
    s = x1.shape[0]

    def body(dh_ref, x1_ref, dy_ref, mx_ref, w_ref, fw_ref, sc_ref, sh_ref, g_ref, pw_ref,
             dx1_ref, dm_ref, da_ref, ds_ref, dfw_ref, dsc_ref, dsh_ref, dg_ref, dpw_ref):
        accs = (dfw_ref, dsc_ref, dsh_ref, dg_ref, dpw_ref)

        @pl.when(pl.program_id(0) == 0)
        def _():
            for r in accs:
                r[...] = jnp.zeros_like(r)

        _, vjp = jax.vjp(_norm_mod, x1_ref[...], fw_ref[...], sc_ref[...], sh_ref[...])
        dx1, dfw, dsc, dsh = vjp(dh_ref[...])
        dx1 = dx1 + dy_ref[...]
        dx1_ref[...] = dx1
        post = lambda m_, g_, w_: g_ * _rms(m_, w_)
        _, vjp2 = jax.vjp(post, mx_ref[...], g_ref[...], pw_ref[...])
        dmixed, dg, dpw = vjp2(dx1)
        dm16 = _b(dmixed)
        dm_ref[...] = dm16
        dmix_in = jnp.dot(dm16, w_ref[...], preferred_element_type=F32)
        da_ref[...] = _b(dmix_in[:, :ATTN_WIDTH])
        ds_ref[...] = dmix_in[:, ATTN_WIDTH:]
        for r, v in zip(accs, (dfw, dsc, dsh, dg, dpw)):
            r[...] += v

    tile = lambda n: pl.BlockSpec((tm, n), _row)
    return pl.pallas_call(
        body, name="mix_bwd", grid=(s // tm,),
        in_specs=[tile(D_MODEL)] * 4 + [pl.BlockSpec((D_MODEL, D_MODEL), _const)] + [_vec(D_MODEL)] * 5,
        out_specs=[tile(D_MODEL), tile(D_MODEL), tile(ATTN_WIDTH), tile(SSM_WIDTH)] + [_vec(D_MODEL)] * 5,
        out_shape=[jax.ShapeDtypeStruct((s, D_MODEL), F32), jax.ShapeDtypeStruct((s, D_MODEL), BF16),
                   jax.ShapeDtypeStruct((s, ATTN_WIDTH), BF16), jax.ShapeDtypeStruct((s, SSM_WIDTH), F32)]
        + [jax.ShapeDtypeStruct((1, D_MODEL), F32)] * 5,
        compiler_params=_params())(dh2, x1, dy, mixed, w_out_t16, pre_ffn_w, scale2, shift2, gate1, post_mix_w)


def inproj_bwd(dqkv, dxbc, dz, ddt, x, dx1, w_in_t16, pre_mix_w, scale1, shift1, tm=TOKEN_TILE):
    s = x.shape[0]

    def body(dq_ref, dxbc_ref, dz_ref, ddt_ref, x_ref, dx1_ref, w_ref, pw_ref, sc_ref, sh_ref,
             gx_ref, dpw_ref, dsc_ref, dsh_ref):
        accs = (dpw_ref, dsc_ref, dsh_ref)

        @pl.when(pl.program_id(0) == 0)
        def _():
            for r in accs:
                r[...] = jnp.zeros_like(r)

        dot = lambda r, lo, hi: jnp.dot(_b(r[...]), w_ref[lo:hi, :], preferred_element_type=F32)
        dh = (dot(dq_ref, 0, OFF_XBC) + dot(dxbc_ref, OFF_XBC, OFF_Z) + dot(dz_ref, OFF_Z, OFF_DT)
              + dot(ddt_ref, OFF_DT, PROJ_PAD))
        _, vjp = jax.vjp(_norm_mod, x_ref[...], pw_ref[...], sc_ref[...], sh_ref[...])
        dx, dpw, dsc, dsh = vjp(dh)
        gx_ref[...] = dx1_ref[...] + dx
        for r, v in zip(accs, (dpw, dsc, dsh)):
            r[...] += v

    tile = lambda n: pl.BlockSpec((tm, n), _row)
    return _grid_call(
        body, s // tm, (dqkv, dxbc, dz, ddt, x, dx1, w_in_t16, pre_mix_w, scale1, shift1), None,
        name="inproj_bwd",
        in_specs=[tile(QKV_W), tile(XBC_WIDTH), tile(SSM_WIDTH), tile(DT_PAD), tile(D_MODEL), tile(D_MODEL),
                  pl.BlockSpec((PROJ_PAD, D_MODEL), _const)] + [_vec(D_MODEL)] * 3,
        out_specs=[tile(D_MODEL)] + [_vec(D_MODEL)] * 3,
        out_shape=[jax.ShapeDtypeStruct((s, D_MODEL), F32)] + [jax.ShapeDtypeStruct((1, D_MODEL), F32)] * 3,
        compiler_params=_params())


def _adam(g, w, m, v):
    new_m = ADAM_B1 * m + (1.0 - ADAM_B1) * g
    new_v = ADAM_B2 * v + (1.0 - ADAM_B2) * jnp.square(g)
    m_hat = new_m / (1.0 - ADAM_B1 ** ADAM_STEP)
    v_hat = new_v / (1.0 - ADAM_B2 ** ADAM_STEP)
    return -ADAM_LR * (m_hat / (jnp.sqrt(v_hat) + ADAM_EPS) + ADAM_WD * w), new_m, new_v


ROW_PARAMS = (("b_ada", 6144, 6144), ("pre_mix_w", 1024, 1024), ("attn_sinks", 128, 8), ("ssm_conv_b", 1024, 1024),
              ("ssm_dt_bias", 128, 8), ("ssm_a_log", 128, 8), ("ssm_d", 128, 8), ("ssm_norm_w", 512, 512),
              ("post_mix_w", 1024, 1024), ("pre_ffn_w", 1024, 1024), ("ffn_conv_b", 5632, 5632),
              ("post_ffn_w", 1024, 1024))
LOSS_LANES = 128


def adamw_small(row_all, rb_all, rel_bias_wmv, row_wmv):
    n_rows = len(ROW_PARAMS)

    def body(*refs):
        row_ref, rb_ref = refs[:2]
        wmv = refs[2:5 + 3 * n_rows]
        outs = refs[5 + 3 * n_rows:]
        g_row, g_rb = row_ref[0], rb_ref[0]
        for k in range(1, N_DEV):
            g_row = g_row + row_ref[k]
            g_rb = g_rb + rb_ref[k]
        outs[0][...] = g_row[:, :LOSS_LANES]
        grads = [g_rb[:, :N_Q_HEADS]]
        off = LOSS_LANES
        for _, lanes, width in ROW_PARAMS:
            grads.append(g_row[:, off:off + width])
            off += lanes
        for i, g in enumerate(grads):
            w_ref, m_ref, v_ref = wmv[3 * i:3 * i + 3]
            g_out, d_out, m_out, v_out = outs[1 + 4 * i:5 + 4 * i]
            g_out[...] = g
            d_out[...], m_out[...], v_out[...] = _adam(g, w_ref[...], m_ref[...], v_ref[...])

    flat_wmv = list(rel_bias_wmv) + [a for wmv in row_wmv for a in wmv]
    shapes = [jax.ShapeDtypeStruct((1, LOSS_LANES), F32)] + [jax.ShapeDtypeStruct((N_BUCKETS, N_Q_HEADS), F32)] * 4
    for _, _, width in ROW_PARAMS:
        shapes += [jax.ShapeDtypeStruct((1, width), F32)] * 4
    return pl.pallas_call(body, name="adamw_small", out_shape=shapes,
                          compiler_params=_params(n_axes=0))(row_all, rb_all, *flat_wmv)


def adamw(parts, w, m, v, name):
    p, r, n = parts.shape
    tr = _row_tile(r)

    def body(p_ref, w_ref, m_ref, v_ref, g_ref, d_ref, nm_ref, nv_ref):
        g = p_ref[0].astype(F32)
        for k in range(1, p):
            g = g + p_ref[k].astype(F32)
        g_ref[...] = g
        d_ref[...], nm_ref[...], nv_ref[...] = _adam(g, w_ref[...], m_ref[...], v_ref[...])

    tile = pl.BlockSpec((tr, n), _row)
    return pl.pallas_call(
        body, name=name, grid=(r // tr,),
        in_specs=[pl.BlockSpec((p, tr, n), lambda i: (0, i, 0)), tile, tile, tile],
        out_specs=[tile] * 4, out_shape=[jax.ShapeDtypeStruct((r, n), F32)] * 4,
        compiler_params=_params())(parts, w, m, v)


def _bucket_table():
    lq = ATTN_BLOCK
    qi = np.arange(lq)[:, None] + lq
    kj = np.arange(2 * lq)[None, :]
    dist = qi - kj
    d = np.maximum(dist, 0)
    max_exact = N_BUCKETS // 2
    nf = np.maximum(d, 1).astype(np.float32)
    large = max_exact + (np.log(nf / max_exact) / math.log(REL_MAX_DIST / max_exact)
                         * (N_BUCKETS - max_exact)).astype(np.int32)
    large = np.minimum(large, N_BUCKETS - 1)
    bucket = np.where(d < max_exact, d, large).astype(np.int32)
    in_band = (dist >= 0) & (dist < REL_MAX_DIST)
    return np.where(in_band, bucket, -1).astype(np.int32)


def _cols_from_blocks(g):
    return jnp.transpose(g, (1, 0, 2)).reshape(g.shape[1], N_DEV * g.shape[2])


def _cols_to_blocks(a):
    r, n = a.shape
    return jnp.transpose(a.reshape(r, N_DEV, n // N_DEV), (1, 0, 2))


def _perm_in(w):
    pad = jnp.zeros((w.shape[0], DT_PAD - SSM_HEADS), w.dtype)
    return jnp.concatenate([w[:, :768], w[:, 768:1280], w[:, 1792:2304], w[:, 1280:1792], w[:, 2304:2312], pad], axis=1)


def _unperm_in(g):
    return jnp.concatenate([g[:, :768], g[:, 768:1280], g[:, 1792:2304], g[:, 1280:1792], g[:, 2304:2312]], axis=1)


def _lane_pad(v, n=128):
    return jnp.pad(v, ((0, 0), (0, n - v.shape[1])))


def kernel(x, c, rel_bias, w_ada, b_ada, pre_mix_w, w_in, attn_sinks, ssm_conv_w, ssm_conv_b, ssm_dt_bias, ssm_a_log, ssm_d, ssm_norm_w, w_out, post_mix_w, pre_ffn_w, w_up, ffn_conv_w, ffn_conv_b, w_down, post_ffn_w, loss_target, m_rel_bias, m_w_ada, m_b_ada, m_pre_mix_w, m_w_in, m_attn_sinks, m_ssm_conv_w, m_ssm_conv_b, m_ssm_dt_bias, m_ssm_a_log, m_ssm_d, m_ssm_norm_w, m_w_out, m_post_mix_w, m_pre_ffn_w, m_w_up, m_ffn_conv_w, m_ffn_conv_b, m_w_down, m_post_ffn_w, v_rel_bias, v_w_ada, v_b_ada, v_pre_mix_w, v_w_in, v_attn_sinks, v_ssm_conv_w, v_ssm_conv_b, v_ssm_dt_bias, v_ssm_a_log, v_ssm_d, v_ssm_norm_w, v_w_out, v_post_mix_w, v_pre_ffn_w, v_w_up, v_ffn_conv_w, v_ffn_conv_b, v_w_down, v_post_ffn_w):
    weights = dict(rel_bias=rel_bias, w_ada=w_ada, b_ada=b_ada, pre_mix_w=pre_mix_w, w_in=w_in, attn_sinks=attn_sinks, ssm_conv_w=ssm_conv_w, ssm_conv_b=ssm_conv_b, ssm_dt_bias=ssm_dt_bias, ssm_a_log=ssm_a_log, ssm_d=ssm_d, ssm_norm_w=ssm_norm_w, w_out=w_out, post_mix_w=post_mix_w, pre_ffn_w=pre_ffn_w, w_up=w_up, ffn_conv_w=ffn_conv_w, ffn_conv_b=ffn_conv_b, w_down=w_down, post_ffn_w=post_ffn_w)
    mom_m = dict(rel_bias=m_rel_bias, w_ada=m_w_ada, b_ada=m_b_ada, pre_mix_w=m_pre_mix_w, w_in=m_w_in, attn_sinks=m_attn_sinks, ssm_conv_w=m_ssm_conv_w, ssm_conv_b=m_ssm_conv_b, ssm_dt_bias=m_ssm_dt_bias, ssm_a_log=m_ssm_a_log, ssm_d=m_ssm_d, ssm_norm_w=m_ssm_norm_w, w_out=m_w_out, post_mix_w=m_post_mix_w, pre_ffn_w=m_pre_ffn_w, w_up=m_w_up, ffn_conv_w=m_ffn_conv_w, ffn_conv_b=m_ffn_conv_b, w_down=m_w_down, post_ffn_w=m_post_ffn_w)
    mom_v = dict(rel_bias=v_rel_bias, w_ada=v_w_ada, b_ada=v_b_ada, pre_mix_w=v_pre_mix_w, w_in=v_w_in, attn_sinks=v_attn_sinks, ssm_conv_w=v_ssm_conv_w, ssm_conv_b=v_ssm_conv_b, ssm_dt_bias=v_ssm_dt_bias, ssm_a_log=v_ssm_a_log, ssm_d=v_ssm_d, ssm_norm_w=v_ssm_norm_w, w_out=v_w_out, post_mix_w=v_post_mix_w, pre_ffn_w=v_pre_ffn_w, w_up=v_w_up, ffn_conv_w=v_ffn_conv_w, ffn_conv_b=v_ffn_conv_b, w_down=v_w_down, post_ffn_w=v_post_ffn_w)
    order = ['rel_bias', 'w_ada', 'b_ada', 'pre_mix_w', 'w_in', 'attn_sinks', 'ssm_conv_w', 'ssm_conv_b', 'ssm_dt_bias', 'ssm_a_log', 'ssm_d', 'ssm_norm_w', 'w_out', 'post_mix_w', 'pre_ffn_w', 'w_up', 'ffn_conv_w', 'ffn_conv_b', 'w_down', 'post_ffn_w']

    me = 4 * lax.axis_index("x") + 2 * lax.axis_index("y") + lax.axis_index("c")
    xs_ = x[0]
    target = loss_target[0]

    (w_in_g, scw_g, fcw_g, c_g) = all_gather([_b(w_in[0]), ssm_conv_w[0], ffn_conv_w[0], c], "gather_weights")
    w_in16 = _perm_in(_cols_from_blocks(w_in_g))
    w_in_t16 = w_in16.T
    ssm_cw = _cols_from_blocks(scw_g)
    ffn_cw = _cols_from_blocks(fcw_g)
    c_all = c_g.reshape(N_DEV, D_MODEL)

    n_cols = w_ada.shape[2]
    b_cols = lax.dynamic_slice(b_ada, (0, me * n_cols), (1, n_cols))
    mod_part = ada_fwd(c_all, w_ada[0], b_cols)
    (mod_rows,) = all_to_all([mod_part.reshape(N_DEV, 1, n_cols)], "scatter_mod")
    mod = mod_rows.reshape(N_MOD, 1, D_MODEL)
    shift1, scale1, gate1, shift2, scale2, gate2 = (mod[i] for i in range(N_MOD))

    bucket_band = jnp.asarray(_bucket_table())
    bias = rel_bias_table(rel_bias, bucket_band)
    sinks_row = _lane_pad(attn_sinks)
    dt_bias, a_log, dskip = _lane_pad(ssm_dt_bias), _lane_pad(ssm_a_log), _lane_pad(ssm_d)

    h1, qkv, xbc_raw, z, dt_raw, w_out_g = pre_mix_inproj(
        xs_, pre_mix_w, scale1, shift1, w_in16, [(_b(w_out[0]), False)])
    attn, w_up_g = attn_fwd(qkv, bias, sinks_row, [(_b(w_up[0]), False)])
    ssm, hprev_all, w_down_g = ssm_fwd(xbc_raw, z, dt_raw, ssm_cw, ssm_conv_b, dt_bias, a_log, dskip, ssm_norm_w,
                                       [(_b(w_down[0]), False)])
    w_out16 = w_out_g.reshape(D_MODEL, D_MODEL)
    w_out_t16 = w_out16.T
    w_up16 = _cols_from_blocks(w_up_g)
    w_up_t16 = w_up16.T
    w_down16 = w_down_g.reshape(D_FF, D_MODEL)
    w_down_t16 = w_down16.T
    mixed, x1, h2 = mix_out(attn, ssm, xs_, w_out16, gate1, post_mix_w, pre_ffn_w, scale2, shift2)
    u, u_raw16, f16 = up_gate(h2, w_up16, ffn_cw, ffn_conv_b)
    d_ffn, dy, loss_part, d_gate2, d_post_ffn_w = down_loss(f16, w_down16, x1, target, gate2, post_ffn_w)

    du_raw, dh2, d_ffn_cw, d_ffn_cb = ffn_bwd(u, u_raw16, d_ffn, ffn_cw, w_down_t16, w_up_t16)
    g_w_down = matmul_tn(f16, d_ffn, "grad_w_down", FF_CHUNK, D_MODEL)
    g_w_up = matmul_tn(h2, du_raw, "grad_w_up", D_MODEL, FF_CHUNK)
    (dx1, dmixed, d_attn, d_ssm, d_pre_ffn_w, d_scale2, d_shift2, d_gate1, d_post_mix_w) = mix_bwd(
        dh2, x1, dy, mixed, w_out_t16, pre_ffn_w, scale2, shift2, gate1, post_mix_w)
    g_w_out = jnp.concatenate([matmul_tn(attn, dmixed, "grad_w_out_attn", ATTN_WIDTH, D_MODEL),
                               matmul_tn(ssm, dmixed, "grad_w_out_ssm", SSM_WIDTH, D_MODEL)], axis=0)
    dq, dk, dv, dbias, dsinks, p_w_down = attn_bwd(
        qkv, bias, sinks_row, d_attn, [(g_w_down.reshape(N_DEV, D_FF // N_DEV, D_MODEL), True)])
    d_rel_bias = rel_bias_grad(dbias, bucket_band)
    (dxbc, dz, ddt, d_ssm_cw, d_ssm_cb, d_dt_bias, d_a_log, d_dskip, d_norm_w, p_w_up, p_w_out) = ssm_bwd(
        xbc_raw, z, dt_raw, hprev_all, d_ssm, ssm_cw, ssm_conv_b, dt_bias, a_log, dskip, ssm_norm_w,
        [(_cols_to_blocks(g_w_up), True), (g_w_out.reshape(N_DEV, D_MODEL // N_DEV, D_MODEL), True)])
    dqkv = jnp.concatenate([dq, dk, dv], axis=1)
    grad_x, d_pre_mix_w, d_scale1, d_shift1 = inproj_bwd(
        dqkv, dxbc, dz, ddt, xs_, dx1, w_in_t16, pre_mix_w, scale1, shift1)
    g_w_in = _unperm_in(jnp.concatenate(
        [matmul_tn(h1, dqkv, "grad_w_in_qkv", D_MODEL, QKV_W), matmul_tn(h1, dxbc, "grad_w_in_xbc", D_MODEL, XBC_WIDTH),
         matmul_tn(h1, dz, "grad_w_in_z", D_MODEL, SSM_WIDTH), matmul_tn(h1, ddt, "grad_w_in_dt", D_MODEL, DT_PAD)],
        axis=1))

    d_mod = jnp.concatenate([d_shift1, d_scale1, d_gate1, d_shift2, d_scale2, d_gate2], axis=1)
    late = ("w_in", "ssm_conv_w", "ffn_conv_w")
    full = [_cols_to_blocks(g_w_in), _cols_to_blocks(d_ssm_cw[:SSM_CONV]), _cols_to_blocks(d_ffn_cw[:FFN_CONV])]
    core = lax.axis_index("c").astype(jnp.int32).reshape(1)
    got = pair_exchange(full, "pair_grads")
    chip_sums = [pair_sum(f_, g_, core, "pair_sum_" + k) for k, f_, g_ in zip(late, full, got)]
    chip_parts = all_to_all(chip_sums, "scatter_grads", CHIP_FLIPS, _chip_index)
    (d_mod_rows,) = all_to_all([d_mod.reshape(N_DEV, 1, n_cols)], "scatter_dmod")
    g_w_ada = ada_bwd(c_all, d_mod_rows.reshape(N_DEV, n_cols))

    row_g = dict(b_ada=d_mod, pre_mix_w=d_pre_mix_w, attn_sinks=dsinks, ssm_conv_b=d_ssm_cb, ssm_dt_bias=d_dt_bias,
                 ssm_a_log=d_a_log, ssm_d=d_dskip, ssm_norm_w=d_norm_w, post_mix_w=d_post_mix_w,
                 pre_ffn_w=d_pre_ffn_w, ffn_conv_b=d_ffn_cb, post_ffn_w=d_post_ffn_w)
    row = jnp.concatenate([loss_part] + [row_g[k] for k, _, _ in ROW_PARAMS], axis=1)
    row_all, rb_all = all_gather([row, d_rel_bias], "gather_small")

    wmv = lambda k: (weights[k], mom_m[k], mom_v[k])
    small = adamw_small(row_all, rb_all, wmv("rel_bias"), [wmv(k) for k, _, _ in ROW_PARAMS])
    loss = small[0][0, 0]
    res = {k: tuple(small[1 + 4 * i:5 + 4 * i]) for i, k in enumerate(["rel_bias"] + [k for k, _, _ in ROW_PARAMS])}
    big = list(zip(late, chip_parts)) + [("w_down", p_w_down), ("w_up", p_w_up), ("w_out", p_w_out),
                                        ("w_ada", g_w_ada[None])]
    for k, parts in big:
        res[k] = tuple(o[None] for o in adamw(parts, weights[k][0], mom_m[k][0], mom_v[k][0], "adamw_" + k))

    outs = [loss, grad_x[None]]
    for field in range(4):
        outs += [res[k][field] for k in order]
    return tuple(outs)
```

```python
import math

import numpy as np
import jax
import jax.numpy as jnp
from jax import lax
from jax.experimental import pallas as pl
from jax.experimental.pallas import tpu as pltpu

F32 = jnp.float32
BF16 = jnp.bfloat16
MESH_ID = pl.DeviceIdType.MESH

N_DEV = 8
D_MODEL = 1024
N_Q_HEADS = 8
N_KV_HEADS = 2
HEAD_DIM = 64
ATTN_WIDTH = 512
KV_WIDTH = 128
ATTN_BLOCK = 128
N_BUCKETS = 32
REL_MAX_DIST = 128
SSM_HEADS = 8
SSM_HEAD_DIM = 64
SSM_WIDTH = 512
SSM_STATE = 128
SSM_GROUPS = 2
SSM_BC = 256
SSM_CONV = 4
SSM_CHUNK = 256
XBC_WIDTH = SSM_WIDTH + 2 * SSM_BC
D_FF = 2816
FFN_CONV = 3
NORM_EPS = 1e-6
N_MOD = 6
IN_PROJ_WIDTH = 2312
QKV_W = ATTN_WIDTH + 2 * KV_WIDTH
OFF_XBC = QKV_W
OFF_Z = OFF_XBC + XBC_WIDTH
OFF_DT = OFF_Z + SSM_WIDTH
DT_PAD = 128
PROJ_PAD = OFF_DT + DT_PAD
FF_CHUNK = 1408

ADAM_LR = 0.001
ADAM_B1 = 0.9
ADAM_B2 = 0.999
ADAM_EPS = 1e-08
ADAM_WD = 0.01
ADAM_STEP = 10

TOKEN_TILE = 256
HALO = 8
NEXT = 16
VMEM_LIMIT = 56 * 1024 * 1024


def _params(vmem=VMEM_LIMIT, n_axes=1):
    return pltpu.CompilerParams(dimension_semantics=("arbitrary",) * n_axes, vmem_limit_bytes=vmem)


def _b(x):
    return x.astype(BF16)


def _nn(a, b):
    return jnp.dot(_b(a), _b(b), preferred_element_type=F32)


def _nt(a, b):
    return lax.dot_general(_b(a), _b(b), (((1,), (1,)), ((), ())), preferred_element_type=F32)


def _tn(a, b):
    return lax.dot_general(_b(a), _b(b), (((0,), (0,)), ((), ())), preferred_element_type=F32)


@jax.custom_vjp
def mm(a, b):
    return _nn(a, b)


mm.defvjp(lambda a, b: (_nn(a, b), (a, b)),
          lambda r, g: (_nt(g, r[1]).astype(r[0].dtype), _tn(r[0], g).astype(r[1].dtype)))


@jax.custom_vjp
def mm_nt(a, b):
    return _nt(a, b)


mm_nt.defvjp(lambda a, b: (_nt(a, b), (a, b)),
             lambda r, g: (_nn(g, r[1]).astype(r[0].dtype), _tn(g, r[0]).astype(r[1].dtype)))


@jax.custom_vjp
def mm_tn(a, b):
    return _tn(a, b)


mm_tn.defvjp(lambda a, b: (_tn(a, b), (a, b)),
             lambda r, g: (_nt(r[1], g).astype(r[0].dtype), _nn(r[0], g).astype(r[1].dtype)))


def _rms(x, w):
    return x * lax.rsqrt(jnp.mean(x * x, axis=-1, keepdims=True) + NORM_EPS) * w


def _norm_mod(x, w, scale, shift):
    return _rms(x, w) * (1.0 + scale) + shift


def _rms_bwd(x, w, dy):
    r = lax.rsqrt(jnp.mean(x * x, axis=-1, keepdims=True) + NORM_EPS)
    xhat = x * r
    g = dy * w
    dx = r * (g - xhat * jnp.mean(g * xhat, axis=-1, keepdims=True))
    return dx, jnp.sum(dy * xhat, axis=0, keepdims=True)


def _norm_mod_bwd(x, w, scale, dh):
    dx, da = _rms_bwd(x, w * (1.0 + scale), dh)
    return dx, da * (1.0 + scale), da * w, jnp.sum(dh, axis=0, keepdims=True)


def _gated_rms_bwd(m, gate, w, dy):
    dm, t = _rms_bwd(m, w * gate, dy)
    return dm, t * w, t * gate


def _silu(x):
    return x * jax.nn.sigmoid(x)


def _conv_rows(xin, w, k):
    acc = xin * w[k - 1:k, :]
    for j in range(1, k):
        acc = acc + pltpu.roll(xin, j, axis=0) * w[k - 1 - j:k - j, :]
    return acc


def _conv_rows_t(du, w, k):
    n = du.shape[0]
    acc = du * w[k - 1:k, :]
    for j in range(1, k):
        acc = acc + pltpu.roll(du, n - j, axis=0) * w[k - 1 - j:k - j, :]
    return acc


def _row(i):
    return (i, 0)


def _const(i):
    return (0, 0)


def _vec(n):
    return pl.BlockSpec((1, n), _const)


def _block_index(p):
    return 4 * p[0] + 2 * p[1] + p[2]


def all_gather(arrs, name):
    n = len(arrs)

    def body(*refs):
        ins, outs = refs[:n], refs[n:2 * n]
        send_sems, recv_sems, local_sems = refs[2 * n:]
        x, y, c = lax.axis_index("x"), lax.axis_index("y"), lax.axis_index("c")
        me, sibling = (x, y, c), (x, y, 1 - c)
        chips = [(1 - x, y), (x, 1 - y), (1 - x, 1 - y)]

        def copy(a, k, block, to, src=None):
            dst = outs[a].at[_block_index(block)]
            return pltpu.make_async_remote_copy(
                src_ref=dst if src is None else src, dst_ref=dst,
                send_sem=send_sems.at[a * 7 + k], recv_sem=recv_sems.at[a * 7 + k],
                device_id=to, device_id_type=MESH_ID)

        mine = [pltpu.make_async_copy(ins[a], outs[a].at[_block_index(me)], local_sems.at[a]) for a in range(n)]
        for cp in mine:
            cp.start()
        first = []
        for a in range(n):
            first.append(copy(a, 0, me, sibling, src=ins[a]))
            first += [copy(a, 1 + j, me, (*chip, c), src=ins[a]) for j, chip in enumerate(chips)]
        for cp in first:
            cp.start()
        passed = []
        for j, chip in enumerate(chips):
            for a in range(n):
                copy(a, 1 + j, (*chip, c), me).wait_recv()
                cp = copy(a, 4 + j, (*chip, c), sibling)
                cp.start()
                passed.append(cp)
        for a in range(n):
            copy(a, 0, sibling, me).wait_recv()
            for j, chip in enumerate(chips):
                copy(a, 4 + j, (*chip, 1 - c), me).wait_recv()
        for cp in first + passed:
            cp.wait_send()
        for cp in mine:
            cp.wait()

    any_spec = pl.BlockSpec(memory_space=pl.ANY)
    return pl.pallas_call(
        body, name=name,
        out_shape=[jax.ShapeDtypeStruct((N_DEV,) + a.shape, a.dtype) for a in arrs],
        in_specs=[any_spec] * n, out_specs=[any_spec] * n,
        scratch_shapes=[pltpu.SemaphoreType.DMA((7 * n,)), pltpu.SemaphoreType.DMA((7 * n,)),
                        pltpu.SemaphoreType.DMA((n,))],
    )(*arrs)


ALL_FLIPS = ((0, 0, 1), (0, 1, 0), (0, 1, 1), (1, 0, 0), (1, 0, 1), (1, 1, 0), (1, 1, 1))
CHIP_FLIPS = ((0, 1, 0), (1, 0, 0), (1, 1, 0))


def _chip_index(p):
    return 2 * p[0] + p[1]


def all_to_all(arrs, name, flips=ALL_FLIPS, index=_block_index):
    n = len(arrs)
    nf = len(flips)

    def body(*refs):
        ins, outs = refs[:n], refs[n:2 * n]
        send_sems, recv_sems, local_sems = refs[2 * n:]
        pos = (lax.axis_index("x"), lax.axis_index("y"), lax.axis_index("c"))
        me = index(pos)
        peers = [tuple(1 - p if f else p for p, f in zip(pos, flip)) for flip in flips]

        def copy(a, k):
            peer = peers[k]
            return pltpu.make_async_remote_copy(
                src_ref=ins[a].at[index(peer)], dst_ref=outs[a].at[me],
                send_sem=send_sems.at[a * nf + k], recv_sem=recv_sems.at[a * nf + k],
                device_id=peer, device_id_type=MESH_ID)

        def landed(a, k):
            slot = outs[a].at[index(peers[k])]
            return pltpu.make_async_remote_copy(
                src_ref=slot, dst_ref=slot,
                send_sem=send_sems.at[a * nf + k], recv_sem=recv_sems.at[a * nf + k],
                device_id=peers[k], device_id_type=MESH_ID)

        mine = [pltpu.make_async_copy(ins[a].at[me], outs[a].at[me], local_sems.at[a]) for a in range(n)]
        for cp in mine:
            cp.start()
        sent = [copy(a, k) for a in range(n) for k in range(nf)]
        for cp in sent:
            cp.start()
        for a in range(n):
            for k in range(nf):
                landed(a, k).wait_recv()
        for cp in sent:
            cp.wait_send()
        for cp in mine:
            cp.wait()

    any_spec = pl.BlockSpec(memory_space=pl.ANY)
    return pl.pallas_call(
        body, name=name,
        out_shape=[jax.ShapeDtypeStruct(a.shape, a.dtype) for a in arrs],
        in_specs=[any_spec] * n, out_specs=[any_spec] * n,
        scratch_shapes=[pltpu.SemaphoreType.DMA((nf * n,)), pltpu.SemaphoreType.DMA((nf * n,)),
                        pltpu.SemaphoreType.DMA((n,))],
    )(*arrs)


def _direct_exchange(src, dst, sems, scatter):
    send_sems, recv_sems, local_sem = sems
    pos = (lax.axis_index("x"), lax.axis_index("y"), lax.axis_index("c"))
    me = _block_index(pos)
    peers = [tuple(1 - p if f else p for p, f in zip(pos, flip)) for flip in ALL_FLIPS]

    def outgoing(k):
        return pltpu.make_async_remote_copy(
            src_ref=src.at[_block_index(peers[k])] if scatter else src, dst_ref=dst.at[me],
            send_sem=send_sems.at[k], recv_sem=recv_sems.at[k], device_id=peers[k], device_id_type=MESH_ID)

    def incoming(k):
        slot = dst.at[_block_index(peers[k])]
        return pltpu.make_async_remote_copy(
            src_ref=slot, dst_ref=slot, send_sem=send_sems.at[k], recv_sem=recv_sems.at[k],
            device_id=peers[k], device_id_type=MESH_ID)

    def local():
        return pltpu.make_async_copy(src.at[me] if scatter else src, dst.at[me], local_sem)

    def start():
        local().start()
        for k in range(len(ALL_FLIPS)):
            outgoing(k).start()

    def finish():
        for k in range(len(ALL_FLIPS)):
            incoming(k).wait_recv()
        for k in range(len(ALL_FLIPS)):
            outgoing(k).wait_send()
        local().wait()

    return start, finish


def hosted_call(body, exchanges, steps, n_in, n_out, **call):
    n_ex = len(exchanges)

    def wrapped(*refs):
        ins, srcs = refs[:n_in], refs[n_in:n_in + n_ex]
        outs = refs[n_in + n_ex:n_in + n_ex + n_out]
        dsts = refs[n_in + n_ex + n_out:n_in + 2 * n_ex + n_out]
        rest = refs[n_in + 2 * n_ex + n_out:]
        scratch, sems = rest[:len(rest) - 3 * n_ex], rest[len(rest) - 3 * n_ex:]
        plans = [_direct_exchange(srcs[e], dsts[e], sems[3 * e:3 * e + 3], exchanges[e][1]) for e in range(n_ex)]

        @pl.when(pl.program_id(0) == 0)
        def _():
            for start, _ in plans:
                start()

        body(*ins, *outs, *scratch)

        @pl.when(pl.program_id(0) == steps - 1)
        def _():
            for _, finish in plans:
                finish()

    any_spec = pl.BlockSpec(memory_space=pl.ANY)
    landings = [jax.ShapeDtypeStruct(src.shape if scatter else (N_DEV,) + src.shape, src.dtype)
                for src, scatter in exchanges]
    n_flips = len(ALL_FLIPS)
    sems = [pltpu.SemaphoreType.DMA((n_flips,)), pltpu.SemaphoreType.DMA((n_flips,)), pltpu.SemaphoreType.DMA(())]
    return pl.pallas_call(
        wrapped, grid=(steps,),
        in_specs=list(call.pop("in_specs")) + [any_spec] * n_ex,
        out_specs=list(call.pop("out_specs")) + [any_spec] * n_ex,
        out_shape=list(call.pop("out_shape")) + landings,
        scratch_shapes=list(call.pop("scratch_shapes", [])) + sems * n_ex,
        **call)


def _grid_call(body, steps, args, exchanges, **call):
    if not exchanges:
        return pl.pallas_call(body, grid=(steps,), **call)(*args)
    srcs = [src for src, _ in exchanges]
    return hosted_call(body, exchanges, steps, len(args), len(call["out_shape"]), **call)(*args, *srcs)


N_CHIPS = 4


def pair_exchange(arrs, name):
    n = len(arrs)

    def body(*refs):
        ins, outs = refs[:n], refs[n:2 * n]
        send_sems, recv_sems = refs[2 * n:]
        x, y, c = lax.axis_index("x"), lax.axis_index("y"), lax.axis_index("c")
        sibling = (x, y, 1 - c)
        sent = []
        for a in range(n):
            for q in range(N_CHIPS):
                cp = pltpu.make_async_remote_copy(
                    src_ref=ins[a].at[2 * q + (1 - c)], dst_ref=outs[a].at[q],
                    send_sem=send_sems.at[a * N_CHIPS + q], recv_sem=recv_sems.at[a * N_CHIPS + q],
                    device_id=sibling, device_id_type=MESH_ID)
                cp.start()
                sent.append(cp)
        for cp in sent:
            cp.wait_recv()
        for cp in sent:
            cp.wait_send()

    any_spec = pl.BlockSpec(memory_space=pl.ANY)
    return pl.pallas_call(
        body, name=name,
        out_shape=[jax.ShapeDtypeStruct((N_CHIPS,) + a.shape[1:], a.dtype) for a in arrs],
        in_specs=[any_spec] * n, out_specs=[any_spec] * n,
        scratch_shapes=[pltpu.SemaphoreType.DMA((N_CHIPS * n,)), pltpu.SemaphoreType.DMA((N_CHIPS * n,))],
    )(*arrs)


def pair_sum(full, got, core, name):
    _, r, n = full.shape
    tr = _row_tile(r)

    def body(c_ref, mine_ref, got_ref, o_ref):
        o_ref[...] = _b(mine_ref[...] + got_ref[...])

    grid_spec = pltpu.PrefetchScalarGridSpec(
        num_scalar_prefetch=1, grid=(N_CHIPS, r // tr),
        in_specs=[pl.BlockSpec((1, tr, n), lambda q, i, c_ref: (2 * q + c_ref[0], i, 0)),
                  pl.BlockSpec((1, tr, n), lambda q, i, c_ref: (q, i, 0))],
        out_specs=pl.BlockSpec((1, tr, n), lambda q, i, c_ref: (q, i, 0)))
    return pl.pallas_call(body, name=name, grid_spec=grid_spec,
                          out_shape=jax.ShapeDtypeStruct((N_CHIPS, r, n), BF16),
                          compiler_params=_params(n_axes=2))(core, full, got)


def _row_tile(r):
    for cand in (256, 128, 64, 32, 16):
        if r % cand == 0 and r > cand:
            return cand
    return r


def ada_fwd(c_all, w_ada, b_cols):
    def body(c_ref, w_ref, b_ref, o_ref):
        o_ref[...] = _nn(_silu(c_ref[...]), w_ref[...]) + b_ref[...]

    return pl.pallas_call(body, name="ada_fwd",
                          out_shape=jax.ShapeDtypeStruct((N_DEV, w_ada.shape[1]), F32),
                          compiler_params=_params(n_axes=0))(c_all, w_ada, b_cols)


def ada_bwd(c_all, g_cols):
    def body(c_ref, g_ref, o_ref):
        o_ref[...] = _tn(_silu(c_ref[...]), g_ref[...])

    return pl.pallas_call(body, name="ada_bwd",
                          out_shape=jax.ShapeDtypeStruct((c_all.shape[1], g_cols.shape[1]), F32),
                          compiler_params=_params(n_axes=0))(c_all, g_cols)


def matmul_tn(a, b, name, bm, bn, tk=512):
    s, m = a.shape
    n = b.shape[1]
    tk = min(tk, s)

    def body(a_ref, b_ref, o_ref):
        @pl.when(pl.program_id(2) == 0)
        def _():
            o_ref[...] = jnp.zeros_like(o_ref)

        o_ref[...] += _tn(a_ref[...], b_ref[...])

    return pl.pallas_call(
        body, name=name, grid=(m // bm, n // bn, s // tk),
        in_specs=[pl.BlockSpec((tk, bm), lambda i, j, k: (k, i)), pl.BlockSpec((tk, bn), lambda i, j, k: (k, j))],
        out_specs=pl.BlockSpec((bm, bn), lambda i, j, k: (i, j)),
        out_shape=jax.ShapeDtypeStruct((m, n), F32),
        compiler_params=_params(n_axes=3))(a, b)


def pre_mix_inproj(x, w, scale, shift, w_in16, exchange=None, tm=TOKEN_TILE):
    s = x.shape[0]

    def body(x_ref, w_ref, sc_ref, sh_ref, win_ref, h_ref, qkv_ref, xbc_ref, z_ref, dt_ref):
        h16 = _b(_norm_mod(x_ref[...], w_ref[...], sc_ref[...], sh_ref[...]))
        h_ref[...] = h16
        dot = lambda lo, hi: jnp.dot(h16, win_ref[:, lo:hi], preferred_element_type=F32)
        qkv_ref[...] = _b(dot(0, OFF_XBC))
        xbc_ref[...] = dot(OFF_XBC, OFF_Z)
        z_ref[...] = dot(OFF_Z, OFF_DT)
        dt_ref[...] = dot(OFF_DT, PROJ_PAD)

    tile = lambda n: pl.BlockSpec((tm, n), _row)
    return _grid_call(
        body, s // tm, (x, w, scale, shift, w_in16), exchange, name="pre_mix_inproj",
        in_specs=[tile(D_MODEL), _vec(D_MODEL), _vec(D_MODEL), _vec(D_MODEL), pl.BlockSpec((D_MODEL, PROJ_PAD), _const)],
        out_specs=[tile(D_MODEL), tile(QKV_W), tile(XBC_WIDTH), tile(SSM_WIDTH), tile(DT_PAD)],
        out_shape=[jax.ShapeDtypeStruct((s, D_MODEL), BF16), jax.ShapeDtypeStruct((s, QKV_W), BF16),
                   jax.ShapeDtypeStruct((s, XBC_WIDTH), F32), jax.ShapeDtypeStruct((s, SSM_WIDTH), F32),
                   jax.ShapeDtypeStruct((s, DT_PAD), F32)],
        compiler_params=_params())


ATTN_QB = 2


def _attn_tile(q, kp, kc, vp, vc, bias, sinks):
    lq = ATTN_BLOCK
    group = N_Q_HEADS // N_KV_HEADS
    lanes = lax.broadcasted_iota(jnp.int32, (1, 128), 1)
    rid = lax.broadcasted_iota(jnp.int32, (group * lq, 1), 0)
    sink_cols = []
    for hk in range(N_KV_HEADS):
        sink = jnp.zeros((group * lq, 1), F32)
        for g in range(group):
            s_h = jnp.sum(jnp.where(lanes == hk * group + g, sinks, 0.0), axis=-1, keepdims=True)
            sink = jnp.where((rid >= g * lq) & (rid < (g + 1) * lq), s_h, sink)
        sink_cols.append(sink)
    kall = jnp.concatenate([kp, kc], axis=0)
    vall = jnp.concatenate([vp, vc], axis=0)
    blocks = []
    for b in range(ATTN_QB):
        qb = q[b * lq:(b + 1) * lq]
        outs = []
        for hk in range(N_KV_HEADS):
            cols = slice(hk * HEAD_DIM, (hk + 1) * HEAD_DIM)
            kb = kall[b * lq:(b + 2) * lq, cols]
            vb = vall[b * lq:(b + 2) * lq, cols]
            qg = jnp.concatenate([qb[:, (hk * group + g) * HEAD_DIM:(hk * group + g + 1) * HEAD_DIM]
                                  for g in range(group)], axis=0)
            sc = mm_nt(qg, kb) * (HEAD_DIM ** -0.5) + bias[b][hk]
            sink = sink_cols[hk]
            m = lax.stop_gradient(jnp.maximum(jnp.max(sc, axis=-1, keepdims=True), sink))
            p = jnp.exp(sc - m)
            probs = p / (jnp.sum(p, axis=-1, keepdims=True) + jnp.exp(sink - m))
            og = mm(probs, vb)
            outs += [og[g * lq:(g + 1) * lq] for g in range(group)]
        blocks.append(jnp.concatenate(outs, axis=1))
    return jnp.concatenate(blocks, axis=0)


def _attn_in_specs(nt, clamp):
    lq, tq = ATTN_BLOCK, ATTN_BLOCK * ATTN_QB
    cur = lambda n: jnp.minimum(n, nt - 1) if clamp else n
    prev = lambda n: jnp.maximum(cur(n) * ATTN_QB - 1, 0)
    kcol, vcol = ATTN_WIDTH // KV_WIDTH, ATTN_WIDTH // KV_WIDTH + 1
    return [pl.BlockSpec((tq, ATTN_WIDTH), lambda n: (cur(n), 0)),
            pl.BlockSpec((lq, KV_WIDTH), lambda n: (prev(n), kcol)),
            pl.BlockSpec((tq, KV_WIDTH), lambda n: (cur(n), kcol)),
            pl.BlockSpec((lq, KV_WIDTH), lambda n: (prev(n), vcol)),
            pl.BlockSpec((tq, KV_WIDTH), lambda n: (cur(n), vcol)),
            pl.BlockSpec((2, N_KV_HEADS, 4 * lq, 2 * lq), lambda n: (0, 0, 0, 0)),
            _vec(128)]


def _tile_bias(bias_ref, first):
    return [[jnp.where(first, bias_ref[1, hk], bias_ref[0, hk]) if b == 0 else bias_ref[0, hk]
             for hk in range(N_KV_HEADS)] for b in range(ATTN_QB)]


def attn_fwd(qkv, bias, sinks_rows, exchange=None):
    s = qkv.shape[0]
    tq = ATTN_BLOCK * ATTN_QB
    nt = s // tq

    def body(q_ref, kp_ref, kc_ref, vp_ref, vc_ref, bias_ref, sk_ref, o_ref):
        f = lambda r: r[...].astype(F32)
        o = _attn_tile(f(q_ref), f(kp_ref), f(kc_ref), f(vp_ref), f(vc_ref),
                       _tile_bias(bias_ref, pl.program_id(0) == 0), sk_ref[...])
        o_ref[...] = _b(o)

    return _grid_call(
        body, nt, (qkv, qkv, qkv, qkv, qkv, bias, sinks_rows), exchange, name="attn_fwd",
        in_specs=_attn_in_specs(nt, False),
        out_specs=[pl.BlockSpec((tq, ATTN_WIDTH), _row)],
        out_shape=[jax.ShapeDtypeStruct((s, ATTN_WIDTH), BF16)],
        compiler_params=_params())


def attn_bwd(qkv, bias, sinks_rows, d_attn, exchange=None):
    s = qkv.shape[0]
    lq, tq = ATTN_BLOCK, ATTN_BLOCK * ATTN_QB
    nt = s // tq

    def body(q_ref, kp_ref, kc_ref, vp_ref, vc_ref, bias_ref, sk_ref, do_ref,
             dq_ref, dk_ref, dv_ref, dbias_ref, dsk_ref, carry_k, carry_v):
        n = pl.program_id(0)

        @pl.when(n == 0)
        def _():
            dbias_ref[...] = jnp.zeros_like(dbias_ref)
            dsk_ref[...] = jnp.zeros_like(dsk_ref)
            carry_k[...] = jnp.zeros_like(carry_k)
            carry_v[...] = jnp.zeros_like(carry_v)

        @pl.when(n < nt)
        def _():
            f = lambda r: r[...].astype(F32)
            _, vjp = jax.vjp(_attn_tile, f(q_ref), f(kp_ref), f(kc_ref), f(vp_ref), f(vc_ref),
                             _tile_bias(bias_ref, n == 0), sk_ref[...])
            dq, dkp, dkc, dvp, dvc, dbias, dsk = vjp(f(do_ref))
            dq_ref[...] = _b(dq)
            done = tq - lq
            dk_ref[:done, :] = _b(carry_k[:done, :])
            dv_ref[:done, :] = _b(carry_v[:done, :])
            dk_ref[done:, :] = _b(carry_k[done:, :] + dkp)
            dv_ref[done:, :] = _b(carry_v[done:, :] + dvp)
            carry_k[...] = dkc
            carry_v[...] = dvc
            dsk_ref[...] += dsk
            first = (n == 0).astype(F32)
            for hk in range(N_KV_HEADS):
                total = dbias[0][hk]
                for b in range(1, ATTN_QB):
                    total = total + dbias[b][hk]
                dbias_ref[0, hk] += total - first * dbias[0][hk]
                dbias_ref[1, hk] += first * dbias[0][hk]

        @pl.when(n == nt)
        def _():
            dk_ref[...] = _b(carry_k[...])
            dv_ref[...] = _b(carry_v[...])

    cur = lambda n: (jnp.minimum(n, nt - 1), 0)
    done_map = lambda n: (jnp.maximum(n - 1, 0), 0)
    return _grid_call(
        body, nt + 1, (qkv, qkv, qkv, qkv, qkv, bias, sinks_rows, d_attn), exchange, name="attn_bwd",
        in_specs=_attn_in_specs(nt, True) + [pl.BlockSpec((tq, ATTN_WIDTH), cur)],
        out_specs=[pl.BlockSpec((tq, ATTN_WIDTH), cur), pl.BlockSpec((tq, KV_WIDTH), done_map),
                   pl.BlockSpec((tq, KV_WIDTH), done_map),
                   pl.BlockSpec((2, N_KV_HEADS, 4 * lq, 2 * lq), lambda n: (0, 0, 0, 0)), _vec(128)],
        out_shape=[jax.ShapeDtypeStruct((s, ATTN_WIDTH), BF16), jax.ShapeDtypeStruct((s, KV_WIDTH), BF16),
                   jax.ShapeDtypeStruct((s, KV_WIDTH), BF16),
                   jax.ShapeDtypeStruct((2, N_KV_HEADS, 4 * lq, 2 * lq), F32), jax.ShapeDtypeStruct((1, 128), F32)],
        scratch_shapes=[pltpu.VMEM((tq, KV_WIDTH), F32), pltpu.VMEM((tq, KV_WIDTH), F32)],
        compiler_params=_params())


def rel_bias_table(rel_bias, bucket):
    lq = ATTN_BLOCK
    group = N_Q_HEADS // N_KV_HEADS

    def body(rb_ref, bk_ref, o_ref):
        bk = bk_ref[...]
        prev_keys = lax.broadcasted_iota(jnp.int32, bk.shape, 1) < lq
        accs = [jnp.full(bk.shape, -1e30, F32) for _ in range(N_Q_HEADS)]
        for b in range(N_BUCKETS):
            hit = bk == b
            accs = [jnp.where(hit, rb_ref[b, h], acc) for h, acc in enumerate(accs)]
        for h in range(N_Q_HEADS):
            rows = slice((h % group) * lq, (h % group + 1) * lq)
            o_ref[0, h // group, rows, :] = accs[h]
            o_ref[1, h // group, rows, :] = jnp.where(prev_keys, -1e30, accs[h])

    return pl.pallas_call(
        body, name="rel_bias_table",
        in_specs=[pl.BlockSpec(memory_space=pltpu.SMEM), pl.BlockSpec(memory_space=pltpu.VMEM)],
        out_shape=jax.ShapeDtypeStruct((2, N_KV_HEADS, group * lq, 2 * lq), F32),
        compiler_params=_params(n_axes=0))(rel_bias, bucket)


def rel_bias_grad(dbias, bucket):
    lq = ATTN_BLOCK
    group = N_Q_HEADS // N_KV_HEADS

    def body(db_ref, bk_ref, o_ref):
        rows = lax.broadcasted_iota(jnp.int32, (N_BUCKETS, 128), 0)
        lanes = lax.broadcasted_iota(jnp.int32, (N_BUCKETS, 128), 1)
        bk = bk_ref[...]
        per_head = []
        for h in range(N_Q_HEADS):
            sl = slice((h % group) * lq, (h % group + 1) * lq)
            per_head.append(db_ref[0, h // group, sl, :] + db_ref[1, h // group, sl, :])

        def per_bucket(b, acc):
            hit = (bk == b).astype(F32)
            for h in range(N_Q_HEADS):
                val = jnp.sum(per_head[h] * hit, keepdims=True)
                acc = acc + jnp.where((rows == b) & (lanes == h), val, 0.0)
            return acc

        o_ref[...] = lax.fori_loop(0, N_BUCKETS, per_bucket, jnp.zeros((N_BUCKETS, 128), F32))

    return pl.pallas_call(body, name="rel_bias_grad", out_shape=jax.ShapeDtypeStruct((N_BUCKETS, 128), F32),
                          compiler_params=_params(n_axes=0))(dbias, bucket)


def _tri_sum(a, upper):
    n = a.shape[0]
    ri = lax.broadcasted_iota(jnp.int32, (n, n), 0)
    ci = lax.broadcasted_iota(jnp.int32, (n, n), 1)
    tri = ((ri <= ci) if upper else (ri >= ci)).astype(BF16)
    hi = a.astype(BF16)
    rest = a - hi.astype(F32)
    mid = rest.astype(BF16)
    lo = (rest - mid.astype(F32)).astype(BF16)
    dot = lambda part: jnp.dot(tri, part, preferred_element_type=F32)
    return dot(hi) + dot(mid) + dot(lo)


@jax.custom_vjp
def _cumsum_rows(a):
    return _tri_sum(a, False)


_cumsum_rows.defvjp(lambda a: (_tri_sum(a, False), None), lambda _, g: (_tri_sum(g, True),))


def _ssm_core(u, z, dt_raw, hprev, dt_bias, a_log, dskip, norm_w):
    lc = u.shape[0]
    xbc = _silu(u)
    xs, bm, cm = xbc[:, :SSM_WIDTH], xbc[:, SSM_WIDTH:SSM_WIDTH + SSM_BC], xbc[:, SSM_WIDTH + SSM_BC:]
    dt = jax.nn.softplus(dt_raw + dt_bias)
    adt = dt * (-jnp.exp(a_log))
    ri = lax.broadcasted_iota(jnp.int32, (lc, lc), 0)
    ci = lax.broadcasted_iota(jnp.int32, (lc, lc), 1)
    causal = ri >= ci
    acum = _cumsum_rows(adt)
    acum_t = acum.T
    last = acum[lc - 1:lc, :]
    per_group = SSM_HEADS // SSM_GROUPS
    lane = lax.broadcasted_iota(jnp.int32, (1, 128), 1)
    rowid = lax.broadcasted_iota(jnp.int32, (128, 1), 0)
    lo_lanes = lane < SSM_HEAD_DIM
    ys, hs = [], []
    for g in range(SSM_GROUPS):
        bg = bm[:, g * SSM_STATE:(g + 1) * SSM_STATE]
        cg = cm[:, g * SSM_STATE:(g + 1) * SSM_STATE]
        cb = mm_nt(cg, bg)
        for pp in range(per_group // 2):
            ha = g * per_group + 2 * pp
            xp = xs[:, ha * SSM_HEAD_DIM:(ha + 2) * SSM_HEAD_DIM]
            hp = hprev[ha * SSM_HEAD_DIM:(ha + 2) * SSM_HEAD_DIM, :]
            xcp = xp * jnp.where(lo_lanes, dt[:, ha:ha + 1], dt[:, ha + 1:ha + 2])
            y_h, st_h = [], []
            for h in (ha, ha + 1):
                col, rowv, lasth = acum[:, h:h + 1], acum_t[h:h + 1, :], last[:, h:h + 1]
                decay = jnp.exp(jnp.where(causal, col - rowv, -1e30))
                y_h.append(mm(cb * decay, xcp) + mm_nt(cg * jnp.exp(col), hp))
                st_h.append(mm_tn(xcp, bg * jnp.exp(lasth - col)))
            y_pair = jnp.where(lo_lanes, y_h[0], y_h[1])
            st_pair = jnp.where(rowid < SSM_HEAD_DIM, st_h[0], st_h[1])
            la, lb = last[:, ha:ha + 1], last[:, ha + 1:ha + 2]
            hs.append(jnp.exp(jnp.where(rowid < SSM_HEAD_DIM, la, lb)) * hp + st_pair)
            dsk = jnp.where(lo_lanes, dskip[:, ha:ha + 1], dskip[:, ha + 1:ha + 2])
            ys.append(y_pair + dsk * xp)
    y = jnp.concatenate(ys, axis=1) * _silu(z)
    gw = SSM_WIDTH // SSM_GROUPS
    outs = []
    for g in range(SSM_GROUPS):
        yg = y[:, g * gw:(g + 1) * gw]
        outs.append(yg * lax.rsqrt(jnp.mean(yg * yg, axis=-1, keepdims=True) + NORM_EPS))
    return jnp.concatenate(outs, axis=1) * norm_w, jnp.concatenate(hs, axis=0)


def _ssm_param_specs():
    return [pl.BlockSpec((SSM_CONV, XBC_WIDTH), _const), _vec(XBC_WIDTH), _vec(128), _vec(128), _vec(128),
            _vec(SSM_WIDTH)]


def ssm_fwd(xbc_raw, z, dt_raw, conv_w, conv_b, dt_bias, a_log, dskip, norm_w, exchange=None):
    s = xbc_raw.shape[0]
    lc = SSM_CHUNK
    nc = s // lc
    hrows = SSM_HEADS * SSM_HEAD_DIM

    def body(x_ref, halo_ref, z_ref, dt_ref, cw_ref, cb_ref, dtb_ref, al_ref, dk_ref, nw_ref,
             o_ref, hp_ref, state):
        i = pl.program_id(0)

        @pl.when(i == 0)
        def _():
            state[...] = jnp.zeros_like(state)

        halo = halo_ref[...] * (i > 0).astype(F32)
        xin = jnp.concatenate([halo, x_ref[...]], axis=0)
        u = (_conv_rows(xin, cw_ref[...], SSM_CONV) + cb_ref[...])[HALO:]
        hprev = state[...]
        hp_ref[...] = hprev
        out, hnew = _ssm_core(u, z_ref[...], dt_ref[...], hprev, dtb_ref[...], al_ref[...], dk_ref[...], nw_ref[...])
        o_ref[...] = _b(out)
        state[...] = hnew

    tile = lambda n: pl.BlockSpec((lc, n), _row)
    halo_spec = pl.BlockSpec((HALO, XBC_WIDTH), lambda i: (jnp.maximum(i * (lc // HALO) - 1, 0), 0))
    return _grid_call(
        body, nc, (xbc_raw, xbc_raw, z, dt_raw, conv_w, conv_b, dt_bias, a_log, dskip, norm_w), exchange,
        name="ssm_fwd",
        in_specs=[tile(XBC_WIDTH), halo_spec, tile(SSM_WIDTH), tile(DT_PAD)] + _ssm_param_specs(),
        out_specs=[tile(SSM_WIDTH), pl.BlockSpec((hrows, SSM_STATE), _row)],
        out_shape=[jax.ShapeDtypeStruct((s, SSM_WIDTH), BF16), jax.ShapeDtypeStruct((nc * hrows, SSM_STATE), F32)],
        scratch_shapes=[pltpu.VMEM((hrows, SSM_STATE), F32)],
        compiler_params=_params())


def ssm_bwd(xbc_raw, z, dt_raw, hprev_all, d_out, conv_w, conv_b, dt_bias, a_log, dskip, norm_w, exchange=None):
    s = xbc_raw.shape[0]
    lc = SSM_CHUNK
    nc = s // lc
    hrows = SSM_HEADS * SSM_HEAD_DIM

    def body(x_ref, halo_ref, z_ref, dt_ref, hp_ref, do_ref, cw_ref, cb_ref, dtb_ref, al_ref, dk_ref, nw_ref,
             dx_ref, dz_ref, ddt_ref, dcw_ref, dcb_ref, ddtb_ref, dal_ref, ddk_ref, dnw_ref, dstate, du_next):
        i = pl.program_id(0)
        chunk = nc - 1 - i

        @pl.when(i == 0)
        def _():
            dstate[...] = jnp.zeros_like(dstate)
            du_next[...] = jnp.zeros_like(du_next)
            for r in (dcw_ref, dcb_ref, ddtb_ref, dal_ref, ddk_ref, dnw_ref):
                r[...] = jnp.zeros_like(r)

        halo = halo_ref[...] * (chunk > 0).astype(F32)
        xin = jnp.concatenate([halo, x_ref[...]], axis=0)
        cw = cw_ref[...]
        u = (_conv_rows(xin, cw, SSM_CONV) + cb_ref[...])[HALO:]
        _, vjp = jax.vjp(_ssm_core, u, z_ref[...], dt_ref[...], hp_ref[...], dtb_ref[...], al_ref[...],
                         dk_ref[...], nw_ref[...])
        du, dz, ddt, dhp, ddtb, dal, ddk, dnw = vjp((do_ref[...], dstate[...]))
        dstate[...] = dhp
        dz_ref[...] = dz
        ddt_ref[...] = ddt
        du_ext = jnp.concatenate([du, du_next[...]], axis=0)
        dx_ref[...] = _conv_rows_t(du_ext, cw, SSM_CONV)[:lc]
        du_next[...] = du[:HALO]
        rows = [jnp.sum(du * pltpu.roll(xin, j, axis=0)[HALO:] if j else du * xin[HALO:], axis=0, keepdims=True)
                for j in range(SSM_CONV)]
        dcw_ref[...] += jnp.concatenate(rows[::-1] + [jnp.zeros((8 - SSM_CONV, XBC_WIDTH), F32)], axis=0)
        dcb_ref[...] += jnp.sum(du, axis=0, keepdims=True)
        ddtb_ref[...] += ddtb
        dal_ref[...] += dal
        ddk_ref[...] += ddk
        dnw_ref[...] += dnw

    rev = lambda i: (nc - 1 - i, 0)
    tile = lambda n: pl.BlockSpec((lc, n), rev)
    halo_spec = pl.BlockSpec((HALO, XBC_WIDTH), lambda i: (jnp.maximum((nc - 1 - i) * (lc // HALO) - 1, 0), 0))
    acc = lambda r, n: pl.BlockSpec((r, n), _const)
    return _grid_call(
        body, nc, (xbc_raw, xbc_raw, z, dt_raw, hprev_all, d_out, conv_w, conv_b, dt_bias, a_log, dskip, norm_w),
        exchange, name="ssm_bwd",
        in_specs=[tile(XBC_WIDTH), halo_spec, tile(SSM_WIDTH), tile(DT_PAD), pl.BlockSpec((hrows, SSM_STATE), rev),
                  tile(SSM_WIDTH)] + _ssm_param_specs(),
        out_specs=[tile(XBC_WIDTH), tile(SSM_WIDTH), tile(DT_PAD), acc(8, XBC_WIDTH), acc(1, XBC_WIDTH),
                   acc(1, 128), acc(1, 128), acc(1, 128), acc(1, SSM_WIDTH)],
        out_shape=[jax.ShapeDtypeStruct((s, XBC_WIDTH), F32), jax.ShapeDtypeStruct((s, SSM_WIDTH), F32),
                   jax.ShapeDtypeStruct((s, DT_PAD), F32), jax.ShapeDtypeStruct((8, XBC_WIDTH), F32),
                   jax.ShapeDtypeStruct((1, XBC_WIDTH), F32), jax.ShapeDtypeStruct((1, 128), F32),
                   jax.ShapeDtypeStruct((1, 128), F32), jax.ShapeDtypeStruct((1, 128), F32),
                   jax.ShapeDtypeStruct((1, SSM_WIDTH), F32)],
        scratch_shapes=[pltpu.VMEM((hrows, SSM_STATE), F32), pltpu.VMEM((HALO, XBC_WIDTH), F32)],
        compiler_params=_params())


def mix_out(attn, ssm, x, w_out16, gate1, post_mix_w, pre_ffn_w, scale2, shift2, tm=TOKEN_TILE):
    s = x.shape[0]

    def body(a_ref, s_ref, x_ref, w_ref, g_ref, pw_ref, fw_ref, sc_ref, sh_ref, mixed_ref, x1_ref, h2_ref):
        mixed = (jnp.dot(a_ref[...], w_ref[:ATTN_WIDTH, :], preferred_element_type=F32)
                 + jnp.dot(s_ref[...], w_ref[ATTN_WIDTH:, :], preferred_element_type=F32))
        mixed_ref[...] = mixed
        x1 = x_ref[...] + g_ref[...] * _rms(mixed, pw_ref[...])
        x1_ref[...] = x1
        h2_ref[...] = _b(_norm_mod(x1, fw_ref[...], sc_ref[...], sh_ref[...]))

    tile = lambda n: pl.BlockSpec((tm, n), _row)
    return pl.pallas_call(
        body, name="mix_out", grid=(s // tm,),
        in_specs=[tile(ATTN_WIDTH), tile(SSM_WIDTH), tile(D_MODEL), pl.BlockSpec((D_MODEL, D_MODEL), _const)]
        + [_vec(D_MODEL)] * 5,
        out_specs=[tile(D_MODEL)] * 3,
        out_shape=[jax.ShapeDtypeStruct((s, D_MODEL), F32), jax.ShapeDtypeStruct((s, D_MODEL), F32),
                   jax.ShapeDtypeStruct((s, D_MODEL), BF16)],
        compiler_params=_params())(attn, ssm, x, w_out16, gate1, post_mix_w, pre_ffn_w, scale2, shift2)


def _gate(ug, uv):
    return jax.nn.gelu(ug, approximate=True) * uv


def _gate_bwd(ug, uv, df):
    k0, k1 = math.sqrt(2.0 / math.pi), 0.044715
    sq = ug * ug
    t = jnp.tanh(k0 * ug * (1.0 + k1 * sq))
    half = 0.5 * (1.0 + t)
    slope = half + (0.5 * k0) * ug * (1.0 - t * t) * (1.0 + (3.0 * k1) * sq)
    return df * uv * slope, df * (ug * half)


def _resident(shape):
    return pl.BlockSpec(shape, _const, pipeline_mode=pl.Buffered(1))


def up_gate(h2, w_up16, conv_w, conv_b, tm=TOKEN_TILE):
    s = h2.shape[0]

    def body(h_ref, halo_ref, w_ref, cw_ref, cb_ref, u_ref, uraw_ref, f_ref):
        halo = halo_ref[...]
        halo = jnp.where(pl.program_id(0) > 0, halo, jnp.zeros_like(halo))
        hin = jnp.concatenate([halo, h_ref[...]], axis=0)
        for lo in range(0, D_FF, FF_CHUNK):
            halves = []
            for base in (lo, D_FF + lo):
                cols = slice(base, base + FF_CHUNK)
                uraw = jnp.dot(hin, w_ref[:, cols], preferred_element_type=F32)
                uraw_ref[:, cols] = _b(uraw[NEXT:])
                u = (_conv_rows(uraw, cw_ref[:, cols], FFN_CONV) + cb_ref[:, cols])[NEXT:]
                u_ref[:, cols] = u
                halves.append(u)
            f_ref[:, lo:lo + FF_CHUNK] = _b(_gate(*halves))

    tile = lambda n: pl.BlockSpec((tm, n), _row)
    halo_spec = pl.BlockSpec((NEXT, D_MODEL), lambda i: (jnp.maximum(i * (tm // NEXT) - 1, 0), 0))
    return pl.pallas_call(
        body, name="up_gate", grid=(s // tm,),
        in_specs=[tile(D_MODEL), halo_spec, _resident((D_MODEL, 2 * D_FF)),
                  pl.BlockSpec((FFN_CONV, 2 * D_FF), _const), _vec(2 * D_FF)],
        out_specs=[tile(2 * D_FF), tile(2 * D_FF), tile(D_FF)],
        out_shape=[jax.ShapeDtypeStruct((s, 2 * D_FF), F32), jax.ShapeDtypeStruct((s, 2 * D_FF), BF16),
                   jax.ShapeDtypeStruct((s, D_FF), BF16)],
        compiler_params=_params())(h2, h2, w_up16, conv_w, conv_b)


def down_loss(f16, w_down16, x1, target, gate2, post_ffn_w, tm=TOKEN_TILE):
    s = x1.shape[0]

    def body(f_ref, wd_ref, x1_ref, t_ref, g_ref, pw_ref, dffn_ref, dy_ref, loss_ref, dg_ref, dpw_ref):
        i = pl.program_id(0)

        @pl.when(i == 0)
        def _():
            loss_ref[...] = jnp.zeros_like(loss_ref)
            dg_ref[...] = jnp.zeros_like(dg_ref)
            dpw_ref[...] = jnp.zeros_like(dpw_ref)

        ffn = jnp.dot(f_ref[...], wd_ref[...], preferred_element_type=F32)
        x1 = x1_ref[...]
        x2 = x1 + g_ref[...] * _rms(ffn, pw_ref[...])
        err = x2 - t_ref[...]
        dy = err * (1.0 / D_MODEL)
        dy_ref[...] = dy
        loss_ref[...] += 0.5 * jnp.sum(jnp.mean(err * err, axis=-1, keepdims=True))
        dffn, dg, dpw = _gated_rms_bwd(ffn, g_ref[...], pw_ref[...], dy)
        dffn_ref[...] = _b(dffn)
        dg_ref[...] += dg
        dpw_ref[...] += dpw

    tile = lambda n: pl.BlockSpec((tm, n), _row)
    return pl.pallas_call(
        body, name="down_loss", grid=(s // tm,),
        in_specs=[tile(D_FF), _resident((D_FF, D_MODEL)), tile(D_MODEL), tile(D_MODEL), _vec(D_MODEL), _vec(D_MODEL)],
        out_specs=[tile(D_MODEL), tile(D_MODEL), _vec(128), _vec(D_MODEL), _vec(D_MODEL)],
        out_shape=[jax.ShapeDtypeStruct((s, D_MODEL), BF16), jax.ShapeDtypeStruct((s, D_MODEL), F32),
                   jax.ShapeDtypeStruct((1, 128), F32), jax.ShapeDtypeStruct((1, D_MODEL), F32),
                   jax.ShapeDtypeStruct((1, D_MODEL), F32)],
        compiler_params=_params())(f16, w_down16, x1, target, gate2, post_ffn_w)


BWD_CHUNK = 256


def ffn_bwd(u, u_raw16, d_ffn, conv_w, w_down_t16, w_up_t16, tm=TOKEN_TILE):
    s = u.shape[0]
    nt = s // tm

    def body(u_ref, unext_ref, uraw_ref, d_ref, dnext_ref, cw_ref, wdt_ref, wut_ref,
             du_ref, dh_ref, dcw_ref, dcb_ref):
        i = pl.program_id(0)

        @pl.when(i == 0)
        def _():
            dcw_ref[...] = jnp.zeros_like(dcw_ref)
            dcb_ref[...] = jnp.zeros_like(dcb_ref)

        dnext = dnext_ref[...]
        dnext = jnp.where(i < nt - 1, dnext, jnp.zeros_like(dnext))
        dff = jnp.concatenate([d_ref[...], dnext], axis=0)
        rows_ext = tm + NEXT
        for lo in range(0, D_FF, BWD_CHUNK):
            gcols, vcols = slice(lo, lo + BWD_CHUNK), slice(D_FF + lo, D_FF + lo + BWD_CHUNK)
            ug = jnp.concatenate([u_ref[:, gcols], unext_ref[:, gcols]], axis=0)
            uv = jnp.concatenate([u_ref[:, vcols], unext_ref[:, vcols]], axis=0)
            df = jnp.dot(dff, wdt_ref[:, gcols], preferred_element_type=F32)
            for cols, du in zip((gcols, vcols), _gate_bwd(ug, uv, df)):
                cw = cw_ref[:, cols]
                du1 = pltpu.roll(du, rows_ext - 1, axis=0)
                du2 = pltpu.roll(du, rows_ext - 2, axis=0)
                du_ref[:, cols] = _b((du * cw[2:3, :] + du1 * cw[1:2, :] + du2 * cw[0:1, :])[:tm])
                xr = uraw_ref[:, cols].astype(F32)
                rows = [jnp.sum(xr * d_[:tm], axis=0, keepdims=True) for d_ in (du2, du1, du)]
                dcw_ref[:, cols] += jnp.concatenate(rows + [jnp.zeros((8 - FFN_CONV, BWD_CHUNK), F32)], axis=0)
                dcb_ref[:, cols] += jnp.sum(du[:tm], axis=0, keepdims=True)
        dh_ref[...] = jnp.dot(du_ref[...], wut_ref[...], preferred_element_type=F32)

    tile = lambda n: pl.BlockSpec((tm, n), _row)
    nxt = lambda i: (jnp.minimum((i + 1) * (tm // NEXT), s // NEXT - 1), 0)
    return pl.pallas_call(
        body, name="ffn_bwd", grid=(nt,),
        in_specs=[tile(2 * D_FF), pl.BlockSpec((NEXT, 2 * D_FF), nxt), tile(2 * D_FF), tile(D_MODEL),
                  pl.BlockSpec((NEXT, D_MODEL), nxt), pl.BlockSpec((FFN_CONV, 2 * D_FF), _const),
                  _resident((D_MODEL, D_FF)), _resident((2 * D_FF, D_MODEL))],
        out_specs=[tile(2 * D_FF), tile(D_MODEL), pl.BlockSpec((8, 2 * D_FF), _const), _vec(2 * D_FF)],
        out_shape=[jax.ShapeDtypeStruct((s, 2 * D_FF), BF16), jax.ShapeDtypeStruct((s, D_MODEL), F32),
                   jax.ShapeDtypeStruct((8, 2 * D_FF), F32), jax.ShapeDtypeStruct((1, 2 * D_FF), F32)],
        compiler_params=_params())(u, u, u_raw16, d_ffn, d_ffn, conv_w, w_down_t16, w_up_t16)


def mix_bwd(dh2, x1, dy, mixed, w_out_t16, pre_ffn_w, scale2, shift2, gate1, post_mix_w, tm=TOKEN_TILE):
    s = x1.shape[0]

    def body(dh_ref, x1_ref, dy_ref, mx_ref, w_ref, fw_ref, sc_ref, sh_ref, g_ref, pw_ref,
             dx1_ref, dm_ref, da_ref, ds_ref, dfw_ref, dsc_ref, dsh_ref, dg_ref, dpw_ref):
        accs = (dfw_ref, dsc_ref, dsh_ref, dg_ref, dpw_ref)

        @pl.when(pl.program_id(0) == 0)
        def _():
            for r in accs:
                r[...] = jnp.zeros_like(r)

        dx1, dfw, dsc, dsh = _norm_mod_bwd(x1_ref[...], fw_ref[...], sc_ref[...], dh_ref[...])
        dx1 = dx1 + dy_ref[...]
        dx1_ref[...] = dx1
        dmixed, dg, dpw = _gated_rms_bwd(mx_ref[...], g_ref[...], pw_ref[...], dx1)
        dm16 = _b(dmixed)
        dm_ref[...] = dm16
        dmix_in = jnp.dot(dm16, w_ref[...], preferred_element_type=F32)
        da_ref[...] = _b(dmix_in[:, :ATTN_WIDTH])
        ds_ref[...] = dmix_in[:, ATTN_WIDTH:]
        for r, v in zip(accs, (dfw, dsc, dsh, dg, dpw)):
            r[...] += v

    tile = lambda n: pl.BlockSpec((tm, n), _row)
    return pl.pallas_call(
        body, name="mix_bwd", grid=(s // tm,),
        in_specs=[tile(D_MODEL)] * 4 + [pl.BlockSpec((D_MODEL, D_MODEL), _const)] + [_vec(D_MODEL)] * 5,
        out_specs=[tile(D_MODEL), tile(D_MODEL), tile(ATTN_WIDTH), tile(SSM_WIDTH)] + [_vec(D_MODEL)] * 5,
        out_shape=[jax.ShapeDtypeStruct((s, D_MODEL), F32), jax.ShapeDtypeStruct((s, D_MODEL), BF16),
                   jax.ShapeDtypeStruct((s, ATTN_WIDTH), BF16), jax.ShapeDtypeStruct((s, SSM_WIDTH), F32)]
        + [jax.ShapeDtypeStruct((1, D_MODEL), F32)] * 5,
        compiler_params=_params())(dh2, x1, dy, mixed, w_out_t16, pre_ffn_w, scale2, shift2, gate1, post_mix_w)


def inproj_bwd(dqkv, dxbc, dz, ddt, x, dx1, w_in_t16, pre_mix_w, scale1, shift1, tm=TOKEN_TILE):
    s = x.shape[0]

    def body(dq_ref, dxbc_ref, dz_ref, ddt_ref, x_ref, dx1_ref, w_ref, pw_ref, sc_ref, sh_ref,
             gx_ref, dpw_ref, dsc_ref, dsh_ref):
        accs = (dpw_ref, dsc_ref, dsh_ref)

        @pl.when(pl.program_id(0) == 0)
        def _():
            for r in accs:
                r[...] = jnp.zeros_like(r)

        dot = lambda r, lo, hi: jnp.dot(_b(r[...]), w_ref[lo:hi, :], preferred_element_type=F32)
        dh = (dot(dq_ref, 0, OFF_XBC) + dot(dxbc_ref, OFF_XBC, OFF_Z) + dot(dz_ref, OFF_Z, OFF_DT)
              + dot(ddt_ref, OFF_DT, PROJ_PAD))
        dx, dpw, dsc, dsh = _norm_mod_bwd(x_ref[...], pw_ref[...], sc_ref[...], dh)
        gx_ref[...] = dx1_ref[...] + dx
        for r, v in zip(accs, (dpw, dsc, dsh)):
            r[...] += v

    tile = lambda n: pl.BlockSpec((tm, n), _row)
    return _grid_call(
        body, s // tm, (dqkv, dxbc, dz, ddt, x, dx1, w_in_t16, pre_mix_w, scale1, shift1), None,
        name="inproj_bwd",
        in_specs=[tile(QKV_W), tile(XBC_WIDTH), tile(SSM_WIDTH), tile(DT_PAD), tile(D_MODEL), tile(D_MODEL),
                  pl.BlockSpec((PROJ_PAD, D_MODEL), _const)] + [_vec(D_MODEL)] * 3,
        out_specs=[tile(D_MODEL)] + [_vec(D_MODEL)] * 3,
        out_shape=[jax.ShapeDtypeStruct((s, D_MODEL), F32)] + [jax.ShapeDtypeStruct((1, D_MODEL), F32)] * 3,
        compiler_params=_params())


def _adam(g, w, m, v):
    new_m = ADAM_B1 * m + (1.0 - ADAM_B1) * g
    new_v = ADAM_B2 * v + (1.0 - ADAM_B2) * jnp.square(g)
    m_hat = new_m / (1.0 - ADAM_B1 ** ADAM_STEP)
    v_hat = new_v / (1.0 - ADAM_B2 ** ADAM_STEP)
    return -ADAM_LR * (m_hat / (jnp.sqrt(v_hat) + ADAM_EPS) + ADAM_WD * w), new_m, new_v


ROW_PARAMS = (("b_ada", 6144, 6144), ("pre_mix_w", 1024, 1024), ("attn_sinks", 128, 8), ("ssm_conv_b", 1024, 1024),
              ("ssm_dt_bias", 128, 8), ("ssm_a_log", 128, 8), ("ssm_d", 128, 8), ("ssm_norm_w", 512, 512),
              ("post_mix_w", 1024, 1024), ("pre_ffn_w", 1024, 1024), ("ffn_conv_b", 5632, 5632),
              ("post_ffn_w", 1024, 1024))
LOSS_LANES = 128


def adamw_small(row_all, rb_all, rel_bias_wmv, row_wmv):
    n_rows = len(ROW_PARAMS)

    def body(*refs):
        row_ref, rb_ref = refs[:2]
        wmv = refs[2:5 + 3 * n_rows]
        outs = refs[5 + 3 * n_rows:]
        g_row, g_rb = row_ref[0], rb_ref[0]
        for k in range(1, N_DEV):
            g_row = g_row + row_ref[k]
            g_rb = g_rb + rb_ref[k]
        outs[0][...] = g_row[:, :LOSS_LANES]
        grads = [g_rb[:, :N_Q_HEADS]]
        off = LOSS_LANES
        for _, lanes, width in ROW_PARAMS:
            grads.append(g_row[:, off:off + width])
            off += lanes
        for i, g in enumerate(grads):
            w_ref, m_ref, v_ref = wmv[3 * i:3 * i + 3]
            g_out, d_out, m_out, v_out = outs[1 + 4 * i:5 + 4 * i]
            g_out[...] = g
            d_out[...], m_out[...], v_out[...] = _adam(g, w_ref[...], m_ref[...], v_ref[...])

    flat_wmv = list(rel_bias_wmv) + [a for wmv in row_wmv for a in wmv]
    shapes = [jax.ShapeDtypeStruct((1, LOSS_LANES), F32)] + [jax.ShapeDtypeStruct((N_BUCKETS, N_Q_HEADS), F32)] * 4
    for _, _, width in ROW_PARAMS:
        shapes += [jax.ShapeDtypeStruct((1, width), F32)] * 4
    return pl.pallas_call(body, name="adamw_small", out_shape=shapes,
                          compiler_params=_params(n_axes=0))(row_all, rb_all, *flat_wmv)


def adamw(parts, w, m, v, name):
    p, r, n = parts.shape
    tr = _row_tile(r)

    def body(p_ref, w_ref, m_ref, v_ref, g_ref, d_ref, nm_ref, nv_ref):
        g = p_ref[0].astype(F32)
        for k in range(1, p):
            g = g + p_ref[k].astype(F32)
        g_ref[...] = g
        d_ref[...], nm_ref[...], nv_ref[...] = _adam(g, w_ref[...], m_ref[...], v_ref[...])

    tile = pl.BlockSpec((tr, n), _row)
    return pl.pallas_call(
        body, name=name, grid=(r // tr,),
        in_specs=[pl.BlockSpec((p, tr, n), lambda i: (0, i, 0)), tile, tile, tile],
        out_specs=[tile] * 4, out_shape=[jax.ShapeDtypeStruct((r, n), F32)] * 4,
        compiler_params=_params())(parts, w, m, v)


def _bucket_table():
    lq = ATTN_BLOCK
    qi = np.arange(lq)[:, None] + lq
    kj = np.arange(2 * lq)[None, :]
    dist = qi - kj
    d = np.maximum(dist, 0)
    max_exact = N_BUCKETS // 2
    nf = np.maximum(d, 1).astype(np.float32)
    large = max_exact + (np.log(nf / max_exact) / math.log(REL_MAX_DIST / max_exact)
                         * (N_BUCKETS - max_exact)).astype(np.int32)
    large = np.minimum(large, N_BUCKETS - 1)
    bucket = np.where(d < max_exact, d, large).astype(np.int32)
    in_band = (dist >= 0) & (dist < REL_MAX_DIST)
    return np.where(in_band, bucket, -1).astype(np.int32)


def _cols_from_blocks(g):
    return jnp.transpose(g, (1, 0, 2)).reshape(g.shape[1], N_DEV * g.shape[2])


def _cols_to_blocks(a):
    r, n = a.shape
    return jnp.transpose(a.reshape(r, N_DEV, n // N_DEV), (1, 0, 2))


def _perm_in(w):
    pad = jnp.zeros((w.shape[0], DT_PAD - SSM_HEADS), w.dtype)
    return jnp.concatenate([w[:, :768], w[:, 768:1280], w[:, 1792:2304], w[:, 1280:1792], w[:, 2304:2312], pad], axis=1)


def _unperm_in(g):
    return jnp.concatenate([g[:, :768], g[:, 768:1280], g[:, 1792:2304], g[:, 1280:1792], g[:, 2304:2312]], axis=1)


def _lane_pad(v, n=128):
    return jnp.pad(v, ((0, 0), (0, n - v.shape[1])))


def kernel(x, c, rel_bias, w_ada, b_ada, pre_mix_w, w_in, attn_sinks, ssm_conv_w, ssm_conv_b, ssm_dt_bias, ssm_a_log, ssm_d, ssm_norm_w, w_out, post_mix_w, pre_ffn_w, w_up, ffn_conv_w, ffn_conv_b, w_down, post_ffn_w, loss_target, m_rel_bias, m_w_ada, m_b_ada, m_pre_mix_w, m_w_in, m_attn_sinks, m_ssm_conv_w, m_ssm_conv_b, m_ssm_dt_bias, m_ssm_a_log, m_ssm_d, m_ssm_norm_w, m_w_out, m_post_mix_w, m_pre_ffn_w, m_w_up, m_ffn_conv_w, m_ffn_conv_b, m_w_down, m_post_ffn_w, v_rel_bias, v_w_ada, v_b_ada, v_pre_mix_w, v_w_in, v_attn_sinks, v_ssm_conv_w, v_ssm_conv_b, v_ssm_dt_bias, v_ssm_a_log, v_ssm_d, v_ssm_norm_w, v_w_out, v_post_mix_w, v_pre_ffn_w, v_w_up, v_ffn_conv_w, v_ffn_conv_b, v_w_down, v_post_ffn_w):
    weights = dict(rel_bias=rel_bias, w_ada=w_ada, b_ada=b_ada, pre_mix_w=pre_mix_w, w_in=w_in, attn_sinks=attn_sinks, ssm_conv_w=ssm_conv_w, ssm_conv_b=ssm_conv_b, ssm_dt_bias=ssm_dt_bias, ssm_a_log=ssm_a_log, ssm_d=ssm_d, ssm_norm_w=ssm_norm_w, w_out=w_out, post_mix_w=post_mix_w, pre_ffn_w=pre_ffn_w, w_up=w_up, ffn_conv_w=ffn_conv_w, ffn_conv_b=ffn_conv_b, w_down=w_down, post_ffn_w=post_ffn_w)
    mom_m = dict(rel_bias=m_rel_bias, w_ada=m_w_ada, b_ada=m_b_ada, pre_mix_w=m_pre_mix_w, w_in=m_w_in, attn_sinks=m_attn_sinks, ssm_conv_w=m_ssm_conv_w, ssm_conv_b=m_ssm_conv_b, ssm_dt_bias=m_ssm_dt_bias, ssm_a_log=m_ssm_a_log, ssm_d=m_ssm_d, ssm_norm_w=m_ssm_norm_w, w_out=m_w_out, post_mix_w=m_post_mix_w, pre_ffn_w=m_pre_ffn_w, w_up=m_w_up, ffn_conv_w=m_ffn_conv_w, ffn_conv_b=m_ffn_conv_b, w_down=m_w_down, post_ffn_w=m_post_ffn_w)
    mom_v = dict(rel_bias=v_rel_bias, w_ada=v_w_ada, b_ada=v_b_ada, pre_mix_w=v_pre_mix_w, w_in=v_w_in, attn_sinks=v_attn_sinks, ssm_conv_w=v_ssm_conv_w, ssm_conv_b=v_ssm_conv_b, ssm_dt_bias=v_ssm_dt_bias, ssm_a_log=v_ssm_a_log, ssm_d=v_ssm_d, ssm_norm_w=v_ssm_norm_w, w_out=v_w_out, post_mix_w=v_post_mix_w, pre_ffn_w=v_pre_ffn_w, w_up=v_w_up, ffn_conv_w=v_ffn_conv_w, ffn_conv_b=v_ffn_conv_b, w_down=v_w_down, post_ffn_w=v_post_ffn_w)
    order = ['rel_bias', 'w_ada', 'b_ada', 'pre_mix_w', 'w_in', 'attn_sinks', 'ssm_conv_w', 'ssm_conv_b', 'ssm_dt_bias', 'ssm_a_log', 'ssm_d', 'ssm_norm_w', 'w_out', 'post_mix_w', 'pre_ffn_w', 'w_up', 'ffn_conv_w', 'ffn_conv_b', 'w_down', 'post_ffn_w']

    me = 4 * lax.axis_index("x") + 2 * lax.axis_index("y") + lax.axis_index("c")
    xs_ = x[0]
    target = loss_target[0]

    (w_in_g, scw_g, fcw_g, c_g) = all_gather([_b(w_in[0]), ssm_conv_w[0], ffn_conv_w[0], c], "gather_weights")
    w_in16 = _perm_in(_cols_from_blocks(w_in_g))
    w_in_t16 = w_in16.T
    ssm_cw = _cols_from_blocks(scw_g)
    ffn_cw = _cols_from_blocks(fcw_g)
    c_all = c_g.reshape(N_DEV, D_MODEL)

    n_cols = w_ada.shape[2]
    b_cols = lax.dynamic_slice(b_ada, (0, me * n_cols), (1, n_cols))
    mod_part = ada_fwd(c_all, w_ada[0], b_cols)
    (mod_rows,) = all_to_all([mod_part.reshape(N_DEV, 1, n_cols)], "scatter_mod")
    mod = mod_rows.reshape(N_MOD, 1, D_MODEL)
    shift1, scale1, gate1, shift2, scale2, gate2 = (mod[i] for i in range(N_MOD))

    bucket_band = jnp.asarray(_bucket_table())
    bias = rel_bias_table(rel_bias, bucket_band)
    sinks_row = _lane_pad(attn_sinks)
    dt_bias, a_log, dskip = _lane_pad(ssm_dt_bias), _lane_pad(ssm_a_log), _lane_pad(ssm_d)

    h1, qkv, xbc_raw, z, dt_raw, w_out_g = pre_mix_inproj(
        xs_, pre_mix_w, scale1, shift1, w_in16, [(_b(w_out[0]), False)])
    attn, w_up_g = attn_fwd(qkv, bias, sinks_row, [(_b(w_up[0]), False)])
    ssm, hprev_all, w_down_g = ssm_fwd(xbc_raw, z, dt_raw, ssm_cw, ssm_conv_b, dt_bias, a_log, dskip, ssm_norm_w,
                                       [(_b(w_down[0]), False)])
    w_out16 = w_out_g.reshape(D_MODEL, D_MODEL)
    w_out_t16 = w_out16.T
    w_up16 = _cols_from_blocks(w_up_g)
    w_up_t16 = w_up16.T
    w_down16 = w_down_g.reshape(D_FF, D_MODEL)
    w_down_t16 = w_down16.T
    mixed, x1, h2 = mix_out(attn, ssm, xs_, w_out16, gate1, post_mix_w, pre_ffn_w, scale2, shift2)
    u, u_raw16, f16 = up_gate(h2, w_up16, ffn_cw, ffn_conv_b)
    d_ffn, dy, loss_part, d_gate2, d_post_ffn_w = down_loss(f16, w_down16, x1, target, gate2, post_ffn_w)

    du_raw, dh2, d_ffn_cw, d_ffn_cb = ffn_bwd(u, u_raw16, d_ffn, ffn_cw, w_down_t16, w_up_t16)
    g_w_down = matmul_tn(f16, d_ffn, "grad_w_down", FF_CHUNK, D_MODEL)
    g_w_up = matmul_tn(h2, du_raw, "grad_w_up", D_MODEL, FF_CHUNK)
    (dx1, dmixed, d_attn, d_ssm, d_pre_ffn_w, d_scale2, d_shift2, d_gate1, d_post_mix_w) = mix_bwd(
        dh2, x1, dy, mixed, w_out_t16, pre_ffn_w, scale2, shift2, gate1, post_mix_w)
    g_w_out = jnp.concatenate([matmul_tn(attn, dmixed, "grad_w_out_attn", ATTN_WIDTH, D_MODEL),
                               matmul_tn(ssm, dmixed, "grad_w_out_ssm", SSM_WIDTH, D_MODEL)], axis=0)
    dq, dk, dv, dbias, dsinks, p_w_down = attn_bwd(
        qkv, bias, sinks_row, d_attn, [(g_w_down.reshape(N_DEV, D_FF // N_DEV, D_MODEL), True)])
    d_rel_bias = rel_bias_grad(dbias, bucket_band)
    (dxbc, dz, ddt, d_ssm_cw, d_ssm_cb, d_dt_bias, d_a_log, d_dskip, d_norm_w, p_w_up, p_w_out) = ssm_bwd(
        xbc_raw, z, dt_raw, hprev_all, d_ssm, ssm_cw, ssm_conv_b, dt_bias, a_log, dskip, ssm_norm_w,
        [(_cols_to_blocks(g_w_up), True), (g_w_out.reshape(N_DEV, D_MODEL // N_DEV, D_MODEL), True)])
    dqkv = jnp.concatenate([dq, dk, dv], axis=1)
    grad_x, d_pre_mix_w, d_scale1, d_shift1 = inproj_bwd(
        dqkv, dxbc, dz, ddt, xs_, dx1, w_in_t16, pre_mix_w, scale1, shift1)
    g_w_in = _unperm_in(jnp.concatenate(
        [matmul_tn(h1, dqkv, "grad_w_in_qkv", D_MODEL, QKV_W), matmul_tn(h1, dxbc, "grad_w_in_xbc", D_MODEL, XBC_WIDTH),
         matmul_tn(h1, dz, "grad_w_in_z", D_MODEL, SSM_WIDTH), matmul_tn(h1, ddt, "grad_w_in_dt", D_MODEL, DT_PAD)],
        axis=1))

    d_mod = jnp.concatenate([d_shift1, d_scale1, d_gate1, d_shift2, d_scale2, d_gate2], axis=1)
    late = ("w_in", "ssm_conv_w", "ffn_conv_w")
    full = [_cols_to_blocks(g_w_in), _cols_to_blocks(d_ssm_cw[:SSM_CONV]), _cols_to_blocks(d_ffn_cw[:FFN_CONV])]
    core = lax.axis_index("c").astype(jnp.int32).reshape(1)
    got = pair_exchange(full, "pair_grads")
    chip_sums = [pair_sum(f_, g_, core, "pair_sum_" + k) for k, f_, g_ in zip(late, full, got)]
    chip_parts = all_to_all(chip_sums, "scatter_grads", CHIP_FLIPS, _chip_index)
    (d_mod_rows,) = all_to_all([d_mod.reshape(N_DEV, 1, n_cols)], "scatter_dmod")
    g_w_ada = ada_bwd(c_all, d_mod_rows.reshape(N_DEV, n_cols))

    row_g = dict(b_ada=d_mod, pre_mix_w=d_pre_mix_w, attn_sinks=dsinks, ssm_conv_b=d_ssm_cb, ssm_dt_bias=d_dt_bias,
                 ssm_a_log=d_a_log, ssm_d=d_dskip, ssm_norm_w=d_norm_w, post_mix_w=d_post_mix_w,
                 pre_ffn_w=d_pre_ffn_w, ffn_conv_b=d_ffn_cb, post_ffn_w=d_post_ffn_w)
    row = jnp.concatenate([loss_part] + [row_g[k] for k, _, _ in ROW_PARAMS], axis=1)
    row_all, rb_all = all_gather([row, d_rel_bias], "gather_small")

    wmv = lambda k: (weights[k], mom_m[k], mom_v[k])
    small = adamw_small(row_all, rb_all, wmv("rel_bias"), [wmv(k) for k, _, _ in ROW_PARAMS])
    loss = small[0][0, 0]
    res = {k: tuple(small[1 + 4 * i:5 + 4 * i]) for i, k in enumerate(["rel_bias"] + [k for k, _, _ in ROW_PARAMS])}
    big = list(zip(late, chip_parts)) + [("w_down", p_w_down), ("w_up", p_w_up), ("w_out", p_w_out),
                                        ("w_ada", g_w_ada[None])]
    for k, parts in big:
        res[k] = tuple(o[None] for o in adamw(parts, weights[k][0], mom_m[k][0], mom_v[k][0], "adamw_" + k))

    outs = [loss, grad_x[None]]
    for field in range(4):
        outs += [res[k][field] for k in order]
    return tuple(outs)
```

```python
import math

import numpy as np
import jax
import jax.numpy as jnp
from jax import lax
from jax.experimental import pallas as pl
from jax.experimental.pallas import tpu as pltpu

F32 = jnp.float32
BF16 = jnp.bfloat16
MESH_ID = pl.DeviceIdType.MESH

N_DEV = 8
D_MODEL = 1024
N_Q_HEADS = 8
N_KV_HEADS = 2
HEAD_DIM = 64
ATTN_WIDTH = 512
KV_WIDTH = 128
ATTN_BLOCK = 128
N_BUCKETS = 32
REL_MAX_DIST = 128
SSM_HEADS = 8
SSM_HEAD_DIM = 64
SSM_WIDTH = 512
SSM_STATE = 128
SSM_GROUPS = 2
SSM_BC = 256
SSM_CONV = 4
SSM_CHUNK = 256
XBC_WIDTH = SSM_WIDTH + 2 * SSM_BC
D_FF = 2816
FFN_CONV = 3
NORM_EPS = 1e-6
N_MOD = 6
IN_PROJ_WIDTH = 2312
QKV_W = ATTN_WIDTH + 2 * KV_WIDTH
OFF_XBC = QKV_W
OFF_Z = OFF_XBC + XBC_WIDTH
OFF_DT = OFF_Z + SSM_WIDTH
DT_PAD = 128
PROJ_PAD = OFF_DT + DT_PAD
FF_CHUNK = 1408

ADAM_LR = 0.001
ADAM_B1 = 0.9
ADAM_B2 = 0.999
ADAM_EPS = 1e-08
ADAM_WD = 0.01
ADAM_STEP = 10

TOKEN_TILE = 256
HALO = 8
NEXT = 16
VMEM_LIMIT = 56 * 1024 * 1024


def _params(vmem=VMEM_LIMIT, n_axes=1):
    return pltpu.CompilerParams(dimension_semantics=("arbitrary",) * n_axes, vmem_limit_bytes=vmem)


def _b(x):
    return x.astype(BF16)


def _nn(a, b):
    return jnp.dot(_b(a), _b(b), preferred_element_type=F32)


def _nt(a, b):
    return lax.dot_general(_b(a), _b(b), (((1,), (1,)), ((), ())), preferred_element_type=F32)


def _tn(a, b):
    return lax.dot_general(_b(a), _b(b), (((0,), (0,)), ((), ())), preferred_element_type=F32)


@jax.custom_vjp
def mm(a, b):
    return _nn(a, b)


mm.defvjp(lambda a, b: (_nn(a, b), (a, b)),
          lambda r, g: (_nt(g, r[1]).astype(r[0].dtype), _tn(r[0], g).astype(r[1].dtype)))


@jax.custom_vjp
def mm_nt(a, b):
    return _nt(a, b)


mm_nt.defvjp(lambda a, b: (_nt(a, b), (a, b)),
             lambda r, g: (_nn(g, r[1]).astype(r[0].dtype), _tn(g, r[0]).astype(r[1].dtype)))


@jax.custom_vjp
def mm_tn(a, b):
    return _tn(a, b)


mm_tn.defvjp(lambda a, b: (_tn(a, b), (a, b)),
             lambda r, g: (_nt(r[1], g).astype(r[0].dtype), _nn(r[0], g).astype(r[1].dtype)))


def _rms(x, w):
    return x * lax.rsqrt(jnp.mean(x * x, axis=-1, keepdims=True) + NORM_EPS) * w


def _norm_mod(x, w, scale, shift):
    return _rms(x, w) * (1.0 + scale) + shift


def _rms_bwd(x, w, dy):
    r = lax.rsqrt(jnp.mean(x * x, axis=-1, keepdims=True) + NORM_EPS)
    xhat = x * r
    g = dy * w
    dx = r * (g - xhat * jnp.mean(g * xhat, axis=-1, keepdims=True))
    return dx, jnp.sum(dy * xhat, axis=0, keepdims=True)


def _norm_mod_bwd(x, w, scale, dh):
    dx, da = _rms_bwd(x, w * (1.0 + scale), dh)
    return dx, da * (1.0 + scale), da * w, jnp.sum(dh, axis=0, keepdims=True)


def _gated_rms_bwd(m, gate, w, dy):
    dm, t = _rms_bwd(m, w * gate, dy)
    return dm, t * w, t * gate


def _silu(x):
    return x * jax.nn.sigmoid(x)


def _conv_rows(xin, w, k):
    acc = xin * w[k - 1:k, :]
    for j in range(1, k):
        acc = acc + pltpu.roll(xin, j, axis=0) * w[k - 1 - j:k - j, :]
    return acc


def _conv_rows_t(du, w, k):
    n = du.shape[0]
    acc = du * w[k - 1:k, :]
    for j in range(1, k):
        acc = acc + pltpu.roll(du, n - j, axis=0) * w[k - 1 - j:k - j, :]
    return acc


def _row(i):
    return (i, 0)


def _const(i):
    return (0, 0)


def _vec(n):
    return pl.BlockSpec((1, n), _const)


def _block_index(p):
    return 4 * p[0] + 2 * p[1] + p[2]


def all_gather(arrs, name):
    n = len(arrs)

    def body(*refs):
        ins, outs = refs[:n], refs[n:2 * n]
        send_sems, recv_sems, local_sems = refs[2 * n:]
        x, y, c = lax.axis_index("x"), lax.axis_index("y"), lax.axis_index("c")
        me, sibling = (x, y, c), (x, y, 1 - c)
        chips = [(1 - x, y), (x, 1 - y), (1 - x, 1 - y)]

        def copy(a, k, block, to, src=None):
            dst = outs[a].at[_block_index(block)]
            return pltpu.make_async_remote_copy(
                src_ref=dst if src is None else src, dst_ref=dst,
                send_sem=send_sems.at[a * 7 + k], recv_sem=recv_sems.at[a * 7 + k],
                device_id=to, device_id_type=MESH_ID)

        mine = [pltpu.make_async_copy(ins[a], outs[a].at[_block_index(me)], local_sems.at[a]) for a in range(n)]
        for cp in mine:
            cp.start()
        first = []
        for a in range(n):
            first.append(copy(a, 0, me, sibling, src=ins[a]))
            first += [copy(a, 1 + j, me, (*chip, c), src=ins[a]) for j, chip in enumerate(chips)]
        for cp in first:
            cp.start()
        passed = []
        for j, chip in enumerate(chips):
            for a in range(n):
                copy(a, 1 + j, (*chip, c), me).wait_recv()
                cp = copy(a, 4 + j, (*chip, c), sibling)
                cp.start()
                passed.append(cp)
        for a in range(n):
            copy(a, 0, sibling, me).wait_recv()
            for j, chip in enumerate(chips):
                copy(a, 4 + j, (*chip, 1 - c), me).wait_recv()
        for cp in first + passed:
            cp.wait_send()
        for cp in mine:
            cp.wait()

    any_spec = pl.BlockSpec(memory_space=pl.ANY)
    return pl.pallas_call(
        body, name=name,
        out_shape=[jax.ShapeDtypeStruct((N_DEV,) + a.shape, a.dtype) for a in arrs],
        in_specs=[any_spec] * n, out_specs=[any_spec] * n,
        scratch_shapes=[pltpu.SemaphoreType.DMA((7 * n,)), pltpu.SemaphoreType.DMA((7 * n,)),
                        pltpu.SemaphoreType.DMA((n,))],
    )(*arrs)


ALL_FLIPS = ((0, 0, 1), (0, 1, 0), (0, 1, 1), (1, 0, 0), (1, 0, 1), (1, 1, 0), (1, 1, 1))
CHIP_FLIPS = ((0, 1, 0), (1, 0, 0), (1, 1, 0))


def _chip_index(p):
    return 2 * p[0] + p[1]


def all_to_all(arrs, name, flips=ALL_FLIPS, index=_block_index):
    n = len(arrs)
    nf = len(flips)

    def body(*refs):
        ins, outs = refs[:n], refs[n:2 * n]
        send_sems, recv_sems, local_sems = refs[2 * n:]
        pos = (lax.axis_index("x"), lax.axis_index("y"), lax.axis_index("c"))
        me = index(pos)
        peers = [tuple(1 - p if f else p for p, f in zip(pos, flip)) for flip in flips]

        def copy(a, k):
            peer = peers[k]
            return pltpu.make_async_remote_copy(
                src_ref=ins[a].at[index(peer)], dst_ref=outs[a].at[me],
                send_sem=send_sems.at[a * nf + k], recv_sem=recv_sems.at[a * nf + k],
                device_id=peer, device_id_type=MESH_ID)

        def landed(a, k):
            slot = outs[a].at[index(peers[k])]
            return pltpu.make_async_remote_copy(
                src_ref=slot, dst_ref=slot,
                send_sem=send_sems.at[a * nf + k], recv_sem=recv_sems.at[a * nf + k],
                device_id=peers[k], device_id_type=MESH_ID)

        mine = [pltpu.make_async_copy(ins[a].at[me], outs[a].at[me], local_sems.at[a]) for a in range(n)]
        for cp in mine:
            cp.start()
        sent = [copy(a, k) for a in range(n) for k in range(nf)]
        for cp in sent:
            cp.start()
        for a in range(n):
            for k in range(nf):
                landed(a, k).wait_recv()
        for cp in sent:
            cp.wait_send()
        for cp in mine:
            cp.wait()

    any_spec = pl.BlockSpec(memory_space=pl.ANY)
    return pl.pallas_call(
        body, name=name,
        out_shape=[jax.ShapeDtypeStruct(a.shape, a.dtype) for a in arrs],
        in_specs=[any_spec] * n, out_specs=[any_spec] * n,
        scratch_shapes=[pltpu.SemaphoreType.DMA((nf * n,)), pltpu.SemaphoreType.DMA((nf * n,)),
                        pltpu.SemaphoreType.DMA((n,))],
    )(*arrs)


def _direct_exchange(src, dst, sems, scatter):
    send_sems, recv_sems, local_sem = sems
    pos = (lax.axis_index("x"), lax.axis_index("y"), lax.axis_index("c"))
    me = _block_index(pos)
    peers = [tuple(1 - p if f else p for p, f in zip(pos, flip)) for flip in ALL_FLIPS]

    def outgoing(k):
        return pltpu.make_async_remote_copy(
            src_ref=src.at[_block_index(peers[k])] if scatter else src, dst_ref=dst.at[me],
            send_sem=send_sems.at[k], recv_sem=recv_sems.at[k], device_id=peers[k], device_id_type=MESH_ID)

    def incoming(k):
        slot = dst.at[_block_index(peers[k])]
        return pltpu.make_async_remote_copy(
            src_ref=slot, dst_ref=slot, send_sem=send_sems.at[k], recv_sem=recv_sems.at[k],
            device_id=peers[k], device_id_type=MESH_ID)

    def local():
        return pltpu.make_async_copy(src.at[me] if scatter else src, dst.at[me], local_sem)

    def start():
        local().start()
        for k in range(len(ALL_FLIPS)):
            outgoing(k).start()

    def finish():
        for k in range(len(ALL_FLIPS)):
            incoming(k).wait_recv()
        for k in range(len(ALL_FLIPS)):
            outgoing(k).wait_send()
        local().wait()

    return start, finish


def hosted_call(body, exchanges, steps, n_in, n_out, **call):
    n_ex = len(exchanges)

    def wrapped(*refs):
        ins, srcs = refs[:n_in], refs[n_in:n_in + n_ex]
        outs = refs[n_in + n_ex:n_in + n_ex + n_out]
        dsts = refs[n_in + n_ex + n_out:n_in + 2 * n_ex + n_out]
        rest = refs[n_in + 2 * n_ex + n_out:]
        scratch, sems = rest[:len(rest) - 3 * n_ex], rest[len(rest) - 3 * n_ex:]
        plans = [_direct_exchange(srcs[e], dsts[e], sems[3 * e:3 * e + 3], exchanges[e][1]) for e in range(n_ex)]

        @pl.when(pl.program_id(0) == 0)
        def _():
            for start, _ in plans:
                start()

        body(*ins, *outs, *scratch)

        @pl.when(pl.program_id(0) == steps - 1)
        def _():
            for _, finish in plans:
                finish()

    any_spec = pl.BlockSpec(memory_space=pl.ANY)
    landings = [jax.ShapeDtypeStruct(src.shape if scatter else (N_DEV,) + src.shape, src.dtype)
                for src, scatter in exchanges]
    n_flips = len(ALL_FLIPS)
    sems = [pltpu.SemaphoreType.DMA((n_flips,)), pltpu.SemaphoreType.DMA((n_flips,)), pltpu.SemaphoreType.DMA(())]
    return pl.pallas_call(
        wrapped, grid=(steps,),
        in_specs=list(call.pop("in_specs")) + [any_spec] * n_ex,
        out_specs=list(call.pop("out_specs")) + [any_spec] * n_ex,
        out_shape=list(call.pop("out_shape")) + landings,
        scratch_shapes=list(call.pop("scratch_shapes", [])) + sems * n_ex,
        **call)


def _grid_call(body, steps, args, exchanges, **call):
    if not exchanges:
        return pl.pallas_call(body, grid=(steps,), **call)(*args)
    srcs = [src for src, _ in exchanges]
    return hosted_call(body, exchanges, steps, len(args), len(call["out_shape"]), **call)(*args, *srcs)


N_CHIPS = 4


def pair_exchange(arrs, name):
    n = len(arrs)

    def body(*refs):
        ins, outs = refs[:n], refs[n:2 * n]
        send_sems, recv_sems = refs[2 * n:]
        x, y, c = lax.axis_index("x"), lax.axis_index("y"), lax.axis_index("c")
        sibling = (x, y, 1 - c)
        sent = []
        for a in range(n):
            for q in range(N_CHIPS):
                cp = pltpu.make_async_remote_copy(
                    src_ref=ins[a].at[2 * q + (1 - c)], dst_ref=outs[a].at[q],
                    send_sem=send_sems.at[a * N_CHIPS + q], recv_sem=recv_sems.at[a * N_CHIPS + q],
                    device_id=sibling, device_id_type=MESH_ID)
                cp.start()
                sent.append(cp)
        for cp in sent:
            cp.wait_recv()
        for cp in sent:
            cp.wait_send()

    any_spec = pl.BlockSpec(memory_space=pl.ANY)
    return pl.pallas_call(
        body, name=name,
        out_shape=[jax.ShapeDtypeStruct((N_CHIPS,) + a.shape[1:], a.dtype) for a in arrs],
        in_specs=[any_spec] * n, out_specs=[any_spec] * n,
        scratch_shapes=[pltpu.SemaphoreType.DMA((N_CHIPS * n,)), pltpu.SemaphoreType.DMA((N_CHIPS * n,))],
    )(*arrs)


def pair_sum(full, got, core, name):
    _, r, n = full.shape
    tr = _row_tile(r)

    def body(c_ref, mine_ref, got_ref, o_ref):
        o_ref[...] = _b(mine_ref[...] + got_ref[...])

    grid_spec = pltpu.PrefetchScalarGridSpec(
        num_scalar_prefetch=1, grid=(N_CHIPS, r // tr),
        in_specs=[pl.BlockSpec((1, tr, n), lambda q, i, c_ref: (2 * q + c_ref[0], i, 0)),
                  pl.BlockSpec((1, tr, n), lambda q, i, c_ref: (q, i, 0))],
        out_specs=pl.BlockSpec((1, tr, n), lambda q, i, c_ref: (q, i, 0)))
    return pl.pallas_call(body, name=name, grid_spec=grid_spec,
                          out_shape=jax.ShapeDtypeStruct((N_CHIPS, r, n), BF16),
                          compiler_params=_params(n_axes=2))(core, full, got)


def _row_tile(r):
    for cand in (256, 128, 64, 32, 16):
        if r % cand == 0 and r > cand:
            return cand
    return r


def ada_fwd(c_all, w_ada, b_cols):
    def body(c_ref, w_ref, b_ref, o_ref):
        o_ref[...] = _nn(_silu(c_ref[...]), w_ref[...]) + b_ref[...]

    return pl.pallas_call(body, name="ada_fwd",
                          out_shape=jax.ShapeDtypeStruct((N_DEV, w_ada.shape[1]), F32),
                          compiler_params=_params(n_axes=0))(c_all, w_ada, b_cols)


def ada_bwd(c_all, g_cols):
    def body(c_ref, g_ref, o_ref):
        o_ref[...] = _tn(_silu(c_ref[...]), g_ref[...])

    return pl.pallas_call(body, name="ada_bwd",
                          out_shape=jax.ShapeDtypeStruct((c_all.shape[1], g_cols.shape[1]), F32),
                          compiler_params=_params(n_axes=0))(c_all, g_cols)


def matmul_tn(a, b, name, bm, bn, tk=512):
    s, m = a.shape
    n = b.shape[1]
    tk = min(tk, s)

    def body(a_ref, b_ref, o_ref):
        @pl.when(pl.program_id(2) == 0)
        def _():
            o_ref[...] = jnp.zeros_like(o_ref)

        o_ref[...] += _tn(a_ref[...], b_ref[...])

    return pl.pallas_call(
        body, name=name, grid=(m // bm, n // bn, s // tk),
        in_specs=[pl.BlockSpec((tk, bm), lambda i, j, k: (k, i)), pl.BlockSpec((tk, bn), lambda i, j, k: (k, j))],
        out_specs=pl.BlockSpec((bm, bn), lambda i, j, k: (i, j)),
        out_shape=jax.ShapeDtypeStruct((m, n), F32),
        compiler_params=_params(n_axes=3))(a, b)


def pre_mix_inproj(x, w, scale, shift, w_in16, exchange=None, tm=TOKEN_TILE):
    s = x.shape[0]

    def body(x_ref, w_ref, sc_ref, sh_ref, win_ref, h_ref, qkv_ref, xbc_ref, z_ref, dt_ref):
        h16 = _b(_norm_mod(x_ref[...], w_ref[...], sc_ref[...], sh_ref[...]))
        h_ref[...] = h16
        dot = lambda lo, hi: jnp.dot(h16, win_ref[:, lo:hi], preferred_element_type=F32)
        qkv_ref[...] = _b(dot(0, OFF_XBC))
        xbc_ref[...] = dot(OFF_XBC, OFF_Z)
        z_ref[...] = dot(OFF_Z, OFF_DT)
        dt_ref[...] = dot(OFF_DT, PROJ_PAD)

    tile = lambda n: pl.BlockSpec((tm, n), _row)
    return _grid_call(
        body, s // tm, (x, w, scale, shift, w_in16), exchange, name="pre_mix_inproj",
        in_specs=[tile(D_MODEL), _vec(D_MODEL), _vec(D_MODEL), _vec(D_MODEL), pl.BlockSpec((D_MODEL, PROJ_PAD), _const)],
        out_specs=[tile(D_MODEL), tile(QKV_W), tile(XBC_WIDTH), tile(SSM_WIDTH), tile(DT_PAD)],
        out_shape=[jax.ShapeDtypeStruct((s, D_MODEL), BF16), jax.ShapeDtypeStruct((s, QKV_W), BF16),
                   jax.ShapeDtypeStruct((s, XBC_WIDTH), F32), jax.ShapeDtypeStruct((s, SSM_WIDTH), F32),
                   jax.ShapeDtypeStruct((s, DT_PAD), F32)],
        compiler_params=_params())


ATTN_QB = 2


def _attn_tile(q, kp, kc, vp, vc, bias, sinks):
    lq = ATTN_BLOCK
    group = N_Q_HEADS // N_KV_HEADS
    lanes = lax.broadcasted_iota(jnp.int32, (1, 128), 1)
    rid = lax.broadcasted_iota(jnp.int32, (group * lq, 1), 0)
    sink_cols = []
    for hk in range(N_KV_HEADS):
        sink = jnp.zeros((group * lq, 1), F32)
        for g in range(group):
            s_h = jnp.sum(jnp.where(lanes == hk * group + g, sinks, 0.0), axis=-1, keepdims=True)
            sink = jnp.where((rid >= g * lq) & (rid < (g + 1) * lq), s_h, sink)
        sink_cols.append(sink)
    kall = jnp.concatenate([kp, kc], axis=0)
    vall = jnp.concatenate([vp, vc], axis=0)
    blocks = []
    for b in range(ATTN_QB):
        qb = q[b * lq:(b + 1) * lq]
        outs = []
        for hk in range(N_KV_HEADS):
            cols = slice(hk * HEAD_DIM, (hk + 1) * HEAD_DIM)
            kb = kall[b * lq:(b + 2) * lq, cols]
            vb = vall[b * lq:(b + 2) * lq, cols]
            qg = jnp.concatenate([qb[:, (hk * group + g) * HEAD_DIM:(hk * group + g + 1) * HEAD_DIM]
                                  for g in range(group)], axis=0)
            sc = mm_nt(qg, kb) * (HEAD_DIM ** -0.5) + bias[b][hk]
            sink = sink_cols[hk]
            m = lax.stop_gradient(jnp.maximum(jnp.max(sc, axis=-1, keepdims=True), sink))
            p = jnp.exp(sc - m)
            probs = p / (jnp.sum(p, axis=-1, keepdims=True) + jnp.exp(sink - m))
            og = mm(probs, vb)
            outs += [og[g * lq:(g + 1) * lq] for g in range(group)]
        blocks.append(jnp.concatenate(outs, axis=1))
    return jnp.concatenate(blocks, axis=0)


def _attn_in_specs(nt, clamp):
    lq, tq = ATTN_BLOCK, ATTN_BLOCK * ATTN_QB
    cur = lambda n: jnp.minimum(n, nt - 1) if clamp else n
    prev = lambda n: jnp.maximum(cur(n) * ATTN_QB - 1, 0)
    kcol, vcol = ATTN_WIDTH // KV_WIDTH, ATTN_WIDTH // KV_WIDTH + 1
    return [pl.BlockSpec((tq, ATTN_WIDTH), lambda n: (cur(n), 0)),
            pl.BlockSpec((lq, KV_WIDTH), lambda n: (prev(n), kcol)),
            pl.BlockSpec((tq, KV_WIDTH), lambda n: (cur(n), kcol)),
            pl.BlockSpec((lq, KV_WIDTH), lambda n: (prev(n), vcol)),
            pl.BlockSpec((tq, KV_WIDTH), lambda n: (cur(n), vcol)),
            pl.BlockSpec((2, N_KV_HEADS, 4 * lq, 2 * lq), lambda n: (0, 0, 0, 0)),
            _vec(128)]


def _tile_bias(bias_ref, first):
    return [[jnp.where(first, bias_ref[1, hk], bias_ref[0, hk]) if b == 0 else bias_ref[0, hk]
             for hk in range(N_KV_HEADS)] for b in range(ATTN_QB)]


def attn_fwd(qkv, bias, sinks_rows, exchange=None):
    s = qkv.shape[0]
    tq = ATTN_BLOCK * ATTN_QB
    nt = s // tq

    def body(q_ref, kp_ref, kc_ref, vp_ref, vc_ref, bias_ref, sk_ref, o_ref):
        f = lambda r: r[...].astype(F32)
        o = _attn_tile(f(q_ref), f(kp_ref), f(kc_ref), f(vp_ref), f(vc_ref),
                       _tile_bias(bias_ref, pl.program_id(0) == 0), sk_ref[...])
        o_ref[...] = _b(o)

    return _grid_call(
        body, nt, (qkv, qkv, qkv, qkv, qkv, bias, sinks_rows), exchange, name="attn_fwd",
        in_specs=_attn_in_specs(nt, False),
        out_specs=[pl.BlockSpec((tq, ATTN_WIDTH), _row)],
        out_shape=[jax.ShapeDtypeStruct((s, ATTN_WIDTH), BF16)],
        compiler_params=_params())


def attn_bwd(qkv, bias, sinks_rows, d_attn, exchange=None):
    s = qkv.shape[0]
    lq, tq = ATTN_BLOCK, ATTN_BLOCK * ATTN_QB
    nt = s // tq

    def body(q_ref, kp_ref, kc_ref, vp_ref, vc_ref, bias_ref, sk_ref, do_ref,
             dq_ref, dk_ref, dv_ref, dbias_ref, dsk_ref, carry_k, carry_v):
        n = pl.program_id(0)

        @pl.when(n == 0)
        def _():
            dbias_ref[...] = jnp.zeros_like(dbias_ref)
            dsk_ref[...] = jnp.zeros_like(dsk_ref)
            carry_k[...] = jnp.zeros_like(carry_k)
            carry_v[...] = jnp.zeros_like(carry_v)

        @pl.when(n < nt)
        def _():
            f = lambda r: r[...].astype(F32)
            _, vjp = jax.vjp(_attn_tile, f(q_ref), f(kp_ref), f(kc_ref), f(vp_ref), f(vc_ref),
                             _tile_bias(bias_ref, n == 0), sk_ref[...])
            dq, dkp, dkc, dvp, dvc, dbias, dsk = vjp(f(do_ref))
            dq_ref[...] = _b(dq)
            done = tq - lq
            dk_ref[:done, :] = _b(carry_k[:done, :])
            dv_ref[:done, :] = _b(carry_v[:done, :])
            dk_ref[done:, :] = _b(carry_k[done:, :] + dkp)
            dv_ref[done:, :] = _b(carry_v[done:, :] + dvp)
            carry_k[...] = dkc
            carry_v[...] = dvc
            dsk_ref[...] += dsk
            first = (n == 0).astype(F32)
            for hk in range(N_KV_HEADS):
                total = dbias[0][hk]
                for b in range(1, ATTN_QB):
                    total = total + dbias[b][hk]
                dbias_ref[0, hk] += total - first * dbias[0][hk]
                dbias_ref[1, hk] += first * dbias[0][hk]

        @pl.when(n == nt)
        def _():
            dk_ref[...] = _b(carry_k[...])
            dv_ref[...] = _b(carry_v[...])

    cur = lambda n: (jnp.minimum(n, nt - 1), 0)
    done_map = lambda n: (jnp.maximum(n - 1, 0), 0)
    return _grid_call(
        body, nt + 1, (qkv, qkv, qkv, qkv, qkv, bias, sinks_rows, d_attn), exchange, name="attn_bwd",
        in_specs=_attn_in_specs(nt, True) + [pl.BlockSpec((tq, ATTN_WIDTH), cur)],
        out_specs=[pl.BlockSpec((tq, ATTN_WIDTH), cur), pl.BlockSpec((tq, KV_WIDTH), done_map),
                   pl.BlockSpec((tq, KV_WIDTH), done_map),
                   pl.BlockSpec((2, N_KV_HEADS, 4 * lq, 2 * lq), lambda n: (0, 0, 0, 0)), _vec(128)],
        out_shape=[jax.ShapeDtypeStruct((s, ATTN_WIDTH), BF16), jax.ShapeDtypeStruct((s, KV_WIDTH), BF16),
                   jax.ShapeDtypeStruct((s, KV_WIDTH), BF16),
                   jax.ShapeDtypeStruct((2, N_KV_HEADS, 4 * lq, 2 * lq), F32), jax.ShapeDtypeStruct((1, 128), F32)],
        scratch_shapes=[pltpu.VMEM((tq, KV_WIDTH), F32), pltpu.VMEM((tq, KV_WIDTH), F32)],
        compiler_params=_params())


def rel_bias_table(rel_bias, bucket):
    lq = ATTN_BLOCK
    group = N_Q_HEADS // N_KV_HEADS

    def body(rb_ref, bk_ref, o_ref):
        bk = bk_ref[...]
        prev_keys = lax.broadcasted_iota(jnp.int32, bk.shape, 1) < lq
        accs = [jnp.full(bk.shape, -1e30, F32) for _ in range(N_Q_HEADS)]
        for b in range(N_BUCKETS):
            hit = bk == b
            accs = [jnp.where(hit, rb_ref[b, h], acc) for h, acc in enumerate(accs)]
        for h in range(N_Q_HEADS):
            rows = slice((h % group) * lq, (h % group + 1) * lq)
            o_ref[0, h // group, rows, :] = accs[h]
            o_ref[1, h // group, rows, :] = jnp.where(prev_keys, -1e30, accs[h])

    return pl.pallas_call(
        body, name="rel_bias_table",
        in_specs=[pl.BlockSpec(memory_space=pltpu.SMEM), pl.BlockSpec(memory_space=pltpu.VMEM)],
        out_shape=jax.ShapeDtypeStruct((2, N_KV_HEADS, group * lq, 2 * lq), F32),
        compiler_params=_params(n_axes=0))(rel_bias, bucket)


def rel_bias_grad(dbias, bucket):
    lq = ATTN_BLOCK
    group = N_Q_HEADS // N_KV_HEADS

    def body(db_ref, bk_ref, o_ref):
        rows = lax.broadcasted_iota(jnp.int32, (N_BUCKETS, 128), 0)
        lanes = lax.broadcasted_iota(jnp.int32, (N_BUCKETS, 128), 1)
        bk = bk_ref[...]
        per_head = []
        for h in range(N_Q_HEADS):
            sl = slice((h % group) * lq, (h % group + 1) * lq)
            per_head.append(db_ref[0, h // group, sl, :] + db_ref[1, h // group, sl, :])

        def per_bucket(b, acc):
            hit = (bk == b).astype(F32)
            for h in range(N_Q_HEADS):
                val = jnp.sum(per_head[h] * hit, keepdims=True)
                acc = acc + jnp.where((rows == b) & (lanes == h), val, 0.0)
            return acc

        o_ref[...] = lax.fori_loop(0, N_BUCKETS, per_bucket, jnp.zeros((N_BUCKETS, 128), F32))

    return pl.pallas_call(body, name="rel_bias_grad", out_shape=jax.ShapeDtypeStruct((N_BUCKETS, 128), F32),
                          compiler_params=_params(n_axes=0))(dbias, bucket)


def _tri_sum(a, upper):
    n = a.shape[0]
    ri = lax.broadcasted_iota(jnp.int32, (n, n), 0)
    ci = lax.broadcasted_iota(jnp.int32, (n, n), 1)
    tri = ((ri <= ci) if upper else (ri >= ci)).astype(BF16)
    hi = a.astype(BF16)
    rest = a - hi.astype(F32)
    mid = rest.astype(BF16)
    lo = (rest - mid.astype(F32)).astype(BF16)
    dot = lambda part: jnp.dot(tri, part, preferred_element_type=F32)
    return dot(hi) + dot(mid) + dot(lo)


@jax.custom_vjp
def _cumsum_rows(a):
    return _tri_sum(a, False)


_cumsum_rows.defvjp(lambda a: (_tri_sum(a, False), None), lambda _, g: (_tri_sum(g, True),))


def _ssm_core(u, z, dt_raw, hprev, dt_bias, a_log, dskip, norm_w):
    lc = u.shape[0]
    xbc = _silu(u)
    xs, bm, cm = xbc[:, :SSM_WIDTH], xbc[:, SSM_WIDTH:SSM_WIDTH + SSM_BC], xbc[:, SSM_WIDTH + SSM_BC:]
    dt = jax.nn.softplus(dt_raw + dt_bias)
    adt = dt * (-jnp.exp(a_log))
    ri = lax.broadcasted_iota(jnp.int32, (lc, lc), 0)
    ci = lax.broadcasted_iota(jnp.int32, (lc, lc), 1)
    causal = ri >= ci
    acum = _cumsum_rows(adt)
    acum_t = acum.T
    last = acum[lc - 1:lc, :]
    per_group = SSM_HEADS // SSM_GROUPS
    lane = lax.broadcasted_iota(jnp.int32, (1, 128), 1)
    rowid = lax.broadcasted_iota(jnp.int32, (128, 1), 0)
    lo_lanes = lane < SSM_HEAD_DIM
    ys, hs = [], []
    for g in range(SSM_GROUPS):
        bg = bm[:, g * SSM_STATE:(g + 1) * SSM_STATE]
        cg = cm[:, g * SSM_STATE:(g + 1) * SSM_STATE]
        cb = mm_nt(cg, bg)
        for pp in range(per_group // 2):
            ha = g * per_group + 2 * pp
            xp = xs[:, ha * SSM_HEAD_DIM:(ha + 2) * SSM_HEAD_DIM]
            hp = hprev[ha * SSM_HEAD_DIM:(ha + 2) * SSM_HEAD_DIM, :]
            xcp = xp * jnp.where(lo_lanes, dt[:, ha:ha + 1], dt[:, ha + 1:ha + 2])
            y_h, st_h = [], []
            for h in (ha, ha + 1):
                col, rowv, lasth = acum[:, h:h + 1], acum_t[h:h + 1, :], last[:, h:h + 1]
                decay = jnp.exp(jnp.where(causal, col - rowv, -1e30))
                y_h.append(mm(cb * decay, xcp) + mm_nt(cg * jnp.exp(col), hp))
                st_h.append(mm_tn(xcp, bg * jnp.exp(lasth - col)))
            y_pair = jnp.where(lo_lanes, y_h[0], y_h[1])
            st_pair = jnp.where(rowid < SSM_HEAD_DIM, st_h[0], st_h[1])
            la, lb = last[:, ha:ha + 1], last[:, ha + 1:ha + 2]
            hs.append(jnp.exp(jnp.where(rowid < SSM_HEAD_DIM, la, lb)) * hp + st_pair)
            dsk = jnp.where(lo_lanes, dskip[:, ha:ha + 1], dskip[:, ha + 1:ha + 2])
            ys.append(y_pair + dsk * xp)
    y = jnp.concatenate(ys, axis=1) * _silu(z)
    gw = SSM_WIDTH // SSM_GROUPS
    outs = []
    for g in range(SSM_GROUPS):
        yg = y[:, g * gw:(g + 1) * gw]
        outs.append(yg * lax.rsqrt(jnp.mean(yg * yg, axis=-1, keepdims=True) + NORM_EPS))
    return jnp.concatenate(outs, axis=1) * norm_w, jnp.concatenate(hs, axis=0)


def _ssm_param_specs():
    return [pl.BlockSpec((SSM_CONV, XBC_WIDTH), _const), _vec(XBC_WIDTH), _vec(128), _vec(128), _vec(128),
            _vec(SSM_WIDTH)]


def ssm_fwd(xbc_raw, z, dt_raw, conv_w, conv_b, dt_bias, a_log, dskip, norm_w, exchange=None):
    s = xbc_raw.shape[0]
    lc = SSM_CHUNK
    nc = s // lc
    hrows = SSM_HEADS * SSM_HEAD_DIM

    def body(x_ref, halo_ref, z_ref, dt_ref, cw_ref, cb_ref, dtb_ref, al_ref, dk_ref, nw_ref,
             o_ref, hp_ref, state):
        i = pl.program_id(0)

        @pl.when(i == 0)
        def _():
            state[...] = jnp.zeros_like(state)

        halo = halo_ref[...] * (i > 0).astype(F32)
        xin = jnp.concatenate([halo, x_ref[...]], axis=0)
        u = (_conv_rows(xin, cw_ref[...], SSM_CONV) + cb_ref[...])[HALO:]
        hprev = state[...]
        hp_ref[...] = hprev
        out, hnew = _ssm_core(u, z_ref[...], dt_ref[...], hprev, dtb_ref[...], al_ref[...], dk_ref[...], nw_ref[...])
        o_ref[...] = _b(out)
        state[...] = hnew

    tile = lambda n: pl.BlockSpec((lc, n), _row)
    halo_spec = pl.BlockSpec((HALO, XBC_WIDTH), lambda i: (jnp.maximum(i * (lc // HALO) - 1, 0), 0))
    return _grid_call(
        body, nc, (xbc_raw, xbc_raw, z, dt_raw, conv_w, conv_b, dt_bias, a_log, dskip, norm_w), exchange,
        name="ssm_fwd",
        in_specs=[tile(XBC_WIDTH), halo_spec, tile(SSM_WIDTH), tile(DT_PAD)] + _ssm_param_specs(),
        out_specs=[tile(SSM_WIDTH), pl.BlockSpec((hrows, SSM_STATE), _row)],
        out_shape=[jax.ShapeDtypeStruct((s, SSM_WIDTH), BF16), jax.ShapeDtypeStruct((nc * hrows, SSM_STATE), F32)],
        scratch_shapes=[pltpu.VMEM((hrows, SSM_STATE), F32)],
        compiler_params=_params())


def ssm_bwd(xbc_raw, z, dt_raw, hprev_all, d_out, conv_w, conv_b, dt_bias, a_log, dskip, norm_w, exchange=None):
    s = xbc_raw.shape[0]
    lc = SSM_CHUNK
    nc = s // lc
    hrows = SSM_HEADS * SSM_HEAD_DIM

    def body(x_ref, halo_ref, z_ref, dt_ref, hp_ref, do_ref, cw_ref, cb_ref, dtb_ref, al_ref, dk_ref, nw_ref,
             dx_ref, dz_ref, ddt_ref, dcw_ref, dcb_ref, ddtb_ref, dal_ref, ddk_ref, dnw_ref, dstate, du_next):
        i = pl.program_id(0)
        chunk = nc - 1 - i

        @pl.when(i == 0)
        def _():
            dstate[...] = jnp.zeros_like(dstate)
            du_next[...] = jnp.zeros_like(du_next)
            for r in (dcw_ref, dcb_ref, ddtb_ref, dal_ref, ddk_ref, dnw_ref):
                r[...] = jnp.zeros_like(r)

        halo = halo_ref[...] * (chunk > 0).astype(F32)
        xin = jnp.concatenate([halo, x_ref[...]], axis=0)
        cw = cw_ref[...]
        u = (_conv_rows(xin, cw, SSM_CONV) + cb_ref[...])[HALO:]
        _, vjp = jax.vjp(_ssm_core, u, z_ref[...], dt_ref[...], hp_ref[...], dtb_ref[...], al_ref[...],
                         dk_ref[...], nw_ref[...])
        du, dz, ddt, dhp, ddtb, dal, ddk, dnw = vjp((do_ref[...], dstate[...]))
        dstate[...] = dhp
        dz_ref[...] = dz
        ddt_ref[...] = ddt
        du_ext = jnp.concatenate([du, du_next[...]], axis=0)
        dx_ref[...] = _conv_rows_t(du_ext, cw, SSM_CONV)[:lc]
        du_next[...] = du[:HALO]
        rows = [jnp.sum(du * pltpu.roll(xin, j, axis=0)[HALO:] if j else du * xin[HALO:], axis=0, keepdims=True)
                for j in range(SSM_CONV)]
        dcw_ref[...] += jnp.concatenate(rows[::-1] + [jnp.zeros((8 - SSM_CONV, XBC_WIDTH), F32)], axis=0)
        dcb_ref[...] += jnp.sum(du, axis=0, keepdims=True)
        ddtb_ref[...] += ddtb
        dal_ref[...] += dal
        ddk_ref[...] += ddk
        dnw_ref[...] += dnw

    rev = lambda i: (nc - 1 - i, 0)
    tile = lambda n: pl.BlockSpec((lc, n), rev)
    halo_spec = pl.BlockSpec((HALO, XBC_WIDTH), lambda i: (jnp.maximum((nc - 1 - i) * (lc // HALO) - 1, 0), 0))
    acc = lambda r, n: pl.BlockSpec((r, n), _const)
    return _grid_call(
        body, nc, (xbc_raw, xbc_raw, z, dt_raw, hprev_all, d_out, conv_w, conv_b, dt_bias, a_log, dskip, norm_w),
        exchange, name="ssm_bwd",
        in_specs=[tile(XBC_WIDTH), halo_spec, tile(SSM_WIDTH), tile(DT_PAD), pl.BlockSpec((hrows, SSM_STATE), rev),
                  tile(SSM_WIDTH)] + _ssm_param_specs(),
        out_specs=[tile(XBC_WIDTH), tile(SSM_WIDTH), tile(DT_PAD), acc(8, XBC_WIDTH), acc(1, XBC_WIDTH),
                   acc(1, 128), acc(1, 128), acc(1, 128), acc(1, SSM_WIDTH)],
        out_shape=[jax.ShapeDtypeStruct((s, XBC_WIDTH), F32), jax.ShapeDtypeStruct((s, SSM_WIDTH), F32),
                   jax.ShapeDtypeStruct((s, DT_PAD), F32), jax.ShapeDtypeStruct((8, XBC_WIDTH), F32),
                   jax.ShapeDtypeStruct((1, XBC_WIDTH), F32), jax.ShapeDtypeStruct((1, 128), F32),
                   jax.ShapeDtypeStruct((1, 128), F32), jax.ShapeDtypeStruct((1, 128), F32),
                   jax.ShapeDtypeStruct((1, SSM_WIDTH), F32)],
        scratch_shapes=[pltpu.VMEM((hrows, SSM_STATE), F32), pltpu.VMEM((HALO, XBC_WIDTH), F32)],
        compiler_params=_params())


def mix_out(attn, ssm, x, w_out16, gate1, post_mix_w, pre_ffn_w, scale2, shift2, tm=TOKEN_TILE):
    s = x.shape[0]

    def body(a_ref, s_ref, x_ref, w_ref, g_ref, pw_ref, fw_ref, sc_ref, sh_ref, mixed_ref, x1_ref, h2_ref):
        mixed = (jnp.dot(a_ref[...], w_ref[:ATTN_WIDTH, :], preferred_element_type=F32)
                 + jnp.dot(s_ref[...], w_ref[ATTN_WIDTH:, :], preferred_element_type=F32))
        mixed_ref[...] = mixed
        x1 = x_ref[...] + g_ref[...] * _rms(mixed, pw_ref[...])
        x1_ref[...] = x1
        h2_ref[...] = _b(_norm_mod(x1, fw_ref[...], sc_ref[...], sh_ref[...]))

    tile = lambda n: pl.BlockSpec((tm, n), _row)
    return pl.pallas_call(
        body, name="mix_out", grid=(s // tm,),
        in_specs=[tile(ATTN_WIDTH), tile(SSM_WIDTH), tile(D_MODEL), pl.BlockSpec((D_MODEL, D_MODEL), _const)]
        + [_vec(D_MODEL)] * 5,
        out_specs=[tile(D_MODEL)] * 3,
        out_shape=[jax.ShapeDtypeStruct((s, D_MODEL), F32), jax.ShapeDtypeStruct((s, D_MODEL), F32),
                   jax.ShapeDtypeStruct((s, D_MODEL), BF16)],
        compiler_params=_params())(attn, ssm, x, w_out16, gate1, post_mix_w, pre_ffn_w, scale2, shift2)


def _gate(ug, uv):
    return jax.nn.gelu(ug, approximate=True) * uv


def _gate_bwd(ug, uv, df):
    k0, k1 = math.sqrt(2.0 / math.pi), 0.044715
    sq = ug * ug
    t = jnp.tanh(k0 * ug * (1.0 + k1 * sq))
    half = 0.5 * (1.0 + t)
    slope = half + (0.5 * k0) * ug * (1.0 - t * t) * (1.0 + (3.0 * k1) * sq)
    return df * uv * slope, df * (ug * half)


def _resident(shape):
    return pl.BlockSpec(shape, _const, pipeline_mode=pl.Buffered(1))


def up_gate(h2, w_up16, conv_w, conv_b, tm=TOKEN_TILE):
    s = h2.shape[0]

    def body(h_ref, halo_ref, w_ref, cw_ref, cb_ref, u_ref, uraw_ref, f_ref):
        halo = halo_ref[...]
        halo = jnp.where(pl.program_id(0) > 0, halo, jnp.zeros_like(halo))
        hin = jnp.concatenate([halo, h_ref[...]], axis=0)
        for lo in range(0, D_FF, FF_CHUNK):
            halves = []
            for base in (lo, D_FF + lo):
                cols = slice(base, base + FF_CHUNK)
                uraw = jnp.dot(hin, w_ref[:, cols], preferred_element_type=F32)
                uraw_ref[:, cols] = _b(uraw[NEXT:])
                u = (_conv_rows(uraw, cw_ref[:, cols], FFN_CONV) + cb_ref[:, cols])[NEXT:]
                u_ref[:, cols] = u
                halves.append(u)
            f_ref[:, lo:lo + FF_CHUNK] = _b(_gate(*halves))

    tile = lambda n: pl.BlockSpec((tm, n), _row)
    halo_spec = pl.BlockSpec((NEXT, D_MODEL), lambda i: (jnp.maximum(i * (tm // NEXT) - 1, 0), 0))
    return pl.pallas_call(
        body, name="up_gate", grid=(s // tm,),
        in_specs=[tile(D_MODEL), halo_spec, _resident((D_MODEL, 2 * D_FF)),
                  pl.BlockSpec((FFN_CONV, 2 * D_FF), _const), _vec(2 * D_FF)],
        out_specs=[tile(2 * D_FF), tile(2 * D_FF), tile(D_FF)],
        out_shape=[jax.ShapeDtypeStruct((s, 2 * D_FF), F32), jax.ShapeDtypeStruct((s, 2 * D_FF), BF16),
                   jax.ShapeDtypeStruct((s, D_FF), BF16)],
        compiler_params=_params())(h2, h2, w_up16, conv_w, conv_b)


def down_loss(f16, w_down16, x1, target, gate2, post_ffn_w, tm=TOKEN_TILE):
    s = x1.shape[0]

    def body(f_ref, wd_ref, x1_ref, t_ref, g_ref, pw_ref, dffn_ref, dy_ref, loss_ref, dg_ref, dpw_ref, gw_ref):
        i = pl.program_id(0)

        @pl.when(i == 0)
        def _():
            loss_ref[...] = jnp.zeros_like(loss_ref)
            dg_ref[...] = jnp.zeros_like(dg_ref)
            dpw_ref[...] = jnp.zeros_like(dpw_ref)
            gw_ref[...] = jnp.zeros_like(gw_ref)

        ffn = jnp.dot(f_ref[...], wd_ref[...], preferred_element_type=F32)
        x1 = x1_ref[...]
        x2 = x1 + g_ref[...] * _rms(ffn, pw_ref[...])
        err = x2 - t_ref[...]
        dy = err * (1.0 / D_MODEL)
        dy_ref[...] = dy
        loss_ref[...] += 0.5 * jnp.sum(jnp.mean(err * err, axis=-1, keepdims=True))
        dffn, dg, dpw = _gated_rms_bwd(ffn, g_ref[...], pw_ref[...], dy)
        dffn16 = _b(dffn)
        dffn_ref[...] = dffn16
        dg_ref[...] += dg
        dpw_ref[...] += dpw
        gw_ref[...] += _tn(f_ref[...], dffn16)

    tile = lambda n: pl.BlockSpec((tm, n), _row)
    return pl.pallas_call(
        body, name="down_loss", grid=(s // tm,),
        in_specs=[tile(D_FF), _resident((D_FF, D_MODEL)), tile(D_MODEL), tile(D_MODEL), _vec(D_MODEL), _vec(D_MODEL)],
        out_specs=[tile(D_MODEL), tile(D_MODEL), _vec(128), _vec(D_MODEL), _vec(D_MODEL),
                   pl.BlockSpec((D_FF, D_MODEL), _const)],
        out_shape=[jax.ShapeDtypeStruct((s, D_MODEL), BF16), jax.ShapeDtypeStruct((s, D_MODEL), F32),
                   jax.ShapeDtypeStruct((1, 128), F32), jax.ShapeDtypeStruct((1, D_MODEL), F32),
                   jax.ShapeDtypeStruct((1, D_MODEL), F32), jax.ShapeDtypeStruct((D_FF, D_MODEL), F32)],
        compiler_params=_params())(f16, w_down16, x1, target, gate2, post_ffn_w)


BWD_CHUNK = 256


def ffn_bwd(u, u_raw16, d_ffn, conv_w, w_down_t16, w_up_t16, tm=TOKEN_TILE):
    s = u.shape[0]
    nt = s // tm

    def body(u_ref, unext_ref, uraw_ref, d_ref, dnext_ref, cw_ref, wdt_ref, wut_ref,
             du_ref, dh_ref, dcw_ref, dcb_ref):
        i = pl.program_id(0)

        @pl.when(i == 0)
        def _():
            dcw_ref[...] = jnp.zeros_like(dcw_ref)
            dcb_ref[...] = jnp.zeros_like(dcb_ref)

        dnext = dnext_ref[...]
        dnext = jnp.where(i < nt - 1, dnext, jnp.zeros_like(dnext))
        dff = jnp.concatenate([d_ref[...], dnext], axis=0)
        rows_ext = tm + NEXT
        for lo in range(0, D_FF, BWD_CHUNK):
            gcols, vcols = slice(lo, lo + BWD_CHUNK), slice(D_FF + lo, D_FF + lo + BWD_CHUNK)
            ug = jnp.concatenate([u_ref[:, gcols], unext_ref[:, gcols]], axis=0)
            uv = jnp.concatenate([u_ref[:, vcols], unext_ref[:, vcols]], axis=0)
            df = jnp.dot(dff, wdt_ref[:, gcols], preferred_element_type=F32)
            for cols, du in zip((gcols, vcols), _gate_bwd(ug, uv, df)):
                cw = cw_ref[:, cols]
                du1 = pltpu.roll(du, rows_ext - 1, axis=0)
                du2 = pltpu.roll(du, rows_ext - 2, axis=0)
                du_ref[:, cols] = _b((du * cw[2:3, :] + du1 * cw[1:2, :] + du2 * cw[0:1, :])[:tm])
                xr = uraw_ref[:, cols].astype(F32)
                rows = [jnp.sum(xr * d_[:tm], axis=0, keepdims=True) for d_ in (du2, du1, du)]
                dcw_ref[:, cols] += jnp.concatenate(rows + [jnp.zeros((8 - FFN_CONV, BWD_CHUNK), F32)], axis=0)
                dcb_ref[:, cols] += jnp.sum(du[:tm], axis=0, keepdims=True)
        dh_ref[...] = jnp.dot(du_ref[...], wut_ref[...], preferred_element_type=F32)

    tile = lambda n: pl.BlockSpec((tm, n), _row)
    nxt = lambda i: (jnp.minimum((i + 1) * (tm // NEXT), s // NEXT - 1), 0)
    return pl.pallas_call(
        body, name="ffn_bwd", grid=(nt,),
        in_specs=[tile(2 * D_FF), pl.BlockSpec((NEXT, 2 * D_FF), nxt), tile(2 * D_FF), tile(D_MODEL),
                  pl.BlockSpec((NEXT, D_MODEL), nxt), pl.BlockSpec((FFN_CONV, 2 * D_FF), _const),
                  _resident((D_MODEL, D_FF)), _resident((2 * D_FF, D_MODEL))],
        out_specs=[tile(2 * D_FF), tile(D_MODEL), pl.BlockSpec((8, 2 * D_FF), _const), _vec(2 * D_FF)],
        out_shape=[jax.ShapeDtypeStruct((s, 2 * D_FF), BF16), jax.ShapeDtypeStruct((s, D_MODEL), F32),
                   jax.ShapeDtypeStruct((8, 2 * D_FF), F32), jax.ShapeDtypeStruct((1, 2 * D_FF), F32)],
        compiler_params=_params())(u, u, u_raw16, d_ffn, d_ffn, conv_w, w_down_t16, w_up_t16)


def mix_bwd(dh2, x1, dy, mixed, attn, ssm, w_out_t16, pre_ffn_w, scale2, gate1, post_mix_w, tm=TOKEN_TILE):
    s = x1.shape[0]

    def body(dh_ref, x1_ref, dy_ref, mx_ref, a_ref, s_ref, w_ref, fw_ref, sc_ref, g_ref, pw_ref,
             dx1_ref, da_ref, ds_ref, dfw_ref, dsc_ref, dsh_ref, dg_ref, dpw_ref, gw_ref):
        accs = (dfw_ref, dsc_ref, dsh_ref, dg_ref, dpw_ref)

        @pl.when(pl.program_id(0) == 0)
        def _():
            for r in accs + (gw_ref,):
                r[...] = jnp.zeros_like(r)

        dx1, dfw, dsc, dsh = _norm_mod_bwd(x1_ref[...], fw_ref[...], sc_ref[...], dh_ref[...])
        dx1 = dx1 + dy_ref[...]
        dx1_ref[...] = dx1
        dmixed, dg, dpw = _gated_rms_bwd(mx_ref[...], g_ref[...], pw_ref[...], dx1)
        dm16 = _b(dmixed)
        dmix_in = jnp.dot(dm16, w_ref[...], preferred_element_type=F32)
        da_ref[...] = _b(dmix_in[:, :ATTN_WIDTH])
        ds_ref[...] = dmix_in[:, ATTN_WIDTH:]
        gw_ref[:ATTN_WIDTH, :] += _tn(a_ref[...], dm16)
        gw_ref[ATTN_WIDTH:, :] += _tn(s_ref[...], dm16)
        for r, v in zip(accs, (dfw, dsc, dsh, dg, dpw)):
            r[...] += v

    tile = lambda n: pl.BlockSpec((tm, n), _row)
    return pl.pallas_call(
        body, name="mix_bwd", grid=(s // tm,),
        in_specs=[tile(D_MODEL)] * 4 + [tile(ATTN_WIDTH), tile(SSM_WIDTH), _resident((D_MODEL, D_MODEL))]
        + [_vec(D_MODEL)] * 4,
        out_specs=[tile(D_MODEL), tile(ATTN_WIDTH), tile(SSM_WIDTH)] + [_vec(D_MODEL)] * 5
        + [pl.BlockSpec((D_MODEL, D_MODEL), _const)],
        out_shape=[jax.ShapeDtypeStruct((s, D_MODEL), F32), jax.ShapeDtypeStruct((s, ATTN_WIDTH), BF16),
                   jax.ShapeDtypeStruct((s, SSM_WIDTH), F32)]
        + [jax.ShapeDtypeStruct((1, D_MODEL), F32)] * 5 + [jax.ShapeDtypeStruct((D_MODEL, D_MODEL), F32)],
        compiler_params=_params())(dh2, x1, dy, mixed, attn, ssm, w_out_t16, pre_ffn_w, scale2, gate1, post_mix_w)


INPROJ_BWD_TILE = 512


def inproj_bwd(dqkv, dxbc, dz, ddt, x, dx1, h1, w_in_t16, pre_mix_w, scale1, tm=INPROJ_BWD_TILE):
    s = x.shape[0]
    tm = min(tm, s)

    def body(dq_ref, dxbc_ref, dz_ref, ddt_ref, x_ref, dx1_ref, h_ref, w_ref, pw_ref, sc_ref,
             gx_ref, dpw_ref, dsc_ref, dsh_ref, gw_ref):
        accs = (dpw_ref, dsc_ref, dsh_ref)

        @pl.when(pl.program_id(0) == 0)
        def _():
            for r in accs + (gw_ref,):
                r[...] = jnp.zeros_like(r)

        h16 = h_ref[...]
        dh = None
        for r, lo, hi in ((dq_ref, 0, OFF_XBC), (dxbc_ref, OFF_XBC, OFF_Z), (dz_ref, OFF_Z, OFF_DT),
                          (ddt_ref, OFF_DT, PROJ_PAD)):
            d16 = _b(r[...])
            part = jnp.dot(d16, w_ref[lo:hi, :], preferred_element_type=F32)
            dh = part if dh is None else dh + part
            gw_ref[:, lo:hi] += _tn(h16, d16)
        dx, dpw, dsc, dsh = _norm_mod_bwd(x_ref[...], pw_ref[...], sc_ref[...], dh)
        gx_ref[...] = dx1_ref[...] + dx
        for r, v in zip(accs, (dpw, dsc, dsh)):
            r[...] += v

    tile = lambda n: pl.BlockSpec((tm, n), _row)
    return pl.pallas_call(
        body, name="inproj_bwd", grid=(s // tm,),
        in_specs=[tile(QKV_W), tile(XBC_WIDTH), tile(SSM_WIDTH), tile(DT_PAD), tile(D_MODEL), tile(D_MODEL),
                  tile(D_MODEL), _resident((PROJ_PAD, D_MODEL))] + [_vec(D_MODEL)] * 2,
        out_specs=[tile(D_MODEL)] + [_vec(D_MODEL)] * 3 + [pl.BlockSpec((D_MODEL, PROJ_PAD), _const)],
        out_shape=[jax.ShapeDtypeStruct((s, D_MODEL), F32)] + [jax.ShapeDtypeStruct((1, D_MODEL), F32)] * 3
        + [jax.ShapeDtypeStruct((D_MODEL, PROJ_PAD), F32)],
        compiler_params=_params())(dqkv, dxbc, dz, ddt, x, dx1, h1, w_in_t16, pre_mix_w, scale1)


def _adam(g, w, m, v):
    new_m = ADAM_B1 * m + (1.0 - ADAM_B1) * g
    new_v = ADAM_B2 * v + (1.0 - ADAM_B2) * jnp.square(g)
    m_hat = new_m / (1.0 - ADAM_B1 ** ADAM_STEP)
    v_hat = new_v / (1.0 - ADAM_B2 ** ADAM_STEP)
    return -ADAM_LR * (m_hat / (jnp.sqrt(v_hat) + ADAM_EPS) + ADAM_WD * w), new_m, new_v


ROW_PARAMS = (("b_ada", 6144, 6144), ("pre_mix_w", 1024, 1024), ("attn_sinks", 128, 8), ("ssm_conv_b", 1024, 1024),
              ("ssm_dt_bias", 128, 8), ("ssm_a_log", 128, 8), ("ssm_d", 128, 8), ("ssm_norm_w", 512, 512),
              ("post_mix_w", 1024, 1024), ("pre_ffn_w", 1024, 1024), ("ffn_conv_b", 5632, 5632),
              ("post_ffn_w", 1024, 1024))
LOSS_LANES = 128


def adamw_small(row_all, rb_all, rel_bias_wmv, row_wmv):
    n_rows = len(ROW_PARAMS)

    def body(*refs):
        row_ref, rb_ref = refs[:2]
        wmv = refs[2:5 + 3 * n_rows]
        outs = refs[5 + 3 * n_rows:]
        g_row, g_rb = row_ref[0], rb_ref[0]
        for k in range(1, N_DEV):
            g_row = g_row + row_ref[k]
            g_rb = g_rb + rb_ref[k]
        outs[0][...] = g_row[:, :LOSS_LANES]
        grads = [g_rb[:, :N_Q_HEADS]]
        off = LOSS_LANES
        for _, lanes, width in ROW_PARAMS:
            grads.append(g_row[:, off:off + width])
            off += lanes
        for i, g in enumerate(grads):
            w_ref, m_ref, v_ref = wmv[3 * i:3 * i + 3]
            g_out, d_out, m_out, v_out = outs[1 + 4 * i:5 + 4 * i]
            g_out[...] = g
            d_out[...], m_out[...], v_out[...] = _adam(g, w_ref[...], m_ref[...], v_ref[...])

    flat_wmv = list(rel_bias_wmv) + [a for wmv in row_wmv for a in wmv]
    shapes = [jax.ShapeDtypeStruct((1, LOSS_LANES), F32)] + [jax.ShapeDtypeStruct((N_BUCKETS, N_Q_HEADS), F32)] * 4
    for _, _, width in ROW_PARAMS:
        shapes += [jax.ShapeDtypeStruct((1, width), F32)] * 4
    return pl.pallas_call(body, name="adamw_small", out_shape=shapes,
                          compiler_params=_params(n_axes=0))(row_all, rb_all, *flat_wmv)


def adamw(parts, w, m, v, name):
    p, r, n = parts.shape
    tr = _row_tile(r)

    def body(p_ref, w_ref, m_ref, v_ref, g_ref, d_ref, nm_ref, nv_ref):
        g = p_ref[0].astype(F32)
        for k in range(1, p):
            g = g + p_ref[k].astype(F32)
        g_ref[...] = g
        d_ref[...], nm_ref[...], nv_ref[...] = _adam(g, w_ref[...], m_ref[...], v_ref[...])

    tile = pl.BlockSpec((tr, n), _row)
    return pl.pallas_call(
        body, name=name, grid=(r // tr,),
        in_specs=[pl.BlockSpec((p, tr, n), lambda i: (0, i, 0)), tile, tile, tile],
        out_specs=[tile] * 4, out_shape=[jax.ShapeDtypeStruct((r, n), F32)] * 4,
        compiler_params=_params())(parts, w, m, v)


def _bucket_table():
    lq = ATTN_BLOCK
    qi = np.arange(lq)[:, None] + lq
    kj = np.arange(2 * lq)[None, :]
    dist = qi - kj
    d = np.maximum(dist, 0)
    max_exact = N_BUCKETS // 2
    nf = np.maximum(d, 1).astype(np.float32)
    large = max_exact + (np.log(nf / max_exact) / math.log(REL_MAX_DIST / max_exact)
                         * (N_BUCKETS - max_exact)).astype(np.int32)
    large = np.minimum(large, N_BUCKETS - 1)
    bucket = np.where(d < max_exact, d, large).astype(np.int32)
    in_band = (dist >= 0) & (dist < REL_MAX_DIST)
    return np.where(in_band, bucket, -1).astype(np.int32)


def _cols_from_blocks(g):
    return jnp.transpose(g, (1, 0, 2)).reshape(g.shape[1], N_DEV * g.shape[2])


def _cols_to_blocks(a):
    r, n = a.shape
    return jnp.transpose(a.reshape(r, N_DEV, n // N_DEV), (1, 0, 2))


def _perm_in(w):
    pad = jnp.zeros((w.shape[0], DT_PAD - SSM_HEADS), w.dtype)
    return jnp.concatenate([w[:, :768], w[:, 768:1280], w[:, 1792:2304], w[:, 1280:1792], w[:, 2304:2312], pad], axis=1)


def _unperm_in(g):
    return jnp.concatenate([g[:, :768], g[:, 768:1280], g[:, 1792:2304], g[:, 1280:1792], g[:, 2304:2312]], axis=1)


def _lane_pad(v, n=128):
    return jnp.pad(v, ((0, 0), (0, n - v.shape[1])))


def kernel(x, c, rel_bias, w_ada, b_ada, pre_mix_w, w_in, attn_sinks, ssm_conv_w, ssm_conv_b, ssm_dt_bias, ssm_a_log, ssm_d, ssm_norm_w, w_out, post_mix_w, pre_ffn_w, w_up, ffn_conv_w, ffn_conv_b, w_down, post_ffn_w, loss_target, m_rel_bias, m_w_ada, m_b_ada, m_pre_mix_w, m_w_in, m_attn_sinks, m_ssm_conv_w, m_ssm_conv_b, m_ssm_dt_bias, m_ssm_a_log, m_ssm_d, m_ssm_norm_w, m_w_out, m_post_mix_w, m_pre_ffn_w, m_w_up, m_ffn_conv_w, m_ffn_conv_b, m_w_down, m_post_ffn_w, v_rel_bias, v_w_ada, v_b_ada, v_pre_mix_w, v_w_in, v_attn_sinks, v_ssm_conv_w, v_ssm_conv_b, v_ssm_dt_bias, v_ssm_a_log, v_ssm_d, v_ssm_norm_w, v_w_out, v_post_mix_w, v_pre_ffn_w, v_w_up, v_ffn_conv_w, v_ffn_conv_b, v_w_down, v_post_ffn_w):
    weights = dict(rel_bias=rel_bias, w_ada=w_ada, b_ada=b_ada, pre_mix_w=pre_mix_w, w_in=w_in, attn_sinks=attn_sinks, ssm_conv_w=ssm_conv_w, ssm_conv_b=ssm_conv_b, ssm_dt_bias=ssm_dt_bias, ssm_a_log=ssm_a_log, ssm_d=ssm_d, ssm_norm_w=ssm_norm_w, w_out=w_out, post_mix_w=post_mix_w, pre_ffn_w=pre_ffn_w, w_up=w_up, ffn_conv_w=ffn_conv_w, ffn_conv_b=ffn_conv_b, w_down=w_down, post_ffn_w=post_ffn_w)
    mom_m = dict(rel_bias=m_rel_bias, w_ada=m_w_ada, b_ada=m_b_ada, pre_mix_w=m_pre_mix_w, w_in=m_w_in, attn_sinks=m_attn_sinks, ssm_conv_w=m_ssm_conv_w, ssm_conv_b=m_ssm_conv_b, ssm_dt_bias=m_ssm_dt_bias, ssm_a_log=m_ssm_a_log, ssm_d=m_ssm_d, ssm_norm_w=m_ssm_norm_w, w_out=m_w_out, post_mix_w=m_post_mix_w, pre_ffn_w=m_pre_ffn_w, w_up=m_w_up, ffn_conv_w=m_ffn_conv_w, ffn_conv_b=m_ffn_conv_b, w_down=m_w_down, post_ffn_w=m_post_ffn_w)
    mom_v = dict(rel_bias=v_rel_bias, w_ada=v_w_ada, b_ada=v_b_ada, pre_mix_w=v_pre_mix_w, w_in=v_w_in, attn_sinks=v_attn_sinks, ssm_conv_w=v_ssm_conv_w, ssm_conv_b=v_ssm_conv_b, ssm_dt_bias=v_ssm_dt_bias, ssm_a_log=v_ssm_a_log, ssm_d=v_ssm_d, ssm_norm_w=v_ssm_norm_w, w_out=v_w_out, post_mix_w=v_post_mix_w, pre_ffn_w=v_pre_ffn_w, w_up=v_w_up, ffn_conv_w=v_ffn_conv_w, ffn_conv_b=v_ffn_conv_b, w_down=v_w_down, post_ffn_w=v_post_ffn_w)
    order = ['rel_bias', 'w_ada', 'b_ada', 'pre_mix_w', 'w_in', 'attn_sinks', 'ssm_conv_w', 'ssm_conv_b', 'ssm_dt_bias', 'ssm_a_log', 'ssm_d', 'ssm_norm_w', 'w_out', 'post_mix_w', 'pre_ffn_w', 'w_up', 'ffn_conv_w', 'ffn_conv_b', 'w_down', 'post_ffn_w']

    me = 4 * lax.axis_index("x") + 2 * lax.axis_index("y") + lax.axis_index("c")
    xs_ = x[0]
    target = loss_target[0]

    (w_in_g, scw_g, fcw_g, c_g) = all_gather([_b(w_in[0]), ssm_conv_w[0], ffn_conv_w[0], c], "gather_weights")
    w_in16 = _perm_in(_cols_from_blocks(w_in_g))
    w_in_t16 = w_in16.T
    ssm_cw = _cols_from_blocks(scw_g)
    ffn_cw = _cols_from_blocks(fcw_g)
    c_all = c_g.reshape(N_DEV, D_MODEL)

    n_cols = w_ada.shape[2]
    b_cols = lax.dynamic_slice(b_ada, (0, me * n_cols), (1, n_cols))
    mod_part = ada_fwd(c_all, w_ada[0], b_cols)
    (mod_rows,) = all_to_all([mod_part.reshape(N_DEV, 1, n_cols)], "scatter_mod")
    mod = mod_rows.reshape(N_MOD, 1, D_MODEL)
    shift1, scale1, gate1, shift2, scale2, gate2 = (mod[i] for i in range(N_MOD))

    bucket_band = jnp.asarray(_bucket_table())
    bias = rel_bias_table(rel_bias, bucket_band)
    sinks_row = _lane_pad(attn_sinks)
    dt_bias, a_log, dskip = _lane_pad(ssm_dt_bias), _lane_pad(ssm_a_log), _lane_pad(ssm_d)

    h1, qkv, xbc_raw, z, dt_raw, w_out_g = pre_mix_inproj(
        xs_, pre_mix_w, scale1, shift1, w_in16, [(_b(w_out[0]), False)])
    attn, w_up_g = attn_fwd(qkv, bias, sinks_row, [(_b(w_up[0]), False)])
    ssm, hprev_all, w_down_g = ssm_fwd(xbc_raw, z, dt_raw, ssm_cw, ssm_conv_b, dt_bias, a_log, dskip, ssm_norm_w,
                                       [(_b(w_down[0]), False)])
    w_out16 = w_out_g.reshape(D_MODEL, D_MODEL)
    w_out_t16 = w_out16.T
    w_up16 = _cols_from_blocks(w_up_g)
    w_up_t16 = w_up16.T
    w_down16 = w_down_g.reshape(D_FF, D_MODEL)
    w_down_t16 = w_down16.T
    mixed, x1, h2 = mix_out(attn, ssm, xs_, w_out16, gate1, post_mix_w, pre_ffn_w, scale2, shift2)
    u, u_raw16, f16 = up_gate(h2, w_up16, ffn_cw, ffn_conv_b)
    d_ffn, dy, loss_part, d_gate2, d_post_ffn_w, g_w_down = down_loss(f16, w_down16, x1, target, gate2, post_ffn_w)

    du_raw, dh2, d_ffn_cw, d_ffn_cb = ffn_bwd(u, u_raw16, d_ffn, ffn_cw, w_down_t16, w_up_t16)
    g_w_up = matmul_tn(h2, du_raw, "grad_w_up", D_MODEL, FF_CHUNK)
    (dx1, d_attn, d_ssm, d_pre_ffn_w, d_scale2, d_shift2, d_gate1, d_post_mix_w, g_w_out) = mix_bwd(
        dh2, x1, dy, mixed, attn, ssm, w_out_t16, pre_ffn_w, scale2, gate1, post_mix_w)
    dq, dk, dv, dbias, dsinks, p_w_down = attn_bwd(
        qkv, bias, sinks_row, d_attn, [(g_w_down.reshape(N_DEV, D_FF // N_DEV, D_MODEL), True)])
    d_rel_bias = rel_bias_grad(dbias, bucket_band)
    (dxbc, dz, ddt, d_ssm_cw, d_ssm_cb, d_dt_bias, d_a_log, d_dskip, d_norm_w, p_w_up, p_w_out) = ssm_bwd(
        xbc_raw, z, dt_raw, hprev_all, d_ssm, ssm_cw, ssm_conv_b, dt_bias, a_log, dskip, ssm_norm_w,
        [(_cols_to_blocks(g_w_up), True), (g_w_out.reshape(N_DEV, D_MODEL // N_DEV, D_MODEL), True)])
    dqkv = jnp.concatenate([dq, dk, dv], axis=1)
    grad_x, d_pre_mix_w, d_scale1, d_shift1, g_w_in_perm = inproj_bwd(
        dqkv, dxbc, dz, ddt, xs_, dx1, h1, w_in_t16, pre_mix_w, scale1)
    g_w_in = _unperm_in(g_w_in_perm)

    d_mod = jnp.concatenate([d_shift1, d_scale1, d_gate1, d_shift2, d_scale2, d_gate2], axis=1)
    late = ("w_in", "ssm_conv_w", "ffn_conv_w")
    full = [_cols_to_blocks(g_w_in), _cols_to_blocks(d_ssm_cw[:SSM_CONV]), _cols_to_blocks(d_ffn_cw[:FFN_CONV])]
    core = lax.axis_index("c").astype(jnp.int32).reshape(1)
    got = pair_exchange(full, "pair_grads")
    chip_sums = [pair_sum(f_, g_, core, "pair_sum_" + k) for k, f_, g_ in zip(late, full, got)]
    chip_parts = all_to_all(chip_sums, "scatter_grads", CHIP_FLIPS, _chip_index)
    (d_mod_rows,) = all_to_all([d_mod.reshape(N_DEV, 1, n_cols)], "scatter_dmod")
    g_w_ada = ada_bwd(c_all, d_mod_rows.reshape(N_DEV, n_cols))

    row_g = dict(b_ada=d_mod, pre_mix_w=d_pre_mix_w, attn_sinks=dsinks, ssm_conv_b=d_ssm_cb, ssm_dt_bias=d_dt_bias,
                 ssm_a_log=d_a_log, ssm_d=d_dskip, ssm_norm_w=d_norm_w, post_mix_w=d_post_mix_w,
                 pre_ffn_w=d_pre_ffn_w, ffn_conv_b=d_ffn_cb, post_ffn_w=d_post_ffn_w)
    row = jnp.concatenate([loss_part] + [row_g[k] for k, _, _ in ROW_PARAMS], axis=1)
    row_all, rb_all = all_gather([row, d_rel_bias], "gather_small")

    wmv = lambda k: (weights[k], mom_m[k], mom_v[k])
    small = adamw_small(row_all, rb_all, wmv("rel_bias"), [wmv(k) for k, _, _ in ROW_PARAMS])
    loss = small[0][0, 0]
    res = {k: tuple(small[1 + 4 * i:5 + 4 * i]) for i, k in enumerate(["rel_bias"] + [k for k, _, _ in ROW_PARAMS])}
    big = list(zip(late, chip_parts)) + [("w_down", p_w_down), ("w_up", p_w_up), ("w_out", p_w_out),
                                        ("w_ada", g_w_ada[None])]
    for k, parts in big:
        res[k] = tuple(o[None] for o in adamw(parts, weights[k][0], mom_m[k][0], mom_v[k][0], "adamw_" + k))

    outs = [loss, grad_x[None]]
    for field in range(4):
        outs += [res[k][field] for k in order]
    return tuple(outs)
```

```python
import math

import numpy as np
import jax
import jax.numpy as jnp
from jax import lax
from jax.experimental import pallas as pl
from jax.experimental.pallas import tpu as pltpu

F32 = jnp.float32
BF16 = jnp.bfloat16
MESH_ID = pl.DeviceIdType.MESH

N_DEV = 8
D_MODEL = 1024
N_Q_HEADS = 8
N_KV_HEADS = 2
HEAD_DIM = 64
ATTN_WIDTH = 512
KV_WIDTH = 128
ATTN_BLOCK = 128
N_BUCKETS = 32
REL_MAX_DIST = 128
SSM_HEADS = 8
SSM_HEAD_DIM = 64
SSM_WIDTH = 512
SSM_STATE = 128
SSM_GROUPS = 2
SSM_BC = 256
SSM_CONV = 4
SSM_CHUNK = 256
XBC_WIDTH = SSM_WIDTH + 2 * SSM_BC
D_FF = 2816
FFN_CONV = 3
NORM_EPS = 1e-6
N_MOD = 6
IN_PROJ_WIDTH = 2312
QKV_W = ATTN_WIDTH + 2 * KV_WIDTH
OFF_XBC = QKV_W
OFF_Z = OFF_XBC + XBC_WIDTH
OFF_DT = OFF_Z + SSM_WIDTH
DT_PAD = 128
PROJ_PAD = OFF_DT + DT_PAD
FF_CHUNK = 1408

ADAM_LR = 0.001
ADAM_B1 = 0.9
ADAM_B2 = 0.999
ADAM_EPS = 1e-08
ADAM_WD = 0.01
ADAM_STEP = 10

TOKEN_TILE = 256
HALO = 8
NEXT = 16
VMEM_LIMIT = 56 * 1024 * 1024


def _params(vmem=VMEM_LIMIT, n_axes=1):
    return pltpu.CompilerParams(dimension_semantics=("arbitrary",) * n_axes, vmem_limit_bytes=vmem)


def _b(x):
    return x.astype(BF16)


def _nn(a, b):
    return jnp.dot(_b(a), _b(b), preferred_element_type=F32)


def _nt(a, b):
    return lax.dot_general(_b(a), _b(b), (((1,), (1,)), ((), ())), preferred_element_type=F32)


def _tn(a, b):
    return lax.dot_general(_b(a), _b(b), (((0,), (0,)), ((), ())), preferred_element_type=F32)


@jax.custom_vjp
def mm(a, b):
    return _nn(a, b)


mm.defvjp(lambda a, b: (_nn(a, b), (a, b)),
          lambda r, g: (_nt(g, r[1]).astype(r[0].dtype), _tn(r[0], g).astype(r[1].dtype)))


@jax.custom_vjp
def mm_nt(a, b):
    return _nt(a, b)


mm_nt.defvjp(lambda a, b: (_nt(a, b), (a, b)),
             lambda r, g: (_nn(g, r[1]).astype(r[0].dtype), _tn(g, r[0]).astype(r[1].dtype)))


@jax.custom_vjp
def mm_tn(a, b):
    return _tn(a, b)


mm_tn.defvjp(lambda a, b: (_tn(a, b), (a, b)),
             lambda r, g: (_nt(r[1], g).astype(r[0].dtype), _nn(r[0], g).astype(r[1].dtype)))


def _rms(x, w):
    return x * lax.rsqrt(jnp.mean(x * x, axis=-1, keepdims=True) + NORM_EPS) * w


def _norm_mod(x, w, scale, shift):
    return _rms(x, w) * (1.0 + scale) + shift


def _rms_bwd(x, w, dy):
    r = lax.rsqrt(jnp.mean(x * x, axis=-1, keepdims=True) + NORM_EPS)
    xhat = x * r
    g = dy * w
    dx = r * (g - xhat * jnp.mean(g * xhat, axis=-1, keepdims=True))
    return dx, jnp.sum(dy * xhat, axis=0, keepdims=True)


def _norm_mod_bwd(x, w, scale, dh):
    dx, da = _rms_bwd(x, w * (1.0 + scale), dh)
    return dx, da * (1.0 + scale), da * w, jnp.sum(dh, axis=0, keepdims=True)


def _gated_rms_bwd(m, gate, w, dy):
    dm, t = _rms_bwd(m, w * gate, dy)
    return dm, t * w, t * gate


def _silu(x):
    return x * jax.nn.sigmoid(x)


def _conv_rows(xin, w, k):
    acc = xin * w[k - 1:k, :]
    for j in range(1, k):
        acc = acc + pltpu.roll(xin, j, axis=0) * w[k - 1 - j:k - j, :]
    return acc


def _conv_rows_t(du, w, k):
    n = du.shape[0]
    acc = du * w[k - 1:k, :]
    for j in range(1, k):
        acc = acc + pltpu.roll(du, n - j, axis=0) * w[k - 1 - j:k - j, :]
    return acc


def _row(i):
    return (i, 0)


def _const(i):
    return (0, 0)


def _vec(n):
    return pl.BlockSpec((1, n), _const)


def _block_index(p):
    return 4 * p[0] + 2 * p[1] + p[2]


def all_gather(arrs, name):
    n = len(arrs)

    def body(*refs):
        ins, outs = refs[:n], refs[n:2 * n]
        send_sems, recv_sems, local_sems = refs[2 * n:]
        x, y, c = lax.axis_index("x"), lax.axis_index("y"), lax.axis_index("c")
        me, sibling = (x, y, c), (x, y, 1 - c)
        chips = [(1 - x, y), (x, 1 - y), (1 - x, 1 - y)]

        def copy(a, k, block, to, src=None):
            dst = outs[a].at[_block_index(block)]
            return pltpu.make_async_remote_copy(
                src_ref=dst if src is None else src, dst_ref=dst,
                send_sem=send_sems.at[a * 7 + k], recv_sem=recv_sems.at[a * 7 + k],
                device_id=to, device_id_type=MESH_ID)

        mine = [pltpu.make_async_copy(ins[a], outs[a].at[_block_index(me)], local_sems.at[a]) for a in range(n)]
        for cp in mine:
            cp.start()
        first = []
        for a in range(n):
            first.append(copy(a, 0, me, sibling, src=ins[a]))
            first += [copy(a, 1 + j, me, (*chip, c), src=ins[a]) for j, chip in enumerate(chips)]
        for cp in first:
            cp.start()
        passed = []
        for j, chip in enumerate(chips):
            for a in range(n):
                copy(a, 1 + j, (*chip, c), me).wait_recv()
                cp = copy(a, 4 + j, (*chip, c), sibling)
                cp.start()
                passed.append(cp)
        for a in range(n):
            copy(a, 0, sibling, me).wait_recv()
            for j, chip in enumerate(chips):
                copy(a, 4 + j, (*chip, 1 - c), me).wait_recv()
        for cp in first + passed:
            cp.wait_send()
        for cp in mine:
            cp.wait()

    any_spec = pl.BlockSpec(memory_space=pl.ANY)
    return pl.pallas_call(
        body, name=name,
        out_shape=[jax.ShapeDtypeStruct((N_DEV,) + a.shape, a.dtype) for a in arrs],
        in_specs=[any_spec] * n, out_specs=[any_spec] * n,
        scratch_shapes=[pltpu.SemaphoreType.DMA((7 * n,)), pltpu.SemaphoreType.DMA((7 * n,)),
                        pltpu.SemaphoreType.DMA((n,))],
    )(*arrs)


ALL_FLIPS = ((0, 0, 1), (0, 1, 0), (0, 1, 1), (1, 0, 0), (1, 0, 1), (1, 1, 0), (1, 1, 1))
CHIP_FLIPS = ((0, 1, 0), (1, 0, 0), (1, 1, 0))


def _chip_index(p):
    return 2 * p[0] + p[1]


def all_to_all(arrs, name, flips=ALL_FLIPS, index=_block_index):
    n = len(arrs)
    nf = len(flips)

    def body(*refs):
        ins, outs = refs[:n], refs[n:2 * n]
        send_sems, recv_sems, local_sems = refs[2 * n:]
        pos = (lax.axis_index("x"), lax.axis_index("y"), lax.axis_index("c"))
        me = index(pos)
        peers = [tuple(1 - p if f else p for p, f in zip(pos, flip)) for flip in flips]

        def copy(a, k):
            peer = peers[k]
            return pltpu.make_async_remote_copy(
                src_ref=ins[a].at[index(peer)], dst_ref=outs[a].at[me],
                send_sem=send_sems.at[a * nf + k], recv_sem=recv_sems.at[a * nf + k],
                device_id=peer, device_id_type=MESH_ID)

        def landed(a, k):
            slot = outs[a].at[index(peers[k])]
            return pltpu.make_async_remote_copy(
                src_ref=slot, dst_ref=slot,
                send_sem=send_sems.at[a * nf + k], recv_sem=recv_sems.at[a * nf + k],
                device_id=peers[k], device_id_type=MESH_ID)

        mine = [pltpu.make_async_copy(ins[a].at[me], outs[a].at[me], local_sems.at[a]) for a in range(n)]
        for cp in mine:
            cp.start()
        sent = [copy(a, k) for a in range(n) for k in range(nf)]
        for cp in sent:
            cp.start()
        for a in range(n):
            for k in range(nf):
                landed(a, k).wait_recv()
        for cp in sent:
            cp.wait_send()
        for cp in mine:
            cp.wait()

    any_spec = pl.BlockSpec(memory_space=pl.ANY)
    return pl.pallas_call(
        body, name=name,
        out_shape=[jax.ShapeDtypeStruct(a.shape, a.dtype) for a in arrs],
        in_specs=[any_spec] * n, out_specs=[any_spec] * n,
        scratch_shapes=[pltpu.SemaphoreType.DMA((nf * n,)), pltpu.SemaphoreType.DMA((nf * n,)),
                        pltpu.SemaphoreType.DMA((n,))],
    )(*arrs)


def _direct_exchange(src, dst, sems, scatter):
    send_sems, recv_sems, local_sem = sems
    pos = (lax.axis_index("x"), lax.axis_index("y"), lax.axis_index("c"))
    me = _block_index(pos)
    peers = [tuple(1 - p if f else p for p, f in zip(pos, flip)) for flip in ALL_FLIPS]

    def outgoing(k):
        return pltpu.make_async_remote_copy(
            src_ref=src.at[_block_index(peers[k])] if scatter else src, dst_ref=dst.at[me],
            send_sem=send_sems.at[k], recv_sem=recv_sems.at[k], device_id=peers[k], device_id_type=MESH_ID)

    def incoming(k):
        slot = dst.at[_block_index(peers[k])]
        return pltpu.make_async_remote_copy(
            src_ref=slot, dst_ref=slot, send_sem=send_sems.at[k], recv_sem=recv_sems.at[k],
            device_id=peers[k], device_id_type=MESH_ID)

    def local():
        return pltpu.make_async_copy(src.at[me] if scatter else src, dst.at[me], local_sem)

    def start():
        local().start()
        for k in range(len(ALL_FLIPS)):
            outgoing(k).start()

    def finish():
        for k in range(len(ALL_FLIPS)):
            incoming(k).wait_recv()
        for k in range(len(ALL_FLIPS)):
            outgoing(k).wait_send()
        local().wait()

    return start, finish


def hosted_call(body, exchanges, steps, n_in, n_out, **call):
    n_ex = len(exchanges)

    def wrapped(*refs):
        ins, srcs = refs[:n_in], refs[n_in:n_in + n_ex]
        outs = refs[n_in + n_ex:n_in + n_ex + n_out]
        dsts = refs[n_in + n_ex + n_out:n_in + 2 * n_ex + n_out]
        rest = refs[n_in + 2 * n_ex + n_out:]
        scratch, sems = rest[:len(rest) - 3 * n_ex], rest[len(rest) - 3 * n_ex:]
        plans = [_direct_exchange(srcs[e], dsts[e], sems[3 * e:3 * e + 3], exchanges[e][1]) for e in range(n_ex)]

        @pl.when(pl.program_id(0) == 0)
        def _():
            for start, _ in plans:
                start()

        body(*ins, *outs, *scratch)

        @pl.when(pl.program_id(0) == steps - 1)
        def _():
            for _, finish in plans:
                finish()

    any_spec = pl.BlockSpec(memory_space=pl.ANY)
    landings = [jax.ShapeDtypeStruct(src.shape if scatter else (N_DEV,) + src.shape, src.dtype)
                for src, scatter in exchanges]
    n_flips = len(ALL_FLIPS)
    sems = [pltpu.SemaphoreType.DMA((n_flips,)), pltpu.SemaphoreType.DMA((n_flips,)), pltpu.SemaphoreType.DMA(())]
    return pl.pallas_call(
        wrapped, grid=(steps,),
        in_specs=list(call.pop("in_specs")) + [any_spec] * n_ex,
        out_specs=list(call.pop("out_specs")) + [any_spec] * n_ex,
        out_shape=list(call.pop("out_shape")) + landings,
        scratch_shapes=list(call.pop("scratch_shapes", [])) + sems * n_ex,
        **call)


def _grid_call(body, steps, args, exchanges, **call):
    if not exchanges:
        return pl.pallas_call(body, grid=(steps,), **call)(*args)
    srcs = [src for src, _ in exchanges]
    return hosted_call(body, exchanges, steps, len(args), len(call["out_shape"]), **call)(*args, *srcs)


N_CHIPS = 4


def pair_exchange(arrs, name):
    n = len(arrs)

    def body(*refs):
        ins, outs = refs[:n], refs[n:2 * n]
        send_sems, recv_sems = refs[2 * n:]
        x, y, c = lax.axis_index("x"), lax.axis_index("y"), lax.axis_index("c")
        sibling = (x, y, 1 - c)
        sent = []
        for a in range(n):
            for q in range(N_CHIPS):
                cp = pltpu.make_async_remote_copy(
                    src_ref=ins[a].at[2 * q + (1 - c)], dst_ref=outs[a].at[q],
                    send_sem=send_sems.at[a * N_CHIPS + q], recv_sem=recv_sems.at[a * N_CHIPS + q],
                    device_id=sibling, device_id_type=MESH_ID)
                cp.start()
                sent.append(cp)
        for cp in sent:
            cp.wait_recv()
        for cp in sent:
            cp.wait_send()

    any_spec = pl.BlockSpec(memory_space=pl.ANY)
    return pl.pallas_call(
        body, name=name,
        out_shape=[jax.ShapeDtypeStruct((N_CHIPS,) + a.shape[1:], a.dtype) for a in arrs],
        in_specs=[any_spec] * n, out_specs=[any_spec] * n,
        scratch_shapes=[pltpu.SemaphoreType.DMA((N_CHIPS * n,)), pltpu.SemaphoreType.DMA((N_CHIPS * n,))],
    )(*arrs)


def pair_sum(full, got, core, name):
    _, r, n = full.shape
    tr = _row_tile(r)

    def body(c_ref, mine_ref, got_ref, o_ref):
        o_ref[...] = _b(mine_ref[...] + got_ref[...])

    grid_spec = pltpu.PrefetchScalarGridSpec(
        num_scalar_prefetch=1, grid=(N_CHIPS, r // tr),
        in_specs=[pl.BlockSpec((1, tr, n), lambda q, i, c_ref: (2 * q + c_ref[0], i, 0)),
                  pl.BlockSpec((1, tr, n), lambda q, i, c_ref: (q, i, 0))],
        out_specs=pl.BlockSpec((1, tr, n), lambda q, i, c_ref: (q, i, 0)))
    return pl.pallas_call(body, name=name, grid_spec=grid_spec,
                          out_shape=jax.ShapeDtypeStruct((N_CHIPS, r, n), BF16),
                          compiler_params=_params(n_axes=2))(core, full, got)


def _row_tile(r):
    for cand in (256, 128, 64, 32, 16):
        if r % cand == 0 and r > cand:
            return cand
    return r


def ada_fwd(c_all, w_ada, b_cols):
    def body(c_ref, w_ref, b_ref, o_ref):
        o_ref[...] = _nn(_silu(c_ref[...]), w_ref[...]) + b_ref[...]

    return pl.pallas_call(body, name="ada_fwd",
                          out_shape=jax.ShapeDtypeStruct((N_DEV, w_ada.shape[1]), F32),
                          compiler_params=_params(n_axes=0))(c_all, w_ada, b_cols)


def ada_bwd(c_all, g_cols):
    def body(c_ref, g_ref, o_ref):
        o_ref[...] = _tn(_silu(c_ref[...]), g_ref[...])

    return pl.pallas_call(body, name="ada_bwd",
                          out_shape=jax.ShapeDtypeStruct((c_all.shape[1], g_cols.shape[1]), F32),
                          compiler_params=_params(n_axes=0))(c_all, g_cols)


def matmul_tn(a, b, name, bm, bn, tk=512):
    s, m = a.shape
    n = b.shape[1]
    tk = min(tk, s)

    def body(a_ref, b_ref, o_ref):
        @pl.when(pl.program_id(2) == 0)
        def _():
            o_ref[...] = jnp.zeros_like(o_ref)

        o_ref[...] += _tn(a_ref[...], b_ref[...])

    return pl.pallas_call(
        body, name=name, grid=(m // bm, n // bn, s // tk),
        in_specs=[pl.BlockSpec((tk, bm), lambda i, j, k: (k, i)), pl.BlockSpec((tk, bn), lambda i, j, k: (k, j))],
        out_specs=pl.BlockSpec((bm, bn), lambda i, j, k: (i, j)),
        out_shape=jax.ShapeDtypeStruct((m, n), F32),
        compiler_params=_params(n_axes=3))(a, b)


def pre_mix_inproj(x, w, scale, shift, w_in16, exchange=None, tm=TOKEN_TILE):
    s = x.shape[0]

    def body(x_ref, w_ref, sc_ref, sh_ref, win_ref, h_ref, qkv_ref, xbc_ref, z_ref, dt_ref):
        h16 = _b(_norm_mod(x_ref[...], w_ref[...], sc_ref[...], sh_ref[...]))
        h_ref[...] = h16
        dot = lambda lo, hi: jnp.dot(h16, win_ref[:, lo:hi], preferred_element_type=F32)
        qkv_ref[...] = _b(dot(0, OFF_XBC))
        xbc_ref[...] = dot(OFF_XBC, OFF_Z)
        z_ref[...] = dot(OFF_Z, OFF_DT)
        dt_ref[...] = dot(OFF_DT, PROJ_PAD)

    tile = lambda n: pl.BlockSpec((tm, n), _row)
    return _grid_call(
        body, s // tm, (x, w, scale, shift, w_in16), exchange, name="pre_mix_inproj",
        in_specs=[tile(D_MODEL), _vec(D_MODEL), _vec(D_MODEL), _vec(D_MODEL), pl.BlockSpec((D_MODEL, PROJ_PAD), _const)],
        out_specs=[tile(D_MODEL), tile(QKV_W), tile(XBC_WIDTH), tile(SSM_WIDTH), tile(DT_PAD)],
        out_shape=[jax.ShapeDtypeStruct((s, D_MODEL), BF16), jax.ShapeDtypeStruct((s, QKV_W), BF16),
                   jax.ShapeDtypeStruct((s, XBC_WIDTH), F32), jax.ShapeDtypeStruct((s, SSM_WIDTH), F32),
                   jax.ShapeDtypeStruct((s, DT_PAD), F32)],
        compiler_params=_params())


ATTN_QB = 2


def _attn_tile(q, kp, kc, vp, vc, bias, sinks):
    lq = ATTN_BLOCK
    group = N_Q_HEADS // N_KV_HEADS
    lanes = lax.broadcasted_iota(jnp.int32, (1, 128), 1)
    rid = lax.broadcasted_iota(jnp.int32, (group * lq, 1), 0)
    sink_cols = []
    for hk in range(N_KV_HEADS):
        sink = jnp.zeros((group * lq, 1), F32)
        for g in range(group):
            s_h = jnp.sum(jnp.where(lanes == hk * group + g, sinks, 0.0), axis=-1, keepdims=True)
            sink = jnp.where((rid >= g * lq) & (rid < (g + 1) * lq), s_h, sink)
        sink_cols.append(sink)
    kall = jnp.concatenate([kp, kc], axis=0)
    vall = jnp.concatenate([vp, vc], axis=0)
    blocks = []
    for b in range(ATTN_QB):
        qb = q[b * lq:(b + 1) * lq]
        outs = []
        for hk in range(N_KV_HEADS):
            cols = slice(hk * HEAD_DIM, (hk + 1) * HEAD_DIM)
            kb = kall[b * lq:(b + 2) * lq, cols]
            vb = vall[b * lq:(b + 2) * lq, cols]
            qg = jnp.concatenate([qb[:, (hk * group + g) * HEAD_DIM:(hk * group + g + 1) * HEAD_DIM]
                                  for g in range(group)], axis=0)
            sc = mm_nt(qg, kb) * (HEAD_DIM ** -0.5) + bias[b][hk]
            sink = sink_cols[hk]
            m = lax.stop_gradient(jnp.maximum(jnp.max(sc, axis=-1, keepdims=True), sink))
            p = jnp.exp(sc - m)
            probs = p / (jnp.sum(p, axis=-1, keepdims=True) + jnp.exp(sink - m))
            og = mm(probs, vb)
            outs += [og[g * lq:(g + 1) * lq] for g in range(group)]
        blocks.append(jnp.concatenate(outs, axis=1))
    return jnp.concatenate(blocks, axis=0)


def _attn_in_specs(nt, clamp):
    lq, tq = ATTN_BLOCK, ATTN_BLOCK * ATTN_QB
    cur = lambda n: jnp.minimum(n, nt - 1) if clamp else n
    prev = lambda n: jnp.maximum(cur(n) * ATTN_QB - 1, 0)
    kcol, vcol = ATTN_WIDTH // KV_WIDTH, ATTN_WIDTH // KV_WIDTH + 1
    return [pl.BlockSpec((tq, ATTN_WIDTH), lambda n: (cur(n), 0)),
            pl.BlockSpec((lq, KV_WIDTH), lambda n: (prev(n), kcol)),
            pl.BlockSpec((tq, KV_WIDTH), lambda n: (cur(n), kcol)),
            pl.BlockSpec((lq, KV_WIDTH), lambda n: (prev(n), vcol)),
            pl.BlockSpec((tq, KV_WIDTH), lambda n: (cur(n), vcol)),
            pl.BlockSpec((2, N_KV_HEADS, 4 * lq, 2 * lq), lambda n: (0, 0, 0, 0)),
            _vec(128)]


def _tile_bias(bias_ref, first):
    return [[jnp.where(first, bias_ref[1, hk], bias_ref[0, hk]) if b == 0 else bias_ref[0, hk]
             for hk in range(N_KV_HEADS)] for b in range(ATTN_QB)]


def attn_fwd(qkv, bias, sinks_rows, exchange=None):
    s = qkv.shape[0]
    tq = ATTN_BLOCK * ATTN_QB
    nt = s // tq

    def body(q_ref, kp_ref, kc_ref, vp_ref, vc_ref, bias_ref, sk_ref, o_ref):
        f = lambda r: r[...].astype(F32)
        o = _attn_tile(f(q_ref), f(kp_ref), f(kc_ref), f(vp_ref), f(vc_ref),
                       _tile_bias(bias_ref, pl.program_id(0) == 0), sk_ref[...])
        o_ref[...] = _b(o)

    return _grid_call(
        body, nt, (qkv, qkv, qkv, qkv, qkv, bias, sinks_rows), exchange, name="attn_fwd",
        in_specs=_attn_in_specs(nt, False),
        out_specs=[pl.BlockSpec((tq, ATTN_WIDTH), _row)],
        out_shape=[jax.ShapeDtypeStruct((s, ATTN_WIDTH), BF16)],
        compiler_params=_params())


def attn_bwd(qkv, bias, sinks_rows, d_attn, exchange=None):
    s = qkv.shape[0]
    lq, tq = ATTN_BLOCK, ATTN_BLOCK * ATTN_QB
    nt = s // tq

    def body(q_ref, kp_ref, kc_ref, vp_ref, vc_ref, bias_ref, sk_ref, do_ref,
             dq_ref, dk_ref, dv_ref, dbias_ref, dsk_ref, carry_k, carry_v):
        n = pl.program_id(0)

        @pl.when(n == 0)
        def _():
            dbias_ref[...] = jnp.zeros_like(dbias_ref)
            dsk_ref[...] = jnp.zeros_like(dsk_ref)
            carry_k[...] = jnp.zeros_like(carry_k)
            carry_v[...] = jnp.zeros_like(carry_v)

        @pl.when(n < nt)
        def _():
            f = lambda r: r[...].astype(F32)
            _, vjp = jax.vjp(_attn_tile, f(q_ref), f(kp_ref), f(kc_ref), f(vp_ref), f(vc_ref),
                             _tile_bias(bias_ref, n == 0), sk_ref[...])
            dq, dkp, dkc, dvp, dvc, dbias, dsk = vjp(f(do_ref))
            dq_ref[...] = _b(dq)
            done = tq - lq
            dk_ref[:done, :] = _b(carry_k[:done, :])
            dv_ref[:done, :] = _b(carry_v[:done, :])
            dk_ref[done:, :] = _b(carry_k[done:, :] + dkp)
            dv_ref[done:, :] = _b(carry_v[done:, :] + dvp)
            carry_k[...] = dkc
            carry_v[...] = dvc
            dsk_ref[...] += dsk
            first = (n == 0).astype(F32)
            for hk in range(N_KV_HEADS):
                total = dbias[0][hk]
                for b in range(1, ATTN_QB):
                    total = total + dbias[b][hk]
                dbias_ref[0, hk] += total - first * dbias[0][hk]
                dbias_ref[1, hk] += first * dbias[0][hk]

        @pl.when(n == nt)
        def _():
            dk_ref[...] = _b(carry_k[...])
            dv_ref[...] = _b(carry_v[...])

    cur = lambda n: (jnp.minimum(n, nt - 1), 0)
    done_map = lambda n: (jnp.maximum(n - 1, 0), 0)
    return _grid_call(
        body, nt + 1, (qkv, qkv, qkv, qkv, qkv, bias, sinks_rows, d_attn), exchange, name="attn_bwd",
        in_specs=_attn_in_specs(nt, True) + [pl.BlockSpec((tq, ATTN_WIDTH), cur)],
        out_specs=[pl.BlockSpec((tq, ATTN_WIDTH), cur), pl.BlockSpec((tq, KV_WIDTH), done_map),
                   pl.BlockSpec((tq, KV_WIDTH), done_map),
                   pl.BlockSpec((2, N_KV_HEADS, 4 * lq, 2 * lq), lambda n: (0, 0, 0, 0)), _vec(128)],
        out_shape=[jax.ShapeDtypeStruct((s, ATTN_WIDTH), BF16), jax.ShapeDtypeStruct((s, KV_WIDTH), BF16),
                   jax.ShapeDtypeStruct((s, KV_WIDTH), BF16),
                   jax.ShapeDtypeStruct((2, N_KV_HEADS, 4 * lq, 2 * lq), F32), jax.ShapeDtypeStruct((1, 128), F32)],
        scratch_shapes=[pltpu.VMEM((tq, KV_WIDTH), F32), pltpu.VMEM((tq, KV_WIDTH), F32)],
        compiler_params=_params())


def rel_bias_table(rel_bias, bucket):
    lq = ATTN_BLOCK
    group = N_Q_HEADS // N_KV_HEADS

    def body(rb_ref, bk_ref, o_ref):
        bk = bk_ref[...]
        prev_keys = lax.broadcasted_iota(jnp.int32, bk.shape, 1) < lq
        accs = [jnp.full(bk.shape, -1e30, F32) for _ in range(N_Q_HEADS)]
        for b in range(N_BUCKETS):
            hit = bk == b
            accs = [jnp.where(hit, rb_ref[b, h], acc) for h, acc in enumerate(accs)]
        for h in range(N_Q_HEADS):
            rows = slice((h % group) * lq, (h % group + 1) * lq)
            o_ref[0, h // group, rows, :] = accs[h]
            o_ref[1, h // group, rows, :] = jnp.where(prev_keys, -1e30, accs[h])

    return pl.pallas_call(
        body, name="rel_bias_table",
        in_specs=[pl.BlockSpec(memory_space=pltpu.SMEM), pl.BlockSpec(memory_space=pltpu.VMEM)],
        out_shape=jax.ShapeDtypeStruct((2, N_KV_HEADS, group * lq, 2 * lq), F32),
        compiler_params=_params(n_axes=0))(rel_bias, bucket)


def rel_bias_grad(dbias, bucket):
    lq = ATTN_BLOCK
    group = N_Q_HEADS // N_KV_HEADS

    def body(db_ref, bk_ref, o_ref):
        rows = lax.broadcasted_iota(jnp.int32, (N_BUCKETS, 128), 0)
        lanes = lax.broadcasted_iota(jnp.int32, (N_BUCKETS, 128), 1)
        bk = bk_ref[...]
        per_head = []
        for h in range(N_Q_HEADS):
            sl = slice((h % group) * lq, (h % group + 1) * lq)
            per_head.append(db_ref[0, h // group, sl, :] + db_ref[1, h // group, sl, :])

        def per_bucket(b, acc):
            hit = (bk == b).astype(F32)
            for h in range(N_Q_HEADS):
                val = jnp.sum(per_head[h] * hit, keepdims=True)
                acc = acc + jnp.where((rows == b) & (lanes == h), val, 0.0)
            return acc

        o_ref[...] = lax.fori_loop(0, N_BUCKETS, per_bucket, jnp.zeros((N_BUCKETS, 128), F32))

    return pl.pallas_call(body, name="rel_bias_grad", out_shape=jax.ShapeDtypeStruct((N_BUCKETS, 128), F32),
                          compiler_params=_params(n_axes=0))(dbias, bucket)


def _tri_sum(a, upper):
    n = a.shape[0]
    ri = lax.broadcasted_iota(jnp.int32, (n, n), 0)
    ci = lax.broadcasted_iota(jnp.int32, (n, n), 1)
    tri = ((ri <= ci) if upper else (ri >= ci)).astype(BF16)
    hi = a.astype(BF16)
    rest = a - hi.astype(F32)
    mid = rest.astype(BF16)
    lo = (rest - mid.astype(F32)).astype(BF16)
    dot = lambda part: jnp.dot(tri, part, preferred_element_type=F32)
    return dot(hi) + dot(mid) + dot(lo)


@jax.custom_vjp
def _cumsum_rows(a):
    return _tri_sum(a, False)


_cumsum_rows.defvjp(lambda a: (_tri_sum(a, False), None), lambda _, g: (_tri_sum(g, True),))


def _ssm_core(u, z, dt_raw, hprev, dt_bias, a_log, dskip, norm_w):
    lc = u.shape[0]
    xbc = _silu(u)
    xs, bm, cm = xbc[:, :SSM_WIDTH], xbc[:, SSM_WIDTH:SSM_WIDTH + SSM_BC], xbc[:, SSM_WIDTH + SSM_BC:]
    dt = jax.nn.softplus(dt_raw + dt_bias)
    adt = dt * (-jnp.exp(a_log))
    ri = lax.broadcasted_iota(jnp.int32, (lc, lc), 0)
    ci = lax.broadcasted_iota(jnp.int32, (lc, lc), 1)
    causal = ri >= ci
    acum = _cumsum_rows(adt)
    acum_t = acum.T
    last = acum[lc - 1:lc, :]
    per_group = SSM_HEADS // SSM_GROUPS
    lane = lax.broadcasted_iota(jnp.int32, (1, 128), 1)
    rowid = lax.broadcasted_iota(jnp.int32, (128, 1), 0)
    lo_lanes = lane < SSM_HEAD_DIM
    ys, hs = [], []
    for g in range(SSM_GROUPS):
        bg = bm[:, g * SSM_STATE:(g + 1) * SSM_STATE]
        cg = cm[:, g * SSM_STATE:(g + 1) * SSM_STATE]
        cb = mm_nt(cg, bg)
        for pp in range(per_group // 2):
            ha = g * per_group + 2 * pp
            xp = xs[:, ha * SSM_HEAD_DIM:(ha + 2) * SSM_HEAD_DIM]
            hp = hprev[ha * SSM_HEAD_DIM:(ha + 2) * SSM_HEAD_DIM, :]
            xcp = xp * jnp.where(lo_lanes, dt[:, ha:ha + 1], dt[:, ha + 1:ha + 2])
            y_h, st_h = [], []
            for h in (ha, ha + 1):
                col, rowv, lasth = acum[:, h:h + 1], acum_t[h:h + 1, :], last[:, h:h + 1]
                decay = jnp.exp(jnp.where(causal, col - rowv, -1e30))
                y_h.append(mm(cb * decay, xcp) + mm_nt(cg * jnp.exp(col), hp))
                st_h.append(mm_tn(xcp, bg * jnp.exp(lasth - col)))
            y_pair = jnp.where(lo_lanes, y_h[0], y_h[1])
            st_pair = jnp.where(rowid < SSM_HEAD_DIM, st_h[0], st_h[1])
            la, lb = last[:, ha:ha + 1], last[:, ha + 1:ha + 2]
            hs.append(jnp.exp(jnp.where(rowid < SSM_HEAD_DIM, la, lb)) * hp + st_pair)
            dsk = jnp.where(lo_lanes, dskip[:, ha:ha + 1], dskip[:, ha + 1:ha + 2])
            ys.append(y_pair + dsk * xp)
    y = jnp.concatenate(ys, axis=1) * _silu(z)
    gw = SSM_WIDTH // SSM_GROUPS
    outs = []
    for g in range(SSM_GROUPS):
        yg = y[:, g * gw:(g + 1) * gw]
        outs.append(yg * lax.rsqrt(jnp.mean(yg * yg, axis=-1, keepdims=True) + NORM_EPS))
    return jnp.concatenate(outs, axis=1) * norm_w, jnp.concatenate(hs, axis=0)


def _ssm_param_specs():
    return [pl.BlockSpec((SSM_CONV, XBC_WIDTH), _const), _vec(XBC_WIDTH), _vec(128), _vec(128), _vec(128),
            _vec(SSM_WIDTH)]


def ssm_fwd(xbc_raw, z, dt_raw, conv_w, conv_b, dt_bias, a_log, dskip, norm_w, exchange=None):
    s = xbc_raw.shape[0]
    lc = SSM_CHUNK
    nc = s // lc
    hrows = SSM_HEADS * SSM_HEAD_DIM

    def body(x_ref, halo_ref, z_ref, dt_ref, cw_ref, cb_ref, dtb_ref, al_ref, dk_ref, nw_ref,
             o_ref, hp_ref, state):
        i = pl.program_id(0)

        @pl.when(i == 0)
        def _():
            state[...] = jnp.zeros_like(state)

        halo = halo_ref[...] * (i > 0).astype(F32)
        xin = jnp.concatenate([halo, x_ref[...]], axis=0)
        u = (_conv_rows(xin, cw_ref[...], SSM_CONV) + cb_ref[...])[HALO:]
        hprev = state[...]
        hp_ref[...] = hprev
        out, hnew = _ssm_core(u, z_ref[...], dt_ref[...], hprev, dtb_ref[...], al_ref[...], dk_ref[...], nw_ref[...])
        o_ref[...] = _b(out)
        state[...] = hnew

    tile = lambda n: pl.BlockSpec((lc, n), _row)
    halo_spec = pl.BlockSpec((HALO, XBC_WIDTH), lambda i: (jnp.maximum(i * (lc // HALO) - 1, 0), 0))
    return _grid_call(
        body, nc, (xbc_raw, xbc_raw, z, dt_raw, conv_w, conv_b, dt_bias, a_log, dskip, norm_w), exchange,
        name="ssm_fwd",
        in_specs=[tile(XBC_WIDTH), halo_spec, tile(SSM_WIDTH), tile(DT_PAD)] + _ssm_param_specs(),
        out_specs=[tile(SSM_WIDTH), pl.BlockSpec((hrows, SSM_STATE), _row)],
        out_shape=[jax.ShapeDtypeStruct((s, SSM_WIDTH), BF16), jax.ShapeDtypeStruct((nc * hrows, SSM_STATE), F32)],
        scratch_shapes=[pltpu.VMEM((hrows, SSM_STATE), F32)],
        compiler_params=_params())


def ssm_bwd(xbc_raw, z, dt_raw, hprev_all, d_out, conv_w, conv_b, dt_bias, a_log, dskip, norm_w, exchange=None):
    s = xbc_raw.shape[0]
    lc = SSM_CHUNK
    nc = s // lc
    hrows = SSM_HEADS * SSM_HEAD_DIM

    def body(x_ref, halo_ref, z_ref, dt_ref, hp_ref, do_ref, cw_ref, cb_ref, dtb_ref, al_ref, dk_ref, nw_ref,
             dx_ref, dz_ref, ddt_ref, dcw_ref, dcb_ref, ddtb_ref, dal_ref, ddk_ref, dnw_ref, dstate, du_next):
        i = pl.program_id(0)
        chunk = nc - 1 - i

        @pl.when(i == 0)
        def _():
            dstate[...] = jnp.zeros_like(dstate)
            du_next[...] = jnp.zeros_like(du_next)
            for r in (dcw_ref, dcb_ref, ddtb_ref, dal_ref, ddk_ref, dnw_ref):
                r[...] = jnp.zeros_like(r)

        halo = halo_ref[...] * (chunk > 0).astype(F32)
        xin = jnp.concatenate([halo, x_ref[...]], axis=0)
        cw = cw_ref[...]
        u = (_conv_rows(xin, cw, SSM_CONV) + cb_ref[...])[HALO:]
        _, vjp = jax.vjp(_ssm_core, u, z_ref[...], dt_ref[...], hp_ref[...], dtb_ref[...], al_ref[...],
                         dk_ref[...], nw_ref[...])
        du, dz, ddt, dhp, ddtb, dal, ddk, dnw = vjp((do_ref[...], dstate[...]))
        dstate[...] = dhp
        dz_ref[...] = dz
        ddt_ref[...] = ddt
        du_ext = jnp.concatenate([du, du_next[...]], axis=0)
        dx_ref[...] = _conv_rows_t(du_ext, cw, SSM_CONV)[:lc]
        du_next[...] = du[:HALO]
        rows = [jnp.sum(du * pltpu.roll(xin, j, axis=0)[HALO:] if j else du * xin[HALO:], axis=0, keepdims=True)
                for j in range(SSM_CONV)]
        dcw_ref[...] += jnp.concatenate(rows[::-1] + [jnp.zeros((8 - SSM_CONV, XBC_WIDTH), F32)], axis=0)
        dcb_ref[...] += jnp.sum(du, axis=0, keepdims=True)
        ddtb_ref[...] += ddtb
        dal_ref[...] += dal
        ddk_ref[...] += ddk
        dnw_ref[...] += dnw

    rev = lambda i: (nc - 1 - i, 0)
    tile = lambda n: pl.BlockSpec((lc, n), rev)
    halo_spec = pl.BlockSpec((HALO, XBC_WIDTH), lambda i: (jnp.maximum((nc - 1 - i) * (lc // HALO) - 1, 0), 0))
    acc = lambda r, n: pl.BlockSpec((r, n), _const)
    return _grid_call(
        body, nc, (xbc_raw, xbc_raw, z, dt_raw, hprev_all, d_out, conv_w, conv_b, dt_bias, a_log, dskip, norm_w),
        exchange, name="ssm_bwd",
        in_specs=[tile(XBC_WIDTH), halo_spec, tile(SSM_WIDTH), tile(DT_PAD), pl.BlockSpec((hrows, SSM_STATE), rev),
                  tile(SSM_WIDTH)] + _ssm_param_specs(),
        out_specs=[tile(XBC_WIDTH), tile(SSM_WIDTH), tile(DT_PAD), acc(8, XBC_WIDTH), acc(1, XBC_WIDTH),
                   acc(1, 128), acc(1, 128), acc(1, 128), acc(1, SSM_WIDTH)],
        out_shape=[jax.ShapeDtypeStruct((s, XBC_WIDTH), F32), jax.ShapeDtypeStruct((s, SSM_WIDTH), F32),
                   jax.ShapeDtypeStruct((s, DT_PAD), F32), jax.ShapeDtypeStruct((8, XBC_WIDTH), F32),
                   jax.ShapeDtypeStruct((1, XBC_WIDTH), F32), jax.ShapeDtypeStruct((1, 128), F32),
                   jax.ShapeDtypeStruct((1, 128), F32), jax.ShapeDtypeStruct((1, 128), F32),
                   jax.ShapeDtypeStruct((1, SSM_WIDTH), F32)],
        scratch_shapes=[pltpu.VMEM((hrows, SSM_STATE), F32), pltpu.VMEM((HALO, XBC_WIDTH), F32)],
        compiler_params=_params())


def mix_out(attn, ssm, x, w_out16, gate1, post_mix_w, pre_ffn_w, scale2, shift2, tm=TOKEN_TILE):
    s = x.shape[0]

    def body(a_ref, s_ref, x_ref, w_ref, g_ref, pw_ref, fw_ref, sc_ref, sh_ref, mixed_ref, x1_ref, h2_ref):
        mixed = (jnp.dot(a_ref[...], w_ref[:ATTN_WIDTH, :], preferred_element_type=F32)
                 + jnp.dot(s_ref[...], w_ref[ATTN_WIDTH:, :], preferred_element_type=F32))
        mixed_ref[...] = mixed
        x1 = x_ref[...] + g_ref[...] * _rms(mixed, pw_ref[...])
        x1_ref[...] = x1
        h2_ref[...] = _b(_norm_mod(x1, fw_ref[...], sc_ref[...], sh_ref[...]))

    tile = lambda n: pl.BlockSpec((tm, n), _row)
    return pl.pallas_call(
        body, name="mix_out", grid=(s // tm,),
        in_specs=[tile(ATTN_WIDTH), tile(SSM_WIDTH), tile(D_MODEL), pl.BlockSpec((D_MODEL, D_MODEL), _const)]
        + [_vec(D_MODEL)] * 5,
        out_specs=[tile(D_MODEL)] * 3,
        out_shape=[jax.ShapeDtypeStruct((s, D_MODEL), F32), jax.ShapeDtypeStruct((s, D_MODEL), F32),
                   jax.ShapeDtypeStruct((s, D_MODEL), BF16)],
        compiler_params=_params())(attn, ssm, x, w_out16, gate1, post_mix_w, pre_ffn_w, scale2, shift2)


def _gate(ug, uv):
    return jax.nn.gelu(ug, approximate=True) * uv


def _gate_bwd(ug, uv, df):
    k0, k1 = math.sqrt(2.0 / math.pi), 0.044715
    sq = ug * ug
    t = jnp.tanh(k0 * ug * (1.0 + k1 * sq))
    half = 0.5 * (1.0 + t)
    slope = half + (0.5 * k0) * ug * (1.0 - t * t) * (1.0 + (3.0 * k1) * sq)
    return df * uv * slope, df * (ug * half)


def _resident(shape):
    return pl.BlockSpec(shape, _const, pipeline_mode=pl.Buffered(1))


def up_gate(h2, w_up16, conv_w, conv_b, tm=TOKEN_TILE):
    s = h2.shape[0]

    def body(h_ref, halo_ref, w_ref, cw_ref, cb_ref, u_ref, uraw_ref, f_ref):
        halo = halo_ref[...]
        halo = jnp.where(pl.program_id(0) > 0, halo, jnp.zeros_like(halo))
        hin = jnp.concatenate([halo, h_ref[...]], axis=0)
        for lo in range(0, D_FF, FF_CHUNK):
            halves = []
            for base in (lo, D_FF + lo):
                cols = slice(base, base + FF_CHUNK)
                uraw = jnp.dot(hin, w_ref[:, cols], preferred_element_type=F32)
                uraw_ref[:, cols] = _b(uraw[NEXT:])
                u = (_conv_rows(uraw, cw_ref[:, cols], FFN_CONV) + cb_ref[:, cols])[NEXT:]
                u_ref[:, cols] = u
                halves.append(u)
            f_ref[:, lo:lo + FF_CHUNK] = _b(_gate(*halves))

    tile = lambda n: pl.BlockSpec((tm, n), _row)
    halo_spec = pl.BlockSpec((NEXT, D_MODEL), lambda i: (jnp.maximum(i * (tm // NEXT) - 1, 0), 0))
    return pl.pallas_call(
        body, name="up_gate", grid=(s // tm,),
        in_specs=[tile(D_MODEL), halo_spec, _resident((D_MODEL, 2 * D_FF)),
                  pl.BlockSpec((FFN_CONV, 2 * D_FF), _const), _vec(2 * D_FF)],
        out_specs=[tile(2 * D_FF), tile(2 * D_FF), tile(D_FF)],
        out_shape=[jax.ShapeDtypeStruct((s, 2 * D_FF), F32), jax.ShapeDtypeStruct((s, 2 * D_FF), BF16),
                   jax.ShapeDtypeStruct((s, D_FF), BF16)],
        compiler_params=_params())(h2, h2, w_up16, conv_w, conv_b)


DOWN_LOSS_TILE = 512


def down_loss(f16, w_down16, x1, target, gate2, post_ffn_w, tm=DOWN_LOSS_TILE):
    s = x1.shape[0]
    tm = min(tm, s)

    def body(f_ref, wd_ref, x1_ref, t_ref, g_ref, pw_ref, dffn_ref, dy_ref, loss_ref, dg_ref, dpw_ref, gw_ref):
        i = pl.program_id(0)

        @pl.when(i == 0)
        def _():
            loss_ref[...] = jnp.zeros_like(loss_ref)
            dg_ref[...] = jnp.zeros_like(dg_ref)
            dpw_ref[...] = jnp.zeros_like(dpw_ref)
            gw_ref[...] = jnp.zeros_like(gw_ref)

        ffn = jnp.dot(f_ref[...], wd_ref[...], preferred_element_type=F32)
        x1 = x1_ref[...]
        x2 = x1 + g_ref[...] * _rms(ffn, pw_ref[...])
        err = x2 - t_ref[...]
        dy = err * (1.0 / D_MODEL)
        dy_ref[...] = dy
        loss_ref[...] += 0.5 * jnp.sum(jnp.mean(err * err, axis=-1, keepdims=True))
        dffn, dg, dpw = _gated_rms_bwd(ffn, g_ref[...], pw_ref[...], dy)
        dffn16 = _b(dffn)
        dffn_ref[...] = dffn16
        dg_ref[...] += dg
        dpw_ref[...] += dpw
        gw_ref[...] += _tn(f_ref[...], dffn16)

    tile = lambda n: pl.BlockSpec((tm, n), _row)
    return pl.pallas_call(
        body, name="down_loss", grid=(s // tm,),
        in_specs=[tile(D_FF), _resident((D_FF, D_MODEL)), tile(D_MODEL), tile(D_MODEL), _vec(D_MODEL), _vec(D_MODEL)],
        out_specs=[tile(D_MODEL), tile(D_MODEL), _vec(128), _vec(D_MODEL), _vec(D_MODEL),
                   pl.BlockSpec((D_FF, D_MODEL), _const)],
        out_shape=[jax.ShapeDtypeStruct((s, D_MODEL), BF16), jax.ShapeDtypeStruct((s, D_MODEL), F32),
                   jax.ShapeDtypeStruct((1, 128), F32), jax.ShapeDtypeStruct((1, D_MODEL), F32),
                   jax.ShapeDtypeStruct((1, D_MODEL), F32), jax.ShapeDtypeStruct((D_FF, D_MODEL), F32)],
        compiler_params=_params())(f16, w_down16, x1, target, gate2, post_ffn_w)


BWD_CHUNK = 256


def ffn_bwd(u, u_raw16, d_ffn, conv_w, w_down_t16, w_up_t16, tm=TOKEN_TILE):
    s = u.shape[0]
    nt = s // tm

    def body(u_ref, unext_ref, uraw_ref, d_ref, dnext_ref, cw_ref, wdt_ref, wut_ref,
             du_ref, dh_ref, dcw_ref, dcb_ref):
        i = pl.program_id(0)

        @pl.when(i == 0)
        def _():
            dcw_ref[...] = jnp.zeros_like(dcw_ref)
            dcb_ref[...] = jnp.zeros_like(dcb_ref)

        dnext = dnext_ref[...]
        dnext = jnp.where(i < nt - 1, dnext, jnp.zeros_like(dnext))
        dff = jnp.concatenate([d_ref[...], dnext], axis=0)
        rows_ext = tm + NEXT
        for lo in range(0, D_FF, BWD_CHUNK):
            gcols, vcols = slice(lo, lo + BWD_CHUNK), slice(D_FF + lo, D_FF + lo + BWD_CHUNK)
            ug = jnp.concatenate([u_ref[:, gcols], unext_ref[:, gcols]], axis=0)
            uv = jnp.concatenate([u_ref[:, vcols], unext_ref[:, vcols]], axis=0)
            df = jnp.dot(dff, wdt_ref[:, gcols], preferred_element_type=F32)
            for cols, du in zip((gcols, vcols), _gate_bwd(ug, uv, df)):
                cw = cw_ref[:, cols]
                du1 = pltpu.roll(du, rows_ext - 1, axis=0)
                du2 = pltpu.roll(du, rows_ext - 2, axis=0)
                du_ref[:, cols] = _b((du * cw[2:3, :] + du1 * cw[1:2, :] + du2 * cw[0:1, :])[:tm])
                xr = uraw_ref[:, cols].astype(F32)
                rows = [jnp.sum(xr * d_[:tm], axis=0, keepdims=True) for d_ in (du2, du1, du)]
                dcw_ref[:, cols] += jnp.concatenate(rows + [jnp.zeros((8 - FFN_CONV, BWD_CHUNK), F32)], axis=0)
                dcb_ref[:, cols] += jnp.sum(du[:tm], axis=0, keepdims=True)
        dh_ref[...] = jnp.dot(du_ref[...], wut_ref[...], preferred_element_type=F32)

    tile = lambda n: pl.BlockSpec((tm, n), _row)
    nxt = lambda i: (jnp.minimum((i + 1) * (tm // NEXT), s // NEXT - 1), 0)
    return pl.pallas_call(
        body, name="ffn_bwd", grid=(nt,),
        in_specs=[tile(2 * D_FF), pl.BlockSpec((NEXT, 2 * D_FF), nxt), tile(2 * D_FF), tile(D_MODEL),
                  pl.BlockSpec((NEXT, D_MODEL), nxt), pl.BlockSpec((FFN_CONV, 2 * D_FF), _const),
                  _resident((D_MODEL, D_FF)), _resident((2 * D_FF, D_MODEL))],
        out_specs=[tile(2 * D_FF), tile(D_MODEL), pl.BlockSpec((8, 2 * D_FF), _const), _vec(2 * D_FF)],
        out_shape=[jax.ShapeDtypeStruct((s, 2 * D_FF), BF16), jax.ShapeDtypeStruct((s, D_MODEL), F32),
                   jax.ShapeDtypeStruct((8, 2 * D_FF), F32), jax.ShapeDtypeStruct((1, 2 * D_FF), F32)],
        compiler_params=_params())(u, u, u_raw16, d_ffn, d_ffn, conv_w, w_down_t16, w_up_t16)


def mix_bwd(dh2, x1, dy, mixed, attn, ssm, w_out_t16, pre_ffn_w, scale2, gate1, post_mix_w, tm=TOKEN_TILE):
    s = x1.shape[0]

    def body(dh_ref, x1_ref, dy_ref, mx_ref, a_ref, s_ref, w_ref, fw_ref, sc_ref, g_ref, pw_ref,
             dx1_ref, da_ref, ds_ref, dfw_ref, dsc_ref, dsh_ref, dg_ref, dpw_ref, gw_ref):
        accs = (dfw_ref, dsc_ref, dsh_ref, dg_ref, dpw_ref)

        @pl.when(pl.program_id(0) == 0)
        def _():
            for r in accs + (gw_ref,):
                r[...] = jnp.zeros_like(r)

        dx1, dfw, dsc, dsh = _norm_mod_bwd(x1_ref[...], fw_ref[...], sc_ref[...], dh_ref[...])
        dx1 = dx1 + dy_ref[...]
        dx1_ref[...] = dx1
        dmixed, dg, dpw = _gated_rms_bwd(mx_ref[...], g_ref[...], pw_ref[...], dx1)
        dm16 = _b(dmixed)
        dmix_in = jnp.dot(dm16, w_ref[...], preferred_element_type=F32)
        da_ref[...] = _b(dmix_in[:, :ATTN_WIDTH])
        ds_ref[...] = dmix_in[:, ATTN_WIDTH:]
        gw_ref[:ATTN_WIDTH, :] += _tn(a_ref[...], dm16)
        gw_ref[ATTN_WIDTH:, :] += _tn(s_ref[...], dm16)
        for r, v in zip(accs, (dfw, dsc, dsh, dg, dpw)):
            r[...] += v

    tile = lambda n: pl.BlockSpec((tm, n), _row)
    return pl.pallas_call(
        body, name="mix_bwd", grid=(s // tm,),
        in_specs=[tile(D_MODEL)] * 4 + [tile(ATTN_WIDTH), tile(SSM_WIDTH), _resident((D_MODEL, D_MODEL))]
        + [_vec(D_MODEL)] * 4,
        out_specs=[tile(D_MODEL), tile(ATTN_WIDTH), tile(SSM_WIDTH)] + [_vec(D_MODEL)] * 5
        + [pl.BlockSpec((D_MODEL, D_MODEL), _const)],
        out_shape=[jax.ShapeDtypeStruct((s, D_MODEL), F32), jax.ShapeDtypeStruct((s, ATTN_WIDTH), BF16),
                   jax.ShapeDtypeStruct((s, SSM_WIDTH), F32)]
        + [jax.ShapeDtypeStruct((1, D_MODEL), F32)] * 5 + [jax.ShapeDtypeStruct((D_MODEL, D_MODEL), F32)],
        compiler_params=_params())(dh2, x1, dy, mixed, attn, ssm, w_out_t16, pre_ffn_w, scale2, gate1, post_mix_w)


INPROJ_BWD_TILE = 512


def inproj_bwd(dq, dk, dv, dxbc, dz, ddt, x, dx1, h1, w_in_t16, pre_mix_w, scale1, tm=INPROJ_BWD_TILE):
    s = x.shape[0]
    tm = min(tm, s)

    def body(dq_ref, dk_ref, dv_ref, dxbc_ref, dz_ref, ddt_ref, x_ref, dx1_ref, h_ref, w_ref, pw_ref, sc_ref,
             gx_ref, dpw_ref, dsc_ref, dsh_ref, gw_ref):
        accs = (dpw_ref, dsc_ref, dsh_ref)

        @pl.when(pl.program_id(0) == 0)
        def _():
            for r in accs + (gw_ref,):
                r[...] = jnp.zeros_like(r)

        h16 = h_ref[...]
        dh = None
        off_k, off_v = ATTN_WIDTH, ATTN_WIDTH + KV_WIDTH
        for r, lo, hi in ((dq_ref, 0, off_k), (dk_ref, off_k, off_v), (dv_ref, off_v, OFF_XBC),
                          (dxbc_ref, OFF_XBC, OFF_Z), (dz_ref, OFF_Z, OFF_DT), (ddt_ref, OFF_DT, PROJ_PAD)):
            d16 = _b(r[...])
            part = jnp.dot(d16, w_ref[lo:hi, :], preferred_element_type=F32)
            dh = part if dh is None else dh + part
            gw_ref[:, lo:hi] += _tn(h16, d16)
        dx, dpw, dsc, dsh = _norm_mod_bwd(x_ref[...], pw_ref[...], sc_ref[...], dh)
        gx_ref[...] = dx1_ref[...] + dx
        for r, v in zip(accs, (dpw, dsc, dsh)):
            r[...] += v

    tile = lambda n: pl.BlockSpec((tm, n), _row)
    return pl.pallas_call(
        body, name="inproj_bwd", grid=(s // tm,),
        in_specs=[tile(ATTN_WIDTH), tile(KV_WIDTH), tile(KV_WIDTH), tile(XBC_WIDTH), tile(SSM_WIDTH), tile(DT_PAD),
                  tile(D_MODEL), tile(D_MODEL), tile(D_MODEL), _resident((PROJ_PAD, D_MODEL))] + [_vec(D_MODEL)] * 2,
        out_specs=[tile(D_MODEL)] + [_vec(D_MODEL)] * 3 + [pl.BlockSpec((D_MODEL, PROJ_PAD), _const)],
        out_shape=[jax.ShapeDtypeStruct((s, D_MODEL), F32)] + [jax.ShapeDtypeStruct((1, D_MODEL), F32)] * 3
        + [jax.ShapeDtypeStruct((D_MODEL, PROJ_PAD), F32)],
        compiler_params=_params())(dq, dk, dv, dxbc, dz, ddt, x, dx1, h1, w_in_t16, pre_mix_w, scale1)


def _adam(g, w, m, v):
    new_m = ADAM_B1 * m + (1.0 - ADAM_B1) * g
    new_v = ADAM_B2 * v + (1.0 - ADAM_B2) * jnp.square(g)
    m_hat = new_m / (1.0 - ADAM_B1 ** ADAM_STEP)
    v_hat = new_v / (1.0 - ADAM_B2 ** ADAM_STEP)
    return -ADAM_LR * (m_hat / (jnp.sqrt(v_hat) + ADAM_EPS) + ADAM_WD * w), new_m, new_v


ROW_PARAMS = (("b_ada", 6144, 6144), ("pre_mix_w", 1024, 1024), ("attn_sinks", 128, 8), ("ssm_conv_b", 1024, 1024),
              ("ssm_dt_bias", 128, 8), ("ssm_a_log", 128, 8), ("ssm_d", 128, 8), ("ssm_norm_w", 512, 512),
              ("post_mix_w", 1024, 1024), ("pre_ffn_w", 1024, 1024), ("ffn_conv_b", 5632, 5632),
              ("post_ffn_w", 1024, 1024))
LOSS_LANES = 128


def adamw_small(row_all, rb_all, rel_bias_wmv, row_wmv):
    n_rows = len(ROW_PARAMS)

    def body(*refs):
        row_ref, rb_ref = refs[:2]
        wmv = refs[2:5 + 3 * n_rows]
        outs = refs[5 + 3 * n_rows:]
        g_row, g_rb = row_ref[0], rb_ref[0]
        for k in range(1, N_DEV):
            g_row = g_row + row_ref[k]
            g_rb = g_rb + rb_ref[k]
        outs[0][...] = g_row[:, :LOSS_LANES]
        grads = [g_rb[:, :N_Q_HEADS]]
        off = LOSS_LANES
        for _, lanes, width in ROW_PARAMS:
            grads.append(g_row[:, off:off + width])
            off += lanes
        for i, g in enumerate(grads):
            w_ref, m_ref, v_ref = wmv[3 * i:3 * i + 3]
            g_out, d_out, m_out, v_out = outs[1 + 4 * i:5 + 4 * i]
            g_out[...] = g
            d_out[...], m_out[...], v_out[...] = _adam(g, w_ref[...], m_ref[...], v_ref[...])

    flat_wmv = list(rel_bias_wmv) + [a for wmv in row_wmv for a in wmv]
    shapes = [jax.ShapeDtypeStruct((1, LOSS_LANES), F32)] + [jax.ShapeDtypeStruct((N_BUCKETS, N_Q_HEADS), F32)] * 4
    for _, _, width in ROW_PARAMS:
        shapes += [jax.ShapeDtypeStruct((1, width), F32)] * 4
    return pl.pallas_call(body, name="adamw_small", out_shape=shapes,
                          compiler_params=_params(n_axes=0))(row_all, rb_all, *flat_wmv)


def adamw(parts, w, m, v, name):
    p, r, n = parts.shape
    tr = _row_tile(r)

    def body(p_ref, w_ref, m_ref, v_ref, g_ref, d_ref, nm_ref, nv_ref):
        g = p_ref[0].astype(F32)
        for k in range(1, p):
            g = g + p_ref[k].astype(F32)
        g_ref[...] = g
        d_ref[...], nm_ref[...], nv_ref[...] = _adam(g, w_ref[...], m_ref[...], v_ref[...])

    tile = pl.BlockSpec((tr, n), _row)
    return pl.pallas_call(
        body, name=name, grid=(r // tr,),
        in_specs=[pl.BlockSpec((p, tr, n), lambda i: (0, i, 0)), tile, tile, tile],
        out_specs=[tile] * 4, out_shape=[jax.ShapeDtypeStruct((r, n), F32)] * 4,
        compiler_params=_params())(parts, w, m, v)


def _bucket_table():
    lq = ATTN_BLOCK
    qi = np.arange(lq)[:, None] + lq
    kj = np.arange(2 * lq)[None, :]
    dist = qi - kj
    d = np.maximum(dist, 0)
    max_exact = N_BUCKETS // 2
    nf = np.maximum(d, 1).astype(np.float32)
    large = max_exact + (np.log(nf / max_exact) / math.log(REL_MAX_DIST / max_exact)
                         * (N_BUCKETS - max_exact)).astype(np.int32)
    large = np.minimum(large, N_BUCKETS - 1)
    bucket = np.where(d < max_exact, d, large).astype(np.int32)
    in_band = (dist >= 0) & (dist < REL_MAX_DIST)
    return np.where(in_band, bucket, -1).astype(np.int32)


def _cols_from_blocks(g):
    return jnp.transpose(g, (1, 0, 2)).reshape(g.shape[1], N_DEV * g.shape[2])


def _cols_to_blocks(a):
    r, n = a.shape
    return jnp.transpose(a.reshape(r, N_DEV, n // N_DEV), (1, 0, 2))


def _perm_in_rows(wt):
    pad = jnp.zeros((DT_PAD - SSM_HEADS, wt.shape[1]), wt.dtype)
    return jnp.concatenate([wt[:768], wt[768:1280], wt[1792:2304], wt[1280:1792], wt[2304:2312], pad], axis=0)


def _unperm_in(g):
    return jnp.concatenate([g[:, :768], g[:, 768:1280], g[:, 1792:2304], g[:, 1280:1792], g[:, 2304:2312]], axis=1)


def _lane_pad(v, n=128):
    return jnp.pad(v, ((0, 0), (0, n - v.shape[1])))


def kernel(x, c, rel_bias, w_ada, b_ada, pre_mix_w, w_in, attn_sinks, ssm_conv_w, ssm_conv_b, ssm_dt_bias, ssm_a_log, ssm_d, ssm_norm_w, w_out, post_mix_w, pre_ffn_w, w_up, ffn_conv_w, ffn_conv_b, w_down, post_ffn_w, loss_target, m_rel_bias, m_w_ada, m_b_ada, m_pre_mix_w, m_w_in, m_attn_sinks, m_ssm_conv_w, m_ssm_conv_b, m_ssm_dt_bias, m_ssm_a_log, m_ssm_d, m_ssm_norm_w, m_w_out, m_post_mix_w, m_pre_ffn_w, m_w_up, m_ffn_conv_w, m_ffn_conv_b, m_w_down, m_post_ffn_w, v_rel_bias, v_w_ada, v_b_ada, v_pre_mix_w, v_w_in, v_attn_sinks, v_ssm_conv_w, v_ssm_conv_b, v_ssm_dt_bias, v_ssm_a_log, v_ssm_d, v_ssm_norm_w, v_w_out, v_post_mix_w, v_pre_ffn_w, v_w_up, v_ffn_conv_w, v_ffn_conv_b, v_w_down, v_post_ffn_w):
    weights = dict(rel_bias=rel_bias, w_ada=w_ada, b_ada=b_ada, pre_mix_w=pre_mix_w, w_in=w_in, attn_sinks=attn_sinks, ssm_conv_w=ssm_conv_w, ssm_conv_b=ssm_conv_b, ssm_dt_bias=ssm_dt_bias, ssm_a_log=ssm_a_log, ssm_d=ssm_d, ssm_norm_w=ssm_norm_w, w_out=w_out, post_mix_w=post_mix_w, pre_ffn_w=pre_ffn_w, w_up=w_up, ffn_conv_w=ffn_conv_w, ffn_conv_b=ffn_conv_b, w_down=w_down, post_ffn_w=post_ffn_w)
    mom_m = dict(rel_bias=m_rel_bias, w_ada=m_w_ada, b_ada=m_b_ada, pre_mix_w=m_pre_mix_w, w_in=m_w_in, attn_sinks=m_attn_sinks, ssm_conv_w=m_ssm_conv_w, ssm_conv_b=m_ssm_conv_b, ssm_dt_bias=m_ssm_dt_bias, ssm_a_log=m_ssm_a_log, ssm_d=m_ssm_d, ssm_norm_w=m_ssm_norm_w, w_out=m_w_out, post_mix_w=m_post_mix_w, pre_ffn_w=m_pre_ffn_w, w_up=m_w_up, ffn_conv_w=m_ffn_conv_w, ffn_conv_b=m_ffn_conv_b, w_down=m_w_down, post_ffn_w=m_post_ffn_w)
    mom_v = dict(rel_bias=v_rel_bias, w_ada=v_w_ada, b_ada=v_b_ada, pre_mix_w=v_pre_mix_w, w_in=v_w_in, attn_sinks=v_attn_sinks, ssm_conv_w=v_ssm_conv_w, ssm_conv_b=v_ssm_conv_b, ssm_dt_bias=v_ssm_dt_bias, ssm_a_log=v_ssm_a_log, ssm_d=v_ssm_d, ssm_norm_w=v_ssm_norm_w, w_out=v_w_out, post_mix_w=v_post_mix_w, pre_ffn_w=v_pre_ffn_w, w_up=v_w_up, ffn_conv_w=v_ffn_conv_w, ffn_conv_b=v_ffn_conv_b, w_down=v_w_down, post_ffn_w=v_post_ffn_w)
    order = ['rel_bias', 'w_ada', 'b_ada', 'pre_mix_w', 'w_in', 'attn_sinks', 'ssm_conv_w', 'ssm_conv_b', 'ssm_dt_bias', 'ssm_a_log', 'ssm_d', 'ssm_norm_w', 'w_out', 'post_mix_w', 'pre_ffn_w', 'w_up', 'ffn_conv_w', 'ffn_conv_b', 'w_down', 'post_ffn_w']

    me = 4 * lax.axis_index("x") + 2 * lax.axis_index("y") + lax.axis_index("c")
    xs_ = x[0]
    target = loss_target[0]

    (w_in_g, scw_g, fcw_g, c_g) = all_gather([_b(w_in[0]).T, ssm_conv_w[0], ffn_conv_w[0], c], "gather_weights")
    w_in_t16 = _perm_in_rows(w_in_g.reshape(IN_PROJ_WIDTH, D_MODEL))
    w_in16 = w_in_t16.T
    ssm_cw = _cols_from_blocks(scw_g)
    ffn_cw = _cols_from_blocks(fcw_g)
    c_all = c_g.reshape(N_DEV, D_MODEL)

    n_cols = w_ada.shape[2]
    b_cols = lax.dynamic_slice(b_ada, (0, me * n_cols), (1, n_cols))
    mod_part = ada_fwd(c_all, w_ada[0], b_cols)
    (mod_rows,) = all_to_all([mod_part.reshape(N_DEV, 1, n_cols)], "scatter_mod")
    mod = mod_rows.reshape(N_MOD, 1, D_MODEL)
    shift1, scale1, gate1, shift2, scale2, gate2 = (mod[i] for i in range(N_MOD))

    bucket_band = jnp.asarray(_bucket_table())
    bias = rel_bias_table(rel_bias, bucket_band)
    sinks_row = _lane_pad(attn_sinks)
    dt_bias, a_log, dskip = _lane_pad(ssm_dt_bias), _lane_pad(ssm_a_log), _lane_pad(ssm_d)

    h1, qkv, xbc_raw, z, dt_raw, w_out_g = pre_mix_inproj(
        xs_, pre_mix_w, scale1, shift1, w_in16, [(_b(w_out[0]), False)])
    attn, w_up_g = attn_fwd(qkv, bias, sinks_row, [(_b(w_up[0]).T, False)])
    ssm, hprev_all, w_down_g = ssm_fwd(xbc_raw, z, dt_raw, ssm_cw, ssm_conv_b, dt_bias, a_log, dskip, ssm_norm_w,
                                       [(_b(w_down[0]), False)])
    w_out16 = w_out_g.reshape(D_MODEL, D_MODEL)
    w_out_t16 = w_out16.T
    w_up_t16 = w_up_g.reshape(2 * D_FF, D_MODEL)
    w_up16 = w_up_t16.T
    w_down16 = w_down_g.reshape(D_FF, D_MODEL)
    w_down_t16 = w_down16.T
    mixed, x1, h2 = mix_out(attn, ssm, xs_, w_out16, gate1, post_mix_w, pre_ffn_w, scale2, shift2)
    u, u_raw16, f16 = up_gate(h2, w_up16, ffn_cw, ffn_conv_b)
    d_ffn, dy, loss_part, d_gate2, d_post_ffn_w, g_w_down = down_loss(f16, w_down16, x1, target, gate2, post_ffn_w)

    du_raw, dh2, d_ffn_cw, d_ffn_cb = ffn_bwd(u, u_raw16, d_ffn, ffn_cw, w_down_t16, w_up_t16)
    g_w_up = matmul_tn(h2, du_raw, "grad_w_up", D_MODEL, FF_CHUNK)
    (dx1, d_attn, d_ssm, d_pre_ffn_w, d_scale2, d_shift2, d_gate1, d_post_mix_w, g_w_out) = mix_bwd(
        dh2, x1, dy, mixed, attn, ssm, w_out_t16, pre_ffn_w, scale2, gate1, post_mix_w)
    dq, dk, dv, dbias, dsinks, p_w_down = attn_bwd(
        qkv, bias, sinks_row, d_attn, [(g_w_down.reshape(N_DEV, D_FF // N_DEV, D_MODEL), True)])
    d_rel_bias = rel_bias_grad(dbias, bucket_band)
    (dxbc, dz, ddt, d_ssm_cw, d_ssm_cb, d_dt_bias, d_a_log, d_dskip, d_norm_w, p_w_up, p_w_out) = ssm_bwd(
        xbc_raw, z, dt_raw, hprev_all, d_ssm, ssm_cw, ssm_conv_b, dt_bias, a_log, dskip, ssm_norm_w,
        [(_cols_to_blocks(g_w_up), True), (g_w_out.reshape(N_DEV, D_MODEL // N_DEV, D_MODEL), True)])
    grad_x, d_pre_mix_w, d_scale1, d_shift1, g_w_in_perm = inproj_bwd(
        dq, dk, dv, dxbc, dz, ddt, xs_, dx1, h1, w_in_t16, pre_mix_w, scale1)
    g_w_in = _unperm_in(g_w_in_perm)

    d_mod = jnp.concatenate([d_shift1, d_scale1, d_gate1, d_shift2, d_scale2, d_gate2], axis=1)
    late = ("w_in", "ssm_conv_w", "ffn_conv_w")
    full = [_cols_to_blocks(g_w_in), _cols_to_blocks(d_ssm_cw[:SSM_CONV]), _cols_to_blocks(d_ffn_cw[:FFN_CONV])]
    core = lax.axis_index("c").astype(jnp.int32).reshape(1)
    got = pair_exchange(full, "pair_grads")
    chip_sums = [pair_sum(f_, g_, core, "pair_sum_" + k) for k, f_, g_ in zip(late, full, got)]
    chip_parts = all_to_all(chip_sums, "scatter_grads", CHIP_FLIPS, _chip_index)
    (d_mod_rows,) = all_to_all([d_mod.reshape(N_DEV, 1, n_cols)], "scatter_dmod")
    g_w_ada = ada_bwd(c_all, d_mod_rows.reshape(N_DEV, n_cols))

    row_g = dict(b_ada=d_mod, pre_mix_w=d_pre_mix_w, attn_sinks=dsinks, ssm_conv_b=d_ssm_cb, ssm_dt_bias=d_dt_bias,
                 ssm_a_log=d_a_log, ssm_d=d_dskip, ssm_norm_w=d_norm_w, post_mix_w=d_post_mix_w,
                 pre_ffn_w=d_pre_ffn_w, ffn_conv_b=d_ffn_cb, post_ffn_w=d_post_ffn_w)
    row = jnp.concatenate([loss_part] + [row_g[k] for k, _, _ in ROW_PARAMS], axis=1)
    row_all, rb_all = all_gather([row, d_rel_bias], "gather_small")

    wmv = lambda k: (weights[k], mom_m[k], mom_v[k])
    small = adamw_small(row_all, rb_all, wmv("rel_bias"), [wmv(k) for k, _, _ in ROW_PARAMS])
    loss = small[0][0, 0]
    res = {k: tuple(small[1 + 4 * i:5 + 4 * i]) for i, k in enumerate(["rel_bias"] + [k for k, _, _ in ROW_PARAMS])}
    big = list(zip(late, chip_parts)) + [("w_down", p_w_down), ("w_up", p_w_up), ("w_out", p_w_out),
                                        ("w_ada", g_w_ada[None])]
    for k, parts in big:
        res[k] = tuple(o[None] for o in adamw(parts, weights[k][0], mom_m[k][0], mom_v[k][0], "adamw_" + k))

    outs = [loss, grad_x[None]]
    for field in range(4):
        outs += [res[k][field] for k in order]
    return tuple(outs)
```

```python
import math

import numpy as np
import jax
import jax.numpy as jnp
from jax import lax
from jax.experimental import pallas as pl
from jax.experimental.pallas import tpu as pltpu

F32 = jnp.float32
BF16 = jnp.bfloat16
MESH_ID = pl.DeviceIdType.MESH

N_DEV = 8
D_MODEL = 1024
N_Q_HEADS = 8
N_KV_HEADS = 2
HEAD_DIM = 64
ATTN_WIDTH = 512
KV_WIDTH = 128
ATTN_BLOCK = 128
N_BUCKETS = 32
REL_MAX_DIST = 128
SSM_HEADS = 8
SSM_HEAD_DIM = 64
SSM_WIDTH = 512
SSM_STATE = 128
SSM_GROUPS = 2
SSM_BC = 256
SSM_CONV = 4
SSM_CHUNK = 256
XBC_WIDTH = SSM_WIDTH + 2 * SSM_BC
D_FF = 2816
FFN_CONV = 3
NORM_EPS = 1e-6
N_MOD = 6
IN_PROJ_WIDTH = 2312
QKV_W = ATTN_WIDTH + 2 * KV_WIDTH
OFF_XBC = QKV_W
OFF_Z = OFF_XBC + XBC_WIDTH
OFF_DT = OFF_Z + SSM_WIDTH
DT_PAD = 128
PROJ_PAD = OFF_DT + DT_PAD
FF_CHUNK = 1408

ADAM_LR = 0.001
ADAM_B1 = 0.9
ADAM_B2 = 0.999
ADAM_EPS = 1e-08
ADAM_WD = 0.01
ADAM_STEP = 10

TOKEN_TILE = 256
HALO = 8
NEXT = 16
VMEM_LIMIT = 56 * 1024 * 1024


def _params(vmem=VMEM_LIMIT, n_axes=1):
    return pltpu.CompilerParams(dimension_semantics=("arbitrary",) * n_axes, vmem_limit_bytes=vmem)


def _b(x):
    return x.astype(BF16)


def _nn(a, b):
    return jnp.dot(_b(a), _b(b), preferred_element_type=F32)


def _nt(a, b):
    return lax.dot_general(_b(a), _b(b), (((1,), (1,)), ((), ())), preferred_element_type=F32)


def _tn(a, b):
    return lax.dot_general(_b(a), _b(b), (((0,), (0,)), ((), ())), preferred_element_type=F32)


@jax.custom_vjp
def mm(a, b):
    return _nn(a, b)


mm.defvjp(lambda a, b: (_nn(a, b), (a, b)),
          lambda r, g: (_nt(g, r[1]).astype(r[0].dtype), _tn(r[0], g).astype(r[1].dtype)))


@jax.custom_vjp
def mm_nt(a, b):
    return _nt(a, b)


mm_nt.defvjp(lambda a, b: (_nt(a, b), (a, b)),
             lambda r, g: (_nn(g, r[1]).astype(r[0].dtype), _tn(g, r[0]).astype(r[1].dtype)))


@jax.custom_vjp
def mm_tn(a, b):
    return _tn(a, b)


mm_tn.defvjp(lambda a, b: (_tn(a, b), (a, b)),
             lambda r, g: (_nt(r[1], g).astype(r[0].dtype), _nn(r[0], g).astype(r[1].dtype)))


def _rms(x, w):
    return x * lax.rsqrt(jnp.mean(x * x, axis=-1, keepdims=True) + NORM_EPS) * w


def _norm_mod(x, w, scale, shift):
    return _rms(x, w) * (1.0 + scale) + shift


def _rms_bwd(x, w, dy):
    r = lax.rsqrt(jnp.mean(x * x, axis=-1, keepdims=True) + NORM_EPS)
    xhat = x * r
    g = dy * w
    dx = r * (g - xhat * jnp.mean(g * xhat, axis=-1, keepdims=True))
    return dx, jnp.sum(dy * xhat, axis=0, keepdims=True)


def _norm_mod_bwd(x, w, scale, dh):
    dx, da = _rms_bwd(x, w * (1.0 + scale), dh)
    return dx, da * (1.0 + scale), da * w, jnp.sum(dh, axis=0, keepdims=True)


def _gated_rms_bwd(m, gate, w, dy):
    dm, t = _rms_bwd(m, w * gate, dy)
    return dm, t * w, t * gate


def _silu(x):
    return x * jax.nn.sigmoid(x)


def _conv_rows(xin, w, k):
    acc = xin * w[k - 1:k, :]
    for j in range(1, k):
        acc = acc + pltpu.roll(xin, j, axis=0) * w[k - 1 - j:k - j, :]
    return acc


def _conv_rows_t(du, w, k):
    n = du.shape[0]
    acc = du * w[k - 1:k, :]
    for j in range(1, k):
        acc = acc + pltpu.roll(du, n - j, axis=0) * w[k - 1 - j:k - j, :]
    return acc


def _row(i):
    return (i, 0)


def _const(i):
    return (0, 0)


def _vec(n):
    return pl.BlockSpec((1, n), _const)


def _block_index(p):
    return 4 * p[0] + 2 * p[1] + p[2]


def all_gather(arrs, name):
    n = len(arrs)

    def body(*refs):
        ins, outs = refs[:n], refs[n:2 * n]
        send_sems, recv_sems, local_sems = refs[2 * n:]
        x, y, c = lax.axis_index("x"), lax.axis_index("y"), lax.axis_index("c")
        me, sibling = (x, y, c), (x, y, 1 - c)
        chips = [(1 - x, y), (x, 1 - y), (1 - x, 1 - y)]

        def copy(a, k, block, to, src=None):
            dst = outs[a].at[_block_index(block)]
            return pltpu.make_async_remote_copy(
                src_ref=dst if src is None else src, dst_ref=dst,
                send_sem=send_sems.at[a * 7 + k], recv_sem=recv_sems.at[a * 7 + k],
                device_id=to, device_id_type=MESH_ID)

        mine = [pltpu.make_async_copy(ins[a], outs[a].at[_block_index(me)], local_sems.at[a]) for a in range(n)]
        for cp in mine:
            cp.start()
        first = []
        for a in range(n):
            first.append(copy(a, 0, me, sibling, src=ins[a]))
            first += [copy(a, 1 + j, me, (*chip, c), src=ins[a]) for j, chip in enumerate(chips)]
        for cp in first:
            cp.start()
        passed = []
        for j, chip in enumerate(chips):
            for a in range(n):
                copy(a, 1 + j, (*chip, c), me).wait_recv()
                cp = copy(a, 4 + j, (*chip, c), sibling)
                cp.start()
                passed.append(cp)
        for a in range(n):
            copy(a, 0, sibling, me).wait_recv()
            for j, chip in enumerate(chips):
                copy(a, 4 + j, (*chip, 1 - c), me).wait_recv()
        for cp in first + passed:
            cp.wait_send()
        for cp in mine:
            cp.wait()

    any_spec = pl.BlockSpec(memory_space=pl.ANY)
    return pl.pallas_call(
        body, name=name,
        out_shape=[jax.ShapeDtypeStruct((N_DEV,) + a.shape, a.dtype) for a in arrs],
        in_specs=[any_spec] * n, out_specs=[any_spec] * n,
        scratch_shapes=[pltpu.SemaphoreType.DMA((7 * n,)), pltpu.SemaphoreType.DMA((7 * n,)),
                        pltpu.SemaphoreType.DMA((n,))],
    )(*arrs)


ALL_FLIPS = ((0, 0, 1), (0, 1, 0), (0, 1, 1), (1, 0, 0), (1, 0, 1), (1, 1, 0), (1, 1, 1))
CHIP_FLIPS = ((0, 1, 0), (1, 0, 0), (1, 1, 0))


def _chip_index(p):
    return 2 * p[0] + p[1]


def all_to_all(arrs, name, flips=ALL_FLIPS, index=_block_index):
    n = len(arrs)
    nf = len(flips)

    def body(*refs):
        ins, outs = refs[:n], refs[n:2 * n]
        send_sems, recv_sems, local_sems = refs[2 * n:]
        pos = (lax.axis_index("x"), lax.axis_index("y"), lax.axis_index("c"))
        me = index(pos)
        peers = [tuple(1 - p if f else p for p, f in zip(pos, flip)) for flip in flips]

        def copy(a, k):
            peer = peers[k]
            return pltpu.make_async_remote_copy(
                src_ref=ins[a].at[index(peer)], dst_ref=outs[a].at[me],
                send_sem=send_sems.at[a * nf + k], recv_sem=recv_sems.at[a * nf + k],
                device_id=peer, device_id_type=MESH_ID)

        def landed(a, k):
            slot = outs[a].at[index(peers[k])]
            return pltpu.make_async_remote_copy(
                src_ref=slot, dst_ref=slot,
                send_sem=send_sems.at[a * nf + k], recv_sem=recv_sems.at[a * nf + k],
                device_id=peers[k], device_id_type=MESH_ID)

        mine = [pltpu.make_async_copy(ins[a].at[me], outs[a].at[me], local_sems.at[a]) for a in range(n)]
        for cp in mine:
            cp.start()
        sent = [copy(a, k) for a in range(n) for k in range(nf)]
        for cp in sent:
            cp.start()
        for a in range(n):
            for k in range(nf):
                landed(a, k).wait_recv()
        for cp in sent:
            cp.wait_send()
        for cp in mine:
            cp.wait()

    any_spec = pl.BlockSpec(memory_space=pl.ANY)
    return pl.pallas_call(
        body, name=name,
        out_shape=[jax.ShapeDtypeStruct(a.shape, a.dtype) for a in arrs],
        in_specs=[any_spec] * n, out_specs=[any_spec] * n,
        scratch_shapes=[pltpu.SemaphoreType.DMA((nf * n,)), pltpu.SemaphoreType.DMA((nf * n,)),
                        pltpu.SemaphoreType.DMA((n,))],
    )(*arrs)


def _direct_exchange(src, dst, sems, scatter):
    send_sems, recv_sems, local_sem = sems
    pos = (lax.axis_index("x"), lax.axis_index("y"), lax.axis_index("c"))
    me = _block_index(pos)
    peers = [tuple(1 - p if f else p for p, f in zip(pos, flip)) for flip in ALL_FLIPS]

    def outgoing(k):
        return pltpu.make_async_remote_copy(
            src_ref=src.at[_block_index(peers[k])] if scatter else src, dst_ref=dst.at[me],
            send_sem=send_sems.at[k], recv_sem=recv_sems.at[k], device_id=peers[k], device_id_type=MESH_ID)

    def incoming(k):
        slot = dst.at[_block_index(peers[k])]
        return pltpu.make_async_remote_copy(
            src_ref=slot, dst_ref=slot, send_sem=send_sems.at[k], recv_sem=recv_sems.at[k],
            device_id=peers[k], device_id_type=MESH_ID)

    def local():
        return pltpu.make_async_copy(src.at[me] if scatter else src, dst.at[me], local_sem)

    def start():
        local().start()
        for k in range(len(ALL_FLIPS)):
            outgoing(k).start()

    def finish():
        for k in range(len(ALL_FLIPS)):
            incoming(k).wait_recv()
        for k in range(len(ALL_FLIPS)):
            outgoing(k).wait_send()
        local().wait()

    return start, finish


def hosted_call(body, exchanges, steps, n_in, n_out, **call):
    n_ex = len(exchanges)

    def wrapped(*refs):
        ins, srcs = refs[:n_in], refs[n_in:n_in + n_ex]
        outs = refs[n_in + n_ex:n_in + n_ex + n_out]
        dsts = refs[n_in + n_ex + n_out:n_in + 2 * n_ex + n_out]
        rest = refs[n_in + 2 * n_ex + n_out:]
        scratch, sems = rest[:len(rest) - 3 * n_ex], rest[len(rest) - 3 * n_ex:]
        plans = [_direct_exchange(srcs[e], dsts[e], sems[3 * e:3 * e + 3], exchanges[e][1]) for e in range(n_ex)]

        @pl.when(pl.program_id(0) == 0)
        def _():
            for start, _ in plans:
                start()

        body(*ins, *outs, *scratch)

        @pl.when(pl.program_id(0) == steps - 1)
        def _():
            for _, finish in plans:
                finish()

    any_spec = pl.BlockSpec(memory_space=pl.ANY)
    landings = [jax.ShapeDtypeStruct(src.shape if scatter else (N_DEV,) + src.shape, src.dtype)
                for src, scatter in exchanges]
    n_flips = len(ALL_FLIPS)
    sems = [pltpu.SemaphoreType.DMA((n_flips,)), pltpu.SemaphoreType.DMA((n_flips,)), pltpu.SemaphoreType.DMA(())]
    return pl.pallas_call(
        wrapped, grid=(steps,),
        in_specs=list(call.pop("in_specs")) + [any_spec] * n_ex,
        out_specs=list(call.pop("out_specs")) + [any_spec] * n_ex,
        out_shape=list(call.pop("out_shape")) + landings,
        scratch_shapes=list(call.pop("scratch_shapes", [])) + sems * n_ex,
        **call)


def _grid_call(body, steps, args, exchanges, **call):
    if not exchanges:
        return pl.pallas_call(body, grid=(steps,), **call)(*args)
    srcs = [src for src, _ in exchanges]
    return hosted_call(body, exchanges, steps, len(args), len(call["out_shape"]), **call)(*args, *srcs)


N_CHIPS = 4


def pair_exchange(arrs, name):
    n = len(arrs)

    def body(*refs):
        ins, outs = refs[:n], refs[n:2 * n]
        send_sems, recv_sems = refs[2 * n:]
        x, y, c = lax.axis_index("x"), lax.axis_index("y"), lax.axis_index("c")
        sibling = (x, y, 1 - c)
        sent = []
        for a in range(n):
            for q in range(N_CHIPS):
                cp = pltpu.make_async_remote_copy(
                    src_ref=ins[a].at[2 * q + (1 - c)], dst_ref=outs[a].at[q],
                    send_sem=send_sems.at[a * N_CHIPS + q], recv_sem=recv_sems.at[a * N_CHIPS + q],
                    device_id=sibling, device_id_type=MESH_ID)
                cp.start()
                sent.append(cp)
        for cp in sent:
            cp.wait_recv()
        for cp in sent:
            cp.wait_send()

    any_spec = pl.BlockSpec(memory_space=pl.ANY)
    return pl.pallas_call(
        body, name=name,
        out_shape=[jax.ShapeDtypeStruct((N_CHIPS,) + a.shape[1:], a.dtype) for a in arrs],
        in_specs=[any_spec] * n, out_specs=[any_spec] * n,
        scratch_shapes=[pltpu.SemaphoreType.DMA((N_CHIPS * n,)), pltpu.SemaphoreType.DMA((N_CHIPS * n,))],
    )(*arrs)


def pair_sum(full, got, core, name):
    _, r, n = full.shape
    tr = _row_tile(r)

    def body(c_ref, mine_ref, got_ref, o_ref):
        o_ref[...] = _b(mine_ref[...] + got_ref[...])

    grid_spec = pltpu.PrefetchScalarGridSpec(
        num_scalar_prefetch=1, grid=(N_CHIPS, r // tr),
        in_specs=[pl.BlockSpec((1, tr, n), lambda q, i, c_ref: (2 * q + c_ref[0], i, 0)),
                  pl.BlockSpec((1, tr, n), lambda q, i, c_ref: (q, i, 0))],
        out_specs=pl.BlockSpec((1, tr, n), lambda q, i, c_ref: (q, i, 0)))
    return pl.pallas_call(body, name=name, grid_spec=grid_spec,
                          out_shape=jax.ShapeDtypeStruct((N_CHIPS, r, n), BF16),
                          compiler_params=_params(n_axes=2))(core, full, got)


def _row_tile(r):
    for cand in (256, 128, 64, 32, 16):
        if r % cand == 0 and r > cand:
            return cand
    return r


def ada_fwd(c_all, w_ada, b_cols):
    def body(c_ref, w_ref, b_ref, o_ref):
        o_ref[...] = _nn(_silu(c_ref[...]), w_ref[...]) + b_ref[...]

    return pl.pallas_call(body, name="ada_fwd",
                          out_shape=jax.ShapeDtypeStruct((N_DEV, w_ada.shape[1]), F32),
                          compiler_params=_params(n_axes=0))(c_all, w_ada, b_cols)


def ada_bwd(c_all, g_cols):
    def body(c_ref, g_ref, o_ref):
        o_ref[...] = _tn(_silu(c_ref[...]), g_ref[...])

    return pl.pallas_call(body, name="ada_bwd",
                          out_shape=jax.ShapeDtypeStruct((c_all.shape[1], g_cols.shape[1]), F32),
                          compiler_params=_params(n_axes=0))(c_all, g_cols)


def matmul_tn(a, b, name, bm, bn, tk=512):
    s, m = a.shape
    n = b.shape[1]
    tk = min(tk, s)

    def body(a_ref, b_ref, o_ref):
        @pl.when(pl.program_id(2) == 0)
        def _():
            o_ref[...] = jnp.zeros_like(o_ref)

        o_ref[...] += _tn(a_ref[...], b_ref[...])

    return pl.pallas_call(
        body, name=name, grid=(m // bm, n // bn, s // tk),
        in_specs=[pl.BlockSpec((tk, bm), lambda i, j, k: (k, i)), pl.BlockSpec((tk, bn), lambda i, j, k: (k, j))],
        out_specs=pl.BlockSpec((bm, bn), lambda i, j, k: (i, j)),
        out_shape=jax.ShapeDtypeStruct((m, n), F32),
        compiler_params=_params(n_axes=3))(a, b)


def pre_mix_inproj(x, w, scale, shift, w_in16, exchange=None, tm=TOKEN_TILE):
    s = x.shape[0]

    def body(x_ref, w_ref, sc_ref, sh_ref, win_ref, h_ref, qkv_ref, xbc_ref, z_ref, dt_ref):
        h16 = _b(_norm_mod(x_ref[...], w_ref[...], sc_ref[...], sh_ref[...]))
        h_ref[...] = h16
        dot = lambda lo, hi: jnp.dot(h16, win_ref[:, lo:hi], preferred_element_type=F32)
        qkv_ref[...] = _b(dot(0, OFF_XBC))
        xbc_ref[...] = dot(OFF_XBC, OFF_Z)
        z_ref[...] = dot(OFF_Z, OFF_DT)
        dt_ref[...] = dot(OFF_DT, PROJ_PAD)

    tile = lambda n: pl.BlockSpec((tm, n), _row)
    return _grid_call(
        body, s // tm, (x, w, scale, shift, w_in16), exchange, name="pre_mix_inproj",
        in_specs=[tile(D_MODEL), _vec(D_MODEL), _vec(D_MODEL), _vec(D_MODEL), pl.BlockSpec((D_MODEL, PROJ_PAD), _const)],
        out_specs=[tile(D_MODEL), tile(QKV_W), tile(XBC_WIDTH), tile(SSM_WIDTH), tile(DT_PAD)],
        out_shape=[jax.ShapeDtypeStruct((s, D_MODEL), BF16), jax.ShapeDtypeStruct((s, QKV_W), BF16),
                   jax.ShapeDtypeStruct((s, XBC_WIDTH), F32), jax.ShapeDtypeStruct((s, SSM_WIDTH), F32),
                   jax.ShapeDtypeStruct((s, DT_PAD), F32)],
        compiler_params=_params())


ATTN_QB = 2


def _attn_tile(q, kp, kc, vp, vc, bias, sinks):
    lq = ATTN_BLOCK
    group = N_Q_HEADS // N_KV_HEADS
    lanes = lax.broadcasted_iota(jnp.int32, (1, 128), 1)
    rid = lax.broadcasted_iota(jnp.int32, (group * lq, 1), 0)
    sink_cols = []
    for hk in range(N_KV_HEADS):
        sink = jnp.zeros((group * lq, 1), F32)
        for g in range(group):
            s_h = jnp.sum(jnp.where(lanes == hk * group + g, sinks, 0.0), axis=-1, keepdims=True)
            sink = jnp.where((rid >= g * lq) & (rid < (g + 1) * lq), s_h, sink)
        sink_cols.append(sink)
    kall = jnp.concatenate([kp, kc], axis=0)
    vall = jnp.concatenate([vp, vc], axis=0)
    blocks = []
    for b in range(ATTN_QB):
        qb = q[b * lq:(b + 1) * lq]
        outs = []
        for hk in range(N_KV_HEADS):
            cols = slice(hk * HEAD_DIM, (hk + 1) * HEAD_DIM)
            kb = kall[b * lq:(b + 2) * lq, cols]
            vb = vall[b * lq:(b + 2) * lq, cols]
            qg = jnp.concatenate([qb[:, (hk * group + g) * HEAD_DIM:(hk * group + g + 1) * HEAD_DIM]
                                  for g in range(group)], axis=0)
            sc = mm_nt(qg, kb) * (HEAD_DIM ** -0.5) + bias[b][hk]
            sink = sink_cols[hk]
            m = lax.stop_gradient(jnp.maximum(jnp.max(sc, axis=-1, keepdims=True), sink))
            p = jnp.exp(sc - m)
            probs = p / (jnp.sum(p, axis=-1, keepdims=True) + jnp.exp(sink - m))
            og = mm(probs, vb)
            outs += [og[g * lq:(g + 1) * lq] for g in range(group)]
        blocks.append(jnp.concatenate(outs, axis=1))
    return jnp.concatenate(blocks, axis=0)


def _attn_in_specs(nt, clamp):
    lq, tq = ATTN_BLOCK, ATTN_BLOCK * ATTN_QB
    cur = lambda n: jnp.minimum(n, nt - 1) if clamp else n
    prev = lambda n: jnp.maximum(cur(n) * ATTN_QB - 1, 0)
    kcol, vcol = ATTN_WIDTH // KV_WIDTH, ATTN_WIDTH // KV_WIDTH + 1
    return [pl.BlockSpec((tq, ATTN_WIDTH), lambda n: (cur(n), 0)),
            pl.BlockSpec((lq, KV_WIDTH), lambda n: (prev(n), kcol)),
            pl.BlockSpec((tq, KV_WIDTH), lambda n: (cur(n), kcol)),
            pl.BlockSpec((lq, KV_WIDTH), lambda n: (prev(n), vcol)),
            pl.BlockSpec((tq, KV_WIDTH), lambda n: (cur(n), vcol)),
            pl.BlockSpec((2, N_KV_HEADS, 4 * lq, 2 * lq), lambda n: (0, 0, 0, 0)),
            _vec(128)]


def _tile_bias(bias_ref, first):
    return [[jnp.where(first, bias_ref[1, hk], bias_ref[0, hk]) if b == 0 else bias_ref[0, hk]
             for hk in range(N_KV_HEADS)] for b in range(ATTN_QB)]


def attn_fwd(qkv, bias, sinks_rows, exchange=None):
    s = qkv.shape[0]
    tq = ATTN_BLOCK * ATTN_QB
    nt = s // tq

    def body(q_ref, kp_ref, kc_ref, vp_ref, vc_ref, bias_ref, sk_ref, o_ref):
        f = lambda r: r[...].astype(F32)
        o = _attn_tile(f(q_ref), f(kp_ref), f(kc_ref), f(vp_ref), f(vc_ref),
                       _tile_bias(bias_ref, pl.program_id(0) == 0), sk_ref[...])
        o_ref[...] = _b(o)

    return _grid_call(
        body, nt, (qkv, qkv, qkv, qkv, qkv, bias, sinks_rows), exchange, name="attn_fwd",
        in_specs=_attn_in_specs(nt, False),
        out_specs=[pl.BlockSpec((tq, ATTN_WIDTH), _row)],
        out_shape=[jax.ShapeDtypeStruct((s, ATTN_WIDTH), BF16)],
        compiler_params=_params())


def attn_bwd(qkv, bias, sinks_rows, d_attn, exchange=None):
    s = qkv.shape[0]
    lq, tq = ATTN_BLOCK, ATTN_BLOCK * ATTN_QB
    nt = s // tq

    def body(q_ref, kp_ref, kc_ref, vp_ref, vc_ref, bias_ref, sk_ref, do_ref,
             dq_ref, dk_ref, dv_ref, dbias_ref, dsk_ref, carry_k, carry_v):
        n = pl.program_id(0)

        @pl.when(n == 0)
        def _():
            dbias_ref[...] = jnp.zeros_like(dbias_ref)
            dsk_ref[...] = jnp.zeros_like(dsk_ref)
            carry_k[...] = jnp.zeros_like(carry_k)
            carry_v[...] = jnp.zeros_like(carry_v)

        @pl.when(n < nt)
        def _():
            f = lambda r: r[...].astype(F32)
            _, vjp = jax.vjp(_attn_tile, f(q_ref), f(kp_ref), f(kc_ref), f(vp_ref), f(vc_ref),
                             _tile_bias(bias_ref, n == 0), sk_ref[...])
            dq, dkp, dkc, dvp, dvc, dbias, dsk = vjp(f(do_ref))
            dq_ref[...] = _b(dq)
            done = tq - lq
            dk_ref[:done, :] = _b(carry_k[:done, :])
            dv_ref[:done, :] = _b(carry_v[:done, :])
            dk_ref[done:, :] = _b(carry_k[done:, :] + dkp)
            dv_ref[done:, :] = _b(carry_v[done:, :] + dvp)
            carry_k[...] = dkc
            carry_v[...] = dvc
            dsk_ref[...] += dsk
            first = (n == 0).astype(F32)
            for hk in range(N_KV_HEADS):
                total = dbias[0][hk]
                for b in range(1, ATTN_QB):
                    total = total + dbias[b][hk]
                dbias_ref[0, hk] += total - first * dbias[0][hk]
                dbias_ref[1, hk] += first * dbias[0][hk]

        @pl.when(n == nt)
        def _():
            dk_ref[...] = _b(carry_k[...])
            dv_ref[...] = _b(carry_v[...])

    cur = lambda n: (jnp.minimum(n, nt - 1), 0)
    done_map = lambda n: (jnp.maximum(n - 1, 0), 0)
    return _grid_call(
        body, nt + 1, (qkv, qkv, qkv, qkv, qkv, bias, sinks_rows, d_attn), exchange, name="attn_bwd",
        in_specs=_attn_in_specs(nt, True) + [pl.BlockSpec((tq, ATTN_WIDTH), cur)],
        out_specs=[pl.BlockSpec((tq, ATTN_WIDTH), cur), pl.BlockSpec((tq, KV_WIDTH), done_map),
                   pl.BlockSpec((tq, KV_WIDTH), done_map),
                   pl.BlockSpec((2, N_KV_HEADS, 4 * lq, 2 * lq), lambda n: (0, 0, 0, 0)), _vec(128)],
        out_shape=[jax.ShapeDtypeStruct((s, ATTN_WIDTH), BF16), jax.ShapeDtypeStruct((s, KV_WIDTH), BF16),
                   jax.ShapeDtypeStruct((s, KV_WIDTH), BF16),
                   jax.ShapeDtypeStruct((2, N_KV_HEADS, 4 * lq, 2 * lq), F32), jax.ShapeDtypeStruct((1, 128), F32)],
        scratch_shapes=[pltpu.VMEM((tq, KV_WIDTH), F32), pltpu.VMEM((tq, KV_WIDTH), F32)],
        compiler_params=_params())


def rel_bias_table(rel_bias, bucket):
    lq = ATTN_BLOCK
    group = N_Q_HEADS // N_KV_HEADS

    def body(rb_ref, bk_ref, o_ref):
        bk = bk_ref[...]
        prev_keys = lax.broadcasted_iota(jnp.int32, bk.shape, 1) < lq
        accs = [jnp.full(bk.shape, -1e30, F32) for _ in range(N_Q_HEADS)]
        for b in range(N_BUCKETS):
            hit = bk == b
            accs = [jnp.where(hit, rb_ref[b, h], acc) for h, acc in enumerate(accs)]
        for h in range(N_Q_HEADS):
            rows = slice((h % group) * lq, (h % group + 1) * lq)
            o_ref[0, h // group, rows, :] = accs[h]
            o_ref[1, h // group, rows, :] = jnp.where(prev_keys, -1e30, accs[h])

    return pl.pallas_call(
        body, name="rel_bias_table",
        in_specs=[pl.BlockSpec(memory_space=pltpu.SMEM), pl.BlockSpec(memory_space=pltpu.VMEM)],
        out_shape=jax.ShapeDtypeStruct((2, N_KV_HEADS, group * lq, 2 * lq), F32),
        compiler_params=_params(n_axes=0))(rel_bias, bucket)


def rel_bias_grad(dbias, bucket):
    lq = ATTN_BLOCK
    group = N_Q_HEADS // N_KV_HEADS

    def body(db_ref, bk_ref, o_ref):
        rows = lax.broadcasted_iota(jnp.int32, (N_BUCKETS, 128), 0)
        lanes = lax.broadcasted_iota(jnp.int32, (N_BUCKETS, 128), 1)
        bk = bk_ref[...]
        per_head = []
        for h in range(N_Q_HEADS):
            sl = slice((h % group) * lq, (h % group + 1) * lq)
            per_head.append(db_ref[0, h // group, sl, :] + db_ref[1, h // group, sl, :])

        def per_bucket(b, acc):
            hit = (bk == b).astype(F32)
            for h in range(N_Q_HEADS):
                val = jnp.sum(per_head[h] * hit, keepdims=True)
                acc = acc + jnp.where((rows == b) & (lanes == h), val, 0.0)
            return acc

        o_ref[...] = lax.fori_loop(0, N_BUCKETS, per_bucket, jnp.zeros((N_BUCKETS, 128), F32))

    return pl.pallas_call(body, name="rel_bias_grad", out_shape=jax.ShapeDtypeStruct((N_BUCKETS, 128), F32),
                          compiler_params=_params(n_axes=0))(dbias, bucket)


def _tri_sum(a, upper):
    n = a.shape[0]
    ri = lax.broadcasted_iota(jnp.int32, (n, n), 0)
    ci = lax.broadcasted_iota(jnp.int32, (n, n), 1)
    tri = ((ri <= ci) if upper else (ri >= ci)).astype(BF16)
    hi = a.astype(BF16)
    rest = a - hi.astype(F32)
    mid = rest.astype(BF16)
    lo = (rest - mid.astype(F32)).astype(BF16)
    dot = lambda part: jnp.dot(tri, part, preferred_element_type=F32)
    return dot(hi) + dot(mid) + dot(lo)


@jax.custom_vjp
def _cumsum_rows(a):
    return _tri_sum(a, False)


_cumsum_rows.defvjp(lambda a: (_tri_sum(a, False), None), lambda _, g: (_tri_sum(g, True),))


def _ssm_core(u, z, dt_raw, hprev, dt_bias, a_log, dskip, norm_w):
    lc = u.shape[0]
    xbc = _silu(u)
    xs, bm, cm = xbc[:, :SSM_WIDTH], xbc[:, SSM_WIDTH:SSM_WIDTH + SSM_BC], xbc[:, SSM_WIDTH + SSM_BC:]
    dt = jax.nn.softplus(dt_raw + dt_bias)
    adt = dt * (-jnp.exp(a_log))
    ri = lax.broadcasted_iota(jnp.int32, (lc, lc), 0)
    ci = lax.broadcasted_iota(jnp.int32, (lc, lc), 1)
    causal = ri >= ci
    acum = _cumsum_rows(adt)
    acum_t = acum.T
    last = acum[lc - 1:lc, :]
    per_group = SSM_HEADS // SSM_GROUPS
    lane = lax.broadcasted_iota(jnp.int32, (1, 128), 1)
    rowid = lax.broadcasted_iota(jnp.int32, (128, 1), 0)
    lo_lanes = lane < SSM_HEAD_DIM
    ys, hs = [], []
    for g in range(SSM_GROUPS):
        bg = bm[:, g * SSM_STATE:(g + 1) * SSM_STATE]
        cg = cm[:, g * SSM_STATE:(g + 1) * SSM_STATE]
        cb = mm_nt(cg, bg)
        for pp in range(per_group // 2):
            ha = g * per_group + 2 * pp
            xp = xs[:, ha * SSM_HEAD_DIM:(ha + 2) * SSM_HEAD_DIM]
            hp = hprev[ha * SSM_HEAD_DIM:(ha + 2) * SSM_HEAD_DIM, :]
            xcp = xp * jnp.where(lo_lanes, dt[:, ha:ha + 1], dt[:, ha + 1:ha + 2])
            y_h, st_h = [], []
            for h in (ha, ha + 1):
                col, rowv, lasth = acum[:, h:h + 1], acum_t[h:h + 1, :], last[:, h:h + 1]
                decay = jnp.exp(jnp.where(causal, col - rowv, -1e30))
                y_h.append(mm(cb * decay, xcp) + mm_nt(cg * jnp.exp(col), hp))
                st_h.append(mm_tn(xcp, bg * jnp.exp(lasth - col)))
            y_pair = jnp.where(lo_lanes, y_h[0], y_h[1])
            st_pair = jnp.where(rowid < SSM_HEAD_DIM, st_h[0], st_h[1])
            la, lb = last[:, ha:ha + 1], last[:, ha + 1:ha + 2]
            hs.append(jnp.exp(jnp.where(rowid < SSM_HEAD_DIM, la, lb)) * hp + st_pair)
            dsk = jnp.where(lo_lanes, dskip[:, ha:ha + 1], dskip[:, ha + 1:ha + 2])
            ys.append(y_pair + dsk * xp)
    y = jnp.concatenate(ys, axis=1) * _silu(z)
    gw = SSM_WIDTH // SSM_GROUPS
    outs = []
    for g in range(SSM_GROUPS):
        yg = y[:, g * gw:(g + 1) * gw]
        outs.append(yg * lax.rsqrt(jnp.mean(yg * yg, axis=-1, keepdims=True) + NORM_EPS))
    return jnp.concatenate(outs, axis=1) * norm_w, jnp.concatenate(hs, axis=0)


def _ssm_param_specs():
    return [pl.BlockSpec((SSM_CONV, XBC_WIDTH), _const), _vec(XBC_WIDTH), _vec(128), _vec(128), _vec(128),
            _vec(SSM_WIDTH)]


def ssm_fwd(xbc_raw, z, dt_raw, conv_w, conv_b, dt_bias, a_log, dskip, norm_w, exchange=None):
    s = xbc_raw.shape[0]
    lc = SSM_CHUNK
    nc = s // lc
    hrows = SSM_HEADS * SSM_HEAD_DIM

    def body(x_ref, halo_ref, z_ref, dt_ref, cw_ref, cb_ref, dtb_ref, al_ref, dk_ref, nw_ref,
             o_ref, hp_ref, state):
        i = pl.program_id(0)

        @pl.when(i == 0)
        def _():
            state[...] = jnp.zeros_like(state)

        halo = halo_ref[...] * (i > 0).astype(F32)
        xin = jnp.concatenate([halo, x_ref[...]], axis=0)
        u = (_conv_rows(xin, cw_ref[...], SSM_CONV) + cb_ref[...])[HALO:]
        hprev = state[...]
        hp_ref[...] = hprev
        out, hnew = _ssm_core(u, z_ref[...], dt_ref[...], hprev, dtb_ref[...], al_ref[...], dk_ref[...], nw_ref[...])
        o_ref[...] = _b(out)
        state[...] = hnew

    tile = lambda n: pl.BlockSpec((lc, n), _row)
    halo_spec = pl.BlockSpec((HALO, XBC_WIDTH), lambda i: (jnp.maximum(i * (lc // HALO) - 1, 0), 0))
    return _grid_call(
        body, nc, (xbc_raw, xbc_raw, z, dt_raw, conv_w, conv_b, dt_bias, a_log, dskip, norm_w), exchange,
        name="ssm_fwd",
        in_specs=[tile(XBC_WIDTH), halo_spec, tile(SSM_WIDTH), tile(DT_PAD)] + _ssm_param_specs(),
        out_specs=[tile(SSM_WIDTH), pl.BlockSpec((hrows, SSM_STATE), _row)],
        out_shape=[jax.ShapeDtypeStruct((s, SSM_WIDTH), BF16), jax.ShapeDtypeStruct((nc * hrows, SSM_STATE), F32)],
        scratch_shapes=[pltpu.VMEM((hrows, SSM_STATE), F32)],
        compiler_params=_params())


def ssm_bwd(xbc_raw, z, dt_raw, hprev_all, d_out, conv_w, conv_b, dt_bias, a_log, dskip, norm_w, exchange=None):
    s = xbc_raw.shape[0]
    lc = SSM_CHUNK
    nc = s // lc
    hrows = SSM_HEADS * SSM_HEAD_DIM

    def body(x_ref, halo_ref, z_ref, dt_ref, hp_ref, do_ref, cw_ref, cb_ref, dtb_ref, al_ref, dk_ref, nw_ref,
             dx_ref, dz_ref, ddt_ref, dcw_ref, dcb_ref, ddtb_ref, dal_ref, ddk_ref, dnw_ref, dstate, du_next):
        i = pl.program_id(0)
        chunk = nc - 1 - i

        @pl.when(i == 0)
        def _():
            dstate[...] = jnp.zeros_like(dstate)
            du_next[...] = jnp.zeros_like(du_next)
            for r in (dcw_ref, dcb_ref, ddtb_ref, dal_ref, ddk_ref, dnw_ref):
                r[...] = jnp.zeros_like(r)

        halo = halo_ref[...] * (chunk > 0).astype(F32)
        xin = jnp.concatenate([halo, x_ref[...]], axis=0)
        cw = cw_ref[...]
        u = (_conv_rows(xin, cw, SSM_CONV) + cb_ref[...])[HALO:]
        _, vjp = jax.vjp(_ssm_core, u, z_ref[...], dt_ref[...], hp_ref[...], dtb_ref[...], al_ref[...],
                         dk_ref[...], nw_ref[...])
        du, dz, ddt, dhp, ddtb, dal, ddk, dnw = vjp((do_ref[...], dstate[...]))
        dstate[...] = dhp
        dz_ref[...] = _b(dz)
        ddt_ref[...] = _b(ddt)
        du_ext = jnp.concatenate([du, du_next[...]], axis=0)
        dx_ref[...] = _b(_conv_rows_t(du_ext, cw, SSM_CONV)[:lc])
        du_next[...] = du[:HALO]
        rows = [jnp.sum(du * pltpu.roll(xin, j, axis=0)[HALO:] if j else du * xin[HALO:], axis=0, keepdims=True)
                for j in range(SSM_CONV)]
        dcw_ref[...] += jnp.concatenate(rows[::-1] + [jnp.zeros((8 - SSM_CONV, XBC_WIDTH), F32)], axis=0)
        dcb_ref[...] += jnp.sum(du, axis=0, keepdims=True)
        ddtb_ref[...] += ddtb
        dal_ref[...] += dal
        ddk_ref[...] += ddk
        dnw_ref[...] += dnw

    rev = lambda i: (nc - 1 - i, 0)
    tile = lambda n: pl.BlockSpec((lc, n), rev)
    halo_spec = pl.BlockSpec((HALO, XBC_WIDTH), lambda i: (jnp.maximum((nc - 1 - i) * (lc // HALO) - 1, 0), 0))
    acc = lambda r, n: pl.BlockSpec((r, n), _const)
    return _grid_call(
        body, nc, (xbc_raw, xbc_raw, z, dt_raw, hprev_all, d_out, conv_w, conv_b, dt_bias, a_log, dskip, norm_w),
        exchange, name="ssm_bwd",
        in_specs=[tile(XBC_WIDTH), halo_spec, tile(SSM_WIDTH), tile(DT_PAD), pl.BlockSpec((hrows, SSM_STATE), rev),
                  tile(SSM_WIDTH)] + _ssm_param_specs(),
        out_specs=[tile(XBC_WIDTH), tile(SSM_WIDTH), tile(DT_PAD), acc(8, XBC_WIDTH), acc(1, XBC_WIDTH),
                   acc(1, 128), acc(1, 128), acc(1, 128), acc(1, SSM_WIDTH)],
        out_shape=[jax.ShapeDtypeStruct((s, XBC_WIDTH), BF16), jax.ShapeDtypeStruct((s, SSM_WIDTH), BF16),
                   jax.ShapeDtypeStruct((s, DT_PAD), BF16), jax.ShapeDtypeStruct((8, XBC_WIDTH), F32),
                   jax.ShapeDtypeStruct((1, XBC_WIDTH), F32), jax.ShapeDtypeStruct((1, 128), F32),
                   jax.ShapeDtypeStruct((1, 128), F32), jax.ShapeDtypeStruct((1, 128), F32),
                   jax.ShapeDtypeStruct((1, SSM_WIDTH), F32)],
        scratch_shapes=[pltpu.VMEM((hrows, SSM_STATE), F32), pltpu.VMEM((HALO, XBC_WIDTH), F32)],
        compiler_params=_params())


def mix_out(attn, ssm, x, w_out16, gate1, post_mix_w, pre_ffn_w, scale2, shift2, tm=TOKEN_TILE):
    s = x.shape[0]

    def body(a_ref, s_ref, x_ref, w_ref, g_ref, pw_ref, fw_ref, sc_ref, sh_ref, mixed_ref, x1_ref, h2_ref):
        mixed = (jnp.dot(a_ref[...], w_ref[:ATTN_WIDTH, :], preferred_element_type=F32)
                 + jnp.dot(s_ref[...], w_ref[ATTN_WIDTH:, :], preferred_element_type=F32))
        mixed_ref[...] = mixed
        x1 = x_ref[...] + g_ref[...] * _rms(mixed, pw_ref[...])
        x1_ref[...] = x1
        h2_ref[...] = _b(_norm_mod(x1, fw_ref[...], sc_ref[...], sh_ref[...]))

    tile = lambda n: pl.BlockSpec((tm, n), _row)
    return pl.pallas_call(
        body, name="mix_out", grid=(s // tm,),
        in_specs=[tile(ATTN_WIDTH), tile(SSM_WIDTH), tile(D_MODEL), pl.BlockSpec((D_MODEL, D_MODEL), _const)]
        + [_vec(D_MODEL)] * 5,
        out_specs=[tile(D_MODEL)] * 3,
        out_shape=[jax.ShapeDtypeStruct((s, D_MODEL), F32), jax.ShapeDtypeStruct((s, D_MODEL), F32),
                   jax.ShapeDtypeStruct((s, D_MODEL), BF16)],
        compiler_params=_params())(attn, ssm, x, w_out16, gate1, post_mix_w, pre_ffn_w, scale2, shift2)


def _gate(ug, uv):
    return jax.nn.gelu(ug, approximate=True) * uv


def _gate_bwd(ug, uv, df):
    k0, k1 = math.sqrt(2.0 / math.pi), 0.044715
    sq = ug * ug
    t = jnp.tanh(k0 * ug * (1.0 + k1 * sq))
    half = 0.5 * (1.0 + t)
    slope = half + (0.5 * k0) * ug * (1.0 - t * t) * (1.0 + (3.0 * k1) * sq)
    return df * uv * slope, df * (ug * half)


def _resident(shape):
    return pl.BlockSpec(shape, _const, pipeline_mode=pl.Buffered(1))


def up_gate(h2, w_up16, conv_w, conv_b, tm=TOKEN_TILE):
    s = h2.shape[0]

    def body(h_ref, halo_ref, w_ref, cw_ref, cb_ref, u_ref, uraw_ref, f_ref):
        halo = halo_ref[...]
        halo = jnp.where(pl.program_id(0) > 0, halo, jnp.zeros_like(halo))
        hin = jnp.concatenate([halo, h_ref[...]], axis=0)
        for lo in range(0, D_FF, FF_CHUNK):
            halves = []
            for base in (lo, D_FF + lo):
                cols = slice(base, base + FF_CHUNK)
                uraw = jnp.dot(hin, w_ref[:, cols], preferred_element_type=F32)
                uraw_ref[:, cols] = _b(uraw[NEXT:])
                u = (_conv_rows(uraw, cw_ref[:, cols], FFN_CONV) + cb_ref[:, cols])[NEXT:]
                u_ref[:, cols] = _b(u)
                halves.append(u)
            f_ref[:, lo:lo + FF_CHUNK] = _b(_gate(*halves))

    tile = lambda n: pl.BlockSpec((tm, n), _row)
    halo_spec = pl.BlockSpec((NEXT, D_MODEL), lambda i: (jnp.maximum(i * (tm // NEXT) - 1, 0), 0))
    return pl.pallas_call(
        body, name="up_gate", grid=(s // tm,),
        in_specs=[tile(D_MODEL), halo_spec, _resident((D_MODEL, 2 * D_FF)),
                  pl.BlockSpec((FFN_CONV, 2 * D_FF), _const), _vec(2 * D_FF)],
        out_specs=[tile(2 * D_FF), tile(2 * D_FF), tile(D_FF)],
        out_shape=[jax.ShapeDtypeStruct((s, 2 * D_FF), BF16), jax.ShapeDtypeStruct((s, 2 * D_FF), BF16),
                   jax.ShapeDtypeStruct((s, D_FF), BF16)],
        compiler_params=_params())(h2, h2, w_up16, conv_w, conv_b)


DOWN_LOSS_TILE = 512


def down_loss(f16, w_down16, x1, target, gate2, post_ffn_w, tm=DOWN_LOSS_TILE):
    s = x1.shape[0]
    tm = min(tm, s)

    def body(f_ref, wd_ref, x1_ref, t_ref, g_ref, pw_ref, dffn_ref, dy_ref, loss_ref, dg_ref, dpw_ref, gw_ref):
        i = pl.program_id(0)

        @pl.when(i == 0)
        def _():
            loss_ref[...] = jnp.zeros_like(loss_ref)
            dg_ref[...] = jnp.zeros_like(dg_ref)
            dpw_ref[...] = jnp.zeros_like(dpw_ref)
            gw_ref[...] = jnp.zeros_like(gw_ref)

        ffn = jnp.dot(f_ref[...], wd_ref[...], preferred_element_type=F32)
        x1 = x1_ref[...]
        x2 = x1 + g_ref[...] * _rms(ffn, pw_ref[...])
        err = x2 - t_ref[...]
        dy = err * (1.0 / D_MODEL)
        dy_ref[...] = dy
        loss_ref[...] += 0.5 * jnp.sum(jnp.mean(err * err, axis=-1, keepdims=True))
        dffn, dg, dpw = _gated_rms_bwd(ffn, g_ref[...], pw_ref[...], dy)
        dffn16 = _b(dffn)
        dffn_ref[...] = dffn16
        dg_ref[...] += dg
        dpw_ref[...] += dpw
        gw_ref[...] += _tn(f_ref[...], dffn16)

    tile = lambda n: pl.BlockSpec((tm, n), _row)
    return pl.pallas_call(
        body, name="down_loss", grid=(s // tm,),
        in_specs=[tile(D_FF), _resident((D_FF, D_MODEL)), tile(D_MODEL), tile(D_MODEL), _vec(D_MODEL), _vec(D_MODEL)],
        out_specs=[tile(D_MODEL), tile(D_MODEL), _vec(128), _vec(D_MODEL), _vec(D_MODEL),
                   pl.BlockSpec((D_FF, D_MODEL), _const)],
        out_shape=[jax.ShapeDtypeStruct((s, D_MODEL), BF16), jax.ShapeDtypeStruct((s, D_MODEL), F32),
                   jax.ShapeDtypeStruct((1, 128), F32), jax.ShapeDtypeStruct((1, D_MODEL), F32),
                   jax.ShapeDtypeStruct((1, D_MODEL), F32), jax.ShapeDtypeStruct((D_FF, D_MODEL), F32)],
        compiler_params=_params())(f16, w_down16, x1, target, gate2, post_ffn_w)


BWD_CHUNK = 256


def ffn_bwd(u, u_raw16, d_ffn, conv_w, w_down_t16, w_up_t16, tm=TOKEN_TILE):
    s = u.shape[0]
    nt = s // tm

    def body(u_ref, unext_ref, uraw_ref, d_ref, dnext_ref, cw_ref, wdt_ref, wut_ref,
             du_ref, dh_ref, dcw_ref, dcb_ref):
        i = pl.program_id(0)

        @pl.when(i == 0)
        def _():
            dcw_ref[...] = jnp.zeros_like(dcw_ref)
            dcb_ref[...] = jnp.zeros_like(dcb_ref)

        dnext = dnext_ref[...]
        dnext = jnp.where(i < nt - 1, dnext, jnp.zeros_like(dnext))
        dff = jnp.concatenate([d_ref[...], dnext], axis=0)
        rows_ext = tm + NEXT
        for lo in range(0, D_FF, BWD_CHUNK):
            gcols, vcols = slice(lo, lo + BWD_CHUNK), slice(D_FF + lo, D_FF + lo + BWD_CHUNK)
            ug = jnp.concatenate([u_ref[:, gcols], unext_ref[:, gcols]], axis=0).astype(F32)
            uv = jnp.concatenate([u_ref[:, vcols], unext_ref[:, vcols]], axis=0).astype(F32)
            df = jnp.dot(dff, wdt_ref[:, gcols], preferred_element_type=F32)
            for cols, du in zip((gcols, vcols), _gate_bwd(ug, uv, df)):
                cw = cw_ref[:, cols]
                du1 = pltpu.roll(du, rows_ext - 1, axis=0)
                du2 = pltpu.roll(du, rows_ext - 2, axis=0)
                du_ref[:, cols] = _b((du * cw[2:3, :] + du1 * cw[1:2, :] + du2 * cw[0:1, :])[:tm])
                xr = uraw_ref[:, cols].astype(F32)
                rows = [jnp.sum(xr * d_[:tm], axis=0, keepdims=True) for d_ in (du2, du1, du)]
                dcw_ref[:, cols] += jnp.concatenate(rows + [jnp.zeros((8 - FFN_CONV, BWD_CHUNK), F32)], axis=0)
                dcb_ref[:, cols] += jnp.sum(du[:tm], axis=0, keepdims=True)
        dh_ref[...] = jnp.dot(du_ref[...], wut_ref[...], preferred_element_type=F32)

    tile = lambda n: pl.BlockSpec((tm, n), _row)
    nxt = lambda i: (jnp.minimum((i + 1) * (tm // NEXT), s // NEXT - 1), 0)
    return pl.pallas_call(
        body, name="ffn_bwd", grid=(nt,),
        in_specs=[tile(2 * D_FF), pl.BlockSpec((NEXT, 2 * D_FF), nxt), tile(2 * D_FF), tile(D_MODEL),
                  pl.BlockSpec((NEXT, D_MODEL), nxt), pl.BlockSpec((FFN_CONV, 2 * D_FF), _const),
                  _resident((D_MODEL, D_FF)), _resident((2 * D_FF, D_MODEL))],
        out_specs=[tile(2 * D_FF), tile(D_MODEL), pl.BlockSpec((8, 2 * D_FF), _const), _vec(2 * D_FF)],
        out_shape=[jax.ShapeDtypeStruct((s, 2 * D_FF), BF16), jax.ShapeDtypeStruct((s, D_MODEL), F32),
                   jax.ShapeDtypeStruct((8, 2 * D_FF), F32), jax.ShapeDtypeStruct((1, 2 * D_FF), F32)],
        compiler_params=_params())(u, u, u_raw16, d_ffn, d_ffn, conv_w, w_down_t16, w_up_t16)


def mix_bwd(dh2, x1, dy, mixed, attn, ssm, w_out_t16, pre_ffn_w, scale2, gate1, post_mix_w, tm=TOKEN_TILE):
    s = x1.shape[0]

    def body(dh_ref, x1_ref, dy_ref, mx_ref, a_ref, s_ref, w_ref, fw_ref, sc_ref, g_ref, pw_ref,
             dx1_ref, da_ref, ds_ref, dfw_ref, dsc_ref, dsh_ref, dg_ref, dpw_ref, gw_ref):
        accs = (dfw_ref, dsc_ref, dsh_ref, dg_ref, dpw_ref)

        @pl.when(pl.program_id(0) == 0)
        def _():
            for r in accs + (gw_ref,):
                r[...] = jnp.zeros_like(r)

        dx1, dfw, dsc, dsh = _norm_mod_bwd(x1_ref[...], fw_ref[...], sc_ref[...], dh_ref[...])
        dx1 = dx1 + dy_ref[...]
        dx1_ref[...] = dx1
        dmixed, dg, dpw = _gated_rms_bwd(mx_ref[...], g_ref[...], pw_ref[...], dx1)
        dm16 = _b(dmixed)
        dmix_in = jnp.dot(dm16, w_ref[...], preferred_element_type=F32)
        da_ref[...] = _b(dmix_in[:, :ATTN_WIDTH])
        ds_ref[...] = dmix_in[:, ATTN_WIDTH:]
        gw_ref[:ATTN_WIDTH, :] += _tn(a_ref[...], dm16)
        gw_ref[ATTN_WIDTH:, :] += _tn(s_ref[...], dm16)
        for r, v in zip(accs, (dfw, dsc, dsh, dg, dpw)):
            r[...] += v

    tile = lambda n: pl.BlockSpec((tm, n), _row)
    return pl.pallas_call(
        body, name="mix_bwd", grid=(s // tm,),
        in_specs=[tile(D_MODEL)] * 4 + [tile(ATTN_WIDTH), tile(SSM_WIDTH), _resident((D_MODEL, D_MODEL))]
        + [_vec(D_MODEL)] * 4,
        out_specs=[tile(D_MODEL), tile(ATTN_WIDTH), tile(SSM_WIDTH)] + [_vec(D_MODEL)] * 5
        + [pl.BlockSpec((D_MODEL, D_MODEL), _const)],
        out_shape=[jax.ShapeDtypeStruct((s, D_MODEL), F32), jax.ShapeDtypeStruct((s, ATTN_WIDTH), BF16),
                   jax.ShapeDtypeStruct((s, SSM_WIDTH), F32)]
        + [jax.ShapeDtypeStruct((1, D_MODEL), F32)] * 5 + [jax.ShapeDtypeStruct((D_MODEL, D_MODEL), F32)],
        compiler_params=_params())(dh2, x1, dy, mixed, attn, ssm, w_out_t16, pre_ffn_w, scale2, gate1, post_mix_w)


INPROJ_BWD_TILE = 512


def inproj_bwd(dq, dk, dv, dxbc, dz, ddt, x, dx1, h1, w_in_t16, pre_mix_w, scale1, tm=INPROJ_BWD_TILE):
    s = x.shape[0]
    tm = min(tm, s)

    def body(dq_ref, dk_ref, dv_ref, dxbc_ref, dz_ref, ddt_ref, x_ref, dx1_ref, h_ref, w_ref, pw_ref, sc_ref,
             gx_ref, dpw_ref, dsc_ref, dsh_ref, gw_ref):
        accs = (dpw_ref, dsc_ref, dsh_ref)

        @pl.when(pl.program_id(0) == 0)
        def _():
            for r in accs + (gw_ref,):
                r[...] = jnp.zeros_like(r)

        h16 = h_ref[...]
        dh = None
        off_k, off_v = ATTN_WIDTH, ATTN_WIDTH + KV_WIDTH
        for r, lo, hi in ((dq_ref, 0, off_k), (dk_ref, off_k, off_v), (dv_ref, off_v, OFF_XBC),
                          (dxbc_ref, OFF_XBC, OFF_Z), (dz_ref, OFF_Z, OFF_DT), (ddt_ref, OFF_DT, PROJ_PAD)):
            d16 = _b(r[...])
            part = jnp.dot(d16, w_ref[lo:hi, :], preferred_element_type=F32)
            dh = part if dh is None else dh + part
            gw_ref[:, lo:hi] += _tn(h16, d16)
        dx, dpw, dsc, dsh = _norm_mod_bwd(x_ref[...], pw_ref[...], sc_ref[...], dh)
        gx_ref[...] = dx1_ref[...] + dx
        for r, v in zip(accs, (dpw, dsc, dsh)):
            r[...] += v

    tile = lambda n: pl.BlockSpec((tm, n), _row)
    return pl.pallas_call(
        body, name="inproj_bwd", grid=(s // tm,),
        in_specs=[tile(ATTN_WIDTH), tile(KV_WIDTH), tile(KV_WIDTH), tile(XBC_WIDTH), tile(SSM_WIDTH), tile(DT_PAD),
                  tile(D_MODEL), tile(D_MODEL), tile(D_MODEL), _resident((PROJ_PAD, D_MODEL))] + [_vec(D_MODEL)] * 2,
        out_specs=[tile(D_MODEL)] + [_vec(D_MODEL)] * 3 + [pl.BlockSpec((D_MODEL, PROJ_PAD), _const)],
        out_shape=[jax.ShapeDtypeStruct((s, D_MODEL), F32)] + [jax.ShapeDtypeStruct((1, D_MODEL), F32)] * 3
        + [jax.ShapeDtypeStruct((D_MODEL, PROJ_PAD), F32)],
        compiler_params=_params())(dq, dk, dv, dxbc, dz, ddt, x, dx1, h1, w_in_t16, pre_mix_w, scale1)


def _adam(g, w, m, v):
    new_m = ADAM_B1 * m + (1.0 - ADAM_B1) * g
    new_v = ADAM_B2 * v + (1.0 - ADAM_B2) * jnp.square(g)
    m_hat = new_m / (1.0 - ADAM_B1 ** ADAM_STEP)
    v_hat = new_v / (1.0 - ADAM_B2 ** ADAM_STEP)
    return -ADAM_LR * (m_hat / (jnp.sqrt(v_hat) + ADAM_EPS) + ADAM_WD * w), new_m, new_v


ROW_PARAMS = (("b_ada", 6144, 6144), ("pre_mix_w", 1024, 1024), ("attn_sinks", 128, 8), ("ssm_conv_b", 1024, 1024),
              ("ssm_dt_bias", 128, 8), ("ssm_a_log", 128, 8), ("ssm_d", 128, 8), ("ssm_norm_w", 512, 512),
              ("post_mix_w", 1024, 1024), ("pre_ffn_w", 1024, 1024), ("ffn_conv_b", 5632, 5632),
              ("post_ffn_w", 1024, 1024))
LOSS_LANES = 128


def adamw_small(row_all, rb_all, rel_bias_wmv, row_wmv):
    n_rows = len(ROW_PARAMS)

    def body(*refs):
        row_ref, rb_ref = refs[:2]
        wmv = refs[2:5 + 3 * n_rows]
        outs = refs[5 + 3 * n_rows:]
        g_row, g_rb = row_ref[0], rb_ref[0]
        for k in range(1, N_DEV):
            g_row = g_row + row_ref[k]
            g_rb = g_rb + rb_ref[k]
        outs[0][...] = g_row[:, :LOSS_LANES]
        grads = [g_rb[:, :N_Q_HEADS]]
        off = LOSS_LANES
        for _, lanes, width in ROW_PARAMS:
            grads.append(g_row[:, off:off + width])
            off += lanes
        for i, g in enumerate(grads):
            w_ref, m_ref, v_ref = wmv[3 * i:3 * i + 3]
            g_out, d_out, m_out, v_out = outs[1 + 4 * i:5 + 4 * i]
            g_out[...] = g
            d_out[...], m_out[...], v_out[...] = _adam(g, w_ref[...], m_ref[...], v_ref[...])

    flat_wmv = list(rel_bias_wmv) + [a for wmv in row_wmv for a in wmv]
    shapes = [jax.ShapeDtypeStruct((1, LOSS_LANES), F32)] + [jax.ShapeDtypeStruct((N_BUCKETS, N_Q_HEADS), F32)] * 4
    for _, _, width in ROW_PARAMS:
        shapes += [jax.ShapeDtypeStruct((1, width), F32)] * 4
    return pl.pallas_call(body, name="adamw_small", out_shape=shapes,
                          compiler_params=_params(n_axes=0))(row_all, rb_all, *flat_wmv)


def adamw(parts, w, m, v, name):
    p, r, n = parts.shape
    tr = _row_tile(r)

    def body(p_ref, w_ref, m_ref, v_ref, g_ref, d_ref, nm_ref, nv_ref):
        g = p_ref[0].astype(F32)
        for k in range(1, p):
            g = g + p_ref[k].astype(F32)
        g_ref[...] = g
        d_ref[...], nm_ref[...], nv_ref[...] = _adam(g, w_ref[...], m_ref[...], v_ref[...])

    tile = pl.BlockSpec((tr, n), _row)
    return pl.pallas_call(
        body, name=name, grid=(r // tr,),
        in_specs=[pl.BlockSpec((p, tr, n), lambda i: (0, i, 0)), tile, tile, tile],
        out_specs=[tile] * 4, out_shape=[jax.ShapeDtypeStruct((r, n), F32)] * 4,
        compiler_params=_params())(parts, w, m, v)


def _bucket_table():
    lq = ATTN_BLOCK
    qi = np.arange(lq)[:, None] + lq
    kj = np.arange(2 * lq)[None, :]
    dist = qi - kj
    d = np.maximum(dist, 0)
    max_exact = N_BUCKETS // 2
    nf = np.maximum(d, 1).astype(np.float32)
    large = max_exact + (np.log(nf / max_exact) / math.log(REL_MAX_DIST / max_exact)
                         * (N_BUCKETS - max_exact)).astype(np.int32)
    large = np.minimum(large, N_BUCKETS - 1)
    bucket = np.where(d < max_exact, d, large).astype(np.int32)
    in_band = (dist >= 0) & (dist < REL_MAX_DIST)
    return np.where(in_band, bucket, -1).astype(np.int32)


def _cols_from_blocks(g):
    return jnp.transpose(g, (1, 0, 2)).reshape(g.shape[1], N_DEV * g.shape[2])


def _cols_to_blocks(a):
    r, n = a.shape
    return jnp.transpose(a.reshape(r, N_DEV, n // N_DEV), (1, 0, 2))


def _perm_in_rows(wt):
    pad = jnp.zeros((DT_PAD - SSM_HEADS, wt.shape[1]), wt.dtype)
    return jnp.concatenate([wt[:768], wt[768:1280], wt[1792:2304], wt[1280:1792], wt[2304:2312], pad], axis=0)


def _unperm_in(g):
    return jnp.concatenate([g[:, :768], g[:, 768:1280], g[:, 1792:2304], g[:, 1280:1792], g[:, 2304:2312]], axis=1)


def _lane_pad(v, n=128):
    return jnp.pad(v, ((0, 0), (0, n - v.shape[1])))


def kernel(x, c, rel_bias, w_ada, b_ada, pre_mix_w, w_in, attn_sinks, ssm_conv_w, ssm_conv_b, ssm_dt_bias, ssm_a_log, ssm_d, ssm_norm_w, w_out, post_mix_w, pre_ffn_w, w_up, ffn_conv_w, ffn_conv_b, w_down, post_ffn_w, loss_target, m_rel_bias, m_w_ada, m_b_ada, m_pre_mix_w, m_w_in, m_attn_sinks, m_ssm_conv_w, m_ssm_conv_b, m_ssm_dt_bias, m_ssm_a_log, m_ssm_d, m_ssm_norm_w, m_w_out, m_post_mix_w, m_pre_ffn_w, m_w_up, m_ffn_conv_w, m_ffn_conv_b, m_w_down, m_post_ffn_w, v_rel_bias, v_w_ada, v_b_ada, v_pre_mix_w, v_w_in, v_attn_sinks, v_ssm_conv_w, v_ssm_conv_b, v_ssm_dt_bias, v_ssm_a_log, v_ssm_d, v_ssm_norm_w, v_w_out, v_post_mix_w, v_pre_ffn_w, v_w_up, v_ffn_conv_w, v_ffn_conv_b, v_w_down, v_post_ffn_w):
    weights = dict(rel_bias=rel_bias, w_ada=w_ada, b_ada=b_ada, pre_mix_w=pre_mix_w, w_in=w_in, attn_sinks=attn_sinks, ssm_conv_w=ssm_conv_w, ssm_conv_b=ssm_conv_b, ssm_dt_bias=ssm_dt_bias, ssm_a_log=ssm_a_log, ssm_d=ssm_d, ssm_norm_w=ssm_norm_w, w_out=w_out, post_mix_w=post_mix_w, pre_ffn_w=pre_ffn_w, w_up=w_up, ffn_conv_w=ffn_conv_w, ffn_conv_b=ffn_conv_b, w_down=w_down, post_ffn_w=post_ffn_w)
    mom_m = dict(rel_bias=m_rel_bias, w_ada=m_w_ada, b_ada=m_b_ada, pre_mix_w=m_pre_mix_w, w_in=m_w_in, attn_sinks=m_attn_sinks, ssm_conv_w=m_ssm_conv_w, ssm_conv_b=m_ssm_conv_b, ssm_dt_bias=m_ssm_dt_bias, ssm_a_log=m_ssm_a_log, ssm_d=m_ssm_d, ssm_norm_w=m_ssm_norm_w, w_out=m_w_out, post_mix_w=m_post_mix_w, pre_ffn_w=m_pre_ffn_w, w_up=m_w_up, ffn_conv_w=m_ffn_conv_w, ffn_conv_b=m_ffn_conv_b, w_down=m_w_down, post_ffn_w=m_post_ffn_w)
    mom_v = dict(rel_bias=v_rel_bias, w_ada=v_w_ada, b_ada=v_b_ada, pre_mix_w=v_pre_mix_w, w_in=v_w_in, attn_sinks=v_attn_sinks, ssm_conv_w=v_ssm_conv_w, ssm_conv_b=v_ssm_conv_b, ssm_dt_bias=v_ssm_dt_bias, ssm_a_log=v_ssm_a_log, ssm_d=v_ssm_d, ssm_norm_w=v_ssm_norm_w, w_out=v_w_out, post_mix_w=v_post_mix_w, pre_ffn_w=v_pre_ffn_w, w_up=v_w_up, ffn_conv_w=v_ffn_conv_w, ffn_conv_b=v_ffn_conv_b, w_down=v_w_down, post_ffn_w=v_post_ffn_w)
    order = ['rel_bias', 'w_ada', 'b_ada', 'pre_mix_w', 'w_in', 'attn_sinks', 'ssm_conv_w', 'ssm_conv_b', 'ssm_dt_bias', 'ssm_a_log', 'ssm_d', 'ssm_norm_w', 'w_out', 'post_mix_w', 'pre_ffn_w', 'w_up', 'ffn_conv_w', 'ffn_conv_b', 'w_down', 'post_ffn_w']

    me = 4 * lax.axis_index("x") + 2 * lax.axis_index("y") + lax.axis_index("c")
    xs_ = x[0]
    target = loss_target[0]

    (w_in_g, scw_g, fcw_g, c_g) = all_gather([_b(w_in[0]).T, ssm_conv_w[0], ffn_conv_w[0], c], "gather_weights")
    w_in_t16 = _perm_in_rows(w_in_g.reshape(IN_PROJ_WIDTH, D_MODEL))
    w_in16 = w_in_t16.T
    ssm_cw = _cols_from_blocks(scw_g)
    ffn_cw = _cols_from_blocks(fcw_g)
    c_all = c_g.reshape(N_DEV, D_MODEL)

    n_cols = w_ada.shape[2]
    b_cols = lax.dynamic_slice(b_ada, (0, me * n_cols), (1, n_cols))
    mod_part = ada_fwd(c_all, w_ada[0], b_cols)
    (mod_rows,) = all_to_all([mod_part.reshape(N_DEV, 1, n_cols)], "scatter_mod")
    mod = mod_rows.reshape(N_MOD, 1, D_MODEL)
    shift1, scale1, gate1, shift2, scale2, gate2 = (mod[i] for i in range(N_MOD))

    bucket_band = jnp.asarray(_bucket_table())
    bias = rel_bias_table(rel_bias, bucket_band)
    sinks_row = _lane_pad(attn_sinks)
    dt_bias, a_log, dskip = _lane_pad(ssm_dt_bias), _lane_pad(ssm_a_log), _lane_pad(ssm_d)

    h1, qkv, xbc_raw, z, dt_raw, w_out_g = pre_mix_inproj(
        xs_, pre_mix_w, scale1, shift1, w_in16, [(_b(w_out[0]), False)])
    attn, w_up_g = attn_fwd(qkv, bias, sinks_row, [(_b(w_up[0]).T, False)])
    ssm, hprev_all, w_down_g = ssm_fwd(xbc_raw, z, dt_raw, ssm_cw, ssm_conv_b, dt_bias, a_log, dskip, ssm_norm_w,
                                       [(_b(w_down[0]), False)])
    w_out16 = w_out_g.reshape(D_MODEL, D_MODEL)
    w_out_t16 = w_out16.T
    w_up_t16 = w_up_g.reshape(2 * D_FF, D_MODEL)
    w_up16 = w_up_t16.T
    w_down16 = w_down_g.reshape(D_FF, D_MODEL)
    w_down_t16 = w_down16.T
    mixed, x1, h2 = mix_out(attn, ssm, xs_, w_out16, gate1, post_mix_w, pre_ffn_w, scale2, shift2)
    u, u_raw16, f16 = up_gate(h2, w_up16, ffn_cw, ffn_conv_b)
    d_ffn, dy, loss_part, d_gate2, d_post_ffn_w, g_w_down = down_loss(f16, w_down16, x1, target, gate2, post_ffn_w)

    du_raw, dh2, d_ffn_cw, d_ffn_cb = ffn_bwd(u, u_raw16, d_ffn, ffn_cw, w_down_t16, w_up_t16)
    g_w_up = matmul_tn(h2, du_raw, "grad_w_up", D_MODEL, FF_CHUNK, tk=2048)
    (dx1, d_attn, d_ssm, d_pre_ffn_w, d_scale2, d_shift2, d_gate1, d_post_mix_w, g_w_out) = mix_bwd(
        dh2, x1, dy, mixed, attn, ssm, w_out_t16, pre_ffn_w, scale2, gate1, post_mix_w)
    dq, dk, dv, dbias, dsinks, p_w_down = attn_bwd(
        qkv, bias, sinks_row, d_attn, [(g_w_down.reshape(N_DEV, D_FF // N_DEV, D_MODEL), True)])
    d_rel_bias = rel_bias_grad(dbias, bucket_band)
    (dxbc, dz, ddt, d_ssm_cw, d_ssm_cb, d_dt_bias, d_a_log, d_dskip, d_norm_w, p_w_up, p_w_out) = ssm_bwd(
        xbc_raw, z, dt_raw, hprev_all, d_ssm, ssm_cw, ssm_conv_b, dt_bias, a_log, dskip, ssm_norm_w,
        [(_cols_to_blocks(g_w_up), True), (g_w_out.reshape(N_DEV, D_MODEL // N_DEV, D_MODEL), True)])
    grad_x, d_pre_mix_w, d_scale1, d_shift1, g_w_in_perm = inproj_bwd(
        dq, dk, dv, dxbc, dz, ddt, xs_, dx1, h1, w_in_t16, pre_mix_w, scale1)
    g_w_in = _unperm_in(g_w_in_perm)

    d_mod = jnp.concatenate([d_shift1, d_scale1, d_gate1, d_shift2, d_scale2, d_gate2], axis=1)
    late = ("w_in", "ssm_conv_w", "ffn_conv_w")
    full = [_cols_to_blocks(g_w_in), _cols_to_blocks(d_ssm_cw[:SSM_CONV]), _cols_to_blocks(d_ffn_cw[:FFN_CONV])]
    core = lax.axis_index("c").astype(jnp.int32).reshape(1)
    got = pair_exchange(full, "pair_grads")
    chip_sums = [pair_sum(f_, g_, core, "pair_sum_" + k) for k, f_, g_ in zip(late, full, got)]
    chip_parts = all_to_all(chip_sums, "scatter_grads", CHIP_FLIPS, _chip_index)
    (d_mod_rows,) = all_to_all([d_mod.reshape(N_DEV, 1, n_cols)], "scatter_dmod")
    g_w_ada = ada_bwd(c_all, d_mod_rows.reshape(N_DEV, n_cols))

    row_g = dict(b_ada=d_mod, pre_mix_w=d_pre_mix_w, attn_sinks=dsinks, ssm_conv_b=d_ssm_cb, ssm_dt_bias=d_dt_bias,
                 ssm_a_log=d_a_log, ssm_d=d_dskip, ssm_norm_w=d_norm_w, post_mix_w=d_post_mix_w,
                 pre_ffn_w=d_pre_ffn_w, ffn_conv_b=d_ffn_cb, post_ffn_w=d_post_ffn_w)
    row = jnp.concatenate([loss_part] + [row_g[k] for k, _, _ in ROW_PARAMS], axis=1)
    row_all, rb_all = all_gather([row, d_rel_bias], "gather_small")

    wmv = lambda k: (weights[k], mom_m[k], mom_v[k])
    small = adamw_small(row_all, rb_all, wmv("rel_bias"), [wmv(k) for k, _, _ in ROW_PARAMS])
    loss = small[0][0, 0]
    res = {k: tuple(small[1 + 4 * i:5 + 4 * i]) for i, k in enumerate(["rel_bias"] + [k for k, _, _ in ROW_PARAMS])}
    big = list(zip(late, chip_parts)) + [("w_down", p_w_down), ("w_up", p_w_up), ("w_out", p_w_out),
                                        ("w_ada", g_w_ada[None])]
    for k, parts in big:
        res[k] = tuple(o[None] for o in adamw(parts, weights[k][0], mom_m[k][0], mom_v[k][0], "adamw_" + k))

    outs = [loss, grad_x[None]]
    for field in range(4):
        outs += [res[k][field] for k in order]
    return tuple(outs)
```

```python
import math

import numpy as np
import jax
import jax.numpy as jnp
from jax import lax
from jax.experimental import pallas as pl
from jax.experimental.pallas import tpu as pltpu

F32 = jnp.float32
BF16 = jnp.bfloat16
MESH_ID = pl.DeviceIdType.MESH

N_DEV = 8
D_MODEL = 1024
N_Q_HEADS = 8
N_KV_HEADS = 2
HEAD_DIM = 64
ATTN_WIDTH = 512
KV_WIDTH = 128
ATTN_BLOCK = 128
N_BUCKETS = 32
REL_MAX_DIST = 128
SSM_HEADS = 8
SSM_HEAD_DIM = 64
SSM_WIDTH = 512
SSM_STATE = 128
SSM_GROUPS = 2
SSM_BC = 256
SSM_CONV = 4
SSM_CHUNK = 256
XBC_WIDTH = SSM_WIDTH + 2 * SSM_BC
D_FF = 2816
FFN_CONV = 3
NORM_EPS = 1e-6
N_MOD = 6
IN_PROJ_WIDTH = 2312
QKV_W = ATTN_WIDTH + 2 * KV_WIDTH
OFF_XBC = QKV_W
OFF_Z = OFF_XBC + XBC_WIDTH
OFF_DT = OFF_Z + SSM_WIDTH
DT_PAD = 128
PROJ_PAD = OFF_DT + DT_PAD
FF_CHUNK = 1408

ADAM_LR = 0.001
ADAM_B1 = 0.9
ADAM_B2 = 0.999
ADAM_EPS = 1e-08
ADAM_WD = 0.01
ADAM_STEP = 10

TOKEN_TILE = 256
HALO = 8
NEXT = 16
VMEM_LIMIT = 56 * 1024 * 1024


def _params(vmem=VMEM_LIMIT, n_axes=1):
    return pltpu.CompilerParams(dimension_semantics=("arbitrary",) * n_axes, vmem_limit_bytes=vmem)


def _b(x):
    return x.astype(BF16)


def _nn(a, b):
    return jnp.dot(_b(a), _b(b), preferred_element_type=F32)


def _nt(a, b):
    return lax.dot_general(_b(a), _b(b), (((1,), (1,)), ((), ())), preferred_element_type=F32)


def _tn(a, b):
    return lax.dot_general(_b(a), _b(b), (((0,), (0,)), ((), ())), preferred_element_type=F32)


@jax.custom_vjp
def mm(a, b):
    return _nn(a, b)


mm.defvjp(lambda a, b: (_nn(a, b), (a, b)),
          lambda r, g: (_nt(g, r[1]).astype(r[0].dtype), _tn(r[0], g).astype(r[1].dtype)))


@jax.custom_vjp
def mm_nt(a, b):
    return _nt(a, b)


mm_nt.defvjp(lambda a, b: (_nt(a, b), (a, b)),
             lambda r, g: (_nn(g, r[1]).astype(r[0].dtype), _tn(g, r[0]).astype(r[1].dtype)))


@jax.custom_vjp
def mm_tn(a, b):
    return _tn(a, b)


mm_tn.defvjp(lambda a, b: (_tn(a, b), (a, b)),
             lambda r, g: (_nt(r[1], g).astype(r[0].dtype), _nn(r[0], g).astype(r[1].dtype)))


def _rms(x, w):
    return x * lax.rsqrt(jnp.mean(x * x, axis=-1, keepdims=True) + NORM_EPS) * w


def _norm_mod(x, w, scale, shift):
    return _rms(x, w) * (1.0 + scale) + shift


def _rms_bwd(x, w, dy):
    r = lax.rsqrt(jnp.mean(x * x, axis=-1, keepdims=True) + NORM_EPS)
    xhat = x * r
    g = dy * w
    dx = r * (g - xhat * jnp.mean(g * xhat, axis=-1, keepdims=True))
    return dx, jnp.sum(dy * xhat, axis=0, keepdims=True)


def _norm_mod_bwd(x, w, scale, dh):
    dx, da = _rms_bwd(x, w * (1.0 + scale), dh)
    return dx, da * (1.0 + scale), da * w, jnp.sum(dh, axis=0, keepdims=True)


def _gated_rms_bwd(m, gate, w, dy):
    dm, t = _rms_bwd(m, w * gate, dy)
    return dm, t * w, t * gate


def _silu(x):
    return x * jax.nn.sigmoid(x)


def _conv_rows(xin, w, k):
    acc = xin * w[k - 1:k, :]
    for j in range(1, k):
        acc = acc + pltpu.roll(xin, j, axis=0) * w[k - 1 - j:k - j, :]
    return acc


def _conv_rows_t(du, w, k):
    n = du.shape[0]
    acc = du * w[k - 1:k, :]
    for j in range(1, k):
        acc = acc + pltpu.roll(du, n - j, axis=0) * w[k - 1 - j:k - j, :]
    return acc


def _row(i):
    return (i, 0)


def _const(i):
    return (0, 0)


def _vec(n):
    return pl.BlockSpec((1, n), _const)


def _block_index(p):
    return 4 * p[0] + 2 * p[1] + p[2]


def all_gather(arrs, name):
    n = len(arrs)

    def body(*refs):
        ins, outs = refs[:n], refs[n:2 * n]
        send_sems, recv_sems, local_sems = refs[2 * n:]
        x, y, c = lax.axis_index("x"), lax.axis_index("y"), lax.axis_index("c")
        me, sibling = (x, y, c), (x, y, 1 - c)
        chips = [(1 - x, y), (x, 1 - y), (1 - x, 1 - y)]

        def copy(a, k, block, to, src=None):
            dst = outs[a].at[_block_index(block)]
            return pltpu.make_async_remote_copy(
                src_ref=dst if src is None else src, dst_ref=dst,
                send_sem=send_sems.at[a * 7 + k], recv_sem=recv_sems.at[a * 7 + k],
                device_id=to, device_id_type=MESH_ID)

        mine = [pltpu.make_async_copy(ins[a], outs[a].at[_block_index(me)], local_sems.at[a]) for a in range(n)]
        for cp in mine:
            cp.start()
        first = []
        for a in range(n):
            first.append(copy(a, 0, me, sibling, src=ins[a]))
            first += [copy(a, 1 + j, me, (*chip, c), src=ins[a]) for j, chip in enumerate(chips)]
        for cp in first:
            cp.start()
        passed = []
        for j, chip in enumerate(chips):
            for a in range(n):
                copy(a, 1 + j, (*chip, c), me).wait_recv()
                cp = copy(a, 4 + j, (*chip, c), sibling)
                cp.start()
                passed.append(cp)
        for a in range(n):
            copy(a, 0, sibling, me).wait_recv()
            for j, chip in enumerate(chips):
                copy(a, 4 + j, (*chip, 1 - c), me).wait_recv()
        for cp in first + passed:
            cp.wait_send()
        for cp in mine:
            cp.wait()

    any_spec = pl.BlockSpec(memory_space=pl.ANY)
    return pl.pallas_call(
        body, name=name,
        out_shape=[jax.ShapeDtypeStruct((N_DEV,) + a.shape, a.dtype) for a in arrs],
        in_specs=[any_spec] * n, out_specs=[any_spec] * n,
        scratch_shapes=[pltpu.SemaphoreType.DMA((7 * n,)), pltpu.SemaphoreType.DMA((7 * n,)),
                        pltpu.SemaphoreType.DMA((n,))],
    )(*arrs)


ALL_FLIPS = ((0, 0, 1), (0, 1, 0), (0, 1, 1), (1, 0, 0), (1, 0, 1), (1, 1, 0), (1, 1, 1))
CHIP_FLIPS = ((0, 1, 0), (1, 0, 0), (1, 1, 0))


def _chip_index(p):
    return 2 * p[0] + p[1]


def all_to_all(arrs, name, flips=ALL_FLIPS, index=_block_index):
    n = len(arrs)
    nf = len(flips)

    def body(*refs):
        ins, outs = refs[:n], refs[n:2 * n]
        send_sems, recv_sems, local_sems = refs[2 * n:]
        pos = (lax.axis_index("x"), lax.axis_index("y"), lax.axis_index("c"))
        me = index(pos)
        peers = [tuple(1 - p if f else p for p, f in zip(pos, flip)) for flip in flips]

        def copy(a, k):
            peer = peers[k]
            return pltpu.make_async_remote_copy(
                src_ref=ins[a].at[index(peer)], dst_ref=outs[a].at[me],
                send_sem=send_sems.at[a * nf + k], recv_sem=recv_sems.at[a * nf + k],
                device_id=peer, device_id_type=MESH_ID)

        def landed(a, k):
            slot = outs[a].at[index(peers[k])]
            return pltpu.make_async_remote_copy(
                src_ref=slot, dst_ref=slot,
                send_sem=send_sems.at[a * nf + k], recv_sem=recv_sems.at[a * nf + k],
                device_id=peers[k], device_id_type=MESH_ID)

        mine = [pltpu.make_async_copy(ins[a].at[me], outs[a].at[me], local_sems.at[a]) for a in range(n)]
        for cp in mine:
            cp.start()
        sent = [copy(a, k) for a in range(n) for k in range(nf)]
        for cp in sent:
            cp.start()
        for a in range(n):
            for k in range(nf):
                landed(a, k).wait_recv()
        for cp in sent:
            cp.wait_send()
        for cp in mine:
            cp.wait()

    any_spec = pl.BlockSpec(memory_space=pl.ANY)
    return pl.pallas_call(
        body, name=name,
        out_shape=[jax.ShapeDtypeStruct(a.shape, a.dtype) for a in arrs],
        in_specs=[any_spec] * n, out_specs=[any_spec] * n,
        scratch_shapes=[pltpu.SemaphoreType.DMA((nf * n,)), pltpu.SemaphoreType.DMA((nf * n,)),
                        pltpu.SemaphoreType.DMA((n,))],
    )(*arrs)


def _direct_exchange(src, dst, sems, scatter):
    send_sems, recv_sems, local_sem = sems
    pos = (lax.axis_index("x"), lax.axis_index("y"), lax.axis_index("c"))
    me = _block_index(pos)
    peers = [tuple(1 - p if f else p for p, f in zip(pos, flip)) for flip in ALL_FLIPS]

    def outgoing(k):
        return pltpu.make_async_remote_copy(
            src_ref=src.at[_block_index(peers[k])] if scatter else src, dst_ref=dst.at[me],
            send_sem=send_sems.at[k], recv_sem=recv_sems.at[k], device_id=peers[k], device_id_type=MESH_ID)

    def incoming(k):
        slot = dst.at[_block_index(peers[k])]
        return pltpu.make_async_remote_copy(
            src_ref=slot, dst_ref=slot, send_sem=send_sems.at[k], recv_sem=recv_sems.at[k],
            device_id=peers[k], device_id_type=MESH_ID)

    def local():
        return pltpu.make_async_copy(src.at[me] if scatter else src, dst.at[me], local_sem)

    def start():
        local().start()
        for k in range(len(ALL_FLIPS)):
            outgoing(k).start()

    def finish():
        for k in range(len(ALL_FLIPS)):
            incoming(k).wait_recv()
        for k in range(len(ALL_FLIPS)):
            outgoing(k).wait_send()
        local().wait()

    return start, finish


def hosted_call(body, exchanges, steps, n_in, n_out, **call):
    n_ex = len(exchanges)

    def wrapped(*refs):
        ins, srcs = refs[:n_in], refs[n_in:n_in + n_ex]
        outs = refs[n_in + n_ex:n_in + n_ex + n_out]
        dsts = refs[n_in + n_ex + n_out:n_in + 2 * n_ex + n_out]
        rest = refs[n_in + 2 * n_ex + n_out:]
        scratch, sems = rest[:len(rest) - 3 * n_ex], rest[len(rest) - 3 * n_ex:]
        plans = [_direct_exchange(srcs[e], dsts[e], sems[3 * e:3 * e + 3], exchanges[e][1]) for e in range(n_ex)]

        @pl.when(pl.program_id(0) == 0)
        def _():
            for start, _ in plans:
                start()

        body(*ins, *outs, *scratch)

        @pl.when(pl.program_id(0) == steps - 1)
        def _():
            for _, finish in plans:
                finish()

    any_spec = pl.BlockSpec(memory_space=pl.ANY)
    landings = [jax.ShapeDtypeStruct(src.shape if scatter else (N_DEV,) + src.shape, src.dtype)
                for src, scatter in exchanges]
    n_flips = len(ALL_FLIPS)
    sems = [pltpu.SemaphoreType.DMA((n_flips,)), pltpu.SemaphoreType.DMA((n_flips,)), pltpu.SemaphoreType.DMA(())]
    return pl.pallas_call(
        wrapped, grid=(steps,),
        in_specs=list(call.pop("in_specs")) + [any_spec] * n_ex,
        out_specs=list(call.pop("out_specs")) + [any_spec] * n_ex,
        out_shape=list(call.pop("out_shape")) + landings,
        scratch_shapes=list(call.pop("scratch_shapes", [])) + sems * n_ex,
        **call)


def _grid_call(body, steps, args, exchanges, **call):
    if not exchanges:
        return pl.pallas_call(body, grid=(steps,), **call)(*args)
    srcs = [src for src, _ in exchanges]
    return hosted_call(body, exchanges, steps, len(args), len(call["out_shape"]), **call)(*args, *srcs)


N_CHIPS = 4


def pair_exchange(arrs, name):
    n = len(arrs)

    def body(*refs):
        ins, outs = refs[:n], refs[n:2 * n]
        send_sems, recv_sems = refs[2 * n:]
        x, y, c = lax.axis_index("x"), lax.axis_index("y"), lax.axis_index("c")
        sibling = (x, y, 1 - c)
        sent = []
        for a in range(n):
            for q in range(N_CHIPS):
                cp = pltpu.make_async_remote_copy(
                    src_ref=ins[a].at[2 * q + (1 - c)], dst_ref=outs[a].at[q],
                    send_sem=send_sems.at[a * N_CHIPS + q], recv_sem=recv_sems.at[a * N_CHIPS + q],
                    device_id=sibling, device_id_type=MESH_ID)
                cp.start()
                sent.append(cp)
        for cp in sent:
            cp.wait_recv()
        for cp in sent:
            cp.wait_send()

    any_spec = pl.BlockSpec(memory_space=pl.ANY)
    return pl.pallas_call(
        body, name=name,
        out_shape=[jax.ShapeDtypeStruct((N_CHIPS,) + a.shape[1:], a.dtype) for a in arrs],
        in_specs=[any_spec] * n, out_specs=[any_spec] * n,
        scratch_shapes=[pltpu.SemaphoreType.DMA((N_CHIPS * n,)), pltpu.SemaphoreType.DMA((N_CHIPS * n,))],
    )(*arrs)


def pair_sum(full, got, core, name):
    _, r, n = full.shape
    tr = _row_tile(r)

    def body(c_ref, mine_ref, got_ref, o_ref):
        o_ref[...] = _b(mine_ref[...] + got_ref[...])

    grid_spec = pltpu.PrefetchScalarGridSpec(
        num_scalar_prefetch=1, grid=(N_CHIPS, r // tr),
        in_specs=[pl.BlockSpec((1, tr, n), lambda q, i, c_ref: (2 * q + c_ref[0], i, 0)),
                  pl.BlockSpec((1, tr, n), lambda q, i, c_ref: (q, i, 0))],
        out_specs=pl.BlockSpec((1, tr, n), lambda q, i, c_ref: (q, i, 0)))
    return pl.pallas_call(body, name=name, grid_spec=grid_spec,
                          out_shape=jax.ShapeDtypeStruct((N_CHIPS, r, n), BF16),
                          compiler_params=_params(n_axes=2))(core, full, got)


def _row_tile(r):
    for cand in (256, 128, 64, 32, 16):
        if r % cand == 0 and r > cand:
            return cand
    return r


def ada_fwd(c_all, w_ada, b_cols):
    def body(c_ref, w_ref, b_ref, o_ref):
        o_ref[...] = _nn(_silu(c_ref[...]), w_ref[...]) + b_ref[...]

    return pl.pallas_call(body, name="ada_fwd",
                          out_shape=jax.ShapeDtypeStruct((N_DEV, w_ada.shape[1]), F32),
                          compiler_params=_params(n_axes=0))(c_all, w_ada, b_cols)


def ada_bwd(c_all, g_cols):
    def body(c_ref, g_ref, o_ref):
        o_ref[...] = _tn(_silu(c_ref[...]), g_ref[...])

    return pl.pallas_call(body, name="ada_bwd",
                          out_shape=jax.ShapeDtypeStruct((c_all.shape[1], g_cols.shape[1]), F32),
                          compiler_params=_params(n_axes=0))(c_all, g_cols)


def matmul_tn(a, b, name, bm, bn, tk=512):
    s, m = a.shape
    n = b.shape[1]
    tk = min(tk, s)

    def body(a_ref, b_ref, o_ref):
        @pl.when(pl.program_id(2) == 0)
        def _():
            o_ref[...] = jnp.zeros_like(o_ref)

        o_ref[...] += _tn(a_ref[...], b_ref[...])

    return pl.pallas_call(
        body, name=name, grid=(m // bm, n // bn, s // tk),
        in_specs=[pl.BlockSpec((tk, bm), lambda i, j, k: (k, i)), pl.BlockSpec((tk, bn), lambda i, j, k: (k, j))],
        out_specs=pl.BlockSpec((bm, bn), lambda i, j, k: (i, j)),
        out_shape=jax.ShapeDtypeStruct((m, n), F32),
        compiler_params=_params(n_axes=3))(a, b)


def pre_mix_inproj(x, w, scale, shift, w_in16, exchange=None, tm=TOKEN_TILE):
    s = x.shape[0]

    def body(x_ref, w_ref, sc_ref, sh_ref, win_ref, h_ref, qkv_ref, xbc_ref, z_ref, dt_ref):
        h16 = _b(_norm_mod(x_ref[...], w_ref[...], sc_ref[...], sh_ref[...]))
        h_ref[...] = h16
        dot = lambda lo, hi: jnp.dot(h16, win_ref[:, lo:hi], preferred_element_type=F32)
        qkv_ref[...] = _b(dot(0, OFF_XBC))
        xbc_ref[...] = dot(OFF_XBC, OFF_Z)
        z_ref[...] = dot(OFF_Z, OFF_DT)
        dt_ref[...] = dot(OFF_DT, PROJ_PAD)

    tile = lambda n: pl.BlockSpec((tm, n), _row)
    return _grid_call(
        body, s // tm, (x, w, scale, shift, w_in16), exchange, name="pre_mix_inproj",
        in_specs=[tile(D_MODEL), _vec(D_MODEL), _vec(D_MODEL), _vec(D_MODEL), pl.BlockSpec((D_MODEL, PROJ_PAD), _const)],
        out_specs=[tile(D_MODEL), tile(QKV_W), tile(XBC_WIDTH), tile(SSM_WIDTH), tile(DT_PAD)],
        out_shape=[jax.ShapeDtypeStruct((s, D_MODEL), BF16), jax.ShapeDtypeStruct((s, QKV_W), BF16),
                   jax.ShapeDtypeStruct((s, XBC_WIDTH), F32), jax.ShapeDtypeStruct((s, SSM_WIDTH), F32),
                   jax.ShapeDtypeStruct((s, DT_PAD), F32)],
        compiler_params=_params())


ATTN_QB = 2


def _attn_tile(q, kp, kc, vp, vc, bias, sinks):
    lq = ATTN_BLOCK
    group = N_Q_HEADS // N_KV_HEADS
    lanes = lax.broadcasted_iota(jnp.int32, (1, 128), 1)
    rid = lax.broadcasted_iota(jnp.int32, (group * lq, 1), 0)
    sink_cols = []
    for hk in range(N_KV_HEADS):
        sink = jnp.zeros((group * lq, 1), F32)
        for g in range(group):
            s_h = jnp.sum(jnp.where(lanes == hk * group + g, sinks, 0.0), axis=-1, keepdims=True)
            sink = jnp.where((rid >= g * lq) & (rid < (g + 1) * lq), s_h, sink)
        sink_cols.append(sink)
    kall = jnp.concatenate([kp, kc], axis=0)
    vall = jnp.concatenate([vp, vc], axis=0)
    blocks = []
    for b in range(ATTN_QB):
        qb = q[b * lq:(b + 1) * lq]
        outs = []
        for hk in range(N_KV_HEADS):
            cols = slice(hk * HEAD_DIM, (hk + 1) * HEAD_DIM)
            kb = kall[b * lq:(b + 2) * lq, cols]
            vb = vall[b * lq:(b + 2) * lq, cols]
            qg = jnp.concatenate([qb[:, (hk * group + g) * HEAD_DIM:(hk * group + g + 1) * HEAD_DIM]
                                  for g in range(group)], axis=0)
            sc = mm_nt(qg, kb) * (HEAD_DIM ** -0.5) + bias[b][hk]
            sink = sink_cols[hk]
            m = lax.stop_gradient(jnp.maximum(jnp.max(sc, axis=-1, keepdims=True), sink))
            p = jnp.exp(sc - m)
            probs = p / (jnp.sum(p, axis=-1, keepdims=True) + jnp.exp(sink - m))
            og = mm(probs, vb)
            outs += [og[g * lq:(g + 1) * lq] for g in range(group)]
        blocks.append(jnp.concatenate(outs, axis=1))
    return jnp.concatenate(blocks, axis=0)


def _attn_tile_bwd(q, kp, kc, vp, vc, bias, sinks, do):
    lq = ATTN_BLOCK
    group = N_Q_HEADS // N_KV_HEADS
    scale = HEAD_DIM ** -0.5
    lanes = lax.broadcasted_iota(jnp.int32, (1, 128), 1)
    rid = lax.broadcasted_iota(jnp.int32, (group * lq, 1), 0)
    sink_cols = []
    for hk in range(N_KV_HEADS):
        sink = jnp.zeros((group * lq, 1), F32)
        for g in range(group):
            s_h = jnp.sum(jnp.where(lanes == hk * group + g, sinks, 0.0), axis=-1, keepdims=True)
            sink = jnp.where((rid >= g * lq) & (rid < (g + 1) * lq), s_h, sink)
        sink_cols.append(sink)
    kall = jnp.concatenate([kp, kc], axis=0)
    vall = jnp.concatenate([vp, vc], axis=0)
    dsk = jnp.zeros((1, 128), F32)
    dq_blocks, dbias = [], []
    dk_parts = [[None] * ATTN_QB for _ in range(N_KV_HEADS)]
    dv_parts = [[None] * ATTN_QB for _ in range(N_KV_HEADS)]
    for b in range(ATTN_QB):
        qb, dob = q[b * lq:(b + 1) * lq], do[b * lq:(b + 1) * lq]
        dq_heads, dbias_b = [], []
        for hk in range(N_KV_HEADS):
            cols = slice(hk * HEAD_DIM, (hk + 1) * HEAD_DIM)
            kb = kall[b * lq:(b + 2) * lq, cols]
            vb = vall[b * lq:(b + 2) * lq, cols]
            heads = [hk * group + g for g in range(group)]
            qg = jnp.concatenate([qb[:, h * HEAD_DIM:(h + 1) * HEAD_DIM] for h in heads], axis=0)
            dog = jnp.concatenate([dob[:, h * HEAD_DIM:(h + 1) * HEAD_DIM] for h in heads], axis=0)
            sink = sink_cols[hk]
            sc = _nt(qg, kb) * scale + bias[b][hk]
            m = jnp.maximum(jnp.max(sc, axis=-1, keepdims=True), sink)
            p = jnp.exp(sc - m)
            es = jnp.exp(sink - m)
            inv = 1.0 / (jnp.sum(p, axis=-1, keepdims=True) + es)
            probs = p * inv
            dprobs = _nt(dog, vb)
            delta = jnp.sum(probs * dprobs, axis=-1, keepdims=True)
            dsc = probs * (dprobs - delta)
            dbias_b.append(dsc)
            dsink = -(es * inv) * delta
            for g, h in enumerate(heads):
                tot = jnp.sum(dsink[g * lq:(g + 1) * lq], axis=0, keepdims=True)
                dsk = dsk + jnp.where(lanes == h, tot, 0.0)
            dqg = _nn(dsc, kb) * scale
            dq_heads += [dqg[g * lq:(g + 1) * lq] for g in range(group)]
            dk_parts[hk][b] = _tn(dsc, qg) * scale
            dv_parts[hk][b] = _tn(probs, dog)
        dq_blocks.append(jnp.concatenate(dq_heads, axis=1))
        dbias.append(dbias_b)

    def overlap_add(parts):
        chunks = []
        for r in range(ATTN_QB + 1):
            acc = None
            if r < ATTN_QB:
                acc = parts[r][:lq]
            if r >= 1:
                tail = parts[r - 1][lq:]
                acc = tail if acc is None else acc + tail
            chunks.append(acc)
        return jnp.concatenate(chunks, axis=0)

    dkall = jnp.concatenate([overlap_add(dk_parts[hk]) for hk in range(N_KV_HEADS)], axis=1)
    dvall = jnp.concatenate([overlap_add(dv_parts[hk]) for hk in range(N_KV_HEADS)], axis=1)
    return jnp.concatenate(dq_blocks, axis=0), dkall, dvall, dbias, dsk


def _attn_in_specs(nt, clamp):
    lq, tq = ATTN_BLOCK, ATTN_BLOCK * ATTN_QB
    cur = lambda n: jnp.minimum(n, nt - 1) if clamp else n
    prev = lambda n: jnp.maximum(cur(n) * ATTN_QB - 1, 0)
    kcol, vcol = ATTN_WIDTH // KV_WIDTH, ATTN_WIDTH // KV_WIDTH + 1
    return [pl.BlockSpec((tq, ATTN_WIDTH), lambda n: (cur(n), 0)),
            pl.BlockSpec((lq, KV_WIDTH), lambda n: (prev(n), kcol)),
            pl.BlockSpec((tq, KV_WIDTH), lambda n: (cur(n), kcol)),
            pl.BlockSpec((lq, KV_WIDTH), lambda n: (prev(n), vcol)),
            pl.BlockSpec((tq, KV_WIDTH), lambda n: (cur(n), vcol)),
            pl.BlockSpec((2, N_KV_HEADS, 4 * lq, 2 * lq), lambda n: (0, 0, 0, 0)),
            _vec(128)]


def _tile_bias(bias_ref, first):
    return [[jnp.where(first, bias_ref[1, hk], bias_ref[0, hk]) if b == 0 else bias_ref[0, hk]
             for hk in range(N_KV_HEADS)] for b in range(ATTN_QB)]


def attn_fwd(qkv, bias, sinks_rows, exchange=None):
    s = qkv.shape[0]
    tq = ATTN_BLOCK * ATTN_QB
    nt = s // tq

    def body(q_ref, kp_ref, kc_ref, vp_ref, vc_ref, bias_ref, sk_ref, o_ref):
        f = lambda r: r[...].astype(F32)
        o = _attn_tile(f(q_ref), f(kp_ref), f(kc_ref), f(vp_ref), f(vc_ref),
                       _tile_bias(bias_ref, pl.program_id(0) == 0), sk_ref[...])
        o_ref[...] = _b(o)

    return _grid_call(
        body, nt, (qkv, qkv, qkv, qkv, qkv, bias, sinks_rows), exchange, name="attn_fwd",
        in_specs=_attn_in_specs(nt, False),
        out_specs=[pl.BlockSpec((tq, ATTN_WIDTH), _row)],
        out_shape=[jax.ShapeDtypeStruct((s, ATTN_WIDTH), BF16)],
        compiler_params=_params())


def attn_bwd(qkv, bias, sinks_rows, d_attn, exchange=None):
    s = qkv.shape[0]
    lq, tq = ATTN_BLOCK, ATTN_BLOCK * ATTN_QB
    nt = s // tq

    def body(q_ref, kp_ref, kc_ref, vp_ref, vc_ref, bias_ref, sk_ref, do_ref,
             dq_ref, dk_ref, dv_ref, dbias_ref, dsk_ref, carry_k, carry_v):
        n = pl.program_id(0)

        @pl.when(n == 0)
        def _():
            dbias_ref[...] = jnp.zeros_like(dbias_ref)
            dsk_ref[...] = jnp.zeros_like(dsk_ref)
            carry_k[...] = jnp.zeros_like(carry_k)
            carry_v[...] = jnp.zeros_like(carry_v)

        @pl.when(n < nt)
        def _():
            f = lambda r: r[...].astype(F32)
            dq, dkall, dvall, dbias, dsk = _attn_tile_bwd(
                f(q_ref), f(kp_ref), f(kc_ref), f(vp_ref), f(vc_ref), _tile_bias(bias_ref, n == 0), sk_ref[...],
                f(do_ref))
            dkp, dkc, dvp, dvc = dkall[:lq], dkall[lq:], dvall[:lq], dvall[lq:]
            dq_ref[...] = _b(dq)
            done = tq - lq
            dk_ref[:done, :] = _b(carry_k[:done, :])
            dv_ref[:done, :] = _b(carry_v[:done, :])
            dk_ref[done:, :] = _b(carry_k[done:, :] + dkp)
            dv_ref[done:, :] = _b(carry_v[done:, :] + dvp)
            carry_k[...] = dkc
            carry_v[...] = dvc
            dsk_ref[...] += dsk
            first = (n == 0).astype(F32)
            for hk in range(N_KV_HEADS):
                total = dbias[0][hk]
                for b in range(1, ATTN_QB):
                    total = total + dbias[b][hk]
                dbias_ref[0, hk] += total - first * dbias[0][hk]
                dbias_ref[1, hk] += first * dbias[0][hk]

        @pl.when(n == nt)
        def _():
            dk_ref[...] = _b(carry_k[...])
            dv_ref[...] = _b(carry_v[...])

    cur = lambda n: (jnp.minimum(n, nt - 1), 0)
    done_map = lambda n: (jnp.maximum(n - 1, 0), 0)
    return _grid_call(
        body, nt + 1, (qkv, qkv, qkv, qkv, qkv, bias, sinks_rows, d_attn), exchange, name="attn_bwd",
        in_specs=_attn_in_specs(nt, True) + [pl.BlockSpec((tq, ATTN_WIDTH), cur)],
        out_specs=[pl.BlockSpec((tq, ATTN_WIDTH), cur), pl.BlockSpec((tq, KV_WIDTH), done_map),
                   pl.BlockSpec((tq, KV_WIDTH), done_map),
                   pl.BlockSpec((2, N_KV_HEADS, 4 * lq, 2 * lq), lambda n: (0, 0, 0, 0)), _vec(128)],
        out_shape=[jax.ShapeDtypeStruct((s, ATTN_WIDTH), BF16), jax.ShapeDtypeStruct((s, KV_WIDTH), BF16),
                   jax.ShapeDtypeStruct((s, KV_WIDTH), BF16),
                   jax.ShapeDtypeStruct((2, N_KV_HEADS, 4 * lq, 2 * lq), F32), jax.ShapeDtypeStruct((1, 128), F32)],
        scratch_shapes=[pltpu.VMEM((tq, KV_WIDTH), F32), pltpu.VMEM((tq, KV_WIDTH), F32)],
        compiler_params=_params())


def rel_bias_table(rel_bias, bucket):
    lq = ATTN_BLOCK
    group = N_Q_HEADS // N_KV_HEADS

    def body(rb_ref, bk_ref, o_ref):
        bk = bk_ref[...]
        prev_keys = lax.broadcasted_iota(jnp.int32, bk.shape, 1) < lq
        accs = [jnp.full(bk.shape, -1e30, F32) for _ in range(N_Q_HEADS)]
        for b in range(N_BUCKETS):
            hit = bk == b
            accs = [jnp.where(hit, rb_ref[b, h], acc) for h, acc in enumerate(accs)]
        for h in range(N_Q_HEADS):
            rows = slice((h % group) * lq, (h % group + 1) * lq)
            o_ref[0, h // group, rows, :] = accs[h]
            o_ref[1, h // group, rows, :] = jnp.where(prev_keys, -1e30, accs[h])

    return pl.pallas_call(
        body, name="rel_bias_table",
        in_specs=[pl.BlockSpec(memory_space=pltpu.SMEM), pl.BlockSpec(memory_space=pltpu.VMEM)],
        out_shape=jax.ShapeDtypeStruct((2, N_KV_HEADS, group * lq, 2 * lq), F32),
        compiler_params=_params(n_axes=0))(rel_bias, bucket)


def rel_bias_grad(dbias, bucket):
    lq = ATTN_BLOCK
    group = N_Q_HEADS // N_KV_HEADS

    def body(db_ref, bk_ref, o_ref):
        rows = lax.broadcasted_iota(jnp.int32, (N_BUCKETS, 128), 0)
        lanes = lax.broadcasted_iota(jnp.int32, (N_BUCKETS, 128), 1)
        bk = bk_ref[...]
        per_head = []
        for h in range(N_Q_HEADS):
            sl = slice((h % group) * lq, (h % group + 1) * lq)
            per_head.append(db_ref[0, h // group, sl, :] + db_ref[1, h // group, sl, :])

        def per_bucket(b, acc):
            hit = (bk == b).astype(F32)
            for h in range(N_Q_HEADS):
                val = jnp.sum(per_head[h] * hit, keepdims=True)
                acc = acc + jnp.where((rows == b) & (lanes == h), val, 0.0)
            return acc

        o_ref[...] = lax.fori_loop(0, N_BUCKETS, per_bucket, jnp.zeros((N_BUCKETS, 128), F32))

    return pl.pallas_call(body, name="rel_bias_grad", out_shape=jax.ShapeDtypeStruct((N_BUCKETS, 128), F32),
                          compiler_params=_params(n_axes=0))(dbias, bucket)


def _tri_sum(a, upper):
    n = a.shape[0]
    ri = lax.broadcasted_iota(jnp.int32, (n, n), 0)
    ci = lax.broadcasted_iota(jnp.int32, (n, n), 1)
    tri = ((ri <= ci) if upper else (ri >= ci)).astype(BF16)
    hi = a.astype(BF16)
    rest = a - hi.astype(F32)
    mid = rest.astype(BF16)
    lo = (rest - mid.astype(F32)).astype(BF16)
    dot = lambda part: jnp.dot(tri, part, preferred_element_type=F32)
    return dot(hi) + dot(mid) + dot(lo)


@jax.custom_vjp
def _cumsum_rows(a):
    return _tri_sum(a, False)


_cumsum_rows.defvjp(lambda a: (_tri_sum(a, False), None), lambda _, g: (_tri_sum(g, True),))


def _ssm_core(u, z, dt_raw, hprev, dt_bias, a_log, dskip, norm_w):
    lc = u.shape[0]
    xbc = _silu(u)
    xs, bm, cm = xbc[:, :SSM_WIDTH], xbc[:, SSM_WIDTH:SSM_WIDTH + SSM_BC], xbc[:, SSM_WIDTH + SSM_BC:]
    dt = jax.nn.softplus(dt_raw + dt_bias)
    adt = dt * (-jnp.exp(a_log))
    ri = lax.broadcasted_iota(jnp.int32, (lc, lc), 0)
    ci = lax.broadcasted_iota(jnp.int32, (lc, lc), 1)
    causal = ri >= ci
    acum = _cumsum_rows(adt)
    acum_t = acum.T
    last = acum[lc - 1:lc, :]
    per_group = SSM_HEADS // SSM_GROUPS
    lane = lax.broadcasted_iota(jnp.int32, (1, 128), 1)
    rowid = lax.broadcasted_iota(jnp.int32, (128, 1), 0)
    lo_lanes = lane < SSM_HEAD_DIM
    ys, hs = [], []
    for g in range(SSM_GROUPS):
        bg = bm[:, g * SSM_STATE:(g + 1) * SSM_STATE]
        cg = cm[:, g * SSM_STATE:(g + 1) * SSM_STATE]
        cb = mm_nt(cg, bg)
        for pp in range(per_group // 2):
            ha = g * per_group + 2 * pp
            xp = xs[:, ha * SSM_HEAD_DIM:(ha + 2) * SSM_HEAD_DIM]
            hp = hprev[ha * SSM_HEAD_DIM:(ha + 2) * SSM_HEAD_DIM, :]
            xcp = xp * jnp.where(lo_lanes, dt[:, ha:ha + 1], dt[:, ha + 1:ha + 2])
            y_h, st_h = [], []
            for h in (ha, ha + 1):
                col, rowv, lasth = acum[:, h:h + 1], acum_t[h:h + 1, :], last[:, h:h + 1]
                decay = jnp.exp(jnp.where(causal, col - rowv, -1e30))
                y_h.append(mm(cb * decay, xcp) + mm_nt(cg * jnp.exp(col), hp))
                st_h.append(mm_tn(xcp, bg * jnp.exp(lasth - col)))
            y_pair = jnp.where(lo_lanes, y_h[0], y_h[1])
            st_pair = jnp.where(rowid < SSM_HEAD_DIM, st_h[0], st_h[1])
            la, lb = last[:, ha:ha + 1], last[:, ha + 1:ha + 2]
            hs.append(jnp.exp(jnp.where(rowid < SSM_HEAD_DIM, la, lb)) * hp + st_pair)
            dsk = jnp.where(lo_lanes, dskip[:, ha:ha + 1], dskip[:, ha + 1:ha + 2])
            ys.append(y_pair + dsk * xp)
    y = jnp.concatenate(ys, axis=1) * _silu(z)
    gw = SSM_WIDTH // SSM_GROUPS
    outs = []
    for g in range(SSM_GROUPS):
        yg = y[:, g * gw:(g + 1) * gw]
        outs.append(yg * lax.rsqrt(jnp.mean(yg * yg, axis=-1, keepdims=True) + NORM_EPS))
    return jnp.concatenate(outs, axis=1) * norm_w, jnp.concatenate(hs, axis=0)


def _ssm_param_specs():
    return [pl.BlockSpec((SSM_CONV, XBC_WIDTH), _const), _vec(XBC_WIDTH), _vec(128), _vec(128), _vec(128),
            _vec(SSM_WIDTH)]


def ssm_fwd(xbc_raw, z, dt_raw, conv_w, conv_b, dt_bias, a_log, dskip, norm_w, exchange=None):
    s = xbc_raw.shape[0]
    lc = SSM_CHUNK
    nc = s // lc
    hrows = SSM_HEADS * SSM_HEAD_DIM

    def body(x_ref, halo_ref, z_ref, dt_ref, cw_ref, cb_ref, dtb_ref, al_ref, dk_ref, nw_ref,
             o_ref, hp_ref, state):
        i = pl.program_id(0)

        @pl.when(i == 0)
        def _():
            state[...] = jnp.zeros_like(state)

        halo = halo_ref[...] * (i > 0).astype(F32)
        xin = jnp.concatenate([halo, x_ref[...]], axis=0)
        u = (_conv_rows(xin, cw_ref[...], SSM_CONV) + cb_ref[...])[HALO:]
        hprev = state[...]
        hp_ref[...] = hprev
        out, hnew = _ssm_core(u, z_ref[...], dt_ref[...], hprev, dtb_ref[...], al_ref[...], dk_ref[...], nw_ref[...])
        o_ref[...] = _b(out)
        state[...] = hnew

    tile = lambda n: pl.BlockSpec((lc, n), _row)
    halo_spec = pl.BlockSpec((HALO, XBC_WIDTH), lambda i: (jnp.maximum(i * (lc // HALO) - 1, 0), 0))
    return _grid_call(
        body, nc, (xbc_raw, xbc_raw, z, dt_raw, conv_w, conv_b, dt_bias, a_log, dskip, norm_w), exchange,
        name="ssm_fwd",
        in_specs=[tile(XBC_WIDTH), halo_spec, tile(SSM_WIDTH), tile(DT_PAD)] + _ssm_param_specs(),
        out_specs=[tile(SSM_WIDTH), pl.BlockSpec((hrows, SSM_STATE), _row)],
        out_shape=[jax.ShapeDtypeStruct((s, SSM_WIDTH), BF16), jax.ShapeDtypeStruct((nc * hrows, SSM_STATE), F32)],
        scratch_shapes=[pltpu.VMEM((hrows, SSM_STATE), F32)],
        compiler_params=_params())


def ssm_bwd(xbc_raw, z, dt_raw, hprev_all, d_out, conv_w, conv_b, dt_bias, a_log, dskip, norm_w, exchange=None):
    s = xbc_raw.shape[0]
    lc = SSM_CHUNK
    nc = s // lc
    hrows = SSM_HEADS * SSM_HEAD_DIM

    def body(x_ref, halo_ref, z_ref, dt_ref, hp_ref, do_ref, cw_ref, cb_ref, dtb_ref, al_ref, dk_ref, nw_ref,
             dx_ref, dz_ref, ddt_ref, dcw_ref, dcb_ref, ddtb_ref, dal_ref, ddk_ref, dnw_ref, dstate, du_next):
        i = pl.program_id(0)
        chunk = nc - 1 - i

        @pl.when(i == 0)
        def _():
            dstate[...] = jnp.zeros_like(dstate)
            du_next[...] = jnp.zeros_like(du_next)
            for r in (dcw_ref, dcb_ref, ddtb_ref, dal_ref, ddk_ref, dnw_ref):
                r[...] = jnp.zeros_like(r)

        halo = halo_ref[...] * (chunk > 0).astype(F32)
        xin = jnp.concatenate([halo, x_ref[...]], axis=0)
        cw = cw_ref[...]
        u = (_conv_rows(xin, cw, SSM_CONV) + cb_ref[...])[HALO:]
        _, vjp = jax.vjp(_ssm_core, u, z_ref[...], dt_ref[...], hp_ref[...], dtb_ref[...], al_ref[...],
                         dk_ref[...], nw_ref[...])
        du, dz, ddt, dhp, ddtb, dal, ddk, dnw = vjp((do_ref[...], dstate[...]))
        dstate[...] = dhp
        dz_ref[...] = _b(dz)
        ddt_ref[...] = _b(ddt)
        du_ext = jnp.concatenate([du, du_next[...]], axis=0)
        dx_ref[...] = _b(_conv_rows_t(du_ext, cw, SSM_CONV)[:lc])
        du_next[...] = du[:HALO]
        rows = [jnp.sum(du * pltpu.roll(xin, j, axis=0)[HALO:] if j else du * xin[HALO:], axis=0, keepdims=True)
                for j in range(SSM_CONV)]
        dcw_ref[...] += jnp.concatenate(rows[::-1] + [jnp.zeros((8 - SSM_CONV, XBC_WIDTH), F32)], axis=0)
        dcb_ref[...] += jnp.sum(du, axis=0, keepdims=True)
        ddtb_ref[...] += ddtb
        dal_ref[...] += dal
        ddk_ref[...] += ddk
        dnw_ref[...] += dnw

    rev = lambda i: (nc - 1 - i, 0)
    tile = lambda n: pl.BlockSpec((lc, n), rev)
    halo_spec = pl.BlockSpec((HALO, XBC_WIDTH), lambda i: (jnp.maximum((nc - 1 - i) * (lc // HALO) - 1, 0), 0))
    acc = lambda r, n: pl.BlockSpec((r, n), _const)
    return _grid_call(
        body, nc, (xbc_raw, xbc_raw, z, dt_raw, hprev_all, d_out, conv_w, conv_b, dt_bias, a_log, dskip, norm_w),
        exchange, name="ssm_bwd",
        in_specs=[tile(XBC_WIDTH), halo_spec, tile(SSM_WIDTH), tile(DT_PAD), pl.BlockSpec((hrows, SSM_STATE), rev),
                  tile(SSM_WIDTH)] + _ssm_param_specs(),
        out_specs=[tile(XBC_WIDTH), tile(SSM_WIDTH), tile(DT_PAD), acc(8, XBC_WIDTH), acc(1, XBC_WIDTH),
                   acc(1, 128), acc(1, 128), acc(1, 128), acc(1, SSM_WIDTH)],
        out_shape=[jax.ShapeDtypeStruct((s, XBC_WIDTH), BF16), jax.ShapeDtypeStruct((s, SSM_WIDTH), BF16),
                   jax.ShapeDtypeStruct((s, DT_PAD), BF16), jax.ShapeDtypeStruct((8, XBC_WIDTH), F32),
                   jax.ShapeDtypeStruct((1, XBC_WIDTH), F32), jax.ShapeDtypeStruct((1, 128), F32),
                   jax.ShapeDtypeStruct((1, 128), F32), jax.ShapeDtypeStruct((1, 128), F32),
                   jax.ShapeDtypeStruct((1, SSM_WIDTH), F32)],
        scratch_shapes=[pltpu.VMEM((hrows, SSM_STATE), F32), pltpu.VMEM((HALO, XBC_WIDTH), F32)],
        compiler_params=_params())


def mix_out(attn, ssm, x, w_out16, gate1, post_mix_w, pre_ffn_w, scale2, shift2, tm=TOKEN_TILE):
    s = x.shape[0]

    def body(a_ref, s_ref, x_ref, w_ref, g_ref, pw_ref, fw_ref, sc_ref, sh_ref, mixed_ref, x1_ref, h2_ref):
        mixed = (jnp.dot(a_ref[...], w_ref[:ATTN_WIDTH, :], preferred_element_type=F32)
                 + jnp.dot(s_ref[...], w_ref[ATTN_WIDTH:, :], preferred_element_type=F32))
        mixed_ref[...] = mixed
        x1 = x_ref[...] + g_ref[...] * _rms(mixed, pw_ref[...])
        x1_ref[...] = x1
        h2_ref[...] = _b(_norm_mod(x1, fw_ref[...], sc_ref[...], sh_ref[...]))

    tile = lambda n: pl.BlockSpec((tm, n), _row)
    return pl.pallas_call(
        body, name="mix_out", grid=(s // tm,),
        in_specs=[tile(ATTN_WIDTH), tile(SSM_WIDTH), tile(D_MODEL), pl.BlockSpec((D_MODEL, D_MODEL), _const)]
        + [_vec(D_MODEL)] * 5,
        out_specs=[tile(D_MODEL)] * 3,
        out_shape=[jax.ShapeDtypeStruct((s, D_MODEL), F32), jax.ShapeDtypeStruct((s, D_MODEL), F32),
                   jax.ShapeDtypeStruct((s, D_MODEL), BF16)],
        compiler_params=_params())(attn, ssm, x, w_out16, gate1, post_mix_w, pre_ffn_w, scale2, shift2)


def _gate(ug, uv):
    return jax.nn.gelu(ug, approximate=True) * uv


def _gate_bwd(ug, uv, df):
    k0, k1 = math.sqrt(2.0 / math.pi), 0.044715
    sq = ug * ug
    t = jnp.tanh(k0 * ug * (1.0 + k1 * sq))
    half = 0.5 * (1.0 + t)
    slope = half + (0.5 * k0) * ug * (1.0 - t * t) * (1.0 + (3.0 * k1) * sq)
    return df * uv * slope, df * (ug * half)


def _resident(shape):
    return pl.BlockSpec(shape, _const, pipeline_mode=pl.Buffered(1))


def up_gate(h2, w_up16, conv_w, conv_b, tm=TOKEN_TILE):
    s = h2.shape[0]

    def body(h_ref, halo_ref, w_ref, cw_ref, cb_ref, u_ref, uraw_ref, f_ref):
        halo = halo_ref[...]
        halo = jnp.where(pl.program_id(0) > 0, halo, jnp.zeros_like(halo))
        hin = jnp.concatenate([halo, h_ref[...]], axis=0)
        for lo in range(0, D_FF, FF_CHUNK):
            halves = []
            for base in (lo, D_FF + lo):
                cols = slice(base, base + FF_CHUNK)
                uraw = jnp.dot(hin, w_ref[:, cols], preferred_element_type=F32)
                uraw_ref[:, cols] = _b(uraw[NEXT:])
                u = (_conv_rows(uraw, cw_ref[:, cols], FFN_CONV) + cb_ref[:, cols])[NEXT:]
                u_ref[:, cols] = u
                halves.append(u)
            f_ref[:, lo:lo + FF_CHUNK] = _b(_gate(*halves))

    tile = lambda n: pl.BlockSpec((tm, n), _row)
    halo_spec = pl.BlockSpec((NEXT, D_MODEL), lambda i: (jnp.maximum(i * (tm // NEXT) - 1, 0), 0))
    return pl.pallas_call(
        body, name="up_gate", grid=(s // tm,),
        in_specs=[tile(D_MODEL), halo_spec, _resident((D_MODEL, 2 * D_FF)),
                  pl.BlockSpec((FFN_CONV, 2 * D_FF), _const), _vec(2 * D_FF)],
        out_specs=[tile(2 * D_FF), tile(2 * D_FF), tile(D_FF)],
        out_shape=[jax.ShapeDtypeStruct((s, 2 * D_FF), F32), jax.ShapeDtypeStruct((s, 2 * D_FF), BF16),
                   jax.ShapeDtypeStruct((s, D_FF), BF16)],
        compiler_params=_params())(h2, h2, w_up16, conv_w, conv_b)


DOWN_LOSS_TILE = 512


def down_loss(f16, w_down16, x1, target, gate2, post_ffn_w, tm=DOWN_LOSS_TILE):
    s = x1.shape[0]
    tm = min(tm, s)

    def body(f_ref, wd_ref, x1_ref, t_ref, g_ref, pw_ref, dffn_ref, dy_ref, loss_ref, dg_ref, dpw_ref, gw_ref):
        i = pl.program_id(0)

        @pl.when(i == 0)
        def _():
            loss_ref[...] = jnp.zeros_like(loss_ref)
            dg_ref[...] = jnp.zeros_like(dg_ref)
            dpw_ref[...] = jnp.zeros_like(dpw_ref)
            gw_ref[...] = jnp.zeros_like(gw_ref)

        ffn = jnp.dot(f_ref[...], wd_ref[...], preferred_element_type=F32)
        x1 = x1_ref[...]
        x2 = x1 + g_ref[...] * _rms(ffn, pw_ref[...])
        err = x2 - t_ref[...]
        dy = err * (1.0 / D_MODEL)
        dy_ref[...] = dy
        loss_ref[...] += 0.5 * jnp.sum(jnp.mean(err * err, axis=-1, keepdims=True))
        dffn, dg, dpw = _gated_rms_bwd(ffn, g_ref[...], pw_ref[...], dy)
        dffn16 = _b(dffn)
        dffn_ref[...] = dffn16
        dg_ref[...] += dg
        dpw_ref[...] += dpw
        gw_ref[...] += _tn(f_ref[...], dffn16)

    tile = lambda n: pl.BlockSpec((tm, n), _row)
    return pl.pallas_call(
        body, name="down_loss", grid=(s // tm,),
        in_specs=[tile(D_FF), _resident((D_FF, D_MODEL)), tile(D_MODEL), tile(D_MODEL), _vec(D_MODEL), _vec(D_MODEL)],
        out_specs=[tile(D_MODEL), tile(D_MODEL), _vec(128), _vec(D_MODEL), _vec(D_MODEL),
                   pl.BlockSpec((D_FF, D_MODEL), _const)],
        out_shape=[jax.ShapeDtypeStruct((s, D_MODEL), BF16), jax.ShapeDtypeStruct((s, D_MODEL), F32),
                   jax.ShapeDtypeStruct((1, 128), F32), jax.ShapeDtypeStruct((1, D_MODEL), F32),
                   jax.ShapeDtypeStruct((1, D_MODEL), F32), jax.ShapeDtypeStruct((D_FF, D_MODEL), F32)],
        compiler_params=_params())(f16, w_down16, x1, target, gate2, post_ffn_w)


BWD_CHUNK = 256


def ffn_bwd(u, u_raw16, d_ffn, conv_w, w_down_t16, w_up_t16, tm=TOKEN_TILE):
    s = u.shape[0]
    nt = s // tm

    def body(u_ref, unext_ref, uraw_ref, d_ref, dnext_ref, cw_ref, wdt_ref, wut_ref,
             du_ref, dh_ref, dcw_ref, dcb_ref):
        i = pl.program_id(0)

        @pl.when(i == 0)
        def _():
            dcw_ref[...] = jnp.zeros_like(dcw_ref)
            dcb_ref[...] = jnp.zeros_like(dcb_ref)

        dnext = dnext_ref[...]
        dnext = jnp.where(i < nt - 1, dnext, jnp.zeros_like(dnext))
        dff = jnp.concatenate([d_ref[...], dnext], axis=0)
        rows_ext = tm + NEXT
        for lo in range(0, D_FF, BWD_CHUNK):
            gcols, vcols = slice(lo, lo + BWD_CHUNK), slice(D_FF + lo, D_FF + lo + BWD_CHUNK)
            ug = jnp.concatenate([u_ref[:, gcols], unext_ref[:, gcols]], axis=0)
            uv = jnp.concatenate([u_ref[:, vcols], unext_ref[:, vcols]], axis=0)
            df = jnp.dot(dff, wdt_ref[:, gcols], preferred_element_type=F32)
            for cols, du in zip((gcols, vcols), _gate_bwd(ug, uv, df)):
                cw = cw_ref[:, cols]
                du1 = pltpu.roll(du, rows_ext - 1, axis=0)
                du2 = pltpu.roll(du, rows_ext - 2, axis=0)
                du_ref[:, cols] = _b((du * cw[2:3, :] + du1 * cw[1:2, :] + du2 * cw[0:1, :])[:tm])
                xr = uraw_ref[:, cols].astype(F32)
                rows = [jnp.sum(xr * d_[:tm], axis=0, keepdims=True) for d_ in (du2, du1, du)]
                dcw_ref[:, cols] += jnp.concatenate(rows + [jnp.zeros((8 - FFN_CONV, BWD_CHUNK), F32)], axis=0)
                dcb_ref[:, cols] += jnp.sum(du[:tm], axis=0, keepdims=True)
        dh_ref[...] = jnp.dot(du_ref[...], wut_ref[...], preferred_element_type=F32)

    tile = lambda n: pl.BlockSpec((tm, n), _row)
    nxt = lambda i: (jnp.minimum((i + 1) * (tm // NEXT), s // NEXT - 1), 0)
    return pl.pallas_call(
        body, name="ffn_bwd", grid=(nt,),
        in_specs=[tile(2 * D_FF), pl.BlockSpec((NEXT, 2 * D_FF), nxt), tile(2 * D_FF), tile(D_MODEL),
                  pl.BlockSpec((NEXT, D_MODEL), nxt), pl.BlockSpec((FFN_CONV, 2 * D_FF), _const),
                  _resident((D_MODEL, D_FF)), _resident((2 * D_FF, D_MODEL))],
        out_specs=[tile(2 * D_FF), tile(D_MODEL), pl.BlockSpec((8, 2 * D_FF), _const), _vec(2 * D_FF)],
        out_shape=[jax.ShapeDtypeStruct((s, 2 * D_FF), BF16), jax.ShapeDtypeStruct((s, D_MODEL), F32),
                   jax.ShapeDtypeStruct((8, 2 * D_FF), F32), jax.ShapeDtypeStruct((1, 2 * D_FF), F32)],
        compiler_params=_params())(u, u, u_raw16, d_ffn, d_ffn, conv_w, w_down_t16, w_up_t16)


def mix_bwd(dh2, x1, dy, mixed, attn, ssm, w_out_t16, pre_ffn_w, scale2, gate1, post_mix_w, tm=TOKEN_TILE):
    s = x1.shape[0]

    def body(dh_ref, x1_ref, dy_ref, mx_ref, a_ref, s_ref, w_ref, fw_ref, sc_ref, g_ref, pw_ref,
             dx1_ref, da_ref, ds_ref, dfw_ref, dsc_ref, dsh_ref, dg_ref, dpw_ref, gw_ref):
        accs = (dfw_ref, dsc_ref, dsh_ref, dg_ref, dpw_ref)

        @pl.when(pl.program_id(0) == 0)
        def _():
            for r in accs + (gw_ref,):
                r[...] = jnp.zeros_like(r)

        dx1, dfw, dsc, dsh = _norm_mod_bwd(x1_ref[...], fw_ref[...], sc_ref[...], dh_ref[...])
        dx1 = dx1 + dy_ref[...]
        dx1_ref[...] = dx1
        dmixed, dg, dpw = _gated_rms_bwd(mx_ref[...], g_ref[...], pw_ref[...], dx1)
        dm16 = _b(dmixed)
        dmix_in = jnp.dot(dm16, w_ref[...], preferred_element_type=F32)
        da_ref[...] = _b(dmix_in[:, :ATTN_WIDTH])
        ds_ref[...] = dmix_in[:, ATTN_WIDTH:]
        gw_ref[:ATTN_WIDTH, :] += _tn(a_ref[...], dm16)
        gw_ref[ATTN_WIDTH:, :] += _tn(s_ref[...], dm16)
        for r, v in zip(accs, (dfw, dsc, dsh, dg, dpw)):
            r[...] += v

    tile = lambda n: pl.BlockSpec((tm, n), _row)
    return pl.pallas_call(
        body, name="mix_bwd", grid=(s // tm,),
        in_specs=[tile(D_MODEL)] * 4 + [tile(ATTN_WIDTH), tile(SSM_WIDTH), _resident((D_MODEL, D_MODEL))]
        + [_vec(D_MODEL)] * 4,
        out_specs=[tile(D_MODEL), tile(ATTN_WIDTH), tile(SSM_WIDTH)] + [_vec(D_MODEL)] * 5
        + [pl.BlockSpec((D_MODEL, D_MODEL), _const)],
        out_shape=[jax.ShapeDtypeStruct((s, D_MODEL), F32), jax.ShapeDtypeStruct((s, ATTN_WIDTH), BF16),
                   jax.ShapeDtypeStruct((s, SSM_WIDTH), F32)]
        + [jax.ShapeDtypeStruct((1, D_MODEL), F32)] * 5 + [jax.ShapeDtypeStruct((D_MODEL, D_MODEL), F32)],
        compiler_params=_params())(dh2, x1, dy, mixed, attn, ssm, w_out_t16, pre_ffn_w, scale2, gate1, post_mix_w)


INPROJ_BWD_TILE = 512


def inproj_bwd(dq, dk, dv, dxbc, dz, ddt, x, dx1, h1, w_in_t16, pre_mix_w, scale1, tm=INPROJ_BWD_TILE):
    s = x.shape[0]
    tm = min(tm, s)

    def body(dq_ref, dk_ref, dv_ref, dxbc_ref, dz_ref, ddt_ref, x_ref, dx1_ref, h_ref, w_ref, pw_ref, sc_ref,
             gx_ref, dpw_ref, dsc_ref, dsh_ref, gw_ref):
        accs = (dpw_ref, dsc_ref, dsh_ref)

        @pl.when(pl.program_id(0) == 0)
        def _():
            for r in accs + (gw_ref,):
                r[...] = jnp.zeros_like(r)

        h16 = h_ref[...]
        dh = None
        off_k, off_v = ATTN_WIDTH, ATTN_WIDTH + KV_WIDTH
        for r, lo, hi in ((dq_ref, 0, off_k), (dk_ref, off_k, off_v), (dv_ref, off_v, OFF_XBC),
                          (dxbc_ref, OFF_XBC, OFF_Z), (dz_ref, OFF_Z, OFF_DT), (ddt_ref, OFF_DT, PROJ_PAD)):
            d16 = _b(r[...])
            part = jnp.dot(d16, w_ref[lo:hi, :], preferred_element_type=F32)
            dh = part if dh is None else dh + part
            gw_ref[:, lo:hi] += _tn(h16, d16)
        dx, dpw, dsc, dsh = _norm_mod_bwd(x_ref[...], pw_ref[...], sc_ref[...], dh)
        gx_ref[...] = dx1_ref[...] + dx
        for r, v in zip(accs, (dpw, dsc, dsh)):
            r[...] += v

    tile = lambda n: pl.BlockSpec((tm, n), _row)
    return pl.pallas_call(
        body, name="inproj_bwd", grid=(s // tm,),
        in_specs=[tile(ATTN_WIDTH), tile(KV_WIDTH), tile(KV_WIDTH), tile(XBC_WIDTH), tile(SSM_WIDTH), tile(DT_PAD),
                  tile(D_MODEL), tile(D_MODEL), tile(D_MODEL), _resident((PROJ_PAD, D_MODEL))] + [_vec(D_MODEL)] * 2,
        out_specs=[tile(D_MODEL)] + [_vec(D_MODEL)] * 3 + [pl.BlockSpec((D_MODEL, PROJ_PAD), _const)],
        out_shape=[jax.ShapeDtypeStruct((s, D_MODEL), F32)] + [jax.ShapeDtypeStruct((1, D_MODEL), F32)] * 3
        + [jax.ShapeDtypeStruct((D_MODEL, PROJ_PAD), F32)],
        compiler_params=_params())(dq, dk, dv, dxbc, dz, ddt, x, dx1, h1, w_in_t16, pre_mix_w, scale1)


def _adam(g, w, m, v):
    new_m = ADAM_B1 * m + (1.0 - ADAM_B1) * g
    new_v = ADAM_B2 * v + (1.0 - ADAM_B2) * jnp.square(g)
    m_hat = new_m / (1.0 - ADAM_B1 ** ADAM_STEP)
    v_hat = new_v / (1.0 - ADAM_B2 ** ADAM_STEP)
    return -ADAM_LR * (m_hat / (jnp.sqrt(v_hat) + ADAM_EPS) + ADAM_WD * w), new_m, new_v


ROW_PARAMS = (("b_ada", 6144, 6144), ("pre_mix_w", 1024, 1024), ("attn_sinks", 128, 8), ("ssm_conv_b", 1024, 1024),
              ("ssm_dt_bias", 128, 8), ("ssm_a_log", 128, 8), ("ssm_d", 128, 8), ("ssm_norm_w", 512, 512),
              ("post_mix_w", 1024, 1024), ("pre_ffn_w", 1024, 1024), ("ffn_conv_b", 5632, 5632),
              ("post_ffn_w", 1024, 1024))
LOSS_LANES = 128


def adamw_small(row_all, rb_all, rel_bias_wmv, row_wmv):
    n_rows = len(ROW_PARAMS)

    def body(*refs):
        row_ref, rb_ref = refs[:2]
        wmv = refs[2:5 + 3 * n_rows]
        outs = refs[5 + 3 * n_rows:]
        g_row, g_rb = row_ref[0], rb_ref[0]
        for k in range(1, N_DEV):
            g_row = g_row + row_ref[k]
            g_rb = g_rb + rb_ref[k]
        outs[0][...] = g_row[:, :LOSS_LANES]
        grads = [g_rb[:, :N_Q_HEADS]]
        off = LOSS_LANES
        for _, lanes, width in ROW_PARAMS:
            grads.append(g_row[:, off:off + width])
            off += lanes
        for i, g in enumerate(grads):
            w_ref, m_ref, v_ref = wmv[3 * i:3 * i + 3]
            g_out, d_out, m_out, v_out = outs[1 + 4 * i:5 + 4 * i]
            g_out[...] = g
            d_out[...], m_out[...], v_out[...] = _adam(g, w_ref[...], m_ref[...], v_ref[...])

    flat_wmv = list(rel_bias_wmv) + [a for wmv in row_wmv for a in wmv]
    shapes = [jax.ShapeDtypeStruct((1, LOSS_LANES), F32)] + [jax.ShapeDtypeStruct((N_BUCKETS, N_Q_HEADS), F32)] * 4
    for _, _, width in ROW_PARAMS:
        shapes += [jax.ShapeDtypeStruct((1, width), F32)] * 4
    return pl.pallas_call(body, name="adamw_small", out_shape=shapes,
                          compiler_params=_params(n_axes=0))(row_all, rb_all, *flat_wmv)


def adamw(parts, w, m, v, name):
    p, r, n = parts.shape
    tr = _row_tile(r)

    def body(p_ref, w_ref, m_ref, v_ref, g_ref, d_ref, nm_ref, nv_ref):
        g = p_ref[0].astype(F32)
        for k in range(1, p):
            g = g + p_ref[k].astype(F32)
        g_ref[...] = g
        d_ref[...], nm_ref[...], nv_ref[...] = _adam(g, w_ref[...], m_ref[...], v_ref[...])

    tile = pl.BlockSpec((tr, n), _row)
    return pl.pallas_call(
        body, name=name, grid=(r // tr,),
        in_specs=[pl.BlockSpec((p, tr, n), lambda i: (0, i, 0)), tile, tile, tile],
        out_specs=[tile] * 4, out_shape=[jax.ShapeDtypeStruct((r, n), F32)] * 4,
        compiler_params=_params())(parts, w, m, v)


def _bucket_table():
    lq = ATTN_BLOCK
    qi = np.arange(lq)[:, None] + lq
    kj = np.arange(2 * lq)[None, :]
    dist = qi - kj
    d = np.maximum(dist, 0)
    max_exact = N_BUCKETS // 2
    nf = np.maximum(d, 1).astype(np.float32)
    large = max_exact + (np.log(nf / max_exact) / math.log(REL_MAX_DIST / max_exact)
                         * (N_BUCKETS - max_exact)).astype(np.int32)
    large = np.minimum(large, N_BUCKETS - 1)
    bucket = np.where(d < max_exact, d, large).astype(np.int32)
    in_band = (dist >= 0) & (dist < REL_MAX_DIST)
    return np.where(in_band, bucket, -1).astype(np.int32)


def _cols_from_blocks(g):
    return jnp.transpose(g, (1, 0, 2)).reshape(g.shape[1], N_DEV * g.shape[2])


def _cols_to_blocks(a):
    r, n = a.shape
    return jnp.transpose(a.reshape(r, N_DEV, n // N_DEV), (1, 0, 2))


def _perm_in_rows(wt):
    pad = jnp.zeros((DT_PAD - SSM_HEADS, wt.shape[1]), wt.dtype)
    return jnp.concatenate([wt[:768], wt[768:1280], wt[1792:2304], wt[1280:1792], wt[2304:2312], pad], axis=0)


def _unperm_in(g):
    return jnp.concatenate([g[:, :768], g[:, 768:1280], g[:, 1792:2304], g[:, 1280:1792], g[:, 2304:2312]], axis=1)


def _lane_pad(v, n=128):
    return jnp.pad(v, ((0, 0), (0, n - v.shape[1])))


def kernel(x, c, rel_bias, w_ada, b_ada, pre_mix_w, w_in, attn_sinks, ssm_conv_w, ssm_conv_b, ssm_dt_bias, ssm_a_log, ssm_d, ssm_norm_w, w_out, post_mix_w, pre_ffn_w, w_up, ffn_conv_w, ffn_conv_b, w_down, post_ffn_w, loss_target, m_rel_bias, m_w_ada, m_b_ada, m_pre_mix_w, m_w_in, m_attn_sinks, m_ssm_conv_w, m_ssm_conv_b, m_ssm_dt_bias, m_ssm_a_log, m_ssm_d, m_ssm_norm_w, m_w_out, m_post_mix_w, m_pre_ffn_w, m_w_up, m_ffn_conv_w, m_ffn_conv_b, m_w_down, m_post_ffn_w, v_rel_bias, v_w_ada, v_b_ada, v_pre_mix_w, v_w_in, v_attn_sinks, v_ssm_conv_w, v_ssm_conv_b, v_ssm_dt_bias, v_ssm_a_log, v_ssm_d, v_ssm_norm_w, v_w_out, v_post_mix_w, v_pre_ffn_w, v_w_up, v_ffn_conv_w, v_ffn_conv_b, v_w_down, v_post_ffn_w):
    weights = dict(rel_bias=rel_bias, w_ada=w_ada, b_ada=b_ada, pre_mix_w=pre_mix_w, w_in=w_in, attn_sinks=attn_sinks, ssm_conv_w=ssm_conv_w, ssm_conv_b=ssm_conv_b, ssm_dt_bias=ssm_dt_bias, ssm_a_log=ssm_a_log, ssm_d=ssm_d, ssm_norm_w=ssm_norm_w, w_out=w_out, post_mix_w=post_mix_w, pre_ffn_w=pre_ffn_w, w_up=w_up, ffn_conv_w=ffn_conv_w, ffn_conv_b=ffn_conv_b, w_down=w_down, post_ffn_w=post_ffn_w)
    mom_m = dict(rel_bias=m_rel_bias, w_ada=m_w_ada, b_ada=m_b_ada, pre_mix_w=m_pre_mix_w, w_in=m_w_in, attn_sinks=m_attn_sinks, ssm_conv_w=m_ssm_conv_w, ssm_conv_b=m_ssm_conv_b, ssm_dt_bias=m_ssm_dt_bias, ssm_a_log=m_ssm_a_log, ssm_d=m_ssm_d, ssm_norm_w=m_ssm_norm_w, w_out=m_w_out, post_mix_w=m_post_mix_w, pre_ffn_w=m_pre_ffn_w, w_up=m_w_up, ffn_conv_w=m_ffn_conv_w, ffn_conv_b=m_ffn_conv_b, w_down=m_w_down, post_ffn_w=m_post_ffn_w)
    mom_v = dict(rel_bias=v_rel_bias, w_ada=v_w_ada, b_ada=v_b_ada, pre_mix_w=v_pre_mix_w, w_in=v_w_in, attn_sinks=v_attn_sinks, ssm_conv_w=v_ssm_conv_w, ssm_conv_b=v_ssm_conv_b, ssm_dt_bias=v_ssm_dt_bias, ssm_a_log=v_ssm_a_log, ssm_d=v_ssm_d, ssm_norm_w=v_ssm_norm_w, w_out=v_w_out, post_mix_w=v_post_mix_w, pre_ffn_w=v_pre_ffn_w, w_up=v_w_up, ffn_conv_w=v_ffn_conv_w, ffn_conv_b=v_ffn_conv_b, w_down=v_w_down, post_ffn_w=v_post_ffn_w)
    order = ['rel_bias', 'w_ada', 'b_ada', 'pre_mix_w', 'w_in', 'attn_sinks', 'ssm_conv_w', 'ssm_conv_b', 'ssm_dt_bias', 'ssm_a_log', 'ssm_d', 'ssm_norm_w', 'w_out', 'post_mix_w', 'pre_ffn_w', 'w_up', 'ffn_conv_w', 'ffn_conv_b', 'w_down', 'post_ffn_w']

    me = 4 * lax.axis_index("x") + 2 * lax.axis_index("y") + lax.axis_index("c")
    xs_ = x[0]
    target = loss_target[0]

    (w_in_g, scw_g, fcw_g, c_g) = all_gather([_b(w_in[0]).T, ssm_conv_w[0], ffn_conv_w[0], c], "gather_weights")
    w_in_t16 = _perm_in_rows(w_in_g.reshape(IN_PROJ_WIDTH, D_MODEL))
    w_in16 = w_in_t16.T
    ssm_cw = _cols_from_blocks(scw_g)
    ffn_cw = _cols_from_blocks(fcw_g)
    c_all = c_g.reshape(N_DEV, D_MODEL)

    n_cols = w_ada.shape[2]
    b_cols = lax.dynamic_slice(b_ada, (0, me * n_cols), (1, n_cols))
    mod_part = ada_fwd(c_all, w_ada[0], b_cols)
    (mod_rows,) = all_to_all([mod_part.reshape(N_DEV, 1, n_cols)], "scatter_mod")
    mod = mod_rows.reshape(N_MOD, 1, D_MODEL)
    shift1, scale1, gate1, shift2, scale2, gate2 = (mod[i] for i in range(N_MOD))

    bucket_band = jnp.asarray(_bucket_table())
    bias = rel_bias_table(rel_bias, bucket_band)
    sinks_row = _lane_pad(attn_sinks)
    dt_bias, a_log, dskip = _lane_pad(ssm_dt_bias), _lane_pad(ssm_a_log), _lane_pad(ssm_d)

    h1, qkv, xbc_raw, z, dt_raw, w_out_g = pre_mix_inproj(
        xs_, pre_mix_w, scale1, shift1, w_in16, [(_b(w_out[0]), False)])
    attn, w_up_g = attn_fwd(qkv, bias, sinks_row, [(_b(w_up[0]).T, False)])
    ssm, hprev_all, w_down_g = ssm_fwd(xbc_raw, z, dt_raw, ssm_cw, ssm_conv_b, dt_bias, a_log, dskip, ssm_norm_w,
                                       [(_b(w_down[0]), False)])
    w_out16 = w_out_g.reshape(D_MODEL, D_MODEL)
    w_out_t16 = w_out16.T
    w_up_t16 = w_up_g.reshape(2 * D_FF, D_MODEL)
    w_up16 = w_up_t16.T
    w_down16 = w_down_g.reshape(D_FF, D_MODEL)
    w_down_t16 = w_down16.T
    mixed, x1, h2 = mix_out(attn, ssm, xs_, w_out16, gate1, post_mix_w, pre_ffn_w, scale2, shift2)
    u, u_raw16, f16 = up_gate(h2, w_up16, ffn_cw, ffn_conv_b)
    d_ffn, dy, loss_part, d_gate2, d_post_ffn_w, g_w_down = down_loss(f16, w_down16, x1, target, gate2, post_ffn_w)

    du_raw, dh2, d_ffn_cw, d_ffn_cb = ffn_bwd(u, u_raw16, d_ffn, ffn_cw, w_down_t16, w_up_t16)
    g_w_up = matmul_tn(h2, du_raw, "grad_w_up", D_MODEL, FF_CHUNK, tk=2048)
    (dx1, d_attn, d_ssm, d_pre_ffn_w, d_scale2, d_shift2, d_gate1, d_post_mix_w, g_w_out) = mix_bwd(
        dh2, x1, dy, mixed, attn, ssm, w_out_t16, pre_ffn_w, scale2, gate1, post_mix_w)
    dq, dk, dv, dbias, dsinks, p_w_down = attn_bwd(
        qkv, bias, sinks_row, d_attn, [(g_w_down.reshape(N_DEV, D_FF // N_DEV, D_MODEL), True)])
    d_rel_bias = rel_bias_grad(dbias, bucket_band)
    (dxbc, dz, ddt, d_ssm_cw, d_ssm_cb, d_dt_bias, d_a_log, d_dskip, d_norm_w, p_w_up, p_w_out) = ssm_bwd(
        xbc_raw, z, dt_raw, hprev_all, d_ssm, ssm_cw, ssm_conv_b, dt_bias, a_log, dskip, ssm_norm_w,
        [(_cols_to_blocks(g_w_up), True), (g_w_out.reshape(N_DEV, D_MODEL // N_DEV, D_MODEL), True)])
    grad_x, d_pre_mix_w, d_scale1, d_shift1, g_w_in_perm = inproj_bwd(
        dq, dk, dv, dxbc, dz, ddt, xs_, dx1, h1, w_in_t16, pre_mix_w, scale1)
    g_w_in = _unperm_in(g_w_in_perm)

    d_mod = jnp.concatenate([d_shift1, d_scale1, d_gate1, d_shift2, d_scale2, d_gate2], axis=1)
    late = ("w_in", "ssm_conv_w", "ffn_conv_w")
    full = [_cols_to_blocks(g_w_in), _cols_to_blocks(d_ssm_cw[:SSM_CONV]), _cols_to_blocks(d_ffn_cw[:FFN_CONV])]
    core = lax.axis_index("c").astype(jnp.int32).reshape(1)
    got = pair_exchange(full, "pair_grads")
    chip_sums = [pair_sum(f_, g_, core, "pair_sum_" + k) for k, f_, g_ in zip(late, full, got)]
    chip_parts = all_to_all(chip_sums, "scatter_grads", CHIP_FLIPS, _chip_index)
    (d_mod_rows,) = all_to_all([d_mod.reshape(N_DEV, 1, n_cols)], "scatter_dmod")
    g_w_ada = ada_bwd(c_all, d_mod_rows.reshape(N_DEV, n_cols))

    row_g = dict(b_ada=d_mod, pre_mix_w=d_pre_mix_w, attn_sinks=dsinks, ssm_conv_b=d_ssm_cb, ssm_dt_bias=d_dt_bias,
                 ssm_a_log=d_a_log, ssm_d=d_dskip, ssm_norm_w=d_norm_w, post_mix_w=d_post_mix_w,
                 pre_ffn_w=d_pre_ffn_w, ffn_conv_b=d_ffn_cb, post_ffn_w=d_post_ffn_w)
    row = jnp.concatenate([loss_part] + [row_g[k] for k, _, _ in ROW_PARAMS], axis=1)
    row_all, rb_all = all_gather([row, d_rel_bias], "gather_small")

    wmv = lambda k: (weights[k], mom_m[k], mom_v[k])
    small = adamw_small(row_all, rb_all, wmv("rel_bias"), [wmv(k) for k, _, _ in ROW_PARAMS])
    loss = small[0][0, 0]
    res = {k: tuple(small[1 + 4 * i:5 + 4 * i]) for i, k in enumerate(["rel_bias"] + [k for k, _, _ in ROW_PARAMS])}
    big = list(zip(late, chip_parts)) + [("w_down", p_w_down), ("w_up", p_w_up), ("w_out", p_w_out),
                                        ("w_ada", g_w_ada[None])]
    for k, parts in big:
        res[k] = tuple(o[None] for o in adamw(parts, weights[k][0], mom_m[k][0], mom_v[k][0], "adamw_" + k))

    outs = [loss, grad_x[None]]
    for field in range(4):
        outs += [res[k][field] for k in order]
    return tuple(outs)
```

```python
import math

import numpy as np
import jax
import jax.numpy as jnp
from jax import lax
from jax.experimental import pallas as pl
from jax.experimental.pallas import tpu as pltpu

F32 = jnp.float32
BF16 = jnp.bfloat16
MESH_ID = pl.DeviceIdType.MESH

N_DEV = 8
D_MODEL = 1024
N_Q_HEADS = 8
N_KV_HEADS = 2
HEAD_DIM = 64
ATTN_WIDTH = 512
KV_WIDTH = 128
ATTN_BLOCK = 128
N_BUCKETS = 32
REL_MAX_DIST = 128
SSM_HEADS = 8
SSM_HEAD_DIM = 64
SSM_WIDTH = 512
SSM_STATE = 128
SSM_GROUPS = 2
SSM_BC = 256
SSM_CONV = 4
SSM_CHUNK = 256
XBC_WIDTH = SSM_WIDTH + 2 * SSM_BC
D_FF = 2816
FFN_CONV = 3
NORM_EPS = 1e-6
N_MOD = 6
IN_PROJ_WIDTH = 2312
QKV_W = ATTN_WIDTH + 2 * KV_WIDTH
OFF_XBC = QKV_W
OFF_Z = OFF_XBC + XBC_WIDTH
OFF_DT = OFF_Z + SSM_WIDTH
DT_PAD = 128
PROJ_PAD = OFF_DT + DT_PAD
FF_CHUNK = 1408

ADAM_LR = 0.001
ADAM_B1 = 0.9
ADAM_B2 = 0.999
ADAM_EPS = 1e-08
ADAM_WD = 0.01
ADAM_STEP = 10

TOKEN_TILE = 256
HALO = 8
NEXT = 16
VMEM_LIMIT = 56 * 1024 * 1024


def _params(vmem=VMEM_LIMIT, n_axes=1):
    return pltpu.CompilerParams(dimension_semantics=("arbitrary",) * n_axes, vmem_limit_bytes=vmem)


def _b(x):
    return x.astype(BF16)


def _nn(a, b):
    return jnp.dot(_b(a), _b(b), preferred_element_type=F32)


def _nt(a, b):
    return lax.dot_general(_b(a), _b(b), (((1,), (1,)), ((), ())), preferred_element_type=F32)


def _tn(a, b):
    return lax.dot_general(_b(a), _b(b), (((0,), (0,)), ((), ())), preferred_element_type=F32)


@jax.custom_vjp
def mm(a, b):
    return _nn(a, b)


mm.defvjp(lambda a, b: (_nn(a, b), (a, b)),
          lambda r, g: (_nt(g, r[1]).astype(r[0].dtype), _tn(r[0], g).astype(r[1].dtype)))


@jax.custom_vjp
def mm_nt(a, b):
    return _nt(a, b)


mm_nt.defvjp(lambda a, b: (_nt(a, b), (a, b)),
             lambda r, g: (_nn(g, r[1]).astype(r[0].dtype), _tn(g, r[0]).astype(r[1].dtype)))


@jax.custom_vjp
def mm_tn(a, b):
    return _tn(a, b)


mm_tn.defvjp(lambda a, b: (_tn(a, b), (a, b)),
             lambda r, g: (_nt(r[1], g).astype(r[0].dtype), _nn(r[0], g).astype(r[1].dtype)))


def _rms(x, w):
    return x * lax.rsqrt(jnp.mean(x * x, axis=-1, keepdims=True) + NORM_EPS) * w


def _norm_mod(x, w, scale, shift):
    return _rms(x, w) * (1.0 + scale) + shift


def _rms_bwd(x, w, dy):
    r = lax.rsqrt(jnp.mean(x * x, axis=-1, keepdims=True) + NORM_EPS)
    xhat = x * r
    g = dy * w
    dx = r * (g - xhat * jnp.mean(g * xhat, axis=-1, keepdims=True))
    return dx, jnp.sum(dy * xhat, axis=0, keepdims=True)


def _norm_mod_bwd(x, w, scale, dh):
    dx, da = _rms_bwd(x, w * (1.0 + scale), dh)
    return dx, da * (1.0 + scale), da * w, jnp.sum(dh, axis=0, keepdims=True)


def _gated_rms_bwd(m, gate, w, dy):
    dm, t = _rms_bwd(m, w * gate, dy)
    return dm, t * w, t * gate


def _silu(x):
    return x * jax.nn.sigmoid(x)


def _conv_rows(xin, w, k):
    acc = xin * w[k - 1:k, :]
    for j in range(1, k):
        acc = acc + pltpu.roll(xin, j, axis=0) * w[k - 1 - j:k - j, :]
    return acc


def _conv_rows_t(du, w, k):
    n = du.shape[0]
    acc = du * w[k - 1:k, :]
    for j in range(1, k):
        acc = acc + pltpu.roll(du, n - j, axis=0) * w[k - 1 - j:k - j, :]
    return acc


def _row(i):
    return (i, 0)


def _const(i):
    return (0, 0)


def _vec(n):
    return pl.BlockSpec((1, n), _const)


def _block_index(p):
    return 4 * p[0] + 2 * p[1] + p[2]


def all_gather(arrs, name):
    n = len(arrs)

    def body(*refs):
        ins, outs = refs[:n], refs[n:2 * n]
        send_sems, recv_sems, local_sems = refs[2 * n:]
        x, y, c = lax.axis_index("x"), lax.axis_index("y"), lax.axis_index("c")
        me, sibling = (x, y, c), (x, y, 1 - c)
        chips = [(1 - x, y), (x, 1 - y), (1 - x, 1 - y)]

        def copy(a, k, block, to, src=None):
            dst = outs[a].at[_block_index(block)]
            return pltpu.make_async_remote_copy(
                src_ref=dst if src is None else src, dst_ref=dst,
                send_sem=send_sems.at[a * 7 + k], recv_sem=recv_sems.at[a * 7 + k],
                device_id=to, device_id_type=MESH_ID)

        mine = [pltpu.make_async_copy(ins[a], outs[a].at[_block_index(me)], local_sems.at[a]) for a in range(n)]
        for cp in mine:
            cp.start()
        first = []
        for a in range(n):
            first.append(copy(a, 0, me, sibling, src=ins[a]))
            first += [copy(a, 1 + j, me, (*chip, c), src=ins[a]) for j, chip in enumerate(chips)]
        for cp in first:
            cp.start()
        passed = []
        for j, chip in enumerate(chips):
            for a in range(n):
                copy(a, 1 + j, (*chip, c), me).wait_recv()
                cp = copy(a, 4 + j, (*chip, c), sibling)
                cp.start()
                passed.append(cp)
        for a in range(n):
            copy(a, 0, sibling, me).wait_recv()
            for j, chip in enumerate(chips):
                copy(a, 4 + j, (*chip, 1 - c), me).wait_recv()
        for cp in first + passed:
            cp.wait_send()
        for cp in mine:
            cp.wait()

    any_spec = pl.BlockSpec(memory_space=pl.ANY)
    return pl.pallas_call(
        body, name=name,
        out_shape=[jax.ShapeDtypeStruct((N_DEV,) + a.shape, a.dtype) for a in arrs],
        in_specs=[any_spec] * n, out_specs=[any_spec] * n,
        scratch_shapes=[pltpu.SemaphoreType.DMA((7 * n,)), pltpu.SemaphoreType.DMA((7 * n,)),
                        pltpu.SemaphoreType.DMA((n,))],
    )(*arrs)


ALL_FLIPS = ((0, 0, 1), (0, 1, 0), (0, 1, 1), (1, 0, 0), (1, 0, 1), (1, 1, 0), (1, 1, 1))
CHIP_FLIPS = ((0, 1, 0), (1, 0, 0), (1, 1, 0))


def _chip_index(p):
    return 2 * p[0] + p[1]


def all_to_all(arrs, name, flips=ALL_FLIPS, index=_block_index):
    n = len(arrs)
    nf = len(flips)

    def body(*refs):
        ins, outs = refs[:n], refs[n:2 * n]
        send_sems, recv_sems, local_sems = refs[2 * n:]
        pos = (lax.axis_index("x"), lax.axis_index("y"), lax.axis_index("c"))
        me = index(pos)
        peers = [tuple(1 - p if f else p for p, f in zip(pos, flip)) for flip in flips]

        def copy(a, k):
            peer = peers[k]
            return pltpu.make_async_remote_copy(
                src_ref=ins[a].at[index(peer)], dst_ref=outs[a].at[me],
                send_sem=send_sems.at[a * nf + k], recv_sem=recv_sems.at[a * nf + k],
                device_id=peer, device_id_type=MESH_ID)

        def landed(a, k):
            slot = outs[a].at[index(peers[k])]
            return pltpu.make_async_remote_copy(
                src_ref=slot, dst_ref=slot,
                send_sem=send_sems.at[a * nf + k], recv_sem=recv_sems.at[a * nf + k],
                device_id=peers[k], device_id_type=MESH_ID)

        mine = [pltpu.make_async_copy(ins[a].at[me], outs[a].at[me], local_sems.at[a]) for a in range(n)]
        for cp in mine:
            cp.start()
        sent = [copy(a, k) for a in range(n) for k in range(nf)]
        for cp in sent:
            cp.start()
        for a in range(n):
            for k in range(nf):
                landed(a, k).wait_recv()
        for cp in sent:
            cp.wait_send()
        for cp in mine:
            cp.wait()

    any_spec = pl.BlockSpec(memory_space=pl.ANY)
    return pl.pallas_call(
        body, name=name,
        out_shape=[jax.ShapeDtypeStruct(a.shape, a.dtype) for a in arrs],
        in_specs=[any_spec] * n, out_specs=[any_spec] * n,
        scratch_shapes=[pltpu.SemaphoreType.DMA((nf * n,)), pltpu.SemaphoreType.DMA((nf * n,)),
                        pltpu.SemaphoreType.DMA((n,))],
    )(*arrs)


def _direct_exchange(src, dst, sems, scatter):
    send_sems, recv_sems, local_sem = sems
    pos = (lax.axis_index("x"), lax.axis_index("y"), lax.axis_index("c"))
    me = _block_index(pos)
    peers = [tuple(1 - p if f else p for p, f in zip(pos, flip)) for flip in ALL_FLIPS]

    def outgoing(k):
        return pltpu.make_async_remote_copy(
            src_ref=src.at[_block_index(peers[k])] if scatter else src, dst_ref=dst.at[me],
            send_sem=send_sems.at[k], recv_sem=recv_sems.at[k], device_id=peers[k], device_id_type=MESH_ID)

    def incoming(k):
        slot = dst.at[_block_index(peers[k])]
        return pltpu.make_async_remote_copy(
            src_ref=slot, dst_ref=slot, send_sem=send_sems.at[k], recv_sem=recv_sems.at[k],
            device_id=peers[k], device_id_type=MESH_ID)

    def local():
        return pltpu.make_async_copy(src.at[me] if scatter else src, dst.at[me], local_sem)

    def start():
        local().start()
        for k in range(len(ALL_FLIPS)):
            outgoing(k).start()

    def finish():
        for k in range(len(ALL_FLIPS)):
            incoming(k).wait_recv()
        for k in range(len(ALL_FLIPS)):
            outgoing(k).wait_send()
        local().wait()

    return start, finish


def hosted_call(body, exchanges, steps, n_in, n_out, **call):
    n_ex = len(exchanges)

    def wrapped(*refs):
        ins, srcs = refs[:n_in], refs[n_in:n_in + n_ex]
        outs = refs[n_in + n_ex:n_in + n_ex + n_out]
        dsts = refs[n_in + n_ex + n_out:n_in + 2 * n_ex + n_out]
        rest = refs[n_in + 2 * n_ex + n_out:]
        scratch, sems = rest[:len(rest) - 3 * n_ex], rest[len(rest) - 3 * n_ex:]
        plans = [_direct_exchange(srcs[e], dsts[e], sems[3 * e:3 * e + 3], exchanges[e][1]) for e in range(n_ex)]

        @pl.when(pl.program_id(0) == 0)
        def _():
            for start, _ in plans:
                start()

        body(*ins, *outs, *scratch)

        @pl.when(pl.program_id(0) == steps - 1)
        def _():
            for _, finish in plans:
                finish()

    any_spec = pl.BlockSpec(memory_space=pl.ANY)
    landings = [jax.ShapeDtypeStruct(src.shape if scatter else (N_DEV,) + src.shape, src.dtype)
                for src, scatter in exchanges]
    n_flips = len(ALL_FLIPS)
    sems = [pltpu.SemaphoreType.DMA((n_flips,)), pltpu.SemaphoreType.DMA((n_flips,)), pltpu.SemaphoreType.DMA(())]
    return pl.pallas_call(
        wrapped, grid=(steps,),
        in_specs=list(call.pop("in_specs")) + [any_spec] * n_ex,
        out_specs=list(call.pop("out_specs")) + [any_spec] * n_ex,
        out_shape=list(call.pop("out_shape")) + landings,
        scratch_shapes=list(call.pop("scratch_shapes", [])) + sems * n_ex,
        **call)


def _grid_call(body, steps, args, exchanges, **call):
    if not exchanges:
        return pl.pallas_call(body, grid=(steps,), **call)(*args)
    srcs = [src for src, _ in exchanges]
    return hosted_call(body, exchanges, steps, len(args), len(call["out_shape"]), **call)(*args, *srcs)


N_CHIPS = 4


def pair_exchange(arrs, name):
    n = len(arrs)

    def body(*refs):
        ins, outs = refs[:n], refs[n:2 * n]
        send_sems, recv_sems = refs[2 * n:]
        x, y, c = lax.axis_index("x"), lax.axis_index("y"), lax.axis_index("c")
        sibling = (x, y, 1 - c)
        sent = []
        for a in range(n):
            for q in range(N_CHIPS):
                cp = pltpu.make_async_remote_copy(
                    src_ref=ins[a].at[2 * q + (1 - c)], dst_ref=outs[a].at[q],
                    send_sem=send_sems.at[a * N_CHIPS + q], recv_sem=recv_sems.at[a * N_CHIPS + q],
                    device_id=sibling, device_id_type=MESH_ID)
                cp.start()
                sent.append(cp)
        for cp in sent:
            cp.wait_recv()
        for cp in sent:
            cp.wait_send()

    any_spec = pl.BlockSpec(memory_space=pl.ANY)
    return pl.pallas_call(
        body, name=name,
        out_shape=[jax.ShapeDtypeStruct((N_CHIPS,) + a.shape[1:], a.dtype) for a in arrs],
        in_specs=[any_spec] * n, out_specs=[any_spec] * n,
        scratch_shapes=[pltpu.SemaphoreType.DMA((N_CHIPS * n,)), pltpu.SemaphoreType.DMA((N_CHIPS * n,))],
    )(*arrs)


def pair_sum(full, got, core, name):
    _, r, n = full.shape
    tr = _row_tile(r)

    def body(c_ref, mine_ref, got_ref, o_ref):
        o_ref[...] = _b(mine_ref[...] + got_ref[...])

    grid_spec = pltpu.PrefetchScalarGridSpec(
        num_scalar_prefetch=1, grid=(N_CHIPS, r // tr),
        in_specs=[pl.BlockSpec((1, tr, n), lambda q, i, c_ref: (2 * q + c_ref[0], i, 0)),
                  pl.BlockSpec((1, tr, n), lambda q, i, c_ref: (q, i, 0))],
        out_specs=pl.BlockSpec((1, tr, n), lambda q, i, c_ref: (q, i, 0)))
    return pl.pallas_call(body, name=name, grid_spec=grid_spec,
                          out_shape=jax.ShapeDtypeStruct((N_CHIPS, r, n), BF16),
                          compiler_params=_params(n_axes=2))(core, full, got)


def _row_tile(r):
    for cand in (256, 128, 64, 32, 16):
        if r % cand == 0 and r > cand:
            return cand
    return r


def ada_fwd(c_all, w_ada, b_cols):
    def body(c_ref, w_ref, b_ref, o_ref):
        o_ref[...] = _nn(_silu(c_ref[...]), w_ref[...]) + b_ref[...]

    return pl.pallas_call(body, name="ada_fwd",
                          out_shape=jax.ShapeDtypeStruct((N_DEV, w_ada.shape[1]), F32),
                          compiler_params=_params(n_axes=0))(c_all, w_ada, b_cols)


def ada_bwd(c_all, g_cols):
    def body(c_ref, g_ref, o_ref):
        o_ref[...] = _tn(_silu(c_ref[...]), g_ref[...])

    return pl.pallas_call(body, name="ada_bwd",
                          out_shape=jax.ShapeDtypeStruct((c_all.shape[1], g_cols.shape[1]), F32),
                          compiler_params=_params(n_axes=0))(c_all, g_cols)


def matmul_tn(a, b, name, bm, bn, tk=512):
    s, m = a.shape
    n = b.shape[1]
    tk = min(tk, s)

    def body(a_ref, b_ref, o_ref):
        @pl.when(pl.program_id(2) == 0)
        def _():
            o_ref[...] = jnp.zeros_like(o_ref)

        o_ref[...] += _tn(a_ref[...], b_ref[...])

    return pl.pallas_call(
        body, name=name, grid=(m // bm, n // bn, s // tk),
        in_specs=[pl.BlockSpec((tk, bm), lambda i, j, k: (k, i)), pl.BlockSpec((tk, bn), lambda i, j, k: (k, j))],
        out_specs=pl.BlockSpec((bm, bn), lambda i, j, k: (i, j)),
        out_shape=jax.ShapeDtypeStruct((m, n), F32),
        compiler_params=_params(n_axes=3))(a, b)


def pre_mix_inproj(x, w, scale, shift, w_in16, exchange=None, tm=2 * TOKEN_TILE):
    s = x.shape[0]

    def body(x_ref, w_ref, sc_ref, sh_ref, win_ref, h_ref, qkv_ref, xbc_ref, z_ref, dt_ref):
        h16 = _b(_norm_mod(x_ref[...], w_ref[...], sc_ref[...], sh_ref[...]))
        h_ref[...] = h16
        dot = lambda lo, hi: jnp.dot(h16, win_ref[:, lo:hi], preferred_element_type=F32)
        qkv_ref[...] = _b(dot(0, OFF_XBC))
        xbc_ref[...] = dot(OFF_XBC, OFF_Z)
        z_ref[...] = dot(OFF_Z, OFF_DT)
        dt_ref[...] = dot(OFF_DT, PROJ_PAD)

    tile = lambda n: pl.BlockSpec((tm, n), _row)
    return _grid_call(
        body, s // tm, (x, w, scale, shift, w_in16), exchange, name="pre_mix_inproj",
        in_specs=[tile(D_MODEL), _vec(D_MODEL), _vec(D_MODEL), _vec(D_MODEL), pl.BlockSpec((D_MODEL, PROJ_PAD), _const)],
        out_specs=[tile(D_MODEL), tile(QKV_W), tile(XBC_WIDTH), tile(SSM_WIDTH), tile(DT_PAD)],
        out_shape=[jax.ShapeDtypeStruct((s, D_MODEL), BF16), jax.ShapeDtypeStruct((s, QKV_W), BF16),
                   jax.ShapeDtypeStruct((s, XBC_WIDTH), F32), jax.ShapeDtypeStruct((s, SSM_WIDTH), F32),
                   jax.ShapeDtypeStruct((s, DT_PAD), F32)],
        compiler_params=_params())


ATTN_QB_FWD, ATTN_QB_BWD = 4, 2


def _attn_tile(q, kp, kc, vp, vc, bias, sinks):
    lq = ATTN_BLOCK
    group = N_Q_HEADS // N_KV_HEADS
    lanes = lax.broadcasted_iota(jnp.int32, (1, 128), 1)
    rid = lax.broadcasted_iota(jnp.int32, (group * lq, 1), 0)
    sink_cols = []
    for hk in range(N_KV_HEADS):
        sink = jnp.zeros((group * lq, 1), F32)
        for g in range(group):
            s_h = jnp.sum(jnp.where(lanes == hk * group + g, sinks, 0.0), axis=-1, keepdims=True)
            sink = jnp.where((rid >= g * lq) & (rid < (g + 1) * lq), s_h, sink)
        sink_cols.append(sink)
    kall = jnp.concatenate([kp, kc], axis=0)
    vall = jnp.concatenate([vp, vc], axis=0)
    blocks = []
    for b in range(q.shape[0] // lq):
        qb = q[b * lq:(b + 1) * lq]
        outs = []
        for hk in range(N_KV_HEADS):
            cols = slice(hk * HEAD_DIM, (hk + 1) * HEAD_DIM)
            kb = kall[b * lq:(b + 2) * lq, cols]
            vb = vall[b * lq:(b + 2) * lq, cols]
            qg = jnp.concatenate([qb[:, (hk * group + g) * HEAD_DIM:(hk * group + g + 1) * HEAD_DIM]
                                  for g in range(group)], axis=0)
            sc = mm_nt(qg, kb) * (HEAD_DIM ** -0.5) + bias[b][hk]
            sink = sink_cols[hk]
            m = lax.stop_gradient(jnp.maximum(jnp.max(sc, axis=-1, keepdims=True), sink))
            p = jnp.exp(sc - m)
            probs = p / (jnp.sum(p, axis=-1, keepdims=True) + jnp.exp(sink - m))
            og = mm(probs, vb)
            outs += [og[g * lq:(g + 1) * lq] for g in range(group)]
        blocks.append(jnp.concatenate(outs, axis=1))
    return jnp.concatenate(blocks, axis=0)


def _attn_tile_bwd(q, kp, kc, vp, vc, bias, sinks, do):
    lq = ATTN_BLOCK
    group = N_Q_HEADS // N_KV_HEADS
    scale = HEAD_DIM ** -0.5
    lanes = lax.broadcasted_iota(jnp.int32, (1, 128), 1)
    rid = lax.broadcasted_iota(jnp.int32, (group * lq, 1), 0)
    sink_cols = []
    for hk in range(N_KV_HEADS):
        sink = jnp.zeros((group * lq, 1), F32)
        for g in range(group):
            s_h = jnp.sum(jnp.where(lanes == hk * group + g, sinks, 0.0), axis=-1, keepdims=True)
            sink = jnp.where((rid >= g * lq) & (rid < (g + 1) * lq), s_h, sink)
        sink_cols.append(sink)
    kall = jnp.concatenate([kp, kc], axis=0)
    vall = jnp.concatenate([vp, vc], axis=0)
    dsk = jnp.zeros((1, 128), F32)
    dq_blocks, dbias = [], []
    nqb = q.shape[0] // lq
    dk_parts = [[None] * nqb for _ in range(N_KV_HEADS)]
    dv_parts = [[None] * nqb for _ in range(N_KV_HEADS)]
    for b in range(nqb):
        qb, dob = q[b * lq:(b + 1) * lq], do[b * lq:(b + 1) * lq]
        dq_heads, dbias_b = [], []
        for hk in range(N_KV_HEADS):
            cols = slice(hk * HEAD_DIM, (hk + 1) * HEAD_DIM)
            kb = kall[b * lq:(b + 2) * lq, cols]
            vb = vall[b * lq:(b + 2) * lq, cols]
            heads = [hk * group + g for g in range(group)]
            qg = jnp.concatenate([qb[:, h * HEAD_DIM:(h + 1) * HEAD_DIM] for h in heads], axis=0)
            dog = jnp.concatenate([dob[:, h * HEAD_DIM:(h + 1) * HEAD_DIM] for h in heads], axis=0)
            sink = sink_cols[hk]
            sc = _nt(qg, kb) * scale + bias[b][hk]
            m = jnp.maximum(jnp.max(sc, axis=-1, keepdims=True), sink)
            p = jnp.exp(sc - m)
            es = jnp.exp(sink - m)
            inv = 1.0 / (jnp.sum(p, axis=-1, keepdims=True) + es)
            probs = p * inv
            dprobs = _nt(dog, vb)
            delta = jnp.sum(probs * dprobs, axis=-1, keepdims=True)
            dsc = probs * (dprobs - delta)
            dbias_b.append(dsc)
            dsink = -(es * inv) * delta
            for g, h in enumerate(heads):
                tot = jnp.sum(dsink[g * lq:(g + 1) * lq], axis=0, keepdims=True)
                dsk = dsk + jnp.where(lanes == h, tot, 0.0)
            dqg = _nn(dsc, kb) * scale
            dq_heads += [dqg[g * lq:(g + 1) * lq] for g in range(group)]
            dk_parts[hk][b] = _tn(dsc, qg) * scale
            dv_parts[hk][b] = _tn(probs, dog)
        dq_blocks.append(jnp.concatenate(dq_heads, axis=1))
        dbias.append(dbias_b)

    def overlap_add(parts):
        chunks = []
        for r in range(nqb + 1):
            acc = None
            if r < nqb:
                acc = parts[r][:lq]
            if r >= 1:
                tail = parts[r - 1][lq:]
                acc = tail if acc is None else acc + tail
            chunks.append(acc)
        return jnp.concatenate(chunks, axis=0)

    dkall = jnp.concatenate([overlap_add(dk_parts[hk]) for hk in range(N_KV_HEADS)], axis=1)
    dvall = jnp.concatenate([overlap_add(dv_parts[hk]) for hk in range(N_KV_HEADS)], axis=1)
    return jnp.concatenate(dq_blocks, axis=0), dkall, dvall, dbias, dsk


def _attn_in_specs(nt, clamp, nqb):
    lq, tq = ATTN_BLOCK, ATTN_BLOCK * nqb
    cur = lambda n: jnp.minimum(n, nt - 1) if clamp else n
    prev = lambda n: jnp.maximum(cur(n) * nqb - 1, 0)
    kcol, vcol = ATTN_WIDTH // KV_WIDTH, ATTN_WIDTH // KV_WIDTH + 1
    return [pl.BlockSpec((tq, ATTN_WIDTH), lambda n: (cur(n), 0)),
            pl.BlockSpec((lq, KV_WIDTH), lambda n: (prev(n), kcol)),
            pl.BlockSpec((tq, KV_WIDTH), lambda n: (cur(n), kcol)),
            pl.BlockSpec((lq, KV_WIDTH), lambda n: (prev(n), vcol)),
            pl.BlockSpec((tq, KV_WIDTH), lambda n: (cur(n), vcol)),
            pl.BlockSpec((2, N_KV_HEADS, 4 * lq, 2 * lq), lambda n: (0, 0, 0, 0)),
            _vec(128)]


def _tile_bias(bias_ref, first, nqb):
    return [[jnp.where(first, bias_ref[1, hk], bias_ref[0, hk]) if b == 0 else bias_ref[0, hk]
             for hk in range(N_KV_HEADS)] for b in range(nqb)]


def attn_fwd(qkv, bias, sinks_rows, exchange=None):
    s = qkv.shape[0]
    nqb = min(ATTN_QB_FWD, s // ATTN_BLOCK)
    tq = ATTN_BLOCK * nqb
    nt = s // tq

    def body(q_ref, kp_ref, kc_ref, vp_ref, vc_ref, bias_ref, sk_ref, o_ref):
        f = lambda r: r[...].astype(F32)
        o = _attn_tile(f(q_ref), f(kp_ref), f(kc_ref), f(vp_ref), f(vc_ref),
                       _tile_bias(bias_ref, pl.program_id(0) == 0, nqb), sk_ref[...])
        o_ref[...] = _b(o)

    return _grid_call(
        body, nt, (qkv, qkv, qkv, qkv, qkv, bias, sinks_rows), exchange, name="attn_fwd",
        in_specs=_attn_in_specs(nt, False, nqb),
        out_specs=[pl.BlockSpec((tq, ATTN_WIDTH), _row)],
        out_shape=[jax.ShapeDtypeStruct((s, ATTN_WIDTH), BF16)],
        compiler_params=_params())


def attn_bwd(qkv, bias, sinks_rows, d_attn, exchange=None):
    s = qkv.shape[0]
    nqb = ATTN_QB_BWD
    lq, tq = ATTN_BLOCK, ATTN_BLOCK * nqb
    nt = s // tq

    def body(q_ref, kp_ref, kc_ref, vp_ref, vc_ref, bias_ref, sk_ref, do_ref,
             dq_ref, dk_ref, dv_ref, dbias_ref, dsk_ref, carry_k, carry_v):
        n = pl.program_id(0)

        @pl.when(n == 0)
        def _():
            dbias_ref[...] = jnp.zeros_like(dbias_ref)
            dsk_ref[...] = jnp.zeros_like(dsk_ref)
            carry_k[...] = jnp.zeros_like(carry_k)
            carry_v[...] = jnp.zeros_like(carry_v)

        @pl.when(n < nt)
        def _():
            f = lambda r: r[...].astype(F32)
            dq, dkall, dvall, dbias, dsk = _attn_tile_bwd(
                f(q_ref), f(kp_ref), f(kc_ref), f(vp_ref), f(vc_ref), _tile_bias(bias_ref, n == 0, nqb), sk_ref[...],
                f(do_ref))
            dkp, dkc, dvp, dvc = dkall[:lq], dkall[lq:], dvall[:lq], dvall[lq:]
            dq_ref[...] = _b(dq)
            done = tq - lq
            dk_ref[:done, :] = _b(carry_k[:done, :])
            dv_ref[:done, :] = _b(carry_v[:done, :])
            dk_ref[done:, :] = _b(carry_k[done:, :] + dkp)
            dv_ref[done:, :] = _b(carry_v[done:, :] + dvp)
            carry_k[...] = dkc
            carry_v[...] = dvc
            dsk_ref[...] += dsk
            first = (n == 0).astype(F32)
            for hk in range(N_KV_HEADS):
                total = dbias[0][hk]
                for b in range(1, nqb):
                    total = total + dbias[b][hk]
                dbias_ref[0, hk] += total - first * dbias[0][hk]
                dbias_ref[1, hk] += first * dbias[0][hk]

        @pl.when(n == nt)
        def _():
            dk_ref[...] = _b(carry_k[...])
            dv_ref[...] = _b(carry_v[...])

    cur = lambda n: (jnp.minimum(n, nt - 1), 0)
    done_map = lambda n: (jnp.maximum(n - 1, 0), 0)
    return _grid_call(
        body, nt + 1, (qkv, qkv, qkv, qkv, qkv, bias, sinks_rows, d_attn), exchange, name="attn_bwd",
        in_specs=_attn_in_specs(nt, True, nqb) + [pl.BlockSpec((tq, ATTN_WIDTH), cur)],
        out_specs=[pl.BlockSpec((tq, ATTN_WIDTH), cur), pl.BlockSpec((tq, KV_WIDTH), done_map),
                   pl.BlockSpec((tq, KV_WIDTH), done_map),
                   pl.BlockSpec((2, N_KV_HEADS, 4 * lq, 2 * lq), lambda n: (0, 0, 0, 0)), _vec(128)],
        out_shape=[jax.ShapeDtypeStruct((s, ATTN_WIDTH), BF16), jax.ShapeDtypeStruct((s, KV_WIDTH), BF16),
                   jax.ShapeDtypeStruct((s, KV_WIDTH), BF16),
                   jax.ShapeDtypeStruct((2, N_KV_HEADS, 4 * lq, 2 * lq), F32), jax.ShapeDtypeStruct((1, 128), F32)],
        scratch_shapes=[pltpu.VMEM((tq, KV_WIDTH), F32), pltpu.VMEM((tq, KV_WIDTH), F32)],
        compiler_params=_params())


def rel_bias_table(rel_bias, bucket):
    lq = ATTN_BLOCK
    group = N_Q_HEADS // N_KV_HEADS

    def body(rb_ref, bk_ref, o_ref):
        bk = bk_ref[...]
        prev_keys = lax.broadcasted_iota(jnp.int32, bk.shape, 1) < lq
        accs = [jnp.full(bk.shape, -1e30, F32) for _ in range(N_Q_HEADS)]
        for b in range(N_BUCKETS):
            hit = bk == b
            accs = [jnp.where(hit, rb_ref[b, h], acc) for h, acc in enumerate(accs)]
        for h in range(N_Q_HEADS):
            rows = slice((h % group) * lq, (h % group + 1) * lq)
            o_ref[0, h // group, rows, :] = accs[h]
            o_ref[1, h // group, rows, :] = jnp.where(prev_keys, -1e30, accs[h])

    return pl.pallas_call(
        body, name="rel_bias_table",
        in_specs=[pl.BlockSpec(memory_space=pltpu.SMEM), pl.BlockSpec(memory_space=pltpu.VMEM)],
        out_shape=jax.ShapeDtypeStruct((2, N_KV_HEADS, group * lq, 2 * lq), F32),
        compiler_params=_params(n_axes=0))(rel_bias, bucket)


def rel_bias_grad(dbias, bucket):
    lq = ATTN_BLOCK
    group = N_Q_HEADS // N_KV_HEADS

    def body(db_ref, bk_ref, o_ref):
        rows = lax.broadcasted_iota(jnp.int32, (N_BUCKETS, 128), 0)
        lanes = lax.broadcasted_iota(jnp.int32, (N_BUCKETS, 128), 1)
        bk = bk_ref[...]
        per_head = []
        for h in range(N_Q_HEADS):
            sl = slice((h % group) * lq, (h % group + 1) * lq)
            per_head.append(db_ref[0, h // group, sl, :] + db_ref[1, h // group, sl, :])

        def per_bucket(b, acc):
            hit = (bk == b).astype(F32)
            for h in range(N_Q_HEADS):
                val = jnp.sum(per_head[h] * hit, keepdims=True)
                acc = acc + jnp.where((rows == b) & (lanes == h), val, 0.0)
            return acc

        o_ref[...] = lax.fori_loop(0, N_BUCKETS, per_bucket, jnp.zeros((N_BUCKETS, 128), F32))

    return pl.pallas_call(body, name="rel_bias_grad", out_shape=jax.ShapeDtypeStruct((N_BUCKETS, 128), F32),
                          compiler_params=_params(n_axes=0))(dbias, bucket)


def _tri_sum(a, upper):
    n = a.shape[0]
    ri = lax.broadcasted_iota(jnp.int32, (n, n), 0)
    ci = lax.broadcasted_iota(jnp.int32, (n, n), 1)
    tri = ((ri <= ci) if upper else (ri >= ci)).astype(BF16)
    hi = a.astype(BF16)
    rest = a - hi.astype(F32)
    mid = rest.astype(BF16)
    lo = (rest - mid.astype(F32)).astype(BF16)
    dot = lambda part: jnp.dot(tri, part, preferred_element_type=F32)
    return dot(hi) + dot(mid) + dot(lo)


@jax.custom_vjp
def _cumsum_rows(a):
    return _tri_sum(a, False)


_cumsum_rows.defvjp(lambda a: (_tri_sum(a, False), None), lambda _, g: (_tri_sum(g, True),))


def _ssm_core(u, z, dt_raw, hprev, dt_bias, a_log, dskip, norm_w):
    lc = u.shape[0]
    xbc = _silu(u)
    xs, bm, cm = xbc[:, :SSM_WIDTH], xbc[:, SSM_WIDTH:SSM_WIDTH + SSM_BC], xbc[:, SSM_WIDTH + SSM_BC:]
    dt = jax.nn.softplus(dt_raw + dt_bias)
    adt = dt * (-jnp.exp(a_log))
    ri = lax.broadcasted_iota(jnp.int32, (lc, lc), 0)
    ci = lax.broadcasted_iota(jnp.int32, (lc, lc), 1)
    causal = ri >= ci
    acum = _cumsum_rows(adt)
    acum_t = acum.T
    last = acum[lc - 1:lc, :]
    per_group = SSM_HEADS // SSM_GROUPS
    lane = lax.broadcasted_iota(jnp.int32, (1, 128), 1)
    rowid = lax.broadcasted_iota(jnp.int32, (128, 1), 0)
    lo_lanes = lane < SSM_HEAD_DIM
    ys, hs = [], []
    for g in range(SSM_GROUPS):
        bg = bm[:, g * SSM_STATE:(g + 1) * SSM_STATE]
        cg = cm[:, g * SSM_STATE:(g + 1) * SSM_STATE]
        cb = mm_nt(cg, bg)
        for pp in range(per_group // 2):
            ha = g * per_group + 2 * pp
            xp = xs[:, ha * SSM_HEAD_DIM:(ha + 2) * SSM_HEAD_DIM]
            hp = hprev[ha * SSM_HEAD_DIM:(ha + 2) * SSM_HEAD_DIM, :]
            xcp = xp * jnp.where(lo_lanes, dt[:, ha:ha + 1], dt[:, ha + 1:ha + 2])
            y_h, st_h = [], []
            for h in (ha, ha + 1):
                col, rowv, lasth = acum[:, h:h + 1], acum_t[h:h + 1, :], last[:, h:h + 1]
                decay = jnp.exp(jnp.where(causal, col - rowv, -1e30))
                y_h.append(mm(cb * decay, xcp) + mm_nt(cg * jnp.exp(col), hp))
                st_h.append(mm_tn(xcp, bg * jnp.exp(lasth - col)))
            y_pair = jnp.where(lo_lanes, y_h[0], y_h[1])
            st_pair = jnp.where(rowid < SSM_HEAD_DIM, st_h[0], st_h[1])
            la, lb = last[:, ha:ha + 1], last[:, ha + 1:ha + 2]
            hs.append(jnp.exp(jnp.where(rowid < SSM_HEAD_DIM, la, lb)) * hp + st_pair)
            dsk = jnp.where(lo_lanes, dskip[:, ha:ha + 1], dskip[:, ha + 1:ha + 2])
            ys.append(y_pair + dsk * xp)
    y = jnp.concatenate(ys, axis=1) * _silu(z)
    gw = SSM_WIDTH // SSM_GROUPS
    outs = []
    for g in range(SSM_GROUPS):
        yg = y[:, g * gw:(g + 1) * gw]
        outs.append(yg * lax.rsqrt(jnp.mean(yg * yg, axis=-1, keepdims=True) + NORM_EPS))
    return jnp.concatenate(outs, axis=1) * norm_w, jnp.concatenate(hs, axis=0)


def _ssm_param_specs():
    return [pl.BlockSpec((SSM_CONV, XBC_WIDTH), _const), _vec(XBC_WIDTH), _vec(128), _vec(128), _vec(128),
            _vec(SSM_WIDTH)]


def ssm_fwd(xbc_raw, z, dt_raw, conv_w, conv_b, dt_bias, a_log, dskip, norm_w, exchange=None):
    s = xbc_raw.shape[0]
    lc = SSM_CHUNK
    nc = s // lc
    hrows = SSM_HEADS * SSM_HEAD_DIM

    def body(x_ref, halo_ref, z_ref, dt_ref, cw_ref, cb_ref, dtb_ref, al_ref, dk_ref, nw_ref,
             o_ref, hp_ref, state):
        i = pl.program_id(0)

        @pl.when(i == 0)
        def _():
            state[...] = jnp.zeros_like(state)

        halo = halo_ref[...] * (i > 0).astype(F32)
        xin = jnp.concatenate([halo, x_ref[...]], axis=0)
        u = (_conv_rows(xin, cw_ref[...], SSM_CONV) + cb_ref[...])[HALO:]
        hprev = state[...]
        hp_ref[...] = hprev
        out, hnew = _ssm_core(u, z_ref[...], dt_ref[...], hprev, dtb_ref[...], al_ref[...], dk_ref[...], nw_ref[...])
        o_ref[...] = _b(out)
        state[...] = hnew

    tile = lambda n: pl.BlockSpec((lc, n), _row)
    halo_spec = pl.BlockSpec((HALO, XBC_WIDTH), lambda i: (jnp.maximum(i * (lc // HALO) - 1, 0), 0))
    return _grid_call(
        body, nc, (xbc_raw, xbc_raw, z, dt_raw, conv_w, conv_b, dt_bias, a_log, dskip, norm_w), exchange,
        name="ssm_fwd",
        in_specs=[tile(XBC_WIDTH), halo_spec, tile(SSM_WIDTH), tile(DT_PAD)] + _ssm_param_specs(),
        out_specs=[tile(SSM_WIDTH), pl.BlockSpec((hrows, SSM_STATE), _row)],
        out_shape=[jax.ShapeDtypeStruct((s, SSM_WIDTH), BF16), jax.ShapeDtypeStruct((nc * hrows, SSM_STATE), F32)],
        scratch_shapes=[pltpu.VMEM((hrows, SSM_STATE), F32)],
        compiler_params=_params())


def ssm_bwd(xbc_raw, z, dt_raw, hprev_all, d_out, conv_w, conv_b, dt_bias, a_log, dskip, norm_w, exchange=None):
    s = xbc_raw.shape[0]
    lc = SSM_CHUNK
    nc = s // lc
    hrows = SSM_HEADS * SSM_HEAD_DIM

    def body(x_ref, halo_ref, z_ref, dt_ref, hp_ref, do_ref, cw_ref, cb_ref, dtb_ref, al_ref, dk_ref, nw_ref,
             dx_ref, dz_ref, ddt_ref, dcw_ref, dcb_ref, ddtb_ref, dal_ref, ddk_ref, dnw_ref, dstate, du_next):
        i = pl.program_id(0)
        chunk = nc - 1 - i

        @pl.when(i == 0)
        def _():
            dstate[...] = jnp.zeros_like(dstate)
            du_next[...] = jnp.zeros_like(du_next)
            for r in (dcw_ref, dcb_ref, ddtb_ref, dal_ref, ddk_ref, dnw_ref):
                r[...] = jnp.zeros_like(r)

        halo = halo_ref[...] * (chunk > 0).astype(F32)
        xin = jnp.concatenate([halo, x_ref[...]], axis=0)
        cw = cw_ref[...]
        u = (_conv_rows(xin, cw, SSM_CONV) + cb_ref[...])[HALO:]
        _, vjp = jax.vjp(_ssm_core, u, z_ref[...], dt_ref[...], hp_ref[...], dtb_ref[...], al_ref[...],
                         dk_ref[...], nw_ref[...])
        du, dz, ddt, dhp, ddtb, dal, ddk, dnw = vjp((do_ref[...], dstate[...]))
        dstate[...] = dhp
        dz_ref[...] = _b(dz)
        ddt_ref[...] = _b(ddt)
        du_ext = jnp.concatenate([du, du_next[...]], axis=0)
        dx_ref[...] = _b(_conv_rows_t(du_ext, cw, SSM_CONV)[:lc])
        du_next[...] = du[:HALO]
        rows = [jnp.sum(du * pltpu.roll(xin, j, axis=0)[HALO:] if j else du * xin[HALO:], axis=0, keepdims=True)
                for j in range(SSM_CONV)]
        dcw_ref[...] += jnp.concatenate(rows[::-1] + [jnp.zeros((8 - SSM_CONV, XBC_WIDTH), F32)], axis=0)
        dcb_ref[...] += jnp.sum(du, axis=0, keepdims=True)
        ddtb_ref[...] += ddtb
        dal_ref[...] += dal
        ddk_ref[...] += ddk
        dnw_ref[...] += dnw

    rev = lambda i: (nc - 1 - i, 0)
    tile = lambda n: pl.BlockSpec((lc, n), rev)
    halo_spec = pl.BlockSpec((HALO, XBC_WIDTH), lambda i: (jnp.maximum((nc - 1 - i) * (lc // HALO) - 1, 0), 0))
    acc = lambda r, n: pl.BlockSpec((r, n), _const)
    return _grid_call(
        body, nc, (xbc_raw, xbc_raw, z, dt_raw, hprev_all, d_out, conv_w, conv_b, dt_bias, a_log, dskip, norm_w),
        exchange, name="ssm_bwd",
        in_specs=[tile(XBC_WIDTH), halo_spec, tile(SSM_WIDTH), tile(DT_PAD), pl.BlockSpec((hrows, SSM_STATE), rev),
                  tile(SSM_WIDTH)] + _ssm_param_specs(),
        out_specs=[tile(XBC_WIDTH), tile(SSM_WIDTH), tile(DT_PAD), acc(8, XBC_WIDTH), acc(1, XBC_WIDTH),
                   acc(1, 128), acc(1, 128), acc(1, 128), acc(1, SSM_WIDTH)],
        out_shape=[jax.ShapeDtypeStruct((s, XBC_WIDTH), BF16), jax.ShapeDtypeStruct((s, SSM_WIDTH), BF16),
                   jax.ShapeDtypeStruct((s, DT_PAD), BF16), jax.ShapeDtypeStruct((8, XBC_WIDTH), F32),
                   jax.ShapeDtypeStruct((1, XBC_WIDTH), F32), jax.ShapeDtypeStruct((1, 128), F32),
                   jax.ShapeDtypeStruct((1, 128), F32), jax.ShapeDtypeStruct((1, 128), F32),
                   jax.ShapeDtypeStruct((1, SSM_WIDTH), F32)],
        scratch_shapes=[pltpu.VMEM((hrows, SSM_STATE), F32), pltpu.VMEM((HALO, XBC_WIDTH), F32)],
        compiler_params=_params())


def mix_out(attn, ssm, x, w_out16, gate1, post_mix_w, pre_ffn_w, scale2, shift2, tm=TOKEN_TILE):
    s = x.shape[0]

    def body(a_ref, s_ref, x_ref, w_ref, g_ref, pw_ref, fw_ref, sc_ref, sh_ref, mixed_ref, x1_ref, h2_ref):
        mixed = (jnp.dot(a_ref[...], w_ref[:ATTN_WIDTH, :], preferred_element_type=F32)
                 + jnp.dot(s_ref[...], w_ref[ATTN_WIDTH:, :], preferred_element_type=F32))
        mixed_ref[...] = mixed
        x1 = x_ref[...] + g_ref[...] * _rms(mixed, pw_ref[...])
        x1_ref[...] = x1
        h2_ref[...] = _b(_norm_mod(x1, fw_ref[...], sc_ref[...], sh_ref[...]))

    tile = lambda n: pl.BlockSpec((tm, n), _row)
    return pl.pallas_call(
        body, name="mix_out", grid=(s // tm,),
        in_specs=[tile(ATTN_WIDTH), tile(SSM_WIDTH), tile(D_MODEL), pl.BlockSpec((D_MODEL, D_MODEL), _const)]
        + [_vec(D_MODEL)] * 5,
        out_specs=[tile(D_MODEL)] * 3,
        out_shape=[jax.ShapeDtypeStruct((s, D_MODEL), F32), jax.ShapeDtypeStruct((s, D_MODEL), F32),
                   jax.ShapeDtypeStruct((s, D_MODEL), BF16)],
        compiler_params=_params())(attn, ssm, x, w_out16, gate1, post_mix_w, pre_ffn_w, scale2, shift2)


GELU_K0, GELU_K1 = math.sqrt(2.0 / math.pi), 0.044715


def _gate(ug, uv):
    return jax.nn.gelu(ug, approximate=True) * uv


def _gate_bwd(ug, uv, df):
    sq = ug * ug
    t = jnp.tanh(ug * (GELU_K0 + (GELU_K0 * GELU_K1) * sq))
    half = 0.5 + 0.5 * t
    slope = half + ug * (1.0 - t * t) * (0.5 * GELU_K0 + (1.5 * GELU_K0 * GELU_K1) * sq)
    return df * uv * slope, df * (ug * half)


def _resident(shape):
    return pl.BlockSpec(shape, _const, pipeline_mode=pl.Buffered(1))


def up_gate(h2, w_up16, conv_w, conv_b, tm=TOKEN_TILE):
    s = h2.shape[0]

    def body(h_ref, halo_ref, w_ref, cw_ref, cb_ref, u_ref, uraw_ref, f_ref):
        halo = halo_ref[...]
        halo = jnp.where(pl.program_id(0) > 0, halo, jnp.zeros_like(halo))
        hin = jnp.concatenate([halo, h_ref[...]], axis=0)
        for lo in range(0, D_FF, FF_CHUNK):
            halves = []
            for base in (lo, D_FF + lo):
                cols = slice(base, base + FF_CHUNK)
                uraw = jnp.dot(hin, w_ref[:, cols], preferred_element_type=F32)
                uraw_ref[:, cols] = _b(uraw[NEXT:])
                u = (_conv_rows(uraw, cw_ref[:, cols], FFN_CONV) + cb_ref[:, cols])[NEXT:]
                u_ref[:, cols] = u
                halves.append(u)
            f_ref[:, lo:lo + FF_CHUNK] = _b(_gate(*halves))

    tile = lambda n: pl.BlockSpec((tm, n), _row)
    halo_spec = pl.BlockSpec((NEXT, D_MODEL), lambda i: (jnp.maximum(i * (tm // NEXT) - 1, 0), 0))
    return pl.pallas_call(
        body, name="up_gate", grid=(s // tm,),
        in_specs=[tile(D_MODEL), halo_spec, _resident((D_MODEL, 2 * D_FF)),
                  pl.BlockSpec((FFN_CONV, 2 * D_FF), _const), _vec(2 * D_FF)],
        out_specs=[tile(2 * D_FF), tile(2 * D_FF), tile(D_FF)],
        out_shape=[jax.ShapeDtypeStruct((s, 2 * D_FF), F32), jax.ShapeDtypeStruct((s, 2 * D_FF), BF16),
                   jax.ShapeDtypeStruct((s, D_FF), BF16)],
        compiler_params=_params())(h2, h2, w_up16, conv_w, conv_b)


DOWN_LOSS_TILE = 512


def down_loss(f16, w_down16, x1, target, gate2, post_ffn_w, tm=DOWN_LOSS_TILE):
    s = x1.shape[0]
    tm = min(tm, s)

    def body(f_ref, wd_ref, x1_ref, t_ref, g_ref, pw_ref, dffn_ref, dy_ref, loss_ref, dg_ref, dpw_ref, gw_ref):
        i = pl.program_id(0)

        @pl.when(i == 0)
        def _():
            loss_ref[...] = jnp.zeros_like(loss_ref)
            dg_ref[...] = jnp.zeros_like(dg_ref)
            dpw_ref[...] = jnp.zeros_like(dpw_ref)
            gw_ref[...] = jnp.zeros_like(gw_ref)

        ffn = jnp.dot(f_ref[...], wd_ref[...], preferred_element_type=F32)
        x1 = x1_ref[...]
        x2 = x1 + g_ref[...] * _rms(ffn, pw_ref[...])
        err = x2 - t_ref[...]
        dy = err * (1.0 / D_MODEL)
        dy_ref[...] = dy
        loss_ref[...] += 0.5 * jnp.sum(jnp.mean(err * err, axis=-1, keepdims=True))
        dffn, dg, dpw = _gated_rms_bwd(ffn, g_ref[...], pw_ref[...], dy)
        dffn16 = _b(dffn)
        dffn_ref[...] = dffn16
        dg_ref[...] += dg
        dpw_ref[...] += dpw
        gw_ref[...] += _tn(f_ref[...], dffn16)

    tile = lambda n: pl.BlockSpec((tm, n), _row)
    return pl.pallas_call(
        body, name="down_loss", grid=(s // tm,),
        in_specs=[tile(D_FF), _resident((D_FF, D_MODEL)), tile(D_MODEL), tile(D_MODEL), _vec(D_MODEL), _vec(D_MODEL)],
        out_specs=[tile(D_MODEL), tile(D_MODEL), _vec(128), _vec(D_MODEL), _vec(D_MODEL),
                   pl.BlockSpec((D_FF, D_MODEL), _const)],
        out_shape=[jax.ShapeDtypeStruct((s, D_MODEL), BF16), jax.ShapeDtypeStruct((s, D_MODEL), F32),
                   jax.ShapeDtypeStruct((1, 128), F32), jax.ShapeDtypeStruct((1, D_MODEL), F32),
                   jax.ShapeDtypeStruct((1, D_MODEL), F32), jax.ShapeDtypeStruct((D_FF, D_MODEL), F32)],
        compiler_params=_params())(f16, w_down16, x1, target, gate2, post_ffn_w)


BWD_CHUNK = 256


def ffn_bwd(u, u_raw16, d_ffn, conv_w, w_down_t16, w_up_t16, tm=TOKEN_TILE):
    s = u.shape[0]
    nt = s // tm

    def body(u_ref, unext_ref, uraw_ref, d_ref, dnext_ref, cw_ref, wdt_ref, wut_ref,
             du_ref, dh_ref, dcw_ref, dcb_ref):
        i = pl.program_id(0)

        @pl.when(i == 0)
        def _():
            dcw_ref[...] = jnp.zeros_like(dcw_ref)
            dcb_ref[...] = jnp.zeros_like(dcb_ref)

        dnext = dnext_ref[...]
        dnext = jnp.where(i < nt - 1, dnext, jnp.zeros_like(dnext))
        dff = jnp.concatenate([d_ref[...], dnext], axis=0)
        rows_ext = tm + NEXT
        for lo in range(0, D_FF, BWD_CHUNK):
            gcols, vcols = slice(lo, lo + BWD_CHUNK), slice(D_FF + lo, D_FF + lo + BWD_CHUNK)
            ug = jnp.concatenate([u_ref[:, gcols], unext_ref[:, gcols]], axis=0)
            uv = jnp.concatenate([u_ref[:, vcols], unext_ref[:, vcols]], axis=0)
            df = jnp.dot(dff, wdt_ref[:, gcols], preferred_element_type=F32)
            for cols, du in zip((gcols, vcols), _gate_bwd(ug, uv, df)):
                cw = cw_ref[:, cols]
                du1 = pltpu.roll(du, rows_ext - 1, axis=0)
                du2 = pltpu.roll(du, rows_ext - 2, axis=0)
                du_ref[:, cols] = _b((du * cw[2:3, :] + du1 * cw[1:2, :] + du2 * cw[0:1, :])[:tm])
                xr = uraw_ref[:, cols].astype(F32)
                rows = [jnp.sum(xr * d_[:tm], axis=0, keepdims=True) for d_ in (du2, du1, du)]
                dcw_ref[:, cols] += jnp.concatenate(rows + [jnp.zeros((8 - FFN_CONV, BWD_CHUNK), F32)], axis=0)
                dcb_ref[:, cols] += jnp.sum(du[:tm], axis=0, keepdims=True)
        dh_ref[...] = jnp.dot(du_ref[...], wut_ref[...], preferred_element_type=F32)

    tile = lambda n: pl.BlockSpec((tm, n), _row)
    nxt = lambda i: (jnp.minimum((i + 1) * (tm // NEXT), s // NEXT - 1), 0)
    return pl.pallas_call(
        body, name="ffn_bwd", grid=(nt,),
        in_specs=[tile(2 * D_FF), pl.BlockSpec((NEXT, 2 * D_FF), nxt), tile(2 * D_FF), tile(D_MODEL),
                  pl.BlockSpec((NEXT, D_MODEL), nxt), pl.BlockSpec((FFN_CONV, 2 * D_FF), _const),
                  _resident((D_MODEL, D_FF)), _resident((2 * D_FF, D_MODEL))],
        out_specs=[tile(2 * D_FF), tile(D_MODEL), pl.BlockSpec((8, 2 * D_FF), _const), _vec(2 * D_FF)],
        out_shape=[jax.ShapeDtypeStruct((s, 2 * D_FF), BF16), jax.ShapeDtypeStruct((s, D_MODEL), F32),
                   jax.ShapeDtypeStruct((8, 2 * D_FF), F32), jax.ShapeDtypeStruct((1, 2 * D_FF), F32)],
        compiler_params=_params())(u, u, u_raw16, d_ffn, d_ffn, conv_w, w_down_t16, w_up_t16)


def mix_bwd(dh2, x1, dy, mixed, attn, ssm, w_out_t16, pre_ffn_w, scale2, gate1, post_mix_w, tm=TOKEN_TILE):
    s = x1.shape[0]

    def body(dh_ref, x1_ref, dy_ref, mx_ref, a_ref, s_ref, w_ref, fw_ref, sc_ref, g_ref, pw_ref,
             dx1_ref, da_ref, ds_ref, dfw_ref, dsc_ref, dsh_ref, dg_ref, dpw_ref, gw_ref):
        accs = (dfw_ref, dsc_ref, dsh_ref, dg_ref, dpw_ref)

        @pl.when(pl.program_id(0) == 0)
        def _():
            for r in accs + (gw_ref,):
                r[...] = jnp.zeros_like(r)

        dx1, dfw, dsc, dsh = _norm_mod_bwd(x1_ref[...], fw_ref[...], sc_ref[...], dh_ref[...])
        dx1 = dx1 + dy_ref[...]
        dx1_ref[...] = dx1
        dmixed, dg, dpw = _gated_rms_bwd(mx_ref[...], g_ref[...], pw_ref[...], dx1)
        dm16 = _b(dmixed)
        dmix_in = jnp.dot(dm16, w_ref[...], preferred_element_type=F32)
        da_ref[...] = _b(dmix_in[:, :ATTN_WIDTH])
        ds_ref[...] = dmix_in[:, ATTN_WIDTH:]
        gw_ref[:ATTN_WIDTH, :] += _tn(a_ref[...], dm16)
        gw_ref[ATTN_WIDTH:, :] += _tn(s_ref[...], dm16)
        for r, v in zip(accs, (dfw, dsc, dsh, dg, dpw)):
            r[...] += v

    tile = lambda n: pl.BlockSpec((tm, n), _row)
    return pl.pallas_call(
        body, name="mix_bwd", grid=(s // tm,),
        in_specs=[tile(D_MODEL)] * 4 + [tile(ATTN_WIDTH), tile(SSM_WIDTH), _resident((D_MODEL, D_MODEL))]
        + [_vec(D_MODEL)] * 4,
        out_specs=[tile(D_MODEL), tile(ATTN_WIDTH), tile(SSM_WIDTH)] + [_vec(D_MODEL)] * 5
        + [pl.BlockSpec((D_MODEL, D_MODEL), _const)],
        out_shape=[jax.ShapeDtypeStruct((s, D_MODEL), F32), jax.ShapeDtypeStruct((s, ATTN_WIDTH), BF16),
                   jax.ShapeDtypeStruct((s, SSM_WIDTH), F32)]
        + [jax.ShapeDtypeStruct((1, D_MODEL), F32)] * 5 + [jax.ShapeDtypeStruct((D_MODEL, D_MODEL), F32)],
        compiler_params=_params())(dh2, x1, dy, mixed, attn, ssm, w_out_t16, pre_ffn_w, scale2, gate1, post_mix_w)


INPROJ_BWD_TILE = 512


def inproj_bwd(dq, dk, dv, dxbc, dz, ddt, x, dx1, h1, w_in_t16, pre_mix_w, scale1, tm=INPROJ_BWD_TILE):
    s = x.shape[0]
    tm = min(tm, s)

    def body(dq_ref, dk_ref, dv_ref, dxbc_ref, dz_ref, ddt_ref, x_ref, dx1_ref, h_ref, w_ref, pw_ref, sc_ref,
             gx_ref, dpw_ref, dsc_ref, dsh_ref, gw_ref):
        accs = (dpw_ref, dsc_ref, dsh_ref)

        @pl.when(pl.program_id(0) == 0)
        def _():
            for r in accs + (gw_ref,):
                r[...] = jnp.zeros_like(r)

        h16 = h_ref[...]
        dh = None
        off_k, off_v = ATTN_WIDTH, ATTN_WIDTH + KV_WIDTH
        for r, lo, hi in ((dq_ref, 0, off_k), (dk_ref, off_k, off_v), (dv_ref, off_v, OFF_XBC),
                          (dxbc_ref, OFF_XBC, OFF_Z), (dz_ref, OFF_Z, OFF_DT), (ddt_ref, OFF_DT, PROJ_PAD)):
            d16 = _b(r[...])
            part = jnp.dot(d16, w_ref[lo:hi, :], preferred_element_type=F32)
            dh = part if dh is None else dh + part
            gw_ref[:, lo:hi] += _tn(h16, d16)
        dx, dpw, dsc, dsh = _norm_mod_bwd(x_ref[...], pw_ref[...], sc_ref[...], dh)
        gx_ref[...] = dx1_ref[...] + dx
        for r, v in zip(accs, (dpw, dsc, dsh)):
            r[...] += v

    tile = lambda n: pl.BlockSpec((tm, n), _row)
    return pl.pallas_call(
        body, name="inproj_bwd", grid=(s // tm,),
        in_specs=[tile(ATTN_WIDTH), tile(KV_WIDTH), tile(KV_WIDTH), tile(XBC_WIDTH), tile(SSM_WIDTH), tile(DT_PAD),
                  tile(D_MODEL), tile(D_MODEL), tile(D_MODEL), _resident((PROJ_PAD, D_MODEL))] + [_vec(D_MODEL)] * 2,
        out_specs=[tile(D_MODEL)] + [_vec(D_MODEL)] * 3 + [pl.BlockSpec((D_MODEL, PROJ_PAD), _const)],
        out_shape=[jax.ShapeDtypeStruct((s, D_MODEL), F32)] + [jax.ShapeDtypeStruct((1, D_MODEL), F32)] * 3
        + [jax.ShapeDtypeStruct((D_MODEL, PROJ_PAD), F32)],
        compiler_params=_params())(dq, dk, dv, dxbc, dz, ddt, x, dx1, h1, w_in_t16, pre_mix_w, scale1)


def _adam(g, w, m, v):
    new_m = ADAM_B1 * m + (1.0 - ADAM_B1) * g
    new_v = ADAM_B2 * v + (1.0 - ADAM_B2) * jnp.square(g)
    m_hat = new_m / (1.0 - ADAM_B1 ** ADAM_STEP)
    v_hat = new_v / (1.0 - ADAM_B2 ** ADAM_STEP)
    return -ADAM_LR * (m_hat / (jnp.sqrt(v_hat) + ADAM_EPS) + ADAM_WD * w), new_m, new_v


ROW_PARAMS = (("b_ada", 6144, 6144), ("pre_mix_w", 1024, 1024), ("attn_sinks", 128, 8), ("ssm_conv_b", 1024, 1024),
              ("ssm_dt_bias", 128, 8), ("ssm_a_log", 128, 8), ("ssm_d", 128, 8), ("ssm_norm_w", 512, 512),
              ("post_mix_w", 1024, 1024), ("pre_ffn_w", 1024, 1024), ("ffn_conv_b", 5632, 5632),
              ("post_ffn_w", 1024, 1024))
LOSS_LANES = 128


def adamw_small(row_all, rb_all, rel_bias_wmv, row_wmv):
    n_rows = len(ROW_PARAMS)

    def body(*refs):
        row_ref, rb_ref = refs[:2]
        wmv = refs[2:5 + 3 * n_rows]
        outs = refs[5 + 3 * n_rows:]
        g_row, g_rb = row_ref[0], rb_ref[0]
        for k in range(1, N_DEV):
            g_row = g_row + row_ref[k]
            g_rb = g_rb + rb_ref[k]
        outs[0][...] = g_row[:, :LOSS_LANES]
        grads = [g_rb[:, :N_Q_HEADS]]
        off = LOSS_LANES
        for _, lanes, width in ROW_PARAMS:
            grads.append(g_row[:, off:off + width])
            off += lanes
        for i, g in enumerate(grads):
            w_ref, m_ref, v_ref = wmv[3 * i:3 * i + 3]
            g_out, d_out, m_out, v_out = outs[1 + 4 * i:5 + 4 * i]
            g_out[...] = g
            d_out[...], m_out[...], v_out[...] = _adam(g, w_ref[...], m_ref[...], v_ref[...])

    flat_wmv = list(rel_bias_wmv) + [a for wmv in row_wmv for a in wmv]
    shapes = [jax.ShapeDtypeStruct((1, LOSS_LANES), F32)] + [jax.ShapeDtypeStruct((N_BUCKETS, N_Q_HEADS), F32)] * 4
    for _, _, width in ROW_PARAMS:
        shapes += [jax.ShapeDtypeStruct((1, width), F32)] * 4
    return pl.pallas_call(body, name="adamw_small", out_shape=shapes,
                          compiler_params=_params(n_axes=0))(row_all, rb_all, *flat_wmv)


def adamw(parts, w, m, v, name):
    p, r, n = parts.shape
    tr = _row_tile(r)

    def body(p_ref, w_ref, m_ref, v_ref, g_ref, d_ref, nm_ref, nv_ref):
        g = p_ref[0].astype(F32)
        for k in range(1, p):
            g = g + p_ref[k].astype(F32)
        g_ref[...] = g
        d_ref[...], nm_ref[...], nv_ref[...] = _adam(g, w_ref[...], m_ref[...], v_ref[...])

    tile = pl.BlockSpec((tr, n), _row)
    return pl.pallas_call(
        body, name=name, grid=(r // tr,),
        in_specs=[pl.BlockSpec((p, tr, n), lambda i: (0, i, 0)), tile, tile, tile],
        out_specs=[tile] * 4, out_shape=[jax.ShapeDtypeStruct((r, n), F32)] * 4,
        compiler_params=_params())(parts, w, m, v)


def _bucket_table():
    lq = ATTN_BLOCK
    qi = np.arange(lq)[:, None] + lq
    kj = np.arange(2 * lq)[None, :]
    dist = qi - kj
    d = np.maximum(dist, 0)
    max_exact = N_BUCKETS // 2
    nf = np.maximum(d, 1).astype(np.float32)
    large = max_exact + (np.log(nf / max_exact) / math.log(REL_MAX_DIST / max_exact)
                         * (N_BUCKETS - max_exact)).astype(np.int32)
    large = np.minimum(large, N_BUCKETS - 1)
    bucket = np.where(d < max_exact, d, large).astype(np.int32)
    in_band = (dist >= 0) & (dist < REL_MAX_DIST)
    return np.where(in_band, bucket, -1).astype(np.int32)


def _cols_from_blocks(g):
    return jnp.transpose(g, (1, 0, 2)).reshape(g.shape[1], N_DEV * g.shape[2])


def _cols_to_blocks(a):
    r, n = a.shape
    return jnp.transpose(a.reshape(r, N_DEV, n // N_DEV), (1, 0, 2))


def _perm_in_rows(wt):
    pad = jnp.zeros((DT_PAD - SSM_HEADS, wt.shape[1]), wt.dtype)
    return jnp.concatenate([wt[:768], wt[768:1280], wt[1792:2304], wt[1280:1792], wt[2304:2312], pad], axis=0)


def _unperm_in(g):
    return jnp.concatenate([g[:, :768], g[:, 768:1280], g[:, 1792:2304], g[:, 1280:1792], g[:, 2304:2312]], axis=1)


def _lane_pad(v, n=128):
    return jnp.pad(v, ((0, 0), (0, n - v.shape[1])))


def kernel(x, c, rel_bias, w_ada, b_ada, pre_mix_w, w_in, attn_sinks, ssm_conv_w, ssm_conv_b, ssm_dt_bias, ssm_a_log, ssm_d, ssm_norm_w, w_out, post_mix_w, pre_ffn_w, w_up, ffn_conv_w, ffn_conv_b, w_down, post_ffn_w, loss_target, m_rel_bias, m_w_ada, m_b_ada, m_pre_mix_w, m_w_in, m_attn_sinks, m_ssm_conv_w, m_ssm_conv_b, m_ssm_dt_bias, m_ssm_a_log, m_ssm_d, m_ssm_norm_w, m_w_out, m_post_mix_w, m_pre_ffn_w, m_w_up, m_ffn_conv_w, m_ffn_conv_b, m_w_down, m_post_ffn_w, v_rel_bias, v_w_ada, v_b_ada, v_pre_mix_w, v_w_in, v_attn_sinks, v_ssm_conv_w, v_ssm_conv_b, v_ssm_dt_bias, v_ssm_a_log, v_ssm_d, v_ssm_norm_w, v_w_out, v_post_mix_w, v_pre_ffn_w, v_w_up, v_ffn_conv_w, v_ffn_conv_b, v_w_down, v_post_ffn_w):
    weights = dict(rel_bias=rel_bias, w_ada=w_ada, b_ada=b_ada, pre_mix_w=pre_mix_w, w_in=w_in, attn_sinks=attn_sinks, ssm_conv_w=ssm_conv_w, ssm_conv_b=ssm_conv_b, ssm_dt_bias=ssm_dt_bias, ssm_a_log=ssm_a_log, ssm_d=ssm_d, ssm_norm_w=ssm_norm_w, w_out=w_out, post_mix_w=post_mix_w, pre_ffn_w=pre_ffn_w, w_up=w_up, ffn_conv_w=ffn_conv_w, ffn_conv_b=ffn_conv_b, w_down=w_down, post_ffn_w=post_ffn_w)
    mom_m = dict(rel_bias=m_rel_bias, w_ada=m_w_ada, b_ada=m_b_ada, pre_mix_w=m_pre_mix_w, w_in=m_w_in, attn_sinks=m_attn_sinks, ssm_conv_w=m_ssm_conv_w, ssm_conv_b=m_ssm_conv_b, ssm_dt_bias=m_ssm_dt_bias, ssm_a_log=m_ssm_a_log, ssm_d=m_ssm_d, ssm_norm_w=m_ssm_norm_w, w_out=m_w_out, post_mix_w=m_post_mix_w, pre_ffn_w=m_pre_ffn_w, w_up=m_w_up, ffn_conv_w=m_ffn_conv_w, ffn_conv_b=m_ffn_conv_b, w_down=m_w_down, post_ffn_w=m_post_ffn_w)
    mom_v = dict(rel_bias=v_rel_bias, w_ada=v_w_ada, b_ada=v_b_ada, pre_mix_w=v_pre_mix_w, w_in=v_w_in, attn_sinks=v_attn_sinks, ssm_conv_w=v_ssm_conv_w, ssm_conv_b=v_ssm_conv_b, ssm_dt_bias=v_ssm_dt_bias, ssm_a_log=v_ssm_a_log, ssm_d=v_ssm_d, ssm_norm_w=v_ssm_norm_w, w_out=v_w_out, post_mix_w=v_post_mix_w, pre_ffn_w=v_pre_ffn_w, w_up=v_w_up, ffn_conv_w=v_ffn_conv_w, ffn_conv_b=v_ffn_conv_b, w_down=v_w_down, post_ffn_w=v_post_ffn_w)
    order = ['rel_bias', 'w_ada', 'b_ada', 'pre_mix_w', 'w_in', 'attn_sinks', 'ssm_conv_w', 'ssm_conv_b', 'ssm_dt_bias', 'ssm_a_log', 'ssm_d', 'ssm_norm_w', 'w_out', 'post_mix_w', 'pre_ffn_w', 'w_up', 'ffn_conv_w', 'ffn_conv_b', 'w_down', 'post_ffn_w']

    me = 4 * lax.axis_index("x") + 2 * lax.axis_index("y") + lax.axis_index("c")
    xs_ = x[0]
    target = loss_target[0]

    (w_in_g, scw_g, fcw_g, c_g) = all_gather([_b(w_in[0]).T, ssm_conv_w[0], ffn_conv_w[0], c], "gather_weights")
    w_in_t16 = _perm_in_rows(w_in_g.reshape(IN_PROJ_WIDTH, D_MODEL))
    w_in16 = w_in_t16.T
    ssm_cw = _cols_from_blocks(scw_g)
    ffn_cw = _cols_from_blocks(fcw_g)
    c_all = c_g.reshape(N_DEV, D_MODEL)

    n_cols = w_ada.shape[2]
    b_cols = lax.dynamic_slice(b_ada, (0, me * n_cols), (1, n_cols))
    mod_part = ada_fwd(c_all, w_ada[0], b_cols)
    (mod_rows,) = all_to_all([mod_part.reshape(N_DEV, 1, n_cols)], "scatter_mod")
    mod = mod_rows.reshape(N_MOD, 1, D_MODEL)
    shift1, scale1, gate1, shift2, scale2, gate2 = (mod[i] for i in range(N_MOD))

    bucket_band = jnp.asarray(_bucket_table())
    bias = rel_bias_table(rel_bias, bucket_band)
    sinks_row = _lane_pad(attn_sinks)
    dt_bias, a_log, dskip = _lane_pad(ssm_dt_bias), _lane_pad(ssm_a_log), _lane_pad(ssm_d)

    h1, qkv, xbc_raw, z, dt_raw, w_out_g = pre_mix_inproj(
        xs_, pre_mix_w, scale1, shift1, w_in16, [(_b(w_out[0]), False)])
    attn, w_up_g = attn_fwd(qkv, bias, sinks_row, [(_b(w_up[0]).T, False)])
    ssm, hprev_all, w_down_g = ssm_fwd(xbc_raw, z, dt_raw, ssm_cw, ssm_conv_b, dt_bias, a_log, dskip, ssm_norm_w,
                                       [(_b(w_down[0]), False)])
    w_out16 = w_out_g.reshape(D_MODEL, D_MODEL)
    w_out_t16 = w_out16.T
    w_up_t16 = w_up_g.reshape(2 * D_FF, D_MODEL)
    w_up16 = w_up_t16.T
    w_down16 = w_down_g.reshape(D_FF, D_MODEL)
    w_down_t16 = w_down16.T
    mixed, x1, h2 = mix_out(attn, ssm, xs_, w_out16, gate1, post_mix_w, pre_ffn_w, scale2, shift2)
    u, u_raw16, f16 = up_gate(h2, w_up16, ffn_cw, ffn_conv_b)
    d_ffn, dy, loss_part, d_gate2, d_post_ffn_w, g_w_down = down_loss(f16, w_down16, x1, target, gate2, post_ffn_w)

    du_raw, dh2, d_ffn_cw, d_ffn_cb = ffn_bwd(u, u_raw16, d_ffn, ffn_cw, w_down_t16, w_up_t16)
    g_w_up = matmul_tn(h2, du_raw, "grad_w_up", D_MODEL, FF_CHUNK, tk=2048)
    (dx1, d_attn, d_ssm, d_pre_ffn_w, d_scale2, d_shift2, d_gate1, d_post_mix_w, g_w_out) = mix_bwd(
        dh2, x1, dy, mixed, attn, ssm, w_out_t16, pre_ffn_w, scale2, gate1, post_mix_w)
    dq, dk, dv, dbias, dsinks, p_w_down = attn_bwd(
        qkv, bias, sinks_row, d_attn, [(g_w_down.reshape(N_DEV, D_FF // N_DEV, D_MODEL), True)])
    d_rel_bias = rel_bias_grad(dbias, bucket_band)
    (dxbc, dz, ddt, d_ssm_cw, d_ssm_cb, d_dt_bias, d_a_log, d_dskip, d_norm_w, p_w_up, p_w_out) = ssm_bwd(
        xbc_raw, z, dt_raw, hprev_all, d_ssm, ssm_cw, ssm_conv_b, dt_bias, a_log, dskip, ssm_norm_w,
        [(_cols_to_blocks(g_w_up), True), (g_w_out.reshape(N_DEV, D_MODEL // N_DEV, D_MODEL), True)])
    grad_x, d_pre_mix_w, d_scale1, d_shift1, g_w_in_perm = inproj_bwd(
        dq, dk, dv, dxbc, dz, ddt, xs_, dx1, h1, w_in_t16, pre_mix_w, scale1)
    g_w_in = _unperm_in(g_w_in_perm)

    d_mod = jnp.concatenate([d_shift1, d_scale1, d_gate1, d_shift2, d_scale2, d_gate2], axis=1)
    late = ("w_in", "ssm_conv_w", "ffn_conv_w")
    full = [_cols_to_blocks(g_w_in), _cols_to_blocks(d_ssm_cw[:SSM_CONV]), _cols_to_blocks(d_ffn_cw[:FFN_CONV])]
    core = lax.axis_index("c").astype(jnp.int32).reshape(1)
    got = pair_exchange(full, "pair_grads")
    chip_sums = [pair_sum(f_, g_, core, "pair_sum_" + k) for k, f_, g_ in zip(late, full, got)]
    chip_parts = all_to_all(chip_sums, "scatter_grads", CHIP_FLIPS, _chip_index)
    (d_mod_rows,) = all_to_all([d_mod.reshape(N_DEV, 1, n_cols)], "scatter_dmod")
    g_w_ada = ada_bwd(c_all, d_mod_rows.reshape(N_DEV, n_cols))

    row_g = dict(b_ada=d_mod, pre_mix_w=d_pre_mix_w, attn_sinks=dsinks, ssm_conv_b=d_ssm_cb, ssm_dt_bias=d_dt_bias,
                 ssm_a_log=d_a_log, ssm_d=d_dskip, ssm_norm_w=d_norm_w, post_mix_w=d_post_mix_w,
                 pre_ffn_w=d_pre_ffn_w, ffn_conv_b=d_ffn_cb, post_ffn_w=d_post_ffn_w)
    row = jnp.concatenate([loss_part] + [row_g[k] for k, _, _ in ROW_PARAMS], axis=1)
    row_all, rb_all = all_gather([row, d_rel_bias], "gather_small")

    wmv = lambda k: (weights[k], mom_m[k], mom_v[k])
    small = adamw_small(row_all, rb_all, wmv("rel_bias"), [wmv(k) for k, _, _ in ROW_PARAMS])
    loss = small[0][0, 0]
    res = {k: tuple(small[1 + 4 * i:5 + 4 * i]) for i, k in enumerate(["rel_bias"] + [k for k, _, _ in ROW_PARAMS])}
    big = list(zip(late, chip_parts)) + [("w_down", p_w_down), ("w_up", p_w_up), ("w_out", p_w_out),
                                        ("w_ada", g_w_ada[None])]
    for k, parts in big:
        res[k] = tuple(o[None] for o in adamw(parts, weights[k][0], mom_m[k][0], mom_v[k][0], "adamw_" + k))

    outs = [loss, grad_x[None]]
    for field in range(4):
        outs += [res[k][field] for k in order]
    return tuple(outs)
```

```python
import math

import numpy as np
import jax
import jax.numpy as jnp
from jax import lax
from jax.experimental import pallas as pl
from jax.experimental.pallas import tpu as pltpu

F32 = jnp.float32
BF16 = jnp.bfloat16
MESH_ID = pl.DeviceIdType.MESH

N_DEV = 8
D_MODEL = 1024
N_Q_HEADS = 8
N_KV_HEADS = 2
HEAD_DIM = 64
ATTN_WIDTH = 512
KV_WIDTH = 128
ATTN_BLOCK = 128
N_BUCKETS = 32
REL_MAX_DIST = 128
SSM_HEADS = 8
SSM_HEAD_DIM = 64
SSM_WIDTH = 512
SSM_STATE = 128
SSM_GROUPS = 2
SSM_BC = 256
SSM_CONV = 4
SSM_CHUNK = 256
XBC_WIDTH = SSM_WIDTH + 2 * SSM_BC
D_FF = 2816
FFN_CONV = 3
NORM_EPS = 1e-6
N_MOD = 6
IN_PROJ_WIDTH = 2312
QKV_W = ATTN_WIDTH + 2 * KV_WIDTH
OFF_XBC = QKV_W
OFF_Z = OFF_XBC + XBC_WIDTH
OFF_DT = OFF_Z + SSM_WIDTH
DT_PAD = 128
PROJ_PAD = OFF_DT + DT_PAD
FF_CHUNK = 1408

ADAM_LR = 0.001
ADAM_B1 = 0.9
ADAM_B2 = 0.999
ADAM_EPS = 1e-08
ADAM_WD = 0.01
ADAM_STEP = 10

TOKEN_TILE = 256
HALO = 8
NEXT = 16
VMEM_LIMIT = 56 * 1024 * 1024


def _params(vmem=VMEM_LIMIT, n_axes=1):
    return pltpu.CompilerParams(dimension_semantics=("arbitrary",) * n_axes, vmem_limit_bytes=vmem)


def _b(x):
    return x.astype(BF16)


def _nn(a, b):
    return jnp.dot(_b(a), _b(b), preferred_element_type=F32)


def _nt(a, b):
    return lax.dot_general(_b(a), _b(b), (((1,), (1,)), ((), ())), preferred_element_type=F32)


def _tn(a, b):
    return lax.dot_general(_b(a), _b(b), (((0,), (0,)), ((), ())), preferred_element_type=F32)


@jax.custom_vjp
def mm(a, b):
    return _nn(a, b)


mm.defvjp(lambda a, b: (_nn(a, b), (a, b)),
          lambda r, g: (_nt(g, r[1]).astype(r[0].dtype), _tn(r[0], g).astype(r[1].dtype)))


@jax.custom_vjp
def mm_nt(a, b):
    return _nt(a, b)


mm_nt.defvjp(lambda a, b: (_nt(a, b), (a, b)),
             lambda r, g: (_nn(g, r[1]).astype(r[0].dtype), _tn(g, r[0]).astype(r[1].dtype)))


@jax.custom_vjp
def mm_tn(a, b):
    return _tn(a, b)


mm_tn.defvjp(lambda a, b: (_tn(a, b), (a, b)),
             lambda r, g: (_nt(r[1], g).astype(r[0].dtype), _nn(r[0], g).astype(r[1].dtype)))


def _rms(x, w):
    return x * lax.rsqrt(jnp.mean(x * x, axis=-1, keepdims=True) + NORM_EPS) * w


def _norm_mod(x, w, scale, shift):
    return _rms(x, w) * (1.0 + scale) + shift


def _rms_bwd(x, w, dy):
    r = lax.rsqrt(jnp.mean(x * x, axis=-1, keepdims=True) + NORM_EPS)
    xhat = x * r
    g = dy * w
    dx = r * (g - xhat * jnp.mean(g * xhat, axis=-1, keepdims=True))
    return dx, jnp.sum(dy * xhat, axis=0, keepdims=True)


def _norm_mod_bwd(x, w, scale, dh):
    dx, da = _rms_bwd(x, w * (1.0 + scale), dh)
    return dx, da * (1.0 + scale), da * w, jnp.sum(dh, axis=0, keepdims=True)


def _gated_rms_bwd(m, gate, w, dy):
    dm, t = _rms_bwd(m, w * gate, dy)
    return dm, t * w, t * gate


def _silu(x):
    return x * jax.nn.sigmoid(x)


def _conv_rows(xin, w, k):
    acc = xin * w[k - 1:k, :]
    for j in range(1, k):
        acc = acc + pltpu.roll(xin, j, axis=0) * w[k - 1 - j:k - j, :]
    return acc


def _conv_rows_t(du, w, k):
    n = du.shape[0]
    acc = du * w[k - 1:k, :]
    for j in range(1, k):
        acc = acc + pltpu.roll(du, n - j, axis=0) * w[k - 1 - j:k - j, :]
    return acc


def _row(i):
    return (i, 0)


def _const(i):
    return (0, 0)


def _vec(n):
    return pl.BlockSpec((1, n), _const)


def _block_index(p):
    return 4 * p[0] + 2 * p[1] + p[2]


def all_gather(arrs, name):
    n = len(arrs)

    def body(*refs):
        ins, outs = refs[:n], refs[n:2 * n]
        send_sems, recv_sems, local_sems = refs[2 * n:]
        x, y, c = lax.axis_index("x"), lax.axis_index("y"), lax.axis_index("c")
        me, sibling = (x, y, c), (x, y, 1 - c)
        chips = [(1 - x, y), (x, 1 - y), (1 - x, 1 - y)]

        def copy(a, k, block, to, src=None):
            dst = outs[a].at[_block_index(block)]
            return pltpu.make_async_remote_copy(
                src_ref=dst if src is None else src, dst_ref=dst,
                send_sem=send_sems.at[a * 7 + k], recv_sem=recv_sems.at[a * 7 + k],
                device_id=to, device_id_type=MESH_ID)

        mine = [pltpu.make_async_copy(ins[a], outs[a].at[_block_index(me)], local_sems.at[a]) for a in range(n)]
        for cp in mine:
            cp.start()
        first = []
        for a in range(n):
            first.append(copy(a, 0, me, sibling, src=ins[a]))
            first += [copy(a, 1 + j, me, (*chip, c), src=ins[a]) for j, chip in enumerate(chips)]
        for cp in first:
            cp.start()
        passed = []
        for j, chip in enumerate(chips):
            for a in range(n):
                copy(a, 1 + j, (*chip, c), me).wait_recv()
                cp = copy(a, 4 + j, (*chip, c), sibling)
                cp.start()
                passed.append(cp)
        for a in range(n):
            copy(a, 0, sibling, me).wait_recv()
            for j, chip in enumerate(chips):
                copy(a, 4 + j, (*chip, 1 - c), me).wait_recv()
        for cp in first + passed:
            cp.wait_send()
        for cp in mine:
            cp.wait()

    any_spec = pl.BlockSpec(memory_space=pl.ANY)
    return pl.pallas_call(
        body, name=name,
        out_shape=[jax.ShapeDtypeStruct((N_DEV,) + a.shape, a.dtype) for a in arrs],
        in_specs=[any_spec] * n, out_specs=[any_spec] * n,
        scratch_shapes=[pltpu.SemaphoreType.DMA((7 * n,)), pltpu.SemaphoreType.DMA((7 * n,)),
                        pltpu.SemaphoreType.DMA((n,))],
    )(*arrs)


ALL_FLIPS = ((0, 0, 1), (0, 1, 0), (0, 1, 1), (1, 0, 0), (1, 0, 1), (1, 1, 0), (1, 1, 1))
CHIP_FLIPS = ((0, 1, 0), (1, 0, 0), (1, 1, 0))


def _chip_index(p):
    return 2 * p[0] + p[1]


def all_to_all(arrs, name, flips=ALL_FLIPS, index=_block_index):
    n = len(arrs)
    nf = len(flips)

    def body(*refs):
        ins, outs = refs[:n], refs[n:2 * n]
        send_sems, recv_sems, local_sems = refs[2 * n:]
        pos = (lax.axis_index("x"), lax.axis_index("y"), lax.axis_index("c"))
        me = index(pos)
        peers = [tuple(1 - p if f else p for p, f in zip(pos, flip)) for flip in flips]

        def copy(a, k):
            peer = peers[k]
            return pltpu.make_async_remote_copy(
                src_ref=ins[a].at[index(peer)], dst_ref=outs[a].at[me],
                send_sem=send_sems.at[a * nf + k], recv_sem=recv_sems.at[a * nf + k],
                device_id=peer, device_id_type=MESH_ID)

        def landed(a, k):
            slot = outs[a].at[index(peers[k])]
            return pltpu.make_async_remote_copy(
                src_ref=slot, dst_ref=slot,
                send_sem=send_sems.at[a * nf + k], recv_sem=recv_sems.at[a * nf + k],
                device_id=peers[k], device_id_type=MESH_ID)

        mine = [pltpu.make_async_copy(ins[a].at[me], outs[a].at[me], local_sems.at[a]) for a in range(n)]
        for cp in mine:
            cp.start()
        sent = [copy(a, k) for a in range(n) for k in range(nf)]
        for cp in sent:
            cp.start()
        for a in range(n):
            for k in range(nf):
                landed(a, k).wait_recv()
        for cp in sent:
            cp.wait_send()
        for cp in mine:
            cp.wait()

    any_spec = pl.BlockSpec(memory_space=pl.ANY)
    return pl.pallas_call(
        body, name=name,
        out_shape=[jax.ShapeDtypeStruct(a.shape, a.dtype) for a in arrs],
        in_specs=[any_spec] * n, out_specs=[any_spec] * n,
        scratch_shapes=[pltpu.SemaphoreType.DMA((nf * n,)), pltpu.SemaphoreType.DMA((nf * n,)),
                        pltpu.SemaphoreType.DMA((n,))],
    )(*arrs)


def _direct_exchange(src, dst, sems, scatter):
    send_sems, recv_sems, local_sem = sems
    pos = (lax.axis_index("x"), lax.axis_index("y"), lax.axis_index("c"))
    me = _block_index(pos)
    peers = [tuple(1 - p if f else p for p, f in zip(pos, flip)) for flip in ALL_FLIPS]

    def outgoing(k):
        return pltpu.make_async_remote_copy(
            src_ref=src.at[_block_index(peers[k])] if scatter else src, dst_ref=dst.at[me],
            send_sem=send_sems.at[k], recv_sem=recv_sems.at[k], device_id=peers[k], device_id_type=MESH_ID)

    def incoming(k):
        slot = dst.at[_block_index(peers[k])]
        return pltpu.make_async_remote_copy(
            src_ref=slot, dst_ref=slot, send_sem=send_sems.at[k], recv_sem=recv_sems.at[k],
            device_id=peers[k], device_id_type=MESH_ID)

    def local():
        return pltpu.make_async_copy(src.at[me] if scatter else src, dst.at[me], local_sem)

    def start():
        local().start()
        for k in range(len(ALL_FLIPS)):
            outgoing(k).start()

    def finish():
        for k in range(len(ALL_FLIPS)):
            incoming(k).wait_recv()
        for k in range(len(ALL_FLIPS)):
            outgoing(k).wait_send()
        local().wait()

    return start, finish


def hosted_call(body, exchanges, steps, n_in, n_out, **call):
    n_ex = len(exchanges)

    def wrapped(*refs):
        ins, srcs = refs[:n_in], refs[n_in:n_in + n_ex]
        outs = refs[n_in + n_ex:n_in + n_ex + n_out]
        dsts = refs[n_in + n_ex + n_out:n_in + 2 * n_ex + n_out]
        rest = refs[n_in + 2 * n_ex + n_out:]
        scratch, sems = rest[:len(rest) - 3 * n_ex], rest[len(rest) - 3 * n_ex:]
        plans = [_direct_exchange(srcs[e], dsts[e], sems[3 * e:3 * e + 3], exchanges[e][1]) for e in range(n_ex)]

        @pl.when(pl.program_id(0) == 0)
        def _():
            for start, _ in plans:
                start()

        body(*ins, *outs, *scratch)

        @pl.when(pl.program_id(0) == steps - 1)
        def _():
            for _, finish in plans:
                finish()

    any_spec = pl.BlockSpec(memory_space=pl.ANY)
    landings = [jax.ShapeDtypeStruct(src.shape if scatter else (N_DEV,) + src.shape, src.dtype)
                for src, scatter in exchanges]
    n_flips = len(ALL_FLIPS)
    sems = [pltpu.SemaphoreType.DMA((n_flips,)), pltpu.SemaphoreType.DMA((n_flips,)), pltpu.SemaphoreType.DMA(())]
    return pl.pallas_call(
        wrapped, grid=(steps,),
        in_specs=list(call.pop("in_specs")) + [any_spec] * n_ex,
        out_specs=list(call.pop("out_specs")) + [any_spec] * n_ex,
        out_shape=list(call.pop("out_shape")) + landings,
        scratch_shapes=list(call.pop("scratch_shapes", [])) + sems * n_ex,
        **call)


def _grid_call(body, steps, args, exchanges, **call):
    if not exchanges:
        return pl.pallas_call(body, grid=(steps,), **call)(*args)
    srcs = [src for src, _ in exchanges]
    return hosted_call(body, exchanges, steps, len(args), len(call["out_shape"]), **call)(*args, *srcs)


N_CHIPS = 4


def pair_exchange(arrs, name):
    n = len(arrs)

    def body(*refs):
        ins, outs = refs[:n], refs[n:2 * n]
        send_sems, recv_sems = refs[2 * n:]
        x, y, c = lax.axis_index("x"), lax.axis_index("y"), lax.axis_index("c")
        sibling = (x, y, 1 - c)
        sent = []
        for a in range(n):
            for q in range(N_CHIPS):
                cp = pltpu.make_async_remote_copy(
                    src_ref=ins[a].at[2 * q + (1 - c)], dst_ref=outs[a].at[q],
                    send_sem=send_sems.at[a * N_CHIPS + q], recv_sem=recv_sems.at[a * N_CHIPS + q],
                    device_id=sibling, device_id_type=MESH_ID)
                cp.start()
                sent.append(cp)
        for cp in sent:
            cp.wait_recv()
        for cp in sent:
            cp.wait_send()

    any_spec = pl.BlockSpec(memory_space=pl.ANY)
    return pl.pallas_call(
        body, name=name,
        out_shape=[jax.ShapeDtypeStruct((N_CHIPS,) + a.shape[1:], a.dtype) for a in arrs],
        in_specs=[any_spec] * n, out_specs=[any_spec] * n,
        scratch_shapes=[pltpu.SemaphoreType.DMA((N_CHIPS * n,)), pltpu.SemaphoreType.DMA((N_CHIPS * n,))],
    )(*arrs)


def pair_sum(full, got, core, name):
    _, r, n = full.shape
    tr = _row_tile(r)

    def body(c_ref, mine_ref, got_ref, o_ref):
        o_ref[...] = _b(mine_ref[...] + got_ref[...])

    grid_spec = pltpu.PrefetchScalarGridSpec(
        num_scalar_prefetch=1, grid=(N_CHIPS, r // tr),
        in_specs=[pl.BlockSpec((1, tr, n), lambda q, i, c_ref: (2 * q + c_ref[0], i, 0)),
                  pl.BlockSpec((1, tr, n), lambda q, i, c_ref: (q, i, 0))],
        out_specs=pl.BlockSpec((1, tr, n), lambda q, i, c_ref: (q, i, 0)))
    return pl.pallas_call(body, name=name, grid_spec=grid_spec,
                          out_shape=jax.ShapeDtypeStruct((N_CHIPS, r, n), BF16),
                          compiler_params=_params(n_axes=2))(core, full, got)


def _row_tile(r):
    for cand in (256, 128, 64, 32, 16):
        if r % cand == 0 and r > cand:
            return cand
    return r


def ada_fwd(c_all, w_ada, b_cols):
    def body(c_ref, w_ref, b_ref, o_ref):
        o_ref[...] = _nn(_silu(c_ref[...]), w_ref[...]) + b_ref[...]

    return pl.pallas_call(body, name="ada_fwd",
                          out_shape=jax.ShapeDtypeStruct((N_DEV, w_ada.shape[1]), F32),
                          compiler_params=_params(n_axes=0))(c_all, w_ada, b_cols)


def ada_bwd(c_all, g_cols):
    def body(c_ref, g_ref, o_ref):
        o_ref[...] = _tn(_silu(c_ref[...]), g_ref[...])

    return pl.pallas_call(body, name="ada_bwd",
                          out_shape=jax.ShapeDtypeStruct((c_all.shape[1], g_cols.shape[1]), F32),
                          compiler_params=_params(n_axes=0))(c_all, g_cols)


def matmul_tn(a, b, name, bm, bn, tk=512):
    s, m = a.shape
    n = b.shape[1]
    tk = min(tk, s)

    def body(a_ref, b_ref, o_ref):
        @pl.when(pl.program_id(2) == 0)
        def _():
            o_ref[...] = jnp.zeros_like(o_ref)

        o_ref[...] += _tn(a_ref[...], b_ref[...])

    return pl.pallas_call(
        body, name=name, grid=(m // bm, n // bn, s // tk),
        in_specs=[pl.BlockSpec((tk, bm), lambda i, j, k: (k, i)), pl.BlockSpec((tk, bn), lambda i, j, k: (k, j))],
        out_specs=pl.BlockSpec((bm, bn), lambda i, j, k: (i, j)),
        out_shape=jax.ShapeDtypeStruct((m, n), F32),
        compiler_params=_params(n_axes=3))(a, b)


def pre_mix_inproj(x, w, scale, shift, w_in16, exchange=None, tm=2 * TOKEN_TILE):
    s = x.shape[0]

    def body(x_ref, w_ref, sc_ref, sh_ref, win_ref, h_ref, qkv_ref, xbc_ref, z_ref, dt_ref):
        h16 = _b(_norm_mod(x_ref[...], w_ref[...], sc_ref[...], sh_ref[...]))
        h_ref[...] = h16
        dot = lambda lo, hi: jnp.dot(h16, win_ref[:, lo:hi], preferred_element_type=F32)
        qkv_ref[...] = _b(dot(0, OFF_XBC))
        xbc_ref[...] = dot(OFF_XBC, OFF_Z)
        z_ref[...] = dot(OFF_Z, OFF_DT)
        dt_ref[...] = dot(OFF_DT, PROJ_PAD)

    tile = lambda n: pl.BlockSpec((tm, n), _row)
    return _grid_call(
        body, s // tm, (x, w, scale, shift, w_in16), exchange, name="pre_mix_inproj",
        in_specs=[tile(D_MODEL), _vec(D_MODEL), _vec(D_MODEL), _vec(D_MODEL), pl.BlockSpec((D_MODEL, PROJ_PAD), _const)],
        out_specs=[tile(D_MODEL), tile(QKV_W), tile(XBC_WIDTH), tile(SSM_WIDTH), tile(DT_PAD)],
        out_shape=[jax.ShapeDtypeStruct((s, D_MODEL), BF16), jax.ShapeDtypeStruct((s, QKV_W), BF16),
                   jax.ShapeDtypeStruct((s, XBC_WIDTH), F32), jax.ShapeDtypeStruct((s, SSM_WIDTH), F32),
                   jax.ShapeDtypeStruct((s, DT_PAD), F32)],
        compiler_params=_params())


ATTN_QB_FWD, ATTN_QB_BWD = 4, 2


def _attn_tile(q, kp, kc, vp, vc, bias, sinks):
    lq = ATTN_BLOCK
    group = N_Q_HEADS // N_KV_HEADS
    lanes = lax.broadcasted_iota(jnp.int32, (1, 128), 1)
    rid = lax.broadcasted_iota(jnp.int32, (group * lq, 1), 0)
    sink_cols = []
    for hk in range(N_KV_HEADS):
        sink = jnp.zeros((group * lq, 1), F32)
        for g in range(group):
            s_h = jnp.sum(jnp.where(lanes == hk * group + g, sinks, 0.0), axis=-1, keepdims=True)
            sink = jnp.where((rid >= g * lq) & (rid < (g + 1) * lq), s_h, sink)
        sink_cols.append(sink)
    kall = jnp.concatenate([kp, kc], axis=0)
    vall = jnp.concatenate([vp, vc], axis=0)
    blocks = []
    for b in range(q.shape[0] // lq):
        qb = q[b * lq:(b + 1) * lq]
        outs = []
        for hk in range(N_KV_HEADS):
            cols = slice(hk * HEAD_DIM, (hk + 1) * HEAD_DIM)
            kb = kall[b * lq:(b + 2) * lq, cols]
            vb = vall[b * lq:(b + 2) * lq, cols]
            qg = jnp.concatenate([qb[:, (hk * group + g) * HEAD_DIM:(hk * group + g + 1) * HEAD_DIM]
                                  for g in range(group)], axis=0)
            sc = mm_nt(qg, kb) * (HEAD_DIM ** -0.5) + bias[b][hk]
            sink = sink_cols[hk]
            m = lax.stop_gradient(jnp.maximum(jnp.max(sc, axis=-1, keepdims=True), sink))
            p = jnp.exp(sc - m)
            probs = p / (jnp.sum(p, axis=-1, keepdims=True) + jnp.exp(sink - m))
            og = mm(probs, vb)
            outs += [og[g * lq:(g + 1) * lq] for g in range(group)]
        blocks.append(jnp.concatenate(outs, axis=1))
    return jnp.concatenate(blocks, axis=0)


def _attn_tile_bwd(q, kp, kc, vp, vc, bias, sinks, do):
    lq = ATTN_BLOCK
    group = N_Q_HEADS // N_KV_HEADS
    scale = HEAD_DIM ** -0.5
    lanes = lax.broadcasted_iota(jnp.int32, (1, 128), 1)
    rid = lax.broadcasted_iota(jnp.int32, (group * lq, 1), 0)
    sink_cols = []
    for hk in range(N_KV_HEADS):
        sink = jnp.zeros((group * lq, 1), F32)
        for g in range(group):
            s_h = jnp.sum(jnp.where(lanes == hk * group + g, sinks, 0.0), axis=-1, keepdims=True)
            sink = jnp.where((rid >= g * lq) & (rid < (g + 1) * lq), s_h, sink)
        sink_cols.append(sink)
    kall = jnp.concatenate([kp, kc], axis=0)
    vall = jnp.concatenate([vp, vc], axis=0)
    dsk = jnp.zeros((1, 128), F32)
    dq_blocks, dbias = [], []
    nqb = q.shape[0] // lq
    dk_parts = [[None] * nqb for _ in range(N_KV_HEADS)]
    dv_parts = [[None] * nqb for _ in range(N_KV_HEADS)]
    for b in range(nqb):
        qb, dob = q[b * lq:(b + 1) * lq], do[b * lq:(b + 1) * lq]
        dq_heads, dbias_b = [], []
        for hk in range(N_KV_HEADS):
            cols = slice(hk * HEAD_DIM, (hk + 1) * HEAD_DIM)
            kb = kall[b * lq:(b + 2) * lq, cols]
            vb = vall[b * lq:(b + 2) * lq, cols]
            heads = [hk * group + g for g in range(group)]
            qg = jnp.concatenate([qb[:, h * HEAD_DIM:(h + 1) * HEAD_DIM] for h in heads], axis=0)
            dog = jnp.concatenate([dob[:, h * HEAD_DIM:(h + 1) * HEAD_DIM] for h in heads], axis=0)
            sink = sink_cols[hk]
            sc = _nt(qg, kb) * scale + bias[b][hk]
            m = jnp.maximum(jnp.max(sc, axis=-1, keepdims=True), sink)
            p = jnp.exp(sc - m)
            es = jnp.exp(sink - m)
            inv = 1.0 / (jnp.sum(p, axis=-1, keepdims=True) + es)
            probs = p * inv
            dprobs = _nt(dog, vb)
            delta = jnp.sum(probs * dprobs, axis=-1, keepdims=True)
            dsc = probs * (dprobs - delta)
            dbias_b.append(dsc)
            dsink = -(es * inv) * delta
            for g, h in enumerate(heads):
                tot = jnp.sum(dsink[g * lq:(g + 1) * lq], axis=0, keepdims=True)
                dsk = dsk + jnp.where(lanes == h, tot, 0.0)
            dqg = _nn(dsc, kb) * scale
            dq_heads += [dqg[g * lq:(g + 1) * lq] for g in range(group)]
            dk_parts[hk][b] = _tn(dsc, qg) * scale
            dv_parts[hk][b] = _tn(probs, dog)
        dq_blocks.append(jnp.concatenate(dq_heads, axis=1))
        dbias.append(dbias_b)

    def overlap_add(parts):
        chunks = []
        for r in range(nqb + 1):
            acc = None
            if r < nqb:
                acc = parts[r][:lq]
            if r >= 1:
                tail = parts[r - 1][lq:]
                acc = tail if acc is None else acc + tail
            chunks.append(acc)
        return jnp.concatenate(chunks, axis=0)

    dkall = jnp.concatenate([overlap_add(dk_parts[hk]) for hk in range(N_KV_HEADS)], axis=1)
    dvall = jnp.concatenate([overlap_add(dv_parts[hk]) for hk in range(N_KV_HEADS)], axis=1)
    return jnp.concatenate(dq_blocks, axis=0), dkall, dvall, dbias, dsk


def _attn_in_specs(nt, clamp, nqb):
    lq, tq = ATTN_BLOCK, ATTN_BLOCK * nqb
    cur = lambda n: jnp.minimum(n, nt - 1) if clamp else n
    prev = lambda n: jnp.maximum(cur(n) * nqb - 1, 0)
    kcol, vcol = ATTN_WIDTH // KV_WIDTH, ATTN_WIDTH // KV_WIDTH + 1
    return [pl.BlockSpec((tq, ATTN_WIDTH), lambda n: (cur(n), 0)),
            pl.BlockSpec((lq, KV_WIDTH), lambda n: (prev(n), kcol)),
            pl.BlockSpec((tq, KV_WIDTH), lambda n: (cur(n), kcol)),
            pl.BlockSpec((lq, KV_WIDTH), lambda n: (prev(n), vcol)),
            pl.BlockSpec((tq, KV_WIDTH), lambda n: (cur(n), vcol)),
            pl.BlockSpec((2, N_KV_HEADS, 4 * lq, 2 * lq), lambda n: (0, 0, 0, 0)),
            _vec(128)]


def _tile_bias(bias_ref, first, nqb):
    return [[jnp.where(first, bias_ref[1, hk], bias_ref[0, hk]) if b == 0 else bias_ref[0, hk]
             for hk in range(N_KV_HEADS)] for b in range(nqb)]


def attn_fwd(qkv, bias, sinks_rows, exchange=None):
    s = qkv.shape[0]
    nqb = min(ATTN_QB_FWD, s // ATTN_BLOCK)
    tq = ATTN_BLOCK * nqb
    nt = s // tq

    def body(q_ref, kp_ref, kc_ref, vp_ref, vc_ref, bias_ref, sk_ref, o_ref):
        f = lambda r: r[...].astype(F32)
        o = _attn_tile(f(q_ref), f(kp_ref), f(kc_ref), f(vp_ref), f(vc_ref),
                       _tile_bias(bias_ref, pl.program_id(0) == 0, nqb), sk_ref[...])
        o_ref[...] = _b(o)

    return _grid_call(
        body, nt, (qkv, qkv, qkv, qkv, qkv, bias, sinks_rows), exchange, name="attn_fwd",
        in_specs=_attn_in_specs(nt, False, nqb),
        out_specs=[pl.BlockSpec((tq, ATTN_WIDTH), _row)],
        out_shape=[jax.ShapeDtypeStruct((s, ATTN_WIDTH), BF16)],
        compiler_params=_params())


def attn_bwd(qkv, bias, sinks_rows, d_attn, exchange=None):
    s = qkv.shape[0]
    nqb = ATTN_QB_BWD
    lq, tq = ATTN_BLOCK, ATTN_BLOCK * nqb
    nt = s // tq

    def body(q_ref, kp_ref, kc_ref, vp_ref, vc_ref, bias_ref, sk_ref, do_ref,
             dq_ref, dk_ref, dv_ref, dbias_ref, dsk_ref, carry_k, carry_v):
        n = pl.program_id(0)

        @pl.when(n == 0)
        def _():
            dbias_ref[...] = jnp.zeros_like(dbias_ref)
            dsk_ref[...] = jnp.zeros_like(dsk_ref)
            carry_k[...] = jnp.zeros_like(carry_k)
            carry_v[...] = jnp.zeros_like(carry_v)

        @pl.when(n < nt)
        def _():
            f = lambda r: r[...].astype(F32)
            dq, dkall, dvall, dbias, dsk = _attn_tile_bwd(
                f(q_ref), f(kp_ref), f(kc_ref), f(vp_ref), f(vc_ref), _tile_bias(bias_ref, n == 0, nqb), sk_ref[...],
                f(do_ref))
            dkp, dkc, dvp, dvc = dkall[:lq], dkall[lq:], dvall[:lq], dvall[lq:]
            dq_ref[...] = _b(dq)
            done = tq - lq
            dk_ref[:done, :] = _b(carry_k[:done, :])
            dv_ref[:done, :] = _b(carry_v[:done, :])
            dk_ref[done:, :] = _b(carry_k[done:, :] + dkp)
            dv_ref[done:, :] = _b(carry_v[done:, :] + dvp)
            carry_k[...] = dkc
            carry_v[...] = dvc
            dsk_ref[...] += dsk
            first = (n == 0).astype(F32)
            for hk in range(N_KV_HEADS):
                total = dbias[0][hk]
                for b in range(1, nqb):
                    total = total + dbias[b][hk]
                dbias_ref[0, hk] += total - first * dbias[0][hk]
                dbias_ref[1, hk] += first * dbias[0][hk]

        @pl.when(n == nt)
        def _():
            dk_ref[...] = _b(carry_k[...])
            dv_ref[...] = _b(carry_v[...])

    cur = lambda n: (jnp.minimum(n, nt - 1), 0)
    done_map = lambda n: (jnp.maximum(n - 1, 0), 0)
    return _grid_call(
        body, nt + 1, (qkv, qkv, qkv, qkv, qkv, bias, sinks_rows, d_attn), exchange, name="attn_bwd",
        in_specs=_attn_in_specs(nt, True, nqb) + [pl.BlockSpec((tq, ATTN_WIDTH), cur)],
        out_specs=[pl.BlockSpec((tq, ATTN_WIDTH), cur), pl.BlockSpec((tq, KV_WIDTH), done_map),
                   pl.BlockSpec((tq, KV_WIDTH), done_map),
                   pl.BlockSpec((2, N_KV_HEADS, 4 * lq, 2 * lq), lambda n: (0, 0, 0, 0)), _vec(128)],
        out_shape=[jax.ShapeDtypeStruct((s, ATTN_WIDTH), BF16), jax.ShapeDtypeStruct((s, KV_WIDTH), BF16),
                   jax.ShapeDtypeStruct((s, KV_WIDTH), BF16),
                   jax.ShapeDtypeStruct((2, N_KV_HEADS, 4 * lq, 2 * lq), F32), jax.ShapeDtypeStruct((1, 128), F32)],
        scratch_shapes=[pltpu.VMEM((tq, KV_WIDTH), F32), pltpu.VMEM((tq, KV_WIDTH), F32)],
        compiler_params=_params())


def rel_bias_table(rel_bias, bucket):
    lq = ATTN_BLOCK
    group = N_Q_HEADS // N_KV_HEADS

    def body(rb_ref, bk_ref, o_ref):
        bk = bk_ref[...]
        prev_keys = lax.broadcasted_iota(jnp.int32, bk.shape, 1) < lq
        accs = [jnp.full(bk.shape, -1e30, F32) for _ in range(N_Q_HEADS)]
        for b in range(N_BUCKETS):
            hit = bk == b
            accs = [jnp.where(hit, rb_ref[b, h], acc) for h, acc in enumerate(accs)]
        for h in range(N_Q_HEADS):
            rows = slice((h % group) * lq, (h % group + 1) * lq)
            o_ref[0, h // group, rows, :] = accs[h]
            o_ref[1, h // group, rows, :] = jnp.where(prev_keys, -1e30, accs[h])

    return pl.pallas_call(
        body, name="rel_bias_table",
        in_specs=[pl.BlockSpec(memory_space=pltpu.SMEM), pl.BlockSpec(memory_space=pltpu.VMEM)],
        out_shape=jax.ShapeDtypeStruct((2, N_KV_HEADS, group * lq, 2 * lq), F32),
        compiler_params=_params(n_axes=0))(rel_bias, bucket)


def rel_bias_grad(dbias, bucket):
    lq = ATTN_BLOCK
    group = N_Q_HEADS // N_KV_HEADS

    def body(db_ref, bk_ref, o_ref):
        rows = lax.broadcasted_iota(jnp.int32, (N_BUCKETS, 128), 0)
        lanes = lax.broadcasted_iota(jnp.int32, (N_BUCKETS, 128), 1)
        bk = bk_ref[...]
        per_head = []
        for h in range(N_Q_HEADS):
            sl = slice((h % group) * lq, (h % group + 1) * lq)
            per_head.append(db_ref[0, h // group, sl, :] + db_ref[1, h // group, sl, :])

        def per_bucket(b, acc):
            hit = (bk == b).astype(F32)
            for h in range(N_Q_HEADS):
                val = jnp.sum(per_head[h] * hit, keepdims=True)
                acc = acc + jnp.where((rows == b) & (lanes == h), val, 0.0)
            return acc

        o_ref[...] = lax.fori_loop(0, N_BUCKETS, per_bucket, jnp.zeros((N_BUCKETS, 128), F32))

    return pl.pallas_call(body, name="rel_bias_grad", out_shape=jax.ShapeDtypeStruct((N_BUCKETS, 128), F32),
                          compiler_params=_params(n_axes=0))(dbias, bucket)


def _tri_sum(a, upper):
    n = a.shape[0]
    ri = lax.broadcasted_iota(jnp.int32, (n, n), 0)
    ci = lax.broadcasted_iota(jnp.int32, (n, n), 1)
    tri = ((ri <= ci) if upper else (ri >= ci)).astype(BF16)
    hi = a.astype(BF16)
    rest = a - hi.astype(F32)
    mid = rest.astype(BF16)
    lo = (rest - mid.astype(F32)).astype(BF16)
    dot = lambda part: jnp.dot(tri, part, preferred_element_type=F32)
    return dot(hi) + dot(mid) + dot(lo)


@jax.custom_vjp
def _cumsum_rows(a):
    return _tri_sum(a, False)


_cumsum_rows.defvjp(lambda a: (_tri_sum(a, False), None), lambda _, g: (_tri_sum(g, True),))


def _ssm_core(u, z, dt_raw, hprev, dt_bias, a_log, dskip, norm_w):
    lc = u.shape[0]
    xbc = _silu(u)
    xs, bm, cm = xbc[:, :SSM_WIDTH], xbc[:, SSM_WIDTH:SSM_WIDTH + SSM_BC], xbc[:, SSM_WIDTH + SSM_BC:]
    dt = jax.nn.softplus(dt_raw + dt_bias)
    adt = dt * (-jnp.exp(a_log))
    ri = lax.broadcasted_iota(jnp.int32, (lc, lc), 0)
    ci = lax.broadcasted_iota(jnp.int32, (lc, lc), 1)
    causal = ri >= ci
    acum = _cumsum_rows(adt)
    acum_t = acum.T
    last = acum[lc - 1:lc, :]
    per_group = SSM_HEADS // SSM_GROUPS
    lane = lax.broadcasted_iota(jnp.int32, (1, 128), 1)
    rowid = lax.broadcasted_iota(jnp.int32, (128, 1), 0)
    lo_lanes = lane < SSM_HEAD_DIM
    ys, hs = [], []
    for g in range(SSM_GROUPS):
        bg = bm[:, g * SSM_STATE:(g + 1) * SSM_STATE]
        cg = cm[:, g * SSM_STATE:(g + 1) * SSM_STATE]
        cb = mm_nt(cg, bg)
        for pp in range(per_group // 2):
            ha = g * per_group + 2 * pp
            xp = xs[:, ha * SSM_HEAD_DIM:(ha + 2) * SSM_HEAD_DIM]
            hp = hprev[ha * SSM_HEAD_DIM:(ha + 2) * SSM_HEAD_DIM, :]
            xcp = xp * jnp.where(lo_lanes, dt[:, ha:ha + 1], dt[:, ha + 1:ha + 2])
            y_h, st_h = [], []
            for h in (ha, ha + 1):
                col, rowv, lasth = acum[:, h:h + 1], acum_t[h:h + 1, :], last[:, h:h + 1]
                decay = jnp.exp(jnp.where(causal, col - rowv, -1e30))
                y_h.append(mm(cb * decay, xcp) + mm_nt(cg * jnp.exp(col), hp))
                st_h.append(mm_tn(xcp, bg * jnp.exp(lasth - col)))
            y_pair = jnp.where(lo_lanes, y_h[0], y_h[1])
            st_pair = jnp.where(rowid < SSM_HEAD_DIM, st_h[0], st_h[1])
            la, lb = last[:, ha:ha + 1], last[:, ha + 1:ha + 2]
            hs.append(jnp.exp(jnp.where(rowid < SSM_HEAD_DIM, la, lb)) * hp + st_pair)
            dsk = jnp.where(lo_lanes, dskip[:, ha:ha + 1], dskip[:, ha + 1:ha + 2])
            ys.append(y_pair + dsk * xp)
    y = jnp.concatenate(ys, axis=1) * _silu(z)
    gw = SSM_WIDTH // SSM_GROUPS
    outs = []
    for g in range(SSM_GROUPS):
        yg = y[:, g * gw:(g + 1) * gw]
        outs.append(yg * lax.rsqrt(jnp.mean(yg * yg, axis=-1, keepdims=True) + NORM_EPS))
    return jnp.concatenate(outs, axis=1) * norm_w, jnp.concatenate(hs, axis=0)


def _ssm_param_specs():
    return [pl.BlockSpec((SSM_CONV, XBC_WIDTH), _const), _vec(XBC_WIDTH), _vec(128), _vec(128), _vec(128),
            _vec(SSM_WIDTH)]


SSM_FWD_SUB = 2
SSM_BWD_SUB = 2


def ssm_fwd(xbc_raw, z, dt_raw, conv_w, conv_b, dt_bias, a_log, dskip, norm_w, exchange=None):
    s = xbc_raw.shape[0]
    lc = SSM_CHUNK
    lt = lc * SSM_FWD_SUB
    hrows = SSM_HEADS * SSM_HEAD_DIM

    def body(x_ref, halo_ref, z_ref, dt_ref, cw_ref, cb_ref, dtb_ref, al_ref, dk_ref, nw_ref,
             o_ref, hp_ref, state):
        i = pl.program_id(0)

        @pl.when(i == 0)
        def _():
            state[...] = jnp.zeros_like(state)

        halo = halo_ref[...] * (i > 0).astype(F32)
        xin = jnp.concatenate([halo, x_ref[...]], axis=0)
        u = (_conv_rows(xin, cw_ref[...], SSM_CONV) + cb_ref[...])[HALO:]
        h = state[...]
        for k in range(SSM_FWD_SUB):
            rows = slice(k * lc, (k + 1) * lc)
            hp_ref[k * hrows:(k + 1) * hrows, :] = h
            out, h = _ssm_core(u[rows], z_ref[rows, :], dt_ref[rows, :], h, dtb_ref[...], al_ref[...], dk_ref[...],
                               nw_ref[...])
            o_ref[rows, :] = _b(out)
        state[...] = h

    tile = lambda n: pl.BlockSpec((lt, n), _row)
    halo_spec = pl.BlockSpec((HALO, XBC_WIDTH), lambda i: (jnp.maximum(i * (lt // HALO) - 1, 0), 0))
    return _grid_call(
        body, s // lt, (xbc_raw, xbc_raw, z, dt_raw, conv_w, conv_b, dt_bias, a_log, dskip, norm_w), exchange,
        name="ssm_fwd",
        in_specs=[tile(XBC_WIDTH), halo_spec, tile(SSM_WIDTH), tile(DT_PAD)] + _ssm_param_specs(),
        out_specs=[tile(SSM_WIDTH), pl.BlockSpec((SSM_FWD_SUB * hrows, SSM_STATE), _row)],
        out_shape=[jax.ShapeDtypeStruct((s, SSM_WIDTH), BF16),
                   jax.ShapeDtypeStruct((s // lc * hrows, SSM_STATE), F32)],
        scratch_shapes=[pltpu.VMEM((hrows, SSM_STATE), F32)],
        compiler_params=_params())


def ssm_bwd(xbc_raw, z, dt_raw, hprev_all, d_out, conv_w, conv_b, dt_bias, a_log, dskip, norm_w, exchange=None):
    s = xbc_raw.shape[0]
    lc = SSM_CHUNK
    sub = SSM_BWD_SUB
    lt = lc * sub
    nt = s // lt
    hrows = SSM_HEADS * SSM_HEAD_DIM

    def body(x_ref, halo_ref, z_ref, dt_ref, hp_ref, do_ref, cw_ref, cb_ref, dtb_ref, al_ref, dk_ref, nw_ref,
             dx_ref, dz_ref, ddt_ref, dcw_ref, dcb_ref, ddtb_ref, dal_ref, ddk_ref, dnw_ref, dstate, du_next):
        i = pl.program_id(0)
        tile_no = nt - 1 - i

        @pl.when(i == 0)
        def _():
            dstate[...] = jnp.zeros_like(dstate)
            du_next[...] = jnp.zeros_like(du_next)
            for r in (dcw_ref, dcb_ref, ddtb_ref, dal_ref, ddk_ref, dnw_ref):
                r[...] = jnp.zeros_like(r)

        halo = halo_ref[...] * (tile_no > 0).astype(F32)
        xin = jnp.concatenate([halo, x_ref[...]], axis=0)
        cw = cw_ref[...]
        u = (_conv_rows(xin, cw, SSM_CONV) + cb_ref[...])[HALO:]
        dh = dstate[...]
        dus = [None] * sub
        for k in reversed(range(sub)):
            rows = slice(k * lc, (k + 1) * lc)
            _, vjp = jax.vjp(_ssm_core, u[rows], z_ref[rows, :], dt_ref[rows, :], hp_ref[k * hrows:(k + 1) * hrows, :],
                             dtb_ref[...], al_ref[...], dk_ref[...], nw_ref[...])
            dus[k], dz, ddt, dh, ddtb, dal, ddk, dnw = vjp((do_ref[rows, :], dh))
            dz_ref[rows, :] = _b(dz)
            ddt_ref[rows, :] = _b(ddt)
            ddtb_ref[...] += ddtb
            dal_ref[...] += dal
            ddk_ref[...] += ddk
            dnw_ref[...] += dnw
        dstate[...] = dh
        du = jnp.concatenate(dus, axis=0)
        du_ext = jnp.concatenate([du, du_next[...]], axis=0)
        dx_ref[...] = _b(_conv_rows_t(du_ext, cw, SSM_CONV)[:lt])
        du_next[...] = du[:HALO]
        sums = [jnp.sum(du * pltpu.roll(xin, j, axis=0)[HALO:] if j else du * xin[HALO:], axis=0, keepdims=True)
                for j in range(SSM_CONV)]
        dcw_ref[...] += jnp.concatenate(sums[::-1] + [jnp.zeros((8 - SSM_CONV, XBC_WIDTH), F32)], axis=0)
        dcb_ref[...] += jnp.sum(du, axis=0, keepdims=True)

    rev = lambda i: (nt - 1 - i, 0)
    tile = lambda n: pl.BlockSpec((lt, n), rev)
    halo_spec = pl.BlockSpec((HALO, XBC_WIDTH), lambda i: (jnp.maximum((nt - 1 - i) * (lt // HALO) - 1, 0), 0))
    acc = lambda r, n: pl.BlockSpec((r, n), _const)
    return _grid_call(
        body, nt, (xbc_raw, xbc_raw, z, dt_raw, hprev_all, d_out, conv_w, conv_b, dt_bias, a_log, dskip, norm_w),
        exchange, name="ssm_bwd",
        in_specs=[tile(XBC_WIDTH), halo_spec, tile(SSM_WIDTH), tile(DT_PAD),
                  pl.BlockSpec((sub * hrows, SSM_STATE), rev), tile(SSM_WIDTH)] + _ssm_param_specs(),
        out_specs=[tile(XBC_WIDTH), tile(SSM_WIDTH), tile(DT_PAD), acc(8, XBC_WIDTH), acc(1, XBC_WIDTH),
                   acc(1, 128), acc(1, 128), acc(1, 128), acc(1, SSM_WIDTH)],
        out_shape=[jax.ShapeDtypeStruct((s, XBC_WIDTH), BF16), jax.ShapeDtypeStruct((s, SSM_WIDTH), BF16),
                   jax.ShapeDtypeStruct((s, DT_PAD), BF16), jax.ShapeDtypeStruct((8, XBC_WIDTH), F32),
                   jax.ShapeDtypeStruct((1, XBC_WIDTH), F32), jax.ShapeDtypeStruct((1, 128), F32),
                   jax.ShapeDtypeStruct((1, 128), F32), jax.ShapeDtypeStruct((1, 128), F32),
                   jax.ShapeDtypeStruct((1, SSM_WIDTH), F32)],
        scratch_shapes=[pltpu.VMEM((hrows, SSM_STATE), F32), pltpu.VMEM((HALO, XBC_WIDTH), F32)],
        compiler_params=_params())


def mix_out(attn, ssm, x, w_out16, gate1, post_mix_w, pre_ffn_w, scale2, shift2, tm=2 * TOKEN_TILE):
    s = x.shape[0]

    def body(a_ref, s_ref, x_ref, w_ref, g_ref, pw_ref, fw_ref, sc_ref, sh_ref, mixed_ref, x1_ref, h2_ref):
        mixed = (jnp.dot(a_ref[...], w_ref[:ATTN_WIDTH, :], preferred_element_type=F32)
                 + jnp.dot(s_ref[...], w_ref[ATTN_WIDTH:, :], preferred_element_type=F32))
        mixed_ref[...] = mixed
        x1 = x_ref[...] + g_ref[...] * _rms(mixed, pw_ref[...])
        x1_ref[...] = x1
        h2_ref[...] = _b(_norm_mod(x1, fw_ref[...], sc_ref[...], sh_ref[...]))

    tile = lambda n: pl.BlockSpec((tm, n), _row)
    return pl.pallas_call(
        body, name="mix_out", grid=(s // tm,),
        in_specs=[tile(ATTN_WIDTH), tile(SSM_WIDTH), tile(D_MODEL), pl.BlockSpec((D_MODEL, D_MODEL), _const)]
        + [_vec(D_MODEL)] * 5,
        out_specs=[tile(D_MODEL)] * 3,
        out_shape=[jax.ShapeDtypeStruct((s, D_MODEL), F32), jax.ShapeDtypeStruct((s, D_MODEL), F32),
                   jax.ShapeDtypeStruct((s, D_MODEL), BF16)],
        compiler_params=_params())(attn, ssm, x, w_out16, gate1, post_mix_w, pre_ffn_w, scale2, shift2)


GELU_K0, GELU_K1 = math.sqrt(2.0 / math.pi), 0.044715


def _gate(ug, uv):
    return jax.nn.gelu(ug, approximate=True) * uv


def _gate_bwd(ug, uv, df):
    sq = ug * ug
    t = jnp.tanh(ug * (GELU_K0 + (GELU_K0 * GELU_K1) * sq))
    half = 0.5 + 0.5 * t
    slope = half + ug * (1.0 - t * t) * (0.5 * GELU_K0 + (1.5 * GELU_K0 * GELU_K1) * sq)
    return df * uv * slope, df * (ug * half)


def _resident(shape):
    return pl.BlockSpec(shape, _const, pipeline_mode=pl.Buffered(1))


def up_gate(h2, w_up16, conv_w, conv_b, tm=TOKEN_TILE):
    s = h2.shape[0]

    def body(h_ref, halo_ref, w_ref, cw_ref, cb_ref, u_ref, uraw_ref, f_ref):
        halo = halo_ref[...]
        halo = jnp.where(pl.program_id(0) > 0, halo, jnp.zeros_like(halo))
        hin = jnp.concatenate([halo, h_ref[...]], axis=0)
        for lo in range(0, D_FF, FF_CHUNK):
            halves = []
            for base in (lo, D_FF + lo):
                cols = slice(base, base + FF_CHUNK)
                uraw = jnp.dot(hin, w_ref[:, cols], preferred_element_type=F32)
                uraw_ref[:, cols] = _b(uraw[NEXT:])
                u = (_conv_rows(uraw, cw_ref[:, cols], FFN_CONV) + cb_ref[:, cols])[NEXT:]
                u_ref[:, cols] = u
                halves.append(u)
            f_ref[:, lo:lo + FF_CHUNK] = _b(_gate(*halves))

    tile = lambda n: pl.BlockSpec((tm, n), _row)
    halo_spec = pl.BlockSpec((NEXT, D_MODEL), lambda i: (jnp.maximum(i * (tm // NEXT) - 1, 0), 0))
    return pl.pallas_call(
        body, name="up_gate", grid=(s // tm,),
        in_specs=[tile(D_MODEL), halo_spec, _resident((D_MODEL, 2 * D_FF)),
                  pl.BlockSpec((FFN_CONV, 2 * D_FF), _const), _vec(2 * D_FF)],
        out_specs=[tile(2 * D_FF), tile(2 * D_FF), tile(D_FF)],
        out_shape=[jax.ShapeDtypeStruct((s, 2 * D_FF), F32), jax.ShapeDtypeStruct((s, 2 * D_FF), BF16),
                   jax.ShapeDtypeStruct((s, D_FF), BF16)],
        compiler_params=_params())(h2, h2, w_up16, conv_w, conv_b)


DOWN_LOSS_TILE = 512


def down_loss(f16, w_down16, x1, target, gate2, post_ffn_w, tm=DOWN_LOSS_TILE):
    s = x1.shape[0]
    tm = min(tm, s)

    def body(f_ref, wd_ref, x1_ref, t_ref, g_ref, pw_ref, dffn_ref, dy_ref, loss_ref, dg_ref, dpw_ref, gw_ref):
        i = pl.program_id(0)

        @pl.when(i == 0)
        def _():
            loss_ref[...] = jnp.zeros_like(loss_ref)
            dg_ref[...] = jnp.zeros_like(dg_ref)
            dpw_ref[...] = jnp.zeros_like(dpw_ref)
            gw_ref[...] = jnp.zeros_like(gw_ref)

        ffn = jnp.dot(f_ref[...], wd_ref[...], preferred_element_type=F32)
        x1 = x1_ref[...]
        x2 = x1 + g_ref[...] * _rms(ffn, pw_ref[...])
        err = x2 - t_ref[...]
        dy = err * (1.0 / D_MODEL)
        dy_ref[...] = dy
        loss_ref[...] += 0.5 * jnp.sum(jnp.mean(err * err, axis=-1, keepdims=True))
        dffn, dg, dpw = _gated_rms_bwd(ffn, g_ref[...], pw_ref[...], dy)
        dffn16 = _b(dffn)
        dffn_ref[...] = dffn16
        dg_ref[...] += dg
        dpw_ref[...] += dpw
        gw_ref[...] += _tn(f_ref[...], dffn16)

    tile = lambda n: pl.BlockSpec((tm, n), _row)
    return pl.pallas_call(
        body, name="down_loss", grid=(s // tm,),
        in_specs=[tile(D_FF), _resident((D_FF, D_MODEL)), tile(D_MODEL), tile(D_MODEL), _vec(D_MODEL), _vec(D_MODEL)],
        out_specs=[tile(D_MODEL), tile(D_MODEL), _vec(128), _vec(D_MODEL), _vec(D_MODEL),
                   pl.BlockSpec((D_FF, D_MODEL), _const)],
        out_shape=[jax.ShapeDtypeStruct((s, D_MODEL), BF16), jax.ShapeDtypeStruct((s, D_MODEL), F32),
                   jax.ShapeDtypeStruct((1, 128), F32), jax.ShapeDtypeStruct((1, D_MODEL), F32),
                   jax.ShapeDtypeStruct((1, D_MODEL), F32), jax.ShapeDtypeStruct((D_FF, D_MODEL), F32)],
        compiler_params=_params())(f16, w_down16, x1, target, gate2, post_ffn_w)


BWD_CHUNK = 256


def ffn_bwd(u, u_raw16, d_ffn, conv_w, w_down_t16, w_up_t16, tm=TOKEN_TILE):
    s = u.shape[0]
    nt = s // tm

    def body(u_ref, unext_ref, uraw_ref, d_ref, dnext_ref, cw_ref, wdt_ref, wut_ref,
             du_ref, dh_ref, dcw_ref, dcb_ref):
        i = pl.program_id(0)

        @pl.when(i == 0)
        def _():
            dcw_ref[...] = jnp.zeros_like(dcw_ref)
            dcb_ref[...] = jnp.zeros_like(dcb_ref)

        dnext = dnext_ref[...]
        dnext = jnp.where(i < nt - 1, dnext, jnp.zeros_like(dnext))
        dff = jnp.concatenate([d_ref[...], dnext], axis=0)
        rows_ext = tm + NEXT
        for lo in range(0, D_FF, BWD_CHUNK):
            gcols, vcols = slice(lo, lo + BWD_CHUNK), slice(D_FF + lo, D_FF + lo + BWD_CHUNK)
            ug = jnp.concatenate([u_ref[:, gcols], unext_ref[:, gcols]], axis=0)
            uv = jnp.concatenate([u_ref[:, vcols], unext_ref[:, vcols]], axis=0)
            df = jnp.dot(dff, wdt_ref[:, gcols], preferred_element_type=F32)
            for cols, du in zip((gcols, vcols), _gate_bwd(ug, uv, df)):
                cw = cw_ref[:, cols]
                du1 = pltpu.roll(du, rows_ext - 1, axis=0)
                du2 = pltpu.roll(du, rows_ext - 2, axis=0)
                du_ref[:, cols] = _b((du * cw[2:3, :] + du1 * cw[1:2, :] + du2 * cw[0:1, :])[:tm])
                xr = uraw_ref[:, cols].astype(F32)
                rows = [jnp.sum(xr * d_[:tm], axis=0, keepdims=True) for d_ in (du2, du1, du)]
                dcw_ref[:, cols] += jnp.concatenate(rows + [jnp.zeros((8 - FFN_CONV, BWD_CHUNK), F32)], axis=0)
                dcb_ref[:, cols] += jnp.sum(du[:tm], axis=0, keepdims=True)
        dh_ref[...] = jnp.dot(du_ref[...], wut_ref[...], preferred_element_type=F32)

    tile = lambda n: pl.BlockSpec((tm, n), _row)
    nxt = lambda i: (jnp.minimum((i + 1) * (tm // NEXT), s // NEXT - 1), 0)
    return pl.pallas_call(
        body, name="ffn_bwd", grid=(nt,),
        in_specs=[tile(2 * D_FF), pl.BlockSpec((NEXT, 2 * D_FF), nxt), tile(2 * D_FF), tile(D_MODEL),
                  pl.BlockSpec((NEXT, D_MODEL), nxt), pl.BlockSpec((FFN_CONV, 2 * D_FF), _const),
                  _resident((D_MODEL, D_FF)), _resident((2 * D_FF, D_MODEL))],
        out_specs=[tile(2 * D_FF), tile(D_MODEL), pl.BlockSpec((8, 2 * D_FF), _const), _vec(2 * D_FF)],
        out_shape=[jax.ShapeDtypeStruct((s, 2 * D_FF), BF16), jax.ShapeDtypeStruct((s, D_MODEL), F32),
                   jax.ShapeDtypeStruct((8, 2 * D_FF), F32), jax.ShapeDtypeStruct((1, 2 * D_FF), F32)],
        compiler_params=_params())(u, u, u_raw16, d_ffn, d_ffn, conv_w, w_down_t16, w_up_t16)


def mix_bwd(dh2, x1, dy, mixed, attn, ssm, w_out_t16, pre_ffn_w, scale2, gate1, post_mix_w, tm=2 * TOKEN_TILE):
    s = x1.shape[0]

    def body(dh_ref, x1_ref, dy_ref, mx_ref, a_ref, s_ref, w_ref, fw_ref, sc_ref, g_ref, pw_ref,
             dx1_ref, da_ref, ds_ref, dfw_ref, dsc_ref, dsh_ref, dg_ref, dpw_ref, gw_ref):
        accs = (dfw_ref, dsc_ref, dsh_ref, dg_ref, dpw_ref)

        @pl.when(pl.program_id(0) == 0)
        def _():
            for r in accs + (gw_ref,):
                r[...] = jnp.zeros_like(r)

        dx1, dfw, dsc, dsh = _norm_mod_bwd(x1_ref[...], fw_ref[...], sc_ref[...], dh_ref[...])
        dx1 = dx1 + dy_ref[...]
        dx1_ref[...] = dx1
        dmixed, dg, dpw = _gated_rms_bwd(mx_ref[...], g_ref[...], pw_ref[...], dx1)
        dm16 = _b(dmixed)
        dmix_in = jnp.dot(dm16, w_ref[...], preferred_element_type=F32)
        da_ref[...] = _b(dmix_in[:, :ATTN_WIDTH])
        ds_ref[...] = dmix_in[:, ATTN_WIDTH:]
        gw_ref[:ATTN_WIDTH, :] += _tn(a_ref[...], dm16)
        gw_ref[ATTN_WIDTH:, :] += _tn(s_ref[...], dm16)
        for r, v in zip(accs, (dfw, dsc, dsh, dg, dpw)):
            r[...] += v

    tile = lambda n: pl.BlockSpec((tm, n), _row)
    return pl.pallas_call(
        body, name="mix_bwd", grid=(s // tm,),
        in_specs=[tile(D_MODEL)] * 4 + [tile(ATTN_WIDTH), tile(SSM_WIDTH), _resident((D_MODEL, D_MODEL))]
        + [_vec(D_MODEL)] * 4,
        out_specs=[tile(D_MODEL), tile(ATTN_WIDTH), tile(SSM_WIDTH)] + [_vec(D_MODEL)] * 5
        + [pl.BlockSpec((D_MODEL, D_MODEL), _const)],
        out_shape=[jax.ShapeDtypeStruct((s, D_MODEL), F32), jax.ShapeDtypeStruct((s, ATTN_WIDTH), BF16),
                   jax.ShapeDtypeStruct((s, SSM_WIDTH), F32)]
        + [jax.ShapeDtypeStruct((1, D_MODEL), F32)] * 5 + [jax.ShapeDtypeStruct((D_MODEL, D_MODEL), F32)],
        compiler_params=_params())(dh2, x1, dy, mixed, attn, ssm, w_out_t16, pre_ffn_w, scale2, gate1, post_mix_w)


INPROJ_BWD_TILE = 512


def inproj_bwd(dq, dk, dv, dxbc, dz, ddt, x, dx1, h1, w_in_t16, pre_mix_w, scale1, tm=INPROJ_BWD_TILE):
    s = x.shape[0]
    tm = min(tm, s)

    def body(dq_ref, dk_ref, dv_ref, dxbc_ref, dz_ref, ddt_ref, x_ref, dx1_ref, h_ref, w_ref, pw_ref, sc_ref,
             gx_ref, dpw_ref, dsc_ref, dsh_ref, gw_ref):
        accs = (dpw_ref, dsc_ref, dsh_ref)

        @pl.when(pl.program_id(0) == 0)
        def _():
            for r in accs + (gw_ref,):
                r[...] = jnp.zeros_like(r)

        h16 = h_ref[...]
        dh = None
        off_k, off_v = ATTN_WIDTH, ATTN_WIDTH + KV_WIDTH
        for r, lo, hi in ((dq_ref, 0, off_k), (dk_ref, off_k, off_v), (dv_ref, off_v, OFF_XBC),
                          (dxbc_ref, OFF_XBC, OFF_Z), (dz_ref, OFF_Z, OFF_DT), (ddt_ref, OFF_DT, PROJ_PAD)):
            d16 = _b(r[...])
            part = jnp.dot(d16, w_ref[lo:hi, :], preferred_element_type=F32)
            dh = part if dh is None else dh + part
            gw_ref[:, lo:hi] += _tn(h16, d16)
        dx, dpw, dsc, dsh = _norm_mod_bwd(x_ref[...], pw_ref[...], sc_ref[...], dh)
        gx_ref[...] = dx1_ref[...] + dx
        for r, v in zip(accs, (dpw, dsc, dsh)):
            r[...] += v

    tile = lambda n: pl.BlockSpec((tm, n), _row)
    return pl.pallas_call(
        body, name="inproj_bwd", grid=(s // tm,),
        in_specs=[tile(ATTN_WIDTH), tile(KV_WIDTH), tile(KV_WIDTH), tile(XBC_WIDTH), tile(SSM_WIDTH), tile(DT_PAD),
                  tile(D_MODEL), tile(D_MODEL), tile(D_MODEL), _resident((PROJ_PAD, D_MODEL))] + [_vec(D_MODEL)] * 2,
        out_specs=[tile(D_MODEL)] + [_vec(D_MODEL)] * 3 + [pl.BlockSpec((D_MODEL, PROJ_PAD), _const)],
        out_shape=[jax.ShapeDtypeStruct((s, D_MODEL), F32)] + [jax.ShapeDtypeStruct((1, D_MODEL), F32)] * 3
        + [jax.ShapeDtypeStruct((D_MODEL, PROJ_PAD), F32)],
        compiler_params=_params())(dq, dk, dv, dxbc, dz, ddt, x, dx1, h1, w_in_t16, pre_mix_w, scale1)


def _adam(g, w, m, v):
    new_m = ADAM_B1 * m + (1.0 - ADAM_B1) * g
    new_v = ADAM_B2 * v + (1.0 - ADAM_B2) * jnp.square(g)
    m_hat = new_m / (1.0 - ADAM_B1 ** ADAM_STEP)
    v_hat = new_v / (1.0 - ADAM_B2 ** ADAM_STEP)
    return -ADAM_LR * (m_hat / (jnp.sqrt(v_hat) + ADAM_EPS) + ADAM_WD * w), new_m, new_v


ROW_PARAMS = (("b_ada", 6144, 6144), ("pre_mix_w", 1024, 1024), ("attn_sinks", 128, 8), ("ssm_conv_b", 1024, 1024),
              ("ssm_dt_bias", 128, 8), ("ssm_a_log", 128, 8), ("ssm_d", 128, 8), ("ssm_norm_w", 512, 512),
              ("post_mix_w", 1024, 1024), ("pre_ffn_w", 1024, 1024), ("ffn_conv_b", 5632, 5632),
              ("post_ffn_w", 1024, 1024))
LOSS_LANES = 128


def adamw_small(row_all, rb_all, rel_bias_wmv, row_wmv):
    n_rows = len(ROW_PARAMS)

    def body(*refs):
        row_ref, rb_ref = refs[:2]
        wmv = refs[2:5 + 3 * n_rows]
        outs = refs[5 + 3 * n_rows:]
        g_row, g_rb = row_ref[0], rb_ref[0]
        for k in range(1, N_DEV):
            g_row = g_row + row_ref[k]
            g_rb = g_rb + rb_ref[k]
        outs[0][...] = g_row[:, :LOSS_LANES]
        grads = [g_rb[:, :N_Q_HEADS]]
        off = LOSS_LANES
        for _, lanes, width in ROW_PARAMS:
            grads.append(g_row[:, off:off + width])
            off += lanes
        for i, g in enumerate(grads):
            w_ref, m_ref, v_ref = wmv[3 * i:3 * i + 3]
            g_out, d_out, m_out, v_out = outs[1 + 4 * i:5 + 4 * i]
            g_out[...] = g
            d_out[...], m_out[...], v_out[...] = _adam(g, w_ref[...], m_ref[...], v_ref[...])

    flat_wmv = list(rel_bias_wmv) + [a for wmv in row_wmv for a in wmv]
    shapes = [jax.ShapeDtypeStruct((1, LOSS_LANES), F32)] + [jax.ShapeDtypeStruct((N_BUCKETS, N_Q_HEADS), F32)] * 4
    for _, _, width in ROW_PARAMS:
        shapes += [jax.ShapeDtypeStruct((1, width), F32)] * 4
    return pl.pallas_call(body, name="adamw_small", out_shape=shapes,
                          compiler_params=_params(n_axes=0))(row_all, rb_all, *flat_wmv)


def adamw(parts, w, m, v, name):
    p, r, n = parts.shape
    tr = _row_tile(r)

    def body(p_ref, w_ref, m_ref, v_ref, g_ref, d_ref, nm_ref, nv_ref):
        g = p_ref[0].astype(F32)
        for k in range(1, p):
            g = g + p_ref[k].astype(F32)
        g_ref[...] = g
        d_ref[...], nm_ref[...], nv_ref[...] = _adam(g, w_ref[...], m_ref[...], v_ref[...])

    tile = pl.BlockSpec((tr, n), _row)
    return pl.pallas_call(
        body, name=name, grid=(r // tr,),
        in_specs=[pl.BlockSpec((p, tr, n), lambda i: (0, i, 0)), tile, tile, tile],
        out_specs=[tile] * 4, out_shape=[jax.ShapeDtypeStruct((r, n), F32)] * 4,
        compiler_params=_params())(parts, w, m, v)


def _bucket_table():
    lq = ATTN_BLOCK
    qi = np.arange(lq)[:, None] + lq
    kj = np.arange(2 * lq)[None, :]
    dist = qi - kj
    d = np.maximum(dist, 0)
    max_exact = N_BUCKETS // 2
    nf = np.maximum(d, 1).astype(np.float32)
    large = max_exact + (np.log(nf / max_exact) / math.log(REL_MAX_DIST / max_exact)
                         * (N_BUCKETS - max_exact)).astype(np.int32)
    large = np.minimum(large, N_BUCKETS - 1)
    bucket = np.where(d < max_exact, d, large).astype(np.int32)
    in_band = (dist >= 0) & (dist < REL_MAX_DIST)
    return np.where(in_band, bucket, -1).astype(np.int32)


def _cols_from_blocks(g):
    return jnp.transpose(g, (1, 0, 2)).reshape(g.shape[1], N_DEV * g.shape[2])


def _cols_to_blocks(a):
    r, n = a.shape
    return jnp.transpose(a.reshape(r, N_DEV, n // N_DEV), (1, 0, 2))


def _perm_in_rows(wt):
    pad = jnp.zeros((DT_PAD - SSM_HEADS, wt.shape[1]), wt.dtype)
    return jnp.concatenate([wt[:768], wt[768:1280], wt[1792:2304], wt[1280:1792], wt[2304:2312], pad], axis=0)


def _unperm_in(g):
    return jnp.concatenate([g[:, :768], g[:, 768:1280], g[:, 1792:2304], g[:, 1280:1792], g[:, 2304:2312]], axis=1)


def _lane_pad(v, n=128):
    return jnp.pad(v, ((0, 0), (0, n - v.shape[1])))


def kernel(x, c, rel_bias, w_ada, b_ada, pre_mix_w, w_in, attn_sinks, ssm_conv_w, ssm_conv_b, ssm_dt_bias, ssm_a_log, ssm_d, ssm_norm_w, w_out, post_mix_w, pre_ffn_w, w_up, ffn_conv_w, ffn_conv_b, w_down, post_ffn_w, loss_target, m_rel_bias, m_w_ada, m_b_ada, m_pre_mix_w, m_w_in, m_attn_sinks, m_ssm_conv_w, m_ssm_conv_b, m_ssm_dt_bias, m_ssm_a_log, m_ssm_d, m_ssm_norm_w, m_w_out, m_post_mix_w, m_pre_ffn_w, m_w_up, m_ffn_conv_w, m_ffn_conv_b, m_w_down, m_post_ffn_w, v_rel_bias, v_w_ada, v_b_ada, v_pre_mix_w, v_w_in, v_attn_sinks, v_ssm_conv_w, v_ssm_conv_b, v_ssm_dt_bias, v_ssm_a_log, v_ssm_d, v_ssm_norm_w, v_w_out, v_post_mix_w, v_pre_ffn_w, v_w_up, v_ffn_conv_w, v_ffn_conv_b, v_w_down, v_post_ffn_w):
    weights = dict(rel_bias=rel_bias, w_ada=w_ada, b_ada=b_ada, pre_mix_w=pre_mix_w, w_in=w_in, attn_sinks=attn_sinks, ssm_conv_w=ssm_conv_w, ssm_conv_b=ssm_conv_b, ssm_dt_bias=ssm_dt_bias, ssm_a_log=ssm_a_log, ssm_d=ssm_d, ssm_norm_w=ssm_norm_w, w_out=w_out, post_mix_w=post_mix_w, pre_ffn_w=pre_ffn_w, w_up=w_up, ffn_conv_w=ffn_conv_w, ffn_conv_b=ffn_conv_b, w_down=w_down, post_ffn_w=post_ffn_w)
    mom_m = dict(rel_bias=m_rel_bias, w_ada=m_w_ada, b_ada=m_b_ada, pre_mix_w=m_pre_mix_w, w_in=m_w_in, attn_sinks=m_attn_sinks, ssm_conv_w=m_ssm_conv_w, ssm_conv_b=m_ssm_conv_b, ssm_dt_bias=m_ssm_dt_bias, ssm_a_log=m_ssm_a_log, ssm_d=m_ssm_d, ssm_norm_w=m_ssm_norm_w, w_out=m_w_out, post_mix_w=m_post_mix_w, pre_ffn_w=m_pre_ffn_w, w_up=m_w_up, ffn_conv_w=m_ffn_conv_w, ffn_conv_b=m_ffn_conv_b, w_down=m_w_down, post_ffn_w=m_post_ffn_w)
    mom_v = dict(rel_bias=v_rel_bias, w_ada=v_w_ada, b_ada=v_b_ada, pre_mix_w=v_pre_mix_w, w_in=v_w_in, attn_sinks=v_attn_sinks, ssm_conv_w=v_ssm_conv_w, ssm_conv_b=v_ssm_conv_b, ssm_dt_bias=v_ssm_dt_bias, ssm_a_log=v_ssm_a_log, ssm_d=v_ssm_d, ssm_norm_w=v_ssm_norm_w, w_out=v_w_out, post_mix_w=v_post_mix_w, pre_ffn_w=v_pre_ffn_w, w_up=v_w_up, ffn_conv_w=v_ffn_conv_w, ffn_conv_b=v_ffn_conv_b, w_down=v_w_down, post_ffn_w=v_post_ffn_w)
    order = ['rel_bias', 'w_ada', 'b_ada', 'pre_mix_w', 'w_in', 'attn_sinks', 'ssm_conv_w', 'ssm_conv_b', 'ssm_dt_bias', 'ssm_a_log', 'ssm_d', 'ssm_norm_w', 'w_out', 'post_mix_w', 'pre_ffn_w', 'w_up', 'ffn_conv_w', 'ffn_conv_b', 'w_down', 'post_ffn_w']

    me = 4 * lax.axis_index("x") + 2 * lax.axis_index("y") + lax.axis_index("c")
    xs_ = x[0]
    target = loss_target[0]

    (w_in_g, scw_g, fcw_g, c_g) = all_gather([_b(w_in[0]).T, ssm_conv_w[0], ffn_conv_w[0], c], "gather_weights")
    w_in_t16 = _perm_in_rows(w_in_g.reshape(IN_PROJ_WIDTH, D_MODEL))
    w_in16 = w_in_t16.T
    ssm_cw = _cols_from_blocks(scw_g)
    ffn_cw = _cols_from_blocks(fcw_g)
    c_all = c_g.reshape(N_DEV, D_MODEL)

    n_cols = w_ada.shape[2]
    b_cols = lax.dynamic_slice(b_ada, (0, me * n_cols), (1, n_cols))
    mod_part = ada_fwd(c_all, w_ada[0], b_cols)
    (mod_rows,) = all_to_all([mod_part.reshape(N_DEV, 1, n_cols)], "scatter_mod")
    mod = mod_rows.reshape(N_MOD, 1, D_MODEL)
    shift1, scale1, gate1, shift2, scale2, gate2 = (mod[i] for i in range(N_MOD))

    bucket_band = jnp.asarray(_bucket_table())
    bias = rel_bias_table(rel_bias, bucket_band)
    sinks_row = _lane_pad(attn_sinks)
    dt_bias, a_log, dskip = _lane_pad(ssm_dt_bias), _lane_pad(ssm_a_log), _lane_pad(ssm_d)

    h1, qkv, xbc_raw, z, dt_raw, w_out_g = pre_mix_inproj(
        xs_, pre_mix_w, scale1, shift1, w_in16, [(_b(w_out[0]), False)])
    attn, w_up_g = attn_fwd(qkv, bias, sinks_row, [(_b(w_up[0]).T, False)])
    ssm, hprev_all, w_down_g = ssm_fwd(xbc_raw, z, dt_raw, ssm_cw, ssm_conv_b, dt_bias, a_log, dskip, ssm_norm_w,
                                       [(_b(w_down[0]), False)])
    w_out16 = w_out_g.reshape(D_MODEL, D_MODEL)
    w_out_t16 = w_out16.T
    w_up_t16 = w_up_g.reshape(2 * D_FF, D_MODEL)
    w_up16 = w_up_t16.T
    w_down16 = w_down_g.reshape(D_FF, D_MODEL)
    w_down_t16 = w_down16.T
    mixed, x1, h2 = mix_out(attn, ssm, xs_, w_out16, gate1, post_mix_w, pre_ffn_w, scale2, shift2)
    u, u_raw16, f16 = up_gate(h2, w_up16, ffn_cw, ffn_conv_b)
    d_ffn, dy, loss_part, d_gate2, d_post_ffn_w, g_w_down = down_loss(f16, w_down16, x1, target, gate2, post_ffn_w)

    du_raw, dh2, d_ffn_cw, d_ffn_cb = ffn_bwd(u, u_raw16, d_ffn, ffn_cw, w_down_t16, w_up_t16)
    g_w_up = matmul_tn(h2, du_raw, "grad_w_up", D_MODEL, FF_CHUNK, tk=2048)
    (dx1, d_attn, d_ssm, d_pre_ffn_w, d_scale2, d_shift2, d_gate1, d_post_mix_w, g_w_out) = mix_bwd(
        dh2, x1, dy, mixed, attn, ssm, w_out_t16, pre_ffn_w, scale2, gate1, post_mix_w)
    dq, dk, dv, dbias, dsinks, p_w_down = attn_bwd(
        qkv, bias, sinks_row, d_attn, [(g_w_down.reshape(N_DEV, D_FF // N_DEV, D_MODEL), True)])
    d_rel_bias = rel_bias_grad(dbias, bucket_band)
    (dxbc, dz, ddt, d_ssm_cw, d_ssm_cb, d_dt_bias, d_a_log, d_dskip, d_norm_w, p_w_up, p_w_out) = ssm_bwd(
        xbc_raw, z, dt_raw, hprev_all, d_ssm, ssm_cw, ssm_conv_b, dt_bias, a_log, dskip, ssm_norm_w,
        [(_cols_to_blocks(g_w_up), True), (g_w_out.reshape(N_DEV, D_MODEL // N_DEV, D_MODEL), True)])
    grad_x, d_pre_mix_w, d_scale1, d_shift1, g_w_in_perm = inproj_bwd(
        dq, dk, dv, dxbc, dz, ddt, xs_, dx1, h1, w_in_t16, pre_mix_w, scale1)
    g_w_in = _unperm_in(g_w_in_perm)

    d_mod = jnp.concatenate([d_shift1, d_scale1, d_gate1, d_shift2, d_scale2, d_gate2], axis=1)
    late = ("w_in", "ssm_conv_w", "ffn_conv_w")
    full = [_cols_to_blocks(g_w_in), _cols_to_blocks(d_ssm_cw[:SSM_CONV]), _cols_to_blocks(d_ffn_cw[:FFN_CONV])]
    core = lax.axis_index("c").astype(jnp.int32).reshape(1)
    got = pair_exchange(full, "pair_grads")
    chip_sums = [pair_sum(f_, g_, core, "pair_sum_" + k) for k, f_, g_ in zip(late, full, got)]
    chip_parts = all_to_all(chip_sums, "scatter_grads", CHIP_FLIPS, _chip_index)
    (d_mod_rows,) = all_to_all([d_mod.reshape(N_DEV, 1, n_cols)], "scatter_dmod")
    g_w_ada = ada_bwd(c_all, d_mod_rows.reshape(N_DEV, n_cols))

    row_g = dict(b_ada=d_mod, pre_mix_w=d_pre_mix_w, attn_sinks=dsinks, ssm_conv_b=d_ssm_cb, ssm_dt_bias=d_dt_bias,
                 ssm_a_log=d_a_log, ssm_d=d_dskip, ssm_norm_w=d_norm_w, post_mix_w=d_post_mix_w,
                 pre_ffn_w=d_pre_ffn_w, ffn_conv_b=d_ffn_cb, post_ffn_w=d_post_ffn_w)
    row = jnp.concatenate([loss_part] + [row_g[k] for k, _, _ in ROW_PARAMS], axis=1)
    row_all, rb_all = all_gather([row, d_rel_bias], "gather_small")

    wmv = lambda k: (weights[k], mom_m[k], mom_v[k])
    small = adamw_small(row_all, rb_all, wmv("rel_bias"), [wmv(k) for k, _, _ in ROW_PARAMS])
    loss = small[0][0, 0]
    res = {k: tuple(small[1 + 4 * i:5 + 4 * i]) for i, k in enumerate(["rel_bias"] + [k for k, _, _ in ROW_PARAMS])}
    big = list(zip(late, chip_parts)) + [("w_down", p_w_down), ("w_up", p_w_up), ("w_out", p_w_out),
                                        ("w_ada", g_w_ada[None])]
    for k, parts in big:
        res[k] = tuple(o[None] for o in adamw(parts, weights[k][0], mom_m[k][0], mom_v[k][0], "adamw_" + k))

    outs = [loss, grad_x[None]]
    for field in range(4):
        outs += [res[k][field] for k in order]
    return tuple(outs)
```

```python
import math

import numpy as np
import jax
import jax.numpy as jnp
from jax import lax
from jax.experimental import pallas as pl
from jax.experimental.pallas import tpu as pltpu

F32 = jnp.float32
BF16 = jnp.bfloat16
MESH_ID = pl.DeviceIdType.MESH

N_DEV = 8
D_MODEL = 1024
N_Q_HEADS = 8
N_KV_HEADS = 2
HEAD_DIM = 64
ATTN_WIDTH = 512
KV_WIDTH = 128
ATTN_BLOCK = 128
N_BUCKETS = 32
REL_MAX_DIST = 128
SSM_HEADS = 8
SSM_HEAD_DIM = 64
SSM_WIDTH = 512
SSM_STATE = 128
SSM_GROUPS = 2
SSM_BC = 256
SSM_CONV = 4
SSM_CHUNK = 256
XBC_WIDTH = SSM_WIDTH + 2 * SSM_BC
D_FF = 2816
FFN_CONV = 3
NORM_EPS = 1e-6
N_MOD = 6
IN_PROJ_WIDTH = 2312
QKV_W = ATTN_WIDTH + 2 * KV_WIDTH
OFF_XBC = QKV_W
OFF_Z = OFF_XBC + XBC_WIDTH
OFF_DT = OFF_Z + SSM_WIDTH
DT_PAD = 128
PROJ_PAD = OFF_DT + DT_PAD
FF_CHUNK = 1408

ADAM_LR = 0.001
ADAM_B1 = 0.9
ADAM_B2 = 0.999
ADAM_EPS = 1e-08
ADAM_WD = 0.01
ADAM_STEP = 10

TOKEN_TILE = 256
HALO = 8
NEXT = 16
VMEM_LIMIT = 56 * 1024 * 1024


def _params(vmem=VMEM_LIMIT, n_axes=1):
    return pltpu.CompilerParams(dimension_semantics=("arbitrary",) * n_axes, vmem_limit_bytes=vmem)


def _b(x):
    return x.astype(BF16)


def _nn(a, b):
    return jnp.dot(_b(a), _b(b), preferred_element_type=F32)


def _nt(a, b):
    return lax.dot_general(_b(a), _b(b), (((1,), (1,)), ((), ())), preferred_element_type=F32)


def _tn(a, b):
    return lax.dot_general(_b(a), _b(b), (((0,), (0,)), ((), ())), preferred_element_type=F32)


@jax.custom_vjp
def mm(a, b):
    return _nn(a, b)


mm.defvjp(lambda a, b: (_nn(a, b), (a, b)),
          lambda r, g: (_nt(g, r[1]).astype(r[0].dtype), _tn(r[0], g).astype(r[1].dtype)))


@jax.custom_vjp
def mm_nt(a, b):
    return _nt(a, b)


mm_nt.defvjp(lambda a, b: (_nt(a, b), (a, b)),
             lambda r, g: (_nn(g, r[1]).astype(r[0].dtype), _tn(g, r[0]).astype(r[1].dtype)))


@jax.custom_vjp
def mm_tn(a, b):
    return _tn(a, b)


mm_tn.defvjp(lambda a, b: (_tn(a, b), (a, b)),
             lambda r, g: (_nt(r[1], g).astype(r[0].dtype), _nn(r[0], g).astype(r[1].dtype)))


def _rms(x, w):
    return x * lax.rsqrt(jnp.mean(x * x, axis=-1, keepdims=True) + NORM_EPS) * w


def _norm_mod(x, w, scale, shift):
    return _rms(x, w) * (1.0 + scale) + shift


def _rms_bwd(x, w, dy):
    r = lax.rsqrt(jnp.mean(x * x, axis=-1, keepdims=True) + NORM_EPS)
    xhat = x * r
    g = dy * w
    dx = r * (g - xhat * jnp.mean(g * xhat, axis=-1, keepdims=True))
    return dx, jnp.sum(dy * xhat, axis=0, keepdims=True)


def _norm_mod_bwd(x, w, scale, dh):
    dx, da = _rms_bwd(x, w * (1.0 + scale), dh)
    return dx, da * (1.0 + scale), da * w, jnp.sum(dh, axis=0, keepdims=True)


def _gated_rms_bwd(m, gate, w, dy):
    dm, t = _rms_bwd(m, w * gate, dy)
    return dm, t * w, t * gate


def _silu(x):
    return x * jax.nn.sigmoid(x)


def _conv_rows(xin, w, k):
    acc = xin * w[k - 1:k, :]
    for j in range(1, k):
        acc = acc + pltpu.roll(xin, j, axis=0) * w[k - 1 - j:k - j, :]
    return acc


def _conv_rows_t(du, w, k):
    n = du.shape[0]
    acc = du * w[k - 1:k, :]
    for j in range(1, k):
        acc = acc + pltpu.roll(du, n - j, axis=0) * w[k - 1 - j:k - j, :]
    return acc


def _row(i):
    return (i, 0)


def _const(i):
    return (0, 0)


def _vec(n):
    return pl.BlockSpec((1, n), _const)


def _block_index(p):
    return 4 * p[0] + 2 * p[1] + p[2]


def all_gather(arrs, name):
    n = len(arrs)

    def body(*refs):
        ins, outs = refs[:n], refs[n:2 * n]
        send_sems, recv_sems, local_sems = refs[2 * n:]
        x, y, c = lax.axis_index("x"), lax.axis_index("y"), lax.axis_index("c")
        me, sibling = (x, y, c), (x, y, 1 - c)
        chips = [(1 - x, y), (x, 1 - y), (1 - x, 1 - y)]

        def copy(a, k, block, to, src=None):
            dst = outs[a].at[_block_index(block)]
            return pltpu.make_async_remote_copy(
                src_ref=dst if src is None else src, dst_ref=dst,
                send_sem=send_sems.at[a * 7 + k], recv_sem=recv_sems.at[a * 7 + k],
                device_id=to, device_id_type=MESH_ID)

        mine = [pltpu.make_async_copy(ins[a], outs[a].at[_block_index(me)], local_sems.at[a]) for a in range(n)]
        for cp in mine:
            cp.start()
        first = []
        for a in range(n):
            first.append(copy(a, 0, me, sibling, src=ins[a]))
            first += [copy(a, 1 + j, me, (*chip, c), src=ins[a]) for j, chip in enumerate(chips)]
        for cp in first:
            cp.start()
        passed = []
        for j, chip in enumerate(chips):
            for a in range(n):
                copy(a, 1 + j, (*chip, c), me).wait_recv()
                cp = copy(a, 4 + j, (*chip, c), sibling)
                cp.start()
                passed.append(cp)
        for a in range(n):
            copy(a, 0, sibling, me).wait_recv()
            for j, chip in enumerate(chips):
                copy(a, 4 + j, (*chip, 1 - c), me).wait_recv()
        for cp in first + passed:
            cp.wait_send()
        for cp in mine:
            cp.wait()

    any_spec = pl.BlockSpec(memory_space=pl.ANY)
    return pl.pallas_call(
        body, name=name,
        out_shape=[jax.ShapeDtypeStruct((N_DEV,) + a.shape, a.dtype) for a in arrs],
        in_specs=[any_spec] * n, out_specs=[any_spec] * n,
        scratch_shapes=[pltpu.SemaphoreType.DMA((7 * n,)), pltpu.SemaphoreType.DMA((7 * n,)),
                        pltpu.SemaphoreType.DMA((n,))],
    )(*arrs)


ALL_FLIPS = ((0, 0, 1), (0, 1, 0), (0, 1, 1), (1, 0, 0), (1, 0, 1), (1, 1, 0), (1, 1, 1))
CHIP_FLIPS = ((0, 1, 0), (1, 0, 0), (1, 1, 0))


def _chip_index(p):
    return 2 * p[0] + p[1]


def all_to_all(arrs, name, flips=ALL_FLIPS, index=_block_index):
    n = len(arrs)
    nf = len(flips)

    def body(*refs):
        ins, outs = refs[:n], refs[n:2 * n]
        send_sems, recv_sems, local_sems = refs[2 * n:]
        pos = (lax.axis_index("x"), lax.axis_index("y"), lax.axis_index("c"))
        me = index(pos)
        peers = [tuple(1 - p if f else p for p, f in zip(pos, flip)) for flip in flips]

        def copy(a, k):
            peer = peers[k]
            return pltpu.make_async_remote_copy(
                src_ref=ins[a].at[index(peer)], dst_ref=outs[a].at[me],
                send_sem=send_sems.at[a * nf + k], recv_sem=recv_sems.at[a * nf + k],
                device_id=peer, device_id_type=MESH_ID)

        def landed(a, k):
            slot = outs[a].at[index(peers[k])]
            return pltpu.make_async_remote_copy(
                src_ref=slot, dst_ref=slot,
                send_sem=send_sems.at[a * nf + k], recv_sem=recv_sems.at[a * nf + k],
                device_id=peers[k], device_id_type=MESH_ID)

        mine = [pltpu.make_async_copy(ins[a].at[me], outs[a].at[me], local_sems.at[a]) for a in range(n)]
        for cp in mine:
            cp.start()
        sent = [copy(a, k) for a in range(n) for k in range(nf)]
        for cp in sent:
            cp.start()
        for a in range(n):
            for k in range(nf):
                landed(a, k).wait_recv()
        for cp in sent:
            cp.wait_send()
        for cp in mine:
            cp.wait()

    any_spec = pl.BlockSpec(memory_space=pl.ANY)
    return pl.pallas_call(
        body, name=name,
        out_shape=[jax.ShapeDtypeStruct(a.shape, a.dtype) for a in arrs],
        in_specs=[any_spec] * n, out_specs=[any_spec] * n,
        scratch_shapes=[pltpu.SemaphoreType.DMA((nf * n,)), pltpu.SemaphoreType.DMA((nf * n,)),
                        pltpu.SemaphoreType.DMA((n,))],
    )(*arrs)


def _direct_exchange(src, dst, sems, scatter):
    send_sems, recv_sems, local_sem = sems
    pos = (lax.axis_index("x"), lax.axis_index("y"), lax.axis_index("c"))
    me = _block_index(pos)
    peers = [tuple(1 - p if f else p for p, f in zip(pos, flip)) for flip in ALL_FLIPS]

    def outgoing(k):
        return pltpu.make_async_remote_copy(
            src_ref=src.at[_block_index(peers[k])] if scatter else src, dst_ref=dst.at[me],
            send_sem=send_sems.at[k], recv_sem=recv_sems.at[k], device_id=peers[k], device_id_type=MESH_ID)

    def incoming(k):
        slot = dst.at[_block_index(peers[k])]
        return pltpu.make_async_remote_copy(
            src_ref=slot, dst_ref=slot, send_sem=send_sems.at[k], recv_sem=recv_sems.at[k],
            device_id=peers[k], device_id_type=MESH_ID)

    def local():
        return pltpu.make_async_copy(src.at[me] if scatter else src, dst.at[me], local_sem)

    def start():
        local().start()
        for k in range(len(ALL_FLIPS)):
            outgoing(k).start()

    def finish():
        for k in range(len(ALL_FLIPS)):
            incoming(k).wait_recv()
        for k in range(len(ALL_FLIPS)):
            outgoing(k).wait_send()
        local().wait()

    return start, finish


def hosted_call(body, exchanges, steps, n_in, n_out, **call):
    n_ex = len(exchanges)

    def wrapped(*refs):
        ins, srcs = refs[:n_in], refs[n_in:n_in + n_ex]
        outs = refs[n_in + n_ex:n_in + n_ex + n_out]
        dsts = refs[n_in + n_ex + n_out:n_in + 2 * n_ex + n_out]
        rest = refs[n_in + 2 * n_ex + n_out:]
        scratch, sems = rest[:len(rest) - 3 * n_ex], rest[len(rest) - 3 * n_ex:]
        plans = [_direct_exchange(srcs[e], dsts[e], sems[3 * e:3 * e + 3], exchanges[e][1]) for e in range(n_ex)]

        @pl.when(pl.program_id(0) == 0)
        def _():
            for start, _ in plans:
                start()

        body(*ins, *outs, *scratch)

        @pl.when(pl.program_id(0) == steps - 1)
        def _():
            for _, finish in plans:
                finish()

    any_spec = pl.BlockSpec(memory_space=pl.ANY)
    landings = [jax.ShapeDtypeStruct(src.shape if scatter else (N_DEV,) + src.shape, src.dtype)
                for src, scatter in exchanges]
    n_flips = len(ALL_FLIPS)
    sems = [pltpu.SemaphoreType.DMA((n_flips,)), pltpu.SemaphoreType.DMA((n_flips,)), pltpu.SemaphoreType.DMA(())]
    return pl.pallas_call(
        wrapped, grid=(steps,),
        in_specs=list(call.pop("in_specs")) + [any_spec] * n_ex,
        out_specs=list(call.pop("out_specs")) + [any_spec] * n_ex,
        out_shape=list(call.pop("out_shape")) + landings,
        scratch_shapes=list(call.pop("scratch_shapes", [])) + sems * n_ex,
        **call)


def _grid_call(body, steps, args, exchanges, **call):
    if not exchanges:
        return pl.pallas_call(body, grid=(steps,), **call)(*args)
    srcs = [src for src, _ in exchanges]
    return hosted_call(body, exchanges, steps, len(args), len(call["out_shape"]), **call)(*args, *srcs)


N_CHIPS = 4


def pair_exchange(arrs, name):
    n = len(arrs)

    def body(*refs):
        ins, outs = refs[:n], refs[n:2 * n]
        send_sems, recv_sems = refs[2 * n:]
        x, y, c = lax.axis_index("x"), lax.axis_index("y"), lax.axis_index("c")
        sibling = (x, y, 1 - c)
        sent = []
        for a in range(n):
            for q in range(N_CHIPS):
                cp = pltpu.make_async_remote_copy(
                    src_ref=ins[a].at[2 * q + (1 - c)], dst_ref=outs[a].at[q],
                    send_sem=send_sems.at[a * N_CHIPS + q], recv_sem=recv_sems.at[a * N_CHIPS + q],
                    device_id=sibling, device_id_type=MESH_ID)
                cp.start()
                sent.append(cp)
        for cp in sent:
            cp.wait_recv()
        for cp in sent:
            cp.wait_send()

    any_spec = pl.BlockSpec(memory_space=pl.ANY)
    return pl.pallas_call(
        body, name=name,
        out_shape=[jax.ShapeDtypeStruct((N_CHIPS,) + a.shape[1:], a.dtype) for a in arrs],
        in_specs=[any_spec] * n, out_specs=[any_spec] * n,
        scratch_shapes=[pltpu.SemaphoreType.DMA((N_CHIPS * n,)), pltpu.SemaphoreType.DMA((N_CHIPS * n,))],
    )(*arrs)


def pair_sum(full, got, core, name):
    _, r, n = full.shape
    tr = _row_tile(r)

    def body(c_ref, mine_ref, got_ref, o_ref):
        o_ref[...] = _b(mine_ref[...] + got_ref[...])

    grid_spec = pltpu.PrefetchScalarGridSpec(
        num_scalar_prefetch=1, grid=(N_CHIPS, r // tr),
        in_specs=[pl.BlockSpec((1, tr, n), lambda q, i, c_ref: (2 * q + c_ref[0], i, 0)),
                  pl.BlockSpec((1, tr, n), lambda q, i, c_ref: (q, i, 0))],
        out_specs=pl.BlockSpec((1, tr, n), lambda q, i, c_ref: (q, i, 0)))
    return pl.pallas_call(body, name=name, grid_spec=grid_spec,
                          out_shape=jax.ShapeDtypeStruct((N_CHIPS, r, n), BF16),
                          compiler_params=_params(n_axes=2))(core, full, got)


def _row_tile(r):
    for cand in (256, 128, 64, 32, 16):
        if r % cand == 0 and r > cand:
            return cand
    return r


def ada_fwd(c_all, w_ada, b_cols):
    def body(c_ref, w_ref, b_ref, o_ref):
        o_ref[...] = _nn(_silu(c_ref[...]), w_ref[...]) + b_ref[...]

    return pl.pallas_call(body, name="ada_fwd",
                          out_shape=jax.ShapeDtypeStruct((N_DEV, w_ada.shape[1]), F32),
                          compiler_params=_params(n_axes=0))(c_all, w_ada, b_cols)


def ada_bwd(c_all, g_cols):
    def body(c_ref, g_ref, o_ref):
        o_ref[...] = _tn(_silu(c_ref[...]), g_ref[...])

    return pl.pallas_call(body, name="ada_bwd",
                          out_shape=jax.ShapeDtypeStruct((c_all.shape[1], g_cols.shape[1]), F32),
                          compiler_params=_params(n_axes=0))(c_all, g_cols)


def matmul_tn(a, b, name, bm, bn, tk=512):
    s, m = a.shape
    n = b.shape[1]
    tk = min(tk, s)

    def body(a_ref, b_ref, o_ref):
        @pl.when(pl.program_id(2) == 0)
        def _():
            o_ref[...] = jnp.zeros_like(o_ref)

        o_ref[...] += _tn(a_ref[...], b_ref[...])

    return pl.pallas_call(
        body, name=name, grid=(m // bm, n // bn, s // tk),
        in_specs=[pl.BlockSpec((tk, bm), lambda i, j, k: (k, i)), pl.BlockSpec((tk, bn), lambda i, j, k: (k, j))],
        out_specs=pl.BlockSpec((bm, bn), lambda i, j, k: (i, j)),
        out_shape=jax.ShapeDtypeStruct((m, n), F32),
        compiler_params=_params(n_axes=3))(a, b)


def pre_mix_inproj(x, w, scale, shift, w_in16, exchange=None, tm=4 * TOKEN_TILE):
    s = x.shape[0]
    tm = min(tm, s)

    def body(x_ref, w_ref, sc_ref, sh_ref, win_ref, h_ref, qkv_ref, xbc_ref, z_ref, dt_ref):
        h16 = _b(_norm_mod(x_ref[...], w_ref[...], sc_ref[...], sh_ref[...]))
        h_ref[...] = h16
        dot = lambda lo, hi: jnp.dot(h16, win_ref[:, lo:hi], preferred_element_type=F32)
        qkv_ref[...] = _b(dot(0, OFF_XBC))
        xbc_ref[...] = dot(OFF_XBC, OFF_Z)
        z_ref[...] = dot(OFF_Z, OFF_DT)
        dt_ref[...] = dot(OFF_DT, PROJ_PAD)

    tile = lambda n: pl.BlockSpec((tm, n), _row)
    return _grid_call(
        body, s // tm, (x, w, scale, shift, w_in16), exchange, name="pre_mix_inproj",
        in_specs=[tile(D_MODEL), _vec(D_MODEL), _vec(D_MODEL), _vec(D_MODEL), pl.BlockSpec((D_MODEL, PROJ_PAD), _const)],
        out_specs=[tile(D_MODEL), tile(QKV_W), tile(XBC_WIDTH), tile(SSM_WIDTH), tile(DT_PAD)],
        out_shape=[jax.ShapeDtypeStruct((s, D_MODEL), BF16), jax.ShapeDtypeStruct((s, QKV_W), BF16),
                   jax.ShapeDtypeStruct((s, XBC_WIDTH), F32), jax.ShapeDtypeStruct((s, SSM_WIDTH), F32),
                   jax.ShapeDtypeStruct((s, DT_PAD), F32)],
        compiler_params=_params())


ATTN_QB_FWD, ATTN_QB_BWD = 4, 2


def _attn_tile(q, kp, kc, vp, vc, bias, sinks):
    lq = ATTN_BLOCK
    group = N_Q_HEADS // N_KV_HEADS
    lanes = lax.broadcasted_iota(jnp.int32, (1, 128), 1)
    rid = lax.broadcasted_iota(jnp.int32, (group * lq, 1), 0)
    sink_cols = []
    for hk in range(N_KV_HEADS):
        sink = jnp.zeros((group * lq, 1), F32)
        for g in range(group):
            s_h = jnp.sum(jnp.where(lanes == hk * group + g, sinks, 0.0), axis=-1, keepdims=True)
            sink = jnp.where((rid >= g * lq) & (rid < (g + 1) * lq), s_h, sink)
        sink_cols.append(sink)
    kall = jnp.concatenate([kp, kc], axis=0)
    vall = jnp.concatenate([vp, vc], axis=0)
    blocks = []
    for b in range(q.shape[0] // lq):
        qb = q[b * lq:(b + 1) * lq]
        outs = []
        for hk in range(N_KV_HEADS):
            cols = slice(hk * HEAD_DIM, (hk + 1) * HEAD_DIM)
            kb = kall[b * lq:(b + 2) * lq, cols]
            vb = vall[b * lq:(b + 2) * lq, cols]
            qg = jnp.concatenate([qb[:, (hk * group + g) * HEAD_DIM:(hk * group + g + 1) * HEAD_DIM]
                                  for g in range(group)], axis=0)
            sc = mm_nt(qg, kb) * (HEAD_DIM ** -0.5) + bias[b][hk]
            sink = sink_cols[hk]
            m = lax.stop_gradient(jnp.maximum(jnp.max(sc, axis=-1, keepdims=True), sink))
            p = jnp.exp(sc - m)
            probs = p / (jnp.sum(p, axis=-1, keepdims=True) + jnp.exp(sink - m))
            og = mm(probs, vb)
            outs += [og[g * lq:(g + 1) * lq] for g in range(group)]
        blocks.append(jnp.concatenate(outs, axis=1))
    return jnp.concatenate(blocks, axis=0)


def _attn_tile_bwd(q, kp, kc, vp, vc, bias, sinks, do):
    lq = ATTN_BLOCK
    group = N_Q_HEADS // N_KV_HEADS
    scale = HEAD_DIM ** -0.5
    lanes = lax.broadcasted_iota(jnp.int32, (1, 128), 1)
    rid = lax.broadcasted_iota(jnp.int32, (group * lq, 1), 0)
    sink_cols = []
    for hk in range(N_KV_HEADS):
        sink = jnp.zeros((group * lq, 1), F32)
        for g in range(group):
            s_h = jnp.sum(jnp.where(lanes == hk * group + g, sinks, 0.0), axis=-1, keepdims=True)
            sink = jnp.where((rid >= g * lq) & (rid < (g + 1) * lq), s_h, sink)
        sink_cols.append(sink)
    kall = jnp.concatenate([kp, kc], axis=0)
    vall = jnp.concatenate([vp, vc], axis=0)
    dsk = jnp.zeros((1, 128), F32)
    dq_blocks, dbias = [], []
    nqb = q.shape[0] // lq
    dk_parts = [[None] * nqb for _ in range(N_KV_HEADS)]
    dv_parts = [[None] * nqb for _ in range(N_KV_HEADS)]
    for b in range(nqb):
        qb, dob = q[b * lq:(b + 1) * lq], do[b * lq:(b + 1) * lq]
        dq_heads, dbias_b = [], []
        for hk in range(N_KV_HEADS):
            cols = slice(hk * HEAD_DIM, (hk + 1) * HEAD_DIM)
            kb = kall[b * lq:(b + 2) * lq, cols]
            vb = vall[b * lq:(b + 2) * lq, cols]
            heads = [hk * group + g for g in range(group)]
            qg = jnp.concatenate([qb[:, h * HEAD_DIM:(h + 1) * HEAD_DIM] for h in heads], axis=0)
            dog = jnp.concatenate([dob[:, h * HEAD_DIM:(h + 1) * HEAD_DIM] for h in heads], axis=0)
            sink = sink_cols[hk]
            sc = _nt(qg, kb) * scale + bias[b][hk]
            m = jnp.maximum(jnp.max(sc, axis=-1, keepdims=True), sink)
            p = jnp.exp(sc - m)
            es = jnp.exp(sink - m)
            inv = 1.0 / (jnp.sum(p, axis=-1, keepdims=True) + es)
            probs = p * inv
            dprobs = _nt(dog, vb)
            delta = jnp.sum(probs * dprobs, axis=-1, keepdims=True)
            dsc = probs * (dprobs - delta)
            dbias_b.append(dsc)
            dsink = -(es * inv) * delta
            for g, h in enumerate(heads):
                tot = jnp.sum(dsink[g * lq:(g + 1) * lq], axis=0, keepdims=True)
                dsk = dsk + jnp.where(lanes == h, tot, 0.0)
            dqg = _nn(dsc, kb) * scale
            dq_heads += [dqg[g * lq:(g + 1) * lq] for g in range(group)]
            dk_parts[hk][b] = _tn(dsc, qg) * scale
            dv_parts[hk][b] = _tn(probs, dog)
        dq_blocks.append(jnp.concatenate(dq_heads, axis=1))
        dbias.append(dbias_b)

    def overlap_add(parts):
        chunks = []
        for r in range(nqb + 1):
            acc = None
            if r < nqb:
                acc = parts[r][:lq]
            if r >= 1:
                tail = parts[r - 1][lq:]
                acc = tail if acc is None else acc + tail
            chunks.append(acc)
        return jnp.concatenate(chunks, axis=0)

    dkall = jnp.concatenate([overlap_add(dk_parts[hk]) for hk in range(N_KV_HEADS)], axis=1)
    dvall = jnp.concatenate([overlap_add(dv_parts[hk]) for hk in range(N_KV_HEADS)], axis=1)
    return jnp.concatenate(dq_blocks, axis=0), dkall, dvall, dbias, dsk


def _attn_in_specs(nt, clamp, nqb):
    lq, tq = ATTN_BLOCK, ATTN_BLOCK * nqb
    cur = lambda n: jnp.minimum(n, nt - 1) if clamp else n
    prev = lambda n: jnp.maximum(cur(n) * nqb - 1, 0)
    kcol, vcol = ATTN_WIDTH // KV_WIDTH, ATTN_WIDTH // KV_WIDTH + 1
    return [pl.BlockSpec((tq, ATTN_WIDTH), lambda n: (cur(n), 0)),
            pl.BlockSpec((lq, KV_WIDTH), lambda n: (prev(n), kcol)),
            pl.BlockSpec((tq, KV_WIDTH), lambda n: (cur(n), kcol)),
            pl.BlockSpec((lq, KV_WIDTH), lambda n: (prev(n), vcol)),
            pl.BlockSpec((tq, KV_WIDTH), lambda n: (cur(n), vcol)),
            pl.BlockSpec((2, N_KV_HEADS, 4 * lq, 2 * lq), lambda n: (0, 0, 0, 0)),
            _vec(128)]


def _tile_bias(bias_ref, first, nqb):
    return [[jnp.where(first, bias_ref[1, hk], bias_ref[0, hk]) if b == 0 else bias_ref[0, hk]
             for hk in range(N_KV_HEADS)] for b in range(nqb)]


def attn_fwd(qkv, bias, sinks_rows, exchange=None):
    s = qkv.shape[0]
    nqb = min(ATTN_QB_FWD, s // ATTN_BLOCK)
    tq = ATTN_BLOCK * nqb
    nt = s // tq

    def body(q_ref, kp_ref, kc_ref, vp_ref, vc_ref, bias_ref, sk_ref, o_ref):
        f = lambda r: r[...].astype(F32)
        o = _attn_tile(f(q_ref), f(kp_ref), f(kc_ref), f(vp_ref), f(vc_ref),
                       _tile_bias(bias_ref, pl.program_id(0) == 0, nqb), sk_ref[...])
        o_ref[...] = _b(o)

    return _grid_call(
        body, nt, (qkv, qkv, qkv, qkv, qkv, bias, sinks_rows), exchange, name="attn_fwd",
        in_specs=_attn_in_specs(nt, False, nqb),
        out_specs=[pl.BlockSpec((tq, ATTN_WIDTH), _row)],
        out_shape=[jax.ShapeDtypeStruct((s, ATTN_WIDTH), BF16)],
        compiler_params=_params())


def attn_bwd(qkv, bias, sinks_rows, d_attn, exchange=None):
    s = qkv.shape[0]
    nqb = ATTN_QB_BWD
    lq, tq = ATTN_BLOCK, ATTN_BLOCK * nqb
    nt = s // tq

    def body(q_ref, kp_ref, kc_ref, vp_ref, vc_ref, bias_ref, sk_ref, do_ref,
             dq_ref, dk_ref, dv_ref, dbias_ref, dsk_ref, carry_k, carry_v):
        n = pl.program_id(0)

        @pl.when(n == 0)
        def _():
            dbias_ref[...] = jnp.zeros_like(dbias_ref)
            dsk_ref[...] = jnp.zeros_like(dsk_ref)
            carry_k[...] = jnp.zeros_like(carry_k)
            carry_v[...] = jnp.zeros_like(carry_v)

        @pl.when(n < nt)
        def _():
            f = lambda r: r[...].astype(F32)
            dq, dkall, dvall, dbias, dsk = _attn_tile_bwd(
                f(q_ref), f(kp_ref), f(kc_ref), f(vp_ref), f(vc_ref), _tile_bias(bias_ref, n == 0, nqb), sk_ref[...],
                f(do_ref))
            dkp, dkc, dvp, dvc = dkall[:lq], dkall[lq:], dvall[:lq], dvall[lq:]
            dq_ref[...] = _b(dq)
            done = tq - lq
            dk_ref[:done, :] = _b(carry_k[:done, :])
            dv_ref[:done, :] = _b(carry_v[:done, :])
            dk_ref[done:, :] = _b(carry_k[done:, :] + dkp)
            dv_ref[done:, :] = _b(carry_v[done:, :] + dvp)
            carry_k[...] = dkc
            carry_v[...] = dvc
            dsk_ref[...] += dsk
            first = (n == 0).astype(F32)
            for hk in range(N_KV_HEADS):
                total = dbias[0][hk]
                for b in range(1, nqb):
                    total = total + dbias[b][hk]
                dbias_ref[0, hk] += total - first * dbias[0][hk]
                dbias_ref[1, hk] += first * dbias[0][hk]

        @pl.when(n == nt)
        def _():
            dk_ref[...] = _b(carry_k[...])
            dv_ref[...] = _b(carry_v[...])

    cur = lambda n: (jnp.minimum(n, nt - 1), 0)
    done_map = lambda n: (jnp.maximum(n - 1, 0), 0)
    return _grid_call(
        body, nt + 1, (qkv, qkv, qkv, qkv, qkv, bias, sinks_rows, d_attn), exchange, name="attn_bwd",
        in_specs=_attn_in_specs(nt, True, nqb) + [pl.BlockSpec((tq, ATTN_WIDTH), cur)],
        out_specs=[pl.BlockSpec((tq, ATTN_WIDTH), cur), pl.BlockSpec((tq, KV_WIDTH), done_map),
                   pl.BlockSpec((tq, KV_WIDTH), done_map),
                   pl.BlockSpec((2, N_KV_HEADS, 4 * lq, 2 * lq), lambda n: (0, 0, 0, 0)), _vec(128)],
        out_shape=[jax.ShapeDtypeStruct((s, ATTN_WIDTH), BF16), jax.ShapeDtypeStruct((s, KV_WIDTH), BF16),
                   jax.ShapeDtypeStruct((s, KV_WIDTH), BF16),
                   jax.ShapeDtypeStruct((2, N_KV_HEADS, 4 * lq, 2 * lq), F32), jax.ShapeDtypeStruct((1, 128), F32)],
        scratch_shapes=[pltpu.VMEM((tq, KV_WIDTH), F32), pltpu.VMEM((tq, KV_WIDTH), F32)],
        compiler_params=_params())


def rel_bias_table(rel_bias, bucket):
    lq = ATTN_BLOCK
    group = N_Q_HEADS // N_KV_HEADS

    def body(rb_ref, bk_ref, o_ref):
        bk = bk_ref[...]
        prev_keys = lax.broadcasted_iota(jnp.int32, bk.shape, 1) < lq
        accs = [jnp.full(bk.shape, -1e30, F32) for _ in range(N_Q_HEADS)]
        for b in range(N_BUCKETS):
            hit = bk == b
            accs = [jnp.where(hit, rb_ref[b, h], acc) for h, acc in enumerate(accs)]
        for h in range(N_Q_HEADS):
            rows = slice((h % group) * lq, (h % group + 1) * lq)
            o_ref[0, h // group, rows, :] = accs[h]
            o_ref[1, h // group, rows, :] = jnp.where(prev_keys, -1e30, accs[h])

    return pl.pallas_call(
        body, name="rel_bias_table",
        in_specs=[pl.BlockSpec(memory_space=pltpu.SMEM), pl.BlockSpec(memory_space=pltpu.VMEM)],
        out_shape=jax.ShapeDtypeStruct((2, N_KV_HEADS, group * lq, 2 * lq), F32),
        compiler_params=_params(n_axes=0))(rel_bias, bucket)


def rel_bias_grad(dbias, bucket):
    lq = ATTN_BLOCK
    group = N_Q_HEADS // N_KV_HEADS

    def body(db_ref, bk_ref, o_ref):
        rows = lax.broadcasted_iota(jnp.int32, (N_BUCKETS, 128), 0)
        lanes = lax.broadcasted_iota(jnp.int32, (N_BUCKETS, 128), 1)
        bk = bk_ref[...]
        per_head = []
        for h in range(N_Q_HEADS):
            sl = slice((h % group) * lq, (h % group + 1) * lq)
            per_head.append(db_ref[0, h // group, sl, :] + db_ref[1, h // group, sl, :])

        def per_bucket(b, acc):
            hit = (bk == b).astype(F32)
            for h in range(N_Q_HEADS):
                val = jnp.sum(per_head[h] * hit, keepdims=True)
                acc = acc + jnp.where((rows == b) & (lanes == h), val, 0.0)
            return acc

        o_ref[...] = lax.fori_loop(0, N_BUCKETS, per_bucket, jnp.zeros((N_BUCKETS, 128), F32))

    return pl.pallas_call(body, name="rel_bias_grad", out_shape=jax.ShapeDtypeStruct((N_BUCKETS, 128), F32),
                          compiler_params=_params(n_axes=0))(dbias, bucket)


def _tri_sum(a, upper):
    n = a.shape[0]
    ri = lax.broadcasted_iota(jnp.int32, (n, n), 0)
    ci = lax.broadcasted_iota(jnp.int32, (n, n), 1)
    tri = ((ri <= ci) if upper else (ri >= ci)).astype(BF16)
    hi = a.astype(BF16)
    rest = a - hi.astype(F32)
    mid = rest.astype(BF16)
    lo = (rest - mid.astype(F32)).astype(BF16)
    dot = lambda part: jnp.dot(tri, part, preferred_element_type=F32)
    return dot(hi) + dot(mid) + dot(lo)


@jax.custom_vjp
def _cumsum_rows(a):
    return _tri_sum(a, False)


_cumsum_rows.defvjp(lambda a: (_tri_sum(a, False), None), lambda _, g: (_tri_sum(g, True),))


def _ssm_core(u, z, dt_raw, hprev, dt_bias, a_log, dskip, norm_w):
    lc = u.shape[0]
    xbc = _silu(u)
    xs, bm, cm = xbc[:, :SSM_WIDTH], xbc[:, SSM_WIDTH:SSM_WIDTH + SSM_BC], xbc[:, SSM_WIDTH + SSM_BC:]
    dt = jax.nn.softplus(dt_raw + dt_bias)
    adt = dt * (-jnp.exp(a_log))
    ri = lax.broadcasted_iota(jnp.int32, (lc, lc), 0)
    ci = lax.broadcasted_iota(jnp.int32, (lc, lc), 1)
    causal = ri >= ci
    acum = _cumsum_rows(adt)
    acum_t = acum.T
    last = acum[lc - 1:lc, :]
    per_group = SSM_HEADS // SSM_GROUPS
    lane = lax.broadcasted_iota(jnp.int32, (1, 128), 1)
    rowid = lax.broadcasted_iota(jnp.int32, (128, 1), 0)
    lo_lanes = lane < SSM_HEAD_DIM
    ys, hs = [], []
    for g in range(SSM_GROUPS):
        bg = bm[:, g * SSM_STATE:(g + 1) * SSM_STATE]
        cg = cm[:, g * SSM_STATE:(g + 1) * SSM_STATE]
        cb = mm_nt(cg, bg)
        for pp in range(per_group // 2):
            ha = g * per_group + 2 * pp
            xp = xs[:, ha * SSM_HEAD_DIM:(ha + 2) * SSM_HEAD_DIM]
            hp = hprev[ha * SSM_HEAD_DIM:(ha + 2) * SSM_HEAD_DIM, :]
            xcp = xp * jnp.where(lo_lanes, dt[:, ha:ha + 1], dt[:, ha + 1:ha + 2])
            y_h, st_h = [], []
            for h in (ha, ha + 1):
                col, rowv, lasth = acum[:, h:h + 1], acum_t[h:h + 1, :], last[:, h:h + 1]
                decay = jnp.exp(jnp.where(causal, col - rowv, -1e30))
                y_h.append(mm(cb * decay, xcp) + mm_nt(cg * jnp.exp(col), hp))
                st_h.append(mm_tn(xcp, bg * jnp.exp(lasth - col)))
            y_pair = jnp.where(lo_lanes, y_h[0], y_h[1])
            st_pair = jnp.where(rowid < SSM_HEAD_DIM, st_h[0], st_h[1])
            la, lb = last[:, ha:ha + 1], last[:, ha + 1:ha + 2]
            hs.append(jnp.exp(jnp.where(rowid < SSM_HEAD_DIM, la, lb)) * hp + st_pair)
            dsk = jnp.where(lo_lanes, dskip[:, ha:ha + 1], dskip[:, ha + 1:ha + 2])
            ys.append(y_pair + dsk * xp)
    y = jnp.concatenate(ys, axis=1) * _silu(z)
    gw = SSM_WIDTH // SSM_GROUPS
    outs = []
    for g in range(SSM_GROUPS):
        yg = y[:, g * gw:(g + 1) * gw]
        outs.append(yg * lax.rsqrt(jnp.mean(yg * yg, axis=-1, keepdims=True) + NORM_EPS))
    return jnp.concatenate(outs, axis=1) * norm_w, jnp.concatenate(hs, axis=0)


def _ssm_param_specs():
    return [pl.BlockSpec((SSM_CONV, XBC_WIDTH), _const), _vec(XBC_WIDTH), _vec(128), _vec(128), _vec(128),
            _vec(SSM_WIDTH)]


SSM_FWD_SUB = 2
SSM_BWD_SUB = 1


def ssm_fwd(xbc_raw, z, dt_raw, conv_w, conv_b, dt_bias, a_log, dskip, norm_w, exchange=None):
    s = xbc_raw.shape[0]
    lc = SSM_CHUNK
    lt = lc * SSM_FWD_SUB
    hrows = SSM_HEADS * SSM_HEAD_DIM

    def body(x_ref, halo_ref, z_ref, dt_ref, cw_ref, cb_ref, dtb_ref, al_ref, dk_ref, nw_ref,
             o_ref, hp_ref, state):
        i = pl.program_id(0)

        @pl.when(i == 0)
        def _():
            state[...] = jnp.zeros_like(state)

        halo = halo_ref[...] * (i > 0).astype(F32)
        xin = jnp.concatenate([halo, x_ref[...]], axis=0)
        u = (_conv_rows(xin, cw_ref[...], SSM_CONV) + cb_ref[...])[HALO:]
        h = state[...]
        for k in range(SSM_FWD_SUB):
            rows = slice(k * lc, (k + 1) * lc)
            hp_ref[k * hrows:(k + 1) * hrows, :] = h
            out, h = _ssm_core(u[rows], z_ref[rows, :], dt_ref[rows, :], h, dtb_ref[...], al_ref[...], dk_ref[...],
                               nw_ref[...])
            o_ref[rows, :] = _b(out)
        state[...] = h

    tile = lambda n: pl.BlockSpec((lt, n), _row)
    halo_spec = pl.BlockSpec((HALO, XBC_WIDTH), lambda i: (jnp.maximum(i * (lt // HALO) - 1, 0), 0))
    return _grid_call(
        body, s // lt, (xbc_raw, xbc_raw, z, dt_raw, conv_w, conv_b, dt_bias, a_log, dskip, norm_w), exchange,
        name="ssm_fwd",
        in_specs=[tile(XBC_WIDTH), halo_spec, tile(SSM_WIDTH), tile(DT_PAD)] + _ssm_param_specs(),
        out_specs=[tile(SSM_WIDTH), pl.BlockSpec((SSM_FWD_SUB * hrows, SSM_STATE), _row)],
        out_shape=[jax.ShapeDtypeStruct((s, SSM_WIDTH), BF16),
                   jax.ShapeDtypeStruct((s // lc * hrows, SSM_STATE), F32)],
        scratch_shapes=[pltpu.VMEM((hrows, SSM_STATE), F32)],
        compiler_params=_params())


def ssm_bwd(xbc_raw, z, dt_raw, hprev_all, d_out, conv_w, conv_b, dt_bias, a_log, dskip, norm_w, exchange=None):
    s = xbc_raw.shape[0]
    lc = SSM_CHUNK
    sub = SSM_BWD_SUB
    lt = lc * sub
    nt = s // lt
    hrows = SSM_HEADS * SSM_HEAD_DIM

    def body(x_ref, halo_ref, z_ref, dt_ref, hp_ref, do_ref, cw_ref, cb_ref, dtb_ref, al_ref, dk_ref, nw_ref,
             dx_ref, dz_ref, ddt_ref, dcw_ref, dcb_ref, ddtb_ref, dal_ref, ddk_ref, dnw_ref, dstate, du_next):
        i = pl.program_id(0)
        tile_no = nt - 1 - i

        @pl.when(i == 0)
        def _():
            dstate[...] = jnp.zeros_like(dstate)
            du_next[...] = jnp.zeros_like(du_next)
            for r in (dcw_ref, dcb_ref, ddtb_ref, dal_ref, ddk_ref, dnw_ref):
                r[...] = jnp.zeros_like(r)

        halo = halo_ref[...] * (tile_no > 0).astype(F32)
        xin = jnp.concatenate([halo, x_ref[...]], axis=0)
        cw = cw_ref[...]
        u = (_conv_rows(xin, cw, SSM_CONV) + cb_ref[...])[HALO:]
        dh = dstate[...]
        dus = [None] * sub
        for k in reversed(range(sub)):
            rows = slice(k * lc, (k + 1) * lc)
            _, vjp = jax.vjp(_ssm_core, u[rows], z_ref[rows, :], dt_ref[rows, :], hp_ref[k * hrows:(k + 1) * hrows, :],
                             dtb_ref[...], al_ref[...], dk_ref[...], nw_ref[...])
            dus[k], dz, ddt, dh, ddtb, dal, ddk, dnw = vjp((do_ref[rows, :], dh))
            dz_ref[rows, :] = _b(dz)
            ddt_ref[rows, :] = _b(ddt)
            ddtb_ref[...] += ddtb
            dal_ref[...] += dal
            ddk_ref[...] += ddk
            dnw_ref[...] += dnw
        dstate[...] = dh
        du = jnp.concatenate(dus, axis=0)
        du_ext = jnp.concatenate([du, du_next[...]], axis=0)
        dx_ref[...] = _b(_conv_rows_t(du_ext, cw, SSM_CONV)[:lt])
        du_next[...] = du[:HALO]
        sums = [jnp.sum(du * pltpu.roll(xin, j, axis=0)[HALO:] if j else du * xin[HALO:], axis=0, keepdims=True)
                for j in range(SSM_CONV)]
        dcw_ref[...] += jnp.concatenate(sums[::-1] + [jnp.zeros((8 - SSM_CONV, XBC_WIDTH), F32)], axis=0)
        dcb_ref[...] += jnp.sum(du, axis=0, keepdims=True)

    rev = lambda i: (nt - 1 - i, 0)
    tile = lambda n: pl.BlockSpec((lt, n), rev)
    halo_spec = pl.BlockSpec((HALO, XBC_WIDTH), lambda i: (jnp.maximum((nt - 1 - i) * (lt // HALO) - 1, 0), 0))
    acc = lambda r, n: pl.BlockSpec((r, n), _const)
    return _grid_call(
        body, nt, (xbc_raw, xbc_raw, z, dt_raw, hprev_all, d_out, conv_w, conv_b, dt_bias, a_log, dskip, norm_w),
        exchange, name="ssm_bwd",
        in_specs=[tile(XBC_WIDTH), halo_spec, tile(SSM_WIDTH), tile(DT_PAD),
                  pl.BlockSpec((sub * hrows, SSM_STATE), rev), tile(SSM_WIDTH)] + _ssm_param_specs(),
        out_specs=[tile(XBC_WIDTH), tile(SSM_WIDTH), tile(DT_PAD), acc(8, XBC_WIDTH), acc(1, XBC_WIDTH),
                   acc(1, 128), acc(1, 128), acc(1, 128), acc(1, SSM_WIDTH)],
        out_shape=[jax.ShapeDtypeStruct((s, XBC_WIDTH), BF16), jax.ShapeDtypeStruct((s, SSM_WIDTH), BF16),
                   jax.ShapeDtypeStruct((s, DT_PAD), BF16), jax.ShapeDtypeStruct((8, XBC_WIDTH), F32),
                   jax.ShapeDtypeStruct((1, XBC_WIDTH), F32), jax.ShapeDtypeStruct((1, 128), F32),
                   jax.ShapeDtypeStruct((1, 128), F32), jax.ShapeDtypeStruct((1, 128), F32),
                   jax.ShapeDtypeStruct((1, SSM_WIDTH), F32)],
        scratch_shapes=[pltpu.VMEM((hrows, SSM_STATE), F32), pltpu.VMEM((HALO, XBC_WIDTH), F32)],
        compiler_params=_params())


def mix_out(attn, ssm, x, w_out16, gate1, post_mix_w, pre_ffn_w, scale2, shift2, tm=4 * TOKEN_TILE):
    s = x.shape[0]
    tm = min(tm, s)

    def body(a_ref, s_ref, x_ref, w_ref, g_ref, pw_ref, fw_ref, sc_ref, sh_ref, mixed_ref, x1_ref, h2_ref):
        mixed = (jnp.dot(a_ref[...], w_ref[:ATTN_WIDTH, :], preferred_element_type=F32)
                 + jnp.dot(s_ref[...], w_ref[ATTN_WIDTH:, :], preferred_element_type=F32))
        mixed_ref[...] = mixed
        x1 = x_ref[...] + g_ref[...] * _rms(mixed, pw_ref[...])
        x1_ref[...] = x1
        h2_ref[...] = _b(_norm_mod(x1, fw_ref[...], sc_ref[...], sh_ref[...]))

    tile = lambda n: pl.BlockSpec((tm, n), _row)
    return pl.pallas_call(
        body, name="mix_out", grid=(s // tm,),
        in_specs=[tile(ATTN_WIDTH), tile(SSM_WIDTH), tile(D_MODEL), pl.BlockSpec((D_MODEL, D_MODEL), _const)]
        + [_vec(D_MODEL)] * 5,
        out_specs=[tile(D_MODEL)] * 3,
        out_shape=[jax.ShapeDtypeStruct((s, D_MODEL), F32), jax.ShapeDtypeStruct((s, D_MODEL), F32),
                   jax.ShapeDtypeStruct((s, D_MODEL), BF16)],
        compiler_params=_params())(attn, ssm, x, w_out16, gate1, post_mix_w, pre_ffn_w, scale2, shift2)


GELU_K0, GELU_K1 = math.sqrt(2.0 / math.pi), 0.044715


def _gate(ug, uv):
    return jax.nn.gelu(ug, approximate=True) * uv


def _gate_bwd(ug, uv, df):
    sq = ug * ug
    t = jnp.tanh(ug * (GELU_K0 + (GELU_K0 * GELU_K1) * sq))
    half = 0.5 + 0.5 * t
    slope = half + ug * (1.0 - t * t) * (0.5 * GELU_K0 + (1.5 * GELU_K0 * GELU_K1) * sq)
    return df * uv * slope, df * (ug * half)


def _resident(shape):
    return pl.BlockSpec(shape, _const, pipeline_mode=pl.Buffered(1))


def up_gate(h2, w_up16, conv_w, conv_b, tm=TOKEN_TILE):
    s = h2.shape[0]

    def body(h_ref, halo_ref, w_ref, cw_ref, cb_ref, u_ref, uraw_ref, f_ref):
        halo = halo_ref[...]
        halo = jnp.where(pl.program_id(0) > 0, halo, jnp.zeros_like(halo))
        hin = jnp.concatenate([halo, h_ref[...]], axis=0)
        for lo in range(0, D_FF, FF_CHUNK):
            halves = []
            for base in (lo, D_FF + lo):
                cols = slice(base, base + FF_CHUNK)
                uraw = jnp.dot(hin, w_ref[:, cols], preferred_element_type=F32)
                uraw_ref[:, cols] = _b(uraw[NEXT:])
                u = (_conv_rows(uraw, cw_ref[:, cols], FFN_CONV) + cb_ref[:, cols])[NEXT:]
                u_ref[:, cols] = u
                halves.append(u)
            f_ref[:, lo:lo + FF_CHUNK] = _b(_gate(*halves))

    tile = lambda n: pl.BlockSpec((tm, n), _row)
    halo_spec = pl.BlockSpec((NEXT, D_MODEL), lambda i: (jnp.maximum(i * (tm // NEXT) - 1, 0), 0))
    return pl.pallas_call(
        body, name="up_gate", grid=(s // tm,),
        in_specs=[tile(D_MODEL), halo_spec, _resident((D_MODEL, 2 * D_FF)),
                  pl.BlockSpec((FFN_CONV, 2 * D_FF), _const), _vec(2 * D_FF)],
        out_specs=[tile(2 * D_FF), tile(2 * D_FF), tile(D_FF)],
        out_shape=[jax.ShapeDtypeStruct((s, 2 * D_FF), F32), jax.ShapeDtypeStruct((s, 2 * D_FF), BF16),
                   jax.ShapeDtypeStruct((s, D_FF), BF16)],
        compiler_params=_params())(h2, h2, w_up16, conv_w, conv_b)


DOWN_LOSS_TILE = 512


def down_loss(f16, w_down16, x1, target, gate2, post_ffn_w, tm=DOWN_LOSS_TILE):
    s = x1.shape[0]
    tm = min(tm, s)

    def body(f_ref, wd_ref, x1_ref, t_ref, g_ref, pw_ref, dffn_ref, dy_ref, loss_ref, dg_ref, dpw_ref, gw_ref):
        i = pl.program_id(0)

        @pl.when(i == 0)
        def _():
            loss_ref[...] = jnp.zeros_like(loss_ref)
            dg_ref[...] = jnp.zeros_like(dg_ref)
            dpw_ref[...] = jnp.zeros_like(dpw_ref)
            gw_ref[...] = jnp.zeros_like(gw_ref)

        ffn = jnp.dot(f_ref[...], wd_ref[...], preferred_element_type=F32)
        x1 = x1_ref[...]
        x2 = x1 + g_ref[...] * _rms(ffn, pw_ref[...])
        err = x2 - t_ref[...]
        dy = err * (1.0 / D_MODEL)
        dy_ref[...] = dy
        loss_ref[...] += 0.5 * jnp.sum(jnp.mean(err * err, axis=-1, keepdims=True))
        dffn, dg, dpw = _gated_rms_bwd(ffn, g_ref[...], pw_ref[...], dy)
        dffn16 = _b(dffn)
        dffn_ref[...] = dffn16
        dg_ref[...] += dg
        dpw_ref[...] += dpw
        gw_ref[...] += _tn(f_ref[...], dffn16)

    tile = lambda n: pl.BlockSpec((tm, n), _row)
    return pl.pallas_call(
        body, name="down_loss", grid=(s // tm,),
        in_specs=[tile(D_FF), _resident((D_FF, D_MODEL)), tile(D_MODEL), tile(D_MODEL), _vec(D_MODEL), _vec(D_MODEL)],
        out_specs=[tile(D_MODEL), tile(D_MODEL), _vec(128), _vec(D_MODEL), _vec(D_MODEL),
                   pl.BlockSpec((D_FF, D_MODEL), _const)],
        out_shape=[jax.ShapeDtypeStruct((s, D_MODEL), BF16), jax.ShapeDtypeStruct((s, D_MODEL), F32),
                   jax.ShapeDtypeStruct((1, 128), F32), jax.ShapeDtypeStruct((1, D_MODEL), F32),
                   jax.ShapeDtypeStruct((1, D_MODEL), F32), jax.ShapeDtypeStruct((D_FF, D_MODEL), F32)],
        compiler_params=_params())(f16, w_down16, x1, target, gate2, post_ffn_w)


BWD_CHUNK = 256


def ffn_bwd(u, u_raw16, d_ffn, conv_w, w_down_t16, w_up_t16, tm=TOKEN_TILE):
    s = u.shape[0]
    nt = s // tm

    def body(u_ref, unext_ref, uraw_ref, d_ref, dnext_ref, cw_ref, wdt_ref, wut_ref,
             du_ref, dh_ref, dcw_ref, dcb_ref):
        i = pl.program_id(0)

        @pl.when(i == 0)
        def _():
            dcw_ref[...] = jnp.zeros_like(dcw_ref)
            dcb_ref[...] = jnp.zeros_like(dcb_ref)

        dnext = dnext_ref[...]
        dnext = jnp.where(i < nt - 1, dnext, jnp.zeros_like(dnext))
        dff = jnp.concatenate([d_ref[...], dnext], axis=0)
        rows_ext = tm + NEXT
        for lo in range(0, D_FF, BWD_CHUNK):
            gcols, vcols = slice(lo, lo + BWD_CHUNK), slice(D_FF + lo, D_FF + lo + BWD_CHUNK)
            ug = jnp.concatenate([u_ref[:, gcols], unext_ref[:, gcols]], axis=0)
            uv = jnp.concatenate([u_ref[:, vcols], unext_ref[:, vcols]], axis=0)
            df = jnp.dot(dff, wdt_ref[:, gcols], preferred_element_type=F32)
            for cols, du in zip((gcols, vcols), _gate_bwd(ug, uv, df)):
                cw = cw_ref[:, cols]
                du1 = pltpu.roll(du, rows_ext - 1, axis=0)
                du2 = pltpu.roll(du, rows_ext - 2, axis=0)
                du_ref[:, cols] = _b((du * cw[2:3, :] + du1 * cw[1:2, :] + du2 * cw[0:1, :])[:tm])
                xr = uraw_ref[:, cols].astype(F32)
                rows = [jnp.sum(xr * d_[:tm], axis=0, keepdims=True) for d_ in (du2, du1, du)]
                dcw_ref[:, cols] += jnp.concatenate(rows + [jnp.zeros((8 - FFN_CONV, BWD_CHUNK), F32)], axis=0)
                dcb_ref[:, cols] += jnp.sum(du[:tm], axis=0, keepdims=True)
        dh_ref[...] = jnp.dot(du_ref[...], wut_ref[...], preferred_element_type=F32)

    tile = lambda n: pl.BlockSpec((tm, n), _row)
    nxt = lambda i: (jnp.minimum((i + 1) * (tm // NEXT), s // NEXT - 1), 0)
    return pl.pallas_call(
        body, name="ffn_bwd", grid=(nt,),
        in_specs=[tile(2 * D_FF), pl.BlockSpec((NEXT, 2 * D_FF), nxt), tile(2 * D_FF), tile(D_MODEL),
                  pl.BlockSpec((NEXT, D_MODEL), nxt), pl.BlockSpec((FFN_CONV, 2 * D_FF), _const),
                  _resident((D_MODEL, D_FF)), _resident((2 * D_FF, D_MODEL))],
        out_specs=[tile(2 * D_FF), tile(D_MODEL), pl.BlockSpec((8, 2 * D_FF), _const), _vec(2 * D_FF)],
        out_shape=[jax.ShapeDtypeStruct((s, 2 * D_FF), BF16), jax.ShapeDtypeStruct((s, D_MODEL), F32),
                   jax.ShapeDtypeStruct((8, 2 * D_FF), F32), jax.ShapeDtypeStruct((1, 2 * D_FF), F32)],
        compiler_params=_params())(u, u, u_raw16, d_ffn, d_ffn, conv_w, w_down_t16, w_up_t16)


def mix_bwd(dh2, x1, dy, mixed, attn, ssm, w_out_t16, pre_ffn_w, scale2, gate1, post_mix_w, tm=2 * TOKEN_TILE):
    s = x1.shape[0]

    def body(dh_ref, x1_ref, dy_ref, mx_ref, a_ref, s_ref, w_ref, fw_ref, sc_ref, g_ref, pw_ref,
             dx1_ref, da_ref, ds_ref, dfw_ref, dsc_ref, dsh_ref, dg_ref, dpw_ref, gw_ref):
        accs = (dfw_ref, dsc_ref, dsh_ref, dg_ref, dpw_ref)

        @pl.when(pl.program_id(0) == 0)
        def _():
            for r in accs + (gw_ref,):
                r[...] = jnp.zeros_like(r)

        dx1, dfw, dsc, dsh = _norm_mod_bwd(x1_ref[...], fw_ref[...], sc_ref[...], dh_ref[...])
        dx1 = dx1 + dy_ref[...]
        dx1_ref[...] = dx1
        dmixed, dg, dpw = _gated_rms_bwd(mx_ref[...], g_ref[...], pw_ref[...], dx1)
        dm16 = _b(dmixed)
        dmix_in = jnp.dot(dm16, w_ref[...], preferred_element_type=F32)
        da_ref[...] = _b(dmix_in[:, :ATTN_WIDTH])
        ds_ref[...] = dmix_in[:, ATTN_WIDTH:]
        gw_ref[:ATTN_WIDTH, :] += _tn(a_ref[...], dm16)
        gw_ref[ATTN_WIDTH:, :] += _tn(s_ref[...], dm16)
        for r, v in zip(accs, (dfw, dsc, dsh, dg, dpw)):
            r[...] += v

    tile = lambda n: pl.BlockSpec((tm, n), _row)
    return pl.pallas_call(
        body, name="mix_bwd", grid=(s // tm,),
        in_specs=[tile(D_MODEL)] * 4 + [tile(ATTN_WIDTH), tile(SSM_WIDTH), _resident((D_MODEL, D_MODEL))]
        + [_vec(D_MODEL)] * 4,
        out_specs=[tile(D_MODEL), tile(ATTN_WIDTH), tile(SSM_WIDTH)] + [_vec(D_MODEL)] * 5
        + [pl.BlockSpec((D_MODEL, D_MODEL), _const)],
        out_shape=[jax.ShapeDtypeStruct((s, D_MODEL), F32), jax.ShapeDtypeStruct((s, ATTN_WIDTH), BF16),
                   jax.ShapeDtypeStruct((s, SSM_WIDTH), F32)]
        + [jax.ShapeDtypeStruct((1, D_MODEL), F32)] * 5 + [jax.ShapeDtypeStruct((D_MODEL, D_MODEL), F32)],
        compiler_params=_params())(dh2, x1, dy, mixed, attn, ssm, w_out_t16, pre_ffn_w, scale2, gate1, post_mix_w)


INPROJ_BWD_TILE = 512


def inproj_bwd(dq, dk, dv, dxbc, dz, ddt, x, dx1, h1, w_in_t16, pre_mix_w, scale1, tm=INPROJ_BWD_TILE):
    s = x.shape[0]
    tm = min(tm, s)

    def body(dq_ref, dk_ref, dv_ref, dxbc_ref, dz_ref, ddt_ref, x_ref, dx1_ref, h_ref, w_ref, pw_ref, sc_ref,
             gx_ref, dpw_ref, dsc_ref, dsh_ref, gw_ref):
        accs = (dpw_ref, dsc_ref, dsh_ref)

        @pl.when(pl.program_id(0) == 0)
        def _():
            for r in accs + (gw_ref,):
                r[...] = jnp.zeros_like(r)

        h16 = h_ref[...]
        dh = None
        off_k, off_v = ATTN_WIDTH, ATTN_WIDTH + KV_WIDTH
        for r, lo, hi in ((dq_ref, 0, off_k), (dk_ref, off_k, off_v), (dv_ref, off_v, OFF_XBC),
                          (dxbc_ref, OFF_XBC, OFF_Z), (dz_ref, OFF_Z, OFF_DT), (ddt_ref, OFF_DT, PROJ_PAD)):
            d16 = _b(r[...])
            part = jnp.dot(d16, w_ref[lo:hi, :], preferred_element_type=F32)
            dh = part if dh is None else dh + part
            gw_ref[:, lo:hi] += _tn(h16, d16)
        dx, dpw, dsc, dsh = _norm_mod_bwd(x_ref[...], pw_ref[...], sc_ref[...], dh)
        gx_ref[...] = dx1_ref[...] + dx
        for r, v in zip(accs, (dpw, dsc, dsh)):
            r[...] += v

    tile = lambda n: pl.BlockSpec((tm, n), _row)
    return pl.pallas_call(
        body, name="inproj_bwd", grid=(s // tm,),
        in_specs=[tile(ATTN_WIDTH), tile(KV_WIDTH), tile(KV_WIDTH), tile(XBC_WIDTH), tile(SSM_WIDTH), tile(DT_PAD),
                  tile(D_MODEL), tile(D_MODEL), tile(D_MODEL), _resident((PROJ_PAD, D_MODEL))] + [_vec(D_MODEL)] * 2,
        out_specs=[tile(D_MODEL)] + [_vec(D_MODEL)] * 3 + [pl.BlockSpec((D_MODEL, PROJ_PAD), _const)],
        out_shape=[jax.ShapeDtypeStruct((s, D_MODEL), F32)] + [jax.ShapeDtypeStruct((1, D_MODEL), F32)] * 3
        + [jax.ShapeDtypeStruct((D_MODEL, PROJ_PAD), F32)],
        compiler_params=_params())(dq, dk, dv, dxbc, dz, ddt, x, dx1, h1, w_in_t16, pre_mix_w, scale1)


def _adam(g, w, m, v):
    new_m = ADAM_B1 * m + (1.0 - ADAM_B1) * g
    new_v = ADAM_B2 * v + (1.0 - ADAM_B2) * jnp.square(g)
    m_hat = new_m / (1.0 - ADAM_B1 ** ADAM_STEP)
    v_hat = new_v / (1.0 - ADAM_B2 ** ADAM_STEP)
    return -ADAM_LR * (m_hat / (jnp.sqrt(v_hat) + ADAM_EPS) + ADAM_WD * w), new_m, new_v


ROW_PARAMS = (("b_ada", 6144, 6144), ("pre_mix_w", 1024, 1024), ("attn_sinks", 128, 8), ("ssm_conv_b", 1024, 1024),
              ("ssm_dt_bias", 128, 8), ("ssm_a_log", 128, 8), ("ssm_d", 128, 8), ("ssm_norm_w", 512, 512),
              ("post_mix_w", 1024, 1024), ("pre_ffn_w", 1024, 1024), ("ffn_conv_b", 5632, 5632),
              ("post_ffn_w", 1024, 1024))
LOSS_LANES = 128


def adamw_small(row_all, rb_all, rel_bias_wmv, row_wmv):
    n_rows = len(ROW_PARAMS)

    def body(*refs):
        row_ref, rb_ref = refs[:2]
        wmv = refs[2:5 + 3 * n_rows]
        outs = refs[5 + 3 * n_rows:]
        g_row, g_rb = row_ref[0], rb_ref[0]
        for k in range(1, N_DEV):
            g_row = g_row + row_ref[k]
            g_rb = g_rb + rb_ref[k]
        outs[0][...] = g_row[:, :LOSS_LANES]
        grads = [g_rb[:, :N_Q_HEADS]]
        off = LOSS_LANES
        for _, lanes, width in ROW_PARAMS:
            grads.append(g_row[:, off:off + width])
            off += lanes
        for i, g in enumerate(grads):
            w_ref, m_ref, v_ref = wmv[3 * i:3 * i + 3]
            g_out, d_out, m_out, v_out = outs[1 + 4 * i:5 + 4 * i]
            g_out[...] = g
            d_out[...], m_out[...], v_out[...] = _adam(g, w_ref[...], m_ref[...], v_ref[...])

    flat_wmv = list(rel_bias_wmv) + [a for wmv in row_wmv for a in wmv]
    shapes = [jax.ShapeDtypeStruct((1, LOSS_LANES), F32)] + [jax.ShapeDtypeStruct((N_BUCKETS, N_Q_HEADS), F32)] * 4
    for _, _, width in ROW_PARAMS:
        shapes += [jax.ShapeDtypeStruct((1, width), F32)] * 4
    return pl.pallas_call(body, name="adamw_small", out_shape=shapes,
                          compiler_params=_params(n_axes=0))(row_all, rb_all, *flat_wmv)


def adamw(parts, w, m, v, name):
    p, r, n = parts.shape
    tr = _row_tile(r)

    def body(p_ref, w_ref, m_ref, v_ref, g_ref, d_ref, nm_ref, nv_ref):
        g = p_ref[0].astype(F32)
        for k in range(1, p):
            g = g + p_ref[k].astype(F32)
        g_ref[...] = g
        d_ref[...], nm_ref[...], nv_ref[...] = _adam(g, w_ref[...], m_ref[...], v_ref[...])

    tile = pl.BlockSpec((tr, n), _row)
    return pl.pallas_call(
        body, name=name, grid=(r // tr,),
        in_specs=[pl.BlockSpec((p, tr, n), lambda i: (0, i, 0)), tile, tile, tile],
        out_specs=[tile] * 4, out_shape=[jax.ShapeDtypeStruct((r, n), F32)] * 4,
        compiler_params=_params())(parts, w, m, v)


def _bucket_table():
    lq = ATTN_BLOCK
    qi = np.arange(lq)[:, None] + lq
    kj = np.arange(2 * lq)[None, :]
    dist = qi - kj
    d = np.maximum(dist, 0)
    max_exact = N_BUCKETS // 2
    nf = np.maximum(d, 1).astype(np.float32)
    large = max_exact + (np.log(nf / max_exact) / math.log(REL_MAX_DIST / max_exact)
                         * (N_BUCKETS - max_exact)).astype(np.int32)
    large = np.minimum(large, N_BUCKETS - 1)
    bucket = np.where(d < max_exact, d, large).astype(np.int32)
    in_band = (dist >= 0) & (dist < REL_MAX_DIST)
    return np.where(in_band, bucket, -1).astype(np.int32)


def _cols_from_blocks(g):
    return jnp.transpose(g, (1, 0, 2)).reshape(g.shape[1], N_DEV * g.shape[2])


def _cols_to_blocks(a):
    r, n = a.shape
    return jnp.transpose(a.reshape(r, N_DEV, n // N_DEV), (1, 0, 2))


def _perm_in_rows(wt):
    pad = jnp.zeros((DT_PAD - SSM_HEADS, wt.shape[1]), wt.dtype)
    return jnp.concatenate([wt[:768], wt[768:1280], wt[1792:2304], wt[1280:1792], wt[2304:2312], pad], axis=0)


def _unperm_in(g):
    return jnp.concatenate([g[:, :768], g[:, 768:1280], g[:, 1792:2304], g[:, 1280:1792], g[:, 2304:2312]], axis=1)


def _lane_pad(v, n=128):
    return jnp.pad(v, ((0, 0), (0, n - v.shape[1])))


def kernel(x, c, rel_bias, w_ada, b_ada, pre_mix_w, w_in, attn_sinks, ssm_conv_w, ssm_conv_b, ssm_dt_bias, ssm_a_log, ssm_d, ssm_norm_w, w_out, post_mix_w, pre_ffn_w, w_up, ffn_conv_w, ffn_conv_b, w_down, post_ffn_w, loss_target, m_rel_bias, m_w_ada, m_b_ada, m_pre_mix_w, m_w_in, m_attn_sinks, m_ssm_conv_w, m_ssm_conv_b, m_ssm_dt_bias, m_ssm_a_log, m_ssm_d, m_ssm_norm_w, m_w_out, m_post_mix_w, m_pre_ffn_w, m_w_up, m_ffn_conv_w, m_ffn_conv_b, m_w_down, m_post_ffn_w, v_rel_bias, v_w_ada, v_b_ada, v_pre_mix_w, v_w_in, v_attn_sinks, v_ssm_conv_w, v_ssm_conv_b, v_ssm_dt_bias, v_ssm_a_log, v_ssm_d, v_ssm_norm_w, v_w_out, v_post_mix_w, v_pre_ffn_w, v_w_up, v_ffn_conv_w, v_ffn_conv_b, v_w_down, v_post_ffn_w):
    weights = dict(rel_bias=rel_bias, w_ada=w_ada, b_ada=b_ada, pre_mix_w=pre_mix_w, w_in=w_in, attn_sinks=attn_sinks, ssm_conv_w=ssm_conv_w, ssm_conv_b=ssm_conv_b, ssm_dt_bias=ssm_dt_bias, ssm_a_log=ssm_a_log, ssm_d=ssm_d, ssm_norm_w=ssm_norm_w, w_out=w_out, post_mix_w=post_mix_w, pre_ffn_w=pre_ffn_w, w_up=w_up, ffn_conv_w=ffn_conv_w, ffn_conv_b=ffn_conv_b, w_down=w_down, post_ffn_w=post_ffn_w)
    mom_m = dict(rel_bias=m_rel_bias, w_ada=m_w_ada, b_ada=m_b_ada, pre_mix_w=m_pre_mix_w, w_in=m_w_in, attn_sinks=m_attn_sinks, ssm_conv_w=m_ssm_conv_w, ssm_conv_b=m_ssm_conv_b, ssm_dt_bias=m_ssm_dt_bias, ssm_a_log=m_ssm_a_log, ssm_d=m_ssm_d, ssm_norm_w=m_ssm_norm_w, w_out=m_w_out, post_mix_w=m_post_mix_w, pre_ffn_w=m_pre_ffn_w, w_up=m_w_up, ffn_conv_w=m_ffn_conv_w, ffn_conv_b=m_ffn_conv_b, w_down=m_w_down, post_ffn_w=m_post_ffn_w)
    mom_v = dict(rel_bias=v_rel_bias, w_ada=v_w_ada, b_ada=v_b_ada, pre_mix_w=v_pre_mix_w, w_in=v_w_in, attn_sinks=v_attn_sinks, ssm_conv_w=v_ssm_conv_w, ssm_conv_b=v_ssm_conv_b, ssm_dt_bias=v_ssm_dt_bias, ssm_a_log=v_ssm_a_log, ssm_d=v_ssm_d, ssm_norm_w=v_ssm_norm_w, w_out=v_w_out, post_mix_w=v_post_mix_w, pre_ffn_w=v_pre_ffn_w, w_up=v_w_up, ffn_conv_w=v_ffn_conv_w, ffn_conv_b=v_ffn_conv_b, w_down=v_w_down, post_ffn_w=v_post_ffn_w)
    order = ['rel_bias', 'w_ada', 'b_ada', 'pre_mix_w', 'w_in', 'attn_sinks', 'ssm_conv_w', 'ssm_conv_b', 'ssm_dt_bias', 'ssm_a_log', 'ssm_d', 'ssm_norm_w', 'w_out', 'post_mix_w', 'pre_ffn_w', 'w_up', 'ffn_conv_w', 'ffn_conv_b', 'w_down', 'post_ffn_w']

    me = 4 * lax.axis_index("x") + 2 * lax.axis_index("y") + lax.axis_index("c")
    xs_ = x[0]
    target = loss_target[0]

    (w_in_g, scw_g, fcw_g, c_g) = all_gather([_b(w_in[0]).T, ssm_conv_w[0], ffn_conv_w[0], c], "gather_weights")
    w_in_t16 = _perm_in_rows(w_in_g.reshape(IN_PROJ_WIDTH, D_MODEL))
    w_in16 = w_in_t16.T
    ssm_cw = _cols_from_blocks(scw_g)
    ffn_cw = _cols_from_blocks(fcw_g)
    c_all = c_g.reshape(N_DEV, D_MODEL)

    n_cols = w_ada.shape[2]
    b_cols = lax.dynamic_slice(b_ada, (0, me * n_cols), (1, n_cols))
    mod_part = ada_fwd(c_all, w_ada[0], b_cols)
    (mod_rows,) = all_to_all([mod_part.reshape(N_DEV, 1, n_cols)], "scatter_mod")
    mod = mod_rows.reshape(N_MOD, 1, D_MODEL)
    shift1, scale1, gate1, shift2, scale2, gate2 = (mod[i] for i in range(N_MOD))

    bucket_band = jnp.asarray(_bucket_table())
    bias = rel_bias_table(rel_bias, bucket_band)
    sinks_row = _lane_pad(attn_sinks)
    dt_bias, a_log, dskip = _lane_pad(ssm_dt_bias), _lane_pad(ssm_a_log), _lane_pad(ssm_d)

    h1, qkv, xbc_raw, z, dt_raw, w_out_g = pre_mix_inproj(
        xs_, pre_mix_w, scale1, shift1, w_in16, [(_b(w_out[0]), False)])
    attn, w_up_g = attn_fwd(qkv, bias, sinks_row, [(_b(w_up[0]).T, False)])
    ssm, hprev_all, w_down_g = ssm_fwd(xbc_raw, z, dt_raw, ssm_cw, ssm_conv_b, dt_bias, a_log, dskip, ssm_norm_w,
                                       [(_b(w_down[0]), False)])
    w_out16 = w_out_g.reshape(D_MODEL, D_MODEL)
    w_out_t16 = w_out16.T
    w_up_t16 = w_up_g.reshape(2 * D_FF, D_MODEL)
    w_up16 = w_up_t16.T
    w_down16 = w_down_g.reshape(D_FF, D_MODEL)
    w_down_t16 = w_down16.T
    mixed, x1, h2 = mix_out(attn, ssm, xs_, w_out16, gate1, post_mix_w, pre_ffn_w, scale2, shift2)
    u, u_raw16, f16 = up_gate(h2, w_up16, ffn_cw, ffn_conv_b)
    d_ffn, dy, loss_part, d_gate2, d_post_ffn_w, g_w_down = down_loss(f16, w_down16, x1, target, gate2, post_ffn_w)

    du_raw, dh2, d_ffn_cw, d_ffn_cb = ffn_bwd(u, u_raw16, d_ffn, ffn_cw, w_down_t16, w_up_t16)
    g_w_up = matmul_tn(h2, du_raw, "grad_w_up", D_MODEL, FF_CHUNK, tk=2048)
    (dx1, d_attn, d_ssm, d_pre_ffn_w, d_scale2, d_shift2, d_gate1, d_post_mix_w, g_w_out) = mix_bwd(
        dh2, x1, dy, mixed, attn, ssm, w_out_t16, pre_ffn_w, scale2, gate1, post_mix_w)
    dq, dk, dv, dbias, dsinks, p_w_down = attn_bwd(
        qkv, bias, sinks_row, d_attn, [(g_w_down.reshape(N_DEV, D_FF // N_DEV, D_MODEL), True)])
    d_rel_bias = rel_bias_grad(dbias, bucket_band)
    (dxbc, dz, ddt, d_ssm_cw, d_ssm_cb, d_dt_bias, d_a_log, d_dskip, d_norm_w, p_w_up, p_w_out) = ssm_bwd(
        xbc_raw, z, dt_raw, hprev_all, d_ssm, ssm_cw, ssm_conv_b, dt_bias, a_log, dskip, ssm_norm_w,
        [(_cols_to_blocks(g_w_up), True), (g_w_out.reshape(N_DEV, D_MODEL // N_DEV, D_MODEL), True)])
    grad_x, d_pre_mix_w, d_scale1, d_shift1, g_w_in_perm = inproj_bwd(
        dq, dk, dv, dxbc, dz, ddt, xs_, dx1, h1, w_in_t16, pre_mix_w, scale1)
    g_w_in = _unperm_in(g_w_in_perm)

    d_mod = jnp.concatenate([d_shift1, d_scale1, d_gate1, d_shift2, d_scale2, d_gate2], axis=1)
    late = ("w_in", "ssm_conv_w", "ffn_conv_w")
    full = [_cols_to_blocks(g_w_in), _cols_to_blocks(d_ssm_cw[:SSM_CONV]), _cols_to_blocks(d_ffn_cw[:FFN_CONV])]
    core = lax.axis_index("c").astype(jnp.int32).reshape(1)
    got = pair_exchange(full, "pair_grads")
    chip_sums = [pair_sum(f_, g_, core, "pair_sum_" + k) for k, f_, g_ in zip(late, full, got)]
    chip_parts = all_to_all(chip_sums, "scatter_grads", CHIP_FLIPS, _chip_index)
    (d_mod_rows,) = all_to_all([d_mod.reshape(N_DEV, 1, n_cols)], "scatter_dmod")
    g_w_ada = ada_bwd(c_all, d_mod_rows.reshape(N_DEV, n_cols))

    row_g = dict(b_ada=d_mod, pre_mix_w=d_pre_mix_w, attn_sinks=dsinks, ssm_conv_b=d_ssm_cb, ssm_dt_bias=d_dt_bias,
                 ssm_a_log=d_a_log, ssm_d=d_dskip, ssm_norm_w=d_norm_w, post_mix_w=d_post_mix_w,
                 pre_ffn_w=d_pre_ffn_w, ffn_conv_b=d_ffn_cb, post_ffn_w=d_post_ffn_w)
    row = jnp.concatenate([loss_part] + [row_g[k] for k, _, _ in ROW_PARAMS], axis=1)
    row_all, rb_all = all_gather([row, d_rel_bias], "gather_small")

    wmv = lambda k: (weights[k], mom_m[k], mom_v[k])
    small = adamw_small(row_all, rb_all, wmv("rel_bias"), [wmv(k) for k, _, _ in ROW_PARAMS])
    loss = small[0][0, 0]
    res = {k: tuple(small[1 + 4 * i:5 + 4 * i]) for i, k in enumerate(["rel_bias"] + [k for k, _, _ in ROW_PARAMS])}
    big = list(zip(late, chip_parts)) + [("w_down", p_w_down), ("w_up", p_w_up), ("w_out", p_w_out),
                                        ("w_ada", g_w_ada[None])]
    for k, parts in big:
        res[k] = tuple(o[None] for o in adamw(parts, weights[k][0], mom_m[k][0], mom_v[k][0], "adamw_" + k))

    outs = [loss, grad_x[None]]
    for field in range(4):
        outs += [res[k][field] for k in order]
    return tuple(outs)
```

```python
import math

import numpy as np
import jax
import jax.numpy as jnp
from jax import lax
from jax.experimental import pallas as pl
from jax.experimental.pallas import tpu as pltpu

F32 = jnp.float32
BF16 = jnp.bfloat16
MESH_ID = pl.DeviceIdType.MESH

N_DEV = 8
D_MODEL = 1024
N_Q_HEADS = 8
N_KV_HEADS = 2
HEAD_DIM = 64
ATTN_WIDTH = 512
KV_WIDTH = 128
ATTN_BLOCK = 128
N_BUCKETS = 32
REL_MAX_DIST = 128
SSM_HEADS = 8
SSM_HEAD_DIM = 64
SSM_WIDTH = 512
SSM_STATE = 128
SSM_GROUPS = 2
SSM_BC = 256
SSM_CONV = 4
SSM_CHUNK = 256
XBC_WIDTH = SSM_WIDTH + 2 * SSM_BC
D_FF = 2816
FFN_CONV = 3
NORM_EPS = 1e-6
N_MOD = 6
IN_PROJ_WIDTH = 2312
QKV_W = ATTN_WIDTH + 2 * KV_WIDTH
OFF_XBC = QKV_W
OFF_Z = OFF_XBC + XBC_WIDTH
OFF_DT = OFF_Z + SSM_WIDTH
DT_PAD = 128
PROJ_PAD = OFF_DT + DT_PAD
FF_CHUNK = 1408

ADAM_LR = 0.001
ADAM_B1 = 0.9
ADAM_B2 = 0.999
ADAM_EPS = 1e-08
ADAM_WD = 0.01
ADAM_STEP = 10

TOKEN_TILE = 256
HALO = 8
NEXT = 16
VMEM_LIMIT = 56 * 1024 * 1024


def _params(vmem=VMEM_LIMIT, n_axes=1):
    return pltpu.CompilerParams(dimension_semantics=("arbitrary",) * n_axes, vmem_limit_bytes=vmem)


def _b(x):
    return x.astype(BF16)


def _nn(a, b):
    return jnp.dot(_b(a), _b(b), preferred_element_type=F32)


def _nt(a, b):
    return lax.dot_general(_b(a), _b(b), (((1,), (1,)), ((), ())), preferred_element_type=F32)


def _tn(a, b):
    return lax.dot_general(_b(a), _b(b), (((0,), (0,)), ((), ())), preferred_element_type=F32)


@jax.custom_vjp
def mm(a, b):
    return _nn(a, b)


mm.defvjp(lambda a, b: (_nn(a, b), (a, b)),
          lambda r, g: (_nt(g, r[1]).astype(r[0].dtype), _tn(r[0], g).astype(r[1].dtype)))


@jax.custom_vjp
def mm_nt(a, b):
    return _nt(a, b)


mm_nt.defvjp(lambda a, b: (_nt(a, b), (a, b)),
             lambda r, g: (_nn(g, r[1]).astype(r[0].dtype), _tn(g, r[0]).astype(r[1].dtype)))


@jax.custom_vjp
def mm_tn(a, b):
    return _tn(a, b)


mm_tn.defvjp(lambda a, b: (_tn(a, b), (a, b)),
             lambda r, g: (_nt(r[1], g).astype(r[0].dtype), _nn(r[0], g).astype(r[1].dtype)))


def _rms(x, w):
    return x * lax.rsqrt(jnp.mean(x * x, axis=-1, keepdims=True) + NORM_EPS) * w


def _norm_mod(x, w, scale, shift):
    return _rms(x, w) * (1.0 + scale) + shift


def _rms_bwd(x, w, dy):
    r = lax.rsqrt(jnp.mean(x * x, axis=-1, keepdims=True) + NORM_EPS)
    xhat = x * r
    g = dy * w
    dx = r * (g - xhat * jnp.mean(g * xhat, axis=-1, keepdims=True))
    return dx, jnp.sum(dy * xhat, axis=0, keepdims=True)


def _norm_mod_bwd(x, w, scale, dh):
    dx, da = _rms_bwd(x, w * (1.0 + scale), dh)
    return dx, da * (1.0 + scale), da * w, jnp.sum(dh, axis=0, keepdims=True)


def _gated_rms_bwd(m, gate, w, dy):
    dm, t = _rms_bwd(m, w * gate, dy)
    return dm, t * w, t * gate


def _silu(x):
    return x * jax.nn.sigmoid(x)


def _conv_rows(xin, w, k):
    acc = xin * w[k - 1:k, :]
    for j in range(1, k):
        acc = acc + pltpu.roll(xin, j, axis=0) * w[k - 1 - j:k - j, :]
    return acc


def _conv_rows_t(du, w, k):
    n = du.shape[0]
    acc = du * w[k - 1:k, :]
    for j in range(1, k):
        acc = acc + pltpu.roll(du, n - j, axis=0) * w[k - 1 - j:k - j, :]
    return acc


def _row(i):
    return (i, 0)


def _const(i):
    return (0, 0)


def _vec(n):
    return pl.BlockSpec((1, n), _const)


def _block_index(p):
    return 4 * p[0] + 2 * p[1] + p[2]


def all_gather(arrs, name):
    n = len(arrs)

    def body(*refs):
        ins, outs = refs[:n], refs[n:2 * n]
        send_sems, recv_sems, local_sems = refs[2 * n:]
        x, y, c = lax.axis_index("x"), lax.axis_index("y"), lax.axis_index("c")
        me, sibling = (x, y, c), (x, y, 1 - c)
        chips = [(1 - x, y), (x, 1 - y), (1 - x, 1 - y)]

        def copy(a, k, block, to, src=None):
            dst = outs[a].at[_block_index(block)]
            return pltpu.make_async_remote_copy(
                src_ref=dst if src is None else src, dst_ref=dst,
                send_sem=send_sems.at[a * 7 + k], recv_sem=recv_sems.at[a * 7 + k],
                device_id=to, device_id_type=MESH_ID)

        mine = [pltpu.make_async_copy(ins[a], outs[a].at[_block_index(me)], local_sems.at[a]) for a in range(n)]
        for cp in mine:
            cp.start()
        first = []
        for a in range(n):
            first.append(copy(a, 0, me, sibling, src=ins[a]))
            first += [copy(a, 1 + j, me, (*chip, c), src=ins[a]) for j, chip in enumerate(chips)]
        for cp in first:
            cp.start()
        passed = []
        for j, chip in enumerate(chips):
            for a in range(n):
                copy(a, 1 + j, (*chip, c), me).wait_recv()
                cp = copy(a, 4 + j, (*chip, c), sibling)
                cp.start()
                passed.append(cp)
        for a in range(n):
            copy(a, 0, sibling, me).wait_recv()
            for j, chip in enumerate(chips):
                copy(a, 4 + j, (*chip, 1 - c), me).wait_recv()
        for cp in first + passed:
            cp.wait_send()
        for cp in mine:
            cp.wait()

    any_spec = pl.BlockSpec(memory_space=pl.ANY)
    return pl.pallas_call(
        body, name=name,
        out_shape=[jax.ShapeDtypeStruct((N_DEV,) + a.shape, a.dtype) for a in arrs],
        in_specs=[any_spec] * n, out_specs=[any_spec] * n,
        scratch_shapes=[pltpu.SemaphoreType.DMA((7 * n,)), pltpu.SemaphoreType.DMA((7 * n,)),
                        pltpu.SemaphoreType.DMA((n,))],
    )(*arrs)


ALL_FLIPS = ((0, 0, 1), (0, 1, 0), (0, 1, 1), (1, 0, 0), (1, 0, 1), (1, 1, 0), (1, 1, 1))
CHIP_FLIPS = ((0, 1, 0), (1, 0, 0), (1, 1, 0))


def _chip_index(p):
    return 2 * p[0] + p[1]


def all_to_all(arrs, name, flips=ALL_FLIPS, index=_block_index):
    n = len(arrs)
    nf = len(flips)

    def body(*refs):
        ins, outs = refs[:n], refs[n:2 * n]
        send_sems, recv_sems, local_sems = refs[2 * n:]
        pos = (lax.axis_index("x"), lax.axis_index("y"), lax.axis_index("c"))
        me = index(pos)
        peers = [tuple(1 - p if f else p for p, f in zip(pos, flip)) for flip in flips]

        def copy(a, k):
            peer = peers[k]
            return pltpu.make_async_remote_copy(
                src_ref=ins[a].at[index(peer)], dst_ref=outs[a].at[me],
                send_sem=send_sems.at[a * nf + k], recv_sem=recv_sems.at[a * nf + k],
                device_id=peer, device_id_type=MESH_ID)

        def landed(a, k):
            slot = outs[a].at[index(peers[k])]
            return pltpu.make_async_remote_copy(
                src_ref=slot, dst_ref=slot,
                send_sem=send_sems.at[a * nf + k], recv_sem=recv_sems.at[a * nf + k],
                device_id=peers[k], device_id_type=MESH_ID)

        mine = [pltpu.make_async_copy(ins[a].at[me], outs[a].at[me], local_sems.at[a]) for a in range(n)]
        for cp in mine:
            cp.start()
        sent = [copy(a, k) for a in range(n) for k in range(nf)]
        for cp in sent:
            cp.start()
        for a in range(n):
            for k in range(nf):
                landed(a, k).wait_recv()
        for cp in sent:
            cp.wait_send()
        for cp in mine:
            cp.wait()

    any_spec = pl.BlockSpec(memory_space=pl.ANY)
    return pl.pallas_call(
        body, name=name,
        out_shape=[jax.ShapeDtypeStruct(a.shape, a.dtype) for a in arrs],
        in_specs=[any_spec] * n, out_specs=[any_spec] * n,
        scratch_shapes=[pltpu.SemaphoreType.DMA((nf * n,)), pltpu.SemaphoreType.DMA((nf * n,)),
                        pltpu.SemaphoreType.DMA((n,))],
    )(*arrs)


def _direct_exchange(src, dst, sems, scatter):
    send_sems, recv_sems, local_sem = sems
    pos = (lax.axis_index("x"), lax.axis_index("y"), lax.axis_index("c"))
    me = _block_index(pos)
    peers = [tuple(1 - p if f else p for p, f in zip(pos, flip)) for flip in ALL_FLIPS]

    def outgoing(k):
        return pltpu.make_async_remote_copy(
            src_ref=src.at[_block_index(peers[k])] if scatter else src, dst_ref=dst.at[me],
            send_sem=send_sems.at[k], recv_sem=recv_sems.at[k], device_id=peers[k], device_id_type=MESH_ID)

    def incoming(k):
        slot = dst.at[_block_index(peers[k])]
        return pltpu.make_async_remote_copy(
            src_ref=slot, dst_ref=slot, send_sem=send_sems.at[k], recv_sem=recv_sems.at[k],
            device_id=peers[k], device_id_type=MESH_ID)

    def local():
        return pltpu.make_async_copy(src.at[me] if scatter else src, dst.at[me], local_sem)

    def start():
        local().start()
        for k in range(len(ALL_FLIPS)):
            outgoing(k).start()

    def finish():
        for k in range(len(ALL_FLIPS)):
            incoming(k).wait_recv()
        for k in range(len(ALL_FLIPS)):
            outgoing(k).wait_send()
        local().wait()

    return start, finish


def hosted_call(body, exchanges, steps, n_in, n_out, **call):
    n_ex = len(exchanges)

    def wrapped(*refs):
        ins, srcs = refs[:n_in], refs[n_in:n_in + n_ex]
        outs = refs[n_in + n_ex:n_in + n_ex + n_out]
        dsts = refs[n_in + n_ex + n_out:n_in + 2 * n_ex + n_out]
        rest = refs[n_in + 2 * n_ex + n_out:]
        scratch, sems = rest[:len(rest) - 3 * n_ex], rest[len(rest) - 3 * n_ex:]
        plans = [_direct_exchange(srcs[e], dsts[e], sems[3 * e:3 * e + 3], exchanges[e][1]) for e in range(n_ex)]

        @pl.when(pl.program_id(0) == 0)
        def _():
            for start, _ in plans:
                start()

        body(*ins, *outs, *scratch)

        @pl.when(pl.program_id(0) == steps - 1)
        def _():
            for _, finish in plans:
                finish()

    any_spec = pl.BlockSpec(memory_space=pl.ANY)
    landings = [jax.ShapeDtypeStruct(src.shape if scatter else (N_DEV,) + src.shape, src.dtype)
                for src, scatter in exchanges]
    n_flips = len(ALL_FLIPS)
    sems = [pltpu.SemaphoreType.DMA((n_flips,)), pltpu.SemaphoreType.DMA((n_flips,)), pltpu.SemaphoreType.DMA(())]
    return pl.pallas_call(
        wrapped, grid=(steps,),
        in_specs=list(call.pop("in_specs")) + [any_spec] * n_ex,
        out_specs=list(call.pop("out_specs")) + [any_spec] * n_ex,
        out_shape=list(call.pop("out_shape")) + landings,
        scratch_shapes=list(call.pop("scratch_shapes", [])) + sems * n_ex,
        **call)


def _grid_call(body, steps, args, exchanges, **call):
    if not exchanges:
        return pl.pallas_call(body, grid=(steps,), **call)(*args)
    srcs = [src for src, _ in exchanges]
    return hosted_call(body, exchanges, steps, len(args), len(call["out_shape"]), **call)(*args, *srcs)


N_CHIPS = 4


def pair_exchange(arrs, name):
    n = len(arrs)

    def body(*refs):
        ins, outs = refs[:n], refs[n:2 * n]
        send_sems, recv_sems = refs[2 * n:]
        x, y, c = lax.axis_index("x"), lax.axis_index("y"), lax.axis_index("c")
        sibling = (x, y, 1 - c)
        sent = []
        for a in range(n):
            for q in range(N_CHIPS):
                cp = pltpu.make_async_remote_copy(
                    src_ref=ins[a].at[2 * q + (1 - c)], dst_ref=outs[a].at[q],
                    send_sem=send_sems.at[a * N_CHIPS + q], recv_sem=recv_sems.at[a * N_CHIPS + q],
                    device_id=sibling, device_id_type=MESH_ID)
                cp.start()
                sent.append(cp)
        for cp in sent:
            cp.wait_recv()
        for cp in sent:
            cp.wait_send()

    any_spec = pl.BlockSpec(memory_space=pl.ANY)
    return pl.pallas_call(
        body, name=name,
        out_shape=[jax.ShapeDtypeStruct((N_CHIPS,) + a.shape[1:], a.dtype) for a in arrs],
        in_specs=[any_spec] * n, out_specs=[any_spec] * n,
        scratch_shapes=[pltpu.SemaphoreType.DMA((N_CHIPS * n,)), pltpu.SemaphoreType.DMA((N_CHIPS * n,))],
    )(*arrs)


def pair_sum(full, got, core, name):
    _, r, n = full.shape
    tr = _row_tile(r)

    def body(c_ref, mine_ref, got_ref, o_ref):
        o_ref[...] = _b(mine_ref[...] + got_ref[...])

    grid_spec = pltpu.PrefetchScalarGridSpec(
        num_scalar_prefetch=1, grid=(N_CHIPS, r // tr),
        in_specs=[pl.BlockSpec((1, tr, n), lambda q, i, c_ref: (2 * q + c_ref[0], i, 0)),
                  pl.BlockSpec((1, tr, n), lambda q, i, c_ref: (q, i, 0))],
        out_specs=pl.BlockSpec((1, tr, n), lambda q, i, c_ref: (q, i, 0)))
    return pl.pallas_call(body, name=name, grid_spec=grid_spec,
                          out_shape=jax.ShapeDtypeStruct((N_CHIPS, r, n), BF16),
                          compiler_params=_params(n_axes=2))(core, full, got)


def _row_tile(r):
    for cand in (256, 128, 64, 32, 16):
        if r % cand == 0 and r > cand:
            return cand
    return r


def ada_fwd(c_all, w_ada, b_cols):
    def body(c_ref, w_ref, b_ref, o_ref):
        o_ref[...] = _nn(_silu(c_ref[...]), w_ref[...]) + b_ref[...]

    return pl.pallas_call(body, name="ada_fwd",
                          out_shape=jax.ShapeDtypeStruct((N_DEV, w_ada.shape[1]), F32),
                          compiler_params=_params(n_axes=0))(c_all, w_ada, b_cols)


def ada_bwd(c_all, g_cols):
    def body(c_ref, g_ref, o_ref):
        o_ref[...] = _tn(_silu(c_ref[...]), g_ref[...])

    return pl.pallas_call(body, name="ada_bwd",
                          out_shape=jax.ShapeDtypeStruct((c_all.shape[1], g_cols.shape[1]), F32),
                          compiler_params=_params(n_axes=0))(c_all, g_cols)


def matmul_tn(a, b, name, bm, bn, tk=512):
    s, m = a.shape
    n = b.shape[1]
    tk = min(tk, s)

    def body(a_ref, b_ref, o_ref):
        @pl.when(pl.program_id(2) == 0)
        def _():
            o_ref[...] = jnp.zeros_like(o_ref)

        o_ref[...] += _tn(a_ref[...], b_ref[...])

    return pl.pallas_call(
        body, name=name, grid=(m // bm, n // bn, s // tk),
        in_specs=[pl.BlockSpec((tk, bm), lambda i, j, k: (k, i)), pl.BlockSpec((tk, bn), lambda i, j, k: (k, j))],
        out_specs=pl.BlockSpec((bm, bn), lambda i, j, k: (i, j)),
        out_shape=jax.ShapeDtypeStruct((m, n), F32),
        compiler_params=_params(n_axes=3))(a, b)


def pre_mix_inproj(x, w, scale, shift, w_in16, exchange=None, tm=4 * TOKEN_TILE):
    s = x.shape[0]
    tm = min(tm, s)

    def body(x_ref, w_ref, sc_ref, sh_ref, win_ref, h_ref, qkv_ref, xbc_ref, z_ref, dt_ref):
        h16 = _b(_norm_mod(x_ref[...], w_ref[...], sc_ref[...], sh_ref[...]))
        h_ref[...] = h16
        dot = lambda lo, hi: jnp.dot(h16, win_ref[:, lo:hi], preferred_element_type=F32)
        qkv_ref[...] = _b(dot(0, OFF_XBC))
        xbc_ref[...] = dot(OFF_XBC, OFF_Z)
        z_ref[...] = dot(OFF_Z, OFF_DT)
        dt_ref[...] = dot(OFF_DT, PROJ_PAD)

    tile = lambda n: pl.BlockSpec((tm, n), _row)
    return _grid_call(
        body, s // tm, (x, w, scale, shift, w_in16), exchange, name="pre_mix_inproj",
        in_specs=[tile(D_MODEL), _vec(D_MODEL), _vec(D_MODEL), _vec(D_MODEL), pl.BlockSpec((D_MODEL, PROJ_PAD), _const)],
        out_specs=[tile(D_MODEL), tile(QKV_W), tile(XBC_WIDTH), tile(SSM_WIDTH), tile(DT_PAD)],
        out_shape=[jax.ShapeDtypeStruct((s, D_MODEL), BF16), jax.ShapeDtypeStruct((s, QKV_W), BF16),
                   jax.ShapeDtypeStruct((s, XBC_WIDTH), F32), jax.ShapeDtypeStruct((s, SSM_WIDTH), F32),
                   jax.ShapeDtypeStruct((s, DT_PAD), F32)],
        compiler_params=_params())


ATTN_QB_FWD, ATTN_QB_BWD = 4, 2


def _attn_tile(q, kp, kc, vp, vc, bias, sinks):
    lq = ATTN_BLOCK
    group = N_Q_HEADS // N_KV_HEADS
    lanes = lax.broadcasted_iota(jnp.int32, (1, 128), 1)
    rid = lax.broadcasted_iota(jnp.int32, (group * lq, 1), 0)
    sink_cols = []
    for hk in range(N_KV_HEADS):
        sink = jnp.zeros((group * lq, 1), F32)
        for g in range(group):
            s_h = jnp.sum(jnp.where(lanes == hk * group + g, sinks, 0.0), axis=-1, keepdims=True)
            sink = jnp.where((rid >= g * lq) & (rid < (g + 1) * lq), s_h, sink)
        sink_cols.append(sink)
    kall = jnp.concatenate([kp, kc], axis=0)
    vall = jnp.concatenate([vp, vc], axis=0)
    blocks = []
    for b in range(q.shape[0] // lq):
        qb = q[b * lq:(b + 1) * lq]
        outs = []
        for hk in range(N_KV_HEADS):
            cols = slice(hk * HEAD_DIM, (hk + 1) * HEAD_DIM)
            kb = kall[b * lq:(b + 2) * lq, cols]
            vb = vall[b * lq:(b + 2) * lq, cols]
            qg = jnp.concatenate([qb[:, (hk * group + g) * HEAD_DIM:(hk * group + g + 1) * HEAD_DIM]
                                  for g in range(group)], axis=0)
            sc = mm_nt(qg, kb) * (HEAD_DIM ** -0.5) + bias[b][hk]
            sink = sink_cols[hk]
            m = lax.stop_gradient(jnp.maximum(jnp.max(sc, axis=-1, keepdims=True), sink))
            p = jnp.exp(sc - m)
            probs = p / (jnp.sum(p, axis=-1, keepdims=True) + jnp.exp(sink - m))
            og = mm(probs, vb)
            outs += [og[g * lq:(g + 1) * lq] for g in range(group)]
        blocks.append(jnp.concatenate(outs, axis=1))
    return jnp.concatenate(blocks, axis=0)


def _attn_tile_bwd(q, kp, kc, vp, vc, bias, sinks, do):
    lq = ATTN_BLOCK
    group = N_Q_HEADS // N_KV_HEADS
    scale = HEAD_DIM ** -0.5
    lanes = lax.broadcasted_iota(jnp.int32, (1, 128), 1)
    rid = lax.broadcasted_iota(jnp.int32, (group * lq, 1), 0)
    sink_cols = []
    for hk in range(N_KV_HEADS):
        sink = jnp.zeros((group * lq, 1), F32)
        for g in range(group):
            s_h = jnp.sum(jnp.where(lanes == hk * group + g, sinks, 0.0), axis=-1, keepdims=True)
            sink = jnp.where((rid >= g * lq) & (rid < (g + 1) * lq), s_h, sink)
        sink_cols.append(sink)
    kall = jnp.concatenate([kp, kc], axis=0)
    vall = jnp.concatenate([vp, vc], axis=0)
    dsk = jnp.zeros((1, 128), F32)
    dq_blocks, dbias = [], []
    nqb = q.shape[0] // lq
    dk_parts = [[None] * nqb for _ in range(N_KV_HEADS)]
    dv_parts = [[None] * nqb for _ in range(N_KV_HEADS)]
    for b in range(nqb):
        qb, dob = q[b * lq:(b + 1) * lq], do[b * lq:(b + 1) * lq]
        dq_heads, dbias_b = [], []
        for hk in range(N_KV_HEADS):
            cols = slice(hk * HEAD_DIM, (hk + 1) * HEAD_DIM)
            kb = kall[b * lq:(b + 2) * lq, cols]
            vb = vall[b * lq:(b + 2) * lq, cols]
            heads = [hk * group + g for g in range(group)]
            qg = jnp.concatenate([qb[:, h * HEAD_DIM:(h + 1) * HEAD_DIM] for h in heads], axis=0)
            dog = jnp.concatenate([dob[:, h * HEAD_DIM:(h + 1) * HEAD_DIM] for h in heads], axis=0)
            sink = sink_cols[hk]
            sc = _nt(qg, kb) * scale + bias[b][hk]
            m = jnp.maximum(jnp.max(sc, axis=-1, keepdims=True), sink)
            p = jnp.exp(sc - m)
            es = jnp.exp(sink - m)
            inv = 1.0 / (jnp.sum(p, axis=-1, keepdims=True) + es)
            probs = p * inv
            dprobs = _nt(dog, vb)
            delta = jnp.sum(probs * dprobs, axis=-1, keepdims=True)
            dsc = probs * (dprobs - delta)
            dbias_b.append(dsc)
            dsink = -(es * inv) * delta
            for g, h in enumerate(heads):
                tot = jnp.sum(dsink[g * lq:(g + 1) * lq], axis=0, keepdims=True)
                dsk = dsk + jnp.where(lanes == h, tot, 0.0)
            dqg = _nn(dsc, kb) * scale
            dq_heads += [dqg[g * lq:(g + 1) * lq] for g in range(group)]
            dk_parts[hk][b] = _tn(dsc, qg) * scale
            dv_parts[hk][b] = _tn(probs, dog)
        dq_blocks.append(jnp.concatenate(dq_heads, axis=1))
        dbias.append(dbias_b)

    def overlap_add(parts):
        chunks = []
        for r in range(nqb + 1):
            acc = None
            if r < nqb:
                acc = parts[r][:lq]
            if r >= 1:
                tail = parts[r - 1][lq:]
                acc = tail if acc is None else acc + tail
            chunks.append(acc)
        return jnp.concatenate(chunks, axis=0)

    dkall = jnp.concatenate([overlap_add(dk_parts[hk]) for hk in range(N_KV_HEADS)], axis=1)
    dvall = jnp.concatenate([overlap_add(dv_parts[hk]) for hk in range(N_KV_HEADS)], axis=1)
    return jnp.concatenate(dq_blocks, axis=0), dkall, dvall, dbias, dsk


def _attn_in_specs(nt, clamp, nqb):
    lq, tq = ATTN_BLOCK, ATTN_BLOCK * nqb
    cur = lambda n: jnp.minimum(n, nt - 1) if clamp else n
    prev = lambda n: jnp.maximum(cur(n) * nqb - 1, 0)
    kcol, vcol = ATTN_WIDTH // KV_WIDTH, ATTN_WIDTH // KV_WIDTH + 1
    return [pl.BlockSpec((tq, ATTN_WIDTH), lambda n: (cur(n), 0)),
            pl.BlockSpec((lq, KV_WIDTH), lambda n: (prev(n), kcol)),
            pl.BlockSpec((tq, KV_WIDTH), lambda n: (cur(n), kcol)),
            pl.BlockSpec((lq, KV_WIDTH), lambda n: (prev(n), vcol)),
            pl.BlockSpec((tq, KV_WIDTH), lambda n: (cur(n), vcol)),
            pl.BlockSpec((2, N_KV_HEADS, 4 * lq, 2 * lq), lambda n: (0, 0, 0, 0)),
            _vec(128)]


def _tile_bias(bias_ref, first, nqb):
    return [[jnp.where(first, bias_ref[1, hk], bias_ref[0, hk]) if b == 0 else bias_ref[0, hk]
             for hk in range(N_KV_HEADS)] for b in range(nqb)]


def attn_fwd(qkv, bias, sinks_rows, exchange=None):
    s = qkv.shape[0]
    nqb = min(ATTN_QB_FWD, s // ATTN_BLOCK)
    tq = ATTN_BLOCK * nqb
    nt = s // tq

    def body(q_ref, kp_ref, kc_ref, vp_ref, vc_ref, bias_ref, sk_ref, o_ref):
        f = lambda r: r[...].astype(F32)
        o = _attn_tile(f(q_ref), f(kp_ref), f(kc_ref), f(vp_ref), f(vc_ref),
                       _tile_bias(bias_ref, pl.program_id(0) == 0, nqb), sk_ref[...])
        o_ref[...] = _b(o)

    return _grid_call(
        body, nt, (qkv, qkv, qkv, qkv, qkv, bias, sinks_rows), exchange, name="attn_fwd",
        in_specs=_attn_in_specs(nt, False, nqb),
        out_specs=[pl.BlockSpec((tq, ATTN_WIDTH), _row)],
        out_shape=[jax.ShapeDtypeStruct((s, ATTN_WIDTH), BF16)],
        compiler_params=_params())


def attn_bwd(qkv, bias, sinks_rows, d_attn, exchange=None):
    s = qkv.shape[0]
    nqb = ATTN_QB_BWD
    lq, tq = ATTN_BLOCK, ATTN_BLOCK * nqb
    nt = s // tq

    def body(q_ref, kp_ref, kc_ref, vp_ref, vc_ref, bias_ref, sk_ref, do_ref,
             dq_ref, dk_ref, dv_ref, dbias_ref, dsk_ref, carry_k, carry_v):
        n = pl.program_id(0)

        @pl.when(n == 0)
        def _():
            dbias_ref[...] = jnp.zeros_like(dbias_ref)
            dsk_ref[...] = jnp.zeros_like(dsk_ref)
            carry_k[...] = jnp.zeros_like(carry_k)
            carry_v[...] = jnp.zeros_like(carry_v)

        @pl.when(n < nt)
        def _():
            f = lambda r: r[...].astype(F32)
            dq, dkall, dvall, dbias, dsk = _attn_tile_bwd(
                f(q_ref), f(kp_ref), f(kc_ref), f(vp_ref), f(vc_ref), _tile_bias(bias_ref, n == 0, nqb), sk_ref[...],
                f(do_ref))
            dkp, dkc, dvp, dvc = dkall[:lq], dkall[lq:], dvall[:lq], dvall[lq:]
            dq_ref[...] = _b(dq)
            done = tq - lq
            dk_ref[:done, :] = _b(carry_k[:done, :])
            dv_ref[:done, :] = _b(carry_v[:done, :])
            dk_ref[done:, :] = _b(carry_k[done:, :] + dkp)
            dv_ref[done:, :] = _b(carry_v[done:, :] + dvp)
            carry_k[...] = dkc
            carry_v[...] = dvc
            dsk_ref[...] += dsk
            first = (n == 0).astype(F32)
            for hk in range(N_KV_HEADS):
                total = dbias[0][hk]
                for b in range(1, nqb):
                    total = total + dbias[b][hk]
                dbias_ref[0, hk] += total - first * dbias[0][hk]
                dbias_ref[1, hk] += first * dbias[0][hk]

        @pl.when(n == nt)
        def _():
            dk_ref[...] = _b(carry_k[...])
            dv_ref[...] = _b(carry_v[...])

    cur = lambda n: (jnp.minimum(n, nt - 1), 0)
    done_map = lambda n: (jnp.maximum(n - 1, 0), 0)
    return _grid_call(
        body, nt + 1, (qkv, qkv, qkv, qkv, qkv, bias, sinks_rows, d_attn), exchange, name="attn_bwd",
        in_specs=_attn_in_specs(nt, True, nqb) + [pl.BlockSpec((tq, ATTN_WIDTH), cur)],
        out_specs=[pl.BlockSpec((tq, ATTN_WIDTH), cur), pl.BlockSpec((tq, KV_WIDTH), done_map),
                   pl.BlockSpec((tq, KV_WIDTH), done_map),
                   pl.BlockSpec((2, N_KV_HEADS, 4 * lq, 2 * lq), lambda n: (0, 0, 0, 0)), _vec(128)],
        out_shape=[jax.ShapeDtypeStruct((s, ATTN_WIDTH), BF16), jax.ShapeDtypeStruct((s, KV_WIDTH), BF16),
                   jax.ShapeDtypeStruct((s, KV_WIDTH), BF16),
                   jax.ShapeDtypeStruct((2, N_KV_HEADS, 4 * lq, 2 * lq), F32), jax.ShapeDtypeStruct((1, 128), F32)],
        scratch_shapes=[pltpu.VMEM((tq, KV_WIDTH), F32), pltpu.VMEM((tq, KV_WIDTH), F32)],
        compiler_params=_params())


def rel_bias_table(rel_bias, bucket):
    lq = ATTN_BLOCK
    group = N_Q_HEADS // N_KV_HEADS

    def body(rb_ref, bk_ref, o_ref):
        bk = bk_ref[...]
        prev_keys = lax.broadcasted_iota(jnp.int32, bk.shape, 1) < lq
        accs = [jnp.full(bk.shape, -1e30, F32) for _ in range(N_Q_HEADS)]
        for b in range(N_BUCKETS):
            hit = bk == b
            accs = [jnp.where(hit, rb_ref[b, h], acc) for h, acc in enumerate(accs)]
        for h in range(N_Q_HEADS):
            rows = slice((h % group) * lq, (h % group + 1) * lq)
            o_ref[0, h // group, rows, :] = accs[h]
            o_ref[1, h // group, rows, :] = jnp.where(prev_keys, -1e30, accs[h])

    return pl.pallas_call(
        body, name="rel_bias_table",
        in_specs=[pl.BlockSpec(memory_space=pltpu.SMEM), pl.BlockSpec(memory_space=pltpu.VMEM)],
        out_shape=jax.ShapeDtypeStruct((2, N_KV_HEADS, group * lq, 2 * lq), F32),
        compiler_params=_params(n_axes=0))(rel_bias, bucket)


def rel_bias_grad(dbias, bucket):
    lq = ATTN_BLOCK
    group = N_Q_HEADS // N_KV_HEADS

    def body(db_ref, bk_ref, o_ref):
        rows = lax.broadcasted_iota(jnp.int32, (N_BUCKETS, 128), 0)
        lanes = lax.broadcasted_iota(jnp.int32, (N_BUCKETS, 128), 1)
        bk = bk_ref[...]
        per_head = []
        for h in range(N_Q_HEADS):
            sl = slice((h % group) * lq, (h % group + 1) * lq)
            per_head.append(db_ref[0, h // group, sl, :] + db_ref[1, h // group, sl, :])

        def per_bucket(b, acc):
            hit = (bk == b).astype(F32)
            for h in range(N_Q_HEADS):
                val = jnp.sum(per_head[h] * hit, keepdims=True)
                acc = acc + jnp.where((rows == b) & (lanes == h), val, 0.0)
            return acc

        o_ref[...] = lax.fori_loop(0, N_BUCKETS, per_bucket, jnp.zeros((N_BUCKETS, 128), F32))

    return pl.pallas_call(body, name="rel_bias_grad", out_shape=jax.ShapeDtypeStruct((N_BUCKETS, 128), F32),
                          compiler_params=_params(n_axes=0))(dbias, bucket)


def _tri_sum(a, upper):
    n = a.shape[0]
    ri = lax.broadcasted_iota(jnp.int32, (n, n), 0)
    ci = lax.broadcasted_iota(jnp.int32, (n, n), 1)
    tri = ((ri <= ci) if upper else (ri >= ci)).astype(BF16)
    hi = a.astype(BF16)
    rest = a - hi.astype(F32)
    mid = rest.astype(BF16)
    lo = (rest - mid.astype(F32)).astype(BF16)
    dot = lambda part: jnp.dot(tri, part, preferred_element_type=F32)
    return dot(hi) + dot(mid) + dot(lo)


@jax.custom_vjp
def _cumsum_rows(a):
    return _tri_sum(a, False)


_cumsum_rows.defvjp(lambda a: (_tri_sum(a, False), None), lambda _, g: (_tri_sum(g, True),))


def _ssm_core(u, z, dt_raw, hprev, dt_bias, a_log, dskip, norm_w):
    lc = u.shape[0]
    xbc = _silu(u)
    xs, bm, cm = xbc[:, :SSM_WIDTH], xbc[:, SSM_WIDTH:SSM_WIDTH + SSM_BC], xbc[:, SSM_WIDTH + SSM_BC:]
    dt = jax.nn.softplus(dt_raw + dt_bias)
    adt = dt * (-jnp.exp(a_log))
    ri = lax.broadcasted_iota(jnp.int32, (lc, lc), 0)
    ci = lax.broadcasted_iota(jnp.int32, (lc, lc), 1)
    causal = ri >= ci
    acum = _cumsum_rows(adt)
    acum_t = acum.T
    last = acum[lc - 1:lc, :]
    per_group = SSM_HEADS // SSM_GROUPS
    lane = lax.broadcasted_iota(jnp.int32, (1, 128), 1)
    rowid = lax.broadcasted_iota(jnp.int32, (128, 1), 0)
    lo_lanes = lane < SSM_HEAD_DIM
    ys, hs = [], []
    for g in range(SSM_GROUPS):
        bg = bm[:, g * SSM_STATE:(g + 1) * SSM_STATE]
        cg = cm[:, g * SSM_STATE:(g + 1) * SSM_STATE]
        cb = mm_nt(cg, bg)
        for pp in range(per_group // 2):
            ha = g * per_group + 2 * pp
            xp = xs[:, ha * SSM_HEAD_DIM:(ha + 2) * SSM_HEAD_DIM]
            hp = hprev[ha * SSM_HEAD_DIM:(ha + 2) * SSM_HEAD_DIM, :]
            xcp = xp * jnp.where(lo_lanes, dt[:, ha:ha + 1], dt[:, ha + 1:ha + 2])
            y_h, st_h = [], []
            for h in (ha, ha + 1):
                col, rowv, lasth = acum[:, h:h + 1], acum_t[h:h + 1, :], last[:, h:h + 1]
                decay = jnp.exp(jnp.where(causal, col - rowv, -1e30))
                y_h.append(mm(cb * decay, xcp) + mm_nt(cg * jnp.exp(col), hp))
                st_h.append(mm_tn(xcp, bg * jnp.exp(lasth - col)))
            y_pair = jnp.where(lo_lanes, y_h[0], y_h[1])
            st_pair = jnp.where(rowid < SSM_HEAD_DIM, st_h[0], st_h[1])
            la, lb = last[:, ha:ha + 1], last[:, ha + 1:ha + 2]
            hs.append(jnp.exp(jnp.where(rowid < SSM_HEAD_DIM, la, lb)) * hp + st_pair)
            dsk = jnp.where(lo_lanes, dskip[:, ha:ha + 1], dskip[:, ha + 1:ha + 2])
            ys.append(y_pair + dsk * xp)
    y = jnp.concatenate(ys, axis=1) * _silu(z)
    gw = SSM_WIDTH // SSM_GROUPS
    outs = []
    for g in range(SSM_GROUPS):
        yg = y[:, g * gw:(g + 1) * gw]
        outs.append(yg * lax.rsqrt(jnp.mean(yg * yg, axis=-1, keepdims=True) + NORM_EPS))
    return jnp.concatenate(outs, axis=1) * norm_w, jnp.concatenate(hs, axis=0)


def _ssm_param_specs():
    return [pl.BlockSpec((SSM_CONV, XBC_WIDTH), _const), _vec(XBC_WIDTH), _vec(128), _vec(128), _vec(128),
            _vec(SSM_WIDTH)]


SSM_FWD_SUB = 2
SSM_BWD_SUB = 1


def ssm_fwd(xbc_raw, z, dt_raw, conv_w, conv_b, dt_bias, a_log, dskip, norm_w, exchange=None):
    s = xbc_raw.shape[0]
    lc = SSM_CHUNK
    lt = lc * SSM_FWD_SUB
    hrows = SSM_HEADS * SSM_HEAD_DIM

    def body(x_ref, halo_ref, z_ref, dt_ref, cw_ref, cb_ref, dtb_ref, al_ref, dk_ref, nw_ref,
             o_ref, hp_ref, state):
        i = pl.program_id(0)

        @pl.when(i == 0)
        def _():
            state[...] = jnp.zeros_like(state)

        halo = halo_ref[...] * (i > 0).astype(F32)
        xin = jnp.concatenate([halo, x_ref[...]], axis=0)
        u = (_conv_rows(xin, cw_ref[...], SSM_CONV) + cb_ref[...])[HALO:]
        h = state[...]
        for k in range(SSM_FWD_SUB):
            rows = slice(k * lc, (k + 1) * lc)
            hp_ref[k * hrows:(k + 1) * hrows, :] = h
            out, h = _ssm_core(u[rows], z_ref[rows, :], dt_ref[rows, :], h, dtb_ref[...], al_ref[...], dk_ref[...],
                               nw_ref[...])
            o_ref[rows, :] = _b(out)
        state[...] = h

    tile = lambda n: pl.BlockSpec((lt, n), _row)
    halo_spec = pl.BlockSpec((HALO, XBC_WIDTH), lambda i: (jnp.maximum(i * (lt // HALO) - 1, 0), 0))
    return _grid_call(
        body, s // lt, (xbc_raw, xbc_raw, z, dt_raw, conv_w, conv_b, dt_bias, a_log, dskip, norm_w), exchange,
        name="ssm_fwd",
        in_specs=[tile(XBC_WIDTH), halo_spec, tile(SSM_WIDTH), tile(DT_PAD)] + _ssm_param_specs(),
        out_specs=[tile(SSM_WIDTH), pl.BlockSpec((SSM_FWD_SUB * hrows, SSM_STATE), _row)],
        out_shape=[jax.ShapeDtypeStruct((s, SSM_WIDTH), BF16),
                   jax.ShapeDtypeStruct((s // lc * hrows, SSM_STATE), F32)],
        scratch_shapes=[pltpu.VMEM((hrows, SSM_STATE), F32)],
        compiler_params=_params())


def ssm_bwd(xbc_raw, z, dt_raw, hprev_all, d_out, conv_w, conv_b, dt_bias, a_log, dskip, norm_w, exchange=None):
    s = xbc_raw.shape[0]
    lc = SSM_CHUNK
    sub = SSM_BWD_SUB
    lt = lc * sub
    nt = s // lt
    hrows = SSM_HEADS * SSM_HEAD_DIM

    def body(x_ref, halo_ref, z_ref, dt_ref, hp_ref, do_ref, cw_ref, cb_ref, dtb_ref, al_ref, dk_ref, nw_ref,
             dx_ref, dz_ref, ddt_ref, dcw_ref, dcb_ref, ddtb_ref, dal_ref, ddk_ref, dnw_ref, dstate, du_next):
        i = pl.program_id(0)
        tile_no = nt - 1 - i

        @pl.when(i == 0)
        def _():
            dstate[...] = jnp.zeros_like(dstate)
            du_next[...] = jnp.zeros_like(du_next)
            for r in (dcw_ref, dcb_ref, ddtb_ref, dal_ref, ddk_ref, dnw_ref):
                r[...] = jnp.zeros_like(r)

        halo = halo_ref[...] * (tile_no > 0).astype(F32)
        xin = jnp.concatenate([halo, x_ref[...]], axis=0)
        cw = cw_ref[...]
        u = (_conv_rows(xin, cw, SSM_CONV) + cb_ref[...])[HALO:]
        dh = dstate[...]
        dus = [None] * sub
        for k in reversed(range(sub)):
            rows = slice(k * lc, (k + 1) * lc)
            _, vjp = jax.vjp(_ssm_core, u[rows], z_ref[rows, :], dt_ref[rows, :], hp_ref[k * hrows:(k + 1) * hrows, :],
                             dtb_ref[...], al_ref[...], dk_ref[...], nw_ref[...])
            dus[k], dz, ddt, dh, ddtb, dal, ddk, dnw = vjp((do_ref[rows, :], dh))
            dz_ref[rows, :] = _b(dz)
            ddt_ref[rows, :] = _b(ddt)
            ddtb_ref[...] += ddtb
            dal_ref[...] += dal
            ddk_ref[...] += ddk
            dnw_ref[...] += dnw
        dstate[...] = dh
        du = jnp.concatenate(dus, axis=0)
        du_ext = jnp.concatenate([du, du_next[...]], axis=0)
        dx_ref[...] = _b(_conv_rows_t(du_ext, cw, SSM_CONV)[:lt])
        du_next[...] = du[:HALO]
        sums = [jnp.sum(du * pltpu.roll(xin, j, axis=0)[HALO:] if j else du * xin[HALO:], axis=0, keepdims=True)
                for j in range(SSM_CONV)]
        dcw_ref[...] += jnp.concatenate(sums[::-1] + [jnp.zeros((8 - SSM_CONV, XBC_WIDTH), F32)], axis=0)
        dcb_ref[...] += jnp.sum(du, axis=0, keepdims=True)

    rev = lambda i: (nt - 1 - i, 0)
    tile = lambda n: pl.BlockSpec((lt, n), rev)
    halo_spec = pl.BlockSpec((HALO, XBC_WIDTH), lambda i: (jnp.maximum((nt - 1 - i) * (lt // HALO) - 1, 0), 0))
    acc = lambda r, n: pl.BlockSpec((r, n), _const)
    return _grid_call(
        body, nt, (xbc_raw, xbc_raw, z, dt_raw, hprev_all, d_out, conv_w, conv_b, dt_bias, a_log, dskip, norm_w),
        exchange, name="ssm_bwd",
        in_specs=[tile(XBC_WIDTH), halo_spec, tile(SSM_WIDTH), tile(DT_PAD),
                  pl.BlockSpec((sub * hrows, SSM_STATE), rev), tile(SSM_WIDTH)] + _ssm_param_specs(),
        out_specs=[tile(XBC_WIDTH), tile(SSM_WIDTH), tile(DT_PAD), acc(8, XBC_WIDTH), acc(1, XBC_WIDTH),
                   acc(1, 128), acc(1, 128), acc(1, 128), acc(1, SSM_WIDTH)],
        out_shape=[jax.ShapeDtypeStruct((s, XBC_WIDTH), BF16), jax.ShapeDtypeStruct((s, SSM_WIDTH), BF16),
                   jax.ShapeDtypeStruct((s, DT_PAD), BF16), jax.ShapeDtypeStruct((8, XBC_WIDTH), F32),
                   jax.ShapeDtypeStruct((1, XBC_WIDTH), F32), jax.ShapeDtypeStruct((1, 128), F32),
                   jax.ShapeDtypeStruct((1, 128), F32), jax.ShapeDtypeStruct((1, 128), F32),
                   jax.ShapeDtypeStruct((1, SSM_WIDTH), F32)],
        scratch_shapes=[pltpu.VMEM((hrows, SSM_STATE), F32), pltpu.VMEM((HALO, XBC_WIDTH), F32)],
        compiler_params=_params())


def mix_out(attn, ssm, x, w_out16, gate1, post_mix_w, pre_ffn_w, scale2, shift2, tm=4 * TOKEN_TILE):
    s = x.shape[0]
    tm = min(tm, s)

    def body(a_ref, s_ref, x_ref, w_ref, g_ref, pw_ref, fw_ref, sc_ref, sh_ref, mixed_ref, x1_ref, h2_ref):
        mixed = (jnp.dot(a_ref[...], w_ref[:ATTN_WIDTH, :], preferred_element_type=F32)
                 + jnp.dot(s_ref[...], w_ref[ATTN_WIDTH:, :], preferred_element_type=F32))
        mixed_ref[...] = mixed
        x1 = x_ref[...] + g_ref[...] * _rms(mixed, pw_ref[...])
        x1_ref[...] = x1
        h2_ref[...] = _b(_norm_mod(x1, fw_ref[...], sc_ref[...], sh_ref[...]))

    tile = lambda n: pl.BlockSpec((tm, n), _row)
    return pl.pallas_call(
        body, name="mix_out", grid=(s // tm,),
        in_specs=[tile(ATTN_WIDTH), tile(SSM_WIDTH), tile(D_MODEL), pl.BlockSpec((D_MODEL, D_MODEL), _const)]
        + [_vec(D_MODEL)] * 5,
        out_specs=[tile(D_MODEL)] * 3,
        out_shape=[jax.ShapeDtypeStruct((s, D_MODEL), F32), jax.ShapeDtypeStruct((s, D_MODEL), F32),
                   jax.ShapeDtypeStruct((s, D_MODEL), BF16)],
        compiler_params=_params())(attn, ssm, x, w_out16, gate1, post_mix_w, pre_ffn_w, scale2, shift2)


GELU_K0, GELU_K1 = math.sqrt(2.0 / math.pi), 0.044715


def _gate(ug, uv):
    return jax.nn.gelu(ug, approximate=True) * uv


def _gate_bwd(ug, uv, df):
    sq = ug * ug
    t = jnp.tanh(ug * (GELU_K0 + (GELU_K0 * GELU_K1) * sq))
    half = 0.5 + 0.5 * t
    slope = half + ug * (1.0 - t * t) * (0.5 * GELU_K0 + (1.5 * GELU_K0 * GELU_K1) * sq)
    return df * uv * slope, df * (ug * half)


def _resident(shape):
    return pl.BlockSpec(shape, _const, pipeline_mode=pl.Buffered(1))


def up_gate(h2, w_up16, conv_w, conv_b, tm=TOKEN_TILE):
    s = h2.shape[0]

    def body(h_ref, halo_ref, w_ref, cw_ref, cb_ref, u_ref, uraw_ref, f_ref):
        halo = halo_ref[...]
        halo = jnp.where(pl.program_id(0) > 0, halo, jnp.zeros_like(halo))
        hin = jnp.concatenate([halo, h_ref[...]], axis=0)
        for lo in range(0, D_FF, FF_CHUNK):
            halves = []
            for base in (lo, D_FF + lo):
                cols = slice(base, base + FF_CHUNK)
                uraw = jnp.dot(hin, w_ref[:, cols], preferred_element_type=F32)
                uraw_ref[:, cols] = _b(uraw[NEXT:])
                u = (_conv_rows(uraw, cw_ref[:, cols], FFN_CONV) + cb_ref[:, cols])[NEXT:]
                u_ref[:, cols] = u
                halves.append(u)
            f_ref[:, lo:lo + FF_CHUNK] = _b(_gate(*halves))

    tile = lambda n: pl.BlockSpec((tm, n), _row)
    halo_spec = pl.BlockSpec((NEXT, D_MODEL), lambda i: (jnp.maximum(i * (tm // NEXT) - 1, 0), 0))
    return pl.pallas_call(
        body, name="up_gate", grid=(s // tm,),
        in_specs=[tile(D_MODEL), halo_spec, _resident((D_MODEL, 2 * D_FF)),
                  pl.BlockSpec((FFN_CONV, 2 * D_FF), _const), _vec(2 * D_FF)],
        out_specs=[tile(2 * D_FF), tile(2 * D_FF), tile(D_FF)],
        out_shape=[jax.ShapeDtypeStruct((s, 2 * D_FF), F32), jax.ShapeDtypeStruct((s, 2 * D_FF), BF16),
                   jax.ShapeDtypeStruct((s, D_FF), BF16)],
        compiler_params=_params())(h2, h2, w_up16, conv_w, conv_b)


DOWN_LOSS_TILE = 512


def down_loss(f16, w_down16, x1, target, gate2, post_ffn_w, tm=DOWN_LOSS_TILE):
    s = x1.shape[0]
    tm = min(tm, s)

    def body(f_ref, wd_ref, x1_ref, t_ref, g_ref, pw_ref, dffn_ref, dy_ref, loss_ref, dg_ref, dpw_ref, gw_ref):
        i = pl.program_id(0)

        @pl.when(i == 0)
        def _():
            loss_ref[...] = jnp.zeros_like(loss_ref)
            dg_ref[...] = jnp.zeros_like(dg_ref)
            dpw_ref[...] = jnp.zeros_like(dpw_ref)
            gw_ref[...] = jnp.zeros_like(gw_ref)

        ffn = jnp.dot(f_ref[...], wd_ref[...], preferred_element_type=F32)
        x1 = x1_ref[...]
        x2 = x1 + g_ref[...] * _rms(ffn, pw_ref[...])
        err = x2 - t_ref[...]
        dy = err * (1.0 / D_MODEL)
        dy_ref[...] = dy
        loss_ref[...] += 0.5 * jnp.sum(jnp.mean(err * err, axis=-1, keepdims=True))
        dffn, dg, dpw = _gated_rms_bwd(ffn, g_ref[...], pw_ref[...], dy)
        dffn16 = _b(dffn)
        dffn_ref[...] = dffn16
        dg_ref[...] += dg
        dpw_ref[...] += dpw
        gw_ref[...] += _tn(f_ref[...], dffn16)

    tile = lambda n: pl.BlockSpec((tm, n), _row)
    return pl.pallas_call(
        body, name="down_loss", grid=(s // tm,),
        in_specs=[tile(D_FF), _resident((D_FF, D_MODEL)), tile(D_MODEL), tile(D_MODEL), _vec(D_MODEL), _vec(D_MODEL)],
        out_specs=[tile(D_MODEL), tile(D_MODEL), _vec(128), _vec(D_MODEL), _vec(D_MODEL),
                   pl.BlockSpec((D_FF, D_MODEL), _const)],
        out_shape=[jax.ShapeDtypeStruct((s, D_MODEL), BF16), jax.ShapeDtypeStruct((s, D_MODEL), F32),
                   jax.ShapeDtypeStruct((1, 128), F32), jax.ShapeDtypeStruct((1, D_MODEL), F32),
                   jax.ShapeDtypeStruct((1, D_MODEL), F32), jax.ShapeDtypeStruct((D_FF, D_MODEL), F32)],
        compiler_params=_params())(f16, w_down16, x1, target, gate2, post_ffn_w)


BWD_CHUNK = 256


def ffn_bwd(u, u_raw16, d_ffn, conv_w, w_down_t16, w_up_t16, tm=TOKEN_TILE):
    s = u.shape[0]
    nt = s // tm

    def body(u_ref, unext_ref, uraw_ref, d_ref, dnext_ref, cw_ref, wdt_ref, wut_ref,
             du_ref, dh_ref, dcw_ref, dcb_ref):
        i = pl.program_id(0)

        @pl.when(i == 0)
        def _():
            dcw_ref[...] = jnp.zeros_like(dcw_ref)
            dcb_ref[...] = jnp.zeros_like(dcb_ref)

        dnext = dnext_ref[...]
        dnext = jnp.where(i < nt - 1, dnext, jnp.zeros_like(dnext))
        dff = jnp.concatenate([d_ref[...], dnext], axis=0)
        rows_ext = tm + NEXT
        for lo in range(0, D_FF, BWD_CHUNK):
            gcols, vcols = slice(lo, lo + BWD_CHUNK), slice(D_FF + lo, D_FF + lo + BWD_CHUNK)
            ug = jnp.concatenate([u_ref[:, gcols], unext_ref[:, gcols]], axis=0)
            uv = jnp.concatenate([u_ref[:, vcols], unext_ref[:, vcols]], axis=0)
            df = jnp.dot(dff, wdt_ref[:, gcols], preferred_element_type=F32)
            for cols, du in zip((gcols, vcols), _gate_bwd(ug, uv, df)):
                cw = cw_ref[:, cols]
                du1 = pltpu.roll(du, rows_ext - 1, axis=0)
                du2 = pltpu.roll(du, rows_ext - 2, axis=0)
                du_ref[:, cols] = _b((du * cw[2:3, :] + du1 * cw[1:2, :] + du2 * cw[0:1, :])[:tm])
                xr = uraw_ref[:, cols].astype(F32)
                rows = [jnp.sum(xr * d_[:tm], axis=0, keepdims=True) for d_ in (du2, du1, du)]
                dcw_ref[:, cols] += jnp.concatenate(rows + [jnp.zeros((8 - FFN_CONV, BWD_CHUNK), F32)], axis=0)
                dcb_ref[:, cols] += jnp.sum(du[:tm], axis=0, keepdims=True)
        dh_ref[...] = jnp.dot(du_ref[...], wut_ref[...], preferred_element_type=F32)

    tile = lambda n: pl.BlockSpec((tm, n), _row)
    nxt = lambda i: (jnp.minimum((i + 1) * (tm // NEXT), s // NEXT - 1), 0)
    return pl.pallas_call(
        body, name="ffn_bwd", grid=(nt,),
        in_specs=[tile(2 * D_FF), pl.BlockSpec((NEXT, 2 * D_FF), nxt), tile(2 * D_FF), tile(D_MODEL),
                  pl.BlockSpec((NEXT, D_MODEL), nxt), pl.BlockSpec((FFN_CONV, 2 * D_FF), _const),
                  _resident((D_MODEL, D_FF)), _resident((2 * D_FF, D_MODEL))],
        out_specs=[tile(2 * D_FF), tile(D_MODEL), pl.BlockSpec((8, 2 * D_FF), _const), _vec(2 * D_FF)],
        out_shape=[jax.ShapeDtypeStruct((s, 2 * D_FF), BF16), jax.ShapeDtypeStruct((s, D_MODEL), F32),
                   jax.ShapeDtypeStruct((8, 2 * D_FF), F32), jax.ShapeDtypeStruct((1, 2 * D_FF), F32)],
        compiler_params=_params())(u, u, u_raw16, d_ffn, d_ffn, conv_w, w_down_t16, w_up_t16)


def mix_bwd(dh2, x1, dy, mixed, attn, ssm, w_out_t16, pre_ffn_w, scale2, gate1, post_mix_w, tm=2 * TOKEN_TILE):
    s = x1.shape[0]

    def body(dh_ref, x1_ref, dy_ref, mx_ref, a_ref, s_ref, w_ref, fw_ref, sc_ref, g_ref, pw_ref,
             dx1_ref, da_ref, ds_ref, dfw_ref, dsc_ref, dsh_ref, dg_ref, dpw_ref, gw_ref):
        accs = (dfw_ref, dsc_ref, dsh_ref, dg_ref, dpw_ref)

        @pl.when(pl.program_id(0) == 0)
        def _():
            for r in accs + (gw_ref,):
                r[...] = jnp.zeros_like(r)

        dx1, dfw, dsc, dsh = _norm_mod_bwd(x1_ref[...], fw_ref[...], sc_ref[...], dh_ref[...])
        dx1 = dx1 + dy_ref[...]
        dx1_ref[...] = dx1
        dmixed, dg, dpw = _gated_rms_bwd(mx_ref[...], g_ref[...], pw_ref[...], dx1)
        dm16 = _b(dmixed)
        dmix_in = jnp.dot(dm16, w_ref[...], preferred_element_type=F32)
        da_ref[...] = _b(dmix_in[:, :ATTN_WIDTH])
        ds_ref[...] = dmix_in[:, ATTN_WIDTH:]
        gw_ref[:ATTN_WIDTH, :] += _tn(a_ref[...], dm16)
        gw_ref[ATTN_WIDTH:, :] += _tn(s_ref[...], dm16)
        for r, v in zip(accs, (dfw, dsc, dsh, dg, dpw)):
            r[...] += v

    tile = lambda n: pl.BlockSpec((tm, n), _row)
    return pl.pallas_call(
        body, name="mix_bwd", grid=(s // tm,),
        in_specs=[tile(D_MODEL)] * 4 + [tile(ATTN_WIDTH), tile(SSM_WIDTH), _resident((D_MODEL, D_MODEL))]
        + [_vec(D_MODEL)] * 4,
        out_specs=[tile(D_MODEL), tile(ATTN_WIDTH), tile(SSM_WIDTH)] + [_vec(D_MODEL)] * 5
        + [pl.BlockSpec((D_MODEL, D_MODEL), _const)],
        out_shape=[jax.ShapeDtypeStruct((s, D_MODEL), F32), jax.ShapeDtypeStruct((s, ATTN_WIDTH), BF16),
                   jax.ShapeDtypeStruct((s, SSM_WIDTH), F32)]
        + [jax.ShapeDtypeStruct((1, D_MODEL), F32)] * 5 + [jax.ShapeDtypeStruct((D_MODEL, D_MODEL), F32)],
        compiler_params=_params())(dh2, x1, dy, mixed, attn, ssm, w_out_t16, pre_ffn_w, scale2, gate1, post_mix_w)


INPROJ_BWD_TILE = 512


def inproj_bwd(dq, dk, dv, dxbc, dz, ddt, x, dx1, h1, w_in_t16, pre_mix_w, scale1, tm=INPROJ_BWD_TILE):
    s = x.shape[0]
    tm = min(tm, s)

    def body(dq_ref, dk_ref, dv_ref, dxbc_ref, dz_ref, ddt_ref, x_ref, dx1_ref, h_ref, w_ref, pw_ref, sc_ref,
             gx_ref, dpw_ref, dsc_ref, dsh_ref, gw_ref):
        accs = (dpw_ref, dsc_ref, dsh_ref)

        @pl.when(pl.program_id(0) == 0)
        def _():
            for r in accs + (gw_ref,):
                r[...] = jnp.zeros_like(r)

        h16 = h_ref[...]
        dh = None
        dqkv = jnp.concatenate([dq_ref[...], dk_ref[...], dv_ref[...]], axis=1)
        for d16, lo, hi in ((dqkv, 0, OFF_XBC), (dxbc_ref[...], OFF_XBC, OFF_Z), (dz_ref[...], OFF_Z, OFF_DT),
                            (ddt_ref[...], OFF_DT, PROJ_PAD)):
            part = jnp.dot(d16, w_ref[lo:hi, :], preferred_element_type=F32)
            dh = part if dh is None else dh + part
            gw_ref[:, lo:hi] += _tn(h16, d16)
        dx, dpw, dsc, dsh = _norm_mod_bwd(x_ref[...], pw_ref[...], sc_ref[...], dh)
        gx_ref[...] = dx1_ref[...] + dx
        for r, v in zip(accs, (dpw, dsc, dsh)):
            r[...] += v

    tile = lambda n: pl.BlockSpec((tm, n), _row)
    return pl.pallas_call(
        body, name="inproj_bwd", grid=(s // tm,),
        in_specs=[tile(ATTN_WIDTH), tile(KV_WIDTH), tile(KV_WIDTH), tile(XBC_WIDTH), tile(SSM_WIDTH), tile(DT_PAD),
                  tile(D_MODEL), tile(D_MODEL), tile(D_MODEL), _resident((PROJ_PAD, D_MODEL))] + [_vec(D_MODEL)] * 2,
        out_specs=[tile(D_MODEL)] + [_vec(D_MODEL)] * 3 + [pl.BlockSpec((D_MODEL, PROJ_PAD), _const)],
        out_shape=[jax.ShapeDtypeStruct((s, D_MODEL), F32)] + [jax.ShapeDtypeStruct((1, D_MODEL), F32)] * 3
        + [jax.ShapeDtypeStruct((D_MODEL, PROJ_PAD), F32)],
        compiler_params=_params())(dq, dk, dv, dxbc, dz, ddt, x, dx1, h1, w_in_t16, pre_mix_w, scale1)


def _adam(g, w, m, v):
    new_m = ADAM_B1 * m + (1.0 - ADAM_B1) * g
    new_v = ADAM_B2 * v + (1.0 - ADAM_B2) * jnp.square(g)
    m_hat = new_m / (1.0 - ADAM_B1 ** ADAM_STEP)
    v_hat = new_v / (1.0 - ADAM_B2 ** ADAM_STEP)
    return -ADAM_LR * (m_hat / (jnp.sqrt(v_hat) + ADAM_EPS) + ADAM_WD * w), new_m, new_v


ROW_PARAMS = (("b_ada", 6144, 6144), ("pre_mix_w", 1024, 1024), ("attn_sinks", 128, 8), ("ssm_conv_b", 1024, 1024),
              ("ssm_dt_bias", 128, 8), ("ssm_a_log", 128, 8), ("ssm_d", 128, 8), ("ssm_norm_w", 512, 512),
              ("post_mix_w", 1024, 1024), ("pre_ffn_w", 1024, 1024), ("ffn_conv_b", 5632, 5632),
              ("post_ffn_w", 1024, 1024))
LOSS_LANES = 128


def adamw_small(row_all, rb_all, rel_bias_wmv, row_wmv):
    n_rows = len(ROW_PARAMS)

    def body(*refs):
        row_ref, rb_ref = refs[:2]
        wmv = refs[2:5 + 3 * n_rows]
        outs = refs[5 + 3 * n_rows:]
        g_row, g_rb = row_ref[0], rb_ref[0]
        for k in range(1, N_DEV):
            g_row = g_row + row_ref[k]
            g_rb = g_rb + rb_ref[k]
        outs[0][...] = g_row[:, :LOSS_LANES]
        grads = [g_rb[:, :N_Q_HEADS]]
        off = LOSS_LANES
        for _, lanes, width in ROW_PARAMS:
            grads.append(g_row[:, off:off + width])
            off += lanes
        for i, g in enumerate(grads):
            w_ref, m_ref, v_ref = wmv[3 * i:3 * i + 3]
            g_out, d_out, m_out, v_out = outs[1 + 4 * i:5 + 4 * i]
            g_out[...] = g
            d_out[...], m_out[...], v_out[...] = _adam(g, w_ref[...], m_ref[...], v_ref[...])

    flat_wmv = list(rel_bias_wmv) + [a for wmv in row_wmv for a in wmv]
    shapes = [jax.ShapeDtypeStruct((1, LOSS_LANES), F32)] + [jax.ShapeDtypeStruct((N_BUCKETS, N_Q_HEADS), F32)] * 4
    for _, _, width in ROW_PARAMS:
        shapes += [jax.ShapeDtypeStruct((1, width), F32)] * 4
    return pl.pallas_call(body, name="adamw_small", out_shape=shapes,
                          compiler_params=_params(n_axes=0))(row_all, rb_all, *flat_wmv)


def adamw(parts, w, m, v, name):
    p, r, n = parts.shape
    tr = _row_tile(r)

    def body(p_ref, w_ref, m_ref, v_ref, g_ref, d_ref, nm_ref, nv_ref):
        g = p_ref[0].astype(F32)
        for k in range(1, p):
            g = g + p_ref[k].astype(F32)
        g_ref[...] = g
        d_ref[...], nm_ref[...], nv_ref[...] = _adam(g, w_ref[...], m_ref[...], v_ref[...])

    tile = pl.BlockSpec((tr, n), _row)
    return pl.pallas_call(
        body, name=name, grid=(r // tr,),
        in_specs=[pl.BlockSpec((p, tr, n), lambda i: (0, i, 0)), tile, tile, tile],
        out_specs=[tile] * 4, out_shape=[jax.ShapeDtypeStruct((r, n), F32)] * 4,
        compiler_params=_params())(parts, w, m, v)


def _bucket_table():
    lq = ATTN_BLOCK
    qi = np.arange(lq)[:, None] + lq
    kj = np.arange(2 * lq)[None, :]
    dist = qi - kj
    d = np.maximum(dist, 0)
    max_exact = N_BUCKETS // 2
    nf = np.maximum(d, 1).astype(np.float32)
    large = max_exact + (np.log(nf / max_exact) / math.log(REL_MAX_DIST / max_exact)
                         * (N_BUCKETS - max_exact)).astype(np.int32)
    large = np.minimum(large, N_BUCKETS - 1)
    bucket = np.where(d < max_exact, d, large).astype(np.int32)
    in_band = (dist >= 0) & (dist < REL_MAX_DIST)
    return np.where(in_band, bucket, -1).astype(np.int32)


def _cols_from_blocks(g):
    return jnp.transpose(g, (1, 0, 2)).reshape(g.shape[1], N_DEV * g.shape[2])


def _cols_to_blocks(a):
    r, n = a.shape
    return jnp.transpose(a.reshape(r, N_DEV, n // N_DEV), (1, 0, 2))


def _perm_in_rows(wt):
    pad = jnp.zeros((DT_PAD - SSM_HEADS, wt.shape[1]), wt.dtype)
    return jnp.concatenate([wt[:768], wt[768:1280], wt[1792:2304], wt[1280:1792], wt[2304:2312], pad], axis=0)


def _unperm_in(g):
    return jnp.concatenate([g[:, :768], g[:, 768:1280], g[:, 1792:2304], g[:, 1280:1792], g[:, 2304:2312]], axis=1)


def _lane_pad(v, n=128):
    return jnp.pad(v, ((0, 0), (0, n - v.shape[1])))


def kernel(x, c, rel_bias, w_ada, b_ada, pre_mix_w, w_in, attn_sinks, ssm_conv_w, ssm_conv_b, ssm_dt_bias, ssm_a_log, ssm_d, ssm_norm_w, w_out, post_mix_w, pre_ffn_w, w_up, ffn_conv_w, ffn_conv_b, w_down, post_ffn_w, loss_target, m_rel_bias, m_w_ada, m_b_ada, m_pre_mix_w, m_w_in, m_attn_sinks, m_ssm_conv_w, m_ssm_conv_b, m_ssm_dt_bias, m_ssm_a_log, m_ssm_d, m_ssm_norm_w, m_w_out, m_post_mix_w, m_pre_ffn_w, m_w_up, m_ffn_conv_w, m_ffn_conv_b, m_w_down, m_post_ffn_w, v_rel_bias, v_w_ada, v_b_ada, v_pre_mix_w, v_w_in, v_attn_sinks, v_ssm_conv_w, v_ssm_conv_b, v_ssm_dt_bias, v_ssm_a_log, v_ssm_d, v_ssm_norm_w, v_w_out, v_post_mix_w, v_pre_ffn_w, v_w_up, v_ffn_conv_w, v_ffn_conv_b, v_w_down, v_post_ffn_w):
    weights = dict(rel_bias=rel_bias, w_ada=w_ada, b_ada=b_ada, pre_mix_w=pre_mix_w, w_in=w_in, attn_sinks=attn_sinks, ssm_conv_w=ssm_conv_w, ssm_conv_b=ssm_conv_b, ssm_dt_bias=ssm_dt_bias, ssm_a_log=ssm_a_log, ssm_d=ssm_d, ssm_norm_w=ssm_norm_w, w_out=w_out, post_mix_w=post_mix_w, pre_ffn_w=pre_ffn_w, w_up=w_up, ffn_conv_w=ffn_conv_w, ffn_conv_b=ffn_conv_b, w_down=w_down, post_ffn_w=post_ffn_w)
    mom_m = dict(rel_bias=m_rel_bias, w_ada=m_w_ada, b_ada=m_b_ada, pre_mix_w=m_pre_mix_w, w_in=m_w_in, attn_sinks=m_attn_sinks, ssm_conv_w=m_ssm_conv_w, ssm_conv_b=m_ssm_conv_b, ssm_dt_bias=m_ssm_dt_bias, ssm_a_log=m_ssm_a_log, ssm_d=m_ssm_d, ssm_norm_w=m_ssm_norm_w, w_out=m_w_out, post_mix_w=m_post_mix_w, pre_ffn_w=m_pre_ffn_w, w_up=m_w_up, ffn_conv_w=m_ffn_conv_w, ffn_conv_b=m_ffn_conv_b, w_down=m_w_down, post_ffn_w=m_post_ffn_w)
    mom_v = dict(rel_bias=v_rel_bias, w_ada=v_w_ada, b_ada=v_b_ada, pre_mix_w=v_pre_mix_w, w_in=v_w_in, attn_sinks=v_attn_sinks, ssm_conv_w=v_ssm_conv_w, ssm_conv_b=v_ssm_conv_b, ssm_dt_bias=v_ssm_dt_bias, ssm_a_log=v_ssm_a_log, ssm_d=v_ssm_d, ssm_norm_w=v_ssm_norm_w, w_out=v_w_out, post_mix_w=v_post_mix_w, pre_ffn_w=v_pre_ffn_w, w_up=v_w_up, ffn_conv_w=v_ffn_conv_w, ffn_conv_b=v_ffn_conv_b, w_down=v_w_down, post_ffn_w=v_post_ffn_w)
    order = ['rel_bias', 'w_ada', 'b_ada', 'pre_mix_w', 'w_in', 'attn_sinks', 'ssm_conv_w', 'ssm_conv_b', 'ssm_dt_bias', 'ssm_a_log', 'ssm_d', 'ssm_norm_w', 'w_out', 'post_mix_w', 'pre_ffn_w', 'w_up', 'ffn_conv_w', 'ffn_conv_b', 'w_down', 'post_ffn_w']

    me = 4 * lax.axis_index("x") + 2 * lax.axis_index("y") + lax.axis_index("c")
    xs_ = x[0]
    target = loss_target[0]

    (w_in_g, scw_g, fcw_g, c_g) = all_gather([_b(w_in[0]).T, ssm_conv_w[0], ffn_conv_w[0], c], "gather_weights")
    w_in_t16 = _perm_in_rows(w_in_g.reshape(IN_PROJ_WIDTH, D_MODEL))
    w_in16 = w_in_t16.T
    ssm_cw = _cols_from_blocks(scw_g)
    ffn_cw = _cols_from_blocks(fcw_g)
    c_all = c_g.reshape(N_DEV, D_MODEL)

    n_cols = w_ada.shape[2]
    b_cols = lax.dynamic_slice(b_ada, (0, me * n_cols), (1, n_cols))
    mod_part = ada_fwd(c_all, w_ada[0], b_cols)
    (mod_rows,) = all_to_all([mod_part.reshape(N_DEV, 1, n_cols)], "scatter_mod")
    mod = mod_rows.reshape(N_MOD, 1, D_MODEL)
    shift1, scale1, gate1, shift2, scale2, gate2 = (mod[i] for i in range(N_MOD))

    bucket_band = jnp.asarray(_bucket_table())
    bias = rel_bias_table(rel_bias, bucket_band)
    sinks_row = _lane_pad(attn_sinks)
    dt_bias, a_log, dskip = _lane_pad(ssm_dt_bias), _lane_pad(ssm_a_log), _lane_pad(ssm_d)

    h1, qkv, xbc_raw, z, dt_raw, w_out_g = pre_mix_inproj(
        xs_, pre_mix_w, scale1, shift1, w_in16, [(_b(w_out[0]), False)])
    attn, w_up_g = attn_fwd(qkv, bias, sinks_row, [(_b(w_up[0]).T, False)])
    ssm, hprev_all, w_down_g = ssm_fwd(xbc_raw, z, dt_raw, ssm_cw, ssm_conv_b, dt_bias, a_log, dskip, ssm_norm_w,
                                       [(_b(w_down[0]), False)])
    w_out16 = w_out_g.reshape(D_MODEL, D_MODEL)
    w_out_t16 = w_out16.T
    w_up_t16 = w_up_g.reshape(2 * D_FF, D_MODEL)
    w_up16 = w_up_t16.T
    w_down16 = w_down_g.reshape(D_FF, D_MODEL)
    w_down_t16 = w_down16.T
    mixed, x1, h2 = mix_out(attn, ssm, xs_, w_out16, gate1, post_mix_w, pre_ffn_w, scale2, shift2)
    u, u_raw16, f16 = up_gate(h2, w_up16, ffn_cw, ffn_conv_b)
    d_ffn, dy, loss_part, d_gate2, d_post_ffn_w, g_w_down = down_loss(f16, w_down16, x1, target, gate2, post_ffn_w)

    du_raw, dh2, d_ffn_cw, d_ffn_cb = ffn_bwd(u, u_raw16, d_ffn, ffn_cw, w_down_t16, w_up_t16)
    g_w_up = matmul_tn(h2, du_raw, "grad_w_up", D_MODEL, FF_CHUNK, tk=2048)
    (dx1, d_attn, d_ssm, d_pre_ffn_w, d_scale2, d_shift2, d_gate1, d_post_mix_w, g_w_out) = mix_bwd(
        dh2, x1, dy, mixed, attn, ssm, w_out_t16, pre_ffn_w, scale2, gate1, post_mix_w)
    dq, dk, dv, dbias, dsinks, p_w_down = attn_bwd(
        qkv, bias, sinks_row, d_attn, [(g_w_down.reshape(N_DEV, D_FF // N_DEV, D_MODEL), True)])
    d_rel_bias = rel_bias_grad(dbias, bucket_band)
    (dxbc, dz, ddt, d_ssm_cw, d_ssm_cb, d_dt_bias, d_a_log, d_dskip, d_norm_w, p_w_up, p_w_out) = ssm_bwd(
        xbc_raw, z, dt_raw, hprev_all, d_ssm, ssm_cw, ssm_conv_b, dt_bias, a_log, dskip, ssm_norm_w,
        [(_cols_to_blocks(g_w_up), True), (g_w_out.reshape(N_DEV, D_MODEL // N_DEV, D_MODEL), True)])
    grad_x, d_pre_mix_w, d_scale1, d_shift1, g_w_in_perm = inproj_bwd(
        dq, dk, dv, dxbc, dz, ddt, xs_, dx1, h1, w_in_t16, pre_mix_w, scale1)
    g_w_in = _unperm_in(g_w_in_perm)

    d_mod = jnp.concatenate([d_shift1, d_scale1, d_gate1, d_shift2, d_scale2, d_gate2], axis=1)
    late = ("w_in", "ssm_conv_w", "ffn_conv_w")
    full = [_cols_to_blocks(g_w_in), _cols_to_blocks(d_ssm_cw[:SSM_CONV]), _cols_to_blocks(d_ffn_cw[:FFN_CONV])]
    core = lax.axis_index("c").astype(jnp.int32).reshape(1)
    got = pair_exchange(full, "pair_grads")
    chip_sums = [pair_sum(f_, g_, core, "pair_sum_" + k) for k, f_, g_ in zip(late, full, got)]
    chip_parts = all_to_all(chip_sums, "scatter_grads", CHIP_FLIPS, _chip_index)
    (d_mod_rows,) = all_to_all([d_mod.reshape(N_DEV, 1, n_cols)], "scatter_dmod")
    g_w_ada = ada_bwd(c_all, d_mod_rows.reshape(N_DEV, n_cols))

    row_g = dict(b_ada=d_mod, pre_mix_w=d_pre_mix_w, attn_sinks=dsinks, ssm_conv_b=d_ssm_cb, ssm_dt_bias=d_dt_bias,
                 ssm_a_log=d_a_log, ssm_d=d_dskip, ssm_norm_w=d_norm_w, post_mix_w=d_post_mix_w,
                 pre_ffn_w=d_pre_ffn_w, ffn_conv_b=d_ffn_cb, post_ffn_w=d_post_ffn_w)
    row = jnp.concatenate([loss_part] + [row_g[k] for k, _, _ in ROW_PARAMS], axis=1)
    row_all, rb_all = all_gather([row, d_rel_bias], "gather_small")

    wmv = lambda k: (weights[k], mom_m[k], mom_v[k])
    small = adamw_small(row_all, rb_all, wmv("rel_bias"), [wmv(k) for k, _, _ in ROW_PARAMS])
    loss = small[0][0, 0]
    res = {k: tuple(small[1 + 4 * i:5 + 4 * i]) for i, k in enumerate(["rel_bias"] + [k for k, _, _ in ROW_PARAMS])}
    big = list(zip(late, chip_parts)) + [("w_down", p_w_down), ("w_up", p_w_up), ("w_out", p_w_out),
                                        ("w_ada", g_w_ada[None])]
    for k, parts in big:
        res[k] = tuple(o[None] for o in adamw(parts, weights[k][0], mom_m[k][0], mom_v[k][0], "adamw_" + k))

    outs = [loss, grad_x[None]]
    for field in range(4):
        outs += [res[k][field] for k in order]
    return tuple(outs)
```

```python
import math

import numpy as np
import jax
import jax.numpy as jnp
from jax import lax
from jax.experimental import pallas as pl
from jax.experimental.pallas import tpu as pltpu

F32 = jnp.float32
BF16 = jnp.bfloat16
MESH_ID = pl.DeviceIdType.MESH

N_DEV = 8
D_MODEL = 1024
N_Q_HEADS = 8
N_KV_HEADS = 2
HEAD_DIM = 64
ATTN_WIDTH = 512
KV_WIDTH = 128
ATTN_BLOCK = 128
N_BUCKETS = 32
REL_MAX_DIST = 128
SSM_HEADS = 8
SSM_HEAD_DIM = 64
SSM_WIDTH = 512
SSM_STATE = 128
SSM_GROUPS = 2
SSM_BC = 256
SSM_CONV = 4
SSM_CHUNK = 256
XBC_WIDTH = SSM_WIDTH + 2 * SSM_BC
D_FF = 2816
FFN_CONV = 3
NORM_EPS = 1e-6
N_MOD = 6
IN_PROJ_WIDTH = 2312
QKV_W = ATTN_WIDTH + 2 * KV_WIDTH
OFF_XBC = QKV_W
OFF_Z = OFF_XBC + XBC_WIDTH
OFF_DT = OFF_Z + SSM_WIDTH
DT_PAD = 128
PROJ_PAD = OFF_DT + DT_PAD
FF_CHUNK = 1408

ADAM_LR = 0.001
ADAM_B1 = 0.9
ADAM_B2 = 0.999
ADAM_EPS = 1e-08
ADAM_WD = 0.01
ADAM_STEP = 10

TOKEN_TILE = 256
HALO = 8
NEXT = 16
VMEM_LIMIT = 56 * 1024 * 1024


def _params(vmem=VMEM_LIMIT, n_axes=1):
    return pltpu.CompilerParams(dimension_semantics=("arbitrary",) * n_axes, vmem_limit_bytes=vmem)


def _b(x):
    return x.astype(BF16)


def _nn(a, b):
    return jnp.dot(_b(a), _b(b), preferred_element_type=F32)


def _nt(a, b):
    return lax.dot_general(_b(a), _b(b), (((1,), (1,)), ((), ())), preferred_element_type=F32)


def _tn(a, b):
    return lax.dot_general(_b(a), _b(b), (((0,), (0,)), ((), ())), preferred_element_type=F32)


@jax.custom_vjp
def mm(a, b):
    return _nn(a, b)


mm.defvjp(lambda a, b: (_nn(a, b), (a, b)),
          lambda r, g: (_nt(g, r[1]).astype(r[0].dtype), _tn(r[0], g).astype(r[1].dtype)))


@jax.custom_vjp
def mm_nt(a, b):
    return _nt(a, b)


mm_nt.defvjp(lambda a, b: (_nt(a, b), (a, b)),
             lambda r, g: (_nn(g, r[1]).astype(r[0].dtype), _tn(g, r[0]).astype(r[1].dtype)))


@jax.custom_vjp
def mm_tn(a, b):
    return _tn(a, b)


mm_tn.defvjp(lambda a, b: (_tn(a, b), (a, b)),
             lambda r, g: (_nt(r[1], g).astype(r[0].dtype), _nn(r[0], g).astype(r[1].dtype)))


def _rms(x, w):
    return x * lax.rsqrt(jnp.mean(x * x, axis=-1, keepdims=True) + NORM_EPS) * w


def _norm_mod(x, w, scale, shift):
    return _rms(x, w) * (1.0 + scale) + shift


def _rms_bwd(x, w, dy):
    r = lax.rsqrt(jnp.mean(x * x, axis=-1, keepdims=True) + NORM_EPS)
    xhat = x * r
    g = dy * w
    dx = r * (g - xhat * jnp.mean(g * xhat, axis=-1, keepdims=True))
    return dx, jnp.sum(dy * xhat, axis=0, keepdims=True)


def _norm_mod_bwd(x, w, scale, dh):
    dx, da = _rms_bwd(x, w * (1.0 + scale), dh)
    return dx, da * (1.0 + scale), da * w, jnp.sum(dh, axis=0, keepdims=True)


def _gated_rms_bwd(m, gate, w, dy):
    dm, t = _rms_bwd(m, w * gate, dy)
    return dm, t * w, t * gate


def _silu(x):
    return x * jax.nn.sigmoid(x)


def _conv_rows(xin, w, k):
    acc = xin * w[k - 1:k, :]
    for j in range(1, k):
        acc = acc + pltpu.roll(xin, j, axis=0) * w[k - 1 - j:k - j, :]
    return acc


def _conv_rows_t(du, w, k):
    n = du.shape[0]
    acc = du * w[k - 1:k, :]
    for j in range(1, k):
        acc = acc + pltpu.roll(du, n - j, axis=0) * w[k - 1 - j:k - j, :]
    return acc


def _row(i):
    return (i, 0)


def _const(i):
    return (0, 0)


def _vec(n):
    return pl.BlockSpec((1, n), _const)


def _resident(shape):
    return pl.BlockSpec(shape, _const, pipeline_mode=pl.Buffered(1))


def _block_index(p):
    return 4 * p[0] + 2 * p[1] + p[2]


def all_gather(arrs, name):
    n = len(arrs)

    def body(*refs):
        ins, outs = refs[:n], refs[n:2 * n]
        send_sems, recv_sems, local_sems = refs[2 * n:]
        x, y, c = lax.axis_index("x"), lax.axis_index("y"), lax.axis_index("c")
        me, sibling = (x, y, c), (x, y, 1 - c)
        chips = [(1 - x, y), (x, 1 - y), (1 - x, 1 - y)]

        def copy(a, k, block, to, src=None):
            dst = outs[a].at[_block_index(block)]
            return pltpu.make_async_remote_copy(
                src_ref=dst if src is None else src, dst_ref=dst,
                send_sem=send_sems.at[a * 7 + k], recv_sem=recv_sems.at[a * 7 + k],
                device_id=to, device_id_type=MESH_ID)

        mine = [pltpu.make_async_copy(ins[a], outs[a].at[_block_index(me)], local_sems.at[a]) for a in range(n)]
        for cp in mine:
            cp.start()
        first = []
        for a in range(n):
            first.append(copy(a, 0, me, sibling, src=ins[a]))
            first += [copy(a, 1 + j, me, (*chip, c), src=ins[a]) for j, chip in enumerate(chips)]
        for cp in first:
            cp.start()
        passed = []
        for j, chip in enumerate(chips):
            for a in range(n):
                copy(a, 1 + j, (*chip, c), me).wait_recv()
                cp = copy(a, 4 + j, (*chip, c), sibling)
                cp.start()
                passed.append(cp)
        for a in range(n):
            copy(a, 0, sibling, me).wait_recv()
            for j, chip in enumerate(chips):
                copy(a, 4 + j, (*chip, 1 - c), me).wait_recv()
        for cp in first + passed:
            cp.wait_send()
        for cp in mine:
            cp.wait()

    any_spec = pl.BlockSpec(memory_space=pl.ANY)
    return pl.pallas_call(
        body, name=name,
        out_shape=[jax.ShapeDtypeStruct((N_DEV,) + a.shape, a.dtype) for a in arrs],
        in_specs=[any_spec] * n, out_specs=[any_spec] * n,
        scratch_shapes=[pltpu.SemaphoreType.DMA((7 * n,)), pltpu.SemaphoreType.DMA((7 * n,)),
                        pltpu.SemaphoreType.DMA((n,))],
    )(*arrs)


ALL_FLIPS = ((0, 0, 1), (0, 1, 0), (0, 1, 1), (1, 0, 0), (1, 0, 1), (1, 1, 0), (1, 1, 1))
CHIP_FLIPS = ((0, 1, 0), (1, 0, 0), (1, 1, 0))


def _chip_index(p):
    return 2 * p[0] + p[1]


def all_to_all(arrs, name, flips=ALL_FLIPS, index=_block_index):
    n = len(arrs)
    nf = len(flips)

    def body(*refs):
        ins, outs = refs[:n], refs[n:2 * n]
        send_sems, recv_sems, local_sems = refs[2 * n:]
        pos = (lax.axis_index("x"), lax.axis_index("y"), lax.axis_index("c"))
        me = index(pos)
        peers = [tuple(1 - p if f else p for p, f in zip(pos, flip)) for flip in flips]

        def copy(a, k):
            peer = peers[k]
            return pltpu.make_async_remote_copy(
                src_ref=ins[a].at[index(peer)], dst_ref=outs[a].at[me],
                send_sem=send_sems.at[a * nf + k], recv_sem=recv_sems.at[a * nf + k],
                device_id=peer, device_id_type=MESH_ID)

        def landed(a, k):
            slot = outs[a].at[index(peers[k])]
            return pltpu.make_async_remote_copy(
                src_ref=slot, dst_ref=slot,
                send_sem=send_sems.at[a * nf + k], recv_sem=recv_sems.at[a * nf + k],
                device_id=peers[k], device_id_type=MESH_ID)

        mine = [pltpu.make_async_copy(ins[a].at[me], outs[a].at[me], local_sems.at[a]) for a in range(n)]
        for cp in mine:
            cp.start()
        sent = [copy(a, k) for a in range(n) for k in range(nf)]
        for cp in sent:
            cp.start()
        for a in range(n):
            for k in range(nf):
                landed(a, k).wait_recv()
        for cp in sent:
            cp.wait_send()
        for cp in mine:
            cp.wait()

    any_spec = pl.BlockSpec(memory_space=pl.ANY)
    return pl.pallas_call(
        body, name=name,
        out_shape=[jax.ShapeDtypeStruct(a.shape, a.dtype) for a in arrs],
        in_specs=[any_spec] * n, out_specs=[any_spec] * n,
        scratch_shapes=[pltpu.SemaphoreType.DMA((nf * n,)), pltpu.SemaphoreType.DMA((nf * n,)),
                        pltpu.SemaphoreType.DMA((n,))],
    )(*arrs)


def _direct_exchange(src, dst, sems, scatter):
    send_sems, recv_sems, local_sem = sems
    pos = (lax.axis_index("x"), lax.axis_index("y"), lax.axis_index("c"))
    me = _block_index(pos)
    peers = [tuple(1 - p if f else p for p, f in zip(pos, flip)) for flip in ALL_FLIPS]

    def outgoing(k):
        return pltpu.make_async_remote_copy(
            src_ref=src.at[_block_index(peers[k])] if scatter else src, dst_ref=dst.at[me],
            send_sem=send_sems.at[k], recv_sem=recv_sems.at[k], device_id=peers[k], device_id_type=MESH_ID)

    def incoming(k):
        slot = dst.at[_block_index(peers[k])]
        return pltpu.make_async_remote_copy(
            src_ref=slot, dst_ref=slot, send_sem=send_sems.at[k], recv_sem=recv_sems.at[k],
            device_id=peers[k], device_id_type=MESH_ID)

    def local():
        return pltpu.make_async_copy(src.at[me] if scatter else src, dst.at[me], local_sem)

    def start():
        local().start()
        for k in range(len(ALL_FLIPS)):
            outgoing(k).start()

    def finish():
        for k in range(len(ALL_FLIPS)):
            incoming(k).wait_recv()
        for k in range(len(ALL_FLIPS)):
            outgoing(k).wait_send()
        local().wait()

    return start, finish


def hosted_call(body, exchanges, steps, n_in, n_out, **call):
    n_ex = len(exchanges)

    def wrapped(*refs):
        ins, srcs = refs[:n_in], refs[n_in:n_in + n_ex]
        outs = refs[n_in + n_ex:n_in + n_ex + n_out]
        dsts = refs[n_in + n_ex + n_out:n_in + 2 * n_ex + n_out]
        rest = refs[n_in + 2 * n_ex + n_out:]
        scratch, sems = rest[:len(rest) - 3 * n_ex], rest[len(rest) - 3 * n_ex:]
        plans = [_direct_exchange(srcs[e], dsts[e], sems[3 * e:3 * e + 3], exchanges[e][1]) for e in range(n_ex)]

        @pl.when(pl.program_id(0) == 0)
        def _():
            for start, _ in plans:
                start()

        body(*ins, *outs, *scratch)

        @pl.when(pl.program_id(0) == steps - 1)
        def _():
            for _, finish in plans:
                finish()

    any_spec = pl.BlockSpec(memory_space=pl.ANY)
    landings = [jax.ShapeDtypeStruct(src.shape if scatter else (N_DEV,) + src.shape, src.dtype)
                for src, scatter in exchanges]
    n_flips = len(ALL_FLIPS)
    sems = [pltpu.SemaphoreType.DMA((n_flips,)), pltpu.SemaphoreType.DMA((n_flips,)), pltpu.SemaphoreType.DMA(())]
    return pl.pallas_call(
        wrapped, grid=(steps,),
        in_specs=list(call.pop("in_specs")) + [any_spec] * n_ex,
        out_specs=list(call.pop("out_specs")) + [any_spec] * n_ex,
        out_shape=list(call.pop("out_shape")) + landings,
        scratch_shapes=list(call.pop("scratch_shapes", [])) + sems * n_ex,
        **call)


def _grid_call(body, steps, args, exchanges, **call):
    if not exchanges:
        return pl.pallas_call(body, grid=(steps,), **call)(*args)
    srcs = [src for src, _ in exchanges]
    return hosted_call(body, exchanges, steps, len(args), len(call["out_shape"]), **call)(*args, *srcs)


N_CHIPS = 4


def pair_exchange(arrs, name):
    n = len(arrs)

    def body(*refs):
        ins, outs = refs[:n], refs[n:2 * n]
        send_sems, recv_sems = refs[2 * n:]
        x, y, c = lax.axis_index("x"), lax.axis_index("y"), lax.axis_index("c")
        sibling = (x, y, 1 - c)
        sent = []
        for a in range(n):
            for q in range(N_CHIPS):
                cp = pltpu.make_async_remote_copy(
                    src_ref=ins[a].at[2 * q + (1 - c)], dst_ref=outs[a].at[q],
                    send_sem=send_sems.at[a * N_CHIPS + q], recv_sem=recv_sems.at[a * N_CHIPS + q],
                    device_id=sibling, device_id_type=MESH_ID)
                cp.start()
                sent.append(cp)
        for cp in sent:
            cp.wait_recv()
        for cp in sent:
            cp.wait_send()

    any_spec = pl.BlockSpec(memory_space=pl.ANY)
    return pl.pallas_call(
        body, name=name,
        out_shape=[jax.ShapeDtypeStruct((N_CHIPS,) + a.shape[1:], a.dtype) for a in arrs],
        in_specs=[any_spec] * n, out_specs=[any_spec] * n,
        scratch_shapes=[pltpu.SemaphoreType.DMA((N_CHIPS * n,)), pltpu.SemaphoreType.DMA((N_CHIPS * n,))],
    )(*arrs)


def pair_sum(full, got, core, name):
    _, r, n = full.shape
    tr = _row_tile(r)

    def body(c_ref, mine_ref, got_ref, o_ref):
        o_ref[...] = _b(mine_ref[...] + got_ref[...])

    grid_spec = pltpu.PrefetchScalarGridSpec(
        num_scalar_prefetch=1, grid=(N_CHIPS, r // tr),
        in_specs=[pl.BlockSpec((1, tr, n), lambda q, i, c_ref: (2 * q + c_ref[0], i, 0)),
                  pl.BlockSpec((1, tr, n), lambda q, i, c_ref: (q, i, 0))],
        out_specs=pl.BlockSpec((1, tr, n), lambda q, i, c_ref: (q, i, 0)))
    return pl.pallas_call(body, name=name, grid_spec=grid_spec,
                          out_shape=jax.ShapeDtypeStruct((N_CHIPS, r, n), BF16),
                          compiler_params=_params(n_axes=2))(core, full, got)


def _row_tile(r):
    for cand in (256, 128, 64, 32, 16):
        if r % cand == 0 and r > cand:
            return cand
    return r


def ada_fwd(c_all, w_ada, b_cols):
    def body(c_ref, w_ref, b_ref, o_ref):
        o_ref[...] = _nn(_silu(c_ref[...]), w_ref[...]) + b_ref[...]

    return pl.pallas_call(body, name="ada_fwd",
                          out_shape=jax.ShapeDtypeStruct((N_DEV, w_ada.shape[1]), F32),
                          compiler_params=_params(n_axes=0))(c_all, w_ada, b_cols)


def ada_bwd(c_all, g_cols):
    def body(c_ref, g_ref, o_ref):
        o_ref[...] = _tn(_silu(c_ref[...]), g_ref[...])

    return pl.pallas_call(body, name="ada_bwd",
                          out_shape=jax.ShapeDtypeStruct((c_all.shape[1], g_cols.shape[1]), F32),
                          compiler_params=_params(n_axes=0))(c_all, g_cols)


def matmul_tn(a, b, name, bm, bn, tk=512):
    s, m = a.shape
    n = b.shape[1]
    tk = min(tk, s)

    def body(a_ref, b_ref, o_ref):
        @pl.when(pl.program_id(2) == 0)
        def _():
            o_ref[...] = jnp.zeros_like(o_ref)

        o_ref[...] += _tn(a_ref[...], b_ref[...])

    return pl.pallas_call(
        body, name=name, grid=(m // bm, n // bn, s // tk),
        in_specs=[pl.BlockSpec((tk, bm), lambda i, j, k: (k, i)), pl.BlockSpec((tk, bn), lambda i, j, k: (k, j))],
        out_specs=pl.BlockSpec((bm, bn), lambda i, j, k: (i, j)),
        out_shape=jax.ShapeDtypeStruct((m, n), F32),
        compiler_params=_params(n_axes=3))(a, b)


def pre_mix_inproj(x, w, scale, shift, w_in16, exchange=None, tm=4 * TOKEN_TILE):
    s = x.shape[0]
    tm = min(tm, s)

    def body(x_ref, w_ref, sc_ref, sh_ref, win_ref, h_ref, qkv_ref, xbc_ref, z_ref, dt_ref):
        h16 = _b(_norm_mod(x_ref[...], w_ref[...], sc_ref[...], sh_ref[...]))
        h_ref[...] = h16
        dot = lambda lo, hi: jnp.dot(h16, win_ref[:, lo:hi], preferred_element_type=F32)
        qkv_ref[...] = _b(dot(0, OFF_XBC))
        xbc_ref[...] = dot(OFF_XBC, OFF_Z)
        z_ref[...] = dot(OFF_Z, OFF_DT)
        dt_ref[...] = dot(OFF_DT, PROJ_PAD)

    tile = lambda n: pl.BlockSpec((tm, n), _row)
    return _grid_call(
        body, s // tm, (x, w, scale, shift, w_in16), exchange, name="pre_mix_inproj",
        in_specs=[tile(D_MODEL), _vec(D_MODEL), _vec(D_MODEL), _vec(D_MODEL), _resident((D_MODEL, PROJ_PAD))],
        out_specs=[tile(D_MODEL), tile(QKV_W), tile(XBC_WIDTH), tile(SSM_WIDTH), tile(DT_PAD)],
        out_shape=[jax.ShapeDtypeStruct((s, D_MODEL), BF16), jax.ShapeDtypeStruct((s, QKV_W), BF16),
                   jax.ShapeDtypeStruct((s, XBC_WIDTH), F32), jax.ShapeDtypeStruct((s, SSM_WIDTH), F32),
                   jax.ShapeDtypeStruct((s, DT_PAD), F32)],
        compiler_params=_params())


ATTN_QB_FWD, ATTN_QB_BWD = 4, 2


def _attn_tile(q, kp, kc, vp, vc, bias, sinks):
    lq = ATTN_BLOCK
    group = N_Q_HEADS // N_KV_HEADS
    lanes = lax.broadcasted_iota(jnp.int32, (1, 128), 1)
    rid = lax.broadcasted_iota(jnp.int32, (group * lq, 1), 0)
    sink_cols = []
    for hk in range(N_KV_HEADS):
        sink = jnp.zeros((group * lq, 1), F32)
        for g in range(group):
            s_h = jnp.sum(jnp.where(lanes == hk * group + g, sinks, 0.0), axis=-1, keepdims=True)
            sink = jnp.where((rid >= g * lq) & (rid < (g + 1) * lq), s_h, sink)
        sink_cols.append(sink)
    kall = jnp.concatenate([kp, kc], axis=0)
    vall = jnp.concatenate([vp, vc], axis=0)
    blocks = []
    for b in range(q.shape[0] // lq):
        qb = q[b * lq:(b + 1) * lq]
        outs = []
        for hk in range(N_KV_HEADS):
            cols = slice(hk * HEAD_DIM, (hk + 1) * HEAD_DIM)
            kb = kall[b * lq:(b + 2) * lq, cols]
            vb = vall[b * lq:(b + 2) * lq, cols]
            qg = jnp.concatenate([qb[:, (hk * group + g) * HEAD_DIM:(hk * group + g + 1) * HEAD_DIM]
                                  for g in range(group)], axis=0)
            sc = mm_nt(qg, kb) * (HEAD_DIM ** -0.5) + bias[b][hk]
            sink = sink_cols[hk]
            m = lax.stop_gradient(jnp.maximum(jnp.max(sc, axis=-1, keepdims=True), sink))
            p = jnp.exp(sc - m)
            probs = p / (jnp.sum(p, axis=-1, keepdims=True) + jnp.exp(sink - m))
            og = mm(probs, vb)
            outs += [og[g * lq:(g + 1) * lq] for g in range(group)]
        blocks.append(jnp.concatenate(outs, axis=1))
    return jnp.concatenate(blocks, axis=0)


def _attn_tile_bwd(q, kp, kc, vp, vc, bias, sinks, do):
    lq = ATTN_BLOCK
    group = N_Q_HEADS // N_KV_HEADS
    scale = HEAD_DIM ** -0.5
    lanes = lax.broadcasted_iota(jnp.int32, (1, 128), 1)
    rid = lax.broadcasted_iota(jnp.int32, (group * lq, 1), 0)
    sink_cols = []
    for hk in range(N_KV_HEADS):
        sink = jnp.zeros((group * lq, 1), F32)
        for g in range(group):
            s_h = jnp.sum(jnp.where(lanes == hk * group + g, sinks, 0.0), axis=-1, keepdims=True)
            sink = jnp.where((rid >= g * lq) & (rid < (g + 1) * lq), s_h, sink)
        sink_cols.append(sink)
    kall = jnp.concatenate([kp, kc], axis=0)
    vall = jnp.concatenate([vp, vc], axis=0)
    dsk = jnp.zeros((1, 128), F32)
    dq_blocks, dbias = [], []
    nqb = q.shape[0] // lq
    dk_parts = [[None] * nqb for _ in range(N_KV_HEADS)]
    dv_parts = [[None] * nqb for _ in range(N_KV_HEADS)]
    for b in range(nqb):
        qb, dob = q[b * lq:(b + 1) * lq], do[b * lq:(b + 1) * lq]
        dq_heads, dbias_b = [], []
        for hk in range(N_KV_HEADS):
            cols = slice(hk * HEAD_DIM, (hk + 1) * HEAD_DIM)
            kb = kall[b * lq:(b + 2) * lq, cols]
            vb = vall[b * lq:(b + 2) * lq, cols]
            heads = [hk * group + g for g in range(group)]
            qg = jnp.concatenate([qb[:, h * HEAD_DIM:(h + 1) * HEAD_DIM] for h in heads], axis=0)
            dog = jnp.concatenate([dob[:, h * HEAD_DIM:(h + 1) * HEAD_DIM] for h in heads], axis=0)
            sink = sink_cols[hk]
            sc = _nt(qg, kb) * scale + bias[b][hk]
            m = jnp.maximum(jnp.max(sc, axis=-1, keepdims=True), sink)
            p = jnp.exp(sc - m)
            es = jnp.exp(sink - m)
            inv = 1.0 / (jnp.sum(p, axis=-1, keepdims=True) + es)
            probs = p * inv
            dprobs = _nt(dog, vb)
            delta = jnp.sum(probs * dprobs, axis=-1, keepdims=True)
            dsc = probs * (dprobs - delta)
            dbias_b.append(dsc)
            dsink = -(es * inv) * delta
            for g, h in enumerate(heads):
                tot = jnp.sum(dsink[g * lq:(g + 1) * lq], axis=0, keepdims=True)
                dsk = dsk + jnp.where(lanes == h, tot, 0.0)
            dqg = _nn(dsc, kb) * scale
            dq_heads += [dqg[g * lq:(g + 1) * lq] for g in range(group)]
            dk_parts[hk][b] = _tn(dsc, qg) * scale
            dv_parts[hk][b] = _tn(probs, dog)
        dq_blocks.append(jnp.concatenate(dq_heads, axis=1))
        dbias.append(dbias_b)

    def overlap_add(parts):
        chunks = []
        for r in range(nqb + 1):
            acc = None
            if r < nqb:
                acc = parts[r][:lq]
            if r >= 1:
                tail = parts[r - 1][lq:]
                acc = tail if acc is None else acc + tail
            chunks.append(acc)
        return jnp.concatenate(chunks, axis=0)

    dkall = jnp.concatenate([overlap_add(dk_parts[hk]) for hk in range(N_KV_HEADS)], axis=1)
    dvall = jnp.concatenate([overlap_add(dv_parts[hk]) for hk in range(N_KV_HEADS)], axis=1)
    return jnp.concatenate(dq_blocks, axis=0), dkall, dvall, dbias, dsk


def _attn_in_specs(nt, clamp, nqb):
    lq, tq = ATTN_BLOCK, ATTN_BLOCK * nqb
    cur = lambda n: jnp.minimum(n, nt - 1) if clamp else n
    prev = lambda n: jnp.maximum(cur(n) * nqb - 1, 0)
    kcol, vcol = ATTN_WIDTH // KV_WIDTH, ATTN_WIDTH // KV_WIDTH + 1
    return [pl.BlockSpec((tq, ATTN_WIDTH), lambda n: (cur(n), 0)),
            pl.BlockSpec((lq, KV_WIDTH), lambda n: (prev(n), kcol)),
            pl.BlockSpec((tq, KV_WIDTH), lambda n: (cur(n), kcol)),
            pl.BlockSpec((lq, KV_WIDTH), lambda n: (prev(n), vcol)),
            pl.BlockSpec((tq, KV_WIDTH), lambda n: (cur(n), vcol)),
            pl.BlockSpec((2, N_KV_HEADS, 4 * lq, 2 * lq), lambda n: (0, 0, 0, 0)),
            _vec(128)]


def _tile_bias(bias_ref, first, nqb):
    return [[jnp.where(first, bias_ref[1, hk], bias_ref[0, hk]) if b == 0 else bias_ref[0, hk]
             for hk in range(N_KV_HEADS)] for b in range(nqb)]


def attn_fwd(qkv, bias, sinks_rows, exchange=None):
    s = qkv.shape[0]
    nqb = min(ATTN_QB_FWD, s // ATTN_BLOCK)
    tq = ATTN_BLOCK * nqb
    nt = s // tq

    def body(q_ref, kp_ref, kc_ref, vp_ref, vc_ref, bias_ref, sk_ref, o_ref):
        f = lambda r: r[...].astype(F32)
        o = _attn_tile(f(q_ref), f(kp_ref), f(kc_ref), f(vp_ref), f(vc_ref),
                       _tile_bias(bias_ref, pl.program_id(0) == 0, nqb), sk_ref[...])
        o_ref[...] = _b(o)

    return _grid_call(
        body, nt, (qkv, qkv, qkv, qkv, qkv, bias, sinks_rows), exchange, name="attn_fwd",
        in_specs=_attn_in_specs(nt, False, nqb),
        out_specs=[pl.BlockSpec((tq, ATTN_WIDTH), _row)],
        out_shape=[jax.ShapeDtypeStruct((s, ATTN_WIDTH), BF16)],
        compiler_params=_params())


def attn_bwd(qkv, bias, sinks_rows, d_attn, exchange=None):
    s = qkv.shape[0]
    nqb = ATTN_QB_BWD
    lq, tq = ATTN_BLOCK, ATTN_BLOCK * nqb
    nt = s // tq

    def body(q_ref, kp_ref, kc_ref, vp_ref, vc_ref, bias_ref, sk_ref, do_ref,
             dq_ref, dk_ref, dv_ref, dbias_ref, dsk_ref, carry_k, carry_v):
        n = pl.program_id(0)

        @pl.when(n == 0)
        def _():
            dbias_ref[...] = jnp.zeros_like(dbias_ref)
            dsk_ref[...] = jnp.zeros_like(dsk_ref)
            carry_k[...] = jnp.zeros_like(carry_k)
            carry_v[...] = jnp.zeros_like(carry_v)

        @pl.when(n < nt)
        def _():
            f = lambda r: r[...].astype(F32)
            dq, dkall, dvall, dbias, dsk = _attn_tile_bwd(
                f(q_ref), f(kp_ref), f(kc_ref), f(vp_ref), f(vc_ref), _tile_bias(bias_ref, n == 0, nqb), sk_ref[...],
                f(do_ref))
            dkp, dkc, dvp, dvc = dkall[:lq], dkall[lq:], dvall[:lq], dvall[lq:]
            dq_ref[...] = _b(dq)
            done = tq - lq
            dk_ref[:done, :] = _b(carry_k[:done, :])
            dv_ref[:done, :] = _b(carry_v[:done, :])
            dk_ref[done:, :] = _b(carry_k[done:, :] + dkp)
            dv_ref[done:, :] = _b(carry_v[done:, :] + dvp)
            carry_k[...] = dkc
            carry_v[...] = dvc
            dsk_ref[...] += dsk
            first = (n == 0).astype(F32)
            for hk in range(N_KV_HEADS):
                total = dbias[0][hk]
                for b in range(1, nqb):
                    total = total + dbias[b][hk]
                dbias_ref[0, hk] += total - first * dbias[0][hk]
                dbias_ref[1, hk] += first * dbias[0][hk]

        @pl.when(n == nt)
        def _():
            dk_ref[...] = _b(carry_k[...])
            dv_ref[...] = _b(carry_v[...])

    cur = lambda n: (jnp.minimum(n, nt - 1), 0)
    done_map = lambda n: (jnp.maximum(n - 1, 0), 0)
    return _grid_call(
        body, nt + 1, (qkv, qkv, qkv, qkv, qkv, bias, sinks_rows, d_attn), exchange, name="attn_bwd",
        in_specs=_attn_in_specs(nt, True, nqb) + [pl.BlockSpec((tq, ATTN_WIDTH), cur)],
        out_specs=[pl.BlockSpec((tq, ATTN_WIDTH), cur), pl.BlockSpec((tq, KV_WIDTH), done_map),
                   pl.BlockSpec((tq, KV_WIDTH), done_map),
                   pl.BlockSpec((2, N_KV_HEADS, 4 * lq, 2 * lq), lambda n: (0, 0, 0, 0)), _vec(128)],
        out_shape=[jax.ShapeDtypeStruct((s, ATTN_WIDTH), BF16), jax.ShapeDtypeStruct((s, KV_WIDTH), BF16),
                   jax.ShapeDtypeStruct((s, KV_WIDTH), BF16),
                   jax.ShapeDtypeStruct((2, N_KV_HEADS, 4 * lq, 2 * lq), F32), jax.ShapeDtypeStruct((1, 128), F32)],
        scratch_shapes=[pltpu.VMEM((tq, KV_WIDTH), F32), pltpu.VMEM((tq, KV_WIDTH), F32)],
        compiler_params=_params())


def rel_bias_table(rel_bias, bucket):
    lq = ATTN_BLOCK
    group = N_Q_HEADS // N_KV_HEADS

    def body(rb_ref, bk_ref, o_ref):
        bk = bk_ref[...]
        prev_keys = lax.broadcasted_iota(jnp.int32, bk.shape, 1) < lq
        accs = [jnp.full(bk.shape, -1e30, F32) for _ in range(N_Q_HEADS)]
        for b in range(N_BUCKETS):
            hit = bk == b
            accs = [jnp.where(hit, rb_ref[b, h], acc) for h, acc in enumerate(accs)]
        for h in range(N_Q_HEADS):
            rows = slice((h % group) * lq, (h % group + 1) * lq)
            o_ref[0, h // group, rows, :] = accs[h]
            o_ref[1, h // group, rows, :] = jnp.where(prev_keys, -1e30, accs[h])

    return pl.pallas_call(
        body, name="rel_bias_table",
        in_specs=[pl.BlockSpec(memory_space=pltpu.SMEM), pl.BlockSpec(memory_space=pltpu.VMEM)],
        out_shape=jax.ShapeDtypeStruct((2, N_KV_HEADS, group * lq, 2 * lq), F32),
        compiler_params=_params(n_axes=0))(rel_bias, bucket)


def rel_bias_grad(dbias, bucket):
    lq = ATTN_BLOCK
    group = N_Q_HEADS // N_KV_HEADS

    def body(db_ref, bk_ref, o_ref):
        rows = lax.broadcasted_iota(jnp.int32, (N_BUCKETS, 128), 0)
        lanes = lax.broadcasted_iota(jnp.int32, (N_BUCKETS, 128), 1)
        bk = bk_ref[...]
        per_head = []
        for h in range(N_Q_HEADS):
            sl = slice((h % group) * lq, (h % group + 1) * lq)
            per_head.append(db_ref[0, h // group, sl, :] + db_ref[1, h // group, sl, :])

        def per_bucket(b, acc):
            hit = (bk == b).astype(F32)
            for h in range(N_Q_HEADS):
                val = jnp.sum(per_head[h] * hit, keepdims=True)
                acc = acc + jnp.where((rows == b) & (lanes == h), val, 0.0)
            return acc

        o_ref[...] = lax.fori_loop(0, N_BUCKETS, per_bucket, jnp.zeros((N_BUCKETS, 128), F32))

    return pl.pallas_call(body, name="rel_bias_grad", out_shape=jax.ShapeDtypeStruct((N_BUCKETS, 128), F32),
                          compiler_params=_params(n_axes=0))(dbias, bucket)


def _tri_sum(a, upper):
    n = a.shape[0]
    ri = lax.broadcasted_iota(jnp.int32, (n, n), 0)
    ci = lax.broadcasted_iota(jnp.int32, (n, n), 1)
    tri = ((ri <= ci) if upper else (ri >= ci)).astype(BF16)
    hi = a.astype(BF16)
    rest = a - hi.astype(F32)
    mid = rest.astype(BF16)
    lo = (rest - mid.astype(F32)).astype(BF16)
    dot = lambda part: jnp.dot(tri, part, preferred_element_type=F32)
    return dot(hi) + dot(mid) + dot(lo)


@jax.custom_vjp
def _cumsum_rows(a):
    return _tri_sum(a, False)


_cumsum_rows.defvjp(lambda a: (_tri_sum(a, False), None), lambda _, g: (_tri_sum(g, True),))


def _ssm_core(u, z, dt_raw, hprev, dt_bias, a_log, dskip, norm_w):
    lc = u.shape[0]
    xbc = _silu(u)
    xs, bm, cm = xbc[:, :SSM_WIDTH], xbc[:, SSM_WIDTH:SSM_WIDTH + SSM_BC], xbc[:, SSM_WIDTH + SSM_BC:]
    dt = jax.nn.softplus(dt_raw + dt_bias)
    adt = dt * (-jnp.exp(a_log))
    ri = lax.broadcasted_iota(jnp.int32, (lc, lc), 0)
    ci = lax.broadcasted_iota(jnp.int32, (lc, lc), 1)
    causal = ri >= ci
    acum = _cumsum_rows(adt)
    acum_t = acum.T
    last = acum[lc - 1:lc, :]
    per_group = SSM_HEADS // SSM_GROUPS
    lane = lax.broadcasted_iota(jnp.int32, (1, 128), 1)
    rowid = lax.broadcasted_iota(jnp.int32, (128, 1), 0)
    lo_lanes = lane < SSM_HEAD_DIM
    ys, hs = [], []
    for g in range(SSM_GROUPS):
        bg = bm[:, g * SSM_STATE:(g + 1) * SSM_STATE]
        cg = cm[:, g * SSM_STATE:(g + 1) * SSM_STATE]
        cb = mm_nt(cg, bg)
        for pp in range(per_group // 2):
            ha = g * per_group + 2 * pp
            xp = xs[:, ha * SSM_HEAD_DIM:(ha + 2) * SSM_HEAD_DIM]
            hp = hprev[ha * SSM_HEAD_DIM:(ha + 2) * SSM_HEAD_DIM, :]
            xcp = xp * jnp.where(lo_lanes, dt[:, ha:ha + 1], dt[:, ha + 1:ha + 2])
            y_h, st_h = [], []
            for h in (ha, ha + 1):
                col, rowv, lasth = acum[:, h:h + 1], acum_t[h:h + 1, :], last[:, h:h + 1]
                decay = jnp.exp(jnp.where(causal, col - rowv, -1e30))
                y_h.append(mm(cb * decay, xcp) + mm_nt(cg * jnp.exp(col), hp))
                st_h.append(mm_tn(xcp, bg * jnp.exp(lasth - col)))
            y_pair = jnp.where(lo_lanes, y_h[0], y_h[1])
            st_pair = jnp.where(rowid < SSM_HEAD_DIM, st_h[0], st_h[1])
            la, lb = last[:, ha:ha + 1], last[:, ha + 1:ha + 2]
            hs.append(jnp.exp(jnp.where(rowid < SSM_HEAD_DIM, la, lb)) * hp + st_pair)
            dsk = jnp.where(lo_lanes, dskip[:, ha:ha + 1], dskip[:, ha + 1:ha + 2])
            ys.append(y_pair + dsk * xp)
    y = jnp.concatenate(ys, axis=1) * _silu(z)
    gw = SSM_WIDTH // SSM_GROUPS
    outs = []
    for g in range(SSM_GROUPS):
        yg = y[:, g * gw:(g + 1) * gw]
        outs.append(yg * lax.rsqrt(jnp.mean(yg * yg, axis=-1, keepdims=True) + NORM_EPS))
    return jnp.concatenate(outs, axis=1) * norm_w, jnp.concatenate(hs, axis=0)


def _ssm_param_specs():
    return [pl.BlockSpec((SSM_CONV, XBC_WIDTH), _const), _vec(XBC_WIDTH), _vec(128), _vec(128), _vec(128),
            _vec(SSM_WIDTH)]


SSM_FWD_SUB = 2
SSM_BWD_SUB = 1


def ssm_fwd(xbc_raw, z, dt_raw, conv_w, conv_b, dt_bias, a_log, dskip, norm_w, exchange=None):
    s = xbc_raw.shape[0]
    lc = SSM_CHUNK
    lt = lc * SSM_FWD_SUB
    hrows = SSM_HEADS * SSM_HEAD_DIM

    def body(x_ref, halo_ref, z_ref, dt_ref, cw_ref, cb_ref, dtb_ref, al_ref, dk_ref, nw_ref,
             o_ref, hp_ref, state):
        i = pl.program_id(0)

        @pl.when(i == 0)
        def _():
            state[...] = jnp.zeros_like(state)

        halo = halo_ref[...] * (i > 0).astype(F32)
        xin = jnp.concatenate([halo, x_ref[...]], axis=0)
        u = (_conv_rows(xin, cw_ref[...], SSM_CONV) + cb_ref[...])[HALO:]
        h = state[...]
        for k in range(SSM_FWD_SUB):
            rows = slice(k * lc, (k + 1) * lc)
            hp_ref[k * hrows:(k + 1) * hrows, :] = h
            out, h = _ssm_core(u[rows], z_ref[rows, :], dt_ref[rows, :], h, dtb_ref[...], al_ref[...], dk_ref[...],
                               nw_ref[...])
            o_ref[rows, :] = _b(out)
        state[...] = h

    tile = lambda n: pl.BlockSpec((lt, n), _row)
    halo_spec = pl.BlockSpec((HALO, XBC_WIDTH), lambda i: (jnp.maximum(i * (lt // HALO) - 1, 0), 0))
    return _grid_call(
        body, s // lt, (xbc_raw, xbc_raw, z, dt_raw, conv_w, conv_b, dt_bias, a_log, dskip, norm_w), exchange,
        name="ssm_fwd",
        in_specs=[tile(XBC_WIDTH), halo_spec, tile(SSM_WIDTH), tile(DT_PAD)] + _ssm_param_specs(),
        out_specs=[tile(SSM_WIDTH), pl.BlockSpec((SSM_FWD_SUB * hrows, SSM_STATE), _row)],
        out_shape=[jax.ShapeDtypeStruct((s, SSM_WIDTH), BF16),
                   jax.ShapeDtypeStruct((s // lc * hrows, SSM_STATE), F32)],
        scratch_shapes=[pltpu.VMEM((hrows, SSM_STATE), F32)],
        compiler_params=_params())


def ssm_bwd(xbc_raw, z, dt_raw, hprev_all, d_out, conv_w, conv_b, dt_bias, a_log, dskip, norm_w, exchange=None):
    s = xbc_raw.shape[0]
    lc = SSM_CHUNK
    sub = SSM_BWD_SUB
    lt = lc * sub
    nt = s // lt
    hrows = SSM_HEADS * SSM_HEAD_DIM

    def body(x_ref, halo_ref, z_ref, dt_ref, hp_ref, do_ref, cw_ref, cb_ref, dtb_ref, al_ref, dk_ref, nw_ref,
             dx_ref, dz_ref, ddt_ref, dcw_ref, dcb_ref, ddtb_ref, dal_ref, ddk_ref, dnw_ref, dstate, du_next):
        i = pl.program_id(0)
        tile_no = nt - 1 - i

        @pl.when(i == 0)
        def _():
            dstate[...] = jnp.zeros_like(dstate)
            du_next[...] = jnp.zeros_like(du_next)
            for r in (dcw_ref, dcb_ref, ddtb_ref, dal_ref, ddk_ref, dnw_ref):
                r[...] = jnp.zeros_like(r)

        halo = halo_ref[...] * (tile_no > 0).astype(F32)
        xin = jnp.concatenate([halo, x_ref[...]], axis=0)
        cw = cw_ref[...]
        u = (_conv_rows(xin, cw, SSM_CONV) + cb_ref[...])[HALO:]
        dh = dstate[...]
        dus = [None] * sub
        for k in reversed(range(sub)):
            rows = slice(k * lc, (k + 1) * lc)
            _, vjp = jax.vjp(_ssm_core, u[rows], z_ref[rows, :], dt_ref[rows, :], hp_ref[k * hrows:(k + 1) * hrows, :],
                             dtb_ref[...], al_ref[...], dk_ref[...], nw_ref[...])
            dus[k], dz, ddt, dh, ddtb, dal, ddk, dnw = vjp((do_ref[rows, :], dh))
            dz_ref[rows, :] = _b(dz)
            ddt_ref[rows, :] = _b(ddt)
            ddtb_ref[...] += ddtb
            dal_ref[...] += dal
            ddk_ref[...] += ddk
            dnw_ref[...] += dnw
        dstate[...] = dh
        du = jnp.concatenate(dus, axis=0)
        du_ext = jnp.concatenate([du, du_next[...]], axis=0)
        dx_ref[...] = _b(_conv_rows_t(du_ext, cw, SSM_CONV)[:lt])
        du_next[...] = du[:HALO]
        sums = [jnp.sum(du * pltpu.roll(xin, j, axis=0)[HALO:] if j else du * xin[HALO:], axis=0, keepdims=True)
                for j in range(SSM_CONV)]
        dcw_ref[...] += jnp.concatenate(sums[::-1] + [jnp.zeros((8 - SSM_CONV, XBC_WIDTH), F32)], axis=0)
        dcb_ref[...] += jnp.sum(du, axis=0, keepdims=True)

    rev = lambda i: (nt - 1 - i, 0)
    tile = lambda n: pl.BlockSpec((lt, n), rev)
    halo_spec = pl.BlockSpec((HALO, XBC_WIDTH), lambda i: (jnp.maximum((nt - 1 - i) * (lt // HALO) - 1, 0), 0))
    acc = lambda r, n: pl.BlockSpec((r, n), _const)
    return _grid_call(
        body, nt, (xbc_raw, xbc_raw, z, dt_raw, hprev_all, d_out, conv_w, conv_b, dt_bias, a_log, dskip, norm_w),
        exchange, name="ssm_bwd",
        in_specs=[tile(XBC_WIDTH), halo_spec, tile(SSM_WIDTH), tile(DT_PAD),
                  pl.BlockSpec((sub * hrows, SSM_STATE), rev), tile(SSM_WIDTH)] + _ssm_param_specs(),
        out_specs=[tile(XBC_WIDTH), tile(SSM_WIDTH), tile(DT_PAD), acc(8, XBC_WIDTH), acc(1, XBC_WIDTH),
                   acc(1, 128), acc(1, 128), acc(1, 128), acc(1, SSM_WIDTH)],
        out_shape=[jax.ShapeDtypeStruct((s, XBC_WIDTH), BF16), jax.ShapeDtypeStruct((s, SSM_WIDTH), BF16),
                   jax.ShapeDtypeStruct((s, DT_PAD), BF16), jax.ShapeDtypeStruct((8, XBC_WIDTH), F32),
                   jax.ShapeDtypeStruct((1, XBC_WIDTH), F32), jax.ShapeDtypeStruct((1, 128), F32),
                   jax.ShapeDtypeStruct((1, 128), F32), jax.ShapeDtypeStruct((1, 128), F32),
                   jax.ShapeDtypeStruct((1, SSM_WIDTH), F32)],
        scratch_shapes=[pltpu.VMEM((hrows, SSM_STATE), F32), pltpu.VMEM((HALO, XBC_WIDTH), F32)],
        compiler_params=_params())


def mix_out(attn, ssm, x, w_out16, gate1, post_mix_w, pre_ffn_w, scale2, shift2, tm=4 * TOKEN_TILE):
    s = x.shape[0]
    tm = min(tm, s)

    def body(a_ref, s_ref, x_ref, w_ref, g_ref, pw_ref, fw_ref, sc_ref, sh_ref, mixed_ref, x1_ref, h2_ref):
        mixed = (jnp.dot(a_ref[...], w_ref[:ATTN_WIDTH, :], preferred_element_type=F32)
                 + jnp.dot(s_ref[...], w_ref[ATTN_WIDTH:, :], preferred_element_type=F32))
        mixed_ref[...] = _b(mixed)
        x1 = x_ref[...] + g_ref[...] * _rms(mixed, pw_ref[...])
        x1_ref[...] = x1
        h2_ref[...] = _b(_norm_mod(x1, fw_ref[...], sc_ref[...], sh_ref[...]))

    tile = lambda n: pl.BlockSpec((tm, n), _row)
    return pl.pallas_call(
        body, name="mix_out", grid=(s // tm,),
        in_specs=[tile(ATTN_WIDTH), tile(SSM_WIDTH), tile(D_MODEL), _resident((D_MODEL, D_MODEL))]
        + [_vec(D_MODEL)] * 5,
        out_specs=[tile(D_MODEL)] * 3,
        out_shape=[jax.ShapeDtypeStruct((s, D_MODEL), BF16), jax.ShapeDtypeStruct((s, D_MODEL), F32),
                   jax.ShapeDtypeStruct((s, D_MODEL), BF16)],
        compiler_params=_params())(attn, ssm, x, w_out16, gate1, post_mix_w, pre_ffn_w, scale2, shift2)


GELU_K0, GELU_K1 = math.sqrt(2.0 / math.pi), 0.044715


def _gate(ug, uv):
    return jax.nn.gelu(ug, approximate=True) * uv


def _gate_bwd(ug, uv, df):
    sq = ug * ug
    t = jnp.tanh(ug * (GELU_K0 + (GELU_K0 * GELU_K1) * sq))
    half = 0.5 + 0.5 * t
    slope = half + ug * (1.0 - t * t) * (0.5 * GELU_K0 + (1.5 * GELU_K0 * GELU_K1) * sq)
    return df * uv * slope, df * (ug * half)


def up_gate(h2, w_up16, conv_w, conv_b, tm=TOKEN_TILE):
    s = h2.shape[0]

    def body(h_ref, halo_ref, w_ref, cw_ref, cb_ref, u_ref, uraw_ref, f_ref):
        halo = halo_ref[...]
        halo = jnp.where(pl.program_id(0) > 0, halo, jnp.zeros_like(halo))
        hin = jnp.concatenate([halo, h_ref[...]], axis=0)
        for lo in range(0, D_FF, FF_CHUNK):
            halves = []
            for base in (lo, D_FF + lo):
                cols = slice(base, base + FF_CHUNK)
                uraw = jnp.dot(hin, w_ref[:, cols], preferred_element_type=F32)
                uraw_ref[:, cols] = _b(uraw[NEXT:])
                u = (_conv_rows(uraw, cw_ref[:, cols], FFN_CONV) + cb_ref[:, cols])[NEXT:]
                u_ref[:, cols] = u
                halves.append(u)
            f_ref[:, lo:lo + FF_CHUNK] = _b(_gate(*halves))

    tile = lambda n: pl.BlockSpec((tm, n), _row)
    halo_spec = pl.BlockSpec((NEXT, D_MODEL), lambda i: (jnp.maximum(i * (tm // NEXT) - 1, 0), 0))
    return pl.pallas_call(
        body, name="up_gate", grid=(s // tm,),
        in_specs=[tile(D_MODEL), halo_spec, _resident((D_MODEL, 2 * D_FF)),
                  pl.BlockSpec((FFN_CONV, 2 * D_FF), _const), _vec(2 * D_FF)],
        out_specs=[tile(2 * D_FF), tile(2 * D_FF), tile(D_FF)],
        out_shape=[jax.ShapeDtypeStruct((s, 2 * D_FF), F32), jax.ShapeDtypeStruct((s, 2 * D_FF), BF16),
                   jax.ShapeDtypeStruct((s, D_FF), BF16)],
        compiler_params=_params())(h2, h2, w_up16, conv_w, conv_b)


DOWN_LOSS_TILE = 512


def down_loss(f16, w_down16, x1, target, gate2, post_ffn_w, tm=DOWN_LOSS_TILE):
    s = x1.shape[0]
    tm = min(tm, s)

    def body(f_ref, wd_ref, x1_ref, t_ref, g_ref, pw_ref, dffn_ref, dy_ref, loss_ref, dg_ref, dpw_ref, gw_ref):
        i = pl.program_id(0)

        @pl.when(i == 0)
        def _():
            loss_ref[...] = jnp.zeros_like(loss_ref)
            dg_ref[...] = jnp.zeros_like(dg_ref)
            dpw_ref[...] = jnp.zeros_like(dpw_ref)
            gw_ref[...] = jnp.zeros_like(gw_ref)

        ffn = jnp.dot(f_ref[...], wd_ref[...], preferred_element_type=F32)
        x1 = x1_ref[...]
        x2 = x1 + g_ref[...] * _rms(ffn, pw_ref[...])
        err = x2 - t_ref[...]
        dy = err * (1.0 / D_MODEL)
        dy_ref[...] = dy
        loss_ref[...] += 0.5 * jnp.sum(jnp.mean(err * err, axis=-1, keepdims=True))
        dffn, dg, dpw = _gated_rms_bwd(ffn, g_ref[...], pw_ref[...], dy)
        dffn16 = _b(dffn)
        dffn_ref[...] = dffn16
        dg_ref[...] += dg
        dpw_ref[...] += dpw
        gw_ref[...] += _tn(f_ref[...], dffn16)

    tile = lambda n: pl.BlockSpec((tm, n), _row)
    return pl.pallas_call(
        body, name="down_loss", grid=(s // tm,),
        in_specs=[tile(D_FF), _resident((D_FF, D_MODEL)), tile(D_MODEL), tile(D_MODEL), _vec(D_MODEL), _vec(D_MODEL)],
        out_specs=[tile(D_MODEL), tile(D_MODEL), _vec(128), _vec(D_MODEL), _vec(D_MODEL),
                   pl.BlockSpec((D_FF, D_MODEL), _const)],
        out_shape=[jax.ShapeDtypeStruct((s, D_MODEL), BF16), jax.ShapeDtypeStruct((s, D_MODEL), F32),
                   jax.ShapeDtypeStruct((1, 128), F32), jax.ShapeDtypeStruct((1, D_MODEL), F32),
                   jax.ShapeDtypeStruct((1, D_MODEL), F32), jax.ShapeDtypeStruct((D_FF, D_MODEL), F32)],
        compiler_params=_params())(f16, w_down16, x1, target, gate2, post_ffn_w)


BWD_CHUNK = 256


def ffn_bwd(u, u_raw16, d_ffn, conv_w, w_down_t16, w_up_t16, tm=TOKEN_TILE):
    s = u.shape[0]
    nt = s // tm

    def body(u_ref, unext_ref, uraw_ref, d_ref, dnext_ref, cw_ref, wdt_ref, wut_ref,
             du_ref, dh_ref, dcw_ref, dcb_ref):
        i = pl.program_id(0)

        @pl.when(i == 0)
        def _():
            dcw_ref[...] = jnp.zeros_like(dcw_ref)
            dcb_ref[...] = jnp.zeros_like(dcb_ref)

        dnext = dnext_ref[...]
        dnext = jnp.where(i < nt - 1, dnext, jnp.zeros_like(dnext))
        dff = jnp.concatenate([d_ref[...], dnext], axis=0)
        rows_ext = tm + NEXT
        for lo in range(0, D_FF, BWD_CHUNK):
            gcols, vcols = slice(lo, lo + BWD_CHUNK), slice(D_FF + lo, D_FF + lo + BWD_CHUNK)
            ug = jnp.concatenate([u_ref[:, gcols], unext_ref[:, gcols]], axis=0)
            uv = jnp.concatenate([u_ref[:, vcols], unext_ref[:, vcols]], axis=0)
            df = jnp.dot(dff, wdt_ref[:, gcols], preferred_element_type=F32)
            for cols, du in zip((gcols, vcols), _gate_bwd(ug, uv, df)):
                cw = cw_ref[:, cols]
                du1 = pltpu.roll(du, rows_ext - 1, axis=0)
                du2 = pltpu.roll(du, rows_ext - 2, axis=0)
                du_ref[:, cols] = _b((du * cw[2:3, :] + du1 * cw[1:2, :] + du2 * cw[0:1, :])[:tm])
                xr = uraw_ref[:, cols].astype(F32)
                rows = [jnp.sum(xr * d_[:tm], axis=0, keepdims=True) for d_ in (du2, du1, du)]
                dcw_ref[:, cols] += jnp.concatenate(rows + [jnp.zeros((8 - FFN_CONV, BWD_CHUNK), F32)], axis=0)
                dcb_ref[:, cols] += jnp.sum(du[:tm], axis=0, keepdims=True)
        dh_ref[...] = jnp.dot(du_ref[...], wut_ref[...], preferred_element_type=F32)

    tile = lambda n: pl.BlockSpec((tm, n), _row)
    nxt = lambda i: (jnp.minimum((i + 1) * (tm // NEXT), s // NEXT - 1), 0)
    return pl.pallas_call(
        body, name="ffn_bwd", grid=(nt,),
        in_specs=[tile(2 * D_FF), pl.BlockSpec((NEXT, 2 * D_FF), nxt), tile(2 * D_FF), tile(D_MODEL),
                  pl.BlockSpec((NEXT, D_MODEL), nxt), pl.BlockSpec((FFN_CONV, 2 * D_FF), _const),
                  _resident((D_MODEL, D_FF)), _resident((2 * D_FF, D_MODEL))],
        out_specs=[tile(2 * D_FF), tile(D_MODEL), pl.BlockSpec((8, 2 * D_FF), _const), _vec(2 * D_FF)],
        out_shape=[jax.ShapeDtypeStruct((s, 2 * D_FF), BF16), jax.ShapeDtypeStruct((s, D_MODEL), F32),
                   jax.ShapeDtypeStruct((8, 2 * D_FF), F32), jax.ShapeDtypeStruct((1, 2 * D_FF), F32)],
        compiler_params=_params())(u, u, u_raw16, d_ffn, d_ffn, conv_w, w_down_t16, w_up_t16)


def mix_bwd(dh2, x1, dy, mixed, attn, ssm, w_out_t16, pre_ffn_w, scale2, gate1, post_mix_w, tm=2 * TOKEN_TILE):
    s = x1.shape[0]

    def body(dh_ref, x1_ref, dy_ref, mx_ref, a_ref, s_ref, w_ref, fw_ref, sc_ref, g_ref, pw_ref,
             dx1_ref, da_ref, ds_ref, dfw_ref, dsc_ref, dsh_ref, dg_ref, dpw_ref, gw_ref):
        accs = (dfw_ref, dsc_ref, dsh_ref, dg_ref, dpw_ref)

        @pl.when(pl.program_id(0) == 0)
        def _():
            for r in accs + (gw_ref,):
                r[...] = jnp.zeros_like(r)

        dx1, dfw, dsc, dsh = _norm_mod_bwd(x1_ref[...], fw_ref[...], sc_ref[...], dh_ref[...])
        dx1 = dx1 + dy_ref[...]
        dx1_ref[...] = dx1
        dmixed, dg, dpw = _gated_rms_bwd(mx_ref[...].astype(F32), g_ref[...], pw_ref[...], dx1)
        dm16 = _b(dmixed)
        dmix_in = jnp.dot(dm16, w_ref[...], preferred_element_type=F32)
        da_ref[...] = _b(dmix_in[:, :ATTN_WIDTH])
        ds_ref[...] = dmix_in[:, ATTN_WIDTH:]
        gw_ref[:ATTN_WIDTH, :] += _tn(a_ref[...], dm16)
        gw_ref[ATTN_WIDTH:, :] += _tn(s_ref[...], dm16)
        for r, v in zip(accs, (dfw, dsc, dsh, dg, dpw)):
            r[...] += v

    tile = lambda n: pl.BlockSpec((tm, n), _row)
    return pl.pallas_call(
        body, name="mix_bwd", grid=(s // tm,),
        in_specs=[tile(D_MODEL)] * 4 + [tile(ATTN_WIDTH), tile(SSM_WIDTH), _resident((D_MODEL, D_MODEL))]
        + [_vec(D_MODEL)] * 4,
        out_specs=[tile(D_MODEL), tile(ATTN_WIDTH), tile(SSM_WIDTH)] + [_vec(D_MODEL)] * 5
        + [pl.BlockSpec((D_MODEL, D_MODEL), _const)],
        out_shape=[jax.ShapeDtypeStruct((s, D_MODEL), F32), jax.ShapeDtypeStruct((s, ATTN_WIDTH), BF16),
                   jax.ShapeDtypeStruct((s, SSM_WIDTH), F32)]
        + [jax.ShapeDtypeStruct((1, D_MODEL), F32)] * 5 + [jax.ShapeDtypeStruct((D_MODEL, D_MODEL), F32)],
        compiler_params=_params())(dh2, x1, dy, mixed, attn, ssm, w_out_t16, pre_ffn_w, scale2, gate1, post_mix_w)


INPROJ_BWD_TILE = 512


def inproj_bwd(dq, dk, dv, dxbc, dz, ddt, x, dx1, h1, w_in_t16, pre_mix_w, scale1, tm=INPROJ_BWD_TILE):
    s = x.shape[0]
    tm = min(tm, s)

    def body(dq_ref, dk_ref, dv_ref, dxbc_ref, dz_ref, ddt_ref, x_ref, dx1_ref, h_ref, w_ref, pw_ref, sc_ref,
             gx_ref, dpw_ref, dsc_ref, dsh_ref, gw_ref):
        accs = (dpw_ref, dsc_ref, dsh_ref)

        @pl.when(pl.program_id(0) == 0)
        def _():
            for r in accs + (gw_ref,):
                r[...] = jnp.zeros_like(r)

        h16 = h_ref[...]
        dh = None
        dqkv = jnp.concatenate([dq_ref[...], dk_ref[...], dv_ref[...]], axis=1)
        for d16, lo, hi in ((dqkv, 0, OFF_XBC), (dxbc_ref[...], OFF_XBC, OFF_Z), (dz_ref[...], OFF_Z, OFF_DT),
                            (ddt_ref[...], OFF_DT, PROJ_PAD)):
            part = jnp.dot(d16, w_ref[lo:hi, :], preferred_element_type=F32)
            dh = part if dh is None else dh + part
            gw_ref[:, lo:hi] += _tn(h16, d16)
        dx, dpw, dsc, dsh = _norm_mod_bwd(x_ref[...], pw_ref[...], sc_ref[...], dh)
        gx_ref[...] = dx1_ref[...] + dx
        for r, v in zip(accs, (dpw, dsc, dsh)):
            r[...] += v

    tile = lambda n: pl.BlockSpec((tm, n), _row)
    return pl.pallas_call(
        body, name="inproj_bwd", grid=(s // tm,),
        in_specs=[tile(ATTN_WIDTH), tile(KV_WIDTH), tile(KV_WIDTH), tile(XBC_WIDTH), tile(SSM_WIDTH), tile(DT_PAD),
                  tile(D_MODEL), tile(D_MODEL), tile(D_MODEL), _resident((PROJ_PAD, D_MODEL))] + [_vec(D_MODEL)] * 2,
        out_specs=[tile(D_MODEL)] + [_vec(D_MODEL)] * 3 + [pl.BlockSpec((D_MODEL, PROJ_PAD), _const)],
        out_shape=[jax.ShapeDtypeStruct((s, D_MODEL), F32)] + [jax.ShapeDtypeStruct((1, D_MODEL), F32)] * 3
        + [jax.ShapeDtypeStruct((D_MODEL, PROJ_PAD), F32)],
        compiler_params=_params())(dq, dk, dv, dxbc, dz, ddt, x, dx1, h1, w_in_t16, pre_mix_w, scale1)


def _adam(g, w, m, v):
    new_m = ADAM_B1 * m + (1.0 - ADAM_B1) * g
    new_v = ADAM_B2 * v + (1.0 - ADAM_B2) * jnp.square(g)
    m_hat = new_m / (1.0 - ADAM_B1 ** ADAM_STEP)
    v_hat = new_v / (1.0 - ADAM_B2 ** ADAM_STEP)
    return -ADAM_LR * (m_hat / (jnp.sqrt(v_hat) + ADAM_EPS) + ADAM_WD * w), new_m, new_v


ROW_PARAMS = (("b_ada", 6144, 6144), ("pre_mix_w", 1024, 1024), ("attn_sinks", 128, 8), ("ssm_conv_b", 1024, 1024),
              ("ssm_dt_bias", 128, 8), ("ssm_a_log", 128, 8), ("ssm_d", 128, 8), ("ssm_norm_w", 512, 512),
              ("post_mix_w", 1024, 1024), ("pre_ffn_w", 1024, 1024), ("ffn_conv_b", 5632, 5632),
              ("post_ffn_w", 1024, 1024))
LOSS_LANES = 128


def adamw_small(row_all, rb_all, rel_bias_wmv, row_wmv):
    n_rows = len(ROW_PARAMS)

    def body(*refs):
        row_ref, rb_ref = refs[:2]
        wmv = refs[2:5 + 3 * n_rows]
        outs = refs[5 + 3 * n_rows:]
        g_row, g_rb = row_ref[0], rb_ref[0]
        for k in range(1, N_DEV):
            g_row = g_row + row_ref[k]
            g_rb = g_rb + rb_ref[k]
        outs[0][...] = g_row[:, :LOSS_LANES]
        grads = [g_rb[:, :N_Q_HEADS]]
        off = LOSS_LANES
        for _, lanes, width in ROW_PARAMS:
            grads.append(g_row[:, off:off + width])
            off += lanes
        for i, g in enumerate(grads):
            w_ref, m_ref, v_ref = wmv[3 * i:3 * i + 3]
            g_out, d_out, m_out, v_out = outs[1 + 4 * i:5 + 4 * i]
            g_out[...] = g
            d_out[...], m_out[...], v_out[...] = _adam(g, w_ref[...], m_ref[...], v_ref[...])

    flat_wmv = list(rel_bias_wmv) + [a for wmv in row_wmv for a in wmv]
    shapes = [jax.ShapeDtypeStruct((1, LOSS_LANES), F32)] + [jax.ShapeDtypeStruct((N_BUCKETS, N_Q_HEADS), F32)] * 4
    for _, _, width in ROW_PARAMS:
        shapes += [jax.ShapeDtypeStruct((1, width), F32)] * 4
    return pl.pallas_call(body, name="adamw_small", out_shape=shapes,
                          compiler_params=_params(n_axes=0))(row_all, rb_all, *flat_wmv)


def adamw(parts, w, m, v, name):
    p, r, n = parts.shape
    tr = _row_tile(r)

    def body(p_ref, w_ref, m_ref, v_ref, g_ref, d_ref, nm_ref, nv_ref):
        g = p_ref[0].astype(F32)
        for k in range(1, p):
            g = g + p_ref[k].astype(F32)
        g_ref[...] = g
        d_ref[...], nm_ref[...], nv_ref[...] = _adam(g, w_ref[...], m_ref[...], v_ref[...])

    tile = pl.BlockSpec((tr, n), _row)
    return pl.pallas_call(
        body, name=name, grid=(r // tr,),
        in_specs=[pl.BlockSpec((p, tr, n), lambda i: (0, i, 0)), tile, tile, tile],
        out_specs=[tile] * 4, out_shape=[jax.ShapeDtypeStruct((r, n), F32)] * 4,
        compiler_params=_params())(parts, w, m, v)


def _bucket_table():
    lq = ATTN_BLOCK
    qi = np.arange(lq)[:, None] + lq
    kj = np.arange(2 * lq)[None, :]
    dist = qi - kj
    d = np.maximum(dist, 0)
    max_exact = N_BUCKETS // 2
    nf = np.maximum(d, 1).astype(np.float32)
    large = max_exact + (np.log(nf / max_exact) / math.log(REL_MAX_DIST / max_exact)
                         * (N_BUCKETS - max_exact)).astype(np.int32)
    large = np.minimum(large, N_BUCKETS - 1)
    bucket = np.where(d < max_exact, d, large).astype(np.int32)
    in_band = (dist >= 0) & (dist < REL_MAX_DIST)
    return np.where(in_band, bucket, -1).astype(np.int32)


def _cols_from_blocks(g):
    return jnp.transpose(g, (1, 0, 2)).reshape(g.shape[1], N_DEV * g.shape[2])


def _cols_to_blocks(a):
    r, n = a.shape
    return jnp.transpose(a.reshape(r, N_DEV, n // N_DEV), (1, 0, 2))


def _perm_in_rows(wt):
    pad = jnp.zeros((DT_PAD - SSM_HEADS, wt.shape[1]), wt.dtype)
    return jnp.concatenate([wt[:768], wt[768:1280], wt[1792:2304], wt[1280:1792], wt[2304:2312], pad], axis=0)


def _unperm_in(g):
    return jnp.concatenate([g[:, :768], g[:, 768:1280], g[:, 1792:2304], g[:, 1280:1792], g[:, 2304:2312]], axis=1)


def _lane_pad(v, n=128):
    return jnp.pad(v, ((0, 0), (0, n - v.shape[1])))


def kernel(x, c, rel_bias, w_ada, b_ada, pre_mix_w, w_in, attn_sinks, ssm_conv_w, ssm_conv_b, ssm_dt_bias, ssm_a_log, ssm_d, ssm_norm_w, w_out, post_mix_w, pre_ffn_w, w_up, ffn_conv_w, ffn_conv_b, w_down, post_ffn_w, loss_target, m_rel_bias, m_w_ada, m_b_ada, m_pre_mix_w, m_w_in, m_attn_sinks, m_ssm_conv_w, m_ssm_conv_b, m_ssm_dt_bias, m_ssm_a_log, m_ssm_d, m_ssm_norm_w, m_w_out, m_post_mix_w, m_pre_ffn_w, m_w_up, m_ffn_conv_w, m_ffn_conv_b, m_w_down, m_post_ffn_w, v_rel_bias, v_w_ada, v_b_ada, v_pre_mix_w, v_w_in, v_attn_sinks, v_ssm_conv_w, v_ssm_conv_b, v_ssm_dt_bias, v_ssm_a_log, v_ssm_d, v_ssm_norm_w, v_w_out, v_post_mix_w, v_pre_ffn_w, v_w_up, v_ffn_conv_w, v_ffn_conv_b, v_w_down, v_post_ffn_w):
    weights = dict(rel_bias=rel_bias, w_ada=w_ada, b_ada=b_ada, pre_mix_w=pre_mix_w, w_in=w_in, attn_sinks=attn_sinks, ssm_conv_w=ssm_conv_w, ssm_conv_b=ssm_conv_b, ssm_dt_bias=ssm_dt_bias, ssm_a_log=ssm_a_log, ssm_d=ssm_d, ssm_norm_w=ssm_norm_w, w_out=w_out, post_mix_w=post_mix_w, pre_ffn_w=pre_ffn_w, w_up=w_up, ffn_conv_w=ffn_conv_w, ffn_conv_b=ffn_conv_b, w_down=w_down, post_ffn_w=post_ffn_w)
    mom_m = dict(rel_bias=m_rel_bias, w_ada=m_w_ada, b_ada=m_b_ada, pre_mix_w=m_pre_mix_w, w_in=m_w_in, attn_sinks=m_attn_sinks, ssm_conv_w=m_ssm_conv_w, ssm_conv_b=m_ssm_conv_b, ssm_dt_bias=m_ssm_dt_bias, ssm_a_log=m_ssm_a_log, ssm_d=m_ssm_d, ssm_norm_w=m_ssm_norm_w, w_out=m_w_out, post_mix_w=m_post_mix_w, pre_ffn_w=m_pre_ffn_w, w_up=m_w_up, ffn_conv_w=m_ffn_conv_w, ffn_conv_b=m_ffn_conv_b, w_down=m_w_down, post_ffn_w=m_post_ffn_w)
    mom_v = dict(rel_bias=v_rel_bias, w_ada=v_w_ada, b_ada=v_b_ada, pre_mix_w=v_pre_mix_w, w_in=v_w_in, attn_sinks=v_attn_sinks, ssm_conv_w=v_ssm_conv_w, ssm_conv_b=v_ssm_conv_b, ssm_dt_bias=v_ssm_dt_bias, ssm_a_log=v_ssm_a_log, ssm_d=v_ssm_d, ssm_norm_w=v_ssm_norm_w, w_out=v_w_out, post_mix_w=v_post_mix_w, pre_ffn_w=v_pre_ffn_w, w_up=v_w_up, ffn_conv_w=v_ffn_conv_w, ffn_conv_b=v_ffn_conv_b, w_down=v_w_down, post_ffn_w=v_post_ffn_w)
    order = ['rel_bias', 'w_ada', 'b_ada', 'pre_mix_w', 'w_in', 'attn_sinks', 'ssm_conv_w', 'ssm_conv_b', 'ssm_dt_bias', 'ssm_a_log', 'ssm_d', 'ssm_norm_w', 'w_out', 'post_mix_w', 'pre_ffn_w', 'w_up', 'ffn_conv_w', 'ffn_conv_b', 'w_down', 'post_ffn_w']

    me = 4 * lax.axis_index("x") + 2 * lax.axis_index("y") + lax.axis_index("c")
    xs_ = x[0]
    target = loss_target[0]

    (w_in_g, scw_g, fcw_g, c_g) = all_gather([_b(w_in[0]).T, ssm_conv_w[0], ffn_conv_w[0], c], "gather_weights")
    w_in_t16 = _perm_in_rows(w_in_g.reshape(IN_PROJ_WIDTH, D_MODEL))
    w_in16 = w_in_t16.T
    ssm_cw = _cols_from_blocks(scw_g)
    ffn_cw = _cols_from_blocks(fcw_g)
    c_all = c_g.reshape(N_DEV, D_MODEL)

    n_cols = w_ada.shape[2]
    b_cols = lax.dynamic_slice(b_ada, (0, me * n_cols), (1, n_cols))
    mod_part = ada_fwd(c_all, w_ada[0], b_cols)
    (mod_rows,) = all_to_all([mod_part.reshape(N_DEV, 1, n_cols)], "scatter_mod")
    mod = mod_rows.reshape(N_MOD, 1, D_MODEL)
    shift1, scale1, gate1, shift2, scale2, gate2 = (mod[i] for i in range(N_MOD))

    bucket_band = jnp.asarray(_bucket_table())
    bias = rel_bias_table(rel_bias, bucket_band)
    sinks_row = _lane_pad(attn_sinks)
    dt_bias, a_log, dskip = _lane_pad(ssm_dt_bias), _lane_pad(ssm_a_log), _lane_pad(ssm_d)

    h1, qkv, xbc_raw, z, dt_raw, w_out_g = pre_mix_inproj(
        xs_, pre_mix_w, scale1, shift1, w_in16, [(_b(w_out[0]), False)])
    attn, w_down_g = attn_fwd(qkv, bias, sinks_row, [(_b(w_down[0]), False)])
    ssm, hprev_all, w_up_g = ssm_fwd(xbc_raw, z, dt_raw, ssm_cw, ssm_conv_b, dt_bias, a_log, dskip, ssm_norm_w,
                                     [(_b(w_up[0]).T, False)])
    w_out16 = w_out_g.reshape(D_MODEL, D_MODEL)
    w_out_t16 = w_out16.T
    w_up_t16 = w_up_g.reshape(2 * D_FF, D_MODEL)
    w_up16 = w_up_t16.T
    w_down16 = w_down_g.reshape(D_FF, D_MODEL)
    w_down_t16 = w_down16.T
    mixed, x1, h2 = mix_out(attn, ssm, xs_, w_out16, gate1, post_mix_w, pre_ffn_w, scale2, shift2)
    u, u_raw16, f16 = up_gate(h2, w_up16, ffn_cw, ffn_conv_b)
    d_ffn, dy, loss_part, d_gate2, d_post_ffn_w, g_w_down = down_loss(f16, w_down16, x1, target, gate2, post_ffn_w)

    du_raw, dh2, d_ffn_cw, d_ffn_cb = ffn_bwd(u, u_raw16, d_ffn, ffn_cw, w_down_t16, w_up_t16)
    g_w_up = matmul_tn(h2, du_raw, "grad_w_up", D_MODEL, FF_CHUNK, tk=2048)
    (dx1, d_attn, d_ssm, d_pre_ffn_w, d_scale2, d_shift2, d_gate1, d_post_mix_w, g_w_out) = mix_bwd(
        dh2, x1, dy, mixed, attn, ssm, w_out_t16, pre_ffn_w, scale2, gate1, post_mix_w)
    dq, dk, dv, dbias, dsinks, p_w_down = attn_bwd(
        qkv, bias, sinks_row, d_attn, [(g_w_down.reshape(N_DEV, D_FF // N_DEV, D_MODEL), True)])
    d_rel_bias = rel_bias_grad(dbias, bucket_band)
    (dxbc, dz, ddt, d_ssm_cw, d_ssm_cb, d_dt_bias, d_a_log, d_dskip, d_norm_w, p_w_up, p_w_out) = ssm_bwd(
        xbc_raw, z, dt_raw, hprev_all, d_ssm, ssm_cw, ssm_conv_b, dt_bias, a_log, dskip, ssm_norm_w,
        [(_cols_to_blocks(g_w_up), True), (g_w_out.reshape(N_DEV, D_MODEL // N_DEV, D_MODEL), True)])
    grad_x, d_pre_mix_w, d_scale1, d_shift1, g_w_in_perm = inproj_bwd(
        dq, dk, dv, dxbc, dz, ddt, xs_, dx1, h1, w_in_t16, pre_mix_w, scale1)
    g_w_in = _unperm_in(g_w_in_perm)

    d_mod = jnp.concatenate([d_shift1, d_scale1, d_gate1, d_shift2, d_scale2, d_gate2], axis=1)
    late = ("w_in", "ssm_conv_w", "ffn_conv_w")
    full = [_cols_to_blocks(g_w_in), _cols_to_blocks(d_ssm_cw[:SSM_CONV]), _cols_to_blocks(d_ffn_cw[:FFN_CONV])]
    core = lax.axis_index("c").astype(jnp.int32).reshape(1)
    got = pair_exchange(full, "pair_grads")
    chip_sums = [pair_sum(f_, g_, core, "pair_sum_" + k) for k, f_, g_ in zip(late, full, got)]
    chip_parts = all_to_all(chip_sums, "scatter_grads", CHIP_FLIPS, _chip_index)

    row_g = dict(b_ada=d_mod, pre_mix_w=d_pre_mix_w, attn_sinks=dsinks, ssm_conv_b=d_ssm_cb, ssm_dt_bias=d_dt_bias,
                 ssm_a_log=d_a_log, ssm_d=d_dskip, ssm_norm_w=d_norm_w, post_mix_w=d_post_mix_w,
                 pre_ffn_w=d_pre_ffn_w, ffn_conv_b=d_ffn_cb, post_ffn_w=d_post_ffn_w)
    row = jnp.concatenate([loss_part] + [row_g[k] for k, _, _ in ROW_PARAMS], axis=1)
    row_all, rb_all = all_gather([row, d_rel_bias], "gather_small")
    d_mod_cols = lax.dynamic_slice(row_all.reshape(N_DEV, row.shape[1]), (0, LOSS_LANES + me * n_cols),
                                   (N_DEV, n_cols))
    g_w_ada = ada_bwd(c_all, d_mod_cols)

    wmv = lambda k: (weights[k], mom_m[k], mom_v[k])
    small = adamw_small(row_all, rb_all, wmv("rel_bias"), [wmv(k) for k, _, _ in ROW_PARAMS])
    loss = small[0][0, 0]
    res = {k: tuple(small[1 + 4 * i:5 + 4 * i]) for i, k in enumerate(["rel_bias"] + [k for k, _, _ in ROW_PARAMS])}
    big = list(zip(late, chip_parts)) + [("w_down", p_w_down), ("w_up", p_w_up), ("w_out", p_w_out),
                                        ("w_ada", g_w_ada[None])]
    for k, parts in big:
        res[k] = tuple(o[None] for o in adamw(parts, weights[k][0], mom_m[k][0], mom_v[k][0], "adamw_" + k))

    outs = [loss, grad_x[None]]
    for field in range(4):
        outs += [res[k][field] for k in order]
    return tuple(outs)
```

```python
import math

import numpy as np
import jax
import jax.numpy as jnp
from jax import lax
from jax.experimental import pallas as pl
from jax.experimental.pallas import tpu as pltpu

F32 = jnp.float32
BF16 = jnp.bfloat16
MESH_ID = pl.DeviceIdType.MESH

N_DEV = 8
D_MODEL = 1024
N_Q_HEADS = 8
N_KV_HEADS = 2
HEAD_DIM = 64
ATTN_WIDTH = 512
KV_WIDTH = 128
ATTN_BLOCK = 128
N_BUCKETS = 32
REL_MAX_DIST = 128
SSM_HEADS = 8
SSM_HEAD_DIM = 64
SSM_WIDTH = 512
SSM_STATE = 128
SSM_GROUPS = 2
SSM_BC = 256
SSM_CONV = 4
SSM_CHUNK = 256
XBC_WIDTH = SSM_WIDTH + 2 * SSM_BC
D_FF = 2816
FFN_CONV = 3
NORM_EPS = 1e-6
N_MOD = 6
IN_PROJ_WIDTH = 2312
QKV_W = ATTN_WIDTH + 2 * KV_WIDTH
OFF_XBC = QKV_W
OFF_Z = OFF_XBC + XBC_WIDTH
OFF_DT = OFF_Z + SSM_WIDTH
DT_PAD = 128
PROJ_PAD = OFF_DT + DT_PAD
FF_CHUNK = 1408

ADAM_LR = 0.001
ADAM_B1 = 0.9
ADAM_B2 = 0.999
ADAM_EPS = 1e-08
ADAM_WD = 0.01
ADAM_STEP = 10

TOKEN_TILE = 256
HALO = 8
NEXT = 16
VMEM_LIMIT = 56 * 1024 * 1024


def _params(vmem=VMEM_LIMIT, n_axes=1):
    return pltpu.CompilerParams(dimension_semantics=("arbitrary",) * n_axes, vmem_limit_bytes=vmem)


def _b(x):
    return x.astype(BF16)


def _nn(a, b):
    return jnp.dot(_b(a), _b(b), preferred_element_type=F32)


def _nt(a, b):
    return lax.dot_general(_b(a), _b(b), (((1,), (1,)), ((), ())), preferred_element_type=F32)


def _tn(a, b):
    return lax.dot_general(_b(a), _b(b), (((0,), (0,)), ((), ())), preferred_element_type=F32)


@jax.custom_vjp
def mm(a, b):
    return _nn(a, b)


mm.defvjp(lambda a, b: (_nn(a, b), (a, b)),
          lambda r, g: (_nt(g, r[1]).astype(r[0].dtype), _tn(r[0], g).astype(r[1].dtype)))


@jax.custom_vjp
def mm_nt(a, b):
    return _nt(a, b)


mm_nt.defvjp(lambda a, b: (_nt(a, b), (a, b)),
             lambda r, g: (_nn(g, r[1]).astype(r[0].dtype), _tn(g, r[0]).astype(r[1].dtype)))


@jax.custom_vjp
def mm_tn(a, b):
    return _tn(a, b)


mm_tn.defvjp(lambda a, b: (_tn(a, b), (a, b)),
             lambda r, g: (_nt(r[1], g).astype(r[0].dtype), _nn(r[0], g).astype(r[1].dtype)))


def _rms(x, w):
    return x * lax.rsqrt(jnp.mean(x * x, axis=-1, keepdims=True) + NORM_EPS) * w


def _norm_mod(x, w, scale, shift):
    return _rms(x, w) * (1.0 + scale) + shift


def _rms_bwd(x, w, dy):
    r = lax.rsqrt(jnp.mean(x * x, axis=-1, keepdims=True) + NORM_EPS)
    xhat = x * r
    g = dy * w
    dx = r * (g - xhat * jnp.mean(g * xhat, axis=-1, keepdims=True))
    return dx, jnp.sum(dy * xhat, axis=0, keepdims=True)


def _norm_mod_bwd(x, w, scale, dh):
    dx, da = _rms_bwd(x, w * (1.0 + scale), dh)
    return dx, da * (1.0 + scale), da * w, jnp.sum(dh, axis=0, keepdims=True)


def _gated_rms_bwd(m, gate, w, dy):
    dm, t = _rms_bwd(m, w * gate, dy)
    return dm, t * w, t * gate


def _silu(x):
    return x * jax.nn.sigmoid(x)


def _conv_rows(xin, w, k):
    acc = xin * w[k - 1:k, :]
    for j in range(1, k):
        acc = acc + pltpu.roll(xin, j, axis=0) * w[k - 1 - j:k - j, :]
    return acc


def _conv_rows_t(du, w, k):
    n = du.shape[0]
    acc = du * w[k - 1:k, :]
    for j in range(1, k):
        acc = acc + pltpu.roll(du, n - j, axis=0) * w[k - 1 - j:k - j, :]
    return acc


def _row(i):
    return (i, 0)


def _const(i):
    return (0, 0)


def _vec(n):
    return pl.BlockSpec((1, n), _const)


def _resident(shape):
    return pl.BlockSpec(shape, _const, pipeline_mode=pl.Buffered(1))


def _block_index(p):
    return 4 * p[0] + 2 * p[1] + p[2]


def all_gather(arrs, name):
    n = len(arrs)

    def body(*refs):
        ins, outs = refs[:n], refs[n:2 * n]
        send_sems, recv_sems, local_sems = refs[2 * n:]
        x, y, c = lax.axis_index("x"), lax.axis_index("y"), lax.axis_index("c")
        me, sibling = (x, y, c), (x, y, 1 - c)
        chips = [(1 - x, y), (x, 1 - y), (1 - x, 1 - y)]

        def copy(a, k, block, to, src=None):
            dst = outs[a].at[_block_index(block)]
            return pltpu.make_async_remote_copy(
                src_ref=dst if src is None else src, dst_ref=dst,
                send_sem=send_sems.at[a * 7 + k], recv_sem=recv_sems.at[a * 7 + k],
                device_id=to, device_id_type=MESH_ID)

        mine = [pltpu.make_async_copy(ins[a], outs[a].at[_block_index(me)], local_sems.at[a]) for a in range(n)]
        for cp in mine:
            cp.start()
        first = []
        for a in range(n):
            first.append(copy(a, 0, me, sibling, src=ins[a]))
            first += [copy(a, 1 + j, me, (*chip, c), src=ins[a]) for j, chip in enumerate(chips)]
        for cp in first:
            cp.start()
        passed = []
        for j, chip in enumerate(chips):
            for a in range(n):
                copy(a, 1 + j, (*chip, c), me).wait_recv()
                cp = copy(a, 4 + j, (*chip, c), sibling)
                cp.start()
                passed.append(cp)
        for a in range(n):
            copy(a, 0, sibling, me).wait_recv()
            for j, chip in enumerate(chips):
                copy(a, 4 + j, (*chip, 1 - c), me).wait_recv()
        for cp in first + passed:
            cp.wait_send()
        for cp in mine:
            cp.wait()

    any_spec = pl.BlockSpec(memory_space=pl.ANY)
    return pl.pallas_call(
        body, name=name,
        out_shape=[jax.ShapeDtypeStruct((N_DEV,) + a.shape, a.dtype) for a in arrs],
        in_specs=[any_spec] * n, out_specs=[any_spec] * n,
        scratch_shapes=[pltpu.SemaphoreType.DMA((7 * n,)), pltpu.SemaphoreType.DMA((7 * n,)),
                        pltpu.SemaphoreType.DMA((n,))],
    )(*arrs)


ALL_FLIPS = ((0, 0, 1), (0, 1, 0), (0, 1, 1), (1, 0, 0), (1, 0, 1), (1, 1, 0), (1, 1, 1))
CHIP_FLIPS = ((0, 1, 0), (1, 0, 0), (1, 1, 0))


def _chip_index(p):
    return 2 * p[0] + p[1]


def all_to_all(arrs, name, flips=ALL_FLIPS, index=_block_index):
    n = len(arrs)
    nf = len(flips)

    def body(*refs):
        ins, outs = refs[:n], refs[n:2 * n]
        send_sems, recv_sems, local_sems = refs[2 * n:]
        pos = (lax.axis_index("x"), lax.axis_index("y"), lax.axis_index("c"))
        me = index(pos)
        peers = [tuple(1 - p if f else p for p, f in zip(pos, flip)) for flip in flips]

        def copy(a, k):
            peer = peers[k]
            return pltpu.make_async_remote_copy(
                src_ref=ins[a].at[index(peer)], dst_ref=outs[a].at[me],
                send_sem=send_sems.at[a * nf + k], recv_sem=recv_sems.at[a * nf + k],
                device_id=peer, device_id_type=MESH_ID)

        def landed(a, k):
            slot = outs[a].at[index(peers[k])]
            return pltpu.make_async_remote_copy(
                src_ref=slot, dst_ref=slot,
                send_sem=send_sems.at[a * nf + k], recv_sem=recv_sems.at[a * nf + k],
                device_id=peers[k], device_id_type=MESH_ID)

        mine = [pltpu.make_async_copy(ins[a].at[me], outs[a].at[me], local_sems.at[a]) for a in range(n)]
        for cp in mine:
            cp.start()
        sent = [copy(a, k) for a in range(n) for k in range(nf)]
        for cp in sent:
            cp.start()
        for a in range(n):
            for k in range(nf):
                landed(a, k).wait_recv()
        for cp in sent:
            cp.wait_send()
        for cp in mine:
            cp.wait()

    any_spec = pl.BlockSpec(memory_space=pl.ANY)
    return pl.pallas_call(
        body, name=name,
        out_shape=[jax.ShapeDtypeStruct(a.shape, a.dtype) for a in arrs],
        in_specs=[any_spec] * n, out_specs=[any_spec] * n,
        scratch_shapes=[pltpu.SemaphoreType.DMA((nf * n,)), pltpu.SemaphoreType.DMA((nf * n,)),
                        pltpu.SemaphoreType.DMA((n,))],
    )(*arrs)


def _direct_exchange(src, dst, sems, scatter):
    send_sems, recv_sems, local_sem = sems
    pos = (lax.axis_index("x"), lax.axis_index("y"), lax.axis_index("c"))
    me = _block_index(pos)
    peers = [tuple(1 - p if f else p for p, f in zip(pos, flip)) for flip in ALL_FLIPS]

    def outgoing(k):
        return pltpu.make_async_remote_copy(
            src_ref=src.at[_block_index(peers[k])] if scatter else src, dst_ref=dst.at[me],
            send_sem=send_sems.at[k], recv_sem=recv_sems.at[k], device_id=peers[k], device_id_type=MESH_ID)

    def incoming(k):
        slot = dst.at[_block_index(peers[k])]
        return pltpu.make_async_remote_copy(
            src_ref=slot, dst_ref=slot, send_sem=send_sems.at[k], recv_sem=recv_sems.at[k],
            device_id=peers[k], device_id_type=MESH_ID)

    def local():
        return pltpu.make_async_copy(src.at[me] if scatter else src, dst.at[me], local_sem)

    def start():
        local().start()
        for k in range(len(ALL_FLIPS)):
            outgoing(k).start()

    def finish():
        for k in range(len(ALL_FLIPS)):
            incoming(k).wait_recv()
        for k in range(len(ALL_FLIPS)):
            outgoing(k).wait_send()
        local().wait()

    return start, finish


def hosted_call(body, exchanges, steps, n_in, n_out, **call):
    n_ex = len(exchanges)

    def wrapped(*refs):
        ins, srcs = refs[:n_in], refs[n_in:n_in + n_ex]
        outs = refs[n_in + n_ex:n_in + n_ex + n_out]
        dsts = refs[n_in + n_ex + n_out:n_in + 2 * n_ex + n_out]
        rest = refs[n_in + 2 * n_ex + n_out:]
        scratch, sems = rest[:len(rest) - 3 * n_ex], rest[len(rest) - 3 * n_ex:]
        plans = [_direct_exchange(srcs[e], dsts[e], sems[3 * e:3 * e + 3], exchanges[e][1]) for e in range(n_ex)]

        @pl.when(pl.program_id(0) == 0)
        def _():
            for start, _ in plans:
                start()

        body(*ins, *outs, *scratch)

        @pl.when(pl.program_id(0) == steps - 1)
        def _():
            for _, finish in plans:
                finish()

    any_spec = pl.BlockSpec(memory_space=pl.ANY)
    landings = [jax.ShapeDtypeStruct(src.shape if scatter else (N_DEV,) + src.shape, src.dtype)
                for src, scatter in exchanges]
    n_flips = len(ALL_FLIPS)
    sems = [pltpu.SemaphoreType.DMA((n_flips,)), pltpu.SemaphoreType.DMA((n_flips,)), pltpu.SemaphoreType.DMA(())]
    return pl.pallas_call(
        wrapped, grid=(steps,),
        in_specs=list(call.pop("in_specs")) + [any_spec] * n_ex,
        out_specs=list(call.pop("out_specs")) + [any_spec] * n_ex,
        out_shape=list(call.pop("out_shape")) + landings,
        scratch_shapes=list(call.pop("scratch_shapes", [])) + sems * n_ex,
        **call)


def _grid_call(body, steps, args, exchanges, **call):
    if not exchanges:
        return pl.pallas_call(body, grid=(steps,), **call)(*args)
    srcs = [src for src, _ in exchanges]
    return hosted_call(body, exchanges, steps, len(args), len(call["out_shape"]), **call)(*args, *srcs)


N_CHIPS = 4


def pair_exchange(arrs, name):
    n = len(arrs)

    def body(*refs):
        ins, outs = refs[:n], refs[n:2 * n]
        send_sems, recv_sems = refs[2 * n:]
        x, y, c = lax.axis_index("x"), lax.axis_index("y"), lax.axis_index("c")
        sibling = (x, y, 1 - c)
        sent = []
        for a in range(n):
            for q in range(N_CHIPS):
                cp = pltpu.make_async_remote_copy(
                    src_ref=ins[a].at[2 * q + (1 - c)], dst_ref=outs[a].at[q],
                    send_sem=send_sems.at[a * N_CHIPS + q], recv_sem=recv_sems.at[a * N_CHIPS + q],
                    device_id=sibling, device_id_type=MESH_ID)
                cp.start()
                sent.append(cp)
        for cp in sent:
            cp.wait_recv()
        for cp in sent:
            cp.wait_send()

    any_spec = pl.BlockSpec(memory_space=pl.ANY)
    return pl.pallas_call(
        body, name=name,
        out_shape=[jax.ShapeDtypeStruct((N_CHIPS,) + a.shape[1:], a.dtype) for a in arrs],
        in_specs=[any_spec] * n, out_specs=[any_spec] * n,
        scratch_shapes=[pltpu.SemaphoreType.DMA((N_CHIPS * n,)), pltpu.SemaphoreType.DMA((N_CHIPS * n,))],
    )(*arrs)


def pair_sum(full, got, core, name):
    _, r, n = full.shape
    tr = _row_tile(r)

    def body(c_ref, mine_ref, got_ref, o_ref):
        o_ref[...] = _b(mine_ref[...] + got_ref[...])

    grid_spec = pltpu.PrefetchScalarGridSpec(
        num_scalar_prefetch=1, grid=(N_CHIPS, r // tr),
        in_specs=[pl.BlockSpec((1, tr, n), lambda q, i, c_ref: (2 * q + c_ref[0], i, 0)),
                  pl.BlockSpec((1, tr, n), lambda q, i, c_ref: (q, i, 0))],
        out_specs=pl.BlockSpec((1, tr, n), lambda q, i, c_ref: (q, i, 0)))
    return pl.pallas_call(body, name=name, grid_spec=grid_spec,
                          out_shape=jax.ShapeDtypeStruct((N_CHIPS, r, n), BF16),
                          compiler_params=_params(n_axes=2))(core, full, got)


def _row_tile(r):
    for cand in (256, 128, 64, 32, 16):
        if r % cand == 0 and r > cand:
            return cand
    return r


def ada_fwd(c_all, w_ada, b_cols):
    def body(c_ref, w_ref, b_ref, o_ref):
        o_ref[...] = _nn(_silu(c_ref[...]), w_ref[...]) + b_ref[...]

    return pl.pallas_call(body, name="ada_fwd",
                          out_shape=jax.ShapeDtypeStruct((N_DEV, w_ada.shape[1]), F32),
                          compiler_params=_params(n_axes=0))(c_all, w_ada, b_cols)


def ada_bwd(c_all, g_cols):
    def body(c_ref, g_ref, o_ref):
        o_ref[...] = _tn(_silu(c_ref[...]), g_ref[...])

    return pl.pallas_call(body, name="ada_bwd",
                          out_shape=jax.ShapeDtypeStruct((c_all.shape[1], g_cols.shape[1]), F32),
                          compiler_params=_params(n_axes=0))(c_all, g_cols)


def matmul_tn(a, b, name, bm, bn, tk=512):
    s, m = a.shape
    n = b.shape[1]
    tk = min(tk, s)

    def body(a_ref, b_ref, o_ref):
        @pl.when(pl.program_id(2) == 0)
        def _():
            o_ref[...] = jnp.zeros_like(o_ref)

        o_ref[...] += _tn(a_ref[...], b_ref[...])

    return pl.pallas_call(
        body, name=name, grid=(m // bm, n // bn, s // tk),
        in_specs=[pl.BlockSpec((tk, bm), lambda i, j, k: (k, i)), pl.BlockSpec((tk, bn), lambda i, j, k: (k, j))],
        out_specs=pl.BlockSpec((bm, bn), lambda i, j, k: (i, j)),
        out_shape=jax.ShapeDtypeStruct((m, n), F32),
        compiler_params=_params(n_axes=3))(a, b)


def pre_mix_inproj(x, w, scale, shift, w_in16, exchange=None, tm=4 * TOKEN_TILE):
    s = x.shape[0]
    tm = min(tm, s)

    def body(x_ref, w_ref, sc_ref, sh_ref, win_ref, h_ref, qkv_ref, xbc_ref, z_ref, dt_ref):
        h16 = _b(_norm_mod(x_ref[...], w_ref[...], sc_ref[...], sh_ref[...]))
        h_ref[...] = h16
        dot = lambda lo, hi: jnp.dot(h16, win_ref[:, lo:hi], preferred_element_type=F32)
        qkv_ref[...] = _b(dot(0, OFF_XBC))
        xbc_ref[...] = dot(OFF_XBC, OFF_Z)
        z_ref[...] = dot(OFF_Z, OFF_DT)
        dt_ref[...] = dot(OFF_DT, PROJ_PAD)

    tile = lambda n: pl.BlockSpec((tm, n), _row)
    return _grid_call(
        body, s // tm, (x, w, scale, shift, w_in16), exchange, name="pre_mix_inproj",
        in_specs=[tile(D_MODEL), _vec(D_MODEL), _vec(D_MODEL), _vec(D_MODEL), _resident((D_MODEL, PROJ_PAD))],
        out_specs=[tile(D_MODEL), tile(QKV_W), tile(XBC_WIDTH), tile(SSM_WIDTH), tile(DT_PAD)],
        out_shape=[jax.ShapeDtypeStruct((s, D_MODEL), BF16), jax.ShapeDtypeStruct((s, QKV_W), BF16),
                   jax.ShapeDtypeStruct((s, XBC_WIDTH), F32), jax.ShapeDtypeStruct((s, SSM_WIDTH), F32),
                   jax.ShapeDtypeStruct((s, DT_PAD), F32)],
        compiler_params=_params())


ATTN_QB_FWD, ATTN_QB_BWD = 4, 2


def _attn_tile(q, kp, kc, vp, vc, bias, sinks):
    lq = ATTN_BLOCK
    group = N_Q_HEADS // N_KV_HEADS
    lanes = lax.broadcasted_iota(jnp.int32, (1, 128), 1)
    rid = lax.broadcasted_iota(jnp.int32, (group * lq, 1), 0)
    sink_cols = []
    for hk in range(N_KV_HEADS):
        sink = jnp.zeros((group * lq, 1), F32)
        for g in range(group):
            s_h = jnp.sum(jnp.where(lanes == hk * group + g, sinks, 0.0), axis=-1, keepdims=True)
            sink = jnp.where((rid >= g * lq) & (rid < (g + 1) * lq), s_h, sink)
        sink_cols.append(sink)
    kall = jnp.concatenate([kp, kc], axis=0)
    vall = jnp.concatenate([vp, vc], axis=0)
    blocks = []
    for b in range(q.shape[0] // lq):
        qb = q[b * lq:(b + 1) * lq]
        outs = []
        for hk in range(N_KV_HEADS):
            cols = slice(hk * HEAD_DIM, (hk + 1) * HEAD_DIM)
            kb = kall[b * lq:(b + 2) * lq, cols]
            vb = vall[b * lq:(b + 2) * lq, cols]
            qg = jnp.concatenate([qb[:, (hk * group + g) * HEAD_DIM:(hk * group + g + 1) * HEAD_DIM]
                                  for g in range(group)], axis=0)
            sc = mm_nt(qg, kb) * (HEAD_DIM ** -0.5) + bias[b][hk]
            sink = sink_cols[hk]
            m = lax.stop_gradient(jnp.maximum(jnp.max(sc, axis=-1, keepdims=True), sink))
            p = jnp.exp(sc - m)
            probs = p / (jnp.sum(p, axis=-1, keepdims=True) + jnp.exp(sink - m))
            og = mm(probs, vb)
            outs += [og[g * lq:(g + 1) * lq] for g in range(group)]
        blocks.append(jnp.concatenate(outs, axis=1))
    return jnp.concatenate(blocks, axis=0)


def _attn_tile_bwd(q, kp, kc, vp, vc, bias, sinks, do):
    lq = ATTN_BLOCK
    group = N_Q_HEADS // N_KV_HEADS
    scale = HEAD_DIM ** -0.5
    lanes = lax.broadcasted_iota(jnp.int32, (1, 128), 1)
    rid = lax.broadcasted_iota(jnp.int32, (group * lq, 1), 0)
    sink_cols = []
    for hk in range(N_KV_HEADS):
        sink = jnp.zeros((group * lq, 1), F32)
        for g in range(group):
            s_h = jnp.sum(jnp.where(lanes == hk * group + g, sinks, 0.0), axis=-1, keepdims=True)
            sink = jnp.where((rid >= g * lq) & (rid < (g + 1) * lq), s_h, sink)
        sink_cols.append(sink)
    kall = jnp.concatenate([kp, kc], axis=0)
    vall = jnp.concatenate([vp, vc], axis=0)
    dsk = jnp.zeros((1, 128), F32)
    dq_blocks, dbias = [], []
    nqb = q.shape[0] // lq
    dk_parts = [[None] * nqb for _ in range(N_KV_HEADS)]
    dv_parts = [[None] * nqb for _ in range(N_KV_HEADS)]
    for b in range(nqb):
        qb, dob = q[b * lq:(b + 1) * lq], do[b * lq:(b + 1) * lq]
        dq_heads, dbias_b = [], []
        for hk in range(N_KV_HEADS):
            cols = slice(hk * HEAD_DIM, (hk + 1) * HEAD_DIM)
            kb = kall[b * lq:(b + 2) * lq, cols]
            vb = vall[b * lq:(b + 2) * lq, cols]
            heads = [hk * group + g for g in range(group)]
            qg = jnp.concatenate([qb[:, h * HEAD_DIM:(h + 1) * HEAD_DIM] for h in heads], axis=0)
            dog = jnp.concatenate([dob[:, h * HEAD_DIM:(h + 1) * HEAD_DIM] for h in heads], axis=0)
            sink = sink_cols[hk]
            sc = _nt(qg, kb) * scale + bias[b][hk]
            m = jnp.maximum(jnp.max(sc, axis=-1, keepdims=True), sink)
            p = jnp.exp(sc - m)
            es = jnp.exp(sink - m)
            inv = 1.0 / (jnp.sum(p, axis=-1, keepdims=True) + es)
            probs = p * inv
            dprobs = _nt(dog, vb)
            delta = jnp.sum(probs * dprobs, axis=-1, keepdims=True)
            dsc = probs * (dprobs - delta)
            dbias_b.append(dsc)
            dsink = -(es * inv) * delta
            for g, h in enumerate(heads):
                tot = jnp.sum(dsink[g * lq:(g + 1) * lq], axis=0, keepdims=True)
                dsk = dsk + jnp.where(lanes == h, tot, 0.0)
            dqg = _nn(dsc, kb) * scale
            dq_heads += [dqg[g * lq:(g + 1) * lq] for g in range(group)]
            dk_parts[hk][b] = _tn(dsc, qg) * scale
            dv_parts[hk][b] = _tn(probs, dog)
        dq_blocks.append(jnp.concatenate(dq_heads, axis=1))
        dbias.append(dbias_b)

    def overlap_add(parts):
        chunks = []
        for r in range(nqb + 1):
            acc = None
            if r < nqb:
                acc = parts[r][:lq]
            if r >= 1:
                tail = parts[r - 1][lq:]
                acc = tail if acc is None else acc + tail
            chunks.append(acc)
        return jnp.concatenate(chunks, axis=0)

    dkall = jnp.concatenate([overlap_add(dk_parts[hk]) for hk in range(N_KV_HEADS)], axis=1)
    dvall = jnp.concatenate([overlap_add(dv_parts[hk]) for hk in range(N_KV_HEADS)], axis=1)
    return jnp.concatenate(dq_blocks, axis=0), dkall, dvall, dbias, dsk


def _attn_in_specs(nt, clamp, nqb):
    lq, tq = ATTN_BLOCK, ATTN_BLOCK * nqb
    cur = lambda n: jnp.minimum(n, nt - 1) if clamp else n
    prev = lambda n: jnp.maximum(cur(n) * nqb - 1, 0)
    kcol, vcol = ATTN_WIDTH // KV_WIDTH, ATTN_WIDTH // KV_WIDTH + 1
    return [pl.BlockSpec((tq, ATTN_WIDTH), lambda n: (cur(n), 0)),
            pl.BlockSpec((lq, KV_WIDTH), lambda n: (prev(n), kcol)),
            pl.BlockSpec((tq, KV_WIDTH), lambda n: (cur(n), kcol)),
            pl.BlockSpec((lq, KV_WIDTH), lambda n: (prev(n), vcol)),
            pl.BlockSpec((tq, KV_WIDTH), lambda n: (cur(n), vcol)),
            pl.BlockSpec((2, N_KV_HEADS, 4 * lq, 2 * lq), lambda n: (0, 0, 0, 0)),
            _vec(128)]


def _tile_bias(bias_ref, first, nqb):
    return [[jnp.where(first, bias_ref[1, hk], bias_ref[0, hk]) if b == 0 else bias_ref[0, hk]
             for hk in range(N_KV_HEADS)] for b in range(nqb)]


def attn_fwd(qkv, bias, sinks_rows, exchange=None):
    s = qkv.shape[0]
    nqb = min(ATTN_QB_FWD, s // ATTN_BLOCK)
    tq = ATTN_BLOCK * nqb
    nt = s // tq

    def body(q_ref, kp_ref, kc_ref, vp_ref, vc_ref, bias_ref, sk_ref, o_ref):
        f = lambda r: r[...].astype(F32)
        o = _attn_tile(f(q_ref), f(kp_ref), f(kc_ref), f(vp_ref), f(vc_ref),
                       _tile_bias(bias_ref, pl.program_id(0) == 0, nqb), sk_ref[...])
        o_ref[...] = _b(o)

    return _grid_call(
        body, nt, (qkv, qkv, qkv, qkv, qkv, bias, sinks_rows), exchange, name="attn_fwd",
        in_specs=_attn_in_specs(nt, False, nqb),
        out_specs=[pl.BlockSpec((tq, ATTN_WIDTH), _row)],
        out_shape=[jax.ShapeDtypeStruct((s, ATTN_WIDTH), BF16)],
        compiler_params=_params())


def attn_bwd(qkv, bias, sinks_rows, d_attn, exchange=None):
    s = qkv.shape[0]
    nqb = ATTN_QB_BWD
    lq, tq = ATTN_BLOCK, ATTN_BLOCK * nqb
    nt = s // tq

    def body(q_ref, kp_ref, kc_ref, vp_ref, vc_ref, bias_ref, sk_ref, do_ref,
             dq_ref, dk_ref, dv_ref, dbias_ref, dsk_ref, carry_k, carry_v):
        n = pl.program_id(0)

        @pl.when(n == 0)
        def _():
            dbias_ref[...] = jnp.zeros_like(dbias_ref)
            dsk_ref[...] = jnp.zeros_like(dsk_ref)
            carry_k[...] = jnp.zeros_like(carry_k)
            carry_v[...] = jnp.zeros_like(carry_v)

        @pl.when(n < nt)
        def _():
            f = lambda r: r[...].astype(F32)
            dq, dkall, dvall, dbias, dsk = _attn_tile_bwd(
                f(q_ref), f(kp_ref), f(kc_ref), f(vp_ref), f(vc_ref), _tile_bias(bias_ref, n == 0, nqb), sk_ref[...],
                f(do_ref))
            dkp, dkc, dvp, dvc = dkall[:lq], dkall[lq:], dvall[:lq], dvall[lq:]
            dq_ref[...] = _b(dq)
            done = tq - lq
            dk_ref[:done, :] = _b(carry_k[:done, :])
            dv_ref[:done, :] = _b(carry_v[:done, :])
            dk_ref[done:, :] = _b(carry_k[done:, :] + dkp)
            dv_ref[done:, :] = _b(carry_v[done:, :] + dvp)
            carry_k[...] = dkc
            carry_v[...] = dvc
            dsk_ref[...] += dsk
            first = (n == 0).astype(F32)
            for hk in range(N_KV_HEADS):
                total = dbias[0][hk]
                for b in range(1, nqb):
                    total = total + dbias[b][hk]
                dbias_ref[0, hk] += total - first * dbias[0][hk]
                dbias_ref[1, hk] += first * dbias[0][hk]

        @pl.when(n == nt)
        def _():
            dk_ref[...] = _b(carry_k[...])
            dv_ref[...] = _b(carry_v[...])

    cur = lambda n: (jnp.minimum(n, nt - 1), 0)
    done_map = lambda n: (jnp.maximum(n - 1, 0), 0)
    return _grid_call(
        body, nt + 1, (qkv, qkv, qkv, qkv, qkv, bias, sinks_rows, d_attn), exchange, name="attn_bwd",
        in_specs=_attn_in_specs(nt, True, nqb) + [pl.BlockSpec((tq, ATTN_WIDTH), cur)],
        out_specs=[pl.BlockSpec((tq, ATTN_WIDTH), cur), pl.BlockSpec((tq, KV_WIDTH), done_map),
                   pl.BlockSpec((tq, KV_WIDTH), done_map),
                   pl.BlockSpec((2, N_KV_HEADS, 4 * lq, 2 * lq), lambda n: (0, 0, 0, 0)), _vec(128)],
        out_shape=[jax.ShapeDtypeStruct((s, ATTN_WIDTH), BF16), jax.ShapeDtypeStruct((s, KV_WIDTH), BF16),
                   jax.ShapeDtypeStruct((s, KV_WIDTH), BF16),
                   jax.ShapeDtypeStruct((2, N_KV_HEADS, 4 * lq, 2 * lq), F32), jax.ShapeDtypeStruct((1, 128), F32)],
        scratch_shapes=[pltpu.VMEM((tq, KV_WIDTH), F32), pltpu.VMEM((tq, KV_WIDTH), F32)],
        compiler_params=_params())


def rel_bias_table(rel_bias, bucket):
    lq = ATTN_BLOCK
    group = N_Q_HEADS // N_KV_HEADS

    def body(rb_ref, bk_ref, o_ref):
        bk = bk_ref[...]
        prev_keys = lax.broadcasted_iota(jnp.int32, bk.shape, 1) < lq
        accs = [jnp.full(bk.shape, -1e30, F32) for _ in range(N_Q_HEADS)]
        for b in range(N_BUCKETS):
            hit = bk == b
            accs = [jnp.where(hit, rb_ref[b, h], acc) for h, acc in enumerate(accs)]
        for h in range(N_Q_HEADS):
            rows = slice((h % group) * lq, (h % group + 1) * lq)
            o_ref[0, h // group, rows, :] = accs[h]
            o_ref[1, h // group, rows, :] = jnp.where(prev_keys, -1e30, accs[h])

    return pl.pallas_call(
        body, name="rel_bias_table",
        in_specs=[pl.BlockSpec(memory_space=pltpu.SMEM), pl.BlockSpec(memory_space=pltpu.VMEM)],
        out_shape=jax.ShapeDtypeStruct((2, N_KV_HEADS, group * lq, 2 * lq), F32),
        compiler_params=_params(n_axes=0))(rel_bias, bucket)


def rel_bias_grad(dbias, bucket):
    lq = ATTN_BLOCK
    group = N_Q_HEADS // N_KV_HEADS

    def body(db_ref, bk_ref, o_ref):
        rows = lax.broadcasted_iota(jnp.int32, (N_BUCKETS, 128), 0)
        lanes = lax.broadcasted_iota(jnp.int32, (N_BUCKETS, 128), 1)
        bk = bk_ref[...]
        per_head = []
        for h in range(N_Q_HEADS):
            sl = slice((h % group) * lq, (h % group + 1) * lq)
            per_head.append(db_ref[0, h // group, sl, :] + db_ref[1, h // group, sl, :])

        def per_bucket(b, acc):
            hit = (bk == b).astype(F32)
            for h in range(N_Q_HEADS):
                val = jnp.sum(per_head[h] * hit, keepdims=True)
                acc = acc + jnp.where((rows == b) & (lanes == h), val, 0.0)
            return acc

        o_ref[...] = lax.fori_loop(0, N_BUCKETS, per_bucket, jnp.zeros((N_BUCKETS, 128), F32))

    return pl.pallas_call(body, name="rel_bias_grad", out_shape=jax.ShapeDtypeStruct((N_BUCKETS, 128), F32),
                          compiler_params=_params(n_axes=0))(dbias, bucket)


def _tri_sum(a, upper):
    n = a.shape[0]
    ri = lax.broadcasted_iota(jnp.int32, (n, n), 0)
    ci = lax.broadcasted_iota(jnp.int32, (n, n), 1)
    tri = ((ri <= ci) if upper else (ri >= ci)).astype(BF16)
    hi = a.astype(BF16)
    rest = a - hi.astype(F32)
    mid = rest.astype(BF16)
    lo = (rest - mid.astype(F32)).astype(BF16)
    dot = lambda part: jnp.dot(tri, part, preferred_element_type=F32)
    return dot(hi) + dot(mid) + dot(lo)


@jax.custom_vjp
def _cumsum_rows(a):
    return _tri_sum(a, False)


_cumsum_rows.defvjp(lambda a: (_tri_sum(a, False), None), lambda _, g: (_tri_sum(g, True),))


def _ssm_core(u, z, dt_raw, hprev, dt_bias, a_log, dskip, norm_w):
    lc = u.shape[0]
    xbc = _silu(u)
    xs, bm, cm = xbc[:, :SSM_WIDTH], xbc[:, SSM_WIDTH:SSM_WIDTH + SSM_BC], xbc[:, SSM_WIDTH + SSM_BC:]
    dt = jax.nn.softplus(dt_raw + dt_bias)
    adt = dt * (-jnp.exp(a_log))
    ri = lax.broadcasted_iota(jnp.int32, (lc, lc), 0)
    ci = lax.broadcasted_iota(jnp.int32, (lc, lc), 1)
    causal = ri >= ci
    acum = _cumsum_rows(adt)
    acum_t = acum.T
    last = acum[lc - 1:lc, :]
    per_group = SSM_HEADS // SSM_GROUPS
    lane = lax.broadcasted_iota(jnp.int32, (1, 128), 1)
    rowid = lax.broadcasted_iota(jnp.int32, (128, 1), 0)
    lo_lanes = lane < SSM_HEAD_DIM
    ys, hs = [], []
    for g in range(SSM_GROUPS):
        bg = bm[:, g * SSM_STATE:(g + 1) * SSM_STATE]
        cg = cm[:, g * SSM_STATE:(g + 1) * SSM_STATE]
        cb = mm_nt(cg, bg)
        for pp in range(per_group // 2):
            ha = g * per_group + 2 * pp
            xp = xs[:, ha * SSM_HEAD_DIM:(ha + 2) * SSM_HEAD_DIM]
            hp = hprev[ha * SSM_HEAD_DIM:(ha + 2) * SSM_HEAD_DIM, :]
            xcp = xp * jnp.where(lo_lanes, dt[:, ha:ha + 1], dt[:, ha + 1:ha + 2])
            y_h, st_h = [], []
            for h in (ha, ha + 1):
                col, rowv, lasth = acum[:, h:h + 1], acum_t[h:h + 1, :], last[:, h:h + 1]
                decay = jnp.exp(jnp.where(causal, col - rowv, -1e30))
                y_h.append(mm(cb * decay, xcp) + mm_nt(cg * jnp.exp(col), hp))
                st_h.append(mm_tn(xcp, bg * jnp.exp(lasth - col)))
            y_pair = jnp.where(lo_lanes, y_h[0], y_h[1])
            st_pair = jnp.where(rowid < SSM_HEAD_DIM, st_h[0], st_h[1])
            la, lb = last[:, ha:ha + 1], last[:, ha + 1:ha + 2]
            hs.append(jnp.exp(jnp.where(rowid < SSM_HEAD_DIM, la, lb)) * hp + st_pair)
            dsk = jnp.where(lo_lanes, dskip[:, ha:ha + 1], dskip[:, ha + 1:ha + 2])
            ys.append(y_pair + dsk * xp)
    y = jnp.concatenate(ys, axis=1) * _silu(z)
    gw = SSM_WIDTH // SSM_GROUPS
    outs = []
    for g in range(SSM_GROUPS):
        yg = y[:, g * gw:(g + 1) * gw]
        outs.append(yg * lax.rsqrt(jnp.mean(yg * yg, axis=-1, keepdims=True) + NORM_EPS))
    return jnp.concatenate(outs, axis=1) * norm_w, jnp.concatenate(hs, axis=0)


def _ssm_param_specs():
    return [pl.BlockSpec((SSM_CONV, XBC_WIDTH), _const), _vec(XBC_WIDTH), _vec(128), _vec(128), _vec(128),
            _vec(SSM_WIDTH)]


SSM_FWD_SUB = 2
SSM_BWD_SUB = 1


def ssm_fwd(xbc_raw, z, dt_raw, conv_w, conv_b, dt_bias, a_log, dskip, norm_w, exchange=None):
    s = xbc_raw.shape[0]
    lc = SSM_CHUNK
    lt = lc * SSM_FWD_SUB
    hrows = SSM_HEADS * SSM_HEAD_DIM

    def body(x_ref, halo_ref, z_ref, dt_ref, cw_ref, cb_ref, dtb_ref, al_ref, dk_ref, nw_ref,
             o_ref, hp_ref, state):
        i = pl.program_id(0)

        @pl.when(i == 0)
        def _():
            state[...] = jnp.zeros_like(state)

        halo = halo_ref[...] * (i > 0).astype(F32)
        xin = jnp.concatenate([halo, x_ref[...]], axis=0)
        u = (_conv_rows(xin, cw_ref[...], SSM_CONV) + cb_ref[...])[HALO:]
        h = state[...]
        for k in range(SSM_FWD_SUB):
            rows = slice(k * lc, (k + 1) * lc)
            hp_ref[k * hrows:(k + 1) * hrows, :] = h
            out, h = _ssm_core(u[rows], z_ref[rows, :], dt_ref[rows, :], h, dtb_ref[...], al_ref[...], dk_ref[...],
                               nw_ref[...])
            o_ref[rows, :] = _b(out)
        state[...] = h

    tile = lambda n: pl.BlockSpec((lt, n), _row)
    halo_spec = pl.BlockSpec((HALO, XBC_WIDTH), lambda i: (jnp.maximum(i * (lt // HALO) - 1, 0), 0))
    return _grid_call(
        body, s // lt, (xbc_raw, xbc_raw, z, dt_raw, conv_w, conv_b, dt_bias, a_log, dskip, norm_w), exchange,
        name="ssm_fwd",
        in_specs=[tile(XBC_WIDTH), halo_spec, tile(SSM_WIDTH), tile(DT_PAD)] + _ssm_param_specs(),
        out_specs=[tile(SSM_WIDTH), pl.BlockSpec((SSM_FWD_SUB * hrows, SSM_STATE), _row)],
        out_shape=[jax.ShapeDtypeStruct((s, SSM_WIDTH), BF16),
                   jax.ShapeDtypeStruct((s // lc * hrows, SSM_STATE), F32)],
        scratch_shapes=[pltpu.VMEM((hrows, SSM_STATE), F32)],
        compiler_params=_params())


def ssm_bwd(xbc_raw, z, dt_raw, hprev_all, d_out, conv_w, conv_b, dt_bias, a_log, dskip, norm_w, exchange=None):
    s = xbc_raw.shape[0]
    lc = SSM_CHUNK
    sub = SSM_BWD_SUB
    lt = lc * sub
    nt = s // lt
    hrows = SSM_HEADS * SSM_HEAD_DIM

    def body(x_ref, halo_ref, z_ref, dt_ref, hp_ref, do_ref, cw_ref, cb_ref, dtb_ref, al_ref, dk_ref, nw_ref,
             dx_ref, dz_ref, ddt_ref, dcw_ref, dcb_ref, ddtb_ref, dal_ref, ddk_ref, dnw_ref, dstate, du_next):
        i = pl.program_id(0)
        tile_no = nt - 1 - i

        @pl.when(i == 0)
        def _():
            dstate[...] = jnp.zeros_like(dstate)
            du_next[...] = jnp.zeros_like(du_next)
            for r in (dcw_ref, dcb_ref, ddtb_ref, dal_ref, ddk_ref, dnw_ref):
                r[...] = jnp.zeros_like(r)

        halo = halo_ref[...] * (tile_no > 0).astype(F32)
        xin = jnp.concatenate([halo, x_ref[...]], axis=0)
        cw = cw_ref[...]
        u = (_conv_rows(xin, cw, SSM_CONV) + cb_ref[...])[HALO:]
        dh = dstate[...]
        dus = [None] * sub
        for k in reversed(range(sub)):
            rows = slice(k * lc, (k + 1) * lc)
            _, vjp = jax.vjp(_ssm_core, u[rows], z_ref[rows, :], dt_ref[rows, :], hp_ref[k * hrows:(k + 1) * hrows, :],
                             dtb_ref[...], al_ref[...], dk_ref[...], nw_ref[...])
            dus[k], dz, ddt, dh, ddtb, dal, ddk, dnw = vjp((do_ref[rows, :], dh))
            dz_ref[rows, :] = _b(dz)
            ddt_ref[rows, :] = _b(ddt)
            ddtb_ref[...] += ddtb
            dal_ref[...] += dal
            ddk_ref[...] += ddk
            dnw_ref[...] += dnw
        dstate[...] = dh
        du = jnp.concatenate(dus, axis=0)
        du_ext = jnp.concatenate([du, du_next[...]], axis=0)
        dx_ref[...] = _b(_conv_rows_t(du_ext, cw, SSM_CONV)[:lt])
        du_next[...] = du[:HALO]
        sums = [jnp.sum(du * pltpu.roll(xin, j, axis=0)[HALO:] if j else du * xin[HALO:], axis=0, keepdims=True)
                for j in range(SSM_CONV)]
        dcw_ref[...] += jnp.concatenate(sums[::-1] + [jnp.zeros((8 - SSM_CONV, XBC_WIDTH), F32)], axis=0)
        dcb_ref[...] += jnp.sum(du, axis=0, keepdims=True)

    rev = lambda i: (nt - 1 - i, 0)
    tile = lambda n: pl.BlockSpec((lt, n), rev)
    halo_spec = pl.BlockSpec((HALO, XBC_WIDTH), lambda i: (jnp.maximum((nt - 1 - i) * (lt // HALO) - 1, 0), 0))
    acc = lambda r, n: pl.BlockSpec((r, n), _const)
    return _grid_call(
        body, nt, (xbc_raw, xbc_raw, z, dt_raw, hprev_all, d_out, conv_w, conv_b, dt_bias, a_log, dskip, norm_w),
        exchange, name="ssm_bwd",
        in_specs=[tile(XBC_WIDTH), halo_spec, tile(SSM_WIDTH), tile(DT_PAD),
                  pl.BlockSpec((sub * hrows, SSM_STATE), rev), tile(SSM_WIDTH)] + _ssm_param_specs(),
        out_specs=[tile(XBC_WIDTH), tile(SSM_WIDTH), tile(DT_PAD), acc(8, XBC_WIDTH), acc(1, XBC_WIDTH),
                   acc(1, 128), acc(1, 128), acc(1, 128), acc(1, SSM_WIDTH)],
        out_shape=[jax.ShapeDtypeStruct((s, XBC_WIDTH), BF16), jax.ShapeDtypeStruct((s, SSM_WIDTH), BF16),
                   jax.ShapeDtypeStruct((s, DT_PAD), BF16), jax.ShapeDtypeStruct((8, XBC_WIDTH), F32),
                   jax.ShapeDtypeStruct((1, XBC_WIDTH), F32), jax.ShapeDtypeStruct((1, 128), F32),
                   jax.ShapeDtypeStruct((1, 128), F32), jax.ShapeDtypeStruct((1, 128), F32),
                   jax.ShapeDtypeStruct((1, SSM_WIDTH), F32)],
        scratch_shapes=[pltpu.VMEM((hrows, SSM_STATE), F32), pltpu.VMEM((HALO, XBC_WIDTH), F32)],
        compiler_params=_params())


def mix_out(attn, ssm, x, w_out16, gate1, post_mix_w, pre_ffn_w, scale2, shift2, tm=4 * TOKEN_TILE):
    s = x.shape[0]
    tm = min(tm, s)

    def body(a_ref, s_ref, x_ref, w_ref, g_ref, pw_ref, fw_ref, sc_ref, sh_ref, mixed_ref, x1_ref, h2_ref):
        mixed = (jnp.dot(a_ref[...], w_ref[:ATTN_WIDTH, :], preferred_element_type=F32)
                 + jnp.dot(s_ref[...], w_ref[ATTN_WIDTH:, :], preferred_element_type=F32))
        mixed_ref[...] = _b(mixed)
        x1 = x_ref[...] + g_ref[...] * _rms(mixed, pw_ref[...])
        x1_ref[...] = x1
        h2_ref[...] = _b(_norm_mod(x1, fw_ref[...], sc_ref[...], sh_ref[...]))

    tile = lambda n: pl.BlockSpec((tm, n), _row)
    return pl.pallas_call(
        body, name="mix_out", grid=(s // tm,),
        in_specs=[tile(ATTN_WIDTH), tile(SSM_WIDTH), tile(D_MODEL), _resident((D_MODEL, D_MODEL))]
        + [_vec(D_MODEL)] * 5,
        out_specs=[tile(D_MODEL)] * 3,
        out_shape=[jax.ShapeDtypeStruct((s, D_MODEL), BF16), jax.ShapeDtypeStruct((s, D_MODEL), F32),
                   jax.ShapeDtypeStruct((s, D_MODEL), BF16)],
        compiler_params=_params())(attn, ssm, x, w_out16, gate1, post_mix_w, pre_ffn_w, scale2, shift2)


GELU_K0, GELU_K1 = math.sqrt(2.0 / math.pi), 0.044715


def _gate(ug, uv):
    return jax.nn.gelu(ug, approximate=True) * uv


def _gate_bwd(ug, uv, df):
    sq = ug * ug
    t = jnp.tanh(ug * (GELU_K0 + (GELU_K0 * GELU_K1) * sq))
    half = 0.5 + 0.5 * t
    slope = half + ug * (1.0 - t * t) * (0.5 * GELU_K0 + (1.5 * GELU_K0 * GELU_K1) * sq)
    return df * uv * slope, df * (ug * half)


def up_gate(h2, w_up16, conv_w, conv_b, tm=TOKEN_TILE):
    s = h2.shape[0]

    def body(h_ref, halo_ref, w_ref, cw_ref, cb_ref, u_ref, uraw_ref, f_ref):
        halo = halo_ref[...]
        halo = jnp.where(pl.program_id(0) > 0, halo, jnp.zeros_like(halo))
        hin = jnp.concatenate([halo, h_ref[...]], axis=0)
        for lo in range(0, D_FF, FF_CHUNK):
            halves = []
            for base in (lo, D_FF + lo):
                cols = slice(base, base + FF_CHUNK)
                uraw = jnp.dot(hin, w_ref[:, cols], preferred_element_type=F32)
                uraw_ref[:, cols] = _b(uraw[NEXT:])
                u = (_conv_rows(uraw, cw_ref[:, cols], FFN_CONV) + cb_ref[:, cols])[NEXT:]
                u_ref[:, cols] = u
                halves.append(u)
            f_ref[:, lo:lo + FF_CHUNK] = _b(_gate(*halves))

    tile = lambda n: pl.BlockSpec((tm, n), _row)
    halo_spec = pl.BlockSpec((NEXT, D_MODEL), lambda i: (jnp.maximum(i * (tm // NEXT) - 1, 0), 0))
    return pl.pallas_call(
        body, name="up_gate", grid=(s // tm,),
        in_specs=[tile(D_MODEL), halo_spec, _resident((D_MODEL, 2 * D_FF)),
                  pl.BlockSpec((FFN_CONV, 2 * D_FF), _const), _vec(2 * D_FF)],
        out_specs=[tile(2 * D_FF), tile(2 * D_FF), tile(D_FF)],
        out_shape=[jax.ShapeDtypeStruct((s, 2 * D_FF), F32), jax.ShapeDtypeStruct((s, 2 * D_FF), BF16),
                   jax.ShapeDtypeStruct((s, D_FF), BF16)],
        compiler_params=_params())(h2, h2, w_up16, conv_w, conv_b)


DOWN_LOSS_TILE = 512


def down_loss(f16, w_down16, x1, target, gate2, post_ffn_w, tm=DOWN_LOSS_TILE):
    s = x1.shape[0]
    tm = min(tm, s)

    def body(f_ref, wd_ref, x1_ref, t_ref, g_ref, pw_ref, dffn_ref, dy_ref, loss_ref, dg_ref, dpw_ref, gw_ref):
        i = pl.program_id(0)

        @pl.when(i == 0)
        def _():
            loss_ref[...] = jnp.zeros_like(loss_ref)
            dg_ref[...] = jnp.zeros_like(dg_ref)
            dpw_ref[...] = jnp.zeros_like(dpw_ref)
            gw_ref[...] = jnp.zeros_like(gw_ref)

        ffn = jnp.dot(f_ref[...], wd_ref[...], preferred_element_type=F32)
        x1 = x1_ref[...]
        x2 = x1 + g_ref[...] * _rms(ffn, pw_ref[...])
        err = x2 - t_ref[...]
        dy = err * (1.0 / D_MODEL)
        dy_ref[...] = dy
        loss_ref[...] += 0.5 * jnp.sum(jnp.mean(err * err, axis=-1, keepdims=True))
        dffn, dg, dpw = _gated_rms_bwd(ffn, g_ref[...], pw_ref[...], dy)
        dffn16 = _b(dffn)
        dffn_ref[...] = dffn16
        dg_ref[...] += dg
        dpw_ref[...] += dpw
        gw_ref[...] += _tn(f_ref[...], dffn16)

    tile = lambda n: pl.BlockSpec((tm, n), _row)
    return pl.pallas_call(
        body, name="down_loss", grid=(s // tm,),
        in_specs=[tile(D_FF), _resident((D_FF, D_MODEL)), tile(D_MODEL), tile(D_MODEL), _vec(D_MODEL), _vec(D_MODEL)],
        out_specs=[tile(D_MODEL), tile(D_MODEL), _vec(128), _vec(D_MODEL), _vec(D_MODEL),
                   pl.BlockSpec((D_FF, D_MODEL), _const)],
        out_shape=[jax.ShapeDtypeStruct((s, D_MODEL), BF16), jax.ShapeDtypeStruct((s, D_MODEL), F32),
                   jax.ShapeDtypeStruct((1, 128), F32), jax.ShapeDtypeStruct((1, D_MODEL), F32),
                   jax.ShapeDtypeStruct((1, D_MODEL), F32), jax.ShapeDtypeStruct((D_FF, D_MODEL), F32)],
        compiler_params=_params())(f16, w_down16, x1, target, gate2, post_ffn_w)


BWD_CHUNK = 256


def ffn_bwd(u, u_raw16, d_ffn, conv_w, w_down_t16, w_up_t16, tm=TOKEN_TILE):
    s = u.shape[0]
    nt = s // tm

    def body(u_ref, unext_ref, uraw_ref, d_ref, dnext_ref, cw_ref, wdt_ref, wut_ref,
             du_ref, dh_ref, dcw_ref, dcb_ref):
        i = pl.program_id(0)

        @pl.when(i == 0)
        def _():
            dcw_ref[...] = jnp.zeros_like(dcw_ref)
            dcb_ref[...] = jnp.zeros_like(dcb_ref)

        dnext = dnext_ref[...]
        dnext = jnp.where(i < nt - 1, dnext, jnp.zeros_like(dnext))
        dff = jnp.concatenate([d_ref[...], dnext], axis=0)
        rows_ext = tm + NEXT
        for lo in range(0, D_FF, BWD_CHUNK):
            gcols, vcols = slice(lo, lo + BWD_CHUNK), slice(D_FF + lo, D_FF + lo + BWD_CHUNK)
            ug = jnp.concatenate([u_ref[:, gcols], unext_ref[:, gcols]], axis=0)
            uv = jnp.concatenate([u_ref[:, vcols], unext_ref[:, vcols]], axis=0)
            df = jnp.dot(dff, wdt_ref[:, gcols], preferred_element_type=F32)
            for cols, du in zip((gcols, vcols), _gate_bwd(ug, uv, df)):
                cw = cw_ref[:, cols]
                du1 = pltpu.roll(du, rows_ext - 1, axis=0)
                du2 = pltpu.roll(du, rows_ext - 2, axis=0)
                du_ref[:, cols] = _b((du * cw[2:3, :] + du1 * cw[1:2, :] + du2 * cw[0:1, :])[:tm])
                xr = uraw_ref[:, cols].astype(F32)
                rows = [jnp.sum(xr * d_[:tm], axis=0, keepdims=True) for d_ in (du2, du1, du)]
                dcw_ref[:, cols] += jnp.concatenate(rows + [jnp.zeros((8 - FFN_CONV, BWD_CHUNK), F32)], axis=0)
                dcb_ref[:, cols] += jnp.sum(du[:tm], axis=0, keepdims=True)
        dh_ref[...] = jnp.dot(du_ref[...], wut_ref[...], preferred_element_type=F32)

    tile = lambda n: pl.BlockSpec((tm, n), _row)
    nxt = lambda i: (jnp.minimum((i + 1) * (tm // NEXT), s // NEXT - 1), 0)
    return pl.pallas_call(
        body, name="ffn_bwd", grid=(nt,),
        in_specs=[tile(2 * D_FF), pl.BlockSpec((NEXT, 2 * D_FF), nxt), tile(2 * D_FF), tile(D_MODEL),
                  pl.BlockSpec((NEXT, D_MODEL), nxt), pl.BlockSpec((FFN_CONV, 2 * D_FF), _const),
                  _resident((D_MODEL, D_FF)), _resident((2 * D_FF, D_MODEL))],
        out_specs=[tile(2 * D_FF), tile(D_MODEL), pl.BlockSpec((8, 2 * D_FF), _const), _vec(2 * D_FF)],
        out_shape=[jax.ShapeDtypeStruct((s, 2 * D_FF), BF16), jax.ShapeDtypeStruct((s, D_MODEL), F32),
                   jax.ShapeDtypeStruct((8, 2 * D_FF), F32), jax.ShapeDtypeStruct((1, 2 * D_FF), F32)],
        compiler_params=_params())(u, u, u_raw16, d_ffn, d_ffn, conv_w, w_down_t16, w_up_t16)


def mix_bwd(dh2, x1, dy, mixed, attn, ssm, w_out_t16, pre_ffn_w, scale2, gate1, post_mix_w, tm=2 * TOKEN_TILE):
    s = x1.shape[0]

    def body(dh_ref, x1_ref, dy_ref, mx_ref, a_ref, s_ref, w_ref, fw_ref, sc_ref, g_ref, pw_ref,
             dx1_ref, da_ref, ds_ref, dfw_ref, dsc_ref, dsh_ref, dg_ref, dpw_ref, gw_ref):
        accs = (dfw_ref, dsc_ref, dsh_ref, dg_ref, dpw_ref)

        @pl.when(pl.program_id(0) == 0)
        def _():
            for r in accs + (gw_ref,):
                r[...] = jnp.zeros_like(r)

        dx1, dfw, dsc, dsh = _norm_mod_bwd(x1_ref[...], fw_ref[...], sc_ref[...], dh_ref[...])
        dx1 = dx1 + dy_ref[...]
        dx1_ref[...] = dx1
        dmixed, dg, dpw = _gated_rms_bwd(mx_ref[...].astype(F32), g_ref[...], pw_ref[...], dx1)
        dm16 = _b(dmixed)
        dmix_in = jnp.dot(dm16, w_ref[...], preferred_element_type=F32)
        da_ref[...] = _b(dmix_in[:, :ATTN_WIDTH])
        ds_ref[...] = dmix_in[:, ATTN_WIDTH:]
        gw_ref[:ATTN_WIDTH, :] += _tn(a_ref[...], dm16)
        gw_ref[ATTN_WIDTH:, :] += _tn(s_ref[...], dm16)
        for r, v in zip(accs, (dfw, dsc, dsh, dg, dpw)):
            r[...] += v

    tile = lambda n: pl.BlockSpec((tm, n), _row)
    return pl.pallas_call(
        body, name="mix_bwd", grid=(s // tm,),
        in_specs=[tile(D_MODEL)] * 4 + [tile(ATTN_WIDTH), tile(SSM_WIDTH), _resident((D_MODEL, D_MODEL))]
        + [_vec(D_MODEL)] * 4,
        out_specs=[tile(D_MODEL), tile(ATTN_WIDTH), tile(SSM_WIDTH)] + [_vec(D_MODEL)] * 5
        + [pl.BlockSpec((D_MODEL, D_MODEL), _const)],
        out_shape=[jax.ShapeDtypeStruct((s, D_MODEL), F32), jax.ShapeDtypeStruct((s, ATTN_WIDTH), BF16),
                   jax.ShapeDtypeStruct((s, SSM_WIDTH), F32)]
        + [jax.ShapeDtypeStruct((1, D_MODEL), F32)] * 5 + [jax.ShapeDtypeStruct((D_MODEL, D_MODEL), F32)],
        compiler_params=_params())(dh2, x1, dy, mixed, attn, ssm, w_out_t16, pre_ffn_w, scale2, gate1, post_mix_w)


INPROJ_BWD_TILE = 512


def inproj_bwd(dq, dk, dv, dxbc, dz, ddt, x, dx1, h1, w_in_t16, pre_mix_w, scale1, tm=INPROJ_BWD_TILE):
    s = x.shape[0]
    tm = min(tm, s)

    def body(dq_ref, dk_ref, dv_ref, dxbc_ref, dz_ref, ddt_ref, x_ref, dx1_ref, h_ref, w_ref, pw_ref, sc_ref,
             gx_ref, dpw_ref, dsc_ref, dsh_ref, gw_ref):
        accs = (dpw_ref, dsc_ref, dsh_ref)

        @pl.when(pl.program_id(0) == 0)
        def _():
            for r in accs + (gw_ref,):
                r[...] = jnp.zeros_like(r)

        h16 = h_ref[...]
        dh = None
        dqkv = jnp.concatenate([dq_ref[...], dk_ref[...], dv_ref[...]], axis=1)
        for d16, lo, hi in ((dqkv, 0, OFF_XBC), (dxbc_ref[...], OFF_XBC, OFF_Z), (dz_ref[...], OFF_Z, OFF_DT),
                            (ddt_ref[...], OFF_DT, PROJ_PAD)):
            part = jnp.dot(d16, w_ref[lo:hi, :], preferred_element_type=F32)
            dh = part if dh is None else dh + part
            gw_ref[lo:hi, :] += _tn(d16, h16)
        dx, dpw, dsc, dsh = _norm_mod_bwd(x_ref[...], pw_ref[...], sc_ref[...], dh)
        gx_ref[...] = dx1_ref[...] + dx
        for r, v in zip(accs, (dpw, dsc, dsh)):
            r[...] += v

    tile = lambda n: pl.BlockSpec((tm, n), _row)
    return pl.pallas_call(
        body, name="inproj_bwd", grid=(s // tm,),
        in_specs=[tile(ATTN_WIDTH), tile(KV_WIDTH), tile(KV_WIDTH), tile(XBC_WIDTH), tile(SSM_WIDTH), tile(DT_PAD),
                  tile(D_MODEL), tile(D_MODEL), tile(D_MODEL), _resident((PROJ_PAD, D_MODEL))] + [_vec(D_MODEL)] * 2,
        out_specs=[tile(D_MODEL)] + [_vec(D_MODEL)] * 3 + [pl.BlockSpec((PROJ_PAD, D_MODEL), _const)],
        out_shape=[jax.ShapeDtypeStruct((s, D_MODEL), F32)] + [jax.ShapeDtypeStruct((1, D_MODEL), F32)] * 3
        + [jax.ShapeDtypeStruct((PROJ_PAD, D_MODEL), F32)],
        compiler_params=_params())(dq, dk, dv, dxbc, dz, ddt, x, dx1, h1, w_in_t16, pre_mix_w, scale1)


def _adam(g, w, m, v):
    new_m = ADAM_B1 * m + (1.0 - ADAM_B1) * g
    new_v = ADAM_B2 * v + (1.0 - ADAM_B2) * jnp.square(g)
    m_hat = new_m / (1.0 - ADAM_B1 ** ADAM_STEP)
    v_hat = new_v / (1.0 - ADAM_B2 ** ADAM_STEP)
    return -ADAM_LR * (m_hat / (jnp.sqrt(v_hat) + ADAM_EPS) + ADAM_WD * w), new_m, new_v


ROW_PARAMS = (("b_ada", 6144, 6144), ("pre_mix_w", 1024, 1024), ("attn_sinks", 128, 8), ("ssm_conv_b", 1024, 1024),
              ("ssm_dt_bias", 128, 8), ("ssm_a_log", 128, 8), ("ssm_d", 128, 8), ("ssm_norm_w", 512, 512),
              ("post_mix_w", 1024, 1024), ("pre_ffn_w", 1024, 1024), ("ffn_conv_b", 5632, 5632),
              ("post_ffn_w", 1024, 1024))
LOSS_LANES = 128


def adamw_small(row_all, rb_all, rel_bias_wmv, row_wmv):
    n_rows = len(ROW_PARAMS)

    def body(*refs):
        row_ref, rb_ref = refs[:2]
        wmv = refs[2:5 + 3 * n_rows]
        outs = refs[5 + 3 * n_rows:]
        g_row, g_rb = row_ref[0], rb_ref[0]
        for k in range(1, N_DEV):
            g_row = g_row + row_ref[k]
            g_rb = g_rb + rb_ref[k]
        outs[0][...] = g_row[:, :LOSS_LANES]
        grads = [g_rb[:, :N_Q_HEADS]]
        off = LOSS_LANES
        for _, lanes, width in ROW_PARAMS:
            grads.append(g_row[:, off:off + width])
            off += lanes
        for i, g in enumerate(grads):
            w_ref, m_ref, v_ref = wmv[3 * i:3 * i + 3]
            g_out, d_out, m_out, v_out = outs[1 + 4 * i:5 + 4 * i]
            g_out[...] = g
            d_out[...], m_out[...], v_out[...] = _adam(g, w_ref[...], m_ref[...], v_ref[...])

    flat_wmv = list(rel_bias_wmv) + [a for wmv in row_wmv for a in wmv]
    shapes = [jax.ShapeDtypeStruct((1, LOSS_LANES), F32)] + [jax.ShapeDtypeStruct((N_BUCKETS, N_Q_HEADS), F32)] * 4
    for _, _, width in ROW_PARAMS:
        shapes += [jax.ShapeDtypeStruct((1, width), F32)] * 4
    return pl.pallas_call(body, name="adamw_small", out_shape=shapes,
                          compiler_params=_params(n_axes=0))(row_all, rb_all, *flat_wmv)


def adamw(parts, w, m, v, name):
    p, r, n = parts.shape
    tr = _row_tile(r)

    def body(p_ref, w_ref, m_ref, v_ref, g_ref, d_ref, nm_ref, nv_ref):
        g = p_ref[0].astype(F32)
        for k in range(1, p):
            g = g + p_ref[k].astype(F32)
        g_ref[...] = g
        d_ref[...], nm_ref[...], nv_ref[...] = _adam(g, w_ref[...], m_ref[...], v_ref[...])

    tile = pl.BlockSpec((tr, n), _row)
    return pl.pallas_call(
        body, name=name, grid=(r // tr,),
        in_specs=[pl.BlockSpec((p, tr, n), lambda i: (0, i, 0)), tile, tile, tile],
        out_specs=[tile] * 4, out_shape=[jax.ShapeDtypeStruct((r, n), F32)] * 4,
        compiler_params=_params())(parts, w, m, v)


def _bucket_table():
    lq = ATTN_BLOCK
    qi = np.arange(lq)[:, None] + lq
    kj = np.arange(2 * lq)[None, :]
    dist = qi - kj
    d = np.maximum(dist, 0)
    max_exact = N_BUCKETS // 2
    nf = np.maximum(d, 1).astype(np.float32)
    large = max_exact + (np.log(nf / max_exact) / math.log(REL_MAX_DIST / max_exact)
                         * (N_BUCKETS - max_exact)).astype(np.int32)
    large = np.minimum(large, N_BUCKETS - 1)
    bucket = np.where(d < max_exact, d, large).astype(np.int32)
    in_band = (dist >= 0) & (dist < REL_MAX_DIST)
    return np.where(in_band, bucket, -1).astype(np.int32)


def _cols_from_blocks(g):
    return jnp.transpose(g, (1, 0, 2)).reshape(g.shape[1], N_DEV * g.shape[2])


def _cols_to_blocks(a):
    r, n = a.shape
    return jnp.transpose(a.reshape(r, N_DEV, n // N_DEV), (1, 0, 2))


def _perm_in_rows(wt):
    pad = jnp.zeros((DT_PAD - SSM_HEADS, wt.shape[1]), wt.dtype)
    return jnp.concatenate([wt[:768], wt[768:1280], wt[1792:2304], wt[1280:1792], wt[2304:2312], pad], axis=0)


def _unperm_in_rows(gt):
    return jnp.concatenate([gt[:768], gt[768:1280], gt[1792:2304], gt[1280:1792], gt[2304:2312]], axis=0)


def _lane_pad(v, n=128):
    return jnp.pad(v, ((0, 0), (0, n - v.shape[1])))


def kernel(x, c, rel_bias, w_ada, b_ada, pre_mix_w, w_in, attn_sinks, ssm_conv_w, ssm_conv_b, ssm_dt_bias, ssm_a_log, ssm_d, ssm_norm_w, w_out, post_mix_w, pre_ffn_w, w_up, ffn_conv_w, ffn_conv_b, w_down, post_ffn_w, loss_target, m_rel_bias, m_w_ada, m_b_ada, m_pre_mix_w, m_w_in, m_attn_sinks, m_ssm_conv_w, m_ssm_conv_b, m_ssm_dt_bias, m_ssm_a_log, m_ssm_d, m_ssm_norm_w, m_w_out, m_post_mix_w, m_pre_ffn_w, m_w_up, m_ffn_conv_w, m_ffn_conv_b, m_w_down, m_post_ffn_w, v_rel_bias, v_w_ada, v_b_ada, v_pre_mix_w, v_w_in, v_attn_sinks, v_ssm_conv_w, v_ssm_conv_b, v_ssm_dt_bias, v_ssm_a_log, v_ssm_d, v_ssm_norm_w, v_w_out, v_post_mix_w, v_pre_ffn_w, v_w_up, v_ffn_conv_w, v_ffn_conv_b, v_w_down, v_post_ffn_w):
    weights = dict(rel_bias=rel_bias, w_ada=w_ada, b_ada=b_ada, pre_mix_w=pre_mix_w, w_in=w_in, attn_sinks=attn_sinks, ssm_conv_w=ssm_conv_w, ssm_conv_b=ssm_conv_b, ssm_dt_bias=ssm_dt_bias, ssm_a_log=ssm_a_log, ssm_d=ssm_d, ssm_norm_w=ssm_norm_w, w_out=w_out, post_mix_w=post_mix_w, pre_ffn_w=pre_ffn_w, w_up=w_up, ffn_conv_w=ffn_conv_w, ffn_conv_b=ffn_conv_b, w_down=w_down, post_ffn_w=post_ffn_w)
    mom_m = dict(rel_bias=m_rel_bias, w_ada=m_w_ada, b_ada=m_b_ada, pre_mix_w=m_pre_mix_w, w_in=m_w_in, attn_sinks=m_attn_sinks, ssm_conv_w=m_ssm_conv_w, ssm_conv_b=m_ssm_conv_b, ssm_dt_bias=m_ssm_dt_bias, ssm_a_log=m_ssm_a_log, ssm_d=m_ssm_d, ssm_norm_w=m_ssm_norm_w, w_out=m_w_out, post_mix_w=m_post_mix_w, pre_ffn_w=m_pre_ffn_w, w_up=m_w_up, ffn_conv_w=m_ffn_conv_w, ffn_conv_b=m_ffn_conv_b, w_down=m_w_down, post_ffn_w=m_post_ffn_w)
    mom_v = dict(rel_bias=v_rel_bias, w_ada=v_w_ada, b_ada=v_b_ada, pre_mix_w=v_pre_mix_w, w_in=v_w_in, attn_sinks=v_attn_sinks, ssm_conv_w=v_ssm_conv_w, ssm_conv_b=v_ssm_conv_b, ssm_dt_bias=v_ssm_dt_bias, ssm_a_log=v_ssm_a_log, ssm_d=v_ssm_d, ssm_norm_w=v_ssm_norm_w, w_out=v_w_out, post_mix_w=v_post_mix_w, pre_ffn_w=v_pre_ffn_w, w_up=v_w_up, ffn_conv_w=v_ffn_conv_w, ffn_conv_b=v_ffn_conv_b, w_down=v_w_down, post_ffn_w=v_post_ffn_w)
    order = ['rel_bias', 'w_ada', 'b_ada', 'pre_mix_w', 'w_in', 'attn_sinks', 'ssm_conv_w', 'ssm_conv_b', 'ssm_dt_bias', 'ssm_a_log', 'ssm_d', 'ssm_norm_w', 'w_out', 'post_mix_w', 'pre_ffn_w', 'w_up', 'ffn_conv_w', 'ffn_conv_b', 'w_down', 'post_ffn_w']

    me = 4 * lax.axis_index("x") + 2 * lax.axis_index("y") + lax.axis_index("c")
    xs_ = x[0]
    target = loss_target[0]

    (w_in_g, scw_g, fcw_g, c_g) = all_gather([_b(w_in[0]).T, ssm_conv_w[0], ffn_conv_w[0], c], "gather_weights")
    w_in_t16 = _perm_in_rows(w_in_g.reshape(IN_PROJ_WIDTH, D_MODEL))
    w_in16 = w_in_t16.T
    ssm_cw = _cols_from_blocks(scw_g)
    ffn_cw = _cols_from_blocks(fcw_g)
    c_all = c_g.reshape(N_DEV, D_MODEL)

    n_cols = w_ada.shape[2]
    b_cols = lax.dynamic_slice(b_ada, (0, me * n_cols), (1, n_cols))
    mod_part = ada_fwd(c_all, w_ada[0], b_cols)
    (mod_rows,) = all_to_all([mod_part.reshape(N_DEV, 1, n_cols)], "scatter_mod")
    mod = mod_rows.reshape(N_MOD, 1, D_MODEL)
    shift1, scale1, gate1, shift2, scale2, gate2 = (mod[i] for i in range(N_MOD))

    bucket_band = jnp.asarray(_bucket_table())
    bias = rel_bias_table(rel_bias, bucket_band)
    sinks_row = _lane_pad(attn_sinks)
    dt_bias, a_log, dskip = _lane_pad(ssm_dt_bias), _lane_pad(ssm_a_log), _lane_pad(ssm_d)

    h1, qkv, xbc_raw, z, dt_raw, w_out_g = pre_mix_inproj(
        xs_, pre_mix_w, scale1, shift1, w_in16, [(_b(w_out[0]), False)])
    attn, w_down_g = attn_fwd(qkv, bias, sinks_row, [(_b(w_down[0]), False)])
    ssm, hprev_all, w_up_g = ssm_fwd(xbc_raw, z, dt_raw, ssm_cw, ssm_conv_b, dt_bias, a_log, dskip, ssm_norm_w,
                                     [(_b(w_up[0]).T, False)])
    w_out16 = w_out_g.reshape(D_MODEL, D_MODEL)
    w_out_t16 = w_out16.T
    w_up_t16 = w_up_g.reshape(2 * D_FF, D_MODEL)
    w_up16 = w_up_t16.T
    w_down16 = w_down_g.reshape(D_FF, D_MODEL)
    w_down_t16 = w_down16.T
    mixed, x1, h2 = mix_out(attn, ssm, xs_, w_out16, gate1, post_mix_w, pre_ffn_w, scale2, shift2)
    u, u_raw16, f16 = up_gate(h2, w_up16, ffn_cw, ffn_conv_b)
    d_ffn, dy, loss_part, d_gate2, d_post_ffn_w, g_w_down = down_loss(f16, w_down16, x1, target, gate2, post_ffn_w)

    du_raw, dh2, d_ffn_cw, d_ffn_cb = ffn_bwd(u, u_raw16, d_ffn, ffn_cw, w_down_t16, w_up_t16)
    g_w_up_t = matmul_tn(du_raw, h2, "grad_w_up", FF_CHUNK, D_MODEL, tk=2048)
    (dx1, d_attn, d_ssm, d_pre_ffn_w, d_scale2, d_shift2, d_gate1, d_post_mix_w, g_w_out) = mix_bwd(
        dh2, x1, dy, mixed, attn, ssm, w_out_t16, pre_ffn_w, scale2, gate1, post_mix_w)
    dq, dk, dv, dbias, dsinks, p_w_down = attn_bwd(
        qkv, bias, sinks_row, d_attn, [(g_w_down.reshape(N_DEV, D_FF // N_DEV, D_MODEL), True)])
    d_rel_bias = rel_bias_grad(dbias, bucket_band)
    (dxbc, dz, ddt, d_ssm_cw, d_ssm_cb, d_dt_bias, d_a_log, d_dskip, d_norm_w, p_w_up, p_w_out) = ssm_bwd(
        xbc_raw, z, dt_raw, hprev_all, d_ssm, ssm_cw, ssm_conv_b, dt_bias, a_log, dskip, ssm_norm_w,
        [(g_w_up_t.reshape(N_DEV, 2 * D_FF // N_DEV, D_MODEL), True),
         (g_w_out.reshape(N_DEV, D_MODEL // N_DEV, D_MODEL), True)])
    grad_x, d_pre_mix_w, d_scale1, d_shift1, g_w_in_perm = inproj_bwd(
        dq, dk, dv, dxbc, dz, ddt, xs_, dx1, h1, w_in_t16, pre_mix_w, scale1)
    g_w_in_t = _unperm_in_rows(g_w_in_perm)

    d_mod = jnp.concatenate([d_shift1, d_scale1, d_gate1, d_shift2, d_scale2, d_gate2], axis=1)
    late = ("w_in", "ssm_conv_w", "ffn_conv_w")
    full = [g_w_in_t.reshape(N_DEV, IN_PROJ_WIDTH // N_DEV, D_MODEL), _cols_to_blocks(d_ssm_cw[:SSM_CONV]),
            _cols_to_blocks(d_ffn_cw[:FFN_CONV])]
    core = lax.axis_index("c").astype(jnp.int32).reshape(1)
    got = pair_exchange(full, "pair_grads")
    chip_sums = [pair_sum(f_, g_, core, "pair_sum_" + k) for k, f_, g_ in zip(late, full, got)]
    chip_parts = all_to_all(chip_sums, "scatter_grads", CHIP_FLIPS, _chip_index)

    row_g = dict(b_ada=d_mod, pre_mix_w=d_pre_mix_w, attn_sinks=dsinks, ssm_conv_b=d_ssm_cb, ssm_dt_bias=d_dt_bias,
                 ssm_a_log=d_a_log, ssm_d=d_dskip, ssm_norm_w=d_norm_w, post_mix_w=d_post_mix_w,
                 pre_ffn_w=d_pre_ffn_w, ffn_conv_b=d_ffn_cb, post_ffn_w=d_post_ffn_w)
    row = jnp.concatenate([loss_part] + [row_g[k] for k, _, _ in ROW_PARAMS], axis=1)
    row_all, rb_all = all_gather([row, d_rel_bias], "gather_small")
    d_mod_cols = lax.dynamic_slice(row_all.reshape(N_DEV, row.shape[1]), (0, LOSS_LANES + me * n_cols),
                                   (N_DEV, n_cols))
    g_w_ada = ada_bwd(c_all, d_mod_cols)

    wmv = lambda k: (weights[k], mom_m[k], mom_v[k])
    small = adamw_small(row_all, rb_all, wmv("rel_bias"), [wmv(k) for k, _, _ in ROW_PARAMS])
    loss = small[0][0, 0]
    res = {k: tuple(small[1 + 4 * i:5 + 4 * i]) for i, k in enumerate(["rel_bias"] + [k for k, _, _ in ROW_PARAMS])}
    big = list(zip(late, chip_parts)) + [("w_down", p_w_down), ("w_up", p_w_up), ("w_out", p_w_out),
                                        ("w_ada", g_w_ada[None])]
    for k, parts in big:
        if k in ("w_in", "w_up"):
            outs_t = adamw(parts, weights[k][0].T, mom_m[k][0].T, mom_v[k][0].T, "adamw_" + k)
            res[k] = tuple(o.T[None] for o in outs_t)
        else:
            res[k] = tuple(o[None] for o in adamw(parts, weights[k][0], mom_m[k][0], mom_v[k][0], "adamw_" + k))

    outs = [loss, grad_x[None]]
    for field in range(4):
        outs += [res[k][field] for k in order]
    return tuple(outs)
```

```python
import math

import numpy as np
import jax
import jax.numpy as jnp
from jax import lax
from jax.experimental import pallas as pl
from jax.experimental.pallas import tpu as pltpu

F32 = jnp.float32
BF16 = jnp.bfloat16
MESH_ID = pl.DeviceIdType.MESH

N_DEV = 8
D_MODEL = 1024
N_Q_HEADS = 8
N_KV_HEADS = 2
HEAD_DIM = 64
ATTN_WIDTH = 512
KV_WIDTH = 128
ATTN_BLOCK = 128
N_BUCKETS = 32
REL_MAX_DIST = 128
SSM_HEADS = 8
SSM_HEAD_DIM = 64
SSM_WIDTH = 512
SSM_STATE = 128
SSM_GROUPS = 2
SSM_BC = 256
SSM_CONV = 4
SSM_CHUNK = 256
XBC_WIDTH = SSM_WIDTH + 2 * SSM_BC
D_FF = 2816
FFN_CONV = 3
NORM_EPS = 1e-6
N_MOD = 6
IN_PROJ_WIDTH = 2312
QKV_W = ATTN_WIDTH + 2 * KV_WIDTH
COLS_XS = (QKV_W, QKV_W + SSM_WIDTH)
COLS_Z = (COLS_XS[1], COLS_XS[1] + SSM_WIDTH)
COLS_BC = (COLS_Z[1], COLS_Z[1] + 2 * SSM_BC)
DT_PAD = 128
COLS_DT = (COLS_BC[1], COLS_BC[1] + DT_PAD)
PROJ_PAD = COLS_DT[1]
FF_CHUNK = 1408

ADAM_LR = 0.001
ADAM_B1 = 0.9
ADAM_B2 = 0.999
ADAM_EPS = 1e-08
ADAM_WD = 0.01
ADAM_STEP = 10

TOKEN_TILE = 256
HALO = 8
NEXT = 16
VMEM_LIMIT = 56 * 1024 * 1024


def _params(vmem=VMEM_LIMIT, n_axes=1):
    return pltpu.CompilerParams(dimension_semantics=("arbitrary",) * n_axes, vmem_limit_bytes=vmem)


def _b(x):
    return x.astype(BF16)


def _nn(a, b):
    return jnp.dot(_b(a), _b(b), preferred_element_type=F32)


def _nt(a, b):
    return lax.dot_general(_b(a), _b(b), (((1,), (1,)), ((), ())), preferred_element_type=F32)


def _tn(a, b):
    return lax.dot_general(_b(a), _b(b), (((0,), (0,)), ((), ())), preferred_element_type=F32)


@jax.custom_vjp
def mm(a, b):
    return _nn(a, b)


mm.defvjp(lambda a, b: (_nn(a, b), (a, b)),
          lambda r, g: (_nt(g, r[1]).astype(r[0].dtype), _tn(r[0], g).astype(r[1].dtype)))


@jax.custom_vjp
def mm_nt(a, b):
    return _nt(a, b)


mm_nt.defvjp(lambda a, b: (_nt(a, b), (a, b)),
             lambda r, g: (_nn(g, r[1]).astype(r[0].dtype), _tn(g, r[0]).astype(r[1].dtype)))


@jax.custom_vjp
def mm_tn(a, b):
    return _tn(a, b)


mm_tn.defvjp(lambda a, b: (_tn(a, b), (a, b)),
             lambda r, g: (_nt(r[1], g).astype(r[0].dtype), _nn(r[0], g).astype(r[1].dtype)))


def _rms(x, w):
    return x * lax.rsqrt(jnp.mean(x * x, axis=-1, keepdims=True) + NORM_EPS) * w


def _norm_mod(x, w, scale, shift):
    return _rms(x, w) * (1.0 + scale) + shift


def _rms_bwd(x, w, dy):
    r = lax.rsqrt(jnp.mean(x * x, axis=-1, keepdims=True) + NORM_EPS)
    xhat = x * r
    g = dy * w
    dx = r * (g - xhat * jnp.mean(g * xhat, axis=-1, keepdims=True))
    return dx, jnp.sum(dy * xhat, axis=0, keepdims=True)


def _norm_mod_bwd(x, w, scale, dh):
    dx, da = _rms_bwd(x, w * (1.0 + scale), dh)
    return dx, da * (1.0 + scale), da * w, jnp.sum(dh, axis=0, keepdims=True)


def _gated_rms_bwd(m, gate, w, dy):
    dm, t = _rms_bwd(m, w * gate, dy)
    return dm, t * w, t * gate


def _silu(x):
    return x * jax.nn.sigmoid(x)


def _conv_rows(xin, w, k):
    acc = xin * w[k - 1:k, :]
    for j in range(1, k):
        acc = acc + pltpu.roll(xin, j, axis=0) * w[k - 1 - j:k - j, :]
    return acc


def _conv_rows_t(du, w, k):
    n = du.shape[0]
    acc = du * w[k - 1:k, :]
    for j in range(1, k):
        acc = acc + pltpu.roll(du, n - j, axis=0) * w[k - 1 - j:k - j, :]
    return acc


def _row(i):
    return (i, 0)


def _const(i):
    return (0, 0)


def _vec(n):
    return pl.BlockSpec((1, n), _const)


def _resident(shape):
    return pl.BlockSpec(shape, _const, pipeline_mode=pl.Buffered(1))


def _block_index(p):
    return 4 * p[0] + 2 * p[1] + p[2]


def all_gather(arrs, name):
    n = len(arrs)

    def body(*refs):
        ins, outs = refs[:n], refs[n:2 * n]
        send_sems, recv_sems, local_sems = refs[2 * n:]
        x, y, c = lax.axis_index("x"), lax.axis_index("y"), lax.axis_index("c")
        me, sibling = (x, y, c), (x, y, 1 - c)
        chips = [(1 - x, y), (x, 1 - y), (1 - x, 1 - y)]

        def copy(a, k, block, to, src=None):
            dst = outs[a].at[_block_index(block)]
            return pltpu.make_async_remote_copy(
                src_ref=dst if src is None else src, dst_ref=dst,
                send_sem=send_sems.at[a * 7 + k], recv_sem=recv_sems.at[a * 7 + k],
                device_id=to, device_id_type=MESH_ID)

        mine = [pltpu.make_async_copy(ins[a], outs[a].at[_block_index(me)], local_sems.at[a]) for a in range(n)]
        for cp in mine:
            cp.start()
        first = []
        for a in range(n):
            first.append(copy(a, 0, me, sibling, src=ins[a]))
            first += [copy(a, 1 + j, me, (*chip, c), src=ins[a]) for j, chip in enumerate(chips)]
        for cp in first:
            cp.start()
        passed = []
        for j, chip in enumerate(chips):
            for a in range(n):
                copy(a, 1 + j, (*chip, c), me).wait_recv()
                cp = copy(a, 4 + j, (*chip, c), sibling)
                cp.start()
                passed.append(cp)
        for a in range(n):
            copy(a, 0, sibling, me).wait_recv()
            for j, chip in enumerate(chips):
                copy(a, 4 + j, (*chip, 1 - c), me).wait_recv()
        for cp in first + passed:
            cp.wait_send()
        for cp in mine:
            cp.wait()

    any_spec = pl.BlockSpec(memory_space=pl.ANY)
    return pl.pallas_call(
        body, name=name,
        out_shape=[jax.ShapeDtypeStruct((N_DEV,) + a.shape, a.dtype) for a in arrs],
        in_specs=[any_spec] * n, out_specs=[any_spec] * n,
        scratch_shapes=[pltpu.SemaphoreType.DMA((7 * n,)), pltpu.SemaphoreType.DMA((7 * n,)),
                        pltpu.SemaphoreType.DMA((n,))],
    )(*arrs)


ALL_FLIPS = ((0, 0, 1), (0, 1, 0), (0, 1, 1), (1, 0, 0), (1, 0, 1), (1, 1, 0), (1, 1, 1))
CHIP_FLIPS = ((0, 1, 0), (1, 0, 0), (1, 1, 0))


def _chip_index(p):
    return 2 * p[0] + p[1]


def all_to_all(arrs, name, flips=ALL_FLIPS, index=_block_index):
    n = len(arrs)
    nf = len(flips)

    def body(*refs):
        ins, outs = refs[:n], refs[n:2 * n]
        send_sems, recv_sems, local_sems = refs[2 * n:]
        pos = (lax.axis_index("x"), lax.axis_index("y"), lax.axis_index("c"))
        me = index(pos)
        peers = [tuple(1 - p if f else p for p, f in zip(pos, flip)) for flip in flips]

        def copy(a, k):
            peer = peers[k]
            return pltpu.make_async_remote_copy(
                src_ref=ins[a].at[index(peer)], dst_ref=outs[a].at[me],
                send_sem=send_sems.at[a * nf + k], recv_sem=recv_sems.at[a * nf + k],
                device_id=peer, device_id_type=MESH_ID)

        def landed(a, k):
            slot = outs[a].at[index(peers[k])]
            return pltpu.make_async_remote_copy(
                src_ref=slot, dst_ref=slot,
                send_sem=send_sems.at[a * nf + k], recv_sem=recv_sems.at[a * nf + k],
                device_id=peers[k], device_id_type=MESH_ID)

        mine = [pltpu.make_async_copy(ins[a].at[me], outs[a].at[me], local_sems.at[a]) for a in range(n)]
        for cp in mine:
            cp.start()
        sent = [copy(a, k) for a in range(n) for k in range(nf)]
        for cp in sent:
            cp.start()
        for a in range(n):
            for k in range(nf):
                landed(a, k).wait_recv()
        for cp in sent:
            cp.wait_send()
        for cp in mine:
            cp.wait()

    any_spec = pl.BlockSpec(memory_space=pl.ANY)
    return pl.pallas_call(
        body, name=name,
        out_shape=[jax.ShapeDtypeStruct(a.shape, a.dtype) for a in arrs],
        in_specs=[any_spec] * n, out_specs=[any_spec] * n,
        scratch_shapes=[pltpu.SemaphoreType.DMA((nf * n,)), pltpu.SemaphoreType.DMA((nf * n,)),
                        pltpu.SemaphoreType.DMA((n,))],
    )(*arrs)


def _direct_exchange(src, dst, sems, scatter):
    send_sems, recv_sems, local_sem = sems
    pos = (lax.axis_index("x"), lax.axis_index("y"), lax.axis_index("c"))
    me = _block_index(pos)
    peers = [tuple(1 - p if f else p for p, f in zip(pos, flip)) for flip in ALL_FLIPS]

    def outgoing(k):
        return pltpu.make_async_remote_copy(
            src_ref=src.at[_block_index(peers[k])] if scatter else src, dst_ref=dst.at[me],
            send_sem=send_sems.at[k], recv_sem=recv_sems.at[k], device_id=peers[k], device_id_type=MESH_ID)

    def incoming(k):
        slot = dst.at[_block_index(peers[k])]
        return pltpu.make_async_remote_copy(
            src_ref=slot, dst_ref=slot, send_sem=send_sems.at[k], recv_sem=recv_sems.at[k],
            device_id=peers[k], device_id_type=MESH_ID)

    def local():
        return pltpu.make_async_copy(src.at[me] if scatter else src, dst.at[me], local_sem)

    def start():
        local().start()
        for k in range(len(ALL_FLIPS)):
            outgoing(k).start()

    def finish():
        for k in range(len(ALL_FLIPS)):
            incoming(k).wait_recv()
        for k in range(len(ALL_FLIPS)):
            outgoing(k).wait_send()
        local().wait()

    return start, finish


def hosted_call(body, exchanges, steps, n_in, n_out, **call):
    n_ex = len(exchanges)

    def wrapped(*refs):
        ins, srcs = refs[:n_in], refs[n_in:n_in + n_ex]
        outs = refs[n_in + n_ex:n_in + n_ex + n_out]
        dsts = refs[n_in + n_ex + n_out:n_in + 2 * n_ex + n_out]
        rest = refs[n_in + 2 * n_ex + n_out:]
        scratch, sems = rest[:len(rest) - 3 * n_ex], rest[len(rest) - 3 * n_ex:]
        plans = [_direct_exchange(srcs[e], dsts[e], sems[3 * e:3 * e + 3], exchanges[e][1]) for e in range(n_ex)]

        @pl.when(pl.program_id(0) == 0)
        def _():
            for start, _ in plans:
                start()

        body(*ins, *outs, *scratch)

        @pl.when(pl.program_id(0) == steps - 1)
        def _():
            for _, finish in plans:
                finish()

    any_spec = pl.BlockSpec(memory_space=pl.ANY)
    landings = [jax.ShapeDtypeStruct(src.shape if scatter else (N_DEV,) + src.shape, src.dtype)
                for src, scatter in exchanges]
    n_flips = len(ALL_FLIPS)
    sems = [pltpu.SemaphoreType.DMA((n_flips,)), pltpu.SemaphoreType.DMA((n_flips,)), pltpu.SemaphoreType.DMA(())]
    return pl.pallas_call(
        wrapped, grid=(steps,),
        in_specs=list(call.pop("in_specs")) + [any_spec] * n_ex,
        out_specs=list(call.pop("out_specs")) + [any_spec] * n_ex,
        out_shape=list(call.pop("out_shape")) + landings,
        scratch_shapes=list(call.pop("scratch_shapes", [])) + sems * n_ex,
        **call)


def _grid_call(body, steps, args, exchanges, **call):
    if not exchanges:
        return pl.pallas_call(body, grid=(steps,), **call)(*args)
    srcs = [src for src, _ in exchanges]
    return hosted_call(body, exchanges, steps, len(args), len(call["out_shape"]), **call)(*args, *srcs)


N_CHIPS = 4


def pair_exchange(arrs, name):
    n = len(arrs)

    def body(*refs):
        ins, outs = refs[:n], refs[n:2 * n]
        send_sems, recv_sems = refs[2 * n:]
        x, y, c = lax.axis_index("x"), lax.axis_index("y"), lax.axis_index("c")
        sibling = (x, y, 1 - c)
        sent = []
        for a in range(n):
            for q in range(N_CHIPS):
                cp = pltpu.make_async_remote_copy(
                    src_ref=ins[a].at[2 * q + (1 - c)], dst_ref=outs[a].at[q],
                    send_sem=send_sems.at[a * N_CHIPS + q], recv_sem=recv_sems.at[a * N_CHIPS + q],
                    device_id=sibling, device_id_type=MESH_ID)
                cp.start()
                sent.append(cp)
        for cp in sent:
            cp.wait_recv()
        for cp in sent:
            cp.wait_send()

    any_spec = pl.BlockSpec(memory_space=pl.ANY)
    return pl.pallas_call(
        body, name=name,
        out_shape=[jax.ShapeDtypeStruct((N_CHIPS,) + a.shape[1:], a.dtype) for a in arrs],
        in_specs=[any_spec] * n, out_specs=[any_spec] * n,
        scratch_shapes=[pltpu.SemaphoreType.DMA((N_CHIPS * n,)), pltpu.SemaphoreType.DMA((N_CHIPS * n,))],
    )(*arrs)


def pair_sum(full, got, core, name):
    _, r, n = full.shape
    tr = _row_tile(r)

    def body(c_ref, mine_ref, got_ref, o_ref):
        o_ref[...] = _b(mine_ref[...] + got_ref[...])

    grid_spec = pltpu.PrefetchScalarGridSpec(
        num_scalar_prefetch=1, grid=(N_CHIPS, r // tr),
        in_specs=[pl.BlockSpec((1, tr, n), lambda q, i, c_ref: (2 * q + c_ref[0], i, 0)),
                  pl.BlockSpec((1, tr, n), lambda q, i, c_ref: (q, i, 0))],
        out_specs=pl.BlockSpec((1, tr, n), lambda q, i, c_ref: (q, i, 0)))
    return pl.pallas_call(body, name=name, grid_spec=grid_spec,
                          out_shape=jax.ShapeDtypeStruct((N_CHIPS, r, n), BF16),
                          compiler_params=_params(n_axes=2))(core, full, got)


def _row_tile(r):
    for cand in (256, 128, 64, 32, 16):
        if r % cand == 0 and r > cand:
            return cand
    return r


def ada_fwd(c_all, w_ada, b_cols):
    def body(c_ref, w_ref, b_ref, o_ref):
        o_ref[...] = _nn(_silu(c_ref[...]), w_ref[...]) + b_ref[...]

    return pl.pallas_call(body, name="ada_fwd",
                          out_shape=jax.ShapeDtypeStruct((N_DEV, w_ada.shape[1]), F32),
                          compiler_params=_params(n_axes=0))(c_all, w_ada, b_cols)


def ada_bwd(c_all, g_cols):
    def body(c_ref, g_ref, o_ref):
        o_ref[...] = _tn(_silu(c_ref[...]), g_ref[...])

    return pl.pallas_call(body, name="ada_bwd",
                          out_shape=jax.ShapeDtypeStruct((c_all.shape[1], g_cols.shape[1]), F32),
                          compiler_params=_params(n_axes=0))(c_all, g_cols)


def matmul_tn(a, b, name, bm, bn, tk=512):
    s, m = a.shape
    n = b.shape[1]
    tk = min(tk, s)

    def body(a_ref, b_ref, o_ref):
        @pl.when(pl.program_id(2) == 0)
        def _():
            o_ref[...] = jnp.zeros_like(o_ref)

        o_ref[...] += _tn(a_ref[...], b_ref[...])

    return pl.pallas_call(
        body, name=name, grid=(m // bm, n // bn, s // tk),
        in_specs=[pl.BlockSpec((tk, bm), lambda i, j, k: (k, i)), pl.BlockSpec((tk, bn), lambda i, j, k: (k, j))],
        out_specs=pl.BlockSpec((bm, bn), lambda i, j, k: (i, j)),
        out_shape=jax.ShapeDtypeStruct((m, n), F32),
        compiler_params=_params(n_axes=3))(a, b)


def pre_mix_inproj(x, w, scale, shift, w_in16, exchange=None, tm=4 * TOKEN_TILE):
    s = x.shape[0]
    tm = min(tm, s)

    def body(x_ref, w_ref, sc_ref, sh_ref, win_ref, h_ref, qkv_ref, xbc_ref, z_ref, dt_ref):
        h16 = _b(_norm_mod(x_ref[...], w_ref[...], sc_ref[...], sh_ref[...]))
        h_ref[...] = h16
        dot = lambda lo, hi: jnp.dot(h16, win_ref[:, lo:hi], preferred_element_type=F32)
        qkv_ref[...] = _b(dot(0, QKV_W))
        xbc_ref[:, :SSM_WIDTH] = dot(*COLS_XS)
        xbc_ref[:, SSM_WIDTH:] = dot(*COLS_BC)
        z_ref[...] = dot(*COLS_Z)
        dt_ref[...] = dot(*COLS_DT)

    tile = lambda n: pl.BlockSpec((tm, n), _row)
    return _grid_call(
        body, s // tm, (x, w, scale, shift, w_in16), exchange, name="pre_mix_inproj",
        in_specs=[tile(D_MODEL), _vec(D_MODEL), _vec(D_MODEL), _vec(D_MODEL), _resident((D_MODEL, PROJ_PAD))],
        out_specs=[tile(D_MODEL), tile(QKV_W), tile(XBC_WIDTH), tile(SSM_WIDTH), tile(DT_PAD)],
        out_shape=[jax.ShapeDtypeStruct((s, D_MODEL), BF16), jax.ShapeDtypeStruct((s, QKV_W), BF16),
                   jax.ShapeDtypeStruct((s, XBC_WIDTH), F32), jax.ShapeDtypeStruct((s, SSM_WIDTH), F32),
                   jax.ShapeDtypeStruct((s, DT_PAD), F32)],
        compiler_params=_params())


ATTN_QB_FWD, ATTN_QB_BWD = 4, 2


def _attn_tile(q, kp, kc, vp, vc, bias, sinks):
    lq = ATTN_BLOCK
    group = N_Q_HEADS // N_KV_HEADS
    lanes = lax.broadcasted_iota(jnp.int32, (1, 128), 1)
    rid = lax.broadcasted_iota(jnp.int32, (group * lq, 1), 0)
    sink_cols = []
    for hk in range(N_KV_HEADS):
        sink = jnp.zeros((group * lq, 1), F32)
        for g in range(group):
            s_h = jnp.sum(jnp.where(lanes == hk * group + g, sinks, 0.0), axis=-1, keepdims=True)
            sink = jnp.where((rid >= g * lq) & (rid < (g + 1) * lq), s_h, sink)
        sink_cols.append(sink)
    kall = jnp.concatenate([kp, kc], axis=0)
    vall = jnp.concatenate([vp, vc], axis=0)
    blocks = []
    for b in range(q.shape[0] // lq):
        qb = q[b * lq:(b + 1) * lq]
        outs = []
        for hk in range(N_KV_HEADS):
            cols = slice(hk * HEAD_DIM, (hk + 1) * HEAD_DIM)
            kb = kall[b * lq:(b + 2) * lq, cols]
            vb = vall[b * lq:(b + 2) * lq, cols]
            qg = jnp.concatenate([qb[:, (hk * group + g) * HEAD_DIM:(hk * group + g + 1) * HEAD_DIM]
                                  for g in range(group)], axis=0)
            sc = mm_nt(qg, kb) * (HEAD_DIM ** -0.5) + bias[b][hk]
            sink = sink_cols[hk]
            m = lax.stop_gradient(jnp.maximum(jnp.max(sc, axis=-1, keepdims=True), sink))
            p = jnp.exp(sc - m)
            probs = p / (jnp.sum(p, axis=-1, keepdims=True) + jnp.exp(sink - m))
            og = mm(probs, vb)
            outs += [og[g * lq:(g + 1) * lq] for g in range(group)]
        blocks.append(jnp.concatenate(outs, axis=1))
    return jnp.concatenate(blocks, axis=0)


def _attn_tile_bwd(q, kp, kc, vp, vc, bias, sinks, do):
    lq = ATTN_BLOCK
    group = N_Q_HEADS // N_KV_HEADS
    scale = HEAD_DIM ** -0.5
    lanes = lax.broadcasted_iota(jnp.int32, (1, 128), 1)
    rid = lax.broadcasted_iota(jnp.int32, (group * lq, 1), 0)
    sink_cols = []
    for hk in range(N_KV_HEADS):
        sink = jnp.zeros((group * lq, 1), F32)
        for g in range(group):
            s_h = jnp.sum(jnp.where(lanes == hk * group + g, sinks, 0.0), axis=-1, keepdims=True)
            sink = jnp.where((rid >= g * lq) & (rid < (g + 1) * lq), s_h, sink)
        sink_cols.append(sink)
    kall = jnp.concatenate([kp, kc], axis=0)
    vall = jnp.concatenate([vp, vc], axis=0)
    dsk = jnp.zeros((1, 128), F32)
    dq_blocks, dbias = [], []
    nqb = q.shape[0] // lq
    dk_parts = [[None] * nqb for _ in range(N_KV_HEADS)]
    dv_parts = [[None] * nqb for _ in range(N_KV_HEADS)]
    for b in range(nqb):
        qb, dob = q[b * lq:(b + 1) * lq], do[b * lq:(b + 1) * lq]
        dq_heads, dbias_b = [], []
        for hk in range(N_KV_HEADS):
            cols = slice(hk * HEAD_DIM, (hk + 1) * HEAD_DIM)
            kb = kall[b * lq:(b + 2) * lq, cols]
            vb = vall[b * lq:(b + 2) * lq, cols]
            heads = [hk * group + g for g in range(group)]
            qg = jnp.concatenate([qb[:, h * HEAD_DIM:(h + 1) * HEAD_DIM] for h in heads], axis=0)
            dog = jnp.concatenate([dob[:, h * HEAD_DIM:(h + 1) * HEAD_DIM] for h in heads], axis=0)
            sink = sink_cols[hk]
            sc = _nt(qg, kb) * scale + bias[b][hk]
            m = jnp.maximum(jnp.max(sc, axis=-1, keepdims=True), sink)
            p = jnp.exp(sc - m)
            es = jnp.exp(sink - m)
            inv = 1.0 / (jnp.sum(p, axis=-1, keepdims=True) + es)
            probs = p * inv
            dprobs = _nt(dog, vb)
            delta = jnp.sum(probs * dprobs, axis=-1, keepdims=True)
            dsc = probs * (dprobs - delta)
            dbias_b.append(dsc)
            dsink = -(es * inv) * delta
            for g, h in enumerate(heads):
                tot = jnp.sum(dsink[g * lq:(g + 1) * lq], axis=0, keepdims=True)
                dsk = dsk + jnp.where(lanes == h, tot, 0.0)
            dqg = _nn(dsc, kb) * scale
            dq_heads += [dqg[g * lq:(g + 1) * lq] for g in range(group)]
            dk_parts[hk][b] = _tn(dsc, qg) * scale
            dv_parts[hk][b] = _tn(probs, dog)
        dq_blocks.append(jnp.concatenate(dq_heads, axis=1))
        dbias.append(dbias_b)

    def overlap_add(parts):
        chunks = []
        for r in range(nqb + 1):
            acc = None
            if r < nqb:
                acc = parts[r][:lq]
            if r >= 1:
                tail = parts[r - 1][lq:]
                acc = tail if acc is None else acc + tail
            chunks.append(acc)
        return jnp.concatenate(chunks, axis=0)

    dkall = jnp.concatenate([overlap_add(dk_parts[hk]) for hk in range(N_KV_HEADS)], axis=1)
    dvall = jnp.concatenate([overlap_add(dv_parts[hk]) for hk in range(N_KV_HEADS)], axis=1)
    return jnp.concatenate(dq_blocks, axis=0), dkall, dvall, dbias, dsk


def _attn_in_specs(nt, clamp, nqb):
    lq, tq = ATTN_BLOCK, ATTN_BLOCK * nqb
    cur = lambda n: jnp.minimum(n, nt - 1) if clamp else n
    prev = lambda n: jnp.maximum(cur(n) * nqb - 1, 0)
    kcol, vcol = ATTN_WIDTH // KV_WIDTH, ATTN_WIDTH // KV_WIDTH + 1
    return [pl.BlockSpec((tq, ATTN_WIDTH), lambda n: (cur(n), 0)),
            pl.BlockSpec((lq, KV_WIDTH), lambda n: (prev(n), kcol)),
            pl.BlockSpec((tq, KV_WIDTH), lambda n: (cur(n), kcol)),
            pl.BlockSpec((lq, KV_WIDTH), lambda n: (prev(n), vcol)),
            pl.BlockSpec((tq, KV_WIDTH), lambda n: (cur(n), vcol)),
            pl.BlockSpec((2, N_KV_HEADS, 4 * lq, 2 * lq), lambda n: (0, 0, 0, 0)),
            _vec(128)]


def _tile_bias(bias_ref, first, nqb):
    return [[jnp.where(first, bias_ref[1, hk], bias_ref[0, hk]) if b == 0 else bias_ref[0, hk]
             for hk in range(N_KV_HEADS)] for b in range(nqb)]


def attn_fwd(qkv, bias, sinks_rows, exchange=None):
    s = qkv.shape[0]
    nqb = min(ATTN_QB_FWD, s // ATTN_BLOCK)
    tq = ATTN_BLOCK * nqb
    nt = s // tq

    def body(q_ref, kp_ref, kc_ref, vp_ref, vc_ref, bias_ref, sk_ref, o_ref):
        f = lambda r: r[...].astype(F32)
        o = _attn_tile(f(q_ref), f(kp_ref), f(kc_ref), f(vp_ref), f(vc_ref),
                       _tile_bias(bias_ref, pl.program_id(0) == 0, nqb), sk_ref[...])
        o_ref[...] = _b(o)

    return _grid_call(
        body, nt, (qkv, qkv, qkv, qkv, qkv, bias, sinks_rows), exchange, name="attn_fwd",
        in_specs=_attn_in_specs(nt, False, nqb),
        out_specs=[pl.BlockSpec((tq, ATTN_WIDTH), _row)],
        out_shape=[jax.ShapeDtypeStruct((s, ATTN_WIDTH), BF16)],
        compiler_params=_params())


def attn_bwd(qkv, bias, sinks_rows, d_attn, exchange=None):
    s = qkv.shape[0]
    nqb = ATTN_QB_BWD
    lq, tq = ATTN_BLOCK, ATTN_BLOCK * nqb
    nt = s // tq

    def body(q_ref, kp_ref, kc_ref, vp_ref, vc_ref, bias_ref, sk_ref, do_ref,
             dq_ref, dk_ref, dv_ref, dbias_ref, dsk_ref, carry_k, carry_v):
        n = pl.program_id(0)

        @pl.when(n == 0)
        def _():
            dbias_ref[...] = jnp.zeros_like(dbias_ref)
            dsk_ref[...] = jnp.zeros_like(dsk_ref)
            carry_k[...] = jnp.zeros_like(carry_k)
            carry_v[...] = jnp.zeros_like(carry_v)

        @pl.when(n < nt)
        def _():
            f = lambda r: r[...].astype(F32)
            dq, dkall, dvall, dbias, dsk = _attn_tile_bwd(
                f(q_ref), f(kp_ref), f(kc_ref), f(vp_ref), f(vc_ref), _tile_bias(bias_ref, n == 0, nqb), sk_ref[...],
                f(do_ref))
            dkp, dkc, dvp, dvc = dkall[:lq], dkall[lq:], dvall[:lq], dvall[lq:]
            dq_ref[...] = _b(dq)
            done = tq - lq
            dk_ref[:done, :] = _b(carry_k[:done, :])
            dv_ref[:done, :] = _b(carry_v[:done, :])
            dk_ref[done:, :] = _b(carry_k[done:, :] + dkp)
            dv_ref[done:, :] = _b(carry_v[done:, :] + dvp)
            carry_k[...] = dkc
            carry_v[...] = dvc
            dsk_ref[...] += dsk
            first = (n == 0).astype(F32)
            for hk in range(N_KV_HEADS):
                total = dbias[0][hk]
                for b in range(1, nqb):
                    total = total + dbias[b][hk]
                dbias_ref[0, hk] += total - first * dbias[0][hk]
                dbias_ref[1, hk] += first * dbias[0][hk]

        @pl.when(n == nt)
        def _():
            dk_ref[...] = _b(carry_k[...])
            dv_ref[...] = _b(carry_v[...])

    cur = lambda n: (jnp.minimum(n, nt - 1), 0)
    done_map = lambda n: (jnp.maximum(n - 1, 0), 0)
    return _grid_call(
        body, nt + 1, (qkv, qkv, qkv, qkv, qkv, bias, sinks_rows, d_attn), exchange, name="attn_bwd",
        in_specs=_attn_in_specs(nt, True, nqb) + [pl.BlockSpec((tq, ATTN_WIDTH), cur)],
        out_specs=[pl.BlockSpec((tq, ATTN_WIDTH), cur), pl.BlockSpec((tq, KV_WIDTH), done_map),
                   pl.BlockSpec((tq, KV_WIDTH), done_map),
                   pl.BlockSpec((2, N_KV_HEADS, 4 * lq, 2 * lq), lambda n: (0, 0, 0, 0)), _vec(128)],
        out_shape=[jax.ShapeDtypeStruct((s, ATTN_WIDTH), BF16), jax.ShapeDtypeStruct((s, KV_WIDTH), BF16),
                   jax.ShapeDtypeStruct((s, KV_WIDTH), BF16),
                   jax.ShapeDtypeStruct((2, N_KV_HEADS, 4 * lq, 2 * lq), F32), jax.ShapeDtypeStruct((1, 128), F32)],
        scratch_shapes=[pltpu.VMEM((tq, KV_WIDTH), F32), pltpu.VMEM((tq, KV_WIDTH), F32)],
        compiler_params=_params())


def rel_bias_table(rel_bias, bucket):
    lq = ATTN_BLOCK
    group = N_Q_HEADS // N_KV_HEADS

    def body(rb_ref, bk_ref, o_ref):
        bk = bk_ref[...]
        prev_keys = lax.broadcasted_iota(jnp.int32, bk.shape, 1) < lq
        accs = [jnp.full(bk.shape, -1e30, F32) for _ in range(N_Q_HEADS)]
        for b in range(N_BUCKETS):
            hit = bk == b
            accs = [jnp.where(hit, rb_ref[b, h], acc) for h, acc in enumerate(accs)]
        for h in range(N_Q_HEADS):
            rows = slice((h % group) * lq, (h % group + 1) * lq)
            o_ref[0, h // group, rows, :] = accs[h]
            o_ref[1, h // group, rows, :] = jnp.where(prev_keys, -1e30, accs[h])

    return pl.pallas_call(
        body, name="rel_bias_table",
        in_specs=[pl.BlockSpec(memory_space=pltpu.SMEM), pl.BlockSpec(memory_space=pltpu.VMEM)],
        out_shape=jax.ShapeDtypeStruct((2, N_KV_HEADS, group * lq, 2 * lq), F32),
        compiler_params=_params(n_axes=0))(rel_bias, bucket)


def rel_bias_grad(dbias, bucket):
    lq = ATTN_BLOCK
    group = N_Q_HEADS // N_KV_HEADS

    def body(db_ref, bk_ref, o_ref):
        rows = lax.broadcasted_iota(jnp.int32, (N_BUCKETS, 128), 0)
        lanes = lax.broadcasted_iota(jnp.int32, (N_BUCKETS, 128), 1)
        bk = bk_ref[...]
        per_head = []
        for h in range(N_Q_HEADS):
            sl = slice((h % group) * lq, (h % group + 1) * lq)
            per_head.append(db_ref[0, h // group, sl, :] + db_ref[1, h // group, sl, :])

        def per_bucket(b, acc):
            hit = (bk == b).astype(F32)
            for h in range(N_Q_HEADS):
                val = jnp.sum(per_head[h] * hit, keepdims=True)
                acc = acc + jnp.where((rows == b) & (lanes == h), val, 0.0)
            return acc

        o_ref[...] = lax.fori_loop(0, N_BUCKETS, per_bucket, jnp.zeros((N_BUCKETS, 128), F32))

    return pl.pallas_call(body, name="rel_bias_grad", out_shape=jax.ShapeDtypeStruct((N_BUCKETS, 128), F32),
                          compiler_params=_params(n_axes=0))(dbias, bucket)


def _tri_sum(a, upper):
    n = a.shape[0]
    ri = lax.broadcasted_iota(jnp.int32, (n, n), 0)
    ci = lax.broadcasted_iota(jnp.int32, (n, n), 1)
    tri = ((ri <= ci) if upper else (ri >= ci)).astype(BF16)
    hi = a.astype(BF16)
    rest = a - hi.astype(F32)
    mid = rest.astype(BF16)
    lo = (rest - mid.astype(F32)).astype(BF16)
    dot = lambda part: jnp.dot(tri, part, preferred_element_type=F32)
    return dot(hi) + dot(mid) + dot(lo)


@jax.custom_vjp
def _cumsum_rows(a):
    return _tri_sum(a, False)


_cumsum_rows.defvjp(lambda a: (_tri_sum(a, False), None), lambda _, g: (_tri_sum(g, True),))


def _ssm_core(u, z, dt_raw, hprev, dt_bias, a_log, dskip, norm_w):
    lc = u.shape[0]
    xbc = _silu(u)
    xs, bm, cm = xbc[:, :SSM_WIDTH], xbc[:, SSM_WIDTH:SSM_WIDTH + SSM_BC], xbc[:, SSM_WIDTH + SSM_BC:]
    dt = jax.nn.softplus(dt_raw + dt_bias)
    adt = dt * (-jnp.exp(a_log))
    ri = lax.broadcasted_iota(jnp.int32, (lc, lc), 0)
    ci = lax.broadcasted_iota(jnp.int32, (lc, lc), 1)
    causal = ri >= ci
    acum = _cumsum_rows(adt)
    acum_t = acum.T
    last = acum[lc - 1:lc, :]
    per_group = SSM_HEADS // SSM_GROUPS
    lane = lax.broadcasted_iota(jnp.int32, (1, 128), 1)
    rowid = lax.broadcasted_iota(jnp.int32, (128, 1), 0)
    lo_lanes = lane < SSM_HEAD_DIM
    ys, hs = [], []
    for g in range(SSM_GROUPS):
        bg = bm[:, g * SSM_STATE:(g + 1) * SSM_STATE]
        cg = cm[:, g * SSM_STATE:(g + 1) * SSM_STATE]
        cb = mm_nt(cg, bg)
        for pp in range(per_group // 2):
            ha = g * per_group + 2 * pp
            xp = xs[:, ha * SSM_HEAD_DIM:(ha + 2) * SSM_HEAD_DIM]
            hp = hprev[ha * SSM_HEAD_DIM:(ha + 2) * SSM_HEAD_DIM, :]
            xcp = xp * jnp.where(lo_lanes, dt[:, ha:ha + 1], dt[:, ha + 1:ha + 2])
            y_h, st_h = [], []
            for h in (ha, ha + 1):
                col, rowv, lasth = acum[:, h:h + 1], acum_t[h:h + 1, :], last[:, h:h + 1]
                decay = jnp.exp(jnp.where(causal, col - rowv, -1e30))
                y_h.append(mm(cb * decay, xcp) + mm_nt(cg * jnp.exp(col), hp))
                st_h.append(mm_tn(xcp, bg * jnp.exp(lasth - col)))
            y_pair = jnp.where(lo_lanes, y_h[0], y_h[1])
            st_pair = jnp.where(rowid < SSM_HEAD_DIM, st_h[0], st_h[1])
            la, lb = last[:, ha:ha + 1], last[:, ha + 1:ha + 2]
            hs.append(jnp.exp(jnp.where(rowid < SSM_HEAD_DIM, la, lb)) * hp + st_pair)
            dsk = jnp.where(lo_lanes, dskip[:, ha:ha + 1], dskip[:, ha + 1:ha + 2])
            ys.append(y_pair + dsk * xp)
    y = jnp.concatenate(ys, axis=1) * _silu(z)
    gw = SSM_WIDTH // SSM_GROUPS
    outs = []
    for g in range(SSM_GROUPS):
        yg = y[:, g * gw:(g + 1) * gw]
        outs.append(yg * lax.rsqrt(jnp.mean(yg * yg, axis=-1, keepdims=True) + NORM_EPS))
    return jnp.concatenate(outs, axis=1) * norm_w, jnp.concatenate(hs, axis=0)


def _ssm_param_specs():
    return [pl.BlockSpec((SSM_CONV, XBC_WIDTH), _const), _vec(XBC_WIDTH), _vec(128), _vec(128), _vec(128),
            _vec(SSM_WIDTH)]


SSM_FWD_SUB = 2
SSM_BWD_SUB = 1


def ssm_fwd(xbc_raw, z, dt_raw, conv_w, conv_b, dt_bias, a_log, dskip, norm_w, exchange=None):
    s = xbc_raw.shape[0]
    lc = SSM_CHUNK
    lt = lc * SSM_FWD_SUB
    hrows = SSM_HEADS * SSM_HEAD_DIM

    def body(x_ref, halo_ref, z_ref, dt_ref, cw_ref, cb_ref, dtb_ref, al_ref, dk_ref, nw_ref,
             o_ref, hp_ref, state):
        i = pl.program_id(0)

        @pl.when(i == 0)
        def _():
            state[...] = jnp.zeros_like(state)

        halo = halo_ref[...] * (i > 0).astype(F32)
        xin = jnp.concatenate([halo, x_ref[...]], axis=0)
        u = (_conv_rows(xin, cw_ref[...], SSM_CONV) + cb_ref[...])[HALO:]
        h = state[...]
        for k in range(SSM_FWD_SUB):
            rows = slice(k * lc, (k + 1) * lc)
            hp_ref[k * hrows:(k + 1) * hrows, :] = h
            out, h = _ssm_core(u[rows], z_ref[rows, :], dt_ref[rows, :], h, dtb_ref[...], al_ref[...], dk_ref[...],
                               nw_ref[...])
            o_ref[rows, :] = _b(out)
        state[...] = h

    tile = lambda n: pl.BlockSpec((lt, n), _row)
    halo_spec = pl.BlockSpec((HALO, XBC_WIDTH), lambda i: (jnp.maximum(i * (lt // HALO) - 1, 0), 0))
    return _grid_call(
        body, s // lt, (xbc_raw, xbc_raw, z, dt_raw, conv_w, conv_b, dt_bias, a_log, dskip, norm_w), exchange,
        name="ssm_fwd",
        in_specs=[tile(XBC_WIDTH), halo_spec, tile(SSM_WIDTH), tile(DT_PAD)] + _ssm_param_specs(),
        out_specs=[tile(SSM_WIDTH), pl.BlockSpec((SSM_FWD_SUB * hrows, SSM_STATE), _row)],
        out_shape=[jax.ShapeDtypeStruct((s, SSM_WIDTH), BF16),
                   jax.ShapeDtypeStruct((s // lc * hrows, SSM_STATE), F32)],
        scratch_shapes=[pltpu.VMEM((hrows, SSM_STATE), F32)],
        compiler_params=_params())


def ssm_bwd(xbc_raw, z, dt_raw, hprev_all, d_out, conv_w, conv_b, dt_bias, a_log, dskip, norm_w, exchange=None):
    s = xbc_raw.shape[0]
    lc = SSM_CHUNK
    sub = SSM_BWD_SUB
    lt = lc * sub
    nt = s // lt
    hrows = SSM_HEADS * SSM_HEAD_DIM

    def body(x_ref, halo_ref, z_ref, dt_ref, hp_ref, do_ref, cw_ref, cb_ref, dtb_ref, al_ref, dk_ref, nw_ref,
             dx_ref, dz_ref, ddt_ref, dcw_ref, dcb_ref, ddtb_ref, dal_ref, ddk_ref, dnw_ref, dstate, du_next):
        i = pl.program_id(0)
        tile_no = nt - 1 - i

        @pl.when(i == 0)
        def _():
            dstate[...] = jnp.zeros_like(dstate)
            du_next[...] = jnp.zeros_like(du_next)
            for r in (dcw_ref, dcb_ref, ddtb_ref, dal_ref, ddk_ref, dnw_ref):
                r[...] = jnp.zeros_like(r)

        halo = halo_ref[...] * (tile_no > 0).astype(F32)
        xin = jnp.concatenate([halo, x_ref[...]], axis=0)
        cw = cw_ref[...]
        u = (_conv_rows(xin, cw, SSM_CONV) + cb_ref[...])[HALO:]
        dh = dstate[...]
        dus = [None] * sub
        for k in reversed(range(sub)):
            rows = slice(k * lc, (k + 1) * lc)
            _, vjp = jax.vjp(_ssm_core, u[rows], z_ref[rows, :], dt_ref[rows, :], hp_ref[k * hrows:(k + 1) * hrows, :],
                             dtb_ref[...], al_ref[...], dk_ref[...], nw_ref[...])
            dus[k], dz, ddt, dh, ddtb, dal, ddk, dnw = vjp((do_ref[rows, :], dh))
            dz_ref[rows, :] = _b(dz)
            ddt_ref[rows, :] = _b(ddt)
            ddtb_ref[...] += ddtb
            dal_ref[...] += dal
            ddk_ref[...] += ddk
            dnw_ref[...] += dnw
        dstate[...] = dh
        du = jnp.concatenate(dus, axis=0)
        du_ext = jnp.concatenate([du, du_next[...]], axis=0)
        dx_ref[...] = _b(_conv_rows_t(du_ext, cw, SSM_CONV)[:lt])
        du_next[...] = du[:HALO]
        sums = [jnp.sum(du * pltpu.roll(xin, j, axis=0)[HALO:] if j else du * xin[HALO:], axis=0, keepdims=True)
                for j in range(SSM_CONV)]
        dcw_ref[...] += jnp.concatenate(sums[::-1] + [jnp.zeros((8 - SSM_CONV, XBC_WIDTH), F32)], axis=0)
        dcb_ref[...] += jnp.sum(du, axis=0, keepdims=True)

    rev = lambda i: (nt - 1 - i, 0)
    tile = lambda n: pl.BlockSpec((lt, n), rev)
    halo_spec = pl.BlockSpec((HALO, XBC_WIDTH), lambda i: (jnp.maximum((nt - 1 - i) * (lt // HALO) - 1, 0), 0))
    acc = lambda r, n: pl.BlockSpec((r, n), _const)
    return _grid_call(
        body, nt, (xbc_raw, xbc_raw, z, dt_raw, hprev_all, d_out, conv_w, conv_b, dt_bias, a_log, dskip, norm_w),
        exchange, name="ssm_bwd",
        in_specs=[tile(XBC_WIDTH), halo_spec, tile(SSM_WIDTH), tile(DT_PAD),
                  pl.BlockSpec((sub * hrows, SSM_STATE), rev), tile(SSM_WIDTH)] + _ssm_param_specs(),
        out_specs=[tile(XBC_WIDTH), tile(SSM_WIDTH), tile(DT_PAD), acc(8, XBC_WIDTH), acc(1, XBC_WIDTH),
                   acc(1, 128), acc(1, 128), acc(1, 128), acc(1, SSM_WIDTH)],
        out_shape=[jax.ShapeDtypeStruct((s, XBC_WIDTH), BF16), jax.ShapeDtypeStruct((s, SSM_WIDTH), BF16),
                   jax.ShapeDtypeStruct((s, DT_PAD), BF16), jax.ShapeDtypeStruct((8, XBC_WIDTH), F32),
                   jax.ShapeDtypeStruct((1, XBC_WIDTH), F32), jax.ShapeDtypeStruct((1, 128), F32),
                   jax.ShapeDtypeStruct((1, 128), F32), jax.ShapeDtypeStruct((1, 128), F32),
                   jax.ShapeDtypeStruct((1, SSM_WIDTH), F32)],
        scratch_shapes=[pltpu.VMEM((hrows, SSM_STATE), F32), pltpu.VMEM((HALO, XBC_WIDTH), F32)],
        compiler_params=_params())


def mix_out(attn, ssm, x, w_out16, gate1, post_mix_w, pre_ffn_w, scale2, shift2, tm=4 * TOKEN_TILE):
    s = x.shape[0]
    tm = min(tm, s)

    def body(a_ref, s_ref, x_ref, w_ref, g_ref, pw_ref, fw_ref, sc_ref, sh_ref, mixed_ref, x1_ref, h2_ref):
        mixed = (jnp.dot(a_ref[...], w_ref[:ATTN_WIDTH, :], preferred_element_type=F32)
                 + jnp.dot(s_ref[...], w_ref[ATTN_WIDTH:, :], preferred_element_type=F32))
        mixed_ref[...] = _b(mixed)
        x1 = x_ref[...] + g_ref[...] * _rms(mixed, pw_ref[...])
        x1_ref[...] = x1
        h2_ref[...] = _b(_norm_mod(x1, fw_ref[...], sc_ref[...], sh_ref[...]))

    tile = lambda n: pl.BlockSpec((tm, n), _row)
    return pl.pallas_call(
        body, name="mix_out", grid=(s // tm,),
        in_specs=[tile(ATTN_WIDTH), tile(SSM_WIDTH), tile(D_MODEL), _resident((D_MODEL, D_MODEL))]
        + [_vec(D_MODEL)] * 5,
        out_specs=[tile(D_MODEL)] * 3,
        out_shape=[jax.ShapeDtypeStruct((s, D_MODEL), BF16), jax.ShapeDtypeStruct((s, D_MODEL), F32),
                   jax.ShapeDtypeStruct((s, D_MODEL), BF16)],
        compiler_params=_params())(attn, ssm, x, w_out16, gate1, post_mix_w, pre_ffn_w, scale2, shift2)


GELU_K0, GELU_K1 = math.sqrt(2.0 / math.pi), 0.044715


def _gate(ug, uv):
    return jax.nn.gelu(ug, approximate=True) * uv


def _gate_bwd(ug, uv, df):
    sq = ug * ug
    t = jnp.tanh(ug * (GELU_K0 + (GELU_K0 * GELU_K1) * sq))
    half = 0.5 + 0.5 * t
    slope = half + ug * (1.0 - t * t) * (0.5 * GELU_K0 + (1.5 * GELU_K0 * GELU_K1) * sq)
    return df * uv * slope, df * (ug * half)


def up_gate(h2, w_up16, conv_w, conv_b, tm=TOKEN_TILE):
    s = h2.shape[0]

    def body(h_ref, halo_ref, w_ref, cw_ref, cb_ref, u_ref, uraw_ref, f_ref):
        halo = halo_ref[...]
        halo = jnp.where(pl.program_id(0) > 0, halo, jnp.zeros_like(halo))
        hin = jnp.concatenate([halo, h_ref[...]], axis=0)
        for lo in range(0, D_FF, FF_CHUNK):
            halves = []
            for base in (lo, D_FF + lo):
                cols = slice(base, base + FF_CHUNK)
                uraw = jnp.dot(hin, w_ref[:, cols], preferred_element_type=F32)
                uraw_ref[:, cols] = _b(uraw[NEXT:])
                u = (_conv_rows(uraw, cw_ref[:, cols], FFN_CONV) + cb_ref[:, cols])[NEXT:]
                u_ref[:, cols] = u
                halves.append(u)
            f_ref[:, lo:lo + FF_CHUNK] = _b(_gate(*halves))

    tile = lambda n: pl.BlockSpec((tm, n), _row)
    halo_spec = pl.BlockSpec((NEXT, D_MODEL), lambda i: (jnp.maximum(i * (tm // NEXT) - 1, 0), 0))
    return pl.pallas_call(
        body, name="up_gate", grid=(s // tm,),
        in_specs=[tile(D_MODEL), halo_spec, _resident((D_MODEL, 2 * D_FF)),
                  pl.BlockSpec((FFN_CONV, 2 * D_FF), _const), _vec(2 * D_FF)],
        out_specs=[tile(2 * D_FF), tile(2 * D_FF), tile(D_FF)],
        out_shape=[jax.ShapeDtypeStruct((s, 2 * D_FF), F32), jax.ShapeDtypeStruct((s, 2 * D_FF), BF16),
                   jax.ShapeDtypeStruct((s, D_FF), BF16)],
        compiler_params=_params())(h2, h2, w_up16, conv_w, conv_b)


DOWN_LOSS_TILE = 512


def down_loss(f16, w_down16, x1, target, gate2, post_ffn_w, tm=DOWN_LOSS_TILE):
    s = x1.shape[0]
    tm = min(tm, s)

    def body(f_ref, wd_ref, x1_ref, t_ref, g_ref, pw_ref, dffn_ref, dy_ref, loss_ref, dg_ref, dpw_ref, gw_ref):
        i = pl.program_id(0)

        @pl.when(i == 0)
        def _():
            loss_ref[...] = jnp.zeros_like(loss_ref)
            dg_ref[...] = jnp.zeros_like(dg_ref)
            dpw_ref[...] = jnp.zeros_like(dpw_ref)
            gw_ref[...] = jnp.zeros_like(gw_ref)

        ffn = jnp.dot(f_ref[...], wd_ref[...], preferred_element_type=F32)
        x1 = x1_ref[...]
        x2 = x1 + g_ref[...] * _rms(ffn, pw_ref[...])
        err = x2 - t_ref[...]
        dy = err * (1.0 / D_MODEL)
        dy_ref[...] = dy
        loss_ref[...] += 0.5 * jnp.sum(jnp.mean(err * err, axis=-1, keepdims=True))
        dffn, dg, dpw = _gated_rms_bwd(ffn, g_ref[...], pw_ref[...], dy)
        dffn16 = _b(dffn)
        dffn_ref[...] = dffn16
        dg_ref[...] += dg
        dpw_ref[...] += dpw
        gw_ref[...] += _tn(f_ref[...], dffn16)

    tile = lambda n: pl.BlockSpec((tm, n), _row)
    return pl.pallas_call(
        body, name="down_loss", grid=(s // tm,),
        in_specs=[tile(D_FF), _resident((D_FF, D_MODEL)), tile(D_MODEL), tile(D_MODEL), _vec(D_MODEL), _vec(D_MODEL)],
        out_specs=[tile(D_MODEL), tile(D_MODEL), _vec(128), _vec(D_MODEL), _vec(D_MODEL),
                   pl.BlockSpec((D_FF, D_MODEL), _const)],
        out_shape=[jax.ShapeDtypeStruct((s, D_MODEL), BF16), jax.ShapeDtypeStruct((s, D_MODEL), F32),
                   jax.ShapeDtypeStruct((1, 128), F32), jax.ShapeDtypeStruct((1, D_MODEL), F32),
                   jax.ShapeDtypeStruct((1, D_MODEL), F32), jax.ShapeDtypeStruct((D_FF, D_MODEL), F32)],
        compiler_params=_params())(f16, w_down16, x1, target, gate2, post_ffn_w)


BWD_CHUNK = 256


def ffn_bwd(u, u_raw16, d_ffn, conv_w, w_down_t16, w_up_t16, tm=TOKEN_TILE):
    s = u.shape[0]
    nt = s // tm

    def body(u_ref, unext_ref, uraw_ref, d_ref, dnext_ref, cw_ref, wdt_ref, wut_ref,
             du_ref, dh_ref, dcw_ref, dcb_ref):
        i = pl.program_id(0)

        @pl.when(i == 0)
        def _():
            dcw_ref[...] = jnp.zeros_like(dcw_ref)
            dcb_ref[...] = jnp.zeros_like(dcb_ref)

        dnext = dnext_ref[...]
        dnext = jnp.where(i < nt - 1, dnext, jnp.zeros_like(dnext))
        dff = jnp.concatenate([d_ref[...], dnext], axis=0)
        rows_ext = tm + NEXT
        for lo in range(0, D_FF, BWD_CHUNK):
            gcols, vcols = slice(lo, lo + BWD_CHUNK), slice(D_FF + lo, D_FF + lo + BWD_CHUNK)
            ug = jnp.concatenate([u_ref[:, gcols], unext_ref[:, gcols]], axis=0)
            uv = jnp.concatenate([u_ref[:, vcols], unext_ref[:, vcols]], axis=0)
            df = jnp.dot(dff, wdt_ref[:, gcols], preferred_element_type=F32)
            for cols, du in zip((gcols, vcols), _gate_bwd(ug, uv, df)):
                cw = cw_ref[:, cols]
                du1 = pltpu.roll(du, rows_ext - 1, axis=0)
                du2 = pltpu.roll(du, rows_ext - 2, axis=0)
                du_ref[:, cols] = _b((du * cw[2:3, :] + du1 * cw[1:2, :] + du2 * cw[0:1, :])[:tm])
                xr = uraw_ref[:, cols].astype(F32)
                rows = [jnp.sum(xr * d_[:tm], axis=0, keepdims=True) for d_ in (du2, du1, du)]
                dcw_ref[:, cols] += jnp.concatenate(rows + [jnp.zeros((8 - FFN_CONV, BWD_CHUNK), F32)], axis=0)
                dcb_ref[:, cols] += jnp.sum(du[:tm], axis=0, keepdims=True)
        dh_ref[...] = jnp.dot(du_ref[...], wut_ref[...], preferred_element_type=F32)

    tile = lambda n: pl.BlockSpec((tm, n), _row)
    nxt = lambda i: (jnp.minimum((i + 1) * (tm // NEXT), s // NEXT - 1), 0)
    return pl.pallas_call(
        body, name="ffn_bwd", grid=(nt,),
        in_specs=[tile(2 * D_FF), pl.BlockSpec((NEXT, 2 * D_FF), nxt), tile(2 * D_FF), tile(D_MODEL),
                  pl.BlockSpec((NEXT, D_MODEL), nxt), pl.BlockSpec((FFN_CONV, 2 * D_FF), _const),
                  _resident((D_MODEL, D_FF)), _resident((2 * D_FF, D_MODEL))],
        out_specs=[tile(2 * D_FF), tile(D_MODEL), pl.BlockSpec((8, 2 * D_FF), _const), _vec(2 * D_FF)],
        out_shape=[jax.ShapeDtypeStruct((s, 2 * D_FF), BF16), jax.ShapeDtypeStruct((s, D_MODEL), F32),
                   jax.ShapeDtypeStruct((8, 2 * D_FF), F32), jax.ShapeDtypeStruct((1, 2 * D_FF), F32)],
        compiler_params=_params())(u, u, u_raw16, d_ffn, d_ffn, conv_w, w_down_t16, w_up_t16)


def mix_bwd(dh2, x1, dy, mixed, attn, ssm, w_out_t16, pre_ffn_w, scale2, gate1, post_mix_w, tm=2 * TOKEN_TILE):
    s = x1.shape[0]

    def body(dh_ref, x1_ref, dy_ref, mx_ref, a_ref, s_ref, w_ref, fw_ref, sc_ref, g_ref, pw_ref,
             dx1_ref, da_ref, ds_ref, dfw_ref, dsc_ref, dsh_ref, dg_ref, dpw_ref, gw_ref):
        accs = (dfw_ref, dsc_ref, dsh_ref, dg_ref, dpw_ref)

        @pl.when(pl.program_id(0) == 0)
        def _():
            for r in accs + (gw_ref,):
                r[...] = jnp.zeros_like(r)

        dx1, dfw, dsc, dsh = _norm_mod_bwd(x1_ref[...], fw_ref[...], sc_ref[...], dh_ref[...])
        dx1 = dx1 + dy_ref[...]
        dx1_ref[...] = dx1
        dmixed, dg, dpw = _gated_rms_bwd(mx_ref[...].astype(F32), g_ref[...], pw_ref[...], dx1)
        dm16 = _b(dmixed)
        dmix_in = jnp.dot(dm16, w_ref[...], preferred_element_type=F32)
        da_ref[...] = _b(dmix_in[:, :ATTN_WIDTH])
        ds_ref[...] = dmix_in[:, ATTN_WIDTH:]
        gw_ref[:ATTN_WIDTH, :] += _tn(a_ref[...], dm16)
        gw_ref[ATTN_WIDTH:, :] += _tn(s_ref[...], dm16)
        for r, v in zip(accs, (dfw, dsc, dsh, dg, dpw)):
            r[...] += v

    tile = lambda n: pl.BlockSpec((tm, n), _row)
    return pl.pallas_call(
        body, name="mix_bwd", grid=(s // tm,),
        in_specs=[tile(D_MODEL)] * 4 + [tile(ATTN_WIDTH), tile(SSM_WIDTH), _resident((D_MODEL, D_MODEL))]
        + [_vec(D_MODEL)] * 4,
        out_specs=[tile(D_MODEL), tile(ATTN_WIDTH), tile(SSM_WIDTH)] + [_vec(D_MODEL)] * 5
        + [pl.BlockSpec((D_MODEL, D_MODEL), _const)],
        out_shape=[jax.ShapeDtypeStruct((s, D_MODEL), F32), jax.ShapeDtypeStruct((s, ATTN_WIDTH), BF16),
                   jax.ShapeDtypeStruct((s, SSM_WIDTH), F32)]
        + [jax.ShapeDtypeStruct((1, D_MODEL), F32)] * 5 + [jax.ShapeDtypeStruct((D_MODEL, D_MODEL), F32)],
        compiler_params=_params())(dh2, x1, dy, mixed, attn, ssm, w_out_t16, pre_ffn_w, scale2, gate1, post_mix_w)


INPROJ_BWD_TILE = 512


def inproj_bwd(dq, dk, dv, dxbc, dz, ddt, x, dx1, h1, w_in_t16, pre_mix_w, scale1, tm=INPROJ_BWD_TILE):
    s = x.shape[0]
    tm = min(tm, s)

    def body(dq_ref, dk_ref, dv_ref, dxbc_ref, dz_ref, ddt_ref, x_ref, dx1_ref, h_ref, w_ref, pw_ref, sc_ref,
             gx_ref, dpw_ref, dsc_ref, dsh_ref, gw_ref):
        accs = (dpw_ref, dsc_ref, dsh_ref)

        @pl.when(pl.program_id(0) == 0)
        def _():
            for r in accs + (gw_ref,):
                r[...] = jnp.zeros_like(r)

        h16 = h_ref[...]
        dh = None
        dqkv = jnp.concatenate([dq_ref[...], dk_ref[...], dv_ref[...]], axis=1)
        for d16, (lo, hi) in ((dqkv, (0, QKV_W)), (dxbc_ref[:, :SSM_WIDTH], COLS_XS), (dxbc_ref[:, SSM_WIDTH:], COLS_BC),
                              (dz_ref[...], COLS_Z), (ddt_ref[...], COLS_DT)):
            part = jnp.dot(d16, w_ref[lo:hi, :], preferred_element_type=F32)
            dh = part if dh is None else dh + part
            gw_ref[lo:hi, :] += _tn(d16, h16)
        dx, dpw, dsc, dsh = _norm_mod_bwd(x_ref[...], pw_ref[...], sc_ref[...], dh)
        gx_ref[...] = dx1_ref[...] + dx
        for r, v in zip(accs, (dpw, dsc, dsh)):
            r[...] += v

    tile = lambda n: pl.BlockSpec((tm, n), _row)
    return pl.pallas_call(
        body, name="inproj_bwd", grid=(s // tm,),
        in_specs=[tile(ATTN_WIDTH), tile(KV_WIDTH), tile(KV_WIDTH), tile(XBC_WIDTH), tile(SSM_WIDTH), tile(DT_PAD),
                  tile(D_MODEL), tile(D_MODEL), tile(D_MODEL), _resident((PROJ_PAD, D_MODEL))] + [_vec(D_MODEL)] * 2,
        out_specs=[tile(D_MODEL)] + [_vec(D_MODEL)] * 3 + [pl.BlockSpec((PROJ_PAD, D_MODEL), _const)],
        out_shape=[jax.ShapeDtypeStruct((s, D_MODEL), F32)] + [jax.ShapeDtypeStruct((1, D_MODEL), F32)] * 3
        + [jax.ShapeDtypeStruct((PROJ_PAD, D_MODEL), F32)],
        compiler_params=_params())(dq, dk, dv, dxbc, dz, ddt, x, dx1, h1, w_in_t16, pre_mix_w, scale1)


def _adam(g, w, m, v):
    new_m = ADAM_B1 * m + (1.0 - ADAM_B1) * g
    new_v = ADAM_B2 * v + (1.0 - ADAM_B2) * jnp.square(g)
    m_hat = new_m / (1.0 - ADAM_B1 ** ADAM_STEP)
    v_hat = new_v / (1.0 - ADAM_B2 ** ADAM_STEP)
    return -ADAM_LR * (m_hat / (jnp.sqrt(v_hat) + ADAM_EPS) + ADAM_WD * w), new_m, new_v


ROW_PARAMS = (("b_ada", 6144, 6144), ("pre_mix_w", 1024, 1024), ("attn_sinks", 128, 8), ("ssm_conv_b", 1024, 1024),
              ("ssm_dt_bias", 128, 8), ("ssm_a_log", 128, 8), ("ssm_d", 128, 8), ("ssm_norm_w", 512, 512),
              ("post_mix_w", 1024, 1024), ("pre_ffn_w", 1024, 1024), ("ffn_conv_b", 5632, 5632),
              ("post_ffn_w", 1024, 1024))
LOSS_LANES = 128


def adamw_small(row_all, rb_all, rel_bias_wmv, row_wmv):
    n_rows = len(ROW_PARAMS)

    def body(*refs):
        row_ref, rb_ref = refs[:2]
        wmv = refs[2:5 + 3 * n_rows]
        outs = refs[5 + 3 * n_rows:]
        g_row, g_rb = row_ref[0], rb_ref[0]
        for k in range(1, N_DEV):
            g_row = g_row + row_ref[k]
            g_rb = g_rb + rb_ref[k]
        outs[0][...] = g_row[:, :LOSS_LANES]
        grads = [g_rb[:, :N_Q_HEADS]]
        off = LOSS_LANES
        for _, lanes, width in ROW_PARAMS:
            grads.append(g_row[:, off:off + width])
            off += lanes
        for i, g in enumerate(grads):
            w_ref, m_ref, v_ref = wmv[3 * i:3 * i + 3]
            g_out, d_out, m_out, v_out = outs[1 + 4 * i:5 + 4 * i]
            g_out[...] = g
            d_out[...], m_out[...], v_out[...] = _adam(g, w_ref[...], m_ref[...], v_ref[...])

    flat_wmv = list(rel_bias_wmv) + [a for wmv in row_wmv for a in wmv]
    shapes = [jax.ShapeDtypeStruct((1, LOSS_LANES), F32)] + [jax.ShapeDtypeStruct((N_BUCKETS, N_Q_HEADS), F32)] * 4
    for _, _, width in ROW_PARAMS:
        shapes += [jax.ShapeDtypeStruct((1, width), F32)] * 4
    return pl.pallas_call(body, name="adamw_small", out_shape=shapes,
                          compiler_params=_params(n_axes=0))(row_all, rb_all, *flat_wmv)


def adamw(parts, w, m, v, name):
    p, r, n = parts.shape
    tr = _row_tile(r)

    def body(p_ref, w_ref, m_ref, v_ref, g_ref, d_ref, nm_ref, nv_ref):
        g = p_ref[0].astype(F32)
        for k in range(1, p):
            g = g + p_ref[k].astype(F32)
        g_ref[...] = g
        d_ref[...], nm_ref[...], nv_ref[...] = _adam(g, w_ref[...], m_ref[...], v_ref[...])

    tile = pl.BlockSpec((tr, n), _row)
    return pl.pallas_call(
        body, name=name, grid=(r // tr,),
        in_specs=[pl.BlockSpec((p, tr, n), lambda i: (0, i, 0)), tile, tile, tile],
        out_specs=[tile] * 4, out_shape=[jax.ShapeDtypeStruct((r, n), F32)] * 4,
        compiler_params=_params())(parts, w, m, v)


def _bucket_table():
    lq = ATTN_BLOCK
    qi = np.arange(lq)[:, None] + lq
    kj = np.arange(2 * lq)[None, :]
    dist = qi - kj
    d = np.maximum(dist, 0)
    max_exact = N_BUCKETS // 2
    nf = np.maximum(d, 1).astype(np.float32)
    large = max_exact + (np.log(nf / max_exact) / math.log(REL_MAX_DIST / max_exact)
                         * (N_BUCKETS - max_exact)).astype(np.int32)
    large = np.minimum(large, N_BUCKETS - 1)
    bucket = np.where(d < max_exact, d, large).astype(np.int32)
    in_band = (dist >= 0) & (dist < REL_MAX_DIST)
    return np.where(in_band, bucket, -1).astype(np.int32)


def _cols_from_blocks(g):
    return jnp.transpose(g, (1, 0, 2)).reshape(g.shape[1], N_DEV * g.shape[2])


def _cols_to_blocks(a):
    r, n = a.shape
    return jnp.transpose(a.reshape(r, N_DEV, n // N_DEV), (1, 0, 2))


def _pad_in_rows(wt):
    return jnp.concatenate([wt, jnp.zeros((PROJ_PAD - wt.shape[0], wt.shape[1]), wt.dtype)], axis=0)


def _lane_pad(v, n=128):
    return jnp.pad(v, ((0, 0), (0, n - v.shape[1])))


def kernel(x, c, rel_bias, w_ada, b_ada, pre_mix_w, w_in, attn_sinks, ssm_conv_w, ssm_conv_b, ssm_dt_bias, ssm_a_log, ssm_d, ssm_norm_w, w_out, post_mix_w, pre_ffn_w, w_up, ffn_conv_w, ffn_conv_b, w_down, post_ffn_w, loss_target, m_rel_bias, m_w_ada, m_b_ada, m_pre_mix_w, m_w_in, m_attn_sinks, m_ssm_conv_w, m_ssm_conv_b, m_ssm_dt_bias, m_ssm_a_log, m_ssm_d, m_ssm_norm_w, m_w_out, m_post_mix_w, m_pre_ffn_w, m_w_up, m_ffn_conv_w, m_ffn_conv_b, m_w_down, m_post_ffn_w, v_rel_bias, v_w_ada, v_b_ada, v_pre_mix_w, v_w_in, v_attn_sinks, v_ssm_conv_w, v_ssm_conv_b, v_ssm_dt_bias, v_ssm_a_log, v_ssm_d, v_ssm_norm_w, v_w_out, v_post_mix_w, v_pre_ffn_w, v_w_up, v_ffn_conv_w, v_ffn_conv_b, v_w_down, v_post_ffn_w):
    weights = dict(rel_bias=rel_bias, w_ada=w_ada, b_ada=b_ada, pre_mix_w=pre_mix_w, w_in=w_in, attn_sinks=attn_sinks, ssm_conv_w=ssm_conv_w, ssm_conv_b=ssm_conv_b, ssm_dt_bias=ssm_dt_bias, ssm_a_log=ssm_a_log, ssm_d=ssm_d, ssm_norm_w=ssm_norm_w, w_out=w_out, post_mix_w=post_mix_w, pre_ffn_w=pre_ffn_w, w_up=w_up, ffn_conv_w=ffn_conv_w, ffn_conv_b=ffn_conv_b, w_down=w_down, post_ffn_w=post_ffn_w)
    mom_m = dict(rel_bias=m_rel_bias, w_ada=m_w_ada, b_ada=m_b_ada, pre_mix_w=m_pre_mix_w, w_in=m_w_in, attn_sinks=m_attn_sinks, ssm_conv_w=m_ssm_conv_w, ssm_conv_b=m_ssm_conv_b, ssm_dt_bias=m_ssm_dt_bias, ssm_a_log=m_ssm_a_log, ssm_d=m_ssm_d, ssm_norm_w=m_ssm_norm_w, w_out=m_w_out, post_mix_w=m_post_mix_w, pre_ffn_w=m_pre_ffn_w, w_up=m_w_up, ffn_conv_w=m_ffn_conv_w, ffn_conv_b=m_ffn_conv_b, w_down=m_w_down, post_ffn_w=m_post_ffn_w)
    mom_v = dict(rel_bias=v_rel_bias, w_ada=v_w_ada, b_ada=v_b_ada, pre_mix_w=v_pre_mix_w, w_in=v_w_in, attn_sinks=v_attn_sinks, ssm_conv_w=v_ssm_conv_w, ssm_conv_b=v_ssm_conv_b, ssm_dt_bias=v_ssm_dt_bias, ssm_a_log=v_ssm_a_log, ssm_d=v_ssm_d, ssm_norm_w=v_ssm_norm_w, w_out=v_w_out, post_mix_w=v_post_mix_w, pre_ffn_w=v_pre_ffn_w, w_up=v_w_up, ffn_conv_w=v_ffn_conv_w, ffn_conv_b=v_ffn_conv_b, w_down=v_w_down, post_ffn_w=v_post_ffn_w)
    order = ['rel_bias', 'w_ada', 'b_ada', 'pre_mix_w', 'w_in', 'attn_sinks', 'ssm_conv_w', 'ssm_conv_b', 'ssm_dt_bias', 'ssm_a_log', 'ssm_d', 'ssm_norm_w', 'w_out', 'post_mix_w', 'pre_ffn_w', 'w_up', 'ffn_conv_w', 'ffn_conv_b', 'w_down', 'post_ffn_w']

    me = 4 * lax.axis_index("x") + 2 * lax.axis_index("y") + lax.axis_index("c")
    xs_ = x[0]
    target = loss_target[0]

    (w_in_g, scw_g, fcw_g, c_g) = all_gather([_b(w_in[0]).T, ssm_conv_w[0], ffn_conv_w[0], c], "gather_weights")
    w_in_t16 = _pad_in_rows(w_in_g.reshape(IN_PROJ_WIDTH, D_MODEL))
    w_in16 = w_in_t16.T
    ssm_cw = _cols_from_blocks(scw_g)
    ffn_cw = _cols_from_blocks(fcw_g)
    c_all = c_g.reshape(N_DEV, D_MODEL)

    n_cols = w_ada.shape[2]
    b_cols = lax.dynamic_slice(b_ada, (0, me * n_cols), (1, n_cols))
    mod_part = ada_fwd(c_all, w_ada[0], b_cols)
    (mod_rows,) = all_to_all([mod_part.reshape(N_DEV, 1, n_cols)], "scatter_mod")
    mod = mod_rows.reshape(N_MOD, 1, D_MODEL)
    shift1, scale1, gate1, shift2, scale2, gate2 = (mod[i] for i in range(N_MOD))

    bucket_band = jnp.asarray(_bucket_table())
    bias = rel_bias_table(rel_bias, bucket_band)
    sinks_row = _lane_pad(attn_sinks)
    dt_bias, a_log, dskip = _lane_pad(ssm_dt_bias), _lane_pad(ssm_a_log), _lane_pad(ssm_d)

    h1, qkv, xbc_raw, z, dt_raw, w_out_g = pre_mix_inproj(
        xs_, pre_mix_w, scale1, shift1, w_in16, [(_b(w_out[0]), False)])
    attn, w_down_g = attn_fwd(qkv, bias, sinks_row, [(_b(w_down[0]), False)])
    ssm, hprev_all, w_up_g = ssm_fwd(xbc_raw, z, dt_raw, ssm_cw, ssm_conv_b, dt_bias, a_log, dskip, ssm_norm_w,
                                     [(_b(w_up[0]).T, False)])
    w_out16 = w_out_g.reshape(D_MODEL, D_MODEL)
    w_out_t16 = w_out16.T
    w_up_t16 = w_up_g.reshape(2 * D_FF, D_MODEL)
    w_up16 = w_up_t16.T
    w_down16 = w_down_g.reshape(D_FF, D_MODEL)
    w_down_t16 = w_down16.T
    mixed, x1, h2 = mix_out(attn, ssm, xs_, w_out16, gate1, post_mix_w, pre_ffn_w, scale2, shift2)
    u, u_raw16, f16 = up_gate(h2, w_up16, ffn_cw, ffn_conv_b)
    d_ffn, dy, loss_part, d_gate2, d_post_ffn_w, g_w_down = down_loss(f16, w_down16, x1, target, gate2, post_ffn_w)

    du_raw, dh2, d_ffn_cw, d_ffn_cb = ffn_bwd(u, u_raw16, d_ffn, ffn_cw, w_down_t16, w_up_t16)
    g_w_up_t = matmul_tn(du_raw, h2, "grad_w_up", FF_CHUNK, D_MODEL, tk=2048)
    (dx1, d_attn, d_ssm, d_pre_ffn_w, d_scale2, d_shift2, d_gate1, d_post_mix_w, g_w_out) = mix_bwd(
        dh2, x1, dy, mixed, attn, ssm, w_out_t16, pre_ffn_w, scale2, gate1, post_mix_w)
    dq, dk, dv, dbias, dsinks, p_w_down = attn_bwd(
        qkv, bias, sinks_row, d_attn, [(g_w_down.reshape(N_DEV, D_FF // N_DEV, D_MODEL), True)])
    d_rel_bias = rel_bias_grad(dbias, bucket_band)
    (dxbc, dz, ddt, d_ssm_cw, d_ssm_cb, d_dt_bias, d_a_log, d_dskip, d_norm_w, p_w_up, p_w_out) = ssm_bwd(
        xbc_raw, z, dt_raw, hprev_all, d_ssm, ssm_cw, ssm_conv_b, dt_bias, a_log, dskip, ssm_norm_w,
        [(g_w_up_t.reshape(N_DEV, 2 * D_FF // N_DEV, D_MODEL), True),
         (g_w_out.reshape(N_DEV, D_MODEL // N_DEV, D_MODEL), True)])
    grad_x, d_pre_mix_w, d_scale1, d_shift1, g_w_in_perm = inproj_bwd(
        dq, dk, dv, dxbc, dz, ddt, xs_, dx1, h1, w_in_t16, pre_mix_w, scale1)
    g_w_in_t = g_w_in_perm[:IN_PROJ_WIDTH]

    d_mod = jnp.concatenate([d_shift1, d_scale1, d_gate1, d_shift2, d_scale2, d_gate2], axis=1)
    late = ("w_in", "ssm_conv_w", "ffn_conv_w")
    full = [g_w_in_t.reshape(N_DEV, IN_PROJ_WIDTH // N_DEV, D_MODEL), _cols_to_blocks(d_ssm_cw[:SSM_CONV]),
            _cols_to_blocks(d_ffn_cw[:FFN_CONV])]
    core = lax.axis_index("c").astype(jnp.int32).reshape(1)
    got = pair_exchange(full, "pair_grads")
    chip_sums = [pair_sum(f_, g_, core, "pair_sum_" + k) for k, f_, g_ in zip(late, full, got)]
    chip_parts = all_to_all(chip_sums, "scatter_grads", CHIP_FLIPS, _chip_index)

    row_g = dict(b_ada=d_mod, pre_mix_w=d_pre_mix_w, attn_sinks=dsinks, ssm_conv_b=d_ssm_cb, ssm_dt_bias=d_dt_bias,
                 ssm_a_log=d_a_log, ssm_d=d_dskip, ssm_norm_w=d_norm_w, post_mix_w=d_post_mix_w,
                 pre_ffn_w=d_pre_ffn_w, ffn_conv_b=d_ffn_cb, post_ffn_w=d_post_ffn_w)
    row = jnp.concatenate([loss_part] + [row_g[k] for k, _, _ in ROW_PARAMS], axis=1)
    row_all, rb_all = all_gather([row, d_rel_bias], "gather_small")
    d_mod_cols = lax.dynamic_slice(row_all.reshape(N_DEV, row.shape[1]), (0, LOSS_LANES + me * n_cols),
                                   (N_DEV, n_cols))
    g_w_ada = ada_bwd(c_all, d_mod_cols)

    wmv = lambda k: (weights[k], mom_m[k], mom_v[k])
    small = adamw_small(row_all, rb_all, wmv("rel_bias"), [wmv(k) for k, _, _ in ROW_PARAMS])
    loss = small[0][0, 0]
    res = {k: tuple(small[1 + 4 * i:5 + 4 * i]) for i, k in enumerate(["rel_bias"] + [k for k, _, _ in ROW_PARAMS])}
    big = list(zip(late, chip_parts)) + [("w_down", p_w_down), ("w_up", p_w_up), ("w_out", p_w_out),
                                        ("w_ada", g_w_ada[None])]
    for k, parts in big:
        if k in ("w_in", "w_up"):
            outs_t = adamw(parts, weights[k][0].T, mom_m[k][0].T, mom_v[k][0].T, "adamw_" + k)
            res[k] = tuple(o.T[None] for o in outs_t)
        else:
            res[k] = tuple(o[None] for o in adamw(parts, weights[k][0], mom_m[k][0], mom_v[k][0], "adamw_" + k))

    outs = [loss, grad_x[None]]
    for field in range(4):
        outs += [res[k][field] for k in order]
    return tuple(outs)
```

```python
import math

import numpy as np
import jax
import jax.numpy as jnp
from jax import lax
from jax.experimental import pallas as pl
from jax.experimental.pallas import tpu as pltpu

F32 = jnp.float32
BF16 = jnp.bfloat16
MESH_ID = pl.DeviceIdType.MESH

N_DEV = 8
D_MODEL = 1024
N_Q_HEADS = 8
N_KV_HEADS = 2
HEAD_DIM = 64
ATTN_WIDTH = 512
KV_WIDTH = 128
ATTN_BLOCK = 128
N_BUCKETS = 32
REL_MAX_DIST = 128
SSM_HEADS = 8
SSM_HEAD_DIM = 64
SSM_WIDTH = 512
SSM_STATE = 128
SSM_GROUPS = 2
SSM_BC = 256
SSM_CONV = 4
SSM_CHUNK = 256
XBC_WIDTH = SSM_WIDTH + 2 * SSM_BC
D_FF = 2816
FFN_CONV = 3
NORM_EPS = 1e-6
N_MOD = 6
IN_PROJ_WIDTH = 2312
QKV_W = ATTN_WIDTH + 2 * KV_WIDTH
COLS_XS = (QKV_W, QKV_W + SSM_WIDTH)
COLS_Z = (COLS_XS[1], COLS_XS[1] + SSM_WIDTH)
COLS_BC = (COLS_Z[1], COLS_Z[1] + 2 * SSM_BC)
DT_PAD = 128
COLS_DT = (COLS_BC[1], COLS_BC[1] + DT_PAD)
PROJ_PAD = COLS_DT[1]
FF_CHUNK = 1408

ADAM_LR = 0.001
ADAM_B1 = 0.9
ADAM_B2 = 0.999
ADAM_EPS = 1e-08
ADAM_WD = 0.01
ADAM_STEP = 10

TOKEN_TILE = 256
HALO = 8
NEXT = 16
VMEM_LIMIT = 56 * 1024 * 1024


def _params(vmem=VMEM_LIMIT, n_axes=1):
    return pltpu.CompilerParams(dimension_semantics=("arbitrary",) * n_axes, vmem_limit_bytes=vmem)


def _b(x):
    return x.astype(BF16)


def _nn(a, b):
    return jnp.dot(_b(a), _b(b), preferred_element_type=F32)


def _nt(a, b):
    return lax.dot_general(_b(a), _b(b), (((1,), (1,)), ((), ())), preferred_element_type=F32)


def _tn(a, b):
    return lax.dot_general(_b(a), _b(b), (((0,), (0,)), ((), ())), preferred_element_type=F32)


@jax.custom_vjp
def mm(a, b):
    return _nn(a, b)


mm.defvjp(lambda a, b: (_nn(a, b), (a, b)),
          lambda r, g: (_nt(g, r[1]).astype(r[0].dtype), _tn(r[0], g).astype(r[1].dtype)))


@jax.custom_vjp
def mm_nt(a, b):
    return _nt(a, b)


mm_nt.defvjp(lambda a, b: (_nt(a, b), (a, b)),
             lambda r, g: (_nn(g, r[1]).astype(r[0].dtype), _tn(g, r[0]).astype(r[1].dtype)))


@jax.custom_vjp
def mm_tn(a, b):
    return _tn(a, b)


mm_tn.defvjp(lambda a, b: (_tn(a, b), (a, b)),
             lambda r, g: (_nt(r[1], g).astype(r[0].dtype), _nn(r[0], g).astype(r[1].dtype)))


def _rms(x, w):
    return x * lax.rsqrt(jnp.mean(x * x, axis=-1, keepdims=True) + NORM_EPS) * w


def _norm_mod(x, w, scale, shift):
    return _rms(x, w) * (1.0 + scale) + shift


def _rms_bwd(x, w, dy):
    r = lax.rsqrt(jnp.mean(x * x, axis=-1, keepdims=True) + NORM_EPS)
    xhat = x * r
    g = dy * w
    dx = r * (g - xhat * jnp.mean(g * xhat, axis=-1, keepdims=True))
    return dx, jnp.sum(dy * xhat, axis=0, keepdims=True)


def _norm_mod_bwd(x, w, scale, dh):
    dx, da = _rms_bwd(x, w * (1.0 + scale), dh)
    return dx, da * (1.0 + scale), da * w, jnp.sum(dh, axis=0, keepdims=True)


def _gated_rms_bwd(m, gate, w, dy):
    dm, t = _rms_bwd(m, w * gate, dy)
    return dm, t * w, t * gate


def _silu(x):
    return x * jax.nn.sigmoid(x)


def _conv_rows(xin, w, k):
    acc = xin * w[k - 1:k, :]
    for j in range(1, k):
        acc = acc + pltpu.roll(xin, j, axis=0) * w[k - 1 - j:k - j, :]
    return acc


def _conv_rows_t(du, w, k):
    n = du.shape[0]
    acc = du * w[k - 1:k, :]
    for j in range(1, k):
        acc = acc + pltpu.roll(du, n - j, axis=0) * w[k - 1 - j:k - j, :]
    return acc


def _row(i):
    return (i, 0)


def _const(i):
    return (0, 0)


def _vec(n):
    return pl.BlockSpec((1, n), _const)


def _resident(shape):
    return pl.BlockSpec(shape, _const, pipeline_mode=pl.Buffered(1))


def _block_index(p):
    return 4 * p[0] + 2 * p[1] + p[2]


def all_gather(arrs, name):
    n = len(arrs)

    def body(*refs):
        ins, outs = refs[:n], refs[n:2 * n]
        send_sems, recv_sems, local_sems = refs[2 * n:]
        x, y, c = lax.axis_index("x"), lax.axis_index("y"), lax.axis_index("c")
        me, sibling = (x, y, c), (x, y, 1 - c)
        chips = [(1 - x, y), (x, 1 - y), (1 - x, 1 - y)]

        def copy(a, k, block, to, src=None):
            dst = outs[a].at[_block_index(block)]
            return pltpu.make_async_remote_copy(
                src_ref=dst if src is None else src, dst_ref=dst,
                send_sem=send_sems.at[a * 7 + k], recv_sem=recv_sems.at[a * 7 + k],
                device_id=to, device_id_type=MESH_ID)

        mine = [pltpu.make_async_copy(ins[a], outs[a].at[_block_index(me)], local_sems.at[a]) for a in range(n)]
        for cp in mine:
            cp.start()
        first = []
        for a in range(n):
            first.append(copy(a, 0, me, sibling, src=ins[a]))
            first += [copy(a, 1 + j, me, (*chip, c), src=ins[a]) for j, chip in enumerate(chips)]
        for cp in first:
            cp.start()
        passed = []
        for j, chip in enumerate(chips):
            for a in range(n):
                copy(a, 1 + j, (*chip, c), me).wait_recv()
                cp = copy(a, 4 + j, (*chip, c), sibling)
                cp.start()
                passed.append(cp)
        for a in range(n):
            copy(a, 0, sibling, me).wait_recv()
            for j, chip in enumerate(chips):
                copy(a, 4 + j, (*chip, 1 - c), me).wait_recv()
        for cp in first + passed:
            cp.wait_send()
        for cp in mine:
            cp.wait()

    any_spec = pl.BlockSpec(memory_space=pl.ANY)
    return pl.pallas_call(
        body, name=name,
        out_shape=[jax.ShapeDtypeStruct((N_DEV,) + a.shape, a.dtype) for a in arrs],
        in_specs=[any_spec] * n, out_specs=[any_spec] * n,
        scratch_shapes=[pltpu.SemaphoreType.DMA((7 * n,)), pltpu.SemaphoreType.DMA((7 * n,)),
                        pltpu.SemaphoreType.DMA((n,))],
    )(*arrs)


ALL_FLIPS = ((0, 0, 1), (0, 1, 0), (0, 1, 1), (1, 0, 0), (1, 0, 1), (1, 1, 0), (1, 1, 1))
CHIP_FLIPS = ((0, 1, 0), (1, 0, 0), (1, 1, 0))


def _chip_index(p):
    return 2 * p[0] + p[1]


def all_to_all(arrs, name, flips=ALL_FLIPS, index=_block_index):
    n = len(arrs)
    nf = len(flips)

    def body(*refs):
        ins, outs = refs[:n], refs[n:2 * n]
        send_sems, recv_sems, local_sems = refs[2 * n:]
        pos = (lax.axis_index("x"), lax.axis_index("y"), lax.axis_index("c"))
        me = index(pos)
        peers = [tuple(1 - p if f else p for p, f in zip(pos, flip)) for flip in flips]

        def copy(a, k):
            peer = peers[k]
            return pltpu.make_async_remote_copy(
                src_ref=ins[a].at[index(peer)], dst_ref=outs[a].at[me],
                send_sem=send_sems.at[a * nf + k], recv_sem=recv_sems.at[a * nf + k],
                device_id=peer, device_id_type=MESH_ID)

        def landed(a, k):
            slot = outs[a].at[index(peers[k])]
            return pltpu.make_async_remote_copy(
                src_ref=slot, dst_ref=slot,
                send_sem=send_sems.at[a * nf + k], recv_sem=recv_sems.at[a * nf + k],
                device_id=peers[k], device_id_type=MESH_ID)

        mine = [pltpu.make_async_copy(ins[a].at[me], outs[a].at[me], local_sems.at[a]) for a in range(n)]
        for cp in mine:
            cp.start()
        sent = [copy(a, k) for a in range(n) for k in range(nf)]
        for cp in sent:
            cp.start()
        for a in range(n):
            for k in range(nf):
                landed(a, k).wait_recv()
        for cp in sent:
            cp.wait_send()
        for cp in mine:
            cp.wait()

    any_spec = pl.BlockSpec(memory_space=pl.ANY)
    return pl.pallas_call(
        body, name=name,
        out_shape=[jax.ShapeDtypeStruct(a.shape, a.dtype) for a in arrs],
        in_specs=[any_spec] * n, out_specs=[any_spec] * n,
        scratch_shapes=[pltpu.SemaphoreType.DMA((nf * n,)), pltpu.SemaphoreType.DMA((nf * n,)),
                        pltpu.SemaphoreType.DMA((n,))],
    )(*arrs)


def _direct_exchange(src, dst, sems, scatter):
    send_sems, recv_sems, local_sem = sems
    pos = (lax.axis_index("x"), lax.axis_index("y"), lax.axis_index("c"))
    me = _block_index(pos)
    peers = [tuple(1 - p if f else p for p, f in zip(pos, flip)) for flip in ALL_FLIPS]

    def outgoing(k):
        return pltpu.make_async_remote_copy(
            src_ref=src.at[_block_index(peers[k])] if scatter else src, dst_ref=dst.at[me],
            send_sem=send_sems.at[k], recv_sem=recv_sems.at[k], device_id=peers[k], device_id_type=MESH_ID)

    def incoming(k):
        slot = dst.at[_block_index(peers[k])]
        return pltpu.make_async_remote_copy(
            src_ref=slot, dst_ref=slot, send_sem=send_sems.at[k], recv_sem=recv_sems.at[k],
            device_id=peers[k], device_id_type=MESH_ID)

    def local():
        return pltpu.make_async_copy(src.at[me] if scatter else src, dst.at[me], local_sem)

    def start():
        local().start()
        for k in range(len(ALL_FLIPS)):
            outgoing(k).start()

    def finish():
        for k in range(len(ALL_FLIPS)):
            incoming(k).wait_recv()
        for k in range(len(ALL_FLIPS)):
            outgoing(k).wait_send()
        local().wait()

    return start, finish


def hosted_call(body, exchanges, steps, n_in, n_out, **call):
    n_ex = len(exchanges)

    def wrapped(*refs):
        ins, srcs = refs[:n_in], refs[n_in:n_in + n_ex]
        outs = refs[n_in + n_ex:n_in + n_ex + n_out]
        dsts = refs[n_in + n_ex + n_out:n_in + 2 * n_ex + n_out]
        rest = refs[n_in + 2 * n_ex + n_out:]
        scratch, sems = rest[:len(rest) - 3 * n_ex], rest[len(rest) - 3 * n_ex:]
        plans = [_direct_exchange(srcs[e], dsts[e], sems[3 * e:3 * e + 3], exchanges[e][1]) for e in range(n_ex)]

        @pl.when(pl.program_id(0) == 0)
        def _():
            for start, _ in plans:
                start()

        body(*ins, *outs, *scratch)

        @pl.when(pl.program_id(0) == steps - 1)
        def _():
            for _, finish in plans:
                finish()

    any_spec = pl.BlockSpec(memory_space=pl.ANY)
    landings = [jax.ShapeDtypeStruct(src.shape if scatter else (N_DEV,) + src.shape, src.dtype)
                for src, scatter in exchanges]
    n_flips = len(ALL_FLIPS)
    sems = [pltpu.SemaphoreType.DMA((n_flips,)), pltpu.SemaphoreType.DMA((n_flips,)), pltpu.SemaphoreType.DMA(())]
    return pl.pallas_call(
        wrapped, grid=(steps,),
        in_specs=list(call.pop("in_specs")) + [any_spec] * n_ex,
        out_specs=list(call.pop("out_specs")) + [any_spec] * n_ex,
        out_shape=list(call.pop("out_shape")) + landings,
        scratch_shapes=list(call.pop("scratch_shapes", [])) + sems * n_ex,
        **call)


def _grid_call(body, steps, args, exchanges, **call):
    if not exchanges:
        return pl.pallas_call(body, grid=(steps,), **call)(*args)
    srcs = [src for src, _ in exchanges]
    return hosted_call(body, exchanges, steps, len(args), len(call["out_shape"]), **call)(*args, *srcs)


N_CHIPS = 4


def pair_exchange(arrs, name):
    n = len(arrs)

    def body(*refs):
        ins, outs = refs[:n], refs[n:2 * n]
        send_sems, recv_sems = refs[2 * n:]
        x, y, c = lax.axis_index("x"), lax.axis_index("y"), lax.axis_index("c")
        sibling = (x, y, 1 - c)
        sent = []
        for a in range(n):
            for q in range(N_CHIPS):
                cp = pltpu.make_async_remote_copy(
                    src_ref=ins[a].at[2 * q + (1 - c)], dst_ref=outs[a].at[q],
                    send_sem=send_sems.at[a * N_CHIPS + q], recv_sem=recv_sems.at[a * N_CHIPS + q],
                    device_id=sibling, device_id_type=MESH_ID)
                cp.start()
                sent.append(cp)
        for cp in sent:
            cp.wait_recv()
        for cp in sent:
            cp.wait_send()

    any_spec = pl.BlockSpec(memory_space=pl.ANY)
    return pl.pallas_call(
        body, name=name,
        out_shape=[jax.ShapeDtypeStruct((N_CHIPS,) + a.shape[1:], a.dtype) for a in arrs],
        in_specs=[any_spec] * n, out_specs=[any_spec] * n,
        scratch_shapes=[pltpu.SemaphoreType.DMA((N_CHIPS * n,)), pltpu.SemaphoreType.DMA((N_CHIPS * n,))],
    )(*arrs)


def pair_sum(full, got, core, name):
    _, r, n = full.shape
    tr = _row_tile(r)

    def body(c_ref, mine_ref, got_ref, o_ref):
        o_ref[...] = _b(mine_ref[...] + got_ref[...])

    grid_spec = pltpu.PrefetchScalarGridSpec(
        num_scalar_prefetch=1, grid=(N_CHIPS, r // tr),
        in_specs=[pl.BlockSpec((1, tr, n), lambda q, i, c_ref: (2 * q + c_ref[0], i, 0)),
                  pl.BlockSpec((1, tr, n), lambda q, i, c_ref: (q, i, 0))],
        out_specs=pl.BlockSpec((1, tr, n), lambda q, i, c_ref: (q, i, 0)))
    return pl.pallas_call(body, name=name, grid_spec=grid_spec,
                          out_shape=jax.ShapeDtypeStruct((N_CHIPS, r, n), BF16),
                          compiler_params=_params(n_axes=2))(core, full, got)


def _row_tile(r):
    for cand in (256, 128, 64, 32, 16):
        if r % cand == 0 and r > cand:
            return cand
    return r


def ada_fwd(c_all, w_ada, b_cols):
    def body(c_ref, w_ref, b_ref, o_ref):
        o_ref[...] = _nn(_silu(c_ref[...]), w_ref[...]) + b_ref[...]

    return pl.pallas_call(body, name="ada_fwd",
                          out_shape=jax.ShapeDtypeStruct((N_DEV, w_ada.shape[1]), F32),
                          compiler_params=_params(n_axes=0))(c_all, w_ada, b_cols)


def ada_bwd(c_all, g_cols):
    def body(c_ref, g_ref, o_ref):
        o_ref[...] = _tn(_silu(c_ref[...]), g_ref[...])

    return pl.pallas_call(body, name="ada_bwd",
                          out_shape=jax.ShapeDtypeStruct((c_all.shape[1], g_cols.shape[1]), F32),
                          compiler_params=_params(n_axes=0))(c_all, g_cols)


def matmul_tn(a, b, name, bm, bn, tk=512):
    s, m = a.shape
    n = b.shape[1]
    tk = min(tk, s)

    def body(a_ref, b_ref, o_ref):
        @pl.when(pl.program_id(2) == 0)
        def _():
            o_ref[...] = jnp.zeros_like(o_ref)

        o_ref[...] += _tn(a_ref[...], b_ref[...])

    return pl.pallas_call(
        body, name=name, grid=(m // bm, n // bn, s // tk),
        in_specs=[pl.BlockSpec((tk, bm), lambda i, j, k: (k, i)), pl.BlockSpec((tk, bn), lambda i, j, k: (k, j))],
        out_specs=pl.BlockSpec((bm, bn), lambda i, j, k: (i, j)),
        out_shape=jax.ShapeDtypeStruct((m, n), F32),
        compiler_params=_params(n_axes=3))(a, b)


def pre_mix_inproj(x, w, scale, shift, w_in16, exchange=None, tm=4 * TOKEN_TILE):
    s = x.shape[0]
    tm = min(tm, s)

    def body(x_ref, w_ref, sc_ref, sh_ref, win_ref, h_ref, qkv_ref, xbc_ref, z_ref, dt_ref):
        h16 = _b(_norm_mod(x_ref[...], w_ref[...], sc_ref[...], sh_ref[...]))
        h_ref[...] = h16
        dot = lambda lo, hi: jnp.dot(h16, win_ref[:, lo:hi], preferred_element_type=F32)
        qkv_ref[...] = _b(dot(0, QKV_W))
        xbc_ref[:, :SSM_WIDTH] = dot(*COLS_XS)
        xbc_ref[:, SSM_WIDTH:] = dot(*COLS_BC)
        z_ref[...] = dot(*COLS_Z)
        dt_ref[...] = dot(*COLS_DT)

    tile = lambda n: pl.BlockSpec((tm, n), _row)
    return _grid_call(
        body, s // tm, (x, w, scale, shift, w_in16), exchange, name="pre_mix_inproj",
        in_specs=[tile(D_MODEL), _vec(D_MODEL), _vec(D_MODEL), _vec(D_MODEL), _resident((D_MODEL, PROJ_PAD))],
        out_specs=[tile(D_MODEL), tile(QKV_W), tile(XBC_WIDTH), tile(SSM_WIDTH), tile(DT_PAD)],
        out_shape=[jax.ShapeDtypeStruct((s, D_MODEL), BF16), jax.ShapeDtypeStruct((s, QKV_W), BF16),
                   jax.ShapeDtypeStruct((s, XBC_WIDTH), F32), jax.ShapeDtypeStruct((s, SSM_WIDTH), F32),
                   jax.ShapeDtypeStruct((s, DT_PAD), F32)],
        compiler_params=_params())


ATTN_QB_FWD, ATTN_QB_BWD = 4, 2


def _attn_tile(q, kp, kc, vp, vc, bias, sinks):
    lq = ATTN_BLOCK
    group = N_Q_HEADS // N_KV_HEADS
    lanes = lax.broadcasted_iota(jnp.int32, (1, 128), 1)
    rid = lax.broadcasted_iota(jnp.int32, (group * lq, 1), 0)
    sink_cols = []
    for hk in range(N_KV_HEADS):
        sink = jnp.zeros((group * lq, 1), F32)
        for g in range(group):
            s_h = jnp.sum(jnp.where(lanes == hk * group + g, sinks, 0.0), axis=-1, keepdims=True)
            sink = jnp.where((rid >= g * lq) & (rid < (g + 1) * lq), s_h, sink)
        sink_cols.append(sink)
    kall = jnp.concatenate([kp, kc], axis=0)
    vall = jnp.concatenate([vp, vc], axis=0)
    blocks = []
    for b in range(q.shape[0] // lq):
        qb = q[b * lq:(b + 1) * lq]
        outs = []
        for hk in range(N_KV_HEADS):
            cols = slice(hk * HEAD_DIM, (hk + 1) * HEAD_DIM)
            kb = kall[b * lq:(b + 2) * lq, cols]
            vb = vall[b * lq:(b + 2) * lq, cols]
            qg = jnp.concatenate([qb[:, (hk * group + g) * HEAD_DIM:(hk * group + g + 1) * HEAD_DIM]
                                  for g in range(group)], axis=0)
            sc = mm_nt(qg, kb) * (HEAD_DIM ** -0.5) + bias[b][hk]
            sink = sink_cols[hk]
            m = lax.stop_gradient(jnp.maximum(jnp.max(sc, axis=-1, keepdims=True), sink))
            p = jnp.exp(sc - m)
            probs = p / (jnp.sum(p, axis=-1, keepdims=True) + jnp.exp(sink - m))
            og = mm(probs, vb)
            outs += [og[g * lq:(g + 1) * lq] for g in range(group)]
        blocks.append(jnp.concatenate(outs, axis=1))
    return jnp.concatenate(blocks, axis=0)


def _attn_tile_bwd(q, kp, kc, vp, vc, bias, sinks, do):
    lq = ATTN_BLOCK
    group = N_Q_HEADS // N_KV_HEADS
    scale = HEAD_DIM ** -0.5
    lanes = lax.broadcasted_iota(jnp.int32, (1, 128), 1)
    rid = lax.broadcasted_iota(jnp.int32, (group * lq, 1), 0)
    sink_cols = []
    for hk in range(N_KV_HEADS):
        sink = jnp.zeros((group * lq, 1), F32)
        for g in range(group):
            s_h = jnp.sum(jnp.where(lanes == hk * group + g, sinks, 0.0), axis=-1, keepdims=True)
            sink = jnp.where((rid >= g * lq) & (rid < (g + 1) * lq), s_h, sink)
        sink_cols.append(sink)
    kall = jnp.concatenate([kp, kc], axis=0)
    vall = jnp.concatenate([vp, vc], axis=0)
    dsk = jnp.zeros((1, 128), F32)
    dq_blocks, dbias = [], []
    nqb = q.shape[0] // lq
    dk_parts = [[None] * nqb for _ in range(N_KV_HEADS)]
    dv_parts = [[None] * nqb for _ in range(N_KV_HEADS)]
    for b in range(nqb):
        qb, dob = q[b * lq:(b + 1) * lq], do[b * lq:(b + 1) * lq]
        dq_heads, dbias_b = [], []
        for hk in range(N_KV_HEADS):
            cols = slice(hk * HEAD_DIM, (hk + 1) * HEAD_DIM)
            kb = kall[b * lq:(b + 2) * lq, cols]
            vb = vall[b * lq:(b + 2) * lq, cols]
            heads = [hk * group + g for g in range(group)]
            qg = jnp.concatenate([qb[:, h * HEAD_DIM:(h + 1) * HEAD_DIM] for h in heads], axis=0)
            dog = jnp.concatenate([dob[:, h * HEAD_DIM:(h + 1) * HEAD_DIM] for h in heads], axis=0)
            sink = sink_cols[hk]
            sc = _nt(qg, kb) * scale + bias[b][hk]
            m = jnp.maximum(jnp.max(sc, axis=-1, keepdims=True), sink)
            p = jnp.exp(sc - m)
            es = jnp.exp(sink - m)
            inv = 1.0 / (jnp.sum(p, axis=-1, keepdims=True) + es)
            probs = p * inv
            dprobs = _nt(dog, vb)
            delta = jnp.sum(probs * dprobs, axis=-1, keepdims=True)
            dsc = probs * (dprobs - delta)
            dbias_b.append(dsc)
            dsink = -(es * inv) * delta
            for g, h in enumerate(heads):
                tot = jnp.sum(dsink[g * lq:(g + 1) * lq], axis=0, keepdims=True)
                dsk = dsk + jnp.where(lanes == h, tot, 0.0)
            dqg = _nn(dsc, kb) * scale
            dq_heads += [dqg[g * lq:(g + 1) * lq] for g in range(group)]
            dk_parts[hk][b] = _tn(dsc, qg) * scale
            dv_parts[hk][b] = _tn(probs, dog)
        dq_blocks.append(jnp.concatenate(dq_heads, axis=1))
        dbias.append(dbias_b)

    def overlap_add(parts):
        chunks = []
        for r in range(nqb + 1):
            acc = None
            if r < nqb:
                acc = parts[r][:lq]
            if r >= 1:
                tail = parts[r - 1][lq:]
                acc = tail if acc is None else acc + tail
            chunks.append(acc)
        return jnp.concatenate(chunks, axis=0)

    dkall = jnp.concatenate([overlap_add(dk_parts[hk]) for hk in range(N_KV_HEADS)], axis=1)
    dvall = jnp.concatenate([overlap_add(dv_parts[hk]) for hk in range(N_KV_HEADS)], axis=1)
    return jnp.concatenate(dq_blocks, axis=0), dkall, dvall, dbias, dsk


def _attn_in_specs(nt, clamp, nqb):
    lq, tq = ATTN_BLOCK, ATTN_BLOCK * nqb
    cur = lambda n: jnp.minimum(n, nt - 1) if clamp else n
    prev = lambda n: jnp.maximum(cur(n) * nqb - 1, 0)
    kcol, vcol = ATTN_WIDTH // KV_WIDTH, ATTN_WIDTH // KV_WIDTH + 1
    return [pl.BlockSpec((tq, ATTN_WIDTH), lambda n: (cur(n), 0)),
            pl.BlockSpec((lq, KV_WIDTH), lambda n: (prev(n), kcol)),
            pl.BlockSpec((tq, KV_WIDTH), lambda n: (cur(n), kcol)),
            pl.BlockSpec((lq, KV_WIDTH), lambda n: (prev(n), vcol)),
            pl.BlockSpec((tq, KV_WIDTH), lambda n: (cur(n), vcol)),
            pl.BlockSpec((2, N_KV_HEADS, 4 * lq, 2 * lq), lambda n: (0, 0, 0, 0)),
            _vec(128)]


def _tile_bias(bias_ref, first, nqb):
    return [[jnp.where(first, bias_ref[1, hk], bias_ref[0, hk]) if b == 0 else bias_ref[0, hk]
             for hk in range(N_KV_HEADS)] for b in range(nqb)]


def attn_fwd(qkv, bias, sinks_rows, exchange=None):
    s = qkv.shape[0]
    nqb = min(ATTN_QB_FWD, s // ATTN_BLOCK)
    tq = ATTN_BLOCK * nqb
    nt = s // tq

    def body(q_ref, kp_ref, kc_ref, vp_ref, vc_ref, bias_ref, sk_ref, o_ref):
        f = lambda r: r[...].astype(F32)
        o = _attn_tile(f(q_ref), f(kp_ref), f(kc_ref), f(vp_ref), f(vc_ref),
                       _tile_bias(bias_ref, pl.program_id(0) == 0, nqb), sk_ref[...])
        o_ref[...] = _b(o)

    return _grid_call(
        body, nt, (qkv, qkv, qkv, qkv, qkv, bias, sinks_rows), exchange, name="attn_fwd",
        in_specs=_attn_in_specs(nt, False, nqb),
        out_specs=[pl.BlockSpec((tq, ATTN_WIDTH), _row)],
        out_shape=[jax.ShapeDtypeStruct((s, ATTN_WIDTH), BF16)],
        compiler_params=_params())


def attn_bwd(qkv, bias, sinks_rows, d_attn, exchange=None):
    s = qkv.shape[0]
    nqb = ATTN_QB_BWD
    lq, tq = ATTN_BLOCK, ATTN_BLOCK * nqb
    nt = s // tq

    def body(q_ref, kp_ref, kc_ref, vp_ref, vc_ref, bias_ref, sk_ref, do_ref,
             dq_ref, dk_ref, dv_ref, dbias_ref, dsk_ref, carry_k, carry_v):
        n = pl.program_id(0)

        @pl.when(n == 0)
        def _():
            dbias_ref[...] = jnp.zeros_like(dbias_ref)
            dsk_ref[...] = jnp.zeros_like(dsk_ref)
            carry_k[...] = jnp.zeros_like(carry_k)
            carry_v[...] = jnp.zeros_like(carry_v)

        @pl.when(n < nt)
        def _():
            f = lambda r: r[...].astype(F32)
            dq, dkall, dvall, dbias, dsk = _attn_tile_bwd(
                f(q_ref), f(kp_ref), f(kc_ref), f(vp_ref), f(vc_ref), _tile_bias(bias_ref, n == 0, nqb), sk_ref[...],
                f(do_ref))
            dkp, dkc, dvp, dvc = dkall[:lq], dkall[lq:], dvall[:lq], dvall[lq:]
            dq_ref[...] = _b(dq)
            done = tq - lq
            dk_ref[:done, :] = _b(carry_k[:done, :])
            dv_ref[:done, :] = _b(carry_v[:done, :])
            dk_ref[done:, :] = _b(carry_k[done:, :] + dkp)
            dv_ref[done:, :] = _b(carry_v[done:, :] + dvp)
            carry_k[...] = dkc
            carry_v[...] = dvc
            dsk_ref[...] += dsk
            first = (n == 0).astype(F32)
            for hk in range(N_KV_HEADS):
                total = dbias[0][hk]
                for b in range(1, nqb):
                    total = total + dbias[b][hk]
                dbias_ref[0, hk] += total - first * dbias[0][hk]
                dbias_ref[1, hk] += first * dbias[0][hk]

        @pl.when(n == nt)
        def _():
            dk_ref[...] = _b(carry_k[...])
            dv_ref[...] = _b(carry_v[...])

    cur = lambda n: (jnp.minimum(n, nt - 1), 0)
    done_map = lambda n: (jnp.maximum(n - 1, 0), 0)
    return _grid_call(
        body, nt + 1, (qkv, qkv, qkv, qkv, qkv, bias, sinks_rows, d_attn), exchange, name="attn_bwd",
        in_specs=_attn_in_specs(nt, True, nqb) + [pl.BlockSpec((tq, ATTN_WIDTH), cur)],
        out_specs=[pl.BlockSpec((tq, ATTN_WIDTH), cur), pl.BlockSpec((tq, KV_WIDTH), done_map),
                   pl.BlockSpec((tq, KV_WIDTH), done_map),
                   pl.BlockSpec((2, N_KV_HEADS, 4 * lq, 2 * lq), lambda n: (0, 0, 0, 0)), _vec(128)],
        out_shape=[jax.ShapeDtypeStruct((s, ATTN_WIDTH), BF16), jax.ShapeDtypeStruct((s, KV_WIDTH), BF16),
                   jax.ShapeDtypeStruct((s, KV_WIDTH), BF16),
                   jax.ShapeDtypeStruct((2, N_KV_HEADS, 4 * lq, 2 * lq), F32), jax.ShapeDtypeStruct((1, 128), F32)],
        scratch_shapes=[pltpu.VMEM((tq, KV_WIDTH), F32), pltpu.VMEM((tq, KV_WIDTH), F32)],
        compiler_params=_params())


def rel_bias_table(rel_bias, bucket):
    lq = ATTN_BLOCK
    group = N_Q_HEADS // N_KV_HEADS

    def body(rb_ref, bk_ref, o_ref):
        bk = bk_ref[...]
        prev_keys = lax.broadcasted_iota(jnp.int32, bk.shape, 1) < lq
        accs = [jnp.full(bk.shape, -1e30, F32) for _ in range(N_Q_HEADS)]
        for b in range(N_BUCKETS):
            hit = bk == b
            accs = [jnp.where(hit, rb_ref[b, h], acc) for h, acc in enumerate(accs)]
        for h in range(N_Q_HEADS):
            rows = slice((h % group) * lq, (h % group + 1) * lq)
            o_ref[0, h // group, rows, :] = accs[h]
            o_ref[1, h // group, rows, :] = jnp.where(prev_keys, -1e30, accs[h])

    return pl.pallas_call(
        body, name="rel_bias_table",
        in_specs=[pl.BlockSpec(memory_space=pltpu.SMEM), pl.BlockSpec(memory_space=pltpu.VMEM)],
        out_shape=jax.ShapeDtypeStruct((2, N_KV_HEADS, group * lq, 2 * lq), F32),
        compiler_params=_params(n_axes=0))(rel_bias, bucket)


def rel_bias_grad(dbias, bucket):
    lq = ATTN_BLOCK
    group = N_Q_HEADS // N_KV_HEADS

    def body(db_ref, bk_ref, o_ref):
        rows = lax.broadcasted_iota(jnp.int32, (N_BUCKETS, 128), 0)
        lanes = lax.broadcasted_iota(jnp.int32, (N_BUCKETS, 128), 1)
        bk = bk_ref[...]
        per_head = []
        for h in range(N_Q_HEADS):
            sl = slice((h % group) * lq, (h % group + 1) * lq)
            per_head.append(db_ref[0, h // group, sl, :] + db_ref[1, h // group, sl, :])

        def per_bucket(b, acc):
            hit = (bk == b).astype(F32)
            for h in range(N_Q_HEADS):
                val = jnp.sum(per_head[h] * hit, keepdims=True)
                acc = acc + jnp.where((rows == b) & (lanes == h), val, 0.0)
            return acc

        o_ref[...] = lax.fori_loop(0, N_BUCKETS, per_bucket, jnp.zeros((N_BUCKETS, 128), F32))

    return pl.pallas_call(body, name="rel_bias_grad", out_shape=jax.ShapeDtypeStruct((N_BUCKETS, 128), F32),
                          compiler_params=_params(n_axes=0))(dbias, bucket)


def _tri_sum(a, upper):
    n = a.shape[0]
    ri = lax.broadcasted_iota(jnp.int32, (n, n), 0)
    ci = lax.broadcasted_iota(jnp.int32, (n, n), 1)
    tri = ((ri <= ci) if upper else (ri >= ci)).astype(BF16)
    hi = a.astype(BF16)
    rest = a - hi.astype(F32)
    mid = rest.astype(BF16)
    lo = (rest - mid.astype(F32)).astype(BF16)
    dot = lambda part: jnp.dot(tri, part, preferred_element_type=F32)
    return dot(hi) + dot(mid) + dot(lo)


@jax.custom_vjp
def _cumsum_rows(a):
    return _tri_sum(a, False)


_cumsum_rows.defvjp(lambda a: (_tri_sum(a, False), None), lambda _, g: (_tri_sum(g, True),))


def _ssm_core(u, z, dt_raw, hprev, dt_bias, a_log, dskip, norm_w):
    lc = u.shape[0]
    xbc = _silu(u)
    xs, bm, cm = xbc[:, :SSM_WIDTH], xbc[:, SSM_WIDTH:SSM_WIDTH + SSM_BC], xbc[:, SSM_WIDTH + SSM_BC:]
    dt = jax.nn.softplus(dt_raw + dt_bias)
    adt = dt * (-jnp.exp(a_log))
    ri = lax.broadcasted_iota(jnp.int32, (lc, lc), 0)
    ci = lax.broadcasted_iota(jnp.int32, (lc, lc), 1)
    causal = ri >= ci
    acum = _cumsum_rows(adt)
    acum_t = acum.T
    last = acum[lc - 1:lc, :]
    per_group = SSM_HEADS // SSM_GROUPS
    lane = lax.broadcasted_iota(jnp.int32, (1, 128), 1)
    rowid = lax.broadcasted_iota(jnp.int32, (128, 1), 0)
    lo_lanes = lane < SSM_HEAD_DIM
    ys, hs = [], []
    for g in range(SSM_GROUPS):
        bg = bm[:, g * SSM_STATE:(g + 1) * SSM_STATE]
        cg = cm[:, g * SSM_STATE:(g + 1) * SSM_STATE]
        cb = mm_nt(cg, bg)
        for pp in range(per_group // 2):
            ha = g * per_group + 2 * pp
            xp = xs[:, ha * SSM_HEAD_DIM:(ha + 2) * SSM_HEAD_DIM]
            hp = hprev[ha * SSM_HEAD_DIM:(ha + 2) * SSM_HEAD_DIM, :]
            xcp = xp * jnp.where(lo_lanes, dt[:, ha:ha + 1], dt[:, ha + 1:ha + 2])
            y_h, st_h = [], []
            for h in (ha, ha + 1):
                col, rowv, lasth = acum[:, h:h + 1], acum_t[h:h + 1, :], last[:, h:h + 1]
                decay = jnp.exp(jnp.where(causal, col - rowv, -1e30))
                y_h.append(mm(cb * decay, xcp) + mm_nt(cg * jnp.exp(col), hp))
                st_h.append(mm_tn(xcp, bg * jnp.exp(lasth - col)))
            y_pair = jnp.where(lo_lanes, y_h[0], y_h[1])
            st_pair = jnp.where(rowid < SSM_HEAD_DIM, st_h[0], st_h[1])
            la, lb = last[:, ha:ha + 1], last[:, ha + 1:ha + 2]
            hs.append(jnp.exp(jnp.where(rowid < SSM_HEAD_DIM, la, lb)) * hp + st_pair)
            dsk = jnp.where(lo_lanes, dskip[:, ha:ha + 1], dskip[:, ha + 1:ha + 2])
            ys.append(y_pair + dsk * xp)
    y = jnp.concatenate(ys, axis=1) * _silu(z)
    gw = SSM_WIDTH // SSM_GROUPS
    outs = []
    for g in range(SSM_GROUPS):
        yg = y[:, g * gw:(g + 1) * gw]
        outs.append(yg * lax.rsqrt(jnp.mean(yg * yg, axis=-1, keepdims=True) + NORM_EPS))
    return jnp.concatenate(outs, axis=1) * norm_w, jnp.concatenate(hs, axis=0)


def _ssm_param_specs():
    return [pl.BlockSpec((SSM_CONV, XBC_WIDTH), _const), _vec(XBC_WIDTH), _vec(128), _vec(128), _vec(128),
            _vec(SSM_WIDTH)]


SSM_FWD_SUB = 2
SSM_BWD_SUB = 1


def ssm_fwd(xbc_raw, z, dt_raw, conv_w, conv_b, dt_bias, a_log, dskip, norm_w, exchange=None):
    s = xbc_raw.shape[0]
    lc = SSM_CHUNK
    lt = lc * SSM_FWD_SUB
    hrows = SSM_HEADS * SSM_HEAD_DIM

    def body(x_ref, halo_ref, z_ref, dt_ref, cw_ref, cb_ref, dtb_ref, al_ref, dk_ref, nw_ref,
             o_ref, hp_ref, state):
        i = pl.program_id(0)

        @pl.when(i == 0)
        def _():
            state[...] = jnp.zeros_like(state)

        halo = halo_ref[...] * (i > 0).astype(F32)
        xin = jnp.concatenate([halo, x_ref[...]], axis=0)
        u = (_conv_rows(xin, cw_ref[...], SSM_CONV) + cb_ref[...])[HALO:]
        h = state[...]
        for k in range(SSM_FWD_SUB):
            rows = slice(k * lc, (k + 1) * lc)
            hp_ref[k * hrows:(k + 1) * hrows, :] = h
            out, h = _ssm_core(u[rows], z_ref[rows, :], dt_ref[rows, :], h, dtb_ref[...], al_ref[...], dk_ref[...],
                               nw_ref[...])
            o_ref[rows, :] = _b(out)
        state[...] = h

    tile = lambda n: pl.BlockSpec((lt, n), _row)
    halo_spec = pl.BlockSpec((HALO, XBC_WIDTH), lambda i: (jnp.maximum(i * (lt // HALO) - 1, 0), 0))
    return _grid_call(
        body, s // lt, (xbc_raw, xbc_raw, z, dt_raw, conv_w, conv_b, dt_bias, a_log, dskip, norm_w), exchange,
        name="ssm_fwd",
        in_specs=[tile(XBC_WIDTH), halo_spec, tile(SSM_WIDTH), tile(DT_PAD)] + _ssm_param_specs(),
        out_specs=[tile(SSM_WIDTH), pl.BlockSpec((SSM_FWD_SUB * hrows, SSM_STATE), _row)],
        out_shape=[jax.ShapeDtypeStruct((s, SSM_WIDTH), BF16),
                   jax.ShapeDtypeStruct((s // lc * hrows, SSM_STATE), F32)],
        scratch_shapes=[pltpu.VMEM((hrows, SSM_STATE), F32)],
        compiler_params=_params())


def ssm_bwd(xbc_raw, z, dt_raw, hprev_all, d_out, conv_w, conv_b, dt_bias, a_log, dskip, norm_w, exchange=None):
    s = xbc_raw.shape[0]
    lc = SSM_CHUNK
    sub = SSM_BWD_SUB
    lt = lc * sub
    nt = s // lt
    hrows = SSM_HEADS * SSM_HEAD_DIM

    def body(x_ref, halo_ref, z_ref, dt_ref, hp_ref, do_ref, cw_ref, cb_ref, dtb_ref, al_ref, dk_ref, nw_ref,
             dx_ref, dz_ref, ddt_ref, dcw_ref, dcb_ref, ddtb_ref, dal_ref, ddk_ref, dnw_ref, dstate, du_next):
        i = pl.program_id(0)
        tile_no = nt - 1 - i

        @pl.when(i == 0)
        def _():
            dstate[...] = jnp.zeros_like(dstate)
            du_next[...] = jnp.zeros_like(du_next)
            for r in (dcw_ref, dcb_ref, ddtb_ref, dal_ref, ddk_ref, dnw_ref):
                r[...] = jnp.zeros_like(r)

        halo = halo_ref[...] * (tile_no > 0).astype(F32)
        xin = jnp.concatenate([halo, x_ref[...]], axis=0)
        cw = cw_ref[...]
        u = (_conv_rows(xin, cw, SSM_CONV) + cb_ref[...])[HALO:]
        dh = dstate[...]
        dus = [None] * sub
        for k in reversed(range(sub)):
            rows = slice(k * lc, (k + 1) * lc)
            _, vjp = jax.vjp(_ssm_core, u[rows], z_ref[rows, :], dt_ref[rows, :], hp_ref[k * hrows:(k + 1) * hrows, :],
                             dtb_ref[...], al_ref[...], dk_ref[...], nw_ref[...])
            dus[k], dz, ddt, dh, ddtb, dal, ddk, dnw = vjp((do_ref[rows, :], dh))
            dz_ref[rows, :] = _b(dz)
            ddt_ref[rows, :] = _b(ddt)
            ddtb_ref[...] += ddtb
            dal_ref[...] += dal
            ddk_ref[...] += ddk
            dnw_ref[...] += dnw
        dstate[...] = dh
        du = jnp.concatenate(dus, axis=0)
        du_ext = jnp.concatenate([du, du_next[...]], axis=0)
        dx_ref[...] = _b(_conv_rows_t(du_ext, cw, SSM_CONV)[:lt])
        du_next[...] = du[:HALO]
        sums = [jnp.sum(du * pltpu.roll(xin, j, axis=0)[HALO:] if j else du * xin[HALO:], axis=0, keepdims=True)
                for j in range(SSM_CONV)]
        dcw_ref[...] += jnp.concatenate(sums[::-1] + [jnp.zeros((8 - SSM_CONV, XBC_WIDTH), F32)], axis=0)
        dcb_ref[...] += jnp.sum(du, axis=0, keepdims=True)

    rev = lambda i: (nt - 1 - i, 0)
    tile = lambda n: pl.BlockSpec((lt, n), rev)
    halo_spec = pl.BlockSpec((HALO, XBC_WIDTH), lambda i: (jnp.maximum((nt - 1 - i) * (lt // HALO) - 1, 0), 0))
    acc = lambda r, n: pl.BlockSpec((r, n), _const)
    return _grid_call(
        body, nt, (xbc_raw, xbc_raw, z, dt_raw, hprev_all, d_out, conv_w, conv_b, dt_bias, a_log, dskip, norm_w),
        exchange, name="ssm_bwd",
        in_specs=[tile(XBC_WIDTH), halo_spec, tile(SSM_WIDTH), tile(DT_PAD),
                  pl.BlockSpec((sub * hrows, SSM_STATE), rev), tile(SSM_WIDTH)] + _ssm_param_specs(),
        out_specs=[tile(XBC_WIDTH), tile(SSM_WIDTH), tile(DT_PAD), acc(8, XBC_WIDTH), acc(1, XBC_WIDTH),
                   acc(1, 128), acc(1, 128), acc(1, 128), acc(1, SSM_WIDTH)],
        out_shape=[jax.ShapeDtypeStruct((s, XBC_WIDTH), BF16), jax.ShapeDtypeStruct((s, SSM_WIDTH), BF16),
                   jax.ShapeDtypeStruct((s, DT_PAD), BF16), jax.ShapeDtypeStruct((8, XBC_WIDTH), F32),
                   jax.ShapeDtypeStruct((1, XBC_WIDTH), F32), jax.ShapeDtypeStruct((1, 128), F32),
                   jax.ShapeDtypeStruct((1, 128), F32), jax.ShapeDtypeStruct((1, 128), F32),
                   jax.ShapeDtypeStruct((1, SSM_WIDTH), F32)],
        scratch_shapes=[pltpu.VMEM((hrows, SSM_STATE), F32), pltpu.VMEM((HALO, XBC_WIDTH), F32)],
        compiler_params=_params())


def mix_out(attn, ssm, x, w_out16, gate1, post_mix_w, pre_ffn_w, scale2, shift2, tm=4 * TOKEN_TILE):
    s = x.shape[0]
    tm = min(tm, s)

    def body(a_ref, s_ref, x_ref, w_ref, g_ref, pw_ref, fw_ref, sc_ref, sh_ref, mixed_ref, x1_ref, h2_ref):
        mixed = (jnp.dot(a_ref[...], w_ref[:ATTN_WIDTH, :], preferred_element_type=F32)
                 + jnp.dot(s_ref[...], w_ref[ATTN_WIDTH:, :], preferred_element_type=F32))
        mixed_ref[...] = _b(mixed)
        x1 = x_ref[...] + g_ref[...] * _rms(mixed, pw_ref[...])
        x1_ref[...] = x1
        h2_ref[...] = _b(_norm_mod(x1, fw_ref[...], sc_ref[...], sh_ref[...]))

    tile = lambda n: pl.BlockSpec((tm, n), _row)
    return pl.pallas_call(
        body, name="mix_out", grid=(s // tm,),
        in_specs=[tile(ATTN_WIDTH), tile(SSM_WIDTH), tile(D_MODEL), _resident((D_MODEL, D_MODEL))]
        + [_vec(D_MODEL)] * 5,
        out_specs=[tile(D_MODEL)] * 3,
        out_shape=[jax.ShapeDtypeStruct((s, D_MODEL), BF16), jax.ShapeDtypeStruct((s, D_MODEL), F32),
                   jax.ShapeDtypeStruct((s, D_MODEL), BF16)],
        compiler_params=_params())(attn, ssm, x, w_out16, gate1, post_mix_w, pre_ffn_w, scale2, shift2)


GELU_K0, GELU_K1 = math.sqrt(2.0 / math.pi), 0.044715


def _gate(ug, uv):
    return jax.nn.gelu(ug, approximate=True) * uv


def _gate_bwd(ug, uv, df):
    sq = ug * ug
    t = jnp.tanh(ug * (GELU_K0 + (GELU_K0 * GELU_K1) * sq))
    half = 0.5 + 0.5 * t
    slope = half + ug * (1.0 - t * t) * (0.5 * GELU_K0 + (1.5 * GELU_K0 * GELU_K1) * sq)
    return df * uv * slope, df * (ug * half)


def up_gate(h2, w_up16, conv_w, conv_b, tm=TOKEN_TILE):
    s = h2.shape[0]

    def body(h_ref, halo_ref, w_ref, cw_ref, cb_ref, u_ref, uraw_ref, f_ref):
        halo = halo_ref[...]
        halo = jnp.where(pl.program_id(0) > 0, halo, jnp.zeros_like(halo))
        hin = jnp.concatenate([halo, h_ref[...]], axis=0)
        for lo in range(0, D_FF, FF_CHUNK):
            halves = []
            for base in (lo, D_FF + lo):
                cols = slice(base, base + FF_CHUNK)
                uraw = jnp.dot(hin, w_ref[:, cols], preferred_element_type=F32)
                uraw_ref[:, cols] = _b(uraw[NEXT:])
                u = (_conv_rows(uraw, cw_ref[:, cols], FFN_CONV) + cb_ref[:, cols])[NEXT:]
                u_ref[:, cols] = u
                halves.append(u)
            f_ref[:, lo:lo + FF_CHUNK] = _b(_gate(*halves))

    tile = lambda n: pl.BlockSpec((tm, n), _row)
    halo_spec = pl.BlockSpec((NEXT, D_MODEL), lambda i: (jnp.maximum(i * (tm // NEXT) - 1, 0), 0))
    return pl.pallas_call(
        body, name="up_gate", grid=(s // tm,),
        in_specs=[tile(D_MODEL), halo_spec, _resident((D_MODEL, 2 * D_FF)),
                  pl.BlockSpec((FFN_CONV, 2 * D_FF), _const), _vec(2 * D_FF)],
        out_specs=[tile(2 * D_FF), tile(2 * D_FF), tile(D_FF)],
        out_shape=[jax.ShapeDtypeStruct((s, 2 * D_FF), F32), jax.ShapeDtypeStruct((s, 2 * D_FF), BF16),
                   jax.ShapeDtypeStruct((s, D_FF), BF16)],
        compiler_params=_params())(h2, h2, w_up16, conv_w, conv_b)


DOWN_LOSS_TILE = 512


def down_loss(f16, w_down16, x1, target, gate2, post_ffn_w, tm=DOWN_LOSS_TILE):
    s = x1.shape[0]
    tm = min(tm, s)

    def body(f_ref, wd_ref, x1_ref, t_ref, g_ref, pw_ref, dffn_ref, dy_ref, loss_ref, dg_ref, dpw_ref, gw_ref):
        i = pl.program_id(0)

        @pl.when(i == 0)
        def _():
            loss_ref[...] = jnp.zeros_like(loss_ref)
            dg_ref[...] = jnp.zeros_like(dg_ref)
            dpw_ref[...] = jnp.zeros_like(dpw_ref)
            gw_ref[...] = jnp.zeros_like(gw_ref)

        ffn = jnp.dot(f_ref[...], wd_ref[...], preferred_element_type=F32)
        x1 = x1_ref[...]
        x2 = x1 + g_ref[...] * _rms(ffn, pw_ref[...])
        err = x2 - t_ref[...]
        dy = err * (1.0 / D_MODEL)
        dy_ref[...] = dy
        loss_ref[...] += 0.5 * jnp.sum(jnp.mean(err * err, axis=-1, keepdims=True))
        dffn, dg, dpw = _gated_rms_bwd(ffn, g_ref[...], pw_ref[...], dy)
        dffn16 = _b(dffn)
        dffn_ref[...] = dffn16
        dg_ref[...] += dg
        dpw_ref[...] += dpw
        gw_ref[...] += _tn(f_ref[...], dffn16)

    tile = lambda n: pl.BlockSpec((tm, n), _row)
    return pl.pallas_call(
        body, name="down_loss", grid=(s // tm,),
        in_specs=[tile(D_FF), _resident((D_FF, D_MODEL)), tile(D_MODEL), tile(D_MODEL), _vec(D_MODEL), _vec(D_MODEL)],
        out_specs=[tile(D_MODEL), tile(D_MODEL), _vec(128), _vec(D_MODEL), _vec(D_MODEL),
                   pl.BlockSpec((D_FF, D_MODEL), _const)],
        out_shape=[jax.ShapeDtypeStruct((s, D_MODEL), BF16), jax.ShapeDtypeStruct((s, D_MODEL), F32),
                   jax.ShapeDtypeStruct((1, 128), F32), jax.ShapeDtypeStruct((1, D_MODEL), F32),
                   jax.ShapeDtypeStruct((1, D_MODEL), F32), jax.ShapeDtypeStruct((D_FF, D_MODEL), F32)],
        compiler_params=_params())(f16, w_down16, x1, target, gate2, post_ffn_w)


BWD_CHUNK = 256


def ffn_bwd(u, u_raw16, d_ffn, conv_w, w_down_t16, w_up_t16, tm=TOKEN_TILE):
    s = u.shape[0]
    nt = s // tm

    def body(u_ref, unext_ref, uraw_ref, d_ref, dnext_ref, cw_ref, wdt_ref, wut_ref,
             du_ref, dh_ref, dcw_ref, dcb_ref):
        i = pl.program_id(0)

        @pl.when(i == 0)
        def _():
            dcw_ref[...] = jnp.zeros_like(dcw_ref)
            dcb_ref[...] = jnp.zeros_like(dcb_ref)

        dnext = dnext_ref[...]
        dnext = jnp.where(i < nt - 1, dnext, jnp.zeros_like(dnext))
        dff = jnp.concatenate([d_ref[...], dnext], axis=0)
        rows_ext = tm + NEXT
        for lo in range(0, D_FF, BWD_CHUNK):
            gcols, vcols = slice(lo, lo + BWD_CHUNK), slice(D_FF + lo, D_FF + lo + BWD_CHUNK)
            ug = jnp.concatenate([u_ref[:, gcols], unext_ref[:, gcols]], axis=0)
            uv = jnp.concatenate([u_ref[:, vcols], unext_ref[:, vcols]], axis=0)
            df = jnp.dot(dff, wdt_ref[:, gcols], preferred_element_type=F32)
            for cols, du in zip((gcols, vcols), _gate_bwd(ug, uv, df)):
                cw = cw_ref[:, cols]
                du1 = pltpu.roll(du, rows_ext - 1, axis=0)
                du2 = pltpu.roll(du, rows_ext - 2, axis=0)
                du_ref[:, cols] = _b((du * cw[2:3, :] + du1 * cw[1:2, :] + du2 * cw[0:1, :])[:tm])
                xr = uraw_ref[:, cols].astype(F32)
                rows = [jnp.sum(xr * d_[:tm], axis=0, keepdims=True) for d_ in (du2, du1, du)]
                dcw_ref[:, cols] += jnp.concatenate(rows + [jnp.zeros((8 - FFN_CONV, BWD_CHUNK), F32)], axis=0)
                dcb_ref[:, cols] += jnp.sum(du[:tm], axis=0, keepdims=True)
        dh_ref[...] = jnp.dot(du_ref[...], wut_ref[...], preferred_element_type=F32)

    tile = lambda n: pl.BlockSpec((tm, n), _row)
    nxt = lambda i: (jnp.minimum((i + 1) * (tm // NEXT), s // NEXT - 1), 0)
    return pl.pallas_call(
        body, name="ffn_bwd", grid=(nt,),
        in_specs=[tile(2 * D_FF), pl.BlockSpec((NEXT, 2 * D_FF), nxt), tile(2 * D_FF), tile(D_MODEL),
                  pl.BlockSpec((NEXT, D_MODEL), nxt), pl.BlockSpec((FFN_CONV, 2 * D_FF), _const),
                  _resident((D_MODEL, D_FF)), _resident((2 * D_FF, D_MODEL))],
        out_specs=[tile(2 * D_FF), tile(D_MODEL), pl.BlockSpec((8, 2 * D_FF), _const), _vec(2 * D_FF)],
        out_shape=[jax.ShapeDtypeStruct((s, 2 * D_FF), BF16), jax.ShapeDtypeStruct((s, D_MODEL), F32),
                   jax.ShapeDtypeStruct((8, 2 * D_FF), F32), jax.ShapeDtypeStruct((1, 2 * D_FF), F32)],
        compiler_params=_params())(u, u, u_raw16, d_ffn, d_ffn, conv_w, w_down_t16, w_up_t16)


def mix_bwd(dh2, x1, dy, mixed, attn, ssm, w_out_t16, pre_ffn_w, scale2, gate1, post_mix_w, tm=2 * TOKEN_TILE):
    s = x1.shape[0]

    def body(dh_ref, x1_ref, dy_ref, mx_ref, a_ref, s_ref, w_ref, fw_ref, sc_ref, g_ref, pw_ref,
             dx1_ref, da_ref, ds_ref, dfw_ref, dsc_ref, dsh_ref, dg_ref, dpw_ref, gw_ref):
        accs = (dfw_ref, dsc_ref, dsh_ref, dg_ref, dpw_ref)

        @pl.when(pl.program_id(0) == 0)
        def _():
            for r in accs + (gw_ref,):
                r[...] = jnp.zeros_like(r)

        dx1, dfw, dsc, dsh = _norm_mod_bwd(x1_ref[...], fw_ref[...], sc_ref[...], dh_ref[...])
        dx1 = dx1 + dy_ref[...]
        dx1_ref[...] = dx1
        dmixed, dg, dpw = _gated_rms_bwd(mx_ref[...].astype(F32), g_ref[...], pw_ref[...], dx1)
        dm16 = _b(dmixed)
        dmix_in = jnp.dot(dm16, w_ref[...], preferred_element_type=F32)
        da_ref[...] = _b(dmix_in[:, :ATTN_WIDTH])
        ds_ref[...] = dmix_in[:, ATTN_WIDTH:]
        gw_ref[:ATTN_WIDTH, :] += _tn(a_ref[...], dm16)
        gw_ref[ATTN_WIDTH:, :] += _tn(s_ref[...], dm16)
        for r, v in zip(accs, (dfw, dsc, dsh, dg, dpw)):
            r[...] += v

    tile = lambda n: pl.BlockSpec((tm, n), _row)
    return pl.pallas_call(
        body, name="mix_bwd", grid=(s // tm,),
        in_specs=[tile(D_MODEL)] * 4 + [tile(ATTN_WIDTH), tile(SSM_WIDTH), _resident((D_MODEL, D_MODEL))]
        + [_vec(D_MODEL)] * 4,
        out_specs=[tile(D_MODEL), tile(ATTN_WIDTH), tile(SSM_WIDTH)] + [_vec(D_MODEL)] * 5
        + [pl.BlockSpec((D_MODEL, D_MODEL), _const)],
        out_shape=[jax.ShapeDtypeStruct((s, D_MODEL), F32), jax.ShapeDtypeStruct((s, ATTN_WIDTH), BF16),
                   jax.ShapeDtypeStruct((s, SSM_WIDTH), F32)]
        + [jax.ShapeDtypeStruct((1, D_MODEL), F32)] * 5 + [jax.ShapeDtypeStruct((D_MODEL, D_MODEL), F32)],
        compiler_params=_params())(dh2, x1, dy, mixed, attn, ssm, w_out_t16, pre_ffn_w, scale2, gate1, post_mix_w)


INPROJ_BWD_TILE = 512


def inproj_bwd(dq, dk, dv, dxbc, dz, ddt, x, dx1, h1, w_in_t16, pre_mix_w, scale1, tm=INPROJ_BWD_TILE):
    s = x.shape[0]
    tm = min(tm, s)

    def body(dq_ref, dk_ref, dv_ref, dxbc_ref, dz_ref, ddt_ref, x_ref, dx1_ref, h_ref, w_ref, pw_ref, sc_ref,
             gx_ref, dpw_ref, dsc_ref, dsh_ref, gw_ref):
        accs = (dpw_ref, dsc_ref, dsh_ref)

        @pl.when(pl.program_id(0) == 0)
        def _():
            for r in accs + (gw_ref,):
                r[...] = jnp.zeros_like(r)

        h16 = h_ref[...]
        dh = None
        dqkv = jnp.concatenate([dq_ref[...], dk_ref[...], dv_ref[...]], axis=1)
        for d16, (lo, hi) in ((dqkv, (0, QKV_W)), (dxbc_ref[:, :SSM_WIDTH], COLS_XS), (dxbc_ref[:, SSM_WIDTH:], COLS_BC),
                              (dz_ref[...], COLS_Z), (ddt_ref[...], COLS_DT)):
            part = jnp.dot(d16, w_ref[lo:hi, :], preferred_element_type=F32)
            dh = part if dh is None else dh + part
            rows = min(hi, IN_PROJ_WIDTH) - lo
            gw_ref[lo:lo + rows, :] += _tn(d16, h16)[:rows]
        dx, dpw, dsc, dsh = _norm_mod_bwd(x_ref[...], pw_ref[...], sc_ref[...], dh)
        gx_ref[...] = dx1_ref[...] + dx
        for r, v in zip(accs, (dpw, dsc, dsh)):
            r[...] += v

    tile = lambda n: pl.BlockSpec((tm, n), _row)
    return pl.pallas_call(
        body, name="inproj_bwd", grid=(s // tm,),
        in_specs=[tile(ATTN_WIDTH), tile(KV_WIDTH), tile(KV_WIDTH), tile(XBC_WIDTH), tile(SSM_WIDTH), tile(DT_PAD),
                  tile(D_MODEL), tile(D_MODEL), tile(D_MODEL), _resident((PROJ_PAD, D_MODEL))] + [_vec(D_MODEL)] * 2,
        out_specs=[tile(D_MODEL)] + [_vec(D_MODEL)] * 3 + [pl.BlockSpec((IN_PROJ_WIDTH, D_MODEL), _const)],
        out_shape=[jax.ShapeDtypeStruct((s, D_MODEL), F32)] + [jax.ShapeDtypeStruct((1, D_MODEL), F32)] * 3
        + [jax.ShapeDtypeStruct((IN_PROJ_WIDTH, D_MODEL), F32)],
        compiler_params=_params())(dq, dk, dv, dxbc, dz, ddt, x, dx1, h1, w_in_t16, pre_mix_w, scale1)


def _adam(g, w, m, v):
    new_m = ADAM_B1 * m + (1.0 - ADAM_B1) * g
    new_v = ADAM_B2 * v + (1.0 - ADAM_B2) * jnp.square(g)
    m_hat = new_m / (1.0 - ADAM_B1 ** ADAM_STEP)
    v_hat = new_v / (1.0 - ADAM_B2 ** ADAM_STEP)
    return -ADAM_LR * (m_hat / (jnp.sqrt(v_hat) + ADAM_EPS) + ADAM_WD * w), new_m, new_v


ROW_PARAMS = (("b_ada", 6144, 6144), ("pre_mix_w", 1024, 1024), ("attn_sinks", 128, 8), ("ssm_conv_b", 1024, 1024),
              ("ssm_dt_bias", 128, 8), ("ssm_a_log", 128, 8), ("ssm_d", 128, 8), ("ssm_norm_w", 512, 512),
              ("post_mix_w", 1024, 1024), ("pre_ffn_w", 1024, 1024), ("ffn_conv_b", 5632, 5632),
              ("post_ffn_w", 1024, 1024))
LOSS_LANES = 128


def adamw_small(row_all, rb_all, rel_bias_wmv, row_wmv):
    n_rows = len(ROW_PARAMS)

    def body(*refs):
        row_ref, rb_ref = refs[:2]
        wmv = refs[2:5 + 3 * n_rows]
        outs = refs[5 + 3 * n_rows:]
        g_row, g_rb = row_ref[0], rb_ref[0]
        for k in range(1, N_DEV):
            g_row = g_row + row_ref[k]
            g_rb = g_rb + rb_ref[k]
        outs[0][...] = g_row[:, :LOSS_LANES]
        grads = [g_rb[:, :N_Q_HEADS]]
        off = LOSS_LANES
        for _, lanes, width in ROW_PARAMS:
            grads.append(g_row[:, off:off + width])
            off += lanes
        for i, g in enumerate(grads):
            w_ref, m_ref, v_ref = wmv[3 * i:3 * i + 3]
            g_out, d_out, m_out, v_out = outs[1 + 4 * i:5 + 4 * i]
            g_out[...] = g
            d_out[...], m_out[...], v_out[...] = _adam(g, w_ref[...], m_ref[...], v_ref[...])

    flat_wmv = list(rel_bias_wmv) + [a for wmv in row_wmv for a in wmv]
    shapes = [jax.ShapeDtypeStruct((1, LOSS_LANES), F32)] + [jax.ShapeDtypeStruct((N_BUCKETS, N_Q_HEADS), F32)] * 4
    for _, _, width in ROW_PARAMS:
        shapes += [jax.ShapeDtypeStruct((1, width), F32)] * 4
    return pl.pallas_call(body, name="adamw_small", out_shape=shapes,
                          compiler_params=_params(n_axes=0))(row_all, rb_all, *flat_wmv)


def adamw(parts, w, m, v, name):
    p, r, n = parts.shape
    tr = _row_tile(r)

    def body(p_ref, w_ref, m_ref, v_ref, g_ref, d_ref, nm_ref, nv_ref):
        g = p_ref[0].astype(F32)
        for k in range(1, p):
            g = g + p_ref[k].astype(F32)
        g_ref[...] = g
        d_ref[...], nm_ref[...], nv_ref[...] = _adam(g, w_ref[...], m_ref[...], v_ref[...])

    tile = pl.BlockSpec((tr, n), _row)
    return pl.pallas_call(
        body, name=name, grid=(r // tr,),
        in_specs=[pl.BlockSpec((p, tr, n), lambda i: (0, i, 0)), tile, tile, tile],
        out_specs=[tile] * 4, out_shape=[jax.ShapeDtypeStruct((r, n), F32)] * 4,
        compiler_params=_params())(parts, w, m, v)


def _bucket_table():
    lq = ATTN_BLOCK
    qi = np.arange(lq)[:, None] + lq
    kj = np.arange(2 * lq)[None, :]
    dist = qi - kj
    d = np.maximum(dist, 0)
    max_exact = N_BUCKETS // 2
    nf = np.maximum(d, 1).astype(np.float32)
    large = max_exact + (np.log(nf / max_exact) / math.log(REL_MAX_DIST / max_exact)
                         * (N_BUCKETS - max_exact)).astype(np.int32)
    large = np.minimum(large, N_BUCKETS - 1)
    bucket = np.where(d < max_exact, d, large).astype(np.int32)
    in_band = (dist >= 0) & (dist < REL_MAX_DIST)
    return np.where(in_band, bucket, -1).astype(np.int32)


def _cols_from_blocks(g):
    return jnp.transpose(g, (1, 0, 2)).reshape(g.shape[1], N_DEV * g.shape[2])


def _cols_to_blocks(a):
    r, n = a.shape
    return jnp.transpose(a.reshape(r, N_DEV, n // N_DEV), (1, 0, 2))


def _pad_in_rows(wt):
    return jnp.concatenate([wt, jnp.zeros((PROJ_PAD - wt.shape[0], wt.shape[1]), wt.dtype)], axis=0)


def _lane_pad(v, n=128):
    return jnp.pad(v, ((0, 0), (0, n - v.shape[1])))


def kernel(x, c, rel_bias, w_ada, b_ada, pre_mix_w, w_in, attn_sinks, ssm_conv_w, ssm_conv_b, ssm_dt_bias, ssm_a_log, ssm_d, ssm_norm_w, w_out, post_mix_w, pre_ffn_w, w_up, ffn_conv_w, ffn_conv_b, w_down, post_ffn_w, loss_target, m_rel_bias, m_w_ada, m_b_ada, m_pre_mix_w, m_w_in, m_attn_sinks, m_ssm_conv_w, m_ssm_conv_b, m_ssm_dt_bias, m_ssm_a_log, m_ssm_d, m_ssm_norm_w, m_w_out, m_post_mix_w, m_pre_ffn_w, m_w_up, m_ffn_conv_w, m_ffn_conv_b, m_w_down, m_post_ffn_w, v_rel_bias, v_w_ada, v_b_ada, v_pre_mix_w, v_w_in, v_attn_sinks, v_ssm_conv_w, v_ssm_conv_b, v_ssm_dt_bias, v_ssm_a_log, v_ssm_d, v_ssm_norm_w, v_w_out, v_post_mix_w, v_pre_ffn_w, v_w_up, v_ffn_conv_w, v_ffn_conv_b, v_w_down, v_post_ffn_w):
    weights = dict(rel_bias=rel_bias, w_ada=w_ada, b_ada=b_ada, pre_mix_w=pre_mix_w, w_in=w_in, attn_sinks=attn_sinks, ssm_conv_w=ssm_conv_w, ssm_conv_b=ssm_conv_b, ssm_dt_bias=ssm_dt_bias, ssm_a_log=ssm_a_log, ssm_d=ssm_d, ssm_norm_w=ssm_norm_w, w_out=w_out, post_mix_w=post_mix_w, pre_ffn_w=pre_ffn_w, w_up=w_up, ffn_conv_w=ffn_conv_w, ffn_conv_b=ffn_conv_b, w_down=w_down, post_ffn_w=post_ffn_w)
    mom_m = dict(rel_bias=m_rel_bias, w_ada=m_w_ada, b_ada=m_b_ada, pre_mix_w=m_pre_mix_w, w_in=m_w_in, attn_sinks=m_attn_sinks, ssm_conv_w=m_ssm_conv_w, ssm_conv_b=m_ssm_conv_b, ssm_dt_bias=m_ssm_dt_bias, ssm_a_log=m_ssm_a_log, ssm_d=m_ssm_d, ssm_norm_w=m_ssm_norm_w, w_out=m_w_out, post_mix_w=m_post_mix_w, pre_ffn_w=m_pre_ffn_w, w_up=m_w_up, ffn_conv_w=m_ffn_conv_w, ffn_conv_b=m_ffn_conv_b, w_down=m_w_down, post_ffn_w=m_post_ffn_w)
    mom_v = dict(rel_bias=v_rel_bias, w_ada=v_w_ada, b_ada=v_b_ada, pre_mix_w=v_pre_mix_w, w_in=v_w_in, attn_sinks=v_attn_sinks, ssm_conv_w=v_ssm_conv_w, ssm_conv_b=v_ssm_conv_b, ssm_dt_bias=v_ssm_dt_bias, ssm_a_log=v_ssm_a_log, ssm_d=v_ssm_d, ssm_norm_w=v_ssm_norm_w, w_out=v_w_out, post_mix_w=v_post_mix_w, pre_ffn_w=v_pre_ffn_w, w_up=v_w_up, ffn_conv_w=v_ffn_conv_w, ffn_conv_b=v_ffn_conv_b, w_down=v_w_down, post_ffn_w=v_post_ffn_w)
    order = ['rel_bias', 'w_ada', 'b_ada', 'pre_mix_w', 'w_in', 'attn_sinks', 'ssm_conv_w', 'ssm_conv_b', 'ssm_dt_bias', 'ssm_a_log', 'ssm_d', 'ssm_norm_w', 'w_out', 'post_mix_w', 'pre_ffn_w', 'w_up', 'ffn_conv_w', 'ffn_conv_b', 'w_down', 'post_ffn_w']

    me = 4 * lax.axis_index("x") + 2 * lax.axis_index("y") + lax.axis_index("c")
    xs_ = x[0]
    target = loss_target[0]

    (w_in_g, scw_g, fcw_g, c_g) = all_gather([_b(w_in[0]).T, ssm_conv_w[0], ffn_conv_w[0], c], "gather_weights")
    w_in_t16 = _pad_in_rows(w_in_g.reshape(IN_PROJ_WIDTH, D_MODEL))
    w_in16 = w_in_t16.T
    ssm_cw = _cols_from_blocks(scw_g)
    ffn_cw = _cols_from_blocks(fcw_g)
    c_all = c_g.reshape(N_DEV, D_MODEL)

    n_cols = w_ada.shape[2]
    b_cols = lax.dynamic_slice(b_ada, (0, me * n_cols), (1, n_cols))
    mod_part = ada_fwd(c_all, w_ada[0], b_cols)
    (mod_rows,) = all_to_all([mod_part.reshape(N_DEV, 1, n_cols)], "scatter_mod")
    mod = mod_rows.reshape(N_MOD, 1, D_MODEL)
    shift1, scale1, gate1, shift2, scale2, gate2 = (mod[i] for i in range(N_MOD))

    bucket_band = jnp.asarray(_bucket_table())
    bias = rel_bias_table(rel_bias, bucket_band)
    sinks_row = _lane_pad(attn_sinks)
    dt_bias, a_log, dskip = _lane_pad(ssm_dt_bias), _lane_pad(ssm_a_log), _lane_pad(ssm_d)

    h1, qkv, xbc_raw, z, dt_raw, w_out_g = pre_mix_inproj(
        xs_, pre_mix_w, scale1, shift1, w_in16, [(_b(w_out[0]), False)])
    attn, w_down_g = attn_fwd(qkv, bias, sinks_row, [(_b(w_down[0]), False)])
    ssm, hprev_all, w_up_g = ssm_fwd(xbc_raw, z, dt_raw, ssm_cw, ssm_conv_b, dt_bias, a_log, dskip, ssm_norm_w,
                                     [(_b(w_up[0]).T, False)])
    w_out16 = w_out_g.reshape(D_MODEL, D_MODEL)
    w_out_t16 = w_out16.T
    w_up_t16 = w_up_g.reshape(2 * D_FF, D_MODEL)
    w_up16 = w_up_t16.T
    w_down16 = w_down_g.reshape(D_FF, D_MODEL)
    w_down_t16 = w_down16.T
    mixed, x1, h2 = mix_out(attn, ssm, xs_, w_out16, gate1, post_mix_w, pre_ffn_w, scale2, shift2)
    u, u_raw16, f16 = up_gate(h2, w_up16, ffn_cw, ffn_conv_b)
    d_ffn, dy, loss_part, d_gate2, d_post_ffn_w, g_w_down = down_loss(f16, w_down16, x1, target, gate2, post_ffn_w)

    du_raw, dh2, d_ffn_cw, d_ffn_cb = ffn_bwd(u, u_raw16, d_ffn, ffn_cw, w_down_t16, w_up_t16)
    g_w_up_t = matmul_tn(du_raw, h2, "grad_w_up", FF_CHUNK, D_MODEL, tk=2048)
    (dx1, d_attn, d_ssm, d_pre_ffn_w, d_scale2, d_shift2, d_gate1, d_post_mix_w, g_w_out) = mix_bwd(
        dh2, x1, dy, mixed, attn, ssm, w_out_t16, pre_ffn_w, scale2, gate1, post_mix_w)
    dq, dk, dv, dbias, dsinks, p_w_down = attn_bwd(
        qkv, bias, sinks_row, d_attn, [(g_w_down.reshape(N_DEV, D_FF // N_DEV, D_MODEL), True)])
    d_rel_bias = rel_bias_grad(dbias, bucket_band)
    (dxbc, dz, ddt, d_ssm_cw, d_ssm_cb, d_dt_bias, d_a_log, d_dskip, d_norm_w, p_w_up, p_w_out) = ssm_bwd(
        xbc_raw, z, dt_raw, hprev_all, d_ssm, ssm_cw, ssm_conv_b, dt_bias, a_log, dskip, ssm_norm_w,
        [(g_w_up_t.reshape(N_DEV, 2 * D_FF // N_DEV, D_MODEL), True),
         (g_w_out.reshape(N_DEV, D_MODEL // N_DEV, D_MODEL), True)])
    grad_x, d_pre_mix_w, d_scale1, d_shift1, g_w_in_perm = inproj_bwd(
        dq, dk, dv, dxbc, dz, ddt, xs_, dx1, h1, w_in_t16, pre_mix_w, scale1)
    g_w_in_t = g_w_in_perm

    d_mod = jnp.concatenate([d_shift1, d_scale1, d_gate1, d_shift2, d_scale2, d_gate2], axis=1)
    late = ("w_in", "ssm_conv_w", "ffn_conv_w")
    full = [g_w_in_t.reshape(N_DEV, IN_PROJ_WIDTH // N_DEV, D_MODEL), _cols_to_blocks(d_ssm_cw[:SSM_CONV]),
            _cols_to_blocks(d_ffn_cw[:FFN_CONV])]
    core = lax.axis_index("c").astype(jnp.int32).reshape(1)
    got = pair_exchange(full, "pair_grads")
    chip_sums = [pair_sum(f_, g_, core, "pair_sum_" + k) for k, f_, g_ in zip(late, full, got)]
    chip_parts = all_to_all(chip_sums, "scatter_grads", CHIP_FLIPS, _chip_index)

    row_g = dict(b_ada=d_mod, pre_mix_w=d_pre_mix_w, attn_sinks=dsinks, ssm_conv_b=d_ssm_cb, ssm_dt_bias=d_dt_bias,
                 ssm_a_log=d_a_log, ssm_d=d_dskip, ssm_norm_w=d_norm_w, post_mix_w=d_post_mix_w,
                 pre_ffn_w=d_pre_ffn_w, ffn_conv_b=d_ffn_cb, post_ffn_w=d_post_ffn_w)
    row = jnp.concatenate([loss_part] + [row_g[k] for k, _, _ in ROW_PARAMS], axis=1)
    row_all, rb_all = all_gather([row, d_rel_bias], "gather_small")
    d_mod_cols = lax.dynamic_slice(row_all.reshape(N_DEV, row.shape[1]), (0, LOSS_LANES + me * n_cols),
                                   (N_DEV, n_cols))
    g_w_ada = ada_bwd(c_all, d_mod_cols)

    wmv = lambda k: (weights[k], mom_m[k], mom_v[k])
    small = adamw_small(row_all, rb_all, wmv("rel_bias"), [wmv(k) for k, _, _ in ROW_PARAMS])
    loss = small[0][0, 0]
    res = {k: tuple(small[1 + 4 * i:5 + 4 * i]) for i, k in enumerate(["rel_bias"] + [k for k, _, _ in ROW_PARAMS])}
    big = list(zip(late, chip_parts)) + [("w_down", p_w_down), ("w_up", p_w_up), ("w_out", p_w_out),
                                        ("w_ada", g_w_ada[None])]
    for k, parts in big:
        if k in ("w_in", "w_up"):
            outs_t = adamw(parts, weights[k][0].T, mom_m[k][0].T, mom_v[k][0].T, "adamw_" + k)
            res[k] = tuple(o.T[None] for o in outs_t)
        else:
            res[k] = tuple(o[None] for o in adamw(parts, weights[k][0], mom_m[k][0], mom_v[k][0], "adamw_" + k))

    outs = [loss, grad_x[None]]
    for field in range(4):
        outs += [res[k][field] for k in order]
    return tuple(outs)
```

```python
import math

import numpy as np
import jax
import jax.numpy as jnp
from jax import lax
from jax.experimental import pallas as pl
from jax.experimental.pallas import tpu as pltpu

F32 = jnp.float32
BF16 = jnp.bfloat16
MESH_ID = pl.DeviceIdType.MESH

N_DEV = 8
D_MODEL = 1024
N_Q_HEADS = 8
N_KV_HEADS = 2
HEAD_DIM = 64
ATTN_WIDTH = 512
KV_WIDTH = 128
ATTN_BLOCK = 128
N_BUCKETS = 32
REL_MAX_DIST = 128
SSM_HEADS = 8
SSM_HEAD_DIM = 64
SSM_WIDTH = 512
SSM_STATE = 128
SSM_GROUPS = 2
SSM_BC = 256
SSM_CONV = 4
SSM_CHUNK = 256
XBC_WIDTH = SSM_WIDTH + 2 * SSM_BC
D_FF = 2816
FFN_CONV = 3
NORM_EPS = 1e-6
N_MOD = 6
IN_PROJ_WIDTH = 2312
QKV_W = ATTN_WIDTH + 2 * KV_WIDTH
COLS_XS = (QKV_W, QKV_W + SSM_WIDTH)
COLS_Z = (COLS_XS[1], COLS_XS[1] + SSM_WIDTH)
COLS_BC = (COLS_Z[1], COLS_Z[1] + 2 * SSM_BC)
DT_PAD = 128
COLS_DT = (COLS_BC[1], COLS_BC[1] + DT_PAD)
PROJ_PAD = COLS_DT[1]
FF_CHUNK = 1408

ADAM_LR = 0.001
ADAM_B1 = 0.9
ADAM_B2 = 0.999
ADAM_EPS = 1e-08
ADAM_WD = 0.01
ADAM_STEP = 10

TOKEN_TILE = 256
HALO = 8
NEXT = 16
VMEM_LIMIT = 56 * 1024 * 1024


def _params(vmem=VMEM_LIMIT, n_axes=1):
    return pltpu.CompilerParams(dimension_semantics=("arbitrary",) * n_axes, vmem_limit_bytes=vmem)


def _b(x):
    return x.astype(BF16)


def _nn(a, b):
    return jnp.dot(_b(a), _b(b), preferred_element_type=F32)


def _nt(a, b):
    return lax.dot_general(_b(a), _b(b), (((1,), (1,)), ((), ())), preferred_element_type=F32)


def _tn(a, b):
    return lax.dot_general(_b(a), _b(b), (((0,), (0,)), ((), ())), preferred_element_type=F32)


@jax.custom_vjp
def mm(a, b):
    return _nn(a, b)


mm.defvjp(lambda a, b: (_nn(a, b), (a, b)),
          lambda r, g: (_nt(g, r[1]).astype(r[0].dtype), _tn(r[0], g).astype(r[1].dtype)))


@jax.custom_vjp
def mm_nt(a, b):
    return _nt(a, b)


mm_nt.defvjp(lambda a, b: (_nt(a, b), (a, b)),
             lambda r, g: (_nn(g, r[1]).astype(r[0].dtype), _tn(g, r[0]).astype(r[1].dtype)))


@jax.custom_vjp
def mm_tn(a, b):
    return _tn(a, b)


mm_tn.defvjp(lambda a, b: (_tn(a, b), (a, b)),
             lambda r, g: (_nt(r[1], g).astype(r[0].dtype), _nn(r[0], g).astype(r[1].dtype)))


def _rms(x, w):
    return x * lax.rsqrt(jnp.mean(x * x, axis=-1, keepdims=True) + NORM_EPS) * w


def _norm_mod(x, w, scale, shift):
    return _rms(x, w) * (1.0 + scale) + shift


def _rms_bwd(x, w, dy):
    r = lax.rsqrt(jnp.mean(x * x, axis=-1, keepdims=True) + NORM_EPS)
    xhat = x * r
    g = dy * w
    dx = r * (g - xhat * jnp.mean(g * xhat, axis=-1, keepdims=True))
    return dx, jnp.sum(dy * xhat, axis=0, keepdims=True)


def _norm_mod_bwd(x, w, scale, dh):
    dx, da = _rms_bwd(x, w * (1.0 + scale), dh)
    return dx, da * (1.0 + scale), da * w, jnp.sum(dh, axis=0, keepdims=True)


def _gated_rms_bwd(m, gate, w, dy):
    dm, t = _rms_bwd(m, w * gate, dy)
    return dm, t * w, t * gate


def _silu(x):
    return x * jax.nn.sigmoid(x)


def _conv_rows(xin, w, k):
    acc = xin * w[k - 1:k, :]
    for j in range(1, k):
        acc = acc + pltpu.roll(xin, j, axis=0) * w[k - 1 - j:k - j, :]
    return acc


def _conv_rows_t(du, w, k):
    n = du.shape[0]
    acc = du * w[k - 1:k, :]
    for j in range(1, k):
        acc = acc + pltpu.roll(du, n - j, axis=0) * w[k - 1 - j:k - j, :]
    return acc


def _row(i):
    return (i, 0)


def _const(i):
    return (0, 0)


def _vec(n):
    return pl.BlockSpec((1, n), _const)


def _resident(shape):
    return pl.BlockSpec(shape, _const, pipeline_mode=pl.Buffered(1))


def _block_index(p):
    return 4 * p[0] + 2 * p[1] + p[2]


def all_gather(arrs, name):
    n = len(arrs)

    def body(*refs):
        ins, outs = refs[:n], refs[n:2 * n]
        send_sems, recv_sems, local_sems = refs[2 * n:]
        x, y, c = lax.axis_index("x"), lax.axis_index("y"), lax.axis_index("c")
        me, sibling = (x, y, c), (x, y, 1 - c)
        chips = [(1 - x, y), (x, 1 - y), (1 - x, 1 - y)]

        def copy(a, k, block, to, src=None):
            dst = outs[a].at[_block_index(block)]
            return pltpu.make_async_remote_copy(
                src_ref=dst if src is None else src, dst_ref=dst,
                send_sem=send_sems.at[a * 7 + k], recv_sem=recv_sems.at[a * 7 + k],
                device_id=to, device_id_type=MESH_ID)

        mine = [pltpu.make_async_copy(ins[a], outs[a].at[_block_index(me)], local_sems.at[a]) for a in range(n)]
        for cp in mine:
            cp.start()
        first = []
        for a in range(n):
            first.append(copy(a, 0, me, sibling, src=ins[a]))
            first += [copy(a, 1 + j, me, (*chip, c), src=ins[a]) for j, chip in enumerate(chips)]
        for cp in first:
            cp.start()
        passed = []
        for j, chip in enumerate(chips):
            for a in range(n):
                copy(a, 1 + j, (*chip, c), me).wait_recv()
                cp = copy(a, 4 + j, (*chip, c), sibling)
                cp.start()
                passed.append(cp)
        for a in range(n):
            copy(a, 0, sibling, me).wait_recv()
            for j, chip in enumerate(chips):
                copy(a, 4 + j, (*chip, 1 - c), me).wait_recv()
        for cp in first + passed:
            cp.wait_send()
        for cp in mine:
            cp.wait()

    any_spec = pl.BlockSpec(memory_space=pl.ANY)
    return pl.pallas_call(
        body, name=name,
        out_shape=[jax.ShapeDtypeStruct((N_DEV,) + a.shape, a.dtype) for a in arrs],
        in_specs=[any_spec] * n, out_specs=[any_spec] * n,
        scratch_shapes=[pltpu.SemaphoreType.DMA((7 * n,)), pltpu.SemaphoreType.DMA((7 * n,)),
                        pltpu.SemaphoreType.DMA((n,))],
    )(*arrs)


ALL_FLIPS = ((0, 0, 1), (0, 1, 0), (0, 1, 1), (1, 0, 0), (1, 0, 1), (1, 1, 0), (1, 1, 1))
CHIP_FLIPS = ((0, 1, 0), (1, 0, 0), (1, 1, 0))


def _chip_index(p):
    return 2 * p[0] + p[1]


def all_to_all(arrs, name, flips=ALL_FLIPS, index=_block_index):
    n = len(arrs)
    nf = len(flips)

    def body(*refs):
        ins, outs = refs[:n], refs[n:2 * n]
        send_sems, recv_sems, local_sems = refs[2 * n:]
        pos = (lax.axis_index("x"), lax.axis_index("y"), lax.axis_index("c"))
        me = index(pos)
        peers = [tuple(1 - p if f else p for p, f in zip(pos, flip)) for flip in flips]

        def copy(a, k):
            peer = peers[k]
            return pltpu.make_async_remote_copy(
                src_ref=ins[a].at[index(peer)], dst_ref=outs[a].at[me],
                send_sem=send_sems.at[a * nf + k], recv_sem=recv_sems.at[a * nf + k],
                device_id=peer, device_id_type=MESH_ID)

        def landed(a, k):
            slot = outs[a].at[index(peers[k])]
            return pltpu.make_async_remote_copy(
                src_ref=slot, dst_ref=slot,
                send_sem=send_sems.at[a * nf + k], recv_sem=recv_sems.at[a * nf + k],
                device_id=peers[k], device_id_type=MESH_ID)

        mine = [pltpu.make_async_copy(ins[a].at[me], outs[a].at[me], local_sems.at[a]) for a in range(n)]
        for cp in mine:
            cp.start()
        sent = [copy(a, k) for a in range(n) for k in range(nf)]
        for cp in sent:
            cp.start()
        for a in range(n):
            for k in range(nf):
                landed(a, k).wait_recv()
        for cp in sent:
            cp.wait_send()
        for cp in mine:
            cp.wait()

    any_spec = pl.BlockSpec(memory_space=pl.ANY)
    return pl.pallas_call(
        body, name=name,
        out_shape=[jax.ShapeDtypeStruct(a.shape, a.dtype) for a in arrs],
        in_specs=[any_spec] * n, out_specs=[any_spec] * n,
        scratch_shapes=[pltpu.SemaphoreType.DMA((nf * n,)), pltpu.SemaphoreType.DMA((nf * n,)),
                        pltpu.SemaphoreType.DMA((n,))],
    )(*arrs)


def _direct_exchange(src, dst, sems, scatter):
    send_sems, recv_sems, local_sem = sems
    pos = (lax.axis_index("x"), lax.axis_index("y"), lax.axis_index("c"))
    me = _block_index(pos)
    peers = [tuple(1 - p if f else p for p, f in zip(pos, flip)) for flip in ALL_FLIPS]

    def outgoing(k):
        return pltpu.make_async_remote_copy(
            src_ref=src.at[_block_index(peers[k])] if scatter else src, dst_ref=dst.at[me],
            send_sem=send_sems.at[k], recv_sem=recv_sems.at[k], device_id=peers[k], device_id_type=MESH_ID)

    def incoming(k):
        slot = dst.at[_block_index(peers[k])]
        return pltpu.make_async_remote_copy(
            src_ref=slot, dst_ref=slot, send_sem=send_sems.at[k], recv_sem=recv_sems.at[k],
            device_id=peers[k], device_id_type=MESH_ID)

    def local():
        return pltpu.make_async_copy(src.at[me] if scatter else src, dst.at[me], local_sem)

    def start():
        local().start()
        for k in range(len(ALL_FLIPS)):
            outgoing(k).start()

    def finish():
        for k in range(len(ALL_FLIPS)):
            incoming(k).wait_recv()
        for k in range(len(ALL_FLIPS)):
            outgoing(k).wait_send()
        local().wait()

    return start, finish


def hosted_call(body, exchanges, steps, n_in, n_out, **call):
    n_ex = len(exchanges)

    def wrapped(*refs):
        ins, srcs = refs[:n_in], refs[n_in:n_in + n_ex]
        outs = refs[n_in + n_ex:n_in + n_ex + n_out]
        dsts = refs[n_in + n_ex + n_out:n_in + 2 * n_ex + n_out]
        rest = refs[n_in + 2 * n_ex + n_out:]
        scratch, sems = rest[:len(rest) - 3 * n_ex], rest[len(rest) - 3 * n_ex:]
        plans = [_direct_exchange(srcs[e], dsts[e], sems[3 * e:3 * e + 3], exchanges[e][1]) for e in range(n_ex)]

        @pl.when(pl.program_id(0) == 0)
        def _():
            for start, _ in plans:
                start()

        body(*ins, *outs, *scratch)

        @pl.when(pl.program_id(0) == steps - 1)
        def _():
            for _, finish in plans:
                finish()

    any_spec = pl.BlockSpec(memory_space=pl.ANY)
    landings = [jax.ShapeDtypeStruct(src.shape if scatter else (N_DEV,) + src.shape, src.dtype)
                for src, scatter in exchanges]
    n_flips = len(ALL_FLIPS)
    sems = [pltpu.SemaphoreType.DMA((n_flips,)), pltpu.SemaphoreType.DMA((n_flips,)), pltpu.SemaphoreType.DMA(())]
    return pl.pallas_call(
        wrapped, grid=(steps,),
        in_specs=list(call.pop("in_specs")) + [any_spec] * n_ex,
        out_specs=list(call.pop("out_specs")) + [any_spec] * n_ex,
        out_shape=list(call.pop("out_shape")) + landings,
        scratch_shapes=list(call.pop("scratch_shapes", [])) + sems * n_ex,
        **call)


def _grid_call(body, steps, args, exchanges, **call):
    if not exchanges:
        return pl.pallas_call(body, grid=(steps,), **call)(*args)
    srcs = [src for src, _ in exchanges]
    return hosted_call(body, exchanges, steps, len(args), len(call["out_shape"]), **call)(*args, *srcs)


N_CHIPS = 4


def pair_exchange(arrs, name):
    n = len(arrs)

    def body(*refs):
        ins, outs = refs[:n], refs[n:2 * n]
        send_sems, recv_sems = refs[2 * n:]
        x, y, c = lax.axis_index("x"), lax.axis_index("y"), lax.axis_index("c")
        sibling = (x, y, 1 - c)
        sent = []
        for a in range(n):
            for q in range(N_CHIPS):
                cp = pltpu.make_async_remote_copy(
                    src_ref=ins[a].at[2 * q + (1 - c)], dst_ref=outs[a].at[q],
                    send_sem=send_sems.at[a * N_CHIPS + q], recv_sem=recv_sems.at[a * N_CHIPS + q],
                    device_id=sibling, device_id_type=MESH_ID)
                cp.start()
                sent.append(cp)
        for cp in sent:
            cp.wait_recv()
        for cp in sent:
            cp.wait_send()

    any_spec = pl.BlockSpec(memory_space=pl.ANY)
    return pl.pallas_call(
        body, name=name,
        out_shape=[jax.ShapeDtypeStruct((N_CHIPS,) + a.shape[1:], a.dtype) for a in arrs],
        in_specs=[any_spec] * n, out_specs=[any_spec] * n,
        scratch_shapes=[pltpu.SemaphoreType.DMA((N_CHIPS * n,)), pltpu.SemaphoreType.DMA((N_CHIPS * n,))],
    )(*arrs)


def pair_sum(full, got, core, name):
    _, r, n = full.shape
    tr = _row_tile(r)

    def body(c_ref, mine_ref, got_ref, o_ref):
        o_ref[...] = _b(mine_ref[...] + got_ref[...])

    grid_spec = pltpu.PrefetchScalarGridSpec(
        num_scalar_prefetch=1, grid=(N_CHIPS, r // tr),
        in_specs=[pl.BlockSpec((1, tr, n), lambda q, i, c_ref: (2 * q + c_ref[0], i, 0)),
                  pl.BlockSpec((1, tr, n), lambda q, i, c_ref: (q, i, 0))],
        out_specs=pl.BlockSpec((1, tr, n), lambda q, i, c_ref: (q, i, 0)))
    return pl.pallas_call(body, name=name, grid_spec=grid_spec,
                          out_shape=jax.ShapeDtypeStruct((N_CHIPS, r, n), BF16),
                          compiler_params=_params(n_axes=2))(core, full, got)


def _row_tile(r):
    for cand in (256, 128, 64, 32, 16):
        if r % cand == 0 and r > cand:
            return cand
    return r


def ada_fwd(c_all, w_ada, b_cols):
    def body(c_ref, w_ref, b_ref, o_ref):
        o_ref[...] = _nn(_silu(c_ref[...]), w_ref[...]) + b_ref[...]

    return pl.pallas_call(body, name="ada_fwd",
                          out_shape=jax.ShapeDtypeStruct((N_DEV, w_ada.shape[1]), F32),
                          compiler_params=_params(n_axes=0))(c_all, w_ada, b_cols)


def ada_bwd(c_all, g_cols):
    def body(c_ref, g_ref, o_ref):
        o_ref[...] = _tn(_silu(c_ref[...]), g_ref[...])

    return pl.pallas_call(body, name="ada_bwd",
                          out_shape=jax.ShapeDtypeStruct((c_all.shape[1], g_cols.shape[1]), F32),
                          compiler_params=_params(n_axes=0))(c_all, g_cols)


def matmul_tn(a, b, name, bm, bn, tk=512):
    s, m = a.shape
    n = b.shape[1]
    tk = min(tk, s)

    def body(a_ref, b_ref, o_ref):
        @pl.when(pl.program_id(2) == 0)
        def _():
            o_ref[...] = jnp.zeros_like(o_ref)

        o_ref[...] += _tn(a_ref[...], b_ref[...])

    return pl.pallas_call(
        body, name=name, grid=(m // bm, n // bn, s // tk),
        in_specs=[pl.BlockSpec((tk, bm), lambda i, j, k: (k, i)), pl.BlockSpec((tk, bn), lambda i, j, k: (k, j))],
        out_specs=pl.BlockSpec((bm, bn), lambda i, j, k: (i, j)),
        out_shape=jax.ShapeDtypeStruct((m, n), F32),
        compiler_params=_params(n_axes=3))(a, b)


def pre_mix_inproj(x, w, scale, shift, w_in16, exchange=None, tm=4 * TOKEN_TILE):
    s = x.shape[0]
    tm = min(tm, s)

    def body(x_ref, w_ref, sc_ref, sh_ref, win_ref, h_ref, qkv_ref, xbc_ref, z_ref, dt_ref):
        h16 = _b(_norm_mod(x_ref[...], w_ref[...], sc_ref[...], sh_ref[...]))
        h_ref[...] = h16
        dot = lambda lo, hi: jnp.dot(h16, win_ref[:, lo:hi], preferred_element_type=F32)
        qkv_ref[...] = _b(dot(0, QKV_W))
        xbc_ref[:, :SSM_WIDTH] = dot(*COLS_XS)
        xbc_ref[:, SSM_WIDTH:] = dot(*COLS_BC)
        z_ref[...] = dot(*COLS_Z)
        dt_ref[...] = dot(*COLS_DT)

    tile = lambda n: pl.BlockSpec((tm, n), _row)
    return _grid_call(
        body, s // tm, (x, w, scale, shift, w_in16), exchange, name="pre_mix_inproj",
        in_specs=[tile(D_MODEL), _vec(D_MODEL), _vec(D_MODEL), _vec(D_MODEL), _resident((D_MODEL, PROJ_PAD))],
        out_specs=[tile(D_MODEL), tile(QKV_W), tile(XBC_WIDTH), tile(SSM_WIDTH), tile(DT_PAD)],
        out_shape=[jax.ShapeDtypeStruct((s, D_MODEL), BF16), jax.ShapeDtypeStruct((s, QKV_W), BF16),
                   jax.ShapeDtypeStruct((s, XBC_WIDTH), F32), jax.ShapeDtypeStruct((s, SSM_WIDTH), F32),
                   jax.ShapeDtypeStruct((s, DT_PAD), F32)],
        compiler_params=_params())


ATTN_QB_FWD, ATTN_QB_BWD = 4, 4


def _attn_tile(q, kp, kc, vp, vc, bias, sinks):
    lq = ATTN_BLOCK
    group = N_Q_HEADS // N_KV_HEADS
    lanes = lax.broadcasted_iota(jnp.int32, (1, 128), 1)
    rid = lax.broadcasted_iota(jnp.int32, (group * lq, 1), 0)
    sink_cols = []
    for hk in range(N_KV_HEADS):
        sink = jnp.zeros((group * lq, 1), F32)
        for g in range(group):
            s_h = jnp.sum(jnp.where(lanes == hk * group + g, sinks, 0.0), axis=-1, keepdims=True)
            sink = jnp.where((rid >= g * lq) & (rid < (g + 1) * lq), s_h, sink)
        sink_cols.append(sink)
    kall = jnp.concatenate([kp, kc], axis=0)
    vall = jnp.concatenate([vp, vc], axis=0)
    blocks = []
    for b in range(q.shape[0] // lq):
        qb = q[b * lq:(b + 1) * lq]
        outs = []
        for hk in range(N_KV_HEADS):
            cols = slice(hk * HEAD_DIM, (hk + 1) * HEAD_DIM)
            kb = kall[b * lq:(b + 2) * lq, cols]
            vb = vall[b * lq:(b + 2) * lq, cols]
            qg = jnp.concatenate([qb[:, (hk * group + g) * HEAD_DIM:(hk * group + g + 1) * HEAD_DIM]
                                  for g in range(group)], axis=0)
            sc = mm_nt(qg, kb) * (HEAD_DIM ** -0.5) + bias[b][hk]
            sink = sink_cols[hk]
            m = lax.stop_gradient(jnp.maximum(jnp.max(sc, axis=-1, keepdims=True), sink))
            p = jnp.exp(sc - m)
            probs = p / (jnp.sum(p, axis=-1, keepdims=True) + jnp.exp(sink - m))
            og = mm(probs, vb)
            outs += [og[g * lq:(g + 1) * lq] for g in range(group)]
        blocks.append(jnp.concatenate(outs, axis=1))
    return jnp.concatenate(blocks, axis=0)


def _attn_tile_bwd(q, kp, kc, vp, vc, bias, sinks, do):
    lq = ATTN_BLOCK
    group = N_Q_HEADS // N_KV_HEADS
    scale = HEAD_DIM ** -0.5
    lanes = lax.broadcasted_iota(jnp.int32, (1, 128), 1)
    rid = lax.broadcasted_iota(jnp.int32, (group * lq, 1), 0)
    sink_cols = []
    for hk in range(N_KV_HEADS):
        sink = jnp.zeros((group * lq, 1), F32)
        for g in range(group):
            s_h = jnp.sum(jnp.where(lanes == hk * group + g, sinks, 0.0), axis=-1, keepdims=True)
            sink = jnp.where((rid >= g * lq) & (rid < (g + 1) * lq), s_h, sink)
        sink_cols.append(sink)
    kall = jnp.concatenate([kp, kc], axis=0)
    vall = jnp.concatenate([vp, vc], axis=0)
    dsk = jnp.zeros((1, 128), F32)
    dq_blocks, dbias = [], []
    nqb = q.shape[0] // lq
    dk_parts = [[None] * nqb for _ in range(N_KV_HEADS)]
    dv_parts = [[None] * nqb for _ in range(N_KV_HEADS)]
    for b in range(nqb):
        qb, dob = q[b * lq:(b + 1) * lq], do[b * lq:(b + 1) * lq]
        dq_heads, dbias_b = [], []
        for hk in range(N_KV_HEADS):
            cols = slice(hk * HEAD_DIM, (hk + 1) * HEAD_DIM)
            kb = kall[b * lq:(b + 2) * lq, cols]
            vb = vall[b * lq:(b + 2) * lq, cols]
            heads = [hk * group + g for g in range(group)]
            qg = jnp.concatenate([qb[:, h * HEAD_DIM:(h + 1) * HEAD_DIM] for h in heads], axis=0)
            dog = jnp.concatenate([dob[:, h * HEAD_DIM:(h + 1) * HEAD_DIM] for h in heads], axis=0)
            sink = sink_cols[hk]
            sc = _nt(qg, kb) * scale + bias[b][hk]
            m = jnp.maximum(jnp.max(sc, axis=-1, keepdims=True), sink)
            p = jnp.exp(sc - m)
            es = jnp.exp(sink - m)
            inv = 1.0 / (jnp.sum(p, axis=-1, keepdims=True) + es)
            probs = p * inv
            dprobs = _nt(dog, vb)
            delta = jnp.sum(probs * dprobs, axis=-1, keepdims=True)
            dsc = probs * (dprobs - delta)
            dbias_b.append(dsc)
            dsink = -(es * inv) * delta
            for g, h in enumerate(heads):
                tot = jnp.sum(dsink[g * lq:(g + 1) * lq], axis=0, keepdims=True)
                dsk = dsk + jnp.where(lanes == h, tot, 0.0)
            dqg = _nn(dsc, kb) * scale
            dq_heads += [dqg[g * lq:(g + 1) * lq] for g in range(group)]
            dk_parts[hk][b] = _tn(dsc, qg) * scale
            dv_parts[hk][b] = _tn(probs, dog)
        dq_blocks.append(jnp.concatenate(dq_heads, axis=1))
        dbias.append(dbias_b)

    def overlap_add(parts):
        chunks = []
        for r in range(nqb + 1):
            acc = None
            if r < nqb:
                acc = parts[r][:lq]
            if r >= 1:
                tail = parts[r - 1][lq:]
                acc = tail if acc is None else acc + tail
            chunks.append(acc)
        return jnp.concatenate(chunks, axis=0)

    dkall = jnp.concatenate([overlap_add(dk_parts[hk]) for hk in range(N_KV_HEADS)], axis=1)
    dvall = jnp.concatenate([overlap_add(dv_parts[hk]) for hk in range(N_KV_HEADS)], axis=1)
    return jnp.concatenate(dq_blocks, axis=0), dkall, dvall, dbias, dsk


def _attn_in_specs(nt, clamp, nqb):
    lq, tq = ATTN_BLOCK, ATTN_BLOCK * nqb
    cur = lambda n: jnp.minimum(n, nt - 1) if clamp else n
    prev = lambda n: jnp.maximum(cur(n) * nqb - 1, 0)
    kcol, vcol = ATTN_WIDTH // KV_WIDTH, ATTN_WIDTH // KV_WIDTH + 1
    return [pl.BlockSpec((tq, ATTN_WIDTH), lambda n: (cur(n), 0)),
            pl.BlockSpec((lq, KV_WIDTH), lambda n: (prev(n), kcol)),
            pl.BlockSpec((tq, KV_WIDTH), lambda n: (cur(n), kcol)),
            pl.BlockSpec((lq, KV_WIDTH), lambda n: (prev(n), vcol)),
            pl.BlockSpec((tq, KV_WIDTH), lambda n: (cur(n), vcol)),
            pl.BlockSpec((2, N_KV_HEADS, 4 * lq, 2 * lq), lambda n: (0, 0, 0, 0)),
            _vec(128)]


def _tile_bias(bias_ref, first, nqb):
    return [[jnp.where(first, bias_ref[1, hk], bias_ref[0, hk]) if b == 0 else bias_ref[0, hk]
             for hk in range(N_KV_HEADS)] for b in range(nqb)]


def attn_fwd(qkv, bias, sinks_rows, exchange=None):
    s = qkv.shape[0]
    nqb = min(ATTN_QB_FWD, s // ATTN_BLOCK)
    tq = ATTN_BLOCK * nqb
    nt = s // tq

    def body(q_ref, kp_ref, kc_ref, vp_ref, vc_ref, bias_ref, sk_ref, o_ref):
        f = lambda r: r[...].astype(F32)
        o = _attn_tile(f(q_ref), f(kp_ref), f(kc_ref), f(vp_ref), f(vc_ref),
                       _tile_bias(bias_ref, pl.program_id(0) == 0, nqb), sk_ref[...])
        o_ref[...] = _b(o)

    return _grid_call(
        body, nt, (qkv, qkv, qkv, qkv, qkv, bias, sinks_rows), exchange, name="attn_fwd",
        in_specs=_attn_in_specs(nt, False, nqb),
        out_specs=[pl.BlockSpec((tq, ATTN_WIDTH), _row)],
        out_shape=[jax.ShapeDtypeStruct((s, ATTN_WIDTH), BF16)],
        compiler_params=_params())


def attn_bwd(qkv, bias, sinks_rows, d_attn, exchange=None):
    s = qkv.shape[0]
    nqb = ATTN_QB_BWD
    lq, tq = ATTN_BLOCK, ATTN_BLOCK * nqb
    nt = s // tq

    def body(q_ref, kp_ref, kc_ref, vp_ref, vc_ref, bias_ref, sk_ref, do_ref,
             dq_ref, dk_ref, dv_ref, dbias_ref, dsk_ref, carry_k, carry_v):
        n = pl.program_id(0)

        @pl.when(n == 0)
        def _():
            dbias_ref[...] = jnp.zeros_like(dbias_ref)
            dsk_ref[...] = jnp.zeros_like(dsk_ref)
            carry_k[...] = jnp.zeros_like(carry_k)
            carry_v[...] = jnp.zeros_like(carry_v)

        @pl.when(n < nt)
        def _():
            f = lambda r: r[...].astype(F32)
            dq, dkall, dvall, dbias, dsk = _attn_tile_bwd(
                f(q_ref), f(kp_ref), f(kc_ref), f(vp_ref), f(vc_ref), _tile_bias(bias_ref, n == 0, nqb), sk_ref[...],
                f(do_ref))
            dkp, dkc, dvp, dvc = dkall[:lq], dkall[lq:], dvall[:lq], dvall[lq:]
            dq_ref[...] = _b(dq)
            done = tq - lq
            dk_ref[:done, :] = _b(carry_k[:done, :])
            dv_ref[:done, :] = _b(carry_v[:done, :])
            dk_ref[done:, :] = _b(carry_k[done:, :] + dkp)
            dv_ref[done:, :] = _b(carry_v[done:, :] + dvp)
            carry_k[...] = dkc
            carry_v[...] = dvc
            dsk_ref[...] += dsk
            first = (n == 0).astype(F32)
            for hk in range(N_KV_HEADS):
                total = dbias[0][hk]
                for b in range(1, nqb):
                    total = total + dbias[b][hk]
                dbias_ref[0, hk] += total - first * dbias[0][hk]
                dbias_ref[1, hk] += first * dbias[0][hk]

        @pl.when(n == nt)
        def _():
            dk_ref[...] = _b(carry_k[...])
            dv_ref[...] = _b(carry_v[...])

    cur = lambda n: (jnp.minimum(n, nt - 1), 0)
    done_map = lambda n: (jnp.maximum(n - 1, 0), 0)
    return _grid_call(
        body, nt + 1, (qkv, qkv, qkv, qkv, qkv, bias, sinks_rows, d_attn), exchange, name="attn_bwd",
        in_specs=_attn_in_specs(nt, True, nqb) + [pl.BlockSpec((tq, ATTN_WIDTH), cur)],
        out_specs=[pl.BlockSpec((tq, ATTN_WIDTH), cur), pl.BlockSpec((tq, KV_WIDTH), done_map),
                   pl.BlockSpec((tq, KV_WIDTH), done_map),
                   pl.BlockSpec((2, N_KV_HEADS, 4 * lq, 2 * lq), lambda n: (0, 0, 0, 0)), _vec(128)],
        out_shape=[jax.ShapeDtypeStruct((s, ATTN_WIDTH), BF16), jax.ShapeDtypeStruct((s, KV_WIDTH), BF16),
                   jax.ShapeDtypeStruct((s, KV_WIDTH), BF16),
                   jax.ShapeDtypeStruct((2, N_KV_HEADS, 4 * lq, 2 * lq), F32), jax.ShapeDtypeStruct((1, 128), F32)],
        scratch_shapes=[pltpu.VMEM((tq, KV_WIDTH), F32), pltpu.VMEM((tq, KV_WIDTH), F32)],
        compiler_params=_params())


def rel_bias_table(rel_bias, bucket):
    lq = ATTN_BLOCK
    group = N_Q_HEADS // N_KV_HEADS

    def body(rb_ref, bk_ref, o_ref):
        bk = bk_ref[...]
        prev_keys = lax.broadcasted_iota(jnp.int32, bk.shape, 1) < lq
        accs = [jnp.full(bk.shape, -1e30, F32) for _ in range(N_Q_HEADS)]
        for b in range(N_BUCKETS):
            hit = bk == b
            accs = [jnp.where(hit, rb_ref[b, h], acc) for h, acc in enumerate(accs)]
        for h in range(N_Q_HEADS):
            rows = slice((h % group) * lq, (h % group + 1) * lq)
            o_ref[0, h // group, rows, :] = accs[h]
            o_ref[1, h // group, rows, :] = jnp.where(prev_keys, -1e30, accs[h])

    return pl.pallas_call(
        body, name="rel_bias_table",
        in_specs=[pl.BlockSpec(memory_space=pltpu.SMEM), pl.BlockSpec(memory_space=pltpu.VMEM)],
        out_shape=jax.ShapeDtypeStruct((2, N_KV_HEADS, group * lq, 2 * lq), F32),
        compiler_params=_params(n_axes=0))(rel_bias, bucket)


def rel_bias_grad(dbias, bucket):
    lq = ATTN_BLOCK
    group = N_Q_HEADS // N_KV_HEADS

    def body(db_ref, bk_ref, o_ref):
        rows = lax.broadcasted_iota(jnp.int32, (N_BUCKETS, 128), 0)
        lanes = lax.broadcasted_iota(jnp.int32, (N_BUCKETS, 128), 1)
        bk = bk_ref[...]
        per_head = []
        for h in range(N_Q_HEADS):
            sl = slice((h % group) * lq, (h % group + 1) * lq)
            per_head.append(db_ref[0, h // group, sl, :] + db_ref[1, h // group, sl, :])

        def per_bucket(b, acc):
            hit = (bk == b).astype(F32)
            for h in range(N_Q_HEADS):
                val = jnp.sum(per_head[h] * hit, keepdims=True)
                acc = acc + jnp.where((rows == b) & (lanes == h), val, 0.0)
            return acc

        o_ref[...] = lax.fori_loop(0, N_BUCKETS, per_bucket, jnp.zeros((N_BUCKETS, 128), F32))

    return pl.pallas_call(body, name="rel_bias_grad", out_shape=jax.ShapeDtypeStruct((N_BUCKETS, 128), F32),
                          compiler_params=_params(n_axes=0))(dbias, bucket)


def _tri_sum(a, upper):
    n = a.shape[0]
    ri = lax.broadcasted_iota(jnp.int32, (n, n), 0)
    ci = lax.broadcasted_iota(jnp.int32, (n, n), 1)
    tri = ((ri <= ci) if upper else (ri >= ci)).astype(BF16)
    hi = a.astype(BF16)
    rest = a - hi.astype(F32)
    mid = rest.astype(BF16)
    lo = (rest - mid.astype(F32)).astype(BF16)
    dot = lambda part: jnp.dot(tri, part, preferred_element_type=F32)
    return dot(hi) + dot(mid) + dot(lo)


@jax.custom_vjp
def _cumsum_rows(a):
    return _tri_sum(a, False)


_cumsum_rows.defvjp(lambda a: (_tri_sum(a, False), None), lambda _, g: (_tri_sum(g, True),))


def _ssm_core(u, z, dt_raw, hprev, dt_bias, a_log, dskip, norm_w):
    lc = u.shape[0]
    xbc = _silu(u)
    xs, bm, cm = xbc[:, :SSM_WIDTH], xbc[:, SSM_WIDTH:SSM_WIDTH + SSM_BC], xbc[:, SSM_WIDTH + SSM_BC:]
    dt = jax.nn.softplus(dt_raw + dt_bias)
    adt = dt * (-jnp.exp(a_log))
    ri = lax.broadcasted_iota(jnp.int32, (lc, lc), 0)
    ci = lax.broadcasted_iota(jnp.int32, (lc, lc), 1)
    causal = ri >= ci
    acum = _cumsum_rows(adt)
    acum_t = acum.T
    last = acum[lc - 1:lc, :]
    per_group = SSM_HEADS // SSM_GROUPS
    lane = lax.broadcasted_iota(jnp.int32, (1, 128), 1)
    rowid = lax.broadcasted_iota(jnp.int32, (128, 1), 0)
    lo_lanes = lane < SSM_HEAD_DIM
    ys, hs = [], []
    for g in range(SSM_GROUPS):
        bg = bm[:, g * SSM_STATE:(g + 1) * SSM_STATE]
        cg = cm[:, g * SSM_STATE:(g + 1) * SSM_STATE]
        cb = mm_nt(cg, bg)
        for pp in range(per_group // 2):
            ha = g * per_group + 2 * pp
            xp = xs[:, ha * SSM_HEAD_DIM:(ha + 2) * SSM_HEAD_DIM]
            hp = hprev[ha * SSM_HEAD_DIM:(ha + 2) * SSM_HEAD_DIM, :]
            xcp = xp * jnp.where(lo_lanes, dt[:, ha:ha + 1], dt[:, ha + 1:ha + 2])
            y_h, st_h = [], []
            for h in (ha, ha + 1):
                col, rowv, lasth = acum[:, h:h + 1], acum_t[h:h + 1, :], last[:, h:h + 1]
                decay = jnp.exp(jnp.where(causal, col - rowv, -1e30))
                y_h.append(mm(cb * decay, xcp) + mm_nt(cg * jnp.exp(col), hp))
                st_h.append(mm_tn(xcp, bg * jnp.exp(lasth - col)))
            y_pair = jnp.where(lo_lanes, y_h[0], y_h[1])
            st_pair = jnp.where(rowid < SSM_HEAD_DIM, st_h[0], st_h[1])
            la, lb = last[:, ha:ha + 1], last[:, ha + 1:ha + 2]
            hs.append(jnp.exp(jnp.where(rowid < SSM_HEAD_DIM, la, lb)) * hp + st_pair)
            dsk = jnp.where(lo_lanes, dskip[:, ha:ha + 1], dskip[:, ha + 1:ha + 2])
            ys.append(y_pair + dsk * xp)
    y = jnp.concatenate(ys, axis=1) * _silu(z)
    gw = SSM_WIDTH // SSM_GROUPS
    outs = []
    for g in range(SSM_GROUPS):
        yg = y[:, g * gw:(g + 1) * gw]
        outs.append(yg * lax.rsqrt(jnp.mean(yg * yg, axis=-1, keepdims=True) + NORM_EPS))
    return jnp.concatenate(outs, axis=1) * norm_w, jnp.concatenate(hs, axis=0)


def _ssm_param_specs():
    return [pl.BlockSpec((SSM_CONV, XBC_WIDTH), _const), _vec(XBC_WIDTH), _vec(128), _vec(128), _vec(128),
            _vec(SSM_WIDTH)]


SSM_FWD_SUB = 2
SSM_BWD_SUB = 1


def ssm_fwd(xbc_raw, z, dt_raw, conv_w, conv_b, dt_bias, a_log, dskip, norm_w, exchange=None):
    s = xbc_raw.shape[0]
    lc = SSM_CHUNK
    lt = lc * SSM_FWD_SUB
    hrows = SSM_HEADS * SSM_HEAD_DIM

    def body(x_ref, halo_ref, z_ref, dt_ref, cw_ref, cb_ref, dtb_ref, al_ref, dk_ref, nw_ref,
             o_ref, hp_ref, state):
        i = pl.program_id(0)

        @pl.when(i == 0)
        def _():
            state[...] = jnp.zeros_like(state)

        halo = halo_ref[...] * (i > 0).astype(F32)
        xin = jnp.concatenate([halo, x_ref[...]], axis=0)
        u = (_conv_rows(xin, cw_ref[...], SSM_CONV) + cb_ref[...])[HALO:]
        h = state[...]
        for k in range(SSM_FWD_SUB):
            rows = slice(k * lc, (k + 1) * lc)
            hp_ref[k * hrows:(k + 1) * hrows, :] = h
            out, h = _ssm_core(u[rows], z_ref[rows, :], dt_ref[rows, :], h, dtb_ref[...], al_ref[...], dk_ref[...],
                               nw_ref[...])
            o_ref[rows, :] = _b(out)
        state[...] = h

    tile = lambda n: pl.BlockSpec((lt, n), _row)
    halo_spec = pl.BlockSpec((HALO, XBC_WIDTH), lambda i: (jnp.maximum(i * (lt // HALO) - 1, 0), 0))
    return _grid_call(
        body, s // lt, (xbc_raw, xbc_raw, z, dt_raw, conv_w, conv_b, dt_bias, a_log, dskip, norm_w), exchange,
        name="ssm_fwd",
        in_specs=[tile(XBC_WIDTH), halo_spec, tile(SSM_WIDTH), tile(DT_PAD)] + _ssm_param_specs(),
        out_specs=[tile(SSM_WIDTH), pl.BlockSpec((SSM_FWD_SUB * hrows, SSM_STATE), _row)],
        out_shape=[jax.ShapeDtypeStruct((s, SSM_WIDTH), BF16),
                   jax.ShapeDtypeStruct((s // lc * hrows, SSM_STATE), F32)],
        scratch_shapes=[pltpu.VMEM((hrows, SSM_STATE), F32)],
        compiler_params=_params())


def ssm_bwd(xbc_raw, z, dt_raw, hprev_all, d_out, conv_w, conv_b, dt_bias, a_log, dskip, norm_w, exchange=None):
    s = xbc_raw.shape[0]
    lc = SSM_CHUNK
    sub = SSM_BWD_SUB
    lt = lc * sub
    nt = s // lt
    hrows = SSM_HEADS * SSM_HEAD_DIM

    def body(x_ref, halo_ref, z_ref, dt_ref, hp_ref, do_ref, cw_ref, cb_ref, dtb_ref, al_ref, dk_ref, nw_ref,
             dx_ref, dz_ref, ddt_ref, dcw_ref, dcb_ref, ddtb_ref, dal_ref, ddk_ref, dnw_ref, dstate, du_next):
        i = pl.program_id(0)
        tile_no = nt - 1 - i

        @pl.when(i == 0)
        def _():
            dstate[...] = jnp.zeros_like(dstate)
            du_next[...] = jnp.zeros_like(du_next)
            for r in (dcw_ref, dcb_ref, ddtb_ref, dal_ref, ddk_ref, dnw_ref):
                r[...] = jnp.zeros_like(r)

        halo = halo_ref[...] * (tile_no > 0).astype(F32)
        xin = jnp.concatenate([halo, x_ref[...]], axis=0)
        cw = cw_ref[...]
        u = (_conv_rows(xin, cw, SSM_CONV) + cb_ref[...])[HALO:]
        dh = dstate[...]
        dus = [None] * sub
        for k in reversed(range(sub)):
            rows = slice(k * lc, (k + 1) * lc)
            _, vjp = jax.vjp(_ssm_core, u[rows], z_ref[rows, :], dt_ref[rows, :], hp_ref[k * hrows:(k + 1) * hrows, :],
                             dtb_ref[...], al_ref[...], dk_ref[...], nw_ref[...])
            dus[k], dz, ddt, dh, ddtb, dal, ddk, dnw = vjp((do_ref[rows, :], dh))
            dz_ref[rows, :] = _b(dz)
            ddt_ref[rows, :] = _b(ddt)
            ddtb_ref[...] += ddtb
            dal_ref[...] += dal
            ddk_ref[...] += ddk
            dnw_ref[...] += dnw
        dstate[...] = dh
        du = jnp.concatenate(dus, axis=0)
        du_ext = jnp.concatenate([du, du_next[...]], axis=0)
        dx_ref[...] = _b(_conv_rows_t(du_ext, cw, SSM_CONV)[:lt])
        du_next[...] = du[:HALO]
        sums = [jnp.sum(du * pltpu.roll(xin, j, axis=0)[HALO:] if j else du * xin[HALO:], axis=0, keepdims=True)
                for j in range(SSM_CONV)]
        dcw_ref[...] += jnp.concatenate(sums[::-1] + [jnp.zeros((8 - SSM_CONV, XBC_WIDTH), F32)], axis=0)
        dcb_ref[...] += jnp.sum(du, axis=0, keepdims=True)

    rev = lambda i: (nt - 1 - i, 0)
    tile = lambda n: pl.BlockSpec((lt, n), rev)
    halo_spec = pl.BlockSpec((HALO, XBC_WIDTH), lambda i: (jnp.maximum((nt - 1 - i) * (lt // HALO) - 1, 0), 0))
    acc = lambda r, n: pl.BlockSpec((r, n), _const)
    return _grid_call(
        body, nt, (xbc_raw, xbc_raw, z, dt_raw, hprev_all, d_out, conv_w, conv_b, dt_bias, a_log, dskip, norm_w),
        exchange, name="ssm_bwd",
        in_specs=[tile(XBC_WIDTH), halo_spec, tile(SSM_WIDTH), tile(DT_PAD),
                  pl.BlockSpec((sub * hrows, SSM_STATE), rev), tile(SSM_WIDTH)] + _ssm_param_specs(),
        out_specs=[tile(XBC_WIDTH), tile(SSM_WIDTH), tile(DT_PAD), acc(8, XBC_WIDTH), acc(1, XBC_WIDTH),
                   acc(1, 128), acc(1, 128), acc(1, 128), acc(1, SSM_WIDTH)],
        out_shape=[jax.ShapeDtypeStruct((s, XBC_WIDTH), BF16), jax.ShapeDtypeStruct((s, SSM_WIDTH), BF16),
                   jax.ShapeDtypeStruct((s, DT_PAD), BF16), jax.ShapeDtypeStruct((8, XBC_WIDTH), F32),
                   jax.ShapeDtypeStruct((1, XBC_WIDTH), F32), jax.ShapeDtypeStruct((1, 128), F32),
                   jax.ShapeDtypeStruct((1, 128), F32), jax.ShapeDtypeStruct((1, 128), F32),
                   jax.ShapeDtypeStruct((1, SSM_WIDTH), F32)],
        scratch_shapes=[pltpu.VMEM((hrows, SSM_STATE), F32), pltpu.VMEM((HALO, XBC_WIDTH), F32)],
        compiler_params=_params())


def mix_out(attn, ssm, x, w_out16, gate1, post_mix_w, pre_ffn_w, scale2, shift2, tm=4 * TOKEN_TILE):
    s = x.shape[0]
    tm = min(tm, s)

    def body(a_ref, s_ref, x_ref, w_ref, g_ref, pw_ref, fw_ref, sc_ref, sh_ref, mixed_ref, x1_ref, h2_ref):
        mixed = (jnp.dot(a_ref[...], w_ref[:ATTN_WIDTH, :], preferred_element_type=F32)
                 + jnp.dot(s_ref[...], w_ref[ATTN_WIDTH:, :], preferred_element_type=F32))
        mixed_ref[...] = _b(mixed)
        x1 = x_ref[...] + g_ref[...] * _rms(mixed, pw_ref[...])
        x1_ref[...] = x1
        h2_ref[...] = _b(_norm_mod(x1, fw_ref[...], sc_ref[...], sh_ref[...]))

    tile = lambda n: pl.BlockSpec((tm, n), _row)
    return pl.pallas_call(
        body, name="mix_out", grid=(s // tm,),
        in_specs=[tile(ATTN_WIDTH), tile(SSM_WIDTH), tile(D_MODEL), _resident((D_MODEL, D_MODEL))]
        + [_vec(D_MODEL)] * 5,
        out_specs=[tile(D_MODEL)] * 3,
        out_shape=[jax.ShapeDtypeStruct((s, D_MODEL), BF16), jax.ShapeDtypeStruct((s, D_MODEL), F32),
                   jax.ShapeDtypeStruct((s, D_MODEL), BF16)],
        compiler_params=_params())(attn, ssm, x, w_out16, gate1, post_mix_w, pre_ffn_w, scale2, shift2)


GELU_K0, GELU_K1 = math.sqrt(2.0 / math.pi), 0.044715


def _gate(ug, uv):
    return jax.nn.gelu(ug, approximate=True) * uv


def _gate_bwd(ug, uv, df):
    sq = ug * ug
    t = jnp.tanh(ug * (GELU_K0 + (GELU_K0 * GELU_K1) * sq))
    half = 0.5 + 0.5 * t
    slope = half + ug * (1.0 - t * t) * (0.5 * GELU_K0 + (1.5 * GELU_K0 * GELU_K1) * sq)
    return df * uv * slope, df * (ug * half)


def up_gate(h2, w_up16, conv_w, conv_b, tm=TOKEN_TILE):
    s = h2.shape[0]

    def body(h_ref, halo_ref, w_ref, cw_ref, cb_ref, u_ref, uraw_ref, f_ref):
        halo = halo_ref[...]
        halo = jnp.where(pl.program_id(0) > 0, halo, jnp.zeros_like(halo))
        hin = jnp.concatenate([halo, h_ref[...]], axis=0)
        for lo in range(0, D_FF, FF_CHUNK):
            halves = []
            for base in (lo, D_FF + lo):
                cols = slice(base, base + FF_CHUNK)
                uraw = jnp.dot(hin, w_ref[:, cols], preferred_element_type=F32)
                uraw_ref[:, cols] = _b(uraw[NEXT:])
                u = (_conv_rows(uraw, cw_ref[:, cols], FFN_CONV) + cb_ref[:, cols])[NEXT:]
                u_ref[:, cols] = u
                halves.append(u)
            f_ref[:, lo:lo + FF_CHUNK] = _b(_gate(*halves))

    tile = lambda n: pl.BlockSpec((tm, n), _row)
    halo_spec = pl.BlockSpec((NEXT, D_MODEL), lambda i: (jnp.maximum(i * (tm // NEXT) - 1, 0), 0))
    return pl.pallas_call(
        body, name="up_gate", grid=(s // tm,),
        in_specs=[tile(D_MODEL), halo_spec, _resident((D_MODEL, 2 * D_FF)),
                  pl.BlockSpec((FFN_CONV, 2 * D_FF), _const), _vec(2 * D_FF)],
        out_specs=[tile(2 * D_FF), tile(2 * D_FF), tile(D_FF)],
        out_shape=[jax.ShapeDtypeStruct((s, 2 * D_FF), F32), jax.ShapeDtypeStruct((s, 2 * D_FF), BF16),
                   jax.ShapeDtypeStruct((s, D_FF), BF16)],
        compiler_params=_params())(h2, h2, w_up16, conv_w, conv_b)


DOWN_LOSS_TILE = 512


def down_loss(f16, w_down16, x1, target, gate2, post_ffn_w, tm=DOWN_LOSS_TILE):
    s = x1.shape[0]
    tm = min(tm, s)

    def body(f_ref, wd_ref, x1_ref, t_ref, g_ref, pw_ref, dffn_ref, dy_ref, loss_ref, dg_ref, dpw_ref, gw_ref):
        i = pl.program_id(0)

        @pl.when(i == 0)
        def _():
            loss_ref[...] = jnp.zeros_like(loss_ref)
            dg_ref[...] = jnp.zeros_like(dg_ref)
            dpw_ref[...] = jnp.zeros_like(dpw_ref)
            gw_ref[...] = jnp.zeros_like(gw_ref)

        ffn = jnp.dot(f_ref[...], wd_ref[...], preferred_element_type=F32)
        x1 = x1_ref[...]
        x2 = x1 + g_ref[...] * _rms(ffn, pw_ref[...])
        err = x2 - t_ref[...]
        dy = err * (1.0 / D_MODEL)
        dy_ref[...] = dy
        loss_ref[...] += 0.5 * jnp.sum(jnp.mean(err * err, axis=-1, keepdims=True))
        dffn, dg, dpw = _gated_rms_bwd(ffn, g_ref[...], pw_ref[...], dy)
        dffn16 = _b(dffn)
        dffn_ref[...] = dffn16
        dg_ref[...] += dg
        dpw_ref[...] += dpw
        gw_ref[...] += _tn(f_ref[...], dffn16)

    tile = lambda n: pl.BlockSpec((tm, n), _row)
    return pl.pallas_call(
        body, name="down_loss", grid=(s // tm,),
        in_specs=[tile(D_FF), _resident((D_FF, D_MODEL)), tile(D_MODEL), tile(D_MODEL), _vec(D_MODEL), _vec(D_MODEL)],
        out_specs=[tile(D_MODEL), tile(D_MODEL), _vec(128), _vec(D_MODEL), _vec(D_MODEL),
                   pl.BlockSpec((D_FF, D_MODEL), _const)],
        out_shape=[jax.ShapeDtypeStruct((s, D_MODEL), BF16), jax.ShapeDtypeStruct((s, D_MODEL), F32),
                   jax.ShapeDtypeStruct((1, 128), F32), jax.ShapeDtypeStruct((1, D_MODEL), F32),
                   jax.ShapeDtypeStruct((1, D_MODEL), F32), jax.ShapeDtypeStruct((D_FF, D_MODEL), F32)],
        compiler_params=_params())(f16, w_down16, x1, target, gate2, post_ffn_w)


BWD_CHUNK = 256


def ffn_bwd(u, u_raw16, d_ffn, conv_w, w_down_t16, w_up_t16, tm=TOKEN_TILE):
    s = u.shape[0]
    nt = s // tm

    def body(u_ref, unext_ref, uraw_ref, d_ref, dnext_ref, cw_ref, wdt_ref, wut_ref,
             du_ref, dh_ref, dcw_ref, dcb_ref):
        i = pl.program_id(0)

        @pl.when(i == 0)
        def _():
            dcw_ref[...] = jnp.zeros_like(dcw_ref)
            dcb_ref[...] = jnp.zeros_like(dcb_ref)

        dnext = dnext_ref[...]
        dnext = jnp.where(i < nt - 1, dnext, jnp.zeros_like(dnext))
        dff = jnp.concatenate([d_ref[...], dnext], axis=0)
        rows_ext = tm + NEXT
        for lo in range(0, D_FF, BWD_CHUNK):
            gcols, vcols = slice(lo, lo + BWD_CHUNK), slice(D_FF + lo, D_FF + lo + BWD_CHUNK)
            ug = jnp.concatenate([u_ref[:, gcols], unext_ref[:, gcols]], axis=0)
            uv = jnp.concatenate([u_ref[:, vcols], unext_ref[:, vcols]], axis=0)
            df = jnp.dot(dff, wdt_ref[:, gcols], preferred_element_type=F32)
            for cols, du in zip((gcols, vcols), _gate_bwd(ug, uv, df)):
                cw = cw_ref[:, cols]
                du1 = pltpu.roll(du, rows_ext - 1, axis=0)
                du2 = pltpu.roll(du, rows_ext - 2, axis=0)
                du_ref[:, cols] = _b((du * cw[2:3, :] + du1 * cw[1:2, :] + du2 * cw[0:1, :])[:tm])
                xr = uraw_ref[:, cols].astype(F32)
                rows = [jnp.sum(xr * d_[:tm], axis=0, keepdims=True) for d_ in (du2, du1, du)]
                dcw_ref[:, cols] += jnp.concatenate(rows + [jnp.zeros((8 - FFN_CONV, BWD_CHUNK), F32)], axis=0)
                dcb_ref[:, cols] += jnp.sum(du[:tm], axis=0, keepdims=True)
        dh_ref[...] = jnp.dot(du_ref[...], wut_ref[...], preferred_element_type=F32)

    tile = lambda n: pl.BlockSpec((tm, n), _row)
    nxt = lambda i: (jnp.minimum((i + 1) * (tm // NEXT), s // NEXT - 1), 0)
    return pl.pallas_call(
        body, name="ffn_bwd", grid=(nt,),
        in_specs=[tile(2 * D_FF), pl.BlockSpec((NEXT, 2 * D_FF), nxt), tile(2 * D_FF), tile(D_MODEL),
                  pl.BlockSpec((NEXT, D_MODEL), nxt), pl.BlockSpec((FFN_CONV, 2 * D_FF), _const),
                  _resident((D_MODEL, D_FF)), _resident((2 * D_FF, D_MODEL))],
        out_specs=[tile(2 * D_FF), tile(D_MODEL), pl.BlockSpec((8, 2 * D_FF), _const), _vec(2 * D_FF)],
        out_shape=[jax.ShapeDtypeStruct((s, 2 * D_FF), BF16), jax.ShapeDtypeStruct((s, D_MODEL), F32),
                   jax.ShapeDtypeStruct((8, 2 * D_FF), F32), jax.ShapeDtypeStruct((1, 2 * D_FF), F32)],
        compiler_params=_params())(u, u, u_raw16, d_ffn, d_ffn, conv_w, w_down_t16, w_up_t16)


def mix_bwd(dh2, x1, dy, mixed, attn, ssm, w_out_t16, pre_ffn_w, scale2, gate1, post_mix_w, tm=2 * TOKEN_TILE):
    s = x1.shape[0]

    def body(dh_ref, x1_ref, dy_ref, mx_ref, a_ref, s_ref, w_ref, fw_ref, sc_ref, g_ref, pw_ref,
             dx1_ref, da_ref, ds_ref, dfw_ref, dsc_ref, dsh_ref, dg_ref, dpw_ref, gw_ref):
        accs = (dfw_ref, dsc_ref, dsh_ref, dg_ref, dpw_ref)

        @pl.when(pl.program_id(0) == 0)
        def _():
            for r in accs + (gw_ref,):
                r[...] = jnp.zeros_like(r)

        dx1, dfw, dsc, dsh = _norm_mod_bwd(x1_ref[...], fw_ref[...], sc_ref[...], dh_ref[...])
        dx1 = dx1 + dy_ref[...]
        dx1_ref[...] = dx1
        dmixed, dg, dpw = _gated_rms_bwd(mx_ref[...].astype(F32), g_ref[...], pw_ref[...], dx1)
        dm16 = _b(dmixed)
        dmix_in = jnp.dot(dm16, w_ref[...], preferred_element_type=F32)
        da_ref[...] = _b(dmix_in[:, :ATTN_WIDTH])
        ds_ref[...] = dmix_in[:, ATTN_WIDTH:]
        gw_ref[:ATTN_WIDTH, :] += _tn(a_ref[...], dm16)
        gw_ref[ATTN_WIDTH:, :] += _tn(s_ref[...], dm16)
        for r, v in zip(accs, (dfw, dsc, dsh, dg, dpw)):
            r[...] += v

    tile = lambda n: pl.BlockSpec((tm, n), _row)
    return pl.pallas_call(
        body, name="mix_bwd", grid=(s // tm,),
        in_specs=[tile(D_MODEL)] * 4 + [tile(ATTN_WIDTH), tile(SSM_WIDTH), _resident((D_MODEL, D_MODEL))]
        + [_vec(D_MODEL)] * 4,
        out_specs=[tile(D_MODEL), tile(ATTN_WIDTH), tile(SSM_WIDTH)] + [_vec(D_MODEL)] * 5
        + [pl.BlockSpec((D_MODEL, D_MODEL), _const)],
        out_shape=[jax.ShapeDtypeStruct((s, D_MODEL), F32), jax.ShapeDtypeStruct((s, ATTN_WIDTH), BF16),
                   jax.ShapeDtypeStruct((s, SSM_WIDTH), F32)]
        + [jax.ShapeDtypeStruct((1, D_MODEL), F32)] * 5 + [jax.ShapeDtypeStruct((D_MODEL, D_MODEL), F32)],
        compiler_params=_params())(dh2, x1, dy, mixed, attn, ssm, w_out_t16, pre_ffn_w, scale2, gate1, post_mix_w)


INPROJ_BWD_TILE = 512


def inproj_bwd(dq, dk, dv, dxbc, dz, ddt, x, dx1, h1, w_in_t16, pre_mix_w, scale1, tm=INPROJ_BWD_TILE):
    s = x.shape[0]
    tm = min(tm, s)

    def body(dq_ref, dk_ref, dv_ref, dxbc_ref, dz_ref, ddt_ref, x_ref, dx1_ref, h_ref, w_ref, pw_ref, sc_ref,
             gx_ref, dpw_ref, dsc_ref, dsh_ref, gw_ref):
        accs = (dpw_ref, dsc_ref, dsh_ref)

        @pl.when(pl.program_id(0) == 0)
        def _():
            for r in accs + (gw_ref,):
                r[...] = jnp.zeros_like(r)

        h16 = h_ref[...]
        dh = None
        dqkv = jnp.concatenate([dq_ref[...], dk_ref[...], dv_ref[...]], axis=1)
        for d16, (lo, hi) in ((dqkv, (0, QKV_W)), (dxbc_ref[:, :SSM_WIDTH], COLS_XS), (dxbc_ref[:, SSM_WIDTH:], COLS_BC),
                              (dz_ref[...], COLS_Z), (ddt_ref[...], COLS_DT)):
            part = jnp.dot(d16, w_ref[lo:hi, :], preferred_element_type=F32)
            dh = part if dh is None else dh + part
            rows = min(hi, IN_PROJ_WIDTH) - lo
            gw_ref[lo:lo + rows, :] += _tn(d16, h16)[:rows]
        dx, dpw, dsc, dsh = _norm_mod_bwd(x_ref[...], pw_ref[...], sc_ref[...], dh)
        gx_ref[...] = dx1_ref[...] + dx
        for r, v in zip(accs, (dpw, dsc, dsh)):
            r[...] += v

    tile = lambda n: pl.BlockSpec((tm, n), _row)
    return pl.pallas_call(
        body, name="inproj_bwd", grid=(s // tm,),
        in_specs=[tile(ATTN_WIDTH), tile(KV_WIDTH), tile(KV_WIDTH), tile(XBC_WIDTH), tile(SSM_WIDTH), tile(DT_PAD),
                  tile(D_MODEL), tile(D_MODEL), tile(D_MODEL), _resident((PROJ_PAD, D_MODEL))] + [_vec(D_MODEL)] * 2,
        out_specs=[tile(D_MODEL)] + [_vec(D_MODEL)] * 3 + [pl.BlockSpec((IN_PROJ_WIDTH, D_MODEL), _const)],
        out_shape=[jax.ShapeDtypeStruct((s, D_MODEL), F32)] + [jax.ShapeDtypeStruct((1, D_MODEL), F32)] * 3
        + [jax.ShapeDtypeStruct((IN_PROJ_WIDTH, D_MODEL), F32)],
        compiler_params=_params())(dq, dk, dv, dxbc, dz, ddt, x, dx1, h1, w_in_t16, pre_mix_w, scale1)


def _adam(g, w, m, v):
    new_m = ADAM_B1 * m + (1.0 - ADAM_B1) * g
    new_v = ADAM_B2 * v + (1.0 - ADAM_B2) * jnp.square(g)
    m_hat = new_m / (1.0 - ADAM_B1 ** ADAM_STEP)
    v_hat = new_v / (1.0 - ADAM_B2 ** ADAM_STEP)
    return -ADAM_LR * (m_hat / (jnp.sqrt(v_hat) + ADAM_EPS) + ADAM_WD * w), new_m, new_v


ROW_PARAMS = (("b_ada", 6144, 6144), ("pre_mix_w", 1024, 1024), ("attn_sinks", 128, 8), ("ssm_conv_b", 1024, 1024),
              ("ssm_dt_bias", 128, 8), ("ssm_a_log", 128, 8), ("ssm_d", 128, 8), ("ssm_norm_w", 512, 512),
              ("post_mix_w", 1024, 1024), ("pre_ffn_w", 1024, 1024), ("ffn_conv_b", 5632, 5632),
              ("post_ffn_w", 1024, 1024))
LOSS_LANES = 128


def adamw_small(row_all, rb_all, rel_bias_wmv, row_wmv):
    n_rows = len(ROW_PARAMS)

    def body(*refs):
        row_ref, rb_ref = refs[:2]
        wmv = refs[2:5 + 3 * n_rows]
        outs = refs[5 + 3 * n_rows:]
        g_row, g_rb = row_ref[0], rb_ref[0]
        for k in range(1, N_DEV):
            g_row = g_row + row_ref[k]
            g_rb = g_rb + rb_ref[k]
        outs[0][...] = g_row[:, :LOSS_LANES]
        grads = [g_rb[:, :N_Q_HEADS]]
        off = LOSS_LANES
        for _, lanes, width in ROW_PARAMS:
            grads.append(g_row[:, off:off + width])
            off += lanes
        for i, g in enumerate(grads):
            w_ref, m_ref, v_ref = wmv[3 * i:3 * i + 3]
            g_out, d_out, m_out, v_out = outs[1 + 4 * i:5 + 4 * i]
            g_out[...] = g
            d_out[...], m_out[...], v_out[...] = _adam(g, w_ref[...], m_ref[...], v_ref[...])

    flat_wmv = list(rel_bias_wmv) + [a for wmv in row_wmv for a in wmv]
    shapes = [jax.ShapeDtypeStruct((1, LOSS_LANES), F32)] + [jax.ShapeDtypeStruct((N_BUCKETS, N_Q_HEADS), F32)] * 4
    for _, _, width in ROW_PARAMS:
        shapes += [jax.ShapeDtypeStruct((1, width), F32)] * 4
    return pl.pallas_call(body, name="adamw_small", out_shape=shapes,
                          compiler_params=_params(n_axes=0))(row_all, rb_all, *flat_wmv)


def adamw(parts, w, m, v, name):
    p, r, n = parts.shape
    tr = _row_tile(r)

    def body(p_ref, w_ref, m_ref, v_ref, g_ref, d_ref, nm_ref, nv_ref):
        g = p_ref[0].astype(F32)
        for k in range(1, p):
            g = g + p_ref[k].astype(F32)
        g_ref[...] = g
        d_ref[...], nm_ref[...], nv_ref[...] = _adam(g, w_ref[...], m_ref[...], v_ref[...])

    tile = pl.BlockSpec((tr, n), _row)
    return pl.pallas_call(
        body, name=name, grid=(r // tr,),
        in_specs=[pl.BlockSpec((p, tr, n), lambda i: (0, i, 0)), tile, tile, tile],
        out_specs=[tile] * 4, out_shape=[jax.ShapeDtypeStruct((r, n), F32)] * 4,
        compiler_params=_params())(parts, w, m, v)


def _bucket_table():
    lq = ATTN_BLOCK
    qi = np.arange(lq)[:, None] + lq
    kj = np.arange(2 * lq)[None, :]
    dist = qi - kj
    d = np.maximum(dist, 0)
    max_exact = N_BUCKETS // 2
    nf = np.maximum(d, 1).astype(np.float32)
    large = max_exact + (np.log(nf / max_exact) / math.log(REL_MAX_DIST / max_exact)
                         * (N_BUCKETS - max_exact)).astype(np.int32)
    large = np.minimum(large, N_BUCKETS - 1)
    bucket = np.where(d < max_exact, d, large).astype(np.int32)
    in_band = (dist >= 0) & (dist < REL_MAX_DIST)
    return np.where(in_band, bucket, -1).astype(np.int32)


def _cols_from_blocks(g):
    return jnp.transpose(g, (1, 0, 2)).reshape(g.shape[1], N_DEV * g.shape[2])


def _cols_to_blocks(a):
    r, n = a.shape
    return jnp.transpose(a.reshape(r, N_DEV, n // N_DEV), (1, 0, 2))


def _pad_in_rows(wt):
    return jnp.concatenate([wt, jnp.zeros((PROJ_PAD - wt.shape[0], wt.shape[1]), wt.dtype)], axis=0)


def _lane_pad(v, n=128):
    return jnp.pad(v, ((0, 0), (0, n - v.shape[1])))


def kernel(x, c, rel_bias, w_ada, b_ada, pre_mix_w, w_in, attn_sinks, ssm_conv_w, ssm_conv_b, ssm_dt_bias, ssm_a_log, ssm_d, ssm_norm_w, w_out, post_mix_w, pre_ffn_w, w_up, ffn_conv_w, ffn_conv_b, w_down, post_ffn_w, loss_target, m_rel_bias, m_w_ada, m_b_ada, m_pre_mix_w, m_w_in, m_attn_sinks, m_ssm_conv_w, m_ssm_conv_b, m_ssm_dt_bias, m_ssm_a_log, m_ssm_d, m_ssm_norm_w, m_w_out, m_post_mix_w, m_pre_ffn_w, m_w_up, m_ffn_conv_w, m_ffn_conv_b, m_w_down, m_post_ffn_w, v_rel_bias, v_w_ada, v_b_ada, v_pre_mix_w, v_w_in, v_attn_sinks, v_ssm_conv_w, v_ssm_conv_b, v_ssm_dt_bias, v_ssm_a_log, v_ssm_d, v_ssm_norm_w, v_w_out, v_post_mix_w, v_pre_ffn_w, v_w_up, v_ffn_conv_w, v_ffn_conv_b, v_w_down, v_post_ffn_w):
    weights = dict(rel_bias=rel_bias, w_ada=w_ada, b_ada=b_ada, pre_mix_w=pre_mix_w, w_in=w_in, attn_sinks=attn_sinks, ssm_conv_w=ssm_conv_w, ssm_conv_b=ssm_conv_b, ssm_dt_bias=ssm_dt_bias, ssm_a_log=ssm_a_log, ssm_d=ssm_d, ssm_norm_w=ssm_norm_w, w_out=w_out, post_mix_w=post_mix_w, pre_ffn_w=pre_ffn_w, w_up=w_up, ffn_conv_w=ffn_conv_w, ffn_conv_b=ffn_conv_b, w_down=w_down, post_ffn_w=post_ffn_w)
    mom_m = dict(rel_bias=m_rel_bias, w_ada=m_w_ada, b_ada=m_b_ada, pre_mix_w=m_pre_mix_w, w_in=m_w_in, attn_sinks=m_attn_sinks, ssm_conv_w=m_ssm_conv_w, ssm_conv_b=m_ssm_conv_b, ssm_dt_bias=m_ssm_dt_bias, ssm_a_log=m_ssm_a_log, ssm_d=m_ssm_d, ssm_norm_w=m_ssm_norm_w, w_out=m_w_out, post_mix_w=m_post_mix_w, pre_ffn_w=m_pre_ffn_w, w_up=m_w_up, ffn_conv_w=m_ffn_conv_w, ffn_conv_b=m_ffn_conv_b, w_down=m_w_down, post_ffn_w=m_post_ffn_w)
    mom_v = dict(rel_bias=v_rel_bias, w_ada=v_w_ada, b_ada=v_b_ada, pre_mix_w=v_pre_mix_w, w_in=v_w_in, attn_sinks=v_attn_sinks, ssm_conv_w=v_ssm_conv_w, ssm_conv_b=v_ssm_conv_b, ssm_dt_bias=v_ssm_dt_bias, ssm_a_log=v_ssm_a_log, ssm_d=v_ssm_d, ssm_norm_w=v_ssm_norm_w, w_out=v_w_out, post_mix_w=v_post_mix_w, pre_ffn_w=v_pre_ffn_w, w_up=v_w_up, ffn_conv_w=v_ffn_conv_w, ffn_conv_b=v_ffn_conv_b, w_down=v_w_down, post_ffn_w=v_post_ffn_w)
    order = ['rel_bias', 'w_ada', 'b_ada', 'pre_mix_w', 'w_in', 'attn_sinks', 'ssm_conv_w', 'ssm_conv_b', 'ssm_dt_bias', 'ssm_a_log', 'ssm_d', 'ssm_norm_w', 'w_out', 'post_mix_w', 'pre_ffn_w', 'w_up', 'ffn_conv_w', 'ffn_conv_b', 'w_down', 'post_ffn_w']

    me = 4 * lax.axis_index("x") + 2 * lax.axis_index("y") + lax.axis_index("c")
    xs_ = x[0]
    target = loss_target[0]

    (w_in_g, scw_g, fcw_g, c_g) = all_gather([_b(w_in[0]).T, ssm_conv_w[0], ffn_conv_w[0], c], "gather_weights")
    w_in_t16 = _pad_in_rows(w_in_g.reshape(IN_PROJ_WIDTH, D_MODEL))
    w_in16 = w_in_t16.T
    ssm_cw = _cols_from_blocks(scw_g)
    ffn_cw = _cols_from_blocks(fcw_g)
    c_all = c_g.reshape(N_DEV, D_MODEL)

    n_cols = w_ada.shape[2]
    b_cols = lax.dynamic_slice(b_ada, (0, me * n_cols), (1, n_cols))
    mod_part = ada_fwd(c_all, w_ada[0], b_cols)
    (mod_rows,) = all_to_all([mod_part.reshape(N_DEV, 1, n_cols)], "scatter_mod")
    mod = mod_rows.reshape(N_MOD, 1, D_MODEL)
    shift1, scale1, gate1, shift2, scale2, gate2 = (mod[i] for i in range(N_MOD))

    bucket_band = jnp.asarray(_bucket_table())
    bias = rel_bias_table(rel_bias, bucket_band)
    sinks_row = _lane_pad(attn_sinks)
    dt_bias, a_log, dskip = _lane_pad(ssm_dt_bias), _lane_pad(ssm_a_log), _lane_pad(ssm_d)

    h1, qkv, xbc_raw, z, dt_raw, w_out_g = pre_mix_inproj(
        xs_, pre_mix_w, scale1, shift1, w_in16, [(_b(w_out[0]), False)])
    attn, w_down_g = attn_fwd(qkv, bias, sinks_row, [(_b(w_down[0]), False)])
    ssm, hprev_all, w_up_g = ssm_fwd(xbc_raw, z, dt_raw, ssm_cw, ssm_conv_b, dt_bias, a_log, dskip, ssm_norm_w,
                                     [(_b(w_up[0]).T, False)])
    w_out16 = w_out_g.reshape(D_MODEL, D_MODEL)
    w_out_t16 = w_out16.T
    w_up_t16 = w_up_g.reshape(2 * D_FF, D_MODEL)
    w_up16 = w_up_t16.T
    w_down16 = w_down_g.reshape(D_FF, D_MODEL)
    w_down_t16 = w_down16.T
    mixed, x1, h2 = mix_out(attn, ssm, xs_, w_out16, gate1, post_mix_w, pre_ffn_w, scale2, shift2)
    u, u_raw16, f16 = up_gate(h2, w_up16, ffn_cw, ffn_conv_b)
    d_ffn, dy, loss_part, d_gate2, d_post_ffn_w, g_w_down = down_loss(f16, w_down16, x1, target, gate2, post_ffn_w)

    du_raw, dh2, d_ffn_cw, d_ffn_cb = ffn_bwd(u, u_raw16, d_ffn, ffn_cw, w_down_t16, w_up_t16)
    g_w_up_t = matmul_tn(du_raw, h2, "grad_w_up", FF_CHUNK, D_MODEL, tk=2048)
    (dx1, d_attn, d_ssm, d_pre_ffn_w, d_scale2, d_shift2, d_gate1, d_post_mix_w, g_w_out) = mix_bwd(
        dh2, x1, dy, mixed, attn, ssm, w_out_t16, pre_ffn_w, scale2, gate1, post_mix_w)
    dq, dk, dv, dbias, dsinks, p_w_down = attn_bwd(
        qkv, bias, sinks_row, d_attn, [(g_w_down.reshape(N_DEV, D_FF // N_DEV, D_MODEL), True)])
    d_rel_bias = rel_bias_grad(dbias, bucket_band)
    (dxbc, dz, ddt, d_ssm_cw, d_ssm_cb, d_dt_bias, d_a_log, d_dskip, d_norm_w, p_w_up, p_w_out) = ssm_bwd(
        xbc_raw, z, dt_raw, hprev_all, d_ssm, ssm_cw, ssm_conv_b, dt_bias, a_log, dskip, ssm_norm_w,
        [(g_w_up_t.reshape(N_DEV, 2 * D_FF // N_DEV, D_MODEL), True),
         (g_w_out.reshape(N_DEV, D_MODEL // N_DEV, D_MODEL), True)])
    grad_x, d_pre_mix_w, d_scale1, d_shift1, g_w_in_perm = inproj_bwd(
        dq, dk, dv, dxbc, dz, ddt, xs_, dx1, h1, w_in_t16, pre_mix_w, scale1)
    g_w_in_t = g_w_in_perm

    d_mod = jnp.concatenate([d_shift1, d_scale1, d_gate1, d_shift2, d_scale2, d_gate2], axis=1)
    late = ("w_in", "ssm_conv_w", "ffn_conv_w")
    full = [g_w_in_t.reshape(N_DEV, IN_PROJ_WIDTH // N_DEV, D_MODEL), _cols_to_blocks(d_ssm_cw[:SSM_CONV]),
            _cols_to_blocks(d_ffn_cw[:FFN_CONV])]
    core = lax.axis_index("c").astype(jnp.int32).reshape(1)
    got = pair_exchange(full, "pair_grads")
    chip_sums = [pair_sum(f_, g_, core, "pair_sum_" + k) for k, f_, g_ in zip(late, full, got)]
    chip_parts = all_to_all(chip_sums, "scatter_grads", CHIP_FLIPS, _chip_index)

    row_g = dict(b_ada=d_mod, pre_mix_w=d_pre_mix_w, attn_sinks=dsinks, ssm_conv_b=d_ssm_cb, ssm_dt_bias=d_dt_bias,
                 ssm_a_log=d_a_log, ssm_d=d_dskip, ssm_norm_w=d_norm_w, post_mix_w=d_post_mix_w,
                 pre_ffn_w=d_pre_ffn_w, ffn_conv_b=d_ffn_cb, post_ffn_w=d_post_ffn_w)
    row = jnp.concatenate([loss_part] + [row_g[k] for k, _, _ in ROW_PARAMS], axis=1)
    row_all, rb_all = all_gather([row, d_rel_bias], "gather_small")
    d_mod_cols = lax.dynamic_slice(row_all.reshape(N_DEV, row.shape[1]), (0, LOSS_LANES + me * n_cols),
                                   (N_DEV, n_cols))
    g_w_ada = ada_bwd(c_all, d_mod_cols)

    wmv = lambda k: (weights[k], mom_m[k], mom_v[k])
    small = adamw_small(row_all, rb_all, wmv("rel_bias"), [wmv(k) for k, _, _ in ROW_PARAMS])
    loss = small[0][0, 0]
    res = {k: tuple(small[1 + 4 * i:5 + 4 * i]) for i, k in enumerate(["rel_bias"] + [k for k, _, _ in ROW_PARAMS])}
    big = list(zip(late, chip_parts)) + [("w_down", p_w_down), ("w_up", p_w_up), ("w_out", p_w_out),
                                        ("w_ada", g_w_ada[None])]
    for k, parts in big:
        if k in ("w_in", "w_up"):
            outs_t = adamw(parts, weights[k][0].T, mom_m[k][0].T, mom_v[k][0].T, "adamw_" + k)
            res[k] = tuple(o.T[None] for o in outs_t)
        else:
            res[k] = tuple(o[None] for o in adamw(parts, weights[k][0], mom_m[k][0], mom_v[k][0], "adamw_" + k))

    outs = [loss, grad_x[None]]
    for field in range(4):
        outs += [res[k][field] for k in order]
    return tuple(outs)
```

```python
import math

import numpy as np
import jax
import jax.numpy as jnp
from jax import lax
from jax.experimental import pallas as pl
from jax.experimental.pallas import tpu as pltpu

F32 = jnp.float32
BF16 = jnp.bfloat16
MESH_ID = pl.DeviceIdType.MESH

N_DEV = 8
D_MODEL = 1024
N_Q_HEADS = 8
N_KV_HEADS = 2
HEAD_DIM = 64
ATTN_WIDTH = 512
KV_WIDTH = 128
ATTN_BLOCK = 128
N_BUCKETS = 32
REL_MAX_DIST = 128
SSM_HEADS = 8
SSM_HEAD_DIM = 64
SSM_WIDTH = 512
SSM_STATE = 128
SSM_GROUPS = 2
SSM_BC = 256
SSM_CONV = 4
SSM_CHUNK = 256
XBC_WIDTH = SSM_WIDTH + 2 * SSM_BC
D_FF = 2816
FFN_CONV = 3
NORM_EPS = 1e-6
N_MOD = 6
IN_PROJ_WIDTH = 2312
QKV_W = ATTN_WIDTH + 2 * KV_WIDTH
COLS_XS = (QKV_W, QKV_W + SSM_WIDTH)
COLS_Z = (COLS_XS[1], COLS_XS[1] + SSM_WIDTH)
COLS_BC = (COLS_Z[1], COLS_Z[1] + 2 * SSM_BC)
DT_PAD = 128
COLS_DT = (COLS_BC[1], COLS_BC[1] + DT_PAD)
PROJ_PAD = COLS_DT[1]
FF_CHUNK = 1408

ADAM_LR = 0.001
ADAM_B1 = 0.9
ADAM_B2 = 0.999
ADAM_EPS = 1e-08
ADAM_WD = 0.01
ADAM_STEP = 10

TOKEN_TILE = 256
HALO = 8
NEXT = 16
VMEM_LIMIT = 56 * 1024 * 1024


def _params(vmem=VMEM_LIMIT, n_axes=1):
    return pltpu.CompilerParams(dimension_semantics=("arbitrary",) * n_axes, vmem_limit_bytes=vmem)


def _b(x):
    return x.astype(BF16)


def _nn(a, b):
    return jnp.dot(_b(a), _b(b), preferred_element_type=F32)


def _nt(a, b):
    return lax.dot_general(_b(a), _b(b), (((1,), (1,)), ((), ())), preferred_element_type=F32)


def _tn(a, b):
    return lax.dot_general(_b(a), _b(b), (((0,), (0,)), ((), ())), preferred_element_type=F32)


@jax.custom_vjp
def mm(a, b):
    return _nn(a, b)


mm.defvjp(lambda a, b: (_nn(a, b), (a, b)),
          lambda r, g: (_nt(g, r[1]).astype(r[0].dtype), _tn(r[0], g).astype(r[1].dtype)))


@jax.custom_vjp
def mm_nt(a, b):
    return _nt(a, b)


mm_nt.defvjp(lambda a, b: (_nt(a, b), (a, b)),
             lambda r, g: (_nn(g, r[1]).astype(r[0].dtype), _tn(g, r[0]).astype(r[1].dtype)))


@jax.custom_vjp
def mm_tn(a, b):
    return _tn(a, b)


mm_tn.defvjp(lambda a, b: (_tn(a, b), (a, b)),
             lambda r, g: (_nt(r[1], g).astype(r[0].dtype), _nn(r[0], g).astype(r[1].dtype)))


def _rms(x, w):
    return x * lax.rsqrt(jnp.mean(x * x, axis=-1, keepdims=True) + NORM_EPS) * w


def _norm_mod(x, w, scale, shift):
    return _rms(x, w) * (1.0 + scale) + shift


def _rms_bwd(x, w, dy):
    r = lax.rsqrt(jnp.mean(x * x, axis=-1, keepdims=True) + NORM_EPS)
    xhat = x * r
    g = dy * w
    dx = r * (g - xhat * jnp.mean(g * xhat, axis=-1, keepdims=True))
    return dx, jnp.sum(dy * xhat, axis=0, keepdims=True)


def _norm_mod_bwd(x, w, scale, dh):
    dx, da = _rms_bwd(x, w * (1.0 + scale), dh)
    return dx, da * (1.0 + scale), da * w, jnp.sum(dh, axis=0, keepdims=True)


def _gated_rms_bwd(m, gate, w, dy):
    dm, t = _rms_bwd(m, w * gate, dy)
    return dm, t * w, t * gate


def _silu(x):
    return x * jax.nn.sigmoid(x)


def _conv_rows(xin, w, k):
    acc = xin * w[k - 1:k, :]
    for j in range(1, k):
        acc = acc + pltpu.roll(xin, j, axis=0) * w[k - 1 - j:k - j, :]
    return acc


def _conv_rows_t(du, w, k):
    n = du.shape[0]
    acc = du * w[k - 1:k, :]
    for j in range(1, k):
        acc = acc + pltpu.roll(du, n - j, axis=0) * w[k - 1 - j:k - j, :]
    return acc


def _row(i):
    return (i, 0)


def _const(i):
    return (0, 0)


def _vec(n):
    return pl.BlockSpec((1, n), _const)


def _resident(shape):
    return pl.BlockSpec(shape, _const, pipeline_mode=pl.Buffered(1))


def _block_index(p):
    return 4 * p[0] + 2 * p[1] + p[2]


def all_gather(arrs, name):
    n = len(arrs)

    def body(*refs):
        ins, outs = refs[:n], refs[n:2 * n]
        send_sems, recv_sems, local_sems = refs[2 * n:]
        x, y, c = lax.axis_index("x"), lax.axis_index("y"), lax.axis_index("c")
        me, sibling = (x, y, c), (x, y, 1 - c)
        chips = [(1 - x, y), (x, 1 - y), (1 - x, 1 - y)]

        def copy(a, k, block, to, src=None):
            dst = outs[a].at[_block_index(block)]
            return pltpu.make_async_remote_copy(
                src_ref=dst if src is None else src, dst_ref=dst,
                send_sem=send_sems.at[a * 7 + k], recv_sem=recv_sems.at[a * 7 + k],
                device_id=to, device_id_type=MESH_ID)

        mine = [pltpu.make_async_copy(ins[a], outs[a].at[_block_index(me)], local_sems.at[a]) for a in range(n)]
        for cp in mine:
            cp.start()
        first = []
        for a in range(n):
            first.append(copy(a, 0, me, sibling, src=ins[a]))
            first += [copy(a, 1 + j, me, (*chip, c), src=ins[a]) for j, chip in enumerate(chips)]
        for cp in first:
            cp.start()
        passed = []
        for j, chip in enumerate(chips):
            for a in range(n):
                copy(a, 1 + j, (*chip, c), me).wait_recv()
                cp = copy(a, 4 + j, (*chip, c), sibling)
                cp.start()
                passed.append(cp)
        for a in range(n):
            copy(a, 0, sibling, me).wait_recv()
            for j, chip in enumerate(chips):
                copy(a, 4 + j, (*chip, 1 - c), me).wait_recv()
        for cp in first + passed:
            cp.wait_send()
        for cp in mine:
            cp.wait()

    any_spec = pl.BlockSpec(memory_space=pl.ANY)
    return pl.pallas_call(
        body, name=name,
        out_shape=[jax.ShapeDtypeStruct((N_DEV,) + a.shape, a.dtype) for a in arrs],
        in_specs=[any_spec] * n, out_specs=[any_spec] * n,
        scratch_shapes=[pltpu.SemaphoreType.DMA((7 * n,)), pltpu.SemaphoreType.DMA((7 * n,)),
                        pltpu.SemaphoreType.DMA((n,))],
    )(*arrs)


ALL_FLIPS = ((0, 0, 1), (0, 1, 0), (0, 1, 1), (1, 0, 0), (1, 0, 1), (1, 1, 0), (1, 1, 1))
CHIP_FLIPS = ((0, 1, 0), (1, 0, 0), (1, 1, 0))


def _chip_index(p):
    return 2 * p[0] + p[1]


def all_to_all(arrs, name, flips=ALL_FLIPS, index=_block_index):
    n = len(arrs)
    nf = len(flips)

    def body(*refs):
        ins, outs = refs[:n], refs[n:2 * n]
        send_sems, recv_sems, local_sems = refs[2 * n:]
        pos = (lax.axis_index("x"), lax.axis_index("y"), lax.axis_index("c"))
        me = index(pos)
        peers = [tuple(1 - p if f else p for p, f in zip(pos, flip)) for flip in flips]

        def copy(a, k):
            peer = peers[k]
            return pltpu.make_async_remote_copy(
                src_ref=ins[a].at[index(peer)], dst_ref=outs[a].at[me],
                send_sem=send_sems.at[a * nf + k], recv_sem=recv_sems.at[a * nf + k],
                device_id=peer, device_id_type=MESH_ID)

        def landed(a, k):
            slot = outs[a].at[index(peers[k])]
            return pltpu.make_async_remote_copy(
                src_ref=slot, dst_ref=slot,
                send_sem=send_sems.at[a * nf + k], recv_sem=recv_sems.at[a * nf + k],
                device_id=peers[k], device_id_type=MESH_ID)

        mine = [pltpu.make_async_copy(ins[a].at[me], outs[a].at[me], local_sems.at[a]) for a in range(n)]
        for cp in mine:
            cp.start()
        sent = [copy(a, k) for a in range(n) for k in range(nf)]
        for cp in sent:
            cp.start()
        for a in range(n):
            for k in range(nf):
                landed(a, k).wait_recv()
        for cp in sent:
            cp.wait_send()
        for cp in mine:
            cp.wait()

    any_spec = pl.BlockSpec(memory_space=pl.ANY)
    return pl.pallas_call(
        body, name=name,
        out_shape=[jax.ShapeDtypeStruct(a.shape, a.dtype) for a in arrs],
        in_specs=[any_spec] * n, out_specs=[any_spec] * n,
        scratch_shapes=[pltpu.SemaphoreType.DMA((nf * n,)), pltpu.SemaphoreType.DMA((nf * n,)),
                        pltpu.SemaphoreType.DMA((n,))],
    )(*arrs)


def _direct_exchange(src, dst, sems, scatter):
    send_sems, recv_sems, local_sem = sems
    pos = (lax.axis_index("x"), lax.axis_index("y"), lax.axis_index("c"))
    me = _block_index(pos)
    peers = [tuple(1 - p if f else p for p, f in zip(pos, flip)) for flip in ALL_FLIPS]

    def outgoing(k):
        return pltpu.make_async_remote_copy(
            src_ref=src.at[_block_index(peers[k])] if scatter else src, dst_ref=dst.at[me],
            send_sem=send_sems.at[k], recv_sem=recv_sems.at[k], device_id=peers[k], device_id_type=MESH_ID)

    def incoming(k):
        slot = dst.at[_block_index(peers[k])]
        return pltpu.make_async_remote_copy(
            src_ref=slot, dst_ref=slot, send_sem=send_sems.at[k], recv_sem=recv_sems.at[k],
            device_id=peers[k], device_id_type=MESH_ID)

    def local():
        return pltpu.make_async_copy(src.at[me] if scatter else src, dst.at[me], local_sem)

    def start():
        local().start()
        for k in range(len(ALL_FLIPS)):
            outgoing(k).start()

    def finish():
        for k in range(len(ALL_FLIPS)):
            incoming(k).wait_recv()
        for k in range(len(ALL_FLIPS)):
            outgoing(k).wait_send()
        local().wait()

    return start, finish


def hosted_call(body, exchanges, steps, n_in, n_out, **call):
    n_ex = len(exchanges)

    def wrapped(*refs):
        ins, srcs = refs[:n_in], refs[n_in:n_in + n_ex]
        outs = refs[n_in + n_ex:n_in + n_ex + n_out]
        dsts = refs[n_in + n_ex + n_out:n_in + 2 * n_ex + n_out]
        rest = refs[n_in + 2 * n_ex + n_out:]
        scratch, sems = rest[:len(rest) - 3 * n_ex], rest[len(rest) - 3 * n_ex:]
        plans = [_direct_exchange(srcs[e], dsts[e], sems[3 * e:3 * e + 3], exchanges[e][1]) for e in range(n_ex)]

        @pl.when(pl.program_id(0) == 0)
        def _():
            for start, _ in plans:
                start()

        body(*ins, *outs, *scratch)

        @pl.when(pl.program_id(0) == steps - 1)
        def _():
            for _, finish in plans:
                finish()

    any_spec = pl.BlockSpec(memory_space=pl.ANY)
    landings = [jax.ShapeDtypeStruct(src.shape if scatter else (N_DEV,) + src.shape, src.dtype)
                for src, scatter in exchanges]
    n_flips = len(ALL_FLIPS)
    sems = [pltpu.SemaphoreType.DMA((n_flips,)), pltpu.SemaphoreType.DMA((n_flips,)), pltpu.SemaphoreType.DMA(())]
    return pl.pallas_call(
        wrapped, grid=(steps,),
        in_specs=list(call.pop("in_specs")) + [any_spec] * n_ex,
        out_specs=list(call.pop("out_specs")) + [any_spec] * n_ex,
        out_shape=list(call.pop("out_shape")) + landings,
        scratch_shapes=list(call.pop("scratch_shapes", [])) + sems * n_ex,
        **call)


def _grid_call(body, steps, args, exchanges, **call):
    if not exchanges:
        return pl.pallas_call(body, grid=(steps,), **call)(*args)
    srcs = [src for src, _ in exchanges]
    return hosted_call(body, exchanges, steps, len(args), len(call["out_shape"]), **call)(*args, *srcs)


N_CHIPS = 4


def pair_exchange(arrs, name):
    n = len(arrs)

    def body(*refs):
        ins, outs = refs[:n], refs[n:2 * n]
        send_sems, recv_sems = refs[2 * n:]
        x, y, c = lax.axis_index("x"), lax.axis_index("y"), lax.axis_index("c")
        sibling = (x, y, 1 - c)
        sent = []
        for a in range(n):
            for q in range(N_CHIPS):
                cp = pltpu.make_async_remote_copy(
                    src_ref=ins[a].at[2 * q + (1 - c)], dst_ref=outs[a].at[q],
                    send_sem=send_sems.at[a * N_CHIPS + q], recv_sem=recv_sems.at[a * N_CHIPS + q],
                    device_id=sibling, device_id_type=MESH_ID)
                cp.start()
                sent.append(cp)
        for cp in sent:
            cp.wait_recv()
        for cp in sent:
            cp.wait_send()

    any_spec = pl.BlockSpec(memory_space=pl.ANY)
    return pl.pallas_call(
        body, name=name,
        out_shape=[jax.ShapeDtypeStruct((N_CHIPS,) + a.shape[1:], a.dtype) for a in arrs],
        in_specs=[any_spec] * n, out_specs=[any_spec] * n,
        scratch_shapes=[pltpu.SemaphoreType.DMA((N_CHIPS * n,)), pltpu.SemaphoreType.DMA((N_CHIPS * n,))],
    )(*arrs)


def pair_sum(full, got, core, name):
    _, r, n = full.shape
    tr = _row_tile(r)

    def body(c_ref, mine_ref, got_ref, o_ref):
        o_ref[...] = _b(mine_ref[...] + got_ref[...])

    grid_spec = pltpu.PrefetchScalarGridSpec(
        num_scalar_prefetch=1, grid=(N_CHIPS, r // tr),
        in_specs=[pl.BlockSpec((1, tr, n), lambda q, i, c_ref: (2 * q + c_ref[0], i, 0)),
                  pl.BlockSpec((1, tr, n), lambda q, i, c_ref: (q, i, 0))],
        out_specs=pl.BlockSpec((1, tr, n), lambda q, i, c_ref: (q, i, 0)))
    return pl.pallas_call(body, name=name, grid_spec=grid_spec,
                          out_shape=jax.ShapeDtypeStruct((N_CHIPS, r, n), BF16),
                          compiler_params=_params(n_axes=2))(core, full, got)


def _row_tile(r):
    for cand in (256, 128, 64, 32, 16):
        if r % cand == 0 and r > cand:
            return cand
    return r


def ada_fwd(c_all, w_ada, b_cols):
    def body(c_ref, w_ref, b_ref, o_ref):
        o_ref[...] = _nn(_silu(c_ref[...]), w_ref[...]) + b_ref[...]

    return pl.pallas_call(body, name="ada_fwd",
                          out_shape=jax.ShapeDtypeStruct((N_DEV, w_ada.shape[1]), F32),
                          compiler_params=_params(n_axes=0))(c_all, w_ada, b_cols)


def ada_bwd(c_all, g_cols):
    def body(c_ref, g_ref, o_ref):
        o_ref[...] = _tn(_silu(c_ref[...]), g_ref[...])

    return pl.pallas_call(body, name="ada_bwd",
                          out_shape=jax.ShapeDtypeStruct((c_all.shape[1], g_cols.shape[1]), F32),
                          compiler_params=_params(n_axes=0))(c_all, g_cols)


def matmul_tn(a, b, name, bm, bn, tk=512):
    s, m = a.shape
    n = b.shape[1]
    tk = min(tk, s)

    def body(a_ref, b_ref, o_ref):
        @pl.when(pl.program_id(2) == 0)
        def _():
            o_ref[...] = jnp.zeros_like(o_ref)

        o_ref[...] += _tn(a_ref[...], b_ref[...])

    return pl.pallas_call(
        body, name=name, grid=(m // bm, n // bn, s // tk),
        in_specs=[pl.BlockSpec((tk, bm), lambda i, j, k: (k, i)), pl.BlockSpec((tk, bn), lambda i, j, k: (k, j))],
        out_specs=pl.BlockSpec((bm, bn), lambda i, j, k: (i, j)),
        out_shape=jax.ShapeDtypeStruct((m, n), F32),
        compiler_params=_params(n_axes=3))(a, b)


def pre_mix_inproj(x, w, scale, shift, w_in16, exchange=None, tm=4 * TOKEN_TILE):
    s = x.shape[0]
    tm = min(tm, s)

    def body(x_ref, w_ref, sc_ref, sh_ref, win_ref, h_ref, qkv_ref, xbc_ref, z_ref, dt_ref):
        h16 = _b(_norm_mod(x_ref[...], w_ref[...], sc_ref[...], sh_ref[...]))
        h_ref[...] = h16
        dot = lambda lo, hi: jnp.dot(h16, win_ref[:, lo:hi], preferred_element_type=F32)
        qkv_ref[...] = _b(dot(0, QKV_W))
        xbc_ref[:, :SSM_WIDTH] = dot(*COLS_XS)
        xbc_ref[:, SSM_WIDTH:] = dot(*COLS_BC)
        z_ref[...] = dot(*COLS_Z)
        dt_ref[...] = dot(*COLS_DT)

    tile = lambda n: pl.BlockSpec((tm, n), _row)
    return _grid_call(
        body, s // tm, (x, w, scale, shift, w_in16), exchange, name="pre_mix_inproj",
        in_specs=[tile(D_MODEL), _vec(D_MODEL), _vec(D_MODEL), _vec(D_MODEL), _resident((D_MODEL, PROJ_PAD))],
        out_specs=[tile(D_MODEL), tile(QKV_W), tile(XBC_WIDTH), tile(SSM_WIDTH), tile(DT_PAD)],
        out_shape=[jax.ShapeDtypeStruct((s, D_MODEL), BF16), jax.ShapeDtypeStruct((s, QKV_W), BF16),
                   jax.ShapeDtypeStruct((s, XBC_WIDTH), F32), jax.ShapeDtypeStruct((s, SSM_WIDTH), F32),
                   jax.ShapeDtypeStruct((s, DT_PAD), F32)],
        compiler_params=_params())


ATTN_QB_FWD, ATTN_QB_BWD = 4, 4


def _attn_tile(q, kp, kc, vp, vc, bias, sinks):
    lq = ATTN_BLOCK
    group = N_Q_HEADS // N_KV_HEADS
    lanes = lax.broadcasted_iota(jnp.int32, (1, 128), 1)
    rid = lax.broadcasted_iota(jnp.int32, (group * lq, 1), 0)
    sink_cols = []
    for hk in range(N_KV_HEADS):
        sink = jnp.zeros((group * lq, 1), F32)
        for g in range(group):
            s_h = jnp.sum(jnp.where(lanes == hk * group + g, sinks, 0.0), axis=-1, keepdims=True)
            sink = jnp.where((rid >= g * lq) & (rid < (g + 1) * lq), s_h, sink)
        sink_cols.append(sink)
    kall = jnp.concatenate([kp, kc], axis=0)
    vall = jnp.concatenate([vp, vc], axis=0)
    blocks = []
    for b in range(q.shape[0] // lq):
        qb = q[b * lq:(b + 1) * lq]
        outs = []
        for hk in range(N_KV_HEADS):
            cols = slice(hk * HEAD_DIM, (hk + 1) * HEAD_DIM)
            kb = kall[b * lq:(b + 2) * lq, cols]
            vb = vall[b * lq:(b + 2) * lq, cols]
            qg = jnp.concatenate([qb[:, (hk * group + g) * HEAD_DIM:(hk * group + g + 1) * HEAD_DIM]
                                  for g in range(group)], axis=0)
            sc = mm_nt(qg, kb) * (HEAD_DIM ** -0.5) + bias[b][hk]
            sink = sink_cols[hk]
            m = lax.stop_gradient(jnp.maximum(jnp.max(sc, axis=-1, keepdims=True), sink))
            p = jnp.exp(sc - m)
            probs = p / (jnp.sum(p, axis=-1, keepdims=True) + jnp.exp(sink - m))
            og = mm(probs, vb)
            outs += [og[g * lq:(g + 1) * lq] for g in range(group)]
        blocks.append(jnp.concatenate(outs, axis=1))
    return jnp.concatenate(blocks, axis=0)


def _attn_tile_bwd(q, kp, kc, vp, vc, bias, sinks, do):
    lq = ATTN_BLOCK
    group = N_Q_HEADS // N_KV_HEADS
    scale = HEAD_DIM ** -0.5
    lanes = lax.broadcasted_iota(jnp.int32, (1, 128), 1)
    rid = lax.broadcasted_iota(jnp.int32, (group * lq, 1), 0)
    sink_cols = []
    for hk in range(N_KV_HEADS):
        sink = jnp.zeros((group * lq, 1), F32)
        for g in range(group):
            s_h = jnp.sum(jnp.where(lanes == hk * group + g, sinks, 0.0), axis=-1, keepdims=True)
            sink = jnp.where((rid >= g * lq) & (rid < (g + 1) * lq), s_h, sink)
        sink_cols.append(sink)
    kall = jnp.concatenate([kp, kc], axis=0)
    vall = jnp.concatenate([vp, vc], axis=0)
    dsk = jnp.zeros((1, 128), F32)
    dq_blocks, dbias = [], []
    nqb = q.shape[0] // lq
    dk_parts = [[None] * nqb for _ in range(N_KV_HEADS)]
    dv_parts = [[None] * nqb for _ in range(N_KV_HEADS)]
    for b in range(nqb):
        qb, dob = q[b * lq:(b + 1) * lq], do[b * lq:(b + 1) * lq]
        dq_heads, dbias_b = [], []
        for hk in range(N_KV_HEADS):
            cols = slice(hk * HEAD_DIM, (hk + 1) * HEAD_DIM)
            kb = kall[b * lq:(b + 2) * lq, cols]
            vb = vall[b * lq:(b + 2) * lq, cols]
            heads = [hk * group + g for g in range(group)]
            qg = jnp.concatenate([qb[:, h * HEAD_DIM:(h + 1) * HEAD_DIM] for h in heads], axis=0)
            dog = jnp.concatenate([dob[:, h * HEAD_DIM:(h + 1) * HEAD_DIM] for h in heads], axis=0)
            sink = sink_cols[hk]
            sc = _nt(qg, kb) * scale + bias[b][hk]
            m = jnp.maximum(jnp.max(sc, axis=-1, keepdims=True), sink)
            p = jnp.exp(sc - m)
            es = jnp.exp(sink - m)
            inv = 1.0 / (jnp.sum(p, axis=-1, keepdims=True) + es)
            probs = p * inv
            dprobs = _nt(dog, vb)
            delta = jnp.sum(probs * dprobs, axis=-1, keepdims=True)
            dsc = probs * (dprobs - delta)
            dbias_b.append(dsc)
            dsink = -(es * inv) * delta
            for g, h in enumerate(heads):
                tot = jnp.sum(dsink[g * lq:(g + 1) * lq], axis=0, keepdims=True)
                dsk = dsk + jnp.where(lanes == h, tot, 0.0)
            dqg = _nn(dsc, kb) * scale
            dq_heads += [dqg[g * lq:(g + 1) * lq] for g in range(group)]
            dk_parts[hk][b] = _tn(dsc, qg) * scale
            dv_parts[hk][b] = _tn(probs, dog)
        dq_blocks.append(jnp.concatenate(dq_heads, axis=1))
        dbias.append(dbias_b)

    def overlap_add(parts):
        chunks = []
        for r in range(nqb + 1):
            acc = None
            if r < nqb:
                acc = parts[r][:lq]
            if r >= 1:
                tail = parts[r - 1][lq:]
                acc = tail if acc is None else acc + tail
            chunks.append(acc)
        return jnp.concatenate(chunks, axis=0)

    dkall = jnp.concatenate([overlap_add(dk_parts[hk]) for hk in range(N_KV_HEADS)], axis=1)
    dvall = jnp.concatenate([overlap_add(dv_parts[hk]) for hk in range(N_KV_HEADS)], axis=1)
    return jnp.concatenate(dq_blocks, axis=0), dkall, dvall, dbias, dsk


def _attn_in_specs(nt, clamp, nqb):
    lq, tq = ATTN_BLOCK, ATTN_BLOCK * nqb
    cur = lambda n: jnp.minimum(n, nt - 1) if clamp else n
    prev = lambda n: jnp.maximum(cur(n) * nqb - 1, 0)
    kcol, vcol = ATTN_WIDTH // KV_WIDTH, ATTN_WIDTH // KV_WIDTH + 1
    return [pl.BlockSpec((tq, ATTN_WIDTH), lambda n: (cur(n), 0)),
            pl.BlockSpec((lq, KV_WIDTH), lambda n: (prev(n), kcol)),
            pl.BlockSpec((tq, KV_WIDTH), lambda n: (cur(n), kcol)),
            pl.BlockSpec((lq, KV_WIDTH), lambda n: (prev(n), vcol)),
            pl.BlockSpec((tq, KV_WIDTH), lambda n: (cur(n), vcol)),
            pl.BlockSpec((2, N_KV_HEADS, 4 * lq, 2 * lq), lambda n: (0, 0, 0, 0)),
            _vec(128)]


def _tile_bias(bias_ref, first, nqb):
    return [[jnp.where(first, bias_ref[1, hk], bias_ref[0, hk]) if b == 0 else bias_ref[0, hk]
             for hk in range(N_KV_HEADS)] for b in range(nqb)]


def attn_fwd(qkv, bias, sinks_rows, exchange=None):
    s = qkv.shape[0]
    nqb = min(ATTN_QB_FWD, s // ATTN_BLOCK)
    tq = ATTN_BLOCK * nqb
    nt = s // tq

    def body(q_ref, kp_ref, kc_ref, vp_ref, vc_ref, bias_ref, sk_ref, o_ref):
        f = lambda r: r[...].astype(F32)
        o = _attn_tile(f(q_ref), f(kp_ref), f(kc_ref), f(vp_ref), f(vc_ref),
                       _tile_bias(bias_ref, pl.program_id(0) == 0, nqb), sk_ref[...])
        o_ref[...] = _b(o)

    return _grid_call(
        body, nt, (qkv, qkv, qkv, qkv, qkv, bias, sinks_rows), exchange, name="attn_fwd",
        in_specs=_attn_in_specs(nt, False, nqb),
        out_specs=[pl.BlockSpec((tq, ATTN_WIDTH), _row)],
        out_shape=[jax.ShapeDtypeStruct((s, ATTN_WIDTH), BF16)],
        compiler_params=_params())


def attn_bwd(qkv, bias, sinks_rows, d_attn, exchange=None):
    s = qkv.shape[0]
    nqb = ATTN_QB_BWD
    lq, tq = ATTN_BLOCK, ATTN_BLOCK * nqb
    nt = s // tq

    def body(q_ref, kp_ref, kc_ref, vp_ref, vc_ref, bias_ref, sk_ref, do_ref,
             dq_ref, dk_ref, dv_ref, dbias_ref, dsk_ref, carry_k, carry_v):
        n = pl.program_id(0)

        @pl.when(n == 0)
        def _():
            dbias_ref[...] = jnp.zeros_like(dbias_ref)
            dsk_ref[...] = jnp.zeros_like(dsk_ref)
            carry_k[...] = jnp.zeros_like(carry_k)
            carry_v[...] = jnp.zeros_like(carry_v)

        @pl.when(n < nt)
        def _():
            f = lambda r: r[...].astype(F32)
            dq, dkall, dvall, dbias, dsk = _attn_tile_bwd(
                f(q_ref), f(kp_ref), f(kc_ref), f(vp_ref), f(vc_ref), _tile_bias(bias_ref, n == 0, nqb), sk_ref[...],
                f(do_ref))
            dkp, dkc, dvp, dvc = dkall[:lq], dkall[lq:], dvall[:lq], dvall[lq:]
            dq_ref[...] = _b(dq)
            done = tq - lq
            dk_ref[:done, :] = _b(carry_k[:done, :])
            dv_ref[:done, :] = _b(carry_v[:done, :])
            dk_ref[done:, :] = _b(carry_k[done:, :] + dkp)
            dv_ref[done:, :] = _b(carry_v[done:, :] + dvp)
            carry_k[...] = dkc
            carry_v[...] = dvc
            dsk_ref[...] += dsk
            first = (n == 0).astype(F32)
            for hk in range(N_KV_HEADS):
                total = dbias[0][hk]
                for b in range(1, nqb):
                    total = total + dbias[b][hk]
                dbias_ref[0, hk] += total - first * dbias[0][hk]
                dbias_ref[1, hk] += first * dbias[0][hk]

        @pl.when(n == nt)
        def _():
            dk_ref[...] = _b(carry_k[...])
            dv_ref[...] = _b(carry_v[...])

    cur = lambda n: (jnp.minimum(n, nt - 1), 0)
    done_map = lambda n: (jnp.maximum(n - 1, 0), 0)
    return _grid_call(
        body, nt + 1, (qkv, qkv, qkv, qkv, qkv, bias, sinks_rows, d_attn), exchange, name="attn_bwd",
        in_specs=_attn_in_specs(nt, True, nqb) + [pl.BlockSpec((tq, ATTN_WIDTH), cur)],
        out_specs=[pl.BlockSpec((tq, ATTN_WIDTH), cur), pl.BlockSpec((tq, KV_WIDTH), done_map),
                   pl.BlockSpec((tq, KV_WIDTH), done_map),
                   pl.BlockSpec((2, N_KV_HEADS, 4 * lq, 2 * lq), lambda n: (0, 0, 0, 0)), _vec(128)],
        out_shape=[jax.ShapeDtypeStruct((s, ATTN_WIDTH), BF16), jax.ShapeDtypeStruct((s, KV_WIDTH), BF16),
                   jax.ShapeDtypeStruct((s, KV_WIDTH), BF16),
                   jax.ShapeDtypeStruct((2, N_KV_HEADS, 4 * lq, 2 * lq), F32), jax.ShapeDtypeStruct((1, 128), F32)],
        scratch_shapes=[pltpu.VMEM((tq, KV_WIDTH), F32), pltpu.VMEM((tq, KV_WIDTH), F32)],
        compiler_params=_params())


def rel_bias_table(rel_bias, bucket):
    lq = ATTN_BLOCK
    group = N_Q_HEADS // N_KV_HEADS

    def body(rb_ref, bk_ref, o_ref):
        bk = bk_ref[...]
        prev_keys = lax.broadcasted_iota(jnp.int32, bk.shape, 1) < lq
        accs = [jnp.full(bk.shape, -1e30, F32) for _ in range(N_Q_HEADS)]
        for b in range(N_BUCKETS):
            hit = bk == b
            accs = [jnp.where(hit, rb_ref[b, h], acc) for h, acc in enumerate(accs)]
        for h in range(N_Q_HEADS):
            rows = slice((h % group) * lq, (h % group + 1) * lq)
            o_ref[0, h // group, rows, :] = accs[h]
            o_ref[1, h // group, rows, :] = jnp.where(prev_keys, -1e30, accs[h])

    return pl.pallas_call(
        body, name="rel_bias_table",
        in_specs=[pl.BlockSpec(memory_space=pltpu.SMEM), pl.BlockSpec(memory_space=pltpu.VMEM)],
        out_shape=jax.ShapeDtypeStruct((2, N_KV_HEADS, group * lq, 2 * lq), F32),
        compiler_params=_params(n_axes=0))(rel_bias, bucket)


def rel_bias_grad(dbias, bucket):
    lq = ATTN_BLOCK
    group = N_Q_HEADS // N_KV_HEADS

    def body(db_ref, bk_ref, o_ref):
        rows = lax.broadcasted_iota(jnp.int32, (N_BUCKETS, 128), 0)
        lanes = lax.broadcasted_iota(jnp.int32, (N_BUCKETS, 128), 1)
        bk = bk_ref[...]
        per_head = []
        for h in range(N_Q_HEADS):
            sl = slice((h % group) * lq, (h % group + 1) * lq)
            per_head.append(db_ref[0, h // group, sl, :] + db_ref[1, h // group, sl, :])

        def per_bucket(b, acc):
            hit = (bk == b).astype(F32)
            for h in range(N_Q_HEADS):
                val = jnp.sum(per_head[h] * hit, keepdims=True)
                acc = acc + jnp.where((rows == b) & (lanes == h), val, 0.0)
            return acc

        o_ref[...] = lax.fori_loop(0, N_BUCKETS, per_bucket, jnp.zeros((N_BUCKETS, 128), F32))

    return pl.pallas_call(body, name="rel_bias_grad", out_shape=jax.ShapeDtypeStruct((N_BUCKETS, 128), F32),
                          compiler_params=_params(n_axes=0))(dbias, bucket)


def _tri_sum(a, upper):
    n = a.shape[0]
    ri = lax.broadcasted_iota(jnp.int32, (n, n), 0)
    ci = lax.broadcasted_iota(jnp.int32, (n, n), 1)
    tri = ((ri <= ci) if upper else (ri >= ci)).astype(BF16)
    hi = a.astype(BF16)
    rest = a - hi.astype(F32)
    mid = rest.astype(BF16)
    lo = (rest - mid.astype(F32)).astype(BF16)
    dot = lambda part: jnp.dot(tri, part, preferred_element_type=F32)
    return dot(hi) + dot(mid) + dot(lo)


@jax.custom_vjp
def _cumsum_rows(a):
    return _tri_sum(a, False)


_cumsum_rows.defvjp(lambda a: (_tri_sum(a, False), None), lambda _, g: (_tri_sum(g, True),))


def _ssm_core(u, z, dt_raw, hprev, dt_bias, a_log, dskip, norm_w):
    lc = u.shape[0]
    xbc = _silu(u)
    xs, bm, cm = xbc[:, :SSM_WIDTH], xbc[:, SSM_WIDTH:SSM_WIDTH + SSM_BC], xbc[:, SSM_WIDTH + SSM_BC:]
    dt = jax.nn.softplus(dt_raw + dt_bias)
    adt = dt * (-jnp.exp(a_log))
    ri = lax.broadcasted_iota(jnp.int32, (lc, lc), 0)
    ci = lax.broadcasted_iota(jnp.int32, (lc, lc), 1)
    causal = ri >= ci
    acum = _cumsum_rows(adt)
    acum_t = acum.T
    last = acum[lc - 1:lc, :]
    per_group = SSM_HEADS // SSM_GROUPS
    lane = lax.broadcasted_iota(jnp.int32, (1, 128), 1)
    rowid = lax.broadcasted_iota(jnp.int32, (128, 1), 0)
    lo_lanes = lane < SSM_HEAD_DIM
    ys, hs = [], []
    for g in range(SSM_GROUPS):
        bg = bm[:, g * SSM_STATE:(g + 1) * SSM_STATE]
        cg = cm[:, g * SSM_STATE:(g + 1) * SSM_STATE]
        cb = mm_nt(cg, bg)
        for pp in range(per_group // 2):
            ha = g * per_group + 2 * pp
            xp = xs[:, ha * SSM_HEAD_DIM:(ha + 2) * SSM_HEAD_DIM]
            hp = hprev[ha * SSM_HEAD_DIM:(ha + 2) * SSM_HEAD_DIM, :]
            xcp = xp * jnp.where(lo_lanes, dt[:, ha:ha + 1], dt[:, ha + 1:ha + 2])
            y_h, st_h = [], []
            for h in (ha, ha + 1):
                col, rowv, lasth = acum[:, h:h + 1], acum_t[h:h + 1, :], last[:, h:h + 1]
                decay = jnp.exp(jnp.where(causal, col - rowv, -1e30))
                y_h.append(mm(cb * decay, xcp) + mm_nt(cg * jnp.exp(col), hp))
                st_h.append(mm_tn(xcp, bg * jnp.exp(lasth - col)))
            y_pair = jnp.where(lo_lanes, y_h[0], y_h[1])
            st_pair = jnp.where(rowid < SSM_HEAD_DIM, st_h[0], st_h[1])
            la, lb = last[:, ha:ha + 1], last[:, ha + 1:ha + 2]
            hs.append(jnp.exp(jnp.where(rowid < SSM_HEAD_DIM, la, lb)) * hp + st_pair)
            dsk = jnp.where(lo_lanes, dskip[:, ha:ha + 1], dskip[:, ha + 1:ha + 2])
            ys.append(y_pair + dsk * xp)
    y = jnp.concatenate(ys, axis=1) * _silu(z)
    gw = SSM_WIDTH // SSM_GROUPS
    outs = []
    for g in range(SSM_GROUPS):
        yg = y[:, g * gw:(g + 1) * gw]
        outs.append(yg * lax.rsqrt(jnp.mean(yg * yg, axis=-1, keepdims=True) + NORM_EPS))
    return jnp.concatenate(outs, axis=1) * norm_w, jnp.concatenate(hs, axis=0)


def _ssm_param_specs():
    return [pl.BlockSpec((SSM_CONV, XBC_WIDTH), _const), _vec(XBC_WIDTH), _vec(128), _vec(128), _vec(128),
            _vec(SSM_WIDTH)]


SSM_FWD_SUB = 2
SSM_BWD_SUB = 1


def ssm_fwd(xbc_raw, z, dt_raw, conv_w, conv_b, dt_bias, a_log, dskip, norm_w, exchange=None):
    s = xbc_raw.shape[0]
    lc = SSM_CHUNK
    lt = lc * SSM_FWD_SUB
    hrows = SSM_HEADS * SSM_HEAD_DIM

    def body(x_ref, halo_ref, z_ref, dt_ref, cw_ref, cb_ref, dtb_ref, al_ref, dk_ref, nw_ref,
             o_ref, hp_ref, state):
        i = pl.program_id(0)

        @pl.when(i == 0)
        def _():
            state[...] = jnp.zeros_like(state)

        halo = halo_ref[...] * (i > 0).astype(F32)
        xin = jnp.concatenate([halo, x_ref[...]], axis=0)
        u = (_conv_rows(xin, cw_ref[...], SSM_CONV) + cb_ref[...])[HALO:]
        h = state[...]
        for k in range(SSM_FWD_SUB):
            rows = slice(k * lc, (k + 1) * lc)
            hp_ref[k * hrows:(k + 1) * hrows, :] = h
            out, h = _ssm_core(u[rows], z_ref[rows, :], dt_ref[rows, :], h, dtb_ref[...], al_ref[...], dk_ref[...],
                               nw_ref[...])
            o_ref[rows, :] = _b(out)
        state[...] = h

    tile = lambda n: pl.BlockSpec((lt, n), _row)
    halo_spec = pl.BlockSpec((HALO, XBC_WIDTH), lambda i: (jnp.maximum(i * (lt // HALO) - 1, 0), 0))
    return _grid_call(
        body, s // lt, (xbc_raw, xbc_raw, z, dt_raw, conv_w, conv_b, dt_bias, a_log, dskip, norm_w), exchange,
        name="ssm_fwd",
        in_specs=[tile(XBC_WIDTH), halo_spec, tile(SSM_WIDTH), tile(DT_PAD)] + _ssm_param_specs(),
        out_specs=[tile(SSM_WIDTH), pl.BlockSpec((SSM_FWD_SUB * hrows, SSM_STATE), _row)],
        out_shape=[jax.ShapeDtypeStruct((s, SSM_WIDTH), BF16),
                   jax.ShapeDtypeStruct((s // lc * hrows, SSM_STATE), F32)],
        scratch_shapes=[pltpu.VMEM((hrows, SSM_STATE), F32)],
        compiler_params=_params())


def ssm_bwd(xbc_raw, z, dt_raw, hprev_all, d_out, conv_w, conv_b, dt_bias, a_log, dskip, norm_w, exchange=None):
    s = xbc_raw.shape[0]
    lc = SSM_CHUNK
    sub = SSM_BWD_SUB
    lt = lc * sub
    nt = s // lt
    hrows = SSM_HEADS * SSM_HEAD_DIM

    def body(x_ref, halo_ref, z_ref, dt_ref, hp_ref, do_ref, cw_ref, cb_ref, dtb_ref, al_ref, dk_ref, nw_ref,
             dx_ref, dz_ref, ddt_ref, dcw_ref, dcb_ref, ddtb_ref, dal_ref, ddk_ref, dnw_ref, dstate, du_next):
        i = pl.program_id(0)
        tile_no = nt - 1 - i

        @pl.when(i == 0)
        def _():
            dstate[...] = jnp.zeros_like(dstate)
            du_next[...] = jnp.zeros_like(du_next)
            for r in (dcw_ref, dcb_ref, ddtb_ref, dal_ref, ddk_ref, dnw_ref):
                r[...] = jnp.zeros_like(r)

        halo = halo_ref[...] * (tile_no > 0).astype(F32)
        xin = jnp.concatenate([halo, x_ref[...]], axis=0)
        cw = cw_ref[...]
        u = (_conv_rows(xin, cw, SSM_CONV) + cb_ref[...])[HALO:]
        dh = dstate[...]
        dus = [None] * sub
        for k in reversed(range(sub)):
            rows = slice(k * lc, (k + 1) * lc)
            _, vjp = jax.vjp(_ssm_core, u[rows], z_ref[rows, :], dt_ref[rows, :], hp_ref[k * hrows:(k + 1) * hrows, :],
                             dtb_ref[...], al_ref[...], dk_ref[...], nw_ref[...])
            dus[k], dz, ddt, dh, ddtb, dal, ddk, dnw = vjp((do_ref[rows, :], dh))
            dz_ref[rows, :] = _b(dz)
            ddt_ref[rows, :] = _b(ddt)
            ddtb_ref[...] += ddtb
            dal_ref[...] += dal
            ddk_ref[...] += ddk
            dnw_ref[...] += dnw
        dstate[...] = dh
        du = jnp.concatenate(dus, axis=0)
        du_ext = jnp.concatenate([du, du_next[...]], axis=0)
        dx_ref[...] = _b(_conv_rows_t(du_ext, cw, SSM_CONV)[:lt])
        du_next[...] = du[:HALO]
        sums = [jnp.sum(du * pltpu.roll(xin, j, axis=0)[HALO:] if j else du * xin[HALO:], axis=0, keepdims=True)
                for j in range(SSM_CONV)]
        dcw_ref[...] += jnp.concatenate(sums[::-1] + [jnp.zeros((8 - SSM_CONV, XBC_WIDTH), F32)], axis=0)
        dcb_ref[...] += jnp.sum(du, axis=0, keepdims=True)

    rev = lambda i: (nt - 1 - i, 0)
    tile = lambda n: pl.BlockSpec((lt, n), rev)
    halo_spec = pl.BlockSpec((HALO, XBC_WIDTH), lambda i: (jnp.maximum((nt - 1 - i) * (lt // HALO) - 1, 0), 0))
    acc = lambda r, n: pl.BlockSpec((r, n), _const)
    return _grid_call(
        body, nt, (xbc_raw, xbc_raw, z, dt_raw, hprev_all, d_out, conv_w, conv_b, dt_bias, a_log, dskip, norm_w),
        exchange, name="ssm_bwd",
        in_specs=[tile(XBC_WIDTH), halo_spec, tile(SSM_WIDTH), tile(DT_PAD),
                  pl.BlockSpec((sub * hrows, SSM_STATE), rev), tile(SSM_WIDTH)] + _ssm_param_specs(),
        out_specs=[tile(XBC_WIDTH), tile(SSM_WIDTH), tile(DT_PAD), acc(8, XBC_WIDTH), acc(1, XBC_WIDTH),
                   acc(1, 128), acc(1, 128), acc(1, 128), acc(1, SSM_WIDTH)],
        out_shape=[jax.ShapeDtypeStruct((s, XBC_WIDTH), BF16), jax.ShapeDtypeStruct((s, SSM_WIDTH), BF16),
                   jax.ShapeDtypeStruct((s, DT_PAD), BF16), jax.ShapeDtypeStruct((8, XBC_WIDTH), F32),
                   jax.ShapeDtypeStruct((1, XBC_WIDTH), F32), jax.ShapeDtypeStruct((1, 128), F32),
                   jax.ShapeDtypeStruct((1, 128), F32), jax.ShapeDtypeStruct((1, 128), F32),
                   jax.ShapeDtypeStruct((1, SSM_WIDTH), F32)],
        scratch_shapes=[pltpu.VMEM((hrows, SSM_STATE), F32), pltpu.VMEM((HALO, XBC_WIDTH), F32)],
        compiler_params=_params())


def mix_out(attn, ssm, x, w_out16, gate1, post_mix_w, pre_ffn_w, scale2, shift2, tm=4 * TOKEN_TILE):
    s = x.shape[0]
    tm = min(tm, s)

    def body(a_ref, s_ref, x_ref, w_ref, g_ref, pw_ref, fw_ref, sc_ref, sh_ref, mixed_ref, x1_ref, h2_ref):
        mixed = (jnp.dot(a_ref[...], w_ref[:ATTN_WIDTH, :], preferred_element_type=F32)
                 + jnp.dot(s_ref[...], w_ref[ATTN_WIDTH:, :], preferred_element_type=F32))
        mixed_ref[...] = _b(mixed)
        x1 = x_ref[...] + g_ref[...] * _rms(mixed, pw_ref[...])
        x1_ref[...] = x1
        h2_ref[...] = _b(_norm_mod(x1, fw_ref[...], sc_ref[...], sh_ref[...]))

    tile = lambda n: pl.BlockSpec((tm, n), _row)
    return pl.pallas_call(
        body, name="mix_out", grid=(s // tm,),
        in_specs=[tile(ATTN_WIDTH), tile(SSM_WIDTH), tile(D_MODEL), _resident((D_MODEL, D_MODEL))]
        + [_vec(D_MODEL)] * 5,
        out_specs=[tile(D_MODEL)] * 3,
        out_shape=[jax.ShapeDtypeStruct((s, D_MODEL), BF16), jax.ShapeDtypeStruct((s, D_MODEL), F32),
                   jax.ShapeDtypeStruct((s, D_MODEL), BF16)],
        compiler_params=_params())(attn, ssm, x, w_out16, gate1, post_mix_w, pre_ffn_w, scale2, shift2)


GELU_K0, GELU_K1 = math.sqrt(2.0 / math.pi), 0.044715


def _gate(ug, uv):
    return jax.nn.gelu(ug, approximate=True) * uv


def _gate_bwd(ug, uv, df):
    sq = ug * ug
    t = jnp.tanh(ug * (GELU_K0 + (GELU_K0 * GELU_K1) * sq))
    half = 0.5 + 0.5 * t
    slope = half + ug * (1.0 - t * t) * (0.5 * GELU_K0 + (1.5 * GELU_K0 * GELU_K1) * sq)
    return df * uv * slope, df * (ug * half)


def up_gate(h2, w_up16, conv_w, conv_b, tm=TOKEN_TILE):
    s = h2.shape[0]

    def body(h_ref, halo_ref, w_ref, cw_ref, cb_ref, u_ref, uraw_ref, f_ref):
        halo = halo_ref[...]
        halo = jnp.where(pl.program_id(0) > 0, halo, jnp.zeros_like(halo))
        hin = jnp.concatenate([halo, h_ref[...]], axis=0)
        for lo in range(0, D_FF, FF_CHUNK):
            halves = []
            for base in (lo, D_FF + lo):
                cols = slice(base, base + FF_CHUNK)
                uraw = jnp.dot(hin, w_ref[:, cols], preferred_element_type=F32)
                uraw_ref[:, cols] = _b(uraw[NEXT:])
                u = (_conv_rows(uraw, cw_ref[:, cols], FFN_CONV) + cb_ref[:, cols])[NEXT:]
                u_ref[:, cols] = u
                halves.append(u)
            f_ref[:, lo:lo + FF_CHUNK] = _b(_gate(*halves))

    tile = lambda n: pl.BlockSpec((tm, n), _row)
    halo_spec = pl.BlockSpec((NEXT, D_MODEL), lambda i: (jnp.maximum(i * (tm // NEXT) - 1, 0), 0))
    return pl.pallas_call(
        body, name="up_gate", grid=(s // tm,),
        in_specs=[tile(D_MODEL), halo_spec, _resident((D_MODEL, 2 * D_FF)),
                  pl.BlockSpec((FFN_CONV, 2 * D_FF), _const), _vec(2 * D_FF)],
        out_specs=[tile(2 * D_FF), tile(2 * D_FF), tile(D_FF)],
        out_shape=[jax.ShapeDtypeStruct((s, 2 * D_FF), F32), jax.ShapeDtypeStruct((s, 2 * D_FF), BF16),
                   jax.ShapeDtypeStruct((s, D_FF), BF16)],
        compiler_params=_params())(h2, h2, w_up16, conv_w, conv_b)


DOWN_LOSS_TILE = 512


def down_loss(f16, w_down16, x1, target, gate2, post_ffn_w, tm=DOWN_LOSS_TILE):
    s = x1.shape[0]
    tm = min(tm, s)

    def body(f_ref, wd_ref, x1_ref, t_ref, g_ref, pw_ref, dffn_ref, dy_ref, loss_ref, dg_ref, dpw_ref, gw_ref):
        i = pl.program_id(0)

        @pl.when(i == 0)
        def _():
            loss_ref[...] = jnp.zeros_like(loss_ref)
            dg_ref[...] = jnp.zeros_like(dg_ref)
            dpw_ref[...] = jnp.zeros_like(dpw_ref)
            gw_ref[...] = jnp.zeros_like(gw_ref)

        ffn = jnp.dot(f_ref[...], wd_ref[...], preferred_element_type=F32)
        x1 = x1_ref[...]
        x2 = x1 + g_ref[...] * _rms(ffn, pw_ref[...])
        err = x2 - t_ref[...]
        dy = err * (1.0 / D_MODEL)
        dy_ref[...] = dy
        loss_ref[...] += 0.5 * jnp.sum(jnp.mean(err * err, axis=-1, keepdims=True))
        dffn, dg, dpw = _gated_rms_bwd(ffn, g_ref[...], pw_ref[...], dy)
        dffn16 = _b(dffn)
        dffn_ref[...] = dffn16
        dg_ref[...] += dg
        dpw_ref[...] += dpw
        gw_ref[...] += _tn(f_ref[...], dffn16)

    tile = lambda n: pl.BlockSpec((tm, n), _row)
    return pl.pallas_call(
        body, name="down_loss", grid=(s // tm,),
        in_specs=[tile(D_FF), _resident((D_FF, D_MODEL)), tile(D_MODEL), tile(D_MODEL), _vec(D_MODEL), _vec(D_MODEL)],
        out_specs=[tile(D_MODEL), tile(D_MODEL), _vec(128), _vec(D_MODEL), _vec(D_MODEL),
                   pl.BlockSpec((D_FF, D_MODEL), _const)],
        out_shape=[jax.ShapeDtypeStruct((s, D_MODEL), BF16), jax.ShapeDtypeStruct((s, D_MODEL), F32),
                   jax.ShapeDtypeStruct((1, 128), F32), jax.ShapeDtypeStruct((1, D_MODEL), F32),
                   jax.ShapeDtypeStruct((1, D_MODEL), F32), jax.ShapeDtypeStruct((D_FF, D_MODEL), F32)],
        compiler_params=_params())(f16, w_down16, x1, target, gate2, post_ffn_w)


BWD_CHUNK = 256


def ffn_bwd(u, u_raw16, d_ffn, conv_w, w_down_t16, w_up_t16, tm=TOKEN_TILE):
    s = u.shape[0]
    nt = s // tm

    def body(u_ref, unext_ref, uraw_ref, d_ref, dnext_ref, cw_ref, wdt_ref, wut_ref,
             du_ref, dh_ref, dcw_ref, dcb_ref):
        i = pl.program_id(0)

        @pl.when(i == 0)
        def _():
            dcw_ref[...] = jnp.zeros_like(dcw_ref)
            dcb_ref[...] = jnp.zeros_like(dcb_ref)

        dnext = dnext_ref[...]
        dnext = jnp.where(i < nt - 1, dnext, jnp.zeros_like(dnext))
        dff = jnp.concatenate([d_ref[...], dnext], axis=0)
        rows_ext = tm + NEXT
        for lo in range(0, D_FF, BWD_CHUNK):
            gcols, vcols = slice(lo, lo + BWD_CHUNK), slice(D_FF + lo, D_FF + lo + BWD_CHUNK)
            ug = jnp.concatenate([u_ref[:, gcols], unext_ref[:, gcols]], axis=0)
            uv = jnp.concatenate([u_ref[:, vcols], unext_ref[:, vcols]], axis=0)
            df = jnp.dot(dff, wdt_ref[:, gcols], preferred_element_type=F32)
            for cols, du in zip((gcols, vcols), _gate_bwd(ug, uv, df)):
                cw = cw_ref[:, cols]
                du1 = pltpu.roll(du, rows_ext - 1, axis=0)
                du2 = pltpu.roll(du, rows_ext - 2, axis=0)
                du_ref[:, cols] = _b((du * cw[2:3, :] + du1 * cw[1:2, :] + du2 * cw[0:1, :])[:tm])
                xr = uraw_ref[:, cols].astype(F32)
                rows = [jnp.sum(xr * d_[:tm], axis=0, keepdims=True) for d_ in (du2, du1, du)]
                dcw_ref[:, cols] += jnp.concatenate(rows + [jnp.zeros((8 - FFN_CONV, BWD_CHUNK), F32)], axis=0)
                dcb_ref[:, cols] += jnp.sum(du[:tm], axis=0, keepdims=True)
        dh_ref[...] = jnp.dot(du_ref[...], wut_ref[...], preferred_element_type=F32)

    tile = lambda n: pl.BlockSpec((tm, n), _row)
    nxt = lambda i: (jnp.minimum((i + 1) * (tm // NEXT), s // NEXT - 1), 0)
    return pl.pallas_call(
        body, name="ffn_bwd", grid=(nt,),
        in_specs=[tile(2 * D_FF), pl.BlockSpec((NEXT, 2 * D_FF), nxt), tile(2 * D_FF), tile(D_MODEL),
                  pl.BlockSpec((NEXT, D_MODEL), nxt), pl.BlockSpec((FFN_CONV, 2 * D_FF), _const),
                  _resident((D_MODEL, D_FF)), _resident((2 * D_FF, D_MODEL))],
        out_specs=[tile(2 * D_FF), tile(D_MODEL), pl.BlockSpec((8, 2 * D_FF), _const), _vec(2 * D_FF)],
        out_shape=[jax.ShapeDtypeStruct((s, 2 * D_FF), BF16), jax.ShapeDtypeStruct((s, D_MODEL), F32),
                   jax.ShapeDtypeStruct((8, 2 * D_FF), F32), jax.ShapeDtypeStruct((1, 2 * D_FF), F32)],
        compiler_params=_params())(u, u, u_raw16, d_ffn, d_ffn, conv_w, w_down_t16, w_up_t16)


def mix_bwd(dh2, x1, dy, mixed, attn, ssm, w_out_t16, pre_ffn_w, scale2, gate1, post_mix_w, tm=2 * TOKEN_TILE):
    s = x1.shape[0]

    def body(dh_ref, x1_ref, dy_ref, mx_ref, a_ref, s_ref, w_ref, fw_ref, sc_ref, g_ref, pw_ref,
             dx1_ref, da_ref, ds_ref, dfw_ref, dsc_ref, dsh_ref, dg_ref, dpw_ref, gw_ref):
        accs = (dfw_ref, dsc_ref, dsh_ref, dg_ref, dpw_ref)

        @pl.when(pl.program_id(0) == 0)
        def _():
            for r in accs + (gw_ref,):
                r[...] = jnp.zeros_like(r)

        dx1, dfw, dsc, dsh = _norm_mod_bwd(x1_ref[...], fw_ref[...], sc_ref[...], dh_ref[...])
        dx1 = dx1 + dy_ref[...]
        dx1_ref[...] = dx1
        dmixed, dg, dpw = _gated_rms_bwd(mx_ref[...].astype(F32), g_ref[...], pw_ref[...], dx1)
        dm16 = _b(dmixed)
        dmix_in = jnp.dot(dm16, w_ref[...], preferred_element_type=F32)
        da_ref[...] = _b(dmix_in[:, :ATTN_WIDTH])
        ds_ref[...] = dmix_in[:, ATTN_WIDTH:]
        gw_ref[:ATTN_WIDTH, :] += _tn(a_ref[...], dm16)
        gw_ref[ATTN_WIDTH:, :] += _tn(s_ref[...], dm16)
        for r, v in zip(accs, (dfw, dsc, dsh, dg, dpw)):
            r[...] += v

    tile = lambda n: pl.BlockSpec((tm, n), _row)
    return pl.pallas_call(
        body, name="mix_bwd", grid=(s // tm,),
        in_specs=[tile(D_MODEL)] * 4 + [tile(ATTN_WIDTH), tile(SSM_WIDTH), _resident((D_MODEL, D_MODEL))]
        + [_vec(D_MODEL)] * 4,
        out_specs=[tile(D_MODEL), tile(ATTN_WIDTH), tile(SSM_WIDTH)] + [_vec(D_MODEL)] * 5
        + [pl.BlockSpec((D_MODEL, D_MODEL), _const)],
        out_shape=[jax.ShapeDtypeStruct((s, D_MODEL), F32), jax.ShapeDtypeStruct((s, ATTN_WIDTH), BF16),
                   jax.ShapeDtypeStruct((s, SSM_WIDTH), F32)]
        + [jax.ShapeDtypeStruct((1, D_MODEL), F32)] * 5 + [jax.ShapeDtypeStruct((D_MODEL, D_MODEL), F32)],
        compiler_params=_params())(dh2, x1, dy, mixed, attn, ssm, w_out_t16, pre_ffn_w, scale2, gate1, post_mix_w)


INPROJ_BWD_TILE = 512


def inproj_bwd(dq, dk, dv, dxbc, dz, ddt, x, dx1, h1, w_in_t16, pre_mix_w, scale1, tm=INPROJ_BWD_TILE):
    s = x.shape[0]
    tm = min(tm, s)

    def body(dq_ref, dk_ref, dv_ref, dxbc_ref, dz_ref, ddt_ref, x_ref, dx1_ref, h_ref, w_ref, pw_ref, sc_ref,
             gx_ref, dpw_ref, dsc_ref, dsh_ref, gw_ref):
        accs = (dpw_ref, dsc_ref, dsh_ref)

        @pl.when(pl.program_id(0) == 0)
        def _():
            for r in accs + (gw_ref,):
                r[...] = jnp.zeros_like(r)

        h16 = h_ref[...]
        dh = None
        dqkv = jnp.concatenate([dq_ref[...], dk_ref[...], dv_ref[...]], axis=1)
        for d16, (lo, hi) in ((dqkv, (0, QKV_W)), (dxbc_ref[:, :SSM_WIDTH], COLS_XS), (dxbc_ref[:, SSM_WIDTH:], COLS_BC),
                              (dz_ref[...], COLS_Z), (ddt_ref[...], COLS_DT)):
            part = jnp.dot(d16, w_ref[lo:hi, :], preferred_element_type=F32)
            dh = part if dh is None else dh + part
            rows = min(hi, IN_PROJ_WIDTH) - lo
            gw_ref[lo:lo + rows, :] += _tn(d16, h16)[:rows]
        dx, dpw, dsc, dsh = _norm_mod_bwd(x_ref[...], pw_ref[...], sc_ref[...], dh)
        gx_ref[...] = dx1_ref[...] + dx
        for r, v in zip(accs, (dpw, dsc, dsh)):
            r[...] += v

    tile = lambda n: pl.BlockSpec((tm, n), _row)
    return pl.pallas_call(
        body, name="inproj_bwd", grid=(s // tm,),
        in_specs=[tile(ATTN_WIDTH), tile(KV_WIDTH), tile(KV_WIDTH), tile(XBC_WIDTH), tile(SSM_WIDTH), tile(DT_PAD),
                  tile(D_MODEL), tile(D_MODEL), tile(D_MODEL), _resident((PROJ_PAD, D_MODEL))] + [_vec(D_MODEL)] * 2,
        out_specs=[tile(D_MODEL)] + [_vec(D_MODEL)] * 3 + [pl.BlockSpec((IN_PROJ_WIDTH, D_MODEL), _const)],
        out_shape=[jax.ShapeDtypeStruct((s, D_MODEL), F32)] + [jax.ShapeDtypeStruct((1, D_MODEL), F32)] * 3
        + [jax.ShapeDtypeStruct((IN_PROJ_WIDTH, D_MODEL), F32)],
        compiler_params=_params())(dq, dk, dv, dxbc, dz, ddt, x, dx1, h1, w_in_t16, pre_mix_w, scale1)


def _adam(g, w, m, v):
    new_m = ADAM_B1 * m + (1.0 - ADAM_B1) * g
    new_v = ADAM_B2 * v + (1.0 - ADAM_B2) * jnp.square(g)
    m_hat = new_m / (1.0 - ADAM_B1 ** ADAM_STEP)
    v_hat = new_v / (1.0 - ADAM_B2 ** ADAM_STEP)
    return -ADAM_LR * (m_hat / (jnp.sqrt(v_hat) + ADAM_EPS) + ADAM_WD * w), new_m, new_v


ROW_PARAMS = (("b_ada", 6144, 6144), ("pre_mix_w", 1024, 1024), ("attn_sinks", 128, 8), ("ssm_conv_b", 1024, 1024),
              ("ssm_dt_bias", 128, 8), ("ssm_a_log", 128, 8), ("ssm_d", 128, 8), ("ssm_norm_w", 512, 512),
              ("post_mix_w", 1024, 1024), ("pre_ffn_w", 1024, 1024), ("ffn_conv_b", 5632, 5632),
              ("post_ffn_w", 1024, 1024))
LOSS_LANES = 128


def adamw_small(row_all, rb_all, rel_bias_wmv, row_wmv):
    n_rows = len(ROW_PARAMS)

    def body(*refs):
        row_ref, rb_ref = refs[:2]
        wmv = refs[2:5 + 3 * n_rows]
        outs = refs[5 + 3 * n_rows:]
        g_row, g_rb = row_ref[0], rb_ref[0]
        for k in range(1, N_DEV):
            g_row = g_row + row_ref[k]
            g_rb = g_rb + rb_ref[k]
        outs[0][...] = g_row[:, :LOSS_LANES]
        grads = [g_rb[:, :N_Q_HEADS]]
        off = LOSS_LANES
        for _, lanes, width in ROW_PARAMS:
            grads.append(g_row[:, off:off + width])
            off += lanes
        for i, g in enumerate(grads):
            w_ref, m_ref, v_ref = wmv[3 * i:3 * i + 3]
            g_out, d_out, m_out, v_out = outs[1 + 4 * i:5 + 4 * i]
            g_out[...] = g
            d_out[...], m_out[...], v_out[...] = _adam(g, w_ref[...], m_ref[...], v_ref[...])

    flat_wmv = list(rel_bias_wmv) + [a for wmv in row_wmv for a in wmv]
    shapes = [jax.ShapeDtypeStruct((1, LOSS_LANES), F32)] + [jax.ShapeDtypeStruct((N_BUCKETS, N_Q_HEADS), F32)] * 4
    for _, _, width in ROW_PARAMS:
        shapes += [jax.ShapeDtypeStruct((1, width), F32)] * 4
    return pl.pallas_call(body, name="adamw_small", out_shape=shapes,
                          compiler_params=_params(n_axes=0))(row_all, rb_all, *flat_wmv)


def adamw(parts, w, m, v, name):
    p, r, n = parts.shape
    tr = _row_tile(r)

    def body(p_ref, w_ref, m_ref, v_ref, g_ref, d_ref, nm_ref, nv_ref):
        g = p_ref[0].astype(F32)
        for k in range(1, p):
            g = g + p_ref[k].astype(F32)
        g_ref[...] = g
        d_ref[...], nm_ref[...], nv_ref[...] = _adam(g, w_ref[...], m_ref[...], v_ref[...])

    tile = pl.BlockSpec((tr, n), _row)
    return pl.pallas_call(
        body, name=name, grid=(r // tr,),
        in_specs=[pl.BlockSpec((p, tr, n), lambda i: (0, i, 0)), tile, tile, tile],
        out_specs=[tile] * 4, out_shape=[jax.ShapeDtypeStruct((r, n), F32)] * 4,
        compiler_params=_params())(parts, w, m, v)


def _bucket_table():
    lq = ATTN_BLOCK
    qi = np.arange(lq)[:, None] + lq
    kj = np.arange(2 * lq)[None, :]
    dist = qi - kj
    d = np.maximum(dist, 0)
    max_exact = N_BUCKETS // 2
    nf = np.maximum(d, 1).astype(np.float32)
    large = max_exact + (np.log(nf / max_exact) / math.log(REL_MAX_DIST / max_exact)
                         * (N_BUCKETS - max_exact)).astype(np.int32)
    large = np.minimum(large, N_BUCKETS - 1)
    bucket = np.where(d < max_exact, d, large).astype(np.int32)
    in_band = (dist >= 0) & (dist < REL_MAX_DIST)
    return np.where(in_band, bucket, -1).astype(np.int32)


def _cols_from_blocks(g):
    return jnp.transpose(g, (1, 0, 2)).reshape(g.shape[1], N_DEV * g.shape[2])


def _cols_to_blocks(a):
    r, n = a.shape
    return jnp.transpose(a.reshape(r, N_DEV, n // N_DEV), (1, 0, 2))


def _pad_in_rows(wt):
    return jnp.concatenate([wt, jnp.zeros((PROJ_PAD - wt.shape[0], wt.shape[1]), wt.dtype)], axis=0)


def _lane_pad(v, n=128):
    return jnp.pad(v, ((0, 0), (0, n - v.shape[1])))


def kernel(x, c, rel_bias, w_ada, b_ada, pre_mix_w, w_in, attn_sinks, ssm_conv_w, ssm_conv_b, ssm_dt_bias, ssm_a_log, ssm_d, ssm_norm_w, w_out, post_mix_w, pre_ffn_w, w_up, ffn_conv_w, ffn_conv_b, w_down, post_ffn_w, loss_target, m_rel_bias, m_w_ada, m_b_ada, m_pre_mix_w, m_w_in, m_attn_sinks, m_ssm_conv_w, m_ssm_conv_b, m_ssm_dt_bias, m_ssm_a_log, m_ssm_d, m_ssm_norm_w, m_w_out, m_post_mix_w, m_pre_ffn_w, m_w_up, m_ffn_conv_w, m_ffn_conv_b, m_w_down, m_post_ffn_w, v_rel_bias, v_w_ada, v_b_ada, v_pre_mix_w, v_w_in, v_attn_sinks, v_ssm_conv_w, v_ssm_conv_b, v_ssm_dt_bias, v_ssm_a_log, v_ssm_d, v_ssm_norm_w, v_w_out, v_post_mix_w, v_pre_ffn_w, v_w_up, v_ffn_conv_w, v_ffn_conv_b, v_w_down, v_post_ffn_w):
    weights = dict(rel_bias=rel_bias, w_ada=w_ada, b_ada=b_ada, pre_mix_w=pre_mix_w, w_in=w_in, attn_sinks=attn_sinks, ssm_conv_w=ssm_conv_w, ssm_conv_b=ssm_conv_b, ssm_dt_bias=ssm_dt_bias, ssm_a_log=ssm_a_log, ssm_d=ssm_d, ssm_norm_w=ssm_norm_w, w_out=w_out, post_mix_w=post_mix_w, pre_ffn_w=pre_ffn_w, w_up=w_up, ffn_conv_w=ffn_conv_w, ffn_conv_b=ffn_conv_b, w_down=w_down, post_ffn_w=post_ffn_w)
    mom_m = dict(rel_bias=m_rel_bias, w_ada=m_w_ada, b_ada=m_b_ada, pre_mix_w=m_pre_mix_w, w_in=m_w_in, attn_sinks=m_attn_sinks, ssm_conv_w=m_ssm_conv_w, ssm_conv_b=m_ssm_conv_b, ssm_dt_bias=m_ssm_dt_bias, ssm_a_log=m_ssm_a_log, ssm_d=m_ssm_d, ssm_norm_w=m_ssm_norm_w, w_out=m_w_out, post_mix_w=m_post_mix_w, pre_ffn_w=m_pre_ffn_w, w_up=m_w_up, ffn_conv_w=m_ffn_conv_w, ffn_conv_b=m_ffn_conv_b, w_down=m_w_down, post_ffn_w=m_post_ffn_w)
    mom_v = dict(rel_bias=v_rel_bias, w_ada=v_w_ada, b_ada=v_b_ada, pre_mix_w=v_pre_mix_w, w_in=v_w_in, attn_sinks=v_attn_sinks, ssm_conv_w=v_ssm_conv_w, ssm_conv_b=v_ssm_conv_b, ssm_dt_bias=v_ssm_dt_bias, ssm_a_log=v_ssm_a_log, ssm_d=v_ssm_d, ssm_norm_w=v_ssm_norm_w, w_out=v_w_out, post_mix_w=v_post_mix_w, pre_ffn_w=v_pre_ffn_w, w_up=v_w_up, ffn_conv_w=v_ffn_conv_w, ffn_conv_b=v_ffn_conv_b, w_down=v_w_down, post_ffn_w=v_post_ffn_w)
    order = ['rel_bias', 'w_ada', 'b_ada', 'pre_mix_w', 'w_in', 'attn_sinks', 'ssm_conv_w', 'ssm_conv_b', 'ssm_dt_bias', 'ssm_a_log', 'ssm_d', 'ssm_norm_w', 'w_out', 'post_mix_w', 'pre_ffn_w', 'w_up', 'ffn_conv_w', 'ffn_conv_b', 'w_down', 'post_ffn_w']

    me = 4 * lax.axis_index("x") + 2 * lax.axis_index("y") + lax.axis_index("c")
    xs_ = x[0]
    target = loss_target[0]

    (w_in_g, scw_g, fcw_g, c_g) = all_gather([_b(w_in[0]).T, ssm_conv_w[0], ffn_conv_w[0], c], "gather_weights")
    w_in_t16 = _pad_in_rows(w_in_g.reshape(IN_PROJ_WIDTH, D_MODEL))
    w_in16 = w_in_t16.T
    ssm_cw = _cols_from_blocks(scw_g)
    ffn_cw = _cols_from_blocks(fcw_g)
    c_all = c_g.reshape(N_DEV, D_MODEL)

    n_cols = w_ada.shape[2]
    b_cols = lax.dynamic_slice(b_ada, (0, me * n_cols), (1, n_cols))
    mod_part = ada_fwd(c_all, w_ada[0], b_cols)
    (mod_rows,) = all_to_all([mod_part.reshape(N_DEV, 1, n_cols)], "scatter_mod")
    mod = mod_rows.reshape(N_MOD, 1, D_MODEL)
    shift1, scale1, gate1, shift2, scale2, gate2 = (mod[i] for i in range(N_MOD))

    bucket_band = jnp.asarray(_bucket_table())
    bias = rel_bias_table(rel_bias, bucket_band)
    sinks_row = _lane_pad(attn_sinks)
    dt_bias, a_log, dskip = _lane_pad(ssm_dt_bias), _lane_pad(ssm_a_log), _lane_pad(ssm_d)

    h1, qkv, xbc_raw, z, dt_raw, w_out_g = pre_mix_inproj(
        xs_, pre_mix_w, scale1, shift1, w_in16, [(_b(w_out[0]), False)])
    attn, w_down_g = attn_fwd(qkv, bias, sinks_row, [(_b(w_down[0]), False)])
    ssm, hprev_all, w_up_g = ssm_fwd(xbc_raw, z, dt_raw, ssm_cw, ssm_conv_b, dt_bias, a_log, dskip, ssm_norm_w,
                                     [(_b(w_up[0]).T, False)])
    w_out16 = w_out_g.reshape(D_MODEL, D_MODEL)
    w_out_t16 = w_out16.T
    w_up_t16 = w_up_g.reshape(2 * D_FF, D_MODEL)
    w_up16 = w_up_t16.T
    w_down16 = w_down_g.reshape(D_FF, D_MODEL)
    w_down_t16 = w_down16.T
    mixed, x1, h2 = mix_out(attn, ssm, xs_, w_out16, gate1, post_mix_w, pre_ffn_w, scale2, shift2)
    u, u_raw16, f16 = up_gate(h2, w_up16, ffn_cw, ffn_conv_b)
    d_ffn, dy, loss_part, d_gate2, d_post_ffn_w, g_w_down = down_loss(f16, w_down16, x1, target, gate2, post_ffn_w)

    du_raw, dh2, d_ffn_cw, d_ffn_cb = ffn_bwd(u, u_raw16, d_ffn, ffn_cw, w_down_t16, w_up_t16)
    g_w_up_t = matmul_tn(du_raw, h2, "grad_w_up", FF_CHUNK, D_MODEL, tk=2048)
    (dx1, d_attn, d_ssm, d_pre_ffn_w, d_scale2, d_shift2, d_gate1, d_post_mix_w, g_w_out) = mix_bwd(
        dh2, x1, dy, mixed, attn, ssm, w_out_t16, pre_ffn_w, scale2, gate1, post_mix_w)
    dq, dk, dv, dbias, dsinks, p_w_down = attn_bwd(
        qkv, bias, sinks_row, d_attn, [(g_w_down.reshape(N_DEV, D_FF // N_DEV, D_MODEL), True)])
    d_rel_bias = rel_bias_grad(dbias, bucket_band)
    (dxbc, dz, ddt, d_ssm_cw, d_ssm_cb, d_dt_bias, d_a_log, d_dskip, d_norm_w, p_w_up, p_w_out) = ssm_bwd(
        xbc_raw, z, dt_raw, hprev_all, d_ssm, ssm_cw, ssm_conv_b, dt_bias, a_log, dskip, ssm_norm_w,
        [(g_w_up_t.reshape(N_DEV, 2 * D_FF // N_DEV, D_MODEL), True),
         (g_w_out.reshape(N_DEV, D_MODEL // N_DEV, D_MODEL), True)])
    grad_x, d_pre_mix_w, d_scale1, d_shift1, g_w_in_perm = inproj_bwd(
        dq, dk, dv, dxbc, dz, ddt, xs_, dx1, h1, w_in_t16, pre_mix_w, scale1)
    g_w_in_t = g_w_in_perm

    d_mod = jnp.concatenate([d_shift1, d_scale1, d_gate1, d_shift2, d_scale2, d_gate2], axis=1)
    late = ("w_in",)
    full = [g_w_in_t.reshape(N_DEV, IN_PROJ_WIDTH // N_DEV, D_MODEL)]
    core = lax.axis_index("c").astype(jnp.int32).reshape(1)
    got = pair_exchange(full, "pair_grads")
    chip_sums = [pair_sum(f_, g_, core, "pair_sum_" + k) for k, f_, g_ in zip(late, full, got)]
    chip_parts = all_to_all(chip_sums, "scatter_grads", CHIP_FLIPS, _chip_index)

    row_g = dict(b_ada=d_mod, pre_mix_w=d_pre_mix_w, attn_sinks=dsinks, ssm_conv_b=d_ssm_cb, ssm_dt_bias=d_dt_bias,
                 ssm_a_log=d_a_log, ssm_d=d_dskip, ssm_norm_w=d_norm_w, post_mix_w=d_post_mix_w,
                 pre_ffn_w=d_pre_ffn_w, ffn_conv_b=d_ffn_cb, post_ffn_w=d_post_ffn_w)
    row = jnp.concatenate([loss_part] + [row_g[k] for k, _, _ in ROW_PARAMS], axis=1)
    row_all, rb_all, scw_all, fcw_all = all_gather(
        [row, d_rel_bias, d_ssm_cw[:SSM_CONV], d_ffn_cw[:FFN_CONV]], "gather_small")
    n_scw, n_fcw = ssm_conv_w.shape[2], ffn_conv_w.shape[2]
    p_scw = lax.dynamic_slice(scw_all, (0, 0, me * n_scw), (N_DEV, SSM_CONV, n_scw))
    p_fcw = lax.dynamic_slice(fcw_all, (0, 0, me * n_fcw), (N_DEV, FFN_CONV, n_fcw))
    d_mod_cols = lax.dynamic_slice(row_all.reshape(N_DEV, row.shape[1]), (0, LOSS_LANES + me * n_cols),
                                   (N_DEV, n_cols))
    g_w_ada = ada_bwd(c_all, d_mod_cols)

    wmv = lambda k: (weights[k], mom_m[k], mom_v[k])
    small = adamw_small(row_all, rb_all, wmv("rel_bias"), [wmv(k) for k, _, _ in ROW_PARAMS])
    loss = small[0][0, 0]
    res = {k: tuple(small[1 + 4 * i:5 + 4 * i]) for i, k in enumerate(["rel_bias"] + [k for k, _, _ in ROW_PARAMS])}
    big = list(zip(late, chip_parts)) + [("ssm_conv_w", p_scw), ("ffn_conv_w", p_fcw), ("w_down", p_w_down),
                                        ("w_up", p_w_up), ("w_out", p_w_out), ("w_ada", g_w_ada[None])]
    for k, parts in big:
        if k in ("w_in", "w_up"):
            outs_t = adamw(parts, weights[k][0].T, mom_m[k][0].T, mom_v[k][0].T, "adamw_" + k)
            res[k] = tuple(o.T[None] for o in outs_t)
        else:
            res[k] = tuple(o[None] for o in adamw(parts, weights[k][0], mom_m[k][0], mom_v[k][0], "adamw_" + k))

    outs = [loss, grad_x[None]]
    for field in range(4):
        outs += [res[k][field] for k in order]
    return tuple(outs)
```
